```python
import jax
import jax.numpy as jnp
from jax import lax
import numpy as np

D_MODEL = 1024
BATCH = 8
SEQ = 2048
DEPTH = 2
DEC_BATCH = 128
DEC_SEQ = 1
PAST_LEN = 16384
PAGE_SIZE = 128

D_PLE = 256
EPS = 1e-6
CONV_W = 4
D_A = D_MODEL
A_BLOCKS = 8
A_BLK = D_A // A_BLOCKS
LRU_C = 8.0
D_B = D_MODEL
HD_B = 64
H_B = D_B // HD_B
N_B = 128
G_B = 2
CONV_DIM_B = D_B + 2 * G_B * N_B
CHUNK_B = 128
D_C = 2 * D_MODEL
H_C = 16
DK_C = 128
DV_C = D_C // H_C
HK_C = H_C * DK_C
CHUNK_C = 64
SUB_C = 16
N_AB = (DEPTH + 1) // 2
N_C = DEPTH // 2
IN_AB = 2 * D_A + D_B + CONV_DIM_B + H_B
IN_C = 2 * HK_C + 2 * D_C

kernel_name = 'hybrid_rglru_ssd_hgrn2_decode_step'

F32 = jnp.float32


def rmsnorm(x, w):
    xf = x.astype(F32)
    y = xf * lax.rsqrt(jnp.mean(xf * xf, axis=-1, keepdims=True) + EPS)
    return (y * w.astype(F32)).astype(x.dtype)


def causal_conv(x, buf, w, b):
    L = x.shape[1]
    xp = jnp.concatenate([buf.astype(x.dtype), x], axis=1).astype(F32)
    w = w.astype(F32)
    y = b.astype(F32) + xp[:, 0:L] * w[0]
    for kk in range(1, CONV_W):
        y = y + xp[:, kk:kk + L] * w[kk]
    return y, xp[:, L:].astype(x.dtype)


def rg_lru(x, h0, w_r, b_r, w_i, b_i, lam):
    Bsz, L, _ = x.shape
    xb = x.reshape(Bsz, L, A_BLOCKS, A_BLK)
    r = jax.nn.sigmoid(jnp.einsum('blkc,kcd->blkd', xb, w_r.astype(F32)).reshape(Bsz, L, D_A) + b_r.astype(F32))
    gi = jax.nn.sigmoid(jnp.einsum('blkc,kcd->blkd', xb, w_i.astype(F32)).reshape(Bsz, L, D_A) + b_i.astype(F32))
    log_a = -LRU_C * r * jax.nn.softplus(-lam.astype(F32))
    a = jnp.exp(log_a)
    u = jnp.sqrt(-jnp.expm1(2.0 * log_a)) * (gi * x)

    def combine(lhs, rhs):
        a1, b1 = lhs
        a2, b2 = rhs
        return a1 * a2, a2 * b1 + b2

    a_cum, h_part = lax.associative_scan(combine, (a, u), axis=1)
    h = h_part + a_cum * h0.astype(F32)[:, None]
    return h, h[:, -1]


def ssd_scan(x, dt, a_neg, bm, cm, s0):
    Bsz, L = x.shape[:2]
    qc = CHUNK_B if L >= CHUNK_B else L
    lp = -(-L // qc) * qc
    nc = lp // qc
    hg = H_B // G_B

    def padt(t):
        return jnp.pad(t, [(0, 0), (0, lp - L)] + [(0, 0)] * (t.ndim - 2))

    xc = padt(x).reshape(Bsz, nc, qc, G_B, hg, HD_B)
    dtc = padt(dt).reshape(Bsz, nc, qc, G_B, hg)
    bc = padt(bm).reshape(Bsz, nc, qc, G_B, N_B)
    cc = padt(cm).reshape(Bsz, nc, qc, G_B, N_B)
    acs = jnp.cumsum(dtc * a_neg.reshape(G_B, hg), axis=2)
    tri = jnp.tril(jnp.ones((qc, qc), bool))
    seg = jnp.where(tri[:, :, None, None], acs[:, :, :, None] - acs[:, :, None], -jnp.inf)
    cb = jnp.einsum('bctgn,bcsgn->bctsg', cc, bc)
    wts = cb[..., None] * jnp.exp(seg) * dtc[:, :, None]
    y_diag = jnp.einsum('bctsgh,bcsghp->bctghp', wts, xc)
    a_last = acs[:, :, -1]
    st = jnp.einsum('bcsgn,bcsgh,bcsghp->bcghpn', bc, jnp.exp(a_last[:, :, None] - acs) * dtc, xc)

    def step(s, inp):
        st_c, al = inp
        return jnp.exp(al)[..., None, None] * s + st_c, s

    s_fin, s_start = lax.scan(step, s0.reshape(Bsz, G_B, hg, HD_B, N_B),
                              (st.transpose(1, 0, 2, 3, 4, 5), a_last.transpose(1, 0, 2, 3)))
    y_off = jnp.einsum('bctgn,cbghpn,bctgh->bctghp', cc, s_start, jnp.exp(acs))
    y = (y_diag + y_off).reshape(Bsz, lp, H_B, HD_B)[:, :L]
    return y, s_fin.reshape(Bsz, H_B, HD_B, N_B)


def hgrn2_scan(q, k, v, g, s0):
    Bsz, L = q.shape[:2]
    ss = SUB_C if L >= SUB_C else L
    qc = CHUNK_C if L >= CHUNK_C else -(-L // ss) * ss
    lp = -(-L // qc) * qc
    nc = lp // qc
    ns = qc // ss

    def chunks(t):
        t = jnp.pad(t, ((0, 0), (0, lp - L), (0, 0), (0, 0)))
        return t.reshape(Bsz, nc, ns, ss, H_C, t.shape[-1]).transpose(1, 0, 4, 2, 3, 5)

    tri = jnp.tril(jnp.ones((ss, ss), bool))
    off = jnp.tril(jnp.ones((ns, ns), bool), -1)
    eye = jnp.eye(ns, dtype=F32)

    def step(S, inp):
        qs, ks, vs, gs = inp
        bcum = jnp.cumsum(gs.reshape(Bsz, H_C, qc, DK_C), axis=2).reshape(Bsz, H_C, ns, ss, DK_C)
        b0 = bcum[:, :, :, :1] - gs[:, :, :, :1]
        o_inter = jnp.einsum('bhitk,bhkv->bhitv', qs * jnp.exp(bcum), S)
        q_t = qs * jnp.exp(bcum - b0)
        e_off = jnp.where(off[:, :, None, None], b0[:, :, :, None] - bcum[:, :, None], -jnp.inf)
        a_off = jnp.einsum('bhitk,bhijsk->bhijts', q_t, ks[:, :, None] * jnp.exp(e_off))
        e_diag = jnp.where(tri[:, :, None], bcum[:, :, :, :, None] - bcum[:, :, :, None], -jnp.inf)
        a_diag = jnp.einsum('bhitk,bhisk,bhitsk->bhits', qs, ks, jnp.exp(e_diag))
        a_all = a_off + eye[:, :, None, None] * a_diag[:, :, :, None]
        o_intra = jnp.einsum('bhijts,bhjsv->bhitv', a_all, vs)
        b_last = bcum[:, :, -1, -1]
        k_dec = ks * jnp.exp(b_last[:, :, None, None] - bcum)
        S_new = jnp.exp(b_last)[..., None] * S + jnp.einsum('bhisk,bhisv->bhkv', k_dec, vs)
        return S_new, o_inter + o_intra

    S_fin, o = lax.scan(step, s0.astype(F32), (chunks(q), chunks(k), chunks(v), chunks(g)))
    o = o.transpose(1, 0, 3, 4, 2, 5).reshape(Bsz, lp, H_C, DV_C)[:, :L]
    return o, S_fin


def per_layer_embed(h, p, w_proj, w_gate):
    gate = jax.nn.sigmoid((h @ w_gate).astype(F32))
    return h + (gate * (p @ w_proj).astype(F32)).astype(h.dtype)


def layer_ab(h, p, s_ah, s_ac, s_bs, s_bc, norm_w, w_in, a_conv_w, a_conv_b, a_w_r, a_b_r, a_w_i, a_b_i,
             a_lam, b_conv_w, b_conv_b, b_dt_bias, b_a_log, b_d, b_norm_w, w_out, ple_w, ple_g):
    Bsz, L, _ = h.shape
    u = rmsnorm(h, norm_w) @ w_in
    a_x, a_gate, b_z, b_xbc, b_dt = jnp.split(
        u, [D_A, 2 * D_A, 2 * D_A + D_B, 2 * D_A + D_B + CONV_DIM_B], axis=-1)
    a_xc, new_ac = causal_conv(a_x, s_ac, a_conv_w, a_conv_b)
    a_y, new_ah = rg_lru(a_xc, s_ah, a_w_r, a_b_r, a_w_i, a_b_i, a_lam)
    a_out = a_y * jax.nn.silu(a_gate.astype(F32))
    xbc, new_bc = causal_conv(b_xbc, s_bc, b_conv_w, b_conv_b)
    xbc = jax.nn.silu(xbc)
    bx, bb, bcm = jnp.split(xbc, [D_B, D_B + G_B * N_B], axis=-1)
    dt = jax.nn.softplus(b_dt.astype(F32) + b_dt_bias.astype(F32))
    a_neg = -jnp.exp(b_a_log.astype(F32))
    bx4 = bx.reshape(Bsz, L, H_B, HD_B)
    y, new_bs = ssd_scan(bx4, dt, a_neg, bb.reshape(Bsz, L, G_B, N_B), bcm.reshape(Bsz, L, G_B, N_B),
                         s_bs.astype(F32))
    y = (y + b_d.astype(F32)[:, None] * bx4).reshape(Bsz, L, D_B) * jax.nn.silu(b_z.astype(F32))
    yg = y.reshape(Bsz, L, G_B, D_B // G_B)
    yg = yg * lax.rsqrt(jnp.mean(yg * yg, axis=-1, keepdims=True) + EPS)
    b_out = yg.reshape(Bsz, L, D_B) * b_norm_w.astype(F32)
    mix = jnp.concatenate([a_out, b_out], axis=-1).astype(h.dtype) @ w_out
    h = h + mix
    h = per_layer_embed(h, p, ple_w, ple_g)
    return h, new_ah, new_ac, new_bs, new_bc


def layer_c(h, p, s_c, lb, norm_w, w_in, c_norm_w, w_out, ple_w, ple_g):
    Bsz, L, _ = h.shape
    u = rmsnorm(h, norm_w) @ w_in
    q, fx, v, gate = jnp.split(u, [HK_C, 2 * HK_C, 2 * HK_C + D_C], axis=-1)
    fx = fx.astype(F32)
    g = jnp.logaddexp(jnp.log(lb), jnp.log1p(-lb) + jax.nn.log_sigmoid(fx))
    k = (1.0 - lb) * jax.nn.sigmoid(-fx)
    shp = (Bsz, L, H_C, DK_C)
    o, new_c = hgrn2_scan(q.astype(F32).reshape(shp) * (DK_C ** -0.5), k.reshape(shp),
                          v.astype(F32).reshape(Bsz, L, H_C, DV_C), g.reshape(shp), s_c)
    o = o * lax.rsqrt(jnp.mean(o * o, axis=-1, keepdims=True) + EPS)
    o = o.reshape(Bsz, L, D_C) * c_norm_w.astype(F32) * jax.nn.silu(gate.astype(F32))
    h = h + o.astype(h.dtype) @ w_out
    h = per_layer_embed(h, p, ple_w, ple_g)
    return h, new_c


def setup_inputs(seed: int = 0) -> dict:
    key = jax.random.key(seed)
    ks = jax.random.split(key, 40)

    def nrm(k, shape, s):
        return s * jax.random.normal(k, shape, F32)

    a0 = jax.random.uniform(ks[15], (N_AB, D_A), F32, 0.9, 0.999)
    sa = a0 ** (1.0 / LRU_C)
    a_lam = jnp.log(sa) - jnp.log1p(-sa)
    dt0 = jnp.exp(jax.random.uniform(ks[18], (N_AB, H_B), F32, np.log(1e-3), np.log(0.1)))
    b_dt_bias = dt0 + jnp.log(-jnp.expm1(-dt0))
    b_a_log = jnp.log(jax.random.uniform(ks[19], (N_AB, H_B), F32, 1.0, 16.0))
    return {
        'x_prompt': nrm(ks[0], (BATCH, SEQ, D_MODEL), 1.0),
        'x_sample': nrm(ks[1], (DEC_BATCH, DEC_SEQ, D_MODEL), 1.0),
        'p_prompt': nrm(ks[2], (DEPTH, BATCH, SEQ, D_PLE), 1.0),
        'p_sample': nrm(ks[3], (DEPTH, DEC_BATCH, DEC_SEQ, D_PLE), 1.0),
        'state_a_h': nrm(ks[4], (N_AB, DEC_BATCH, D_A), 0.5),
        'state_a_conv': nrm(ks[5], (N_AB, DEC_BATCH, CONV_W - 1, D_A), 1.0),
        'state_b_ssm': nrm(ks[6], (N_AB, DEC_BATCH, H_B, HD_B, N_B), 0.3),
        'state_b_conv': nrm(ks[7], (N_AB, DEC_BATCH, CONV_W - 1, CONV_DIM_B), 1.0),
        'state_c': nrm(ks[8], (N_C, DEC_BATCH, H_C, DK_C, DV_C), 0.3),
        'norm_w': 1.0 + nrm(ks[9], (DEPTH, D_MODEL), 0.02),
        'norm_f': 1.0 + nrm(ks[10], (D_MODEL,), 0.02),
        'ab_w_in': nrm(ks[11], (N_AB, D_MODEL, IN_AB), D_MODEL ** -0.5),
        'a_conv_w': nrm(ks[12], (N_AB, CONV_W, D_A), 0.5),
        'a_conv_b': nrm(ks[13], (N_AB, D_A), 0.01),
        'a_w_r': nrm(ks[14], (N_AB, A_BLOCKS, A_BLK, A_BLK), A_BLK ** -0.5),
        'a_b_r': nrm(ks[16], (N_AB, D_A), 0.01),
        'a_w_i': nrm(ks[17], (N_AB, A_BLOCKS, A_BLK, A_BLK), A_BLK ** -0.5),
        'a_b_i': nrm(ks[20], (N_AB, D_A), 0.01),
        'a_lam': a_lam,
        'b_conv_w': nrm(ks[21], (N_AB, CONV_W, CONV_DIM_B), 0.5),
        'b_conv_b': nrm(ks[22], (N_AB, CONV_DIM_B), 0.01),
        'b_dt_bias': b_dt_bias,
        'b_a_log': b_a_log,
        'b_d': 1.0 + nrm(ks[23], (N_AB, H_B), 0.02),
        'b_norm_w': 1.0 + nrm(ks[24], (N_AB, D_B), 0.02),
        'ab_w_out': nrm(ks[25], (N_AB, D_A + D_B, D_MODEL), (D_A + D_B) ** -0.5),
        'c_w_in': nrm(ks[26], (N_C, D_MODEL, IN_C), D_MODEL ** -0.5),
        'c_lb': nrm(ks[27], (DEPTH, HK_C), 0.1),
        'c_norm_w': 1.0 + nrm(ks[28], (N_C, D_C), 0.02),
        'c_w_out': nrm(ks[29], (N_C, D_C, D_MODEL), D_C ** -0.5),
        'ple_proj': nrm(ks[30], (DEPTH, D_PLE, D_MODEL), D_PLE ** -0.5),
        'ple_gate': nrm(ks[31], (DEPTH, D_MODEL, D_MODEL), D_MODEL ** -0.5),
    }


def reference(x_prompt, x_sample, p_prompt, p_sample, state_a_h, state_a_conv, state_b_ssm, state_b_conv,
              state_c, norm_w, norm_f, ab_w_in, a_conv_w, a_conv_b, a_w_r, a_b_r, a_w_i, a_b_i, a_lam,
              b_conv_w, b_conv_b, b_dt_bias, b_a_log, b_d, b_norm_w, ab_w_out, c_w_in, c_lb, c_norm_w,
              c_w_out, ple_proj, ple_gate):
    lb_tab = jnp.cumsum(jax.nn.softmax(c_lb.astype(F32), axis=0), axis=0)
    lb_tab = lb_tab - lb_tab[:1]
    bp = x_prompt.shape[0]
    hp, hs = x_prompt, x_sample
    ah_p, ac_p, bs_p, bc_p, c_p = [], [], [], [], []
    ah_s, ac_s, bs_s, bc_s, c_s = [], [], [], [], []
    for i in range(DEPTH):
        j = i // 2
        if i % 2 == 0:
            w = (norm_w[i], ab_w_in[j], a_conv_w[j], a_conv_b[j], a_w_r[j], a_b_r[j], a_w_i[j], a_b_i[j],
                 a_lam[j], b_conv_w[j], b_conv_b[j], b_dt_bias[j], b_a_log[j], b_d[j], b_norm_w[j],
                 ab_w_out[j], ple_proj[i], ple_gate[i])
            hp, s1, s2, s3, s4 = layer_ab(
                hp, p_prompt[i], jnp.zeros((bp, D_A), F32), jnp.zeros((bp, CONV_W - 1, D_A), x_prompt.dtype),
                jnp.zeros((bp, H_B, HD_B, N_B), F32), jnp.zeros((bp, CONV_W - 1, CONV_DIM_B), x_prompt.dtype), *w)
            ah_p.append(s1); ac_p.append(s2); bs_p.append(s3); bc_p.append(s4)
            hs, s1, s2, s3, s4 = layer_ab(hs, p_sample[i], state_a_h[j], state_a_conv[j], state_b_ssm[j],
                                          state_b_conv[j], *w)
            ah_s.append(s1); ac_s.append(s2); bs_s.append(s3); bc_s.append(s4)
        else:
            lb = lb_tab[i]
            w = (norm_w[i], c_w_in[j], c_norm_w[j], c_w_out[j], ple_proj[i], ple_gate[i])
            hp, s1 = layer_c(hp, p_prompt[i], jnp.zeros((bp, H_C, DK_C, DV_C), F32), lb, *w)
            c_p.append(s1)
            hs, s1 = layer_c(hs, p_sample[i], state_c[j], lb, *w)
            c_s.append(s1)
    y_prompt = rmsnorm(hp, norm_f)
    y_sample = rmsnorm(hs, norm_f)
    return (y_prompt, y_sample,
            jnp.stack(ah_p), jnp.stack(ac_p), jnp.stack(bs_p), jnp.stack(bc_p), jnp.stack(c_p),
            jnp.stack(ah_s), jnp.stack(ac_s), jnp.stack(bs_s), jnp.stack(bc_s), jnp.stack(c_s))
```

```python
import functools

import jax
import jax.numpy as jnp
from jax import lax
from jax.experimental import pallas as pl
from jax.experimental.pallas import tpu as pltpu

F32 = jnp.float32
BF16 = jnp.bfloat16

D_MODEL = 1024
D_PLE = 256
EPS = 1e-6
CONV_W = 4
D_A = D_MODEL
A_BLOCKS = 8
A_BLK = D_A // A_BLOCKS
LRU_C = 8.0
D_B = D_MODEL
HD_B = 64
H_B = D_B // HD_B
N_B = 128
G_B = 2
CONV_DIM_B = D_B + 2 * G_B * N_B
D_C = 2 * D_MODEL
H_C = 16
DK_C = 128
DV_C = D_C // H_C
HK_C = H_C * DK_C
AB_MAIN = 2 * D_A + D_B + CONV_DIM_B
IN_C = 2 * HK_C + 2 * D_C

LANE = 128
SUBLANE = 8
VMEM_LIMIT = 52 * 1024 * 1024

SSD_CHUNK = 128
HG_CHUNK = 64
HG_LEVELS = 6
HG_BLOCK = 256
SAMPLE_BB = 8

_DN_TR = (((1,), (1,)), ((), ()))
_DN_TL = (((0,), (0,)), ((), ()))


def _dot(a, b):
    return jnp.dot(a, b, preferred_element_type=F32)


def _dot_tr(a, b):
    return lax.dot_general(a, b, _DN_TR, preferred_element_type=F32)


def _dot_tl(a, b):
    return lax.dot_general(a, b, _DN_TL, preferred_element_type=F32)


def _silu(x):
    return x * jax.nn.sigmoid(x)


def _rmsnorm(x, w):
    return x * lax.rsqrt(jnp.mean(x * x, axis=-1, keepdims=True) + EPS) * w


def _params(*sem):
    return pltpu.CompilerParams(dimension_semantics=sem, vmem_limit_bytes=VMEM_LIMIT)


def _in_proj_kernel(x_ref, nw_ref, w_ref, *rest, has_extra, head_major):
    if has_extra:
        wx_ref, o_ref, ox_ref, xn_ref = rest
    else:
        o_ref, xn_ref = rest

    @pl.when(pl.program_id(1) == 0)
    def _():
        xn_ref[...] = _rmsnorm(x_ref[...], nw_ref[...]).astype(BF16)
        if has_extra:
            ox_ref[...] = _dot(xn_ref[...], wx_ref[...])

    acc = _dot(xn_ref[...], w_ref[...])
    if head_major:
        for k in range(acc.shape[1] // LANE):
            o_ref[k] = acc[:, k * LANE:(k + 1) * LANE]
    else:
        o_ref[...] = acc


def _in_proj(x, nw, w, w_extra=None, *, head_major=False, tn=512):
    m, k = x.shape
    n = w.shape[1]
    tm = min(m, 1024)
    assert m % tm == 0 and n % tn == 0 and tn % LANE == 0
    grid = (m // tm, n // tn)
    in_specs = [pl.BlockSpec((tm, k), lambda i, j: (i, 0)),
                pl.BlockSpec((1, k), lambda i, j: (0, 0)),
                pl.BlockSpec((k, tn), lambda i, j: (0, j))]
    args = [x, nw.reshape(1, k), w]
    if head_major:
        out_shape = [jax.ShapeDtypeStruct((n // LANE, m, LANE), F32)]
        out_specs = [pl.BlockSpec((tn // LANE, tm, LANE), lambda i, j: (j, i, 0))]
    else:
        out_shape = [jax.ShapeDtypeStruct((m, n), F32)]
        out_specs = [pl.BlockSpec((tm, tn), lambda i, j: (i, j))]
    if w_extra is not None:
        in_specs.append(pl.BlockSpec((k, LANE), lambda i, j: (0, 0)))
        args.append(w_extra)
        out_shape.append(jax.ShapeDtypeStruct((m, LANE), F32))
        out_specs.append(pl.BlockSpec((tm, LANE), lambda i, j: (i, 0)))
    outs = pl.pallas_call(
        functools.partial(_in_proj_kernel, has_extra=w_extra is not None, head_major=head_major),
        grid=grid, in_specs=in_specs, out_specs=out_specs, out_shape=out_shape,
        scratch_shapes=[pltpu.VMEM((tm, k), BF16)],
        compiler_params=_params("parallel", "arbitrary"),
        name="in_proj",
    )(*args)
    return outs if w_extra is not None else outs[0]


def _out_proj_kernel(mix_ref, h_ref, p_ref, wo_ref, wg_ref, wp_ref, nf_ref, o_ref, *, head_major, final):
    if head_major:
        mix = jnp.concatenate([mix_ref[k] for k in range(mix_ref.shape[0])], axis=-1)
    else:
        mix = mix_ref[...]
    h = h_ref[...] + _dot(mix.astype(BF16), wo_ref[...])
    gate = jax.nn.sigmoid(_dot(h.astype(BF16), wg_ref[...]))
    pe = _dot(p_ref[...].astype(BF16), wp_ref[...])
    h = h + gate * pe
    if final:
        h = _rmsnorm(h, nf_ref[...])
    o_ref[...] = h


def _out_proj(mix, h, p, wo, wg, wp, nf, *, head_major, final):
    m, d = h.shape
    tm = min(m, 512)
    assert m % tm == 0
    if head_major:
        mix_spec = pl.BlockSpec((mix.shape[0], tm, LANE), lambda i: (0, i, 0))
    else:
        mix_spec = pl.BlockSpec((tm, mix.shape[1]), lambda i: (i, 0))
    const = lambda i: (0, 0)
    return pl.pallas_call(
        functools.partial(_out_proj_kernel, head_major=head_major, final=final),
        grid=(m // tm,),
        in_specs=[mix_spec,
                  pl.BlockSpec((tm, d), lambda i: (i, 0)),
                  pl.BlockSpec((tm, p.shape[1]), lambda i: (i, 0)),
                  pl.BlockSpec(wo.shape, const), pl.BlockSpec(wg.shape, const),
                  pl.BlockSpec(wp.shape, const), pl.BlockSpec((1, d), const)],
        out_specs=pl.BlockSpec((tm, d), lambda i: (i, 0)),
        out_shape=jax.ShapeDtypeStruct((m, d), F32),
        compiler_params=_params("parallel"),
        name="out_proj",
    )(mix, h, p, wo, wg, wp, nf.reshape(1, d))


def _lru_gates(xc, wr_ref, br_ref, wi_ref, bi_ref, lam_ref):
    xcb = xc.astype(BF16)
    r_parts, i_parts = [], []
    for k in range(A_BLOCKS):
        xk = xcb[:, k * A_BLK:(k + 1) * A_BLK]
        r_parts.append(_dot(xk, wr_ref[k]))
        i_parts.append(_dot(xk, wi_ref[k]))
    r = jax.nn.sigmoid(jnp.concatenate(r_parts, axis=-1) + br_ref[...])
    gi = jax.nn.sigmoid(jnp.concatenate(i_parts, axis=-1) + bi_ref[...])
    log_a = (-LRU_C) * r * jax.nn.softplus(-lam_ref[...])
    a = jnp.exp(log_a)
    u = jnp.sqrt(jnp.tanh(-log_a) * (a * a + 1.0)) * (gi * xc)
    return a, u


def _group_norm_gate(y, bx, z, dexp_ref, bnw_ref):
    y = (y + dexp_ref[...] * bx) * _silu(z)
    gw = D_B // G_B
    parts = []
    for g in range(G_B):
        yg = y[:, g * gw:(g + 1) * gw]
        parts.append(yg * lax.rsqrt(jnp.mean(yg * yg, axis=-1, keepdims=True) + EPS))
    return jnp.concatenate(parts, axis=-1) * bnw_ref[...]


def _ab_prompt_kernel(ax_ref, ag_ref, z_ref, xbc_ref, dt_ref,
                      acw_ref, acb_ref, wr_ref, br_ref, wi_ref, bi_ref, lam_ref,
                      bcw_ref, bcb_ref, dtb_ref, alog_ref, dexp_ref, bnw_ref,
                      mix_ref, ah_ref, ac_ref, bs_ref, bc_ref,
                      xpa_ref, xpb_ref, h_ref, s_ref):
    c = pl.program_id(1)
    last = pl.num_programs(1) - 1
    t = SSD_CHUNK
    tail = SUBLANE

    @pl.when(c == 0)
    def _():
        xpa_ref[0:tail, :] = jnp.zeros((tail, D_A), F32)
        xpb_ref[0:tail, :] = jnp.zeros((tail, CONV_DIM_B), F32)
        h_ref[...] = jnp.zeros_like(h_ref)
        s_ref[...] = jnp.zeros_like(s_ref)

    def conv(x, xp_ref, w_ref, b_ref):
        xp_ref[tail:tail + t, :] = x
        y = b_ref[...]
        for k in range(CONV_W):
            off = tail - (CONV_W - 1) + k
            y = y + w_ref[k:k + 1, :] * xp_ref[off:off + t, :]
        xp_ref[0:tail, :] = x[t - tail:t, :]
        return y

    ax = ax_ref[...]
    xc = conv(ax, xpa_ref, acw_ref, acb_ref)
    a, u = _lru_gates(xc, wr_ref, br_ref, wi_ref, bi_ref, lam_ref)
    row = lax.broadcasted_iota(jnp.int32, (t, D_A), 0)
    step = 1
    while step < t:
        m = row >= step
        u = jnp.where(m, a * pltpu.roll(u, step, 0) + u, u)
        a = jnp.where(m, a * pltpu.roll(a, step, 0), a)
        step *= 2
    h = u + a * h_ref[0:1, :]
    h_ref[0:1, :] = h[t - 1:t, :]
    mix_ref[:, 0:D_A] = (h * _silu(ag_ref[...])).astype(BF16)

    xb = xbc_ref[...]
    xbc = _silu(conv(xb, xpb_ref, bcw_ref, bcb_ref))
    bx = xbc[:, 0:D_B]
    bxb = bx.astype(BF16)
    dt = jax.nn.softplus(dt_ref[...] + dtb_ref[...])
    adt = dt * (-jnp.exp(alog_ref[...]))
    ti = lax.broadcasted_iota(jnp.int32, (t, t), 0)
    si = lax.broadcasted_iota(jnp.int32, (t, t), 1)
    causal = ti >= si
    acs = jnp.dot(causal.astype(F32), adt, preferred_element_type=F32,
                  precision=lax.Precision.HIGHEST)
    a_last = acs[t - 1:t, :]
    wq = jnp.exp(a_last - acs) * dt
    eacs = jnp.exp(acs)
    ealast = jnp.exp(a_last)
    acs_t = acs.T
    dt_t = dt.T
    lane = lax.broadcasted_iota(jnp.int32, (t, LANE), 1)
    rowi = lax.broadcasted_iota(jnp.int32, (LANE, N_B), 0)
    hpg = H_B // G_B
    ys = []
    cb = None
    for j in range(H_B // 2):
        g = (2 * j) // hpg
        bg = xbc[:, D_B + g * N_B:D_B + (g + 1) * N_B]
        cg = xbc[:, D_B + G_B * N_B + g * N_B:D_B + G_B * N_B + (g + 1) * N_B]
        if (2 * j) % hpg == 0:
            cb = _dot_tr(cg.astype(BF16), bg.astype(BF16))
        xpair = bxb[:, j * LANE:(j + 1) * LANE]
        sp = s_ref[j]
        spb = sp.astype(BF16)
        y_h, up_h = [], []
        for hh in range(2):
            hd = 2 * j + hh
            seg = jnp.broadcast_to(acs[:, hd:hd + 1], (t, t)) - jnp.broadcast_to(acs_t[hd:hd + 1, :], (t, t))
            lmat = jnp.exp(jnp.where(causal, seg, -1e30))
            mmat = (cb * lmat * jnp.broadcast_to(dt_t[hd:hd + 1, :], (t, t))).astype(BF16)
            ec = (jnp.broadcast_to(eacs[:, hd:hd + 1], (t, N_B)) * cg).astype(BF16)
            y_h.append(_dot(mmat, xpair) + _dot_tr(ec, spb))
            bw = (bg * jnp.broadcast_to(wq[:, hd:hd + 1], (t, N_B))).astype(BF16)
            up_h.append(_dot_tl(xpair, bw))
        ys.append(jnp.where(lane < HD_B, y_h[0], y_h[1]))
        dec = jnp.where(rowi < HD_B,
                        jnp.broadcast_to(ealast[:, 2 * j:2 * j + 1], (LANE, N_B)),
                        jnp.broadcast_to(ealast[:, 2 * j + 1:2 * j + 2], (LANE, N_B)))
        s_ref[j] = dec * sp + jnp.where(rowi < HD_B, up_h[0], up_h[1])
    y = jnp.concatenate(ys, axis=-1)
    mix_ref[:, D_A:D_A + D_B] = _group_norm_gate(y, bx, z_ref[...], dexp_ref, bnw_ref).astype(BF16)

    @pl.when(c == last)
    def _():
        ah_ref[0] = h[t - 1:t, :]
        ac_ref[0] = ax[t - (CONV_W - 1):t, :]
        bc_ref[0] = xb[t - (CONV_W - 1):t, :]
        for j in range(H_B // 2):
            sj = s_ref[j]
            bs_ref[0, 2 * j] = sj[0:HD_B, :]
            bs_ref[0, 2 * j + 1] = sj[HD_B:2 * HD_B, :]


def _ab_prompt(u, dtr, bsz, seq, w):
    t = SSD_CHUNK
    assert seq % t == 0
    nc = seq // t
    m = bsz * seq
    rows = lambda b, c: b * nc + c
    cvec = lambda b, c: (0, 0)
    c3 = lambda b, c: (0, 0, 0)
    in_specs = [
        pl.BlockSpec((t, D_A), lambda b, c: (rows(b, c), 0)),
        pl.BlockSpec((t, D_A), lambda b, c: (rows(b, c), 1)),
        pl.BlockSpec((t, D_B), lambda b, c: (rows(b, c), 2)),
        pl.BlockSpec((t, CONV_DIM_B), lambda b, c: (rows(b, c), 2)),
        pl.BlockSpec((t, LANE), lambda b, c: (rows(b, c), 0)),
        pl.BlockSpec((CONV_W, D_A), cvec), pl.BlockSpec((1, D_A), cvec),
        pl.BlockSpec((A_BLOCKS, A_BLK, A_BLK), c3), pl.BlockSpec((1, D_A), cvec),
        pl.BlockSpec((A_BLOCKS, A_BLK, A_BLK), c3), pl.BlockSpec((1, D_A), cvec),
        pl.BlockSpec((1, D_A), cvec),
        pl.BlockSpec((CONV_W, CONV_DIM_B), cvec), pl.BlockSpec((1, CONV_DIM_B), cvec),
        pl.BlockSpec((1, LANE), cvec), pl.BlockSpec((1, LANE), cvec),
        pl.BlockSpec((1, D_B), cvec), pl.BlockSpec((1, D_B), cvec),
    ]
    out_shape = [
        jax.ShapeDtypeStruct((m, D_A + D_B), BF16),
        jax.ShapeDtypeStruct((bsz, 1, D_A), F32),
        jax.ShapeDtypeStruct((bsz, CONV_W - 1, D_A), F32),
        jax.ShapeDtypeStruct((bsz, H_B, HD_B, N_B), F32),
        jax.ShapeDtypeStruct((bsz, CONV_W - 1, CONV_DIM_B), F32),
    ]
    out_specs = [
        pl.BlockSpec((t, D_A + D_B), lambda b, c: (rows(b, c), 0)),
        pl.BlockSpec((1, 1, D_A), lambda b, c: (b, 0, 0)),
        pl.BlockSpec((1, CONV_W - 1, D_A), lambda b, c: (b, 0, 0)),
        pl.BlockSpec((1, H_B, HD_B, N_B), lambda b, c: (b, 0, 0, 0)),
        pl.BlockSpec((1, CONV_W - 1, CONV_DIM_B), lambda b, c: (b, 0, 0)),
    ]
    mix, ah, ac, bs, bc = pl.pallas_call(
        _ab_prompt_kernel, grid=(bsz, nc), in_specs=in_specs, out_specs=out_specs, out_shape=out_shape,
        scratch_shapes=[pltpu.VMEM((SUBLANE + t, D_A), F32), pltpu.VMEM((SUBLANE + t, CONV_DIM_B), F32),
                        pltpu.VMEM((SUBLANE, D_A), F32), pltpu.VMEM((H_B // 2, 2 * HD_B, N_B), F32)],
        compiler_params=_params("parallel", "arbitrary"),
        name="ab_prompt",
    )(u, u, u, u, dtr, w["acw"], w["acb"], w["wr"], w["br"], w["wi"], w["bi"], w["lam"],
      w["bcw"], w["bcb"], w["dtb"], w["alog"], w["dexp"], w["bnw"])
    return mix, ah.reshape(bsz, D_A), ac, bs, bc


def _ab_sample_rows_kernel(ax_ref, ag_ref, xbc_ref, dt_ref, sah_ref, sac_ref, sbc_ref,
                           acw_ref, acb_ref, wr_ref, br_ref, wi_ref, bi_ref, lam_ref,
                           bcw_ref, bcb_ref, dtb_ref,
                           aout_ref, ah_ref, ac_ref, bc_ref, xact_ref, dts_ref):
    def conv1(x, buf_ref, w_ref, b_ref, nbuf_ref, width):
        y = b_ref[...] + w_ref[CONV_W - 1:CONV_W, :] * x
        for k in range(CONV_W - 1):
            y = y + w_ref[k:k + 1, :] * buf_ref[:, k * width:(k + 1) * width]
        for k in range(CONV_W - 2):
            nbuf_ref[:, k * width:(k + 1) * width] = buf_ref[:, (k + 1) * width:(k + 2) * width]
        nbuf_ref[:, (CONV_W - 2) * width:(CONV_W - 1) * width] = x
        return y

    xc = conv1(ax_ref[...], sac_ref, acw_ref, acb_ref, ac_ref, D_A)
    a, u = _lru_gates(xc, wr_ref, br_ref, wi_ref, bi_ref, lam_ref)
    h = a * sah_ref[...] + u
    ah_ref[...] = h
    aout_ref[...] = h * _silu(ag_ref[...])
    xact_ref[...] = _silu(conv1(xbc_ref[...], sbc_ref, bcw_ref, bcb_ref, bc_ref, CONV_DIM_B))
    dts_ref[...] = jax.nn.softplus(dt_ref[...] + dtb_ref[...])


def _pad_rows_t(x):
    pad = jnp.zeros((LANE - x.shape[0], x.shape[1]), F32)
    return jnp.concatenate([x, pad], axis=0).T


def _ab_sample_state_kernel(s_ref, xact_ref, dts_ref, z_ref, alog_ref, dexp_ref, bnw_ref,
                            so_ref, bout_ref, y_ref):
    bb = SAMPLE_BB
    xact = xact_ref[...]
    bx = xact[:, 0:D_B]
    dts = dts_ref[...]
    dec_t = _pad_rows_t(jnp.exp(dts * (-jnp.exp(alog_ref[...]))))
    dts_t = _pad_rows_t(dts)
    hpg = H_B // G_B
    for j in range(H_B // 2):
        g = (2 * j) // hpg
        xt = _pad_rows_t(bx[:, j * LANE:(j + 1) * LANE])
        dtp = jnp.concatenate([jnp.broadcast_to(dts_t[2 * j:2 * j + 1, :], (HD_B, LANE)),
                               jnp.broadcast_to(dts_t[2 * j + 1:2 * j + 2, :], (HD_B, LANE))], axis=0)
        xdt = xt * dtp
        for i in range(bb):
            brow = jnp.broadcast_to(xact[i:i + 1, D_B + g * N_B:D_B + (g + 1) * N_B], (2 * HD_B, N_B))
            crow = jnp.broadcast_to(
                xact[i:i + 1, D_B + G_B * N_B + g * N_B:D_B + G_B * N_B + (g + 1) * N_B], (2 * SUBLANE, N_B))
            upd = jnp.broadcast_to(xdt[:, i:i + 1], (2 * HD_B, N_B)) * brow
            news = []
            for hh in range(2):
                hd = 2 * j + hh
                dec = jnp.broadcast_to(dec_t[hd:hd + 1, i:i + 1], (HD_B, N_B))
                sn = dec * s_ref[i, hd] + upd[hh * HD_B:(hh + 1) * HD_B, :]
                so_ref[i, hd] = sn
                news.append(sn)
            spair = jnp.concatenate(news, axis=0).astype(BF16)
            yrow = _dot_tr(crow.astype(BF16), spair)
            y_ref[i:i + 1, j * LANE:(j + 1) * LANE] = yrow[0:1, :]
    bout_ref[...] = _group_norm_gate(y_ref[...], bx, z_ref[...], dexp_ref, bnw_ref)


def _ab_sample(u, dtr, s_ah, s_ac, s_bs, s_bc, w):
    bsz = u.shape[0]
    full = lambda shape: pl.BlockSpec(shape, lambda i: tuple(0 for _ in shape))
    cw = CONV_W - 1
    aout, ah, ac, bc, xact, dts = pl.pallas_call(
        _ab_sample_rows_kernel, grid=(1,),
        in_specs=[pl.BlockSpec((bsz, D_A), lambda i: (0, 0)), pl.BlockSpec((bsz, D_A), lambda i: (0, 1)),
                  pl.BlockSpec((bsz, CONV_DIM_B), lambda i: (0, 2)), full((bsz, LANE)),
                  full((bsz, D_A)), full((bsz, cw * D_A)), full((bsz, cw * CONV_DIM_B)),
                  full((CONV_W, D_A)), full((1, D_A)),
                  full((A_BLOCKS, A_BLK, A_BLK)), full((1, D_A)),
                  full((A_BLOCKS, A_BLK, A_BLK)), full((1, D_A)), full((1, D_A)),
                  full((CONV_W, CONV_DIM_B)), full((1, CONV_DIM_B)), full((1, LANE))],
        out_specs=[full((bsz, D_A)), full((bsz, D_A)), full((bsz, cw * D_A)), full((bsz, cw * CONV_DIM_B)),
                   full((bsz, CONV_DIM_B)), full((bsz, LANE))],
        out_shape=[jax.ShapeDtypeStruct((bsz, D_A), F32), jax.ShapeDtypeStruct((bsz, D_A), F32),
                   jax.ShapeDtypeStruct((bsz, cw * D_A), F32), jax.ShapeDtypeStruct((bsz, cw * CONV_DIM_B), F32),
                   jax.ShapeDtypeStruct((bsz, CONV_DIM_B), F32), jax.ShapeDtypeStruct((bsz, LANE), F32)],
        compiler_params=_params("arbitrary"),
        name="ab_sample_rows",
    )(u, u, u, dtr, s_ah, s_ac.reshape(bsz, cw * D_A), s_bc.reshape(bsz, cw * CONV_DIM_B),
      w["acw"], w["acb"], w["wr"], w["br"], w["wi"], w["bi"], w["lam"], w["bcw"], w["bcb"], w["dtb"])

    bb = SAMPLE_BB
    assert bsz % bb == 0
    cvec = lambda i: (0, 0)
    bs, bout = pl.pallas_call(
        _ab_sample_state_kernel, grid=(bsz // bb,),
        in_specs=[pl.BlockSpec((bb, H_B, HD_B, N_B), lambda i: (i, 0, 0, 0)),
                  pl.BlockSpec((bb, CONV_DIM_B), lambda i: (i, 0)),
                  pl.BlockSpec((bb, LANE), lambda i: (i, 0)),
                  pl.BlockSpec((bb, D_B), lambda i: (i, 2)),
                  pl.BlockSpec((1, LANE), cvec), pl.BlockSpec((1, D_B), cvec), pl.BlockSpec((1, D_B), cvec)],
        out_specs=[pl.BlockSpec((bb, H_B, HD_B, N_B), lambda i: (i, 0, 0, 0)),
                   pl.BlockSpec((bb, D_B), lambda i: (i, 0))],
        out_shape=[jax.ShapeDtypeStruct(s_bs.shape, F32), jax.ShapeDtypeStruct((bsz, D_B), F32)],
        scratch_shapes=[pltpu.VMEM((bb, D_B), F32)],
        compiler_params=_params("parallel"),
        name="ab_sample_state",
    )(s_bs, xact, dts, u, w["alog"], w["dexp"], w["bnw"])
    mix = jnp.concatenate([aout, bout], axis=-1)
    return mix, ah, ac.reshape(bsz, cw, D_A), bs, bc.reshape(bsz, cw, CONV_DIM_B)


def _hg_lower_bound(clb, layer):
    mx = jnp.max(clb, axis=0, keepdims=True)
    ex = jnp.exp(clb - mx)
    return jnp.sum(ex[1:layer + 1], axis=0, keepdims=True) / jnp.sum(ex, axis=0, keepdims=True)


def _hg_gates(fx, lb):
    e = jnp.exp(-jnp.abs(fx))
    r = 1.0 / (1.0 + e)
    er = e * r
    pos = fx >= 0.0
    f = lb + (1.0 - lb) * jnp.where(pos, r, er)
    k = (1.0 - lb) * jnp.where(pos, er, r)
    return f, k


def _hg_out(o, gate, cnw):
    return o * lax.rsqrt(jnp.mean(o * o, axis=-1, keepdims=True) + EPS) * cnw * _silu(gate)


def _hg_gamma():
    import numpy as np
    q = HG_CHUNK
    t = np.arange(q)[:, None]
    tau = np.arange(q)[None, :]
    mats = [(tau <= t)]
    for l in range(HG_LEVELS):
        w = 1 << l
        ref = (t // (2 * w)) * (2 * w) + w - 1
        upper = (t % (2 * w)) >= w
        mats.append(np.where(upper, (tau > ref) & (tau <= t), (tau > t) & (tau <= ref)))
    gam = np.concatenate(mats, axis=0).astype(np.float32)
    return jnp.asarray(np.concatenate([gam, gam, gam], axis=1), dtype=BF16)


def _hg_level_table():
    import numpy as np
    q = HG_CHUNK
    t = np.arange(q)[:, None]
    s = np.arange(q)[None, :]
    x = t ^ s
    lvl = np.floor(np.log2(np.maximum(x, 1))).astype(np.int32)
    return jnp.asarray(np.where(t > s, lvl, -1).astype(np.int32))


def _c_prompt_kernel(q_ref, f_ref, v_ref, g_ref, clb_ref, cnw_ref, gam_ref, lvl_ref,
                     og_ref, cs_ref, st_ref, *, layer):
    c = pl.program_id(1)
    last = pl.num_programs(1) - 1
    qc = HG_CHUNK

    @pl.when(c == 0)
    def _():
        st_ref[...] = jnp.zeros_like(st_ref)

    trow = lax.broadcasted_iota(jnp.int32, (qc, DK_C), 0)
    lvl = lvl_ref[...]
    gam = gam_ref[...]

    def body(idx, carry):
        hd = idx // (HG_BLOCK // qc)
        ci = idx % (HG_BLOCK // qc)
        rows = pl.ds(pl.multiple_of(ci * qc, qc), qc)
        qh = q_ref[hd, rows, :] * (DK_C ** -0.5)
        v = v_ref[hd, rows, :]
        lb = _hg_lower_bound(clb_ref[hd], layer)
        f, kk = _hg_gates(f_ref[hd, rows, :], lb)
        g = jnp.log(f)
        g1 = g.astype(BF16)
        r1 = g - g1.astype(F32)
        g2 = r1.astype(BF16)
        g3 = (r1 - g2.astype(F32)).astype(BF16)
        sums = _dot(gam, jnp.concatenate([g1, g2, g3], axis=0))
        bcum = sums[0:qc]
        blast = bcum[qc - 1:qc, :]
        st = st_ref[hd]
        o = _dot_tr((qh * jnp.exp(bcum)).astype(BF16), st.astype(BF16))
        amat = jnp.zeros((qc, qc), F32)
        for l in range(HG_LEVELS):
            upper = (trow & (1 << l)) != 0
            x = (jnp.where(upper, qh, kk) * jnp.exp(sums[(l + 1) * qc:(l + 2) * qc])).astype(BF16)
            amat = jnp.where(lvl == l, _dot_tr(x, x), amat)
        vb = v.astype(BF16)
        o = o + _dot(amat.astype(BF16), vb) + jnp.sum(qh * kk, axis=-1, keepdims=True) * v
        kdec = (kk * jnp.exp(blast - bcum)).astype(BF16)
        st_ref[hd] = st * jnp.exp(blast) + _dot_tl(vb, kdec)
        og_ref[hd, rows, :] = _hg_out(o, g_ref[hd, rows, :], cnw_ref[hd]).astype(BF16)
        return carry

    lax.fori_loop(0, H_C * (HG_BLOCK // qc), body, 0)

    @pl.when(c == last)
    def _():
        for hd in range(H_C):
            cs_ref[0, hd] = st_ref[hd].T


def _c_prompt(u, bsz, seq, w, layer):
    tb = HG_BLOCK
    assert seq % tb == 0 and tb % HG_CHUNK == 0 and (1 << HG_LEVELS) == HG_CHUNK
    nc = seq // tb
    m = bsz * seq
    depth = w["clb"].shape[1]

    def part(k):
        return pl.BlockSpec((H_C, tb, LANE), lambda b, c: (k, b * nc + c, 0))

    c2 = lambda b, c: (0, 0)
    c3 = lambda b, c: (0, 0, 0)
    og, cs = pl.pallas_call(
        functools.partial(_c_prompt_kernel, layer=layer), grid=(bsz, nc),
        in_specs=[part(0), part(1), part(2), part(3),
                  pl.BlockSpec((H_C, depth, DK_C), c3), pl.BlockSpec((H_C, 1, DV_C), c3),
                  pl.BlockSpec(w["gam"].shape, c2), pl.BlockSpec(w["lvl"].shape, c2)],
        out_specs=[pl.BlockSpec((H_C, tb, LANE), lambda b, c: (0, b * nc + c, 0)),
                   pl.BlockSpec((1, H_C, DK_C, DV_C), lambda b, c: (b, 0, 0, 0))],
        out_shape=[jax.ShapeDtypeStruct((H_C, m, DV_C), BF16),
                   jax.ShapeDtypeStruct((bsz, H_C, DK_C, DV_C), F32)],
        scratch_shapes=[pltpu.VMEM((H_C, DV_C, DK_C), F32)],
        compiler_params=_params("parallel", "arbitrary"),
        name="c_prompt",
    )(u, u, u, u, w["clb"], w["cnw"], w["gam"], w["lvl"])
    return og, cs


def _c_sample_kernel(q_ref, f_ref, v_ref, g_ref, s_ref, clb_ref, cnw_ref, og_ref, so_ref, *, layer):
    bb = SAMPLE_BB
    for hd in range(H_C):
        lb = _hg_lower_bound(clb_ref[hd], layer)
        f, kk = _hg_gates(f_ref[hd], lb)
        f_t = _pad_rows_t(f)
        k_t = _pad_rows_t(kk)
        qs = q_ref[hd] * (DK_C ** -0.5)
        v = v_ref[hd]
        orows = []
        for i in range(bb):
            fcol = jnp.broadcast_to(f_t[:, i:i + 1], (DK_C, DV_C))
            kcol = jnp.broadcast_to(k_t[:, i:i + 1], (DK_C, DV_C))
            sn = fcol * s_ref[i, hd] + kcol * jnp.broadcast_to(v[i:i + 1, :], (DK_C, DV_C))
            so_ref[i, hd] = sn
            qrow = jnp.broadcast_to(qs[i:i + 1, :], (2 * SUBLANE, DK_C)).astype(BF16)
            orows.append(_dot(qrow, sn.astype(BF16))[0:1, :])
        o = jnp.concatenate(orows, axis=0)
        og_ref[hd] = _hg_out(o, g_ref[hd], cnw_ref[hd])


def _c_sample(u, s_c, w, layer):
    bsz = s_c.shape[0]
    bb = SAMPLE_BB
    assert bsz % bb == 0
    depth = w["clb"].shape[1]

    def part(k):
        return pl.BlockSpec((H_C, bb, LANE), lambda i: (k, i, 0))

    c3 = lambda i: (0, 0, 0)
    og, so = pl.pallas_call(
        functools.partial(_c_sample_kernel, layer=layer), grid=(bsz // bb,),
        in_specs=[part(0), part(1), part(2), part(3),
                  pl.BlockSpec((bb, H_C, DK_C, DV_C), lambda i: (i, 0, 0, 0)),
                  pl.BlockSpec((H_C, depth, DK_C), c3), pl.BlockSpec((H_C, 1, DV_C), c3)],
        out_specs=[pl.BlockSpec((H_C, bb, LANE), lambda i: (0, i, 0)),
                   pl.BlockSpec((bb, H_C, DK_C, DV_C), lambda i: (i, 0, 0, 0))],
        out_shape=[jax.ShapeDtypeStruct((H_C, bsz, DV_C), F32), jax.ShapeDtypeStruct(s_c.shape, F32)],
        compiler_params=_params("parallel"),
        name="c_sample",
    )(u, u, u, u, s_c, w["clb"], w["cnw"])
    return og, so


def _row(v, width=None):
    v = v.astype(F32).reshape(1, -1)
    if width is not None and v.shape[1] < width:
        v = jnp.pad(v, ((0, 0), (0, width - v.shape[1])))
    return v


def kernel(x_prompt, x_sample, p_prompt, p_sample, state_a_h, state_a_conv, state_b_ssm, state_b_conv, state_c,
           norm_w, norm_f, ab_w_in, a_conv_w, a_conv_b, a_w_r, a_b_r, a_w_i, a_b_i, a_lam, b_conv_w, b_conv_b,
           b_dt_bias, b_a_log, b_d, b_norm_w, ab_w_out, c_w_in, c_lb, c_norm_w, c_w_out, ple_proj, ple_gate):
    depth = norm_w.shape[0]
    bp, seq, _ = x_prompt.shape
    bs = x_sample.shape[0]
    hp = x_prompt.reshape(bp * seq, D_MODEL)
    hs = x_sample.reshape(bs, D_MODEL)
    gam, lvl = _hg_gamma(), _hg_level_table()
    clb = c_lb.astype(F32).reshape(depth, H_C, DK_C).transpose(1, 0, 2)

    ah_p, ac_p, bs_p, bc_p, c_p = [], [], [], [], []
    ah_s, ac_s, bs_s, bc_s, c_s = [], [], [], [], []
    for i in range(depth):
        j = i // 2
        final = i == depth - 1
        wg = ple_gate[i].astype(BF16)
        wp = ple_proj[i].astype(BF16)
        pp = p_prompt[i].reshape(bp * seq, D_PLE)
        ps = p_sample[i].reshape(bs, D_PLE)
        if i % 2 == 0:
            w_main = ab_w_in[j][:, :AB_MAIN].astype(BF16)
            w_dt = jnp.pad(ab_w_in[j][:, AB_MAIN:], ((0, 0), (0, LANE - H_B))).astype(BF16)
            wo = ab_w_out[j].astype(BF16)
            w = dict(acw=a_conv_w[j].astype(F32), acb=_row(a_conv_b[j]),
                     wr=a_w_r[j].astype(BF16), br=_row(a_b_r[j]), wi=a_w_i[j].astype(BF16), bi=_row(a_b_i[j]),
                     lam=_row(a_lam[j]), bcw=b_conv_w[j].astype(F32), bcb=_row(b_conv_b[j]),
                     dtb=_row(b_dt_bias[j], LANE), alog=_row(b_a_log[j], LANE),
                     dexp=_row(jnp.repeat(b_d[j], HD_B)), bnw=_row(b_norm_w[j]))
            u, dtr = _in_proj(hp, norm_w[i], w_main, w_dt)
            mix, s1, s2, s3, s4 = _ab_prompt(u, dtr, bp, seq, w)
            ah_p.append(s1); ac_p.append(s2); bs_p.append(s3); bc_p.append(s4)
            hp = _out_proj(mix, hp, pp, wo, wg, wp, norm_f, head_major=False, final=final)
            u, dtr = _in_proj(hs, norm_w[i], w_main, w_dt)
            mix, s1, s2, s3, s4 = _ab_sample(u, dtr, state_a_h[j], state_a_conv[j], state_b_ssm[j],
                                             state_b_conv[j], w)
            ah_s.append(s1); ac_s.append(s2); bs_s.append(s3); bc_s.append(s4)
            hs = _out_proj(mix, hs, ps, wo, wg, wp, norm_f, head_major=False, final=final)
        else:
            w_in = c_w_in[j].astype(BF16)
            wo = c_w_out[j].astype(BF16)
            w = dict(clb=clb, cnw=c_norm_w[j].astype(F32).reshape(H_C, 1, DV_C), gam=gam, lvl=lvl)
            u = _in_proj(hp, norm_w[i], w_in, head_major=True)
            og, s1 = _c_prompt(u, bp, seq, w, i)
            c_p.append(s1)
            hp = _out_proj(og, hp, pp, wo, wg, wp, norm_f, head_major=True, final=final)
            u = _in_proj(hs, norm_w[i], w_in, head_major=True)
            og, s1 = _c_sample(u, state_c[j], w, i)
            c_s.append(s1)
            hs = _out_proj(og, hs, ps, wo, wg, wp, norm_f, head_major=True, final=final)
    return (hp.reshape(bp, seq, D_MODEL), hs.reshape(bs, 1, D_MODEL),
            jnp.stack(ah_p), jnp.stack(ac_p), jnp.stack(bs_p), jnp.stack(bc_p), jnp.stack(c_p),
            jnp.stack(ah_s), jnp.stack(ac_s), jnp.stack(bs_s), jnp.stack(bc_s), jnp.stack(c_s))
```

```python
import functools

import jax
import jax.numpy as jnp
from jax import lax
from jax.experimental import pallas as pl
from jax.experimental.pallas import tpu as pltpu

F32 = jnp.float32
BF16 = jnp.bfloat16

D_MODEL = 1024
D_PLE = 256
EPS = 1e-6
CONV_W = 4
D_A = D_MODEL
A_BLOCKS = 8
A_BLK = D_A // A_BLOCKS
LRU_C = 8.0
D_B = D_MODEL
HD_B = 64
H_B = D_B // HD_B
N_B = 128
G_B = 2
CONV_DIM_B = D_B + 2 * G_B * N_B
D_C = 2 * D_MODEL
H_C = 16
DK_C = 128
DV_C = D_C // H_C
HK_C = H_C * DK_C
AB_MAIN = 2 * D_A + D_B + CONV_DIM_B
IN_C = 2 * HK_C + 2 * D_C

LANE = 128
SUBLANE = 8
VMEM_LIMIT = 52 * 1024 * 1024

PROJ_MAX_TN = 2048
SSD_CHUNK = 128
HG_CHUNK = 64
HG_LEVELS = 6
HG_BLOCK = 256
HG_UNROLL = 8
SAMPLE_BB = 8

_DN_TR = (((1,), (1,)), ((), ()))
_DN_TL = (((0,), (0,)), ((), ()))


def _dot(a, b):
    return jnp.dot(a, b, preferred_element_type=F32)


def _dot_tr(a, b):
    return lax.dot_general(a, b, _DN_TR, preferred_element_type=F32)


def _dot_tl(a, b):
    return lax.dot_general(a, b, _DN_TL, preferred_element_type=F32)


def _silu(x):
    return x * jax.nn.sigmoid(x)


def _rmsnorm(x, w):
    return x * lax.rsqrt(jnp.mean(x * x, axis=-1, keepdims=True) + EPS) * w


def _params(*sem):
    return pltpu.CompilerParams(dimension_semantics=sem, vmem_limit_bytes=VMEM_LIMIT)


def _in_proj_kernel(x_ref, nw_ref, w_ref, *rest, has_extra, head_major):
    if has_extra:
        wx_ref, o_ref, ox_ref, xn_ref = rest
    else:
        o_ref, xn_ref = rest

    @pl.when(pl.program_id(1) == 0)
    def _():
        xn_ref[...] = _rmsnorm(x_ref[...], nw_ref[...]).astype(BF16)
        if has_extra:
            ox_ref[...] = _dot(xn_ref[...], wx_ref[...])

    acc = _dot(xn_ref[...], w_ref[...])
    if head_major:
        for k in range(acc.shape[1] // LANE):
            o_ref[k] = acc[:, k * LANE:(k + 1) * LANE]
    else:
        o_ref[...] = acc


def _in_proj(x, nw, w, w_extra=None, *, head_major=False):
    m, k = x.shape
    n = w.shape[1]
    tm = min(m, 1024)
    tn = next(c for c in range(PROJ_MAX_TN, 0, -LANE) if n % c == 0)
    assert m % tm == 0 and n % tn == 0 and tn % LANE == 0
    grid = (m // tm, n // tn)
    in_specs = [pl.BlockSpec((tm, k), lambda i, j: (i, 0)),
                pl.BlockSpec((1, k), lambda i, j: (0, 0)),
                pl.BlockSpec((k, tn), lambda i, j: (0, j))]
    args = [x, nw.reshape(1, k), w]
    if head_major:
        out_shape = [jax.ShapeDtypeStruct((n // LANE, m, LANE), F32)]
        out_specs = [pl.BlockSpec((tn // LANE, tm, LANE), lambda i, j: (j, i, 0))]
    else:
        out_shape = [jax.ShapeDtypeStruct((m, n), F32)]
        out_specs = [pl.BlockSpec((tm, tn), lambda i, j: (i, j))]
    if w_extra is not None:
        in_specs.append(pl.BlockSpec((k, LANE), lambda i, j: (0, 0)))
        args.append(w_extra)
        out_shape.append(jax.ShapeDtypeStruct((m, LANE), F32))
        out_specs.append(pl.BlockSpec((tm, LANE), lambda i, j: (i, 0)))
    outs = pl.pallas_call(
        functools.partial(_in_proj_kernel, has_extra=w_extra is not None, head_major=head_major),
        grid=grid, in_specs=in_specs, out_specs=out_specs, out_shape=out_shape,
        scratch_shapes=[pltpu.VMEM((tm, k), BF16)],
        compiler_params=_params("parallel", "arbitrary"),
        name="in_proj",
    )(*args)
    return outs if w_extra is not None else outs[0]


def _out_proj_kernel(mix_ref, h_ref, p_ref, wo_ref, wg_ref, wp_ref, nf_ref, o_ref, *, head_major, final):
    if head_major:
        mix = jnp.concatenate([mix_ref[k] for k in range(mix_ref.shape[0])], axis=-1)
    else:
        mix = mix_ref[...]
    h = h_ref[...] + _dot(mix.astype(BF16), wo_ref[...])
    gate = jax.nn.sigmoid(_dot(h.astype(BF16), wg_ref[...]))
    pe = _dot(p_ref[...].astype(BF16), wp_ref[...])
    h = h + gate * pe
    if final:
        h = _rmsnorm(h, nf_ref[...])
    o_ref[...] = h


def _out_proj(mix, h, p, wo, wg, wp, nf, *, head_major, final):
    m, d = h.shape
    tm = min(m, 512)
    assert m % tm == 0
    if head_major:
        mix_spec = pl.BlockSpec((mix.shape[0], tm, LANE), lambda i: (0, i, 0))
    else:
        mix_spec = pl.BlockSpec((tm, mix.shape[1]), lambda i: (i, 0))
    const = lambda i: (0, 0)
    return pl.pallas_call(
        functools.partial(_out_proj_kernel, head_major=head_major, final=final),
        grid=(m // tm,),
        in_specs=[mix_spec,
                  pl.BlockSpec((tm, d), lambda i: (i, 0)),
                  pl.BlockSpec((tm, p.shape[1]), lambda i: (i, 0)),
                  pl.BlockSpec(wo.shape, const), pl.BlockSpec(wg.shape, const),
                  pl.BlockSpec(wp.shape, const), pl.BlockSpec((1, d), const)],
        out_specs=pl.BlockSpec((tm, d), lambda i: (i, 0)),
        out_shape=jax.ShapeDtypeStruct((m, d), F32),
        compiler_params=_params("parallel"),
        name="out_proj",
    )(mix, h, p, wo, wg, wp, nf.reshape(1, d))


def _lru_gates(xc, wr_ref, br_ref, wi_ref, bi_ref, lam_ref):
    xcb = xc.astype(BF16)
    r_parts, i_parts = [], []
    for k in range(A_BLOCKS):
        xk = xcb[:, k * A_BLK:(k + 1) * A_BLK]
        r_parts.append(_dot(xk, wr_ref[k]))
        i_parts.append(_dot(xk, wi_ref[k]))
    r = jax.nn.sigmoid(jnp.concatenate(r_parts, axis=-1) + br_ref[...])
    gi = jax.nn.sigmoid(jnp.concatenate(i_parts, axis=-1) + bi_ref[...])
    log_a = (-LRU_C) * r * jax.nn.softplus(-lam_ref[...])
    a = jnp.exp(log_a)
    u = jnp.sqrt(jnp.tanh(-log_a) * (a * a + 1.0)) * (gi * xc)
    return a, u


def _group_norm_gate(y, bx, z, dexp_ref, bnw_ref):
    y = (y + dexp_ref[...] * bx) * _silu(z)
    gw = D_B // G_B
    parts = []
    for g in range(G_B):
        yg = y[:, g * gw:(g + 1) * gw]
        parts.append(yg * lax.rsqrt(jnp.mean(yg * yg, axis=-1, keepdims=True) + EPS))
    return jnp.concatenate(parts, axis=-1) * bnw_ref[...]


def _ab_prompt_kernel(ax_ref, ag_ref, z_ref, xbc_ref, dt_ref,
                      acw_ref, acb_ref, wr_ref, br_ref, wi_ref, bi_ref, lam_ref,
                      bcw_ref, bcb_ref, dtb_ref, alog_ref, dexp_ref, bnw_ref,
                      mix_ref, ah_ref, ac_ref, bs_ref, bc_ref,
                      xpa_ref, xpb_ref, h_ref, s_ref):
    c = pl.program_id(1)
    last = pl.num_programs(1) - 1
    t = SSD_CHUNK
    ntile = t // SUBLANE

    @pl.when(c == 0)
    def _():
        xpa_ref[...] = jnp.zeros_like(xpa_ref)
        xpb_ref[...] = jnp.zeros_like(xpb_ref)
        h_ref[...] = jnp.zeros_like(h_ref)
        s_ref[...] = jnp.zeros_like(s_ref)

    def tiles(x):
        return [x[i * SUBLANE:(i + 1) * SUBLANE, :] for i in range(ntile)]

    def conv(x, tail_ref, w_ref, b_ref):
        sub = lax.broadcasted_iota(jnp.int32, (SUBLANE, x.shape[1]), 0)
        xt = [tail_ref[...]] + tiles(x)
        acc = [b_ref[...] + w_ref[CONV_W - 1:CONV_W, :] * xt[i + 1] for i in range(ntile)]
        for s in range(1, CONV_W):
            wk = w_ref[CONV_W - 1 - s:CONV_W - s, :]
            rolled = [pltpu.roll(xi, s, 0) for xi in xt]
            for i in range(ntile):
                acc[i] = acc[i] + wk * jnp.where(sub < s, rolled[i], rolled[i + 1])
        tail_ref[...] = xt[ntile]
        return jnp.concatenate(acc, axis=0)

    ax = ax_ref[...]
    xc = conv(ax, xpa_ref, acw_ref, acb_ref)
    a, u = _lru_gates(xc, wr_ref, br_ref, wi_ref, bi_ref, lam_ref)
    sub = lax.broadcasted_iota(jnp.int32, (SUBLANE, D_A), 0)
    at, ut = tiles(a), tiles(u)
    step = 1
    while step < SUBLANE:
        m = sub >= step
        for i in range(ntile):
            ut[i] = jnp.where(m, at[i] * pltpu.roll(ut[i], step, 0) + ut[i], ut[i])
            at[i] = jnp.where(m, at[i] * pltpu.roll(at[i], step, 0), at[i])
        step *= 2
    carry = h_ref[0:1, :]
    hs = []
    for i in range(ntile):
        hs.append(ut[i] + at[i] * carry)
        carry = hs[i][SUBLANE - 1:SUBLANE, :]
    h = jnp.concatenate(hs, axis=0)
    h_ref[0:1, :] = carry
    mix_ref[:, 0:D_A] = (h * _silu(ag_ref[...])).astype(BF16)

    xb = xbc_ref[...]
    xbc = _silu(conv(xb, xpb_ref, bcw_ref, bcb_ref))
    bx = xbc[:, 0:D_B]
    bxb = bx.astype(BF16)
    dt = jax.nn.softplus(dt_ref[...] + dtb_ref[...])
    adt = dt * (-jnp.exp(alog_ref[...]))
    ti = lax.broadcasted_iota(jnp.int32, (t, t), 0)
    si = lax.broadcasted_iota(jnp.int32, (t, t), 1)
    causal = ti >= si
    acs = jnp.dot(causal.astype(F32), adt, preferred_element_type=F32,
                  precision=lax.Precision.HIGHEST)
    a_last = acs[t - 1:t, :]
    wq = jnp.exp(a_last - acs) * dt
    eacs = jnp.exp(acs)
    ealast = jnp.exp(a_last)
    acs_t = acs.T
    dt_t = dt.T
    lane = lax.broadcasted_iota(jnp.int32, (t, LANE), 1)
    rowi = lax.broadcasted_iota(jnp.int32, (LANE, N_B), 0)
    hpg = H_B // G_B
    ys = []
    cb = None
    for j in range(H_B // 2):
        g = (2 * j) // hpg
        bg = xbc[:, D_B + g * N_B:D_B + (g + 1) * N_B]
        cg = xbc[:, D_B + G_B * N_B + g * N_B:D_B + G_B * N_B + (g + 1) * N_B]
        if (2 * j) % hpg == 0:
            cb = _dot_tr(cg.astype(BF16), bg.astype(BF16))
        xpair = bxb[:, j * LANE:(j + 1) * LANE]
        sp = s_ref[j]
        spb = sp.astype(BF16)
        y_h, up_h = [], []
        for hh in range(2):
            hd = 2 * j + hh
            seg = jnp.broadcast_to(acs[:, hd:hd + 1], (t, t)) - jnp.broadcast_to(acs_t[hd:hd + 1, :], (t, t))
            lmat = jnp.exp(jnp.where(causal, seg, -1e30))
            mmat = (cb * lmat * jnp.broadcast_to(dt_t[hd:hd + 1, :], (t, t))).astype(BF16)
            ec = (jnp.broadcast_to(eacs[:, hd:hd + 1], (t, N_B)) * cg).astype(BF16)
            y_h.append(_dot(mmat, xpair) + _dot_tr(ec, spb))
            bw = (bg * jnp.broadcast_to(wq[:, hd:hd + 1], (t, N_B))).astype(BF16)
            up_h.append(_dot_tl(xpair, bw))
        ys.append(jnp.where(lane < HD_B, y_h[0], y_h[1]))
        dec = jnp.where(rowi < HD_B,
                        jnp.broadcast_to(ealast[:, 2 * j:2 * j + 1], (LANE, N_B)),
                        jnp.broadcast_to(ealast[:, 2 * j + 1:2 * j + 2], (LANE, N_B)))
        s_ref[j] = dec * sp + jnp.where(rowi < HD_B, up_h[0], up_h[1])
    y = jnp.concatenate(ys, axis=-1)
    mix_ref[:, D_A:D_A + D_B] = _group_norm_gate(y, bx, z_ref[...], dexp_ref, bnw_ref).astype(BF16)

    @pl.when(c == last)
    def _():
        ah_ref[0] = h[t - 1:t, :]
        ac_ref[0] = ax[t - (CONV_W - 1):t, :]
        bc_ref[0] = xb[t - (CONV_W - 1):t, :]
        for j in range(H_B // 2):
            sj = s_ref[j]
            bs_ref[0, 2 * j] = sj[0:HD_B, :]
            bs_ref[0, 2 * j + 1] = sj[HD_B:2 * HD_B, :]


def _ab_prompt(u, dtr, bsz, seq, w):
    t = SSD_CHUNK
    assert seq % t == 0
    nc = seq // t
    m = bsz * seq
    rows = lambda b, c: b * nc + c
    cvec = lambda b, c: (0, 0)
    c3 = lambda b, c: (0, 0, 0)
    in_specs = [
        pl.BlockSpec((t, D_A), lambda b, c: (rows(b, c), 0)),
        pl.BlockSpec((t, D_A), lambda b, c: (rows(b, c), 1)),
        pl.BlockSpec((t, D_B), lambda b, c: (rows(b, c), 2)),
        pl.BlockSpec((t, CONV_DIM_B), lambda b, c: (rows(b, c), 2)),
        pl.BlockSpec((t, LANE), lambda b, c: (rows(b, c), 0)),
        pl.BlockSpec((CONV_W, D_A), cvec), pl.BlockSpec((1, D_A), cvec),
        pl.BlockSpec((A_BLOCKS, A_BLK, A_BLK), c3), pl.BlockSpec((1, D_A), cvec),
        pl.BlockSpec((A_BLOCKS, A_BLK, A_BLK), c3), pl.BlockSpec((1, D_A), cvec),
        pl.BlockSpec((1, D_A), cvec),
        pl.BlockSpec((CONV_W, CONV_DIM_B), cvec), pl.BlockSpec((1, CONV_DIM_B), cvec),
        pl.BlockSpec((1, LANE), cvec), pl.BlockSpec((1, LANE), cvec),
        pl.BlockSpec((1, D_B), cvec), pl.BlockSpec((1, D_B), cvec),
    ]
    out_shape = [
        jax.ShapeDtypeStruct((m, D_A + D_B), BF16),
        jax.ShapeDtypeStruct((bsz, 1, D_A), F32),
        jax.ShapeDtypeStruct((bsz, CONV_W - 1, D_A), F32),
        jax.ShapeDtypeStruct((bsz, H_B, HD_B, N_B), F32),
        jax.ShapeDtypeStruct((bsz, CONV_W - 1, CONV_DIM_B), F32),
    ]
    out_specs = [
        pl.BlockSpec((t, D_A + D_B), lambda b, c: (rows(b, c), 0)),
        pl.BlockSpec((1, 1, D_A), lambda b, c: (b, 0, 0)),
        pl.BlockSpec((1, CONV_W - 1, D_A), lambda b, c: (b, 0, 0)),
        pl.BlockSpec((1, H_B, HD_B, N_B), lambda b, c: (b, 0, 0, 0)),
        pl.BlockSpec((1, CONV_W - 1, CONV_DIM_B), lambda b, c: (b, 0, 0)),
    ]
    mix, ah, ac, bs, bc = pl.pallas_call(
        _ab_prompt_kernel, grid=(bsz, nc), in_specs=in_specs, out_specs=out_specs, out_shape=out_shape,
        scratch_shapes=[pltpu.VMEM((SUBLANE, D_A), F32), pltpu.VMEM((SUBLANE, CONV_DIM_B), F32),
                        pltpu.VMEM((SUBLANE, D_A), F32), pltpu.VMEM((H_B // 2, 2 * HD_B, N_B), F32)],
        compiler_params=_params("parallel", "arbitrary"),
        name="ab_prompt",
    )(u, u, u, u, dtr, w["acw"], w["acb"], w["wr"], w["br"], w["wi"], w["bi"], w["lam"],
      w["bcw"], w["bcb"], w["dtb"], w["alog"], w["dexp"], w["bnw"])
    return mix, ah.reshape(bsz, D_A), ac, bs, bc


def _ab_sample_rows_kernel(ax_ref, ag_ref, xbc_ref, dt_ref, sah_ref, sac_ref, sbc_ref,
                           acw_ref, acb_ref, wr_ref, br_ref, wi_ref, bi_ref, lam_ref,
                           bcw_ref, bcb_ref, dtb_ref,
                           aout_ref, ah_ref, ac_ref, bc_ref, xact_ref, dts_ref):
    def conv1(x, buf_ref, w_ref, b_ref, nbuf_ref, width):
        y = b_ref[...] + w_ref[CONV_W - 1:CONV_W, :] * x
        for k in range(CONV_W - 1):
            y = y + w_ref[k:k + 1, :] * buf_ref[:, k * width:(k + 1) * width]
        for k in range(CONV_W - 2):
            nbuf_ref[:, k * width:(k + 1) * width] = buf_ref[:, (k + 1) * width:(k + 2) * width]
        nbuf_ref[:, (CONV_W - 2) * width:(CONV_W - 1) * width] = x
        return y

    xc = conv1(ax_ref[...], sac_ref, acw_ref, acb_ref, ac_ref, D_A)
    a, u = _lru_gates(xc, wr_ref, br_ref, wi_ref, bi_ref, lam_ref)
    h = a * sah_ref[...] + u
    ah_ref[...] = h
    aout_ref[...] = h * _silu(ag_ref[...])
    xact_ref[...] = _silu(conv1(xbc_ref[...], sbc_ref, bcw_ref, bcb_ref, bc_ref, CONV_DIM_B))
    dts_ref[...] = jax.nn.softplus(dt_ref[...] + dtb_ref[...])


def _pad_rows_t(x):
    pad = jnp.zeros((LANE - x.shape[0], x.shape[1]), F32)
    return jnp.concatenate([x, pad], axis=0).T


def _ab_sample_state_kernel(s_ref, xact_ref, dts_ref, z_ref, alog_ref, dexp_ref, bnw_ref,
                            so_ref, bout_ref, y_ref):
    bb = SAMPLE_BB
    xact = xact_ref[...]
    bx = xact[:, 0:D_B]
    dts = dts_ref[...]
    dec_t = _pad_rows_t(jnp.exp(dts * (-jnp.exp(alog_ref[...]))))
    dts_t = _pad_rows_t(dts)
    hpg = H_B // G_B
    for j in range(H_B // 2):
        g = (2 * j) // hpg
        xt = _pad_rows_t(bx[:, j * LANE:(j + 1) * LANE])
        dtp = jnp.concatenate([jnp.broadcast_to(dts_t[2 * j:2 * j + 1, :], (HD_B, LANE)),
                               jnp.broadcast_to(dts_t[2 * j + 1:2 * j + 2, :], (HD_B, LANE))], axis=0)
        xdt = xt * dtp
        for i in range(bb):
            brow = jnp.broadcast_to(xact[i:i + 1, D_B + g * N_B:D_B + (g + 1) * N_B], (2 * HD_B, N_B))
            crow = jnp.broadcast_to(
                xact[i:i + 1, D_B + G_B * N_B + g * N_B:D_B + G_B * N_B + (g + 1) * N_B], (2 * SUBLANE, N_B))
            upd = jnp.broadcast_to(xdt[:, i:i + 1], (2 * HD_B, N_B)) * brow
            news = []
            for hh in range(2):
                hd = 2 * j + hh
                dec = jnp.broadcast_to(dec_t[hd:hd + 1, i:i + 1], (HD_B, N_B))
                sn = dec * s_ref[i, hd] + upd[hh * HD_B:(hh + 1) * HD_B, :]
                so_ref[i, hd] = sn
                news.append(sn)
            spair = jnp.concatenate(news, axis=0).astype(BF16)
            yrow = _dot_tr(crow.astype(BF16), spair)
            y_ref[i:i + 1, j * LANE:(j + 1) * LANE] = yrow[0:1, :]
    bout_ref[...] = _group_norm_gate(y_ref[...], bx, z_ref[...], dexp_ref, bnw_ref)


def _ab_sample(u, dtr, s_ah, s_ac, s_bs, s_bc, w):
    bsz = u.shape[0]
    full = lambda shape: pl.BlockSpec(shape, lambda i: tuple(0 for _ in shape))
    cw = CONV_W - 1
    aout, ah, ac, bc, xact, dts = pl.pallas_call(
        _ab_sample_rows_kernel, grid=(1,),
        in_specs=[pl.BlockSpec((bsz, D_A), lambda i: (0, 0)), pl.BlockSpec((bsz, D_A), lambda i: (0, 1)),
                  pl.BlockSpec((bsz, CONV_DIM_B), lambda i: (0, 2)), full((bsz, LANE)),
                  full((bsz, D_A)), full((bsz, cw * D_A)), full((bsz, cw * CONV_DIM_B)),
                  full((CONV_W, D_A)), full((1, D_A)),
                  full((A_BLOCKS, A_BLK, A_BLK)), full((1, D_A)),
                  full((A_BLOCKS, A_BLK, A_BLK)), full((1, D_A)), full((1, D_A)),
                  full((CONV_W, CONV_DIM_B)), full((1, CONV_DIM_B)), full((1, LANE))],
        out_specs=[full((bsz, D_A)), full((bsz, D_A)), full((bsz, cw * D_A)), full((bsz, cw * CONV_DIM_B)),
                   full((bsz, CONV_DIM_B)), full((bsz, LANE))],
        out_shape=[jax.ShapeDtypeStruct((bsz, D_A), F32), jax.ShapeDtypeStruct((bsz, D_A), F32),
                   jax.ShapeDtypeStruct((bsz, cw * D_A), F32), jax.ShapeDtypeStruct((bsz, cw * CONV_DIM_B), F32),
                   jax.ShapeDtypeStruct((bsz, CONV_DIM_B), F32), jax.ShapeDtypeStruct((bsz, LANE), F32)],
        compiler_params=_params("arbitrary"),
        name="ab_sample_rows",
    )(u, u, u, dtr, s_ah, s_ac.reshape(bsz, cw * D_A), s_bc.reshape(bsz, cw * CONV_DIM_B),
      w["acw"], w["acb"], w["wr"], w["br"], w["wi"], w["bi"], w["lam"], w["bcw"], w["bcb"], w["dtb"])

    bb = SAMPLE_BB
    assert bsz % bb == 0
    cvec = lambda i: (0, 0)
    bs, bout = pl.pallas_call(
        _ab_sample_state_kernel, grid=(bsz // bb,),
        in_specs=[pl.BlockSpec((bb, H_B, HD_B, N_B), lambda i: (i, 0, 0, 0)),
                  pl.BlockSpec((bb, CONV_DIM_B), lambda i: (i, 0)),
                  pl.BlockSpec((bb, LANE), lambda i: (i, 0)),
                  pl.BlockSpec((bb, D_B), lambda i: (i, 2)),
                  pl.BlockSpec((1, LANE), cvec), pl.BlockSpec((1, D_B), cvec), pl.BlockSpec((1, D_B), cvec)],
        out_specs=[pl.BlockSpec((bb, H_B, HD_B, N_B), lambda i: (i, 0, 0, 0)),
                   pl.BlockSpec((bb, D_B), lambda i: (i, 0))],
        out_shape=[jax.ShapeDtypeStruct(s_bs.shape, F32), jax.ShapeDtypeStruct((bsz, D_B), F32)],
        scratch_shapes=[pltpu.VMEM((bb, D_B), F32)],
        compiler_params=_params("parallel"),
        name="ab_sample_state",
    )(s_bs, xact, dts, u, w["alog"], w["dexp"], w["bnw"])
    mix = jnp.concatenate([aout, bout], axis=-1)
    return mix, ah, ac.reshape(bsz, cw, D_A), bs, bc.reshape(bsz, cw, CONV_DIM_B)


def _hg_lower_bound(clb, layer):
    mx = jnp.max(clb, axis=0, keepdims=True)
    ex = jnp.exp(clb - mx)
    return jnp.sum(ex[1:layer + 1], axis=0, keepdims=True) / jnp.sum(ex, axis=0, keepdims=True)


def _hg_gates(fx, lb):
    f = 0.5 * (1.0 + lb) + (0.5 * (1.0 - lb)) * jnp.tanh(0.5 * fx)
    return f, 1.0 - f


def _hg_out(o, gate, cnw):
    return o * lax.rsqrt(jnp.mean(o * o, axis=-1, keepdims=True) + EPS) * cnw * _silu(gate)


def _hg_gamma():
    import numpy as np
    q = HG_CHUNK
    t = np.arange(q)[:, None]
    tau = np.arange(q)[None, :]
    mats = [(tau <= t)]
    for l in range(HG_LEVELS):
        w = 1 << l
        ref = (t // (2 * w)) * (2 * w) + w - 1
        upper = (t % (2 * w)) >= w
        mats.append(np.where(upper, (tau > ref) & (tau <= t), (tau > t) & (tau <= ref)))
    gam = np.concatenate(mats, axis=0).astype(np.float32)
    return jnp.asarray(np.concatenate([gam, gam, gam], axis=1), dtype=BF16)


def _hg_level_table():
    import numpy as np
    q = HG_CHUNK
    t = np.arange(q)[:, None]
    s = np.arange(q)[None, :]
    x = t ^ s
    lvl = np.floor(np.log2(np.maximum(x, 1))).astype(np.int32)
    return jnp.asarray(np.where(t > s, lvl, -1).astype(np.int32))


def _c_prompt_kernel(q_ref, f_ref, v_ref, g_ref, clb_ref, cnw_ref, gam_ref, lvl_ref,
                     og_ref, cs_ref, st_ref, *, layer):
    c = pl.program_id(1)
    last = pl.num_programs(1) - 1
    qc = HG_CHUNK

    @pl.when(c == 0)
    def _():
        st_ref[...] = jnp.zeros_like(st_ref)

    gam = gam_ref[...]
    ntile = qc // SUBLANE
    sub = lax.broadcasted_iota(jnp.int32, (SUBLANE, DK_C), 0)
    sub_levels = min(HG_LEVELS, 3)
    sub_upper = [(sub & (1 << l)) != 0 for l in range(sub_levels)]

    def tiles(x):
        return [x[i * SUBLANE:(i + 1) * SUBLANE, :] for i in range(ntile)]

    def gate_split(hd, rows):
        f, kk = _hg_gates(f_ref[hd, rows, :], _hg_lower_bound(clb_ref[hd], layer))
        g = jnp.log(f)
        g1 = g.astype(BF16)
        r1 = g - g1.astype(F32)
        g2 = r1.astype(BF16)
        g3 = (r1 - g2.astype(F32)).astype(BF16)
        return kk, jnp.concatenate([g1, g2, g3], axis=0)

    def scores(hd, rows, kk, sums):
        qh = q_ref[hd, rows, :] * (DK_C ** -0.5)
        bcum = sums[0:qc]
        st = st_ref[hd]
        o = _dot_tr((qh * jnp.exp(bcum)).astype(BF16), st.astype(BF16))
        qt, kt = tiles(qh), tiles(kk)
        prods = []
        for l in range(HG_LEVELS):
            if l < sub_levels:
                sel = [jnp.where(sub_upper[l], qt[i], kt[i]) for i in range(ntile)]
            else:
                sel = [qt[i] if (i >> (l - sub_levels)) & 1 else kt[i] for i in range(ntile)]
            x = (jnp.concatenate(sel, axis=0) * jnp.exp(sums[(l + 1) * qc:(l + 2) * qc])).astype(BF16)
            prods.append(tiles(_dot_tr(x, x)))
        return qh, st, o, prods

    def combine(hd, rows, kk, bcum, qh, st, o, prods):
        arows = []
        for i in range(ntile):
            lv = lvl_ref[i * SUBLANE:(i + 1) * SUBLANE, :]
            a = jnp.zeros((SUBLANE, qc), F32)
            for l in range(HG_LEVELS):
                if l < sub_levels or (i >> (l - sub_levels)) & 1:
                    a = jnp.where(lv == l, prods[l][i], a)
            arows.append(a)
        amat = jnp.concatenate(arows, axis=0)
        v = v_ref[hd, rows, :]
        vb = v.astype(BF16)
        o = o + _dot(amat.astype(BF16), vb) + jnp.sum(qh * kk, axis=-1, keepdims=True) * v
        blast = bcum[qc - 1:qc, :]
        kdec = (kk * jnp.exp(blast - bcum)).astype(BF16)
        st_ref[hd] = st * jnp.exp(blast) + _dot_tl(vb, kdec)
        return o

    nchunk = HG_BLOCK // qc

    def body(idx, carry):
        hg = idx // nchunk
        rows = pl.ds(pl.multiple_of((idx % nchunk) * qc, qc), qc)
        heads = [hg * HG_UNROLL + k for k in range(HG_UNROLL)]
        gs = [gate_split(hd, rows) for hd in heads]
        sums = _dot(gam, jnp.concatenate([s for _, s in gs], axis=1))
        sums = [sums[:, k * DK_C:(k + 1) * DK_C] for k in range(HG_UNROLL)]
        sc = [scores(hd, rows, gs[k][0], sums[k]) for k, hd in enumerate(heads)]
        outs = [combine(hd, rows, gs[k][0], sums[k][0:qc], *sc[k]) for k, hd in enumerate(heads)]
        for k, hd in enumerate(heads):
            og_ref[hd, rows, :] = _hg_out(outs[k], g_ref[hd, rows, :], cnw_ref[hd]).astype(BF16)
        return carry

    lax.fori_loop(0, (H_C // HG_UNROLL) * nchunk, body, 0)

    @pl.when(c == last)
    def _():
        for hd in range(H_C):
            cs_ref[0, hd] = st_ref[hd].T


def _c_prompt(u, bsz, seq, w, layer):
    tb = HG_BLOCK
    assert seq % tb == 0 and tb % HG_CHUNK == 0 and (1 << HG_LEVELS) == HG_CHUNK
    nc = seq // tb
    m = bsz * seq
    depth = w["clb"].shape[1]

    def part(k):
        return pl.BlockSpec((H_C, tb, LANE), lambda b, c: (k, b * nc + c, 0))

    c2 = lambda b, c: (0, 0)
    c3 = lambda b, c: (0, 0, 0)
    og, cs = pl.pallas_call(
        functools.partial(_c_prompt_kernel, layer=layer), grid=(bsz, nc),
        in_specs=[part(0), part(1), part(2), part(3),
                  pl.BlockSpec((H_C, depth, DK_C), c3), pl.BlockSpec((H_C, 1, DV_C), c3),
                  pl.BlockSpec(w["gam"].shape, c2), pl.BlockSpec(w["lvl"].shape, c2)],
        out_specs=[pl.BlockSpec((H_C, tb, LANE), lambda b, c: (0, b * nc + c, 0)),
                   pl.BlockSpec((1, H_C, DK_C, DV_C), lambda b, c: (b, 0, 0, 0))],
        out_shape=[jax.ShapeDtypeStruct((H_C, m, DV_C), BF16),
                   jax.ShapeDtypeStruct((bsz, H_C, DK_C, DV_C), F32)],
        scratch_shapes=[pltpu.VMEM((H_C, DV_C, DK_C), F32)],
        compiler_params=_params("parallel", "arbitrary"),
        name="c_prompt",
    )(u, u, u, u, w["clb"], w["cnw"], w["gam"], w["lvl"])
    return og, cs


def _c_sample_kernel(q_ref, f_ref, v_ref, g_ref, s_ref, clb_ref, cnw_ref, og_ref, so_ref, *, layer):
    bb = SAMPLE_BB
    for hd in range(H_C):
        lb = _hg_lower_bound(clb_ref[hd], layer)
        f, kk = _hg_gates(f_ref[hd], lb)
        f_t = _pad_rows_t(f)
        k_t = _pad_rows_t(kk)
        qs = q_ref[hd] * (DK_C ** -0.5)
        v = v_ref[hd]
        orows = []
        for i in range(bb):
            fcol = jnp.broadcast_to(f_t[:, i:i + 1], (DK_C, DV_C))
            kcol = jnp.broadcast_to(k_t[:, i:i + 1], (DK_C, DV_C))
            sn = fcol * s_ref[i, hd] + kcol * jnp.broadcast_to(v[i:i + 1, :], (DK_C, DV_C))
            so_ref[i, hd] = sn
            qrow = jnp.broadcast_to(qs[i:i + 1, :], (2 * SUBLANE, DK_C)).astype(BF16)
            orows.append(_dot(qrow, sn.astype(BF16))[0:1, :])
        o = jnp.concatenate(orows, axis=0)
        og_ref[hd] = _hg_out(o, g_ref[hd], cnw_ref[hd])


def _c_sample(u, s_c, w, layer):
    bsz = s_c.shape[0]
    bb = SAMPLE_BB
    assert bsz % bb == 0
    depth = w["clb"].shape[1]

    def part(k):
        return pl.BlockSpec((H_C, bb, LANE), lambda i: (k, i, 0))

    c3 = lambda i: (0, 0, 0)
    og, so = pl.pallas_call(
        functools.partial(_c_sample_kernel, layer=layer), grid=(bsz // bb,),
        in_specs=[part(0), part(1), part(2), part(3),
                  pl.BlockSpec((bb, H_C, DK_C, DV_C), lambda i: (i, 0, 0, 0)),
                  pl.BlockSpec((H_C, depth, DK_C), c3), pl.BlockSpec((H_C, 1, DV_C), c3)],
        out_specs=[pl.BlockSpec((H_C, bb, LANE), lambda i: (0, i, 0)),
                   pl.BlockSpec((bb, H_C, DK_C, DV_C), lambda i: (i, 0, 0, 0))],
        out_shape=[jax.ShapeDtypeStruct((H_C, bsz, DV_C), F32), jax.ShapeDtypeStruct(s_c.shape, F32)],
        compiler_params=_params("parallel"),
        name="c_sample",
    )(u, u, u, u, s_c, w["clb"], w["cnw"])
    return og, so


def _row(v, width=None):
    v = v.astype(F32).reshape(1, -1)
    if width is not None and v.shape[1] < width:
        v = jnp.pad(v, ((0, 0), (0, width - v.shape[1])))
    return v


def kernel(x_prompt, x_sample, p_prompt, p_sample, state_a_h, state_a_conv, state_b_ssm, state_b_conv, state_c,
           norm_w, norm_f, ab_w_in, a_conv_w, a_conv_b, a_w_r, a_b_r, a_w_i, a_b_i, a_lam, b_conv_w, b_conv_b,
           b_dt_bias, b_a_log, b_d, b_norm_w, ab_w_out, c_w_in, c_lb, c_norm_w, c_w_out, ple_proj, ple_gate):
    depth = norm_w.shape[0]
    bp, seq, _ = x_prompt.shape
    bs = x_sample.shape[0]
    hp = x_prompt.reshape(bp * seq, D_MODEL)
    hs = x_sample.reshape(bs, D_MODEL)
    gam, lvl = _hg_gamma(), _hg_level_table()
    clb = c_lb.astype(F32).reshape(depth, H_C, DK_C).transpose(1, 0, 2)

    ah_p, ac_p, bs_p, bc_p, c_p = [], [], [], [], []
    ah_s, ac_s, bs_s, bc_s, c_s = [], [], [], [], []
    for i in range(depth):
        j = i // 2
        final = i == depth - 1
        wg = ple_gate[i].astype(BF16)
        wp = ple_proj[i].astype(BF16)
        pp = p_prompt[i].reshape(bp * seq, D_PLE)
        ps = p_sample[i].reshape(bs, D_PLE)
        if i % 2 == 0:
            w_main = ab_w_in[j][:, :AB_MAIN].astype(BF16)
            w_dt = jnp.pad(ab_w_in[j][:, AB_MAIN:], ((0, 0), (0, LANE - H_B))).astype(BF16)
            wo = ab_w_out[j].astype(BF16)
            w = dict(acw=a_conv_w[j].astype(F32), acb=_row(a_conv_b[j]),
                     wr=a_w_r[j].astype(BF16), br=_row(a_b_r[j]), wi=a_w_i[j].astype(BF16), bi=_row(a_b_i[j]),
                     lam=_row(a_lam[j]), bcw=b_conv_w[j].astype(F32), bcb=_row(b_conv_b[j]),
                     dtb=_row(b_dt_bias[j], LANE), alog=_row(b_a_log[j], LANE),
                     dexp=_row(jnp.repeat(b_d[j], HD_B)), bnw=_row(b_norm_w[j]))
            u, dtr = _in_proj(hp, norm_w[i], w_main, w_dt)
            mix, s1, s2, s3, s4 = _ab_prompt(u, dtr, bp, seq, w)
            ah_p.append(s1); ac_p.append(s2); bs_p.append(s3); bc_p.append(s4)
            hp = _out_proj(mix, hp, pp, wo, wg, wp, norm_f, head_major=False, final=final)
            u, dtr = _in_proj(hs, norm_w[i], w_main, w_dt)
            mix, s1, s2, s3, s4 = _ab_sample(u, dtr, state_a_h[j], state_a_conv[j], state_b_ssm[j],
                                             state_b_conv[j], w)
            ah_s.append(s1); ac_s.append(s2); bs_s.append(s3); bc_s.append(s4)
            hs = _out_proj(mix, hs, ps, wo, wg, wp, norm_f, head_major=False, final=final)
        else:
            w_in = c_w_in[j].astype(BF16)
            wo = c_w_out[j].astype(BF16)
            w = dict(clb=clb, cnw=c_norm_w[j].astype(F32).reshape(H_C, 1, DV_C), gam=gam, lvl=lvl)
            u = _in_proj(hp, norm_w[i], w_in, head_major=True)
            og, s1 = _c_prompt(u, bp, seq, w, i)
            c_p.append(s1)
            hp = _out_proj(og, hp, pp, wo, wg, wp, norm_f, head_major=True, final=final)
            u = _in_proj(hs, norm_w[i], w_in, head_major=True)
            og, s1 = _c_sample(u, state_c[j], w, i)
            c_s.append(s1)
            hs = _out_proj(og, hs, ps, wo, wg, wp, norm_f, head_major=True, final=final)
    return (hp.reshape(bp, seq, D_MODEL), hs.reshape(bs, 1, D_MODEL),
            jnp.stack(ah_p), jnp.stack(ac_p), jnp.stack(bs_p), jnp.stack(bc_p), jnp.stack(c_p),
            jnp.stack(ah_s), jnp.stack(ac_s), jnp.stack(bs_s), jnp.stack(bc_s), jnp.stack(c_s))
```

```python
import functools

import jax
import jax.numpy as jnp
from jax import lax
from jax.experimental import pallas as pl
from jax.experimental.pallas import tpu as pltpu

F32 = jnp.float32
BF16 = jnp.bfloat16

D_MODEL = 1024
D_PLE = 256
EPS = 1e-6
CONV_W = 4
D_A = D_MODEL
A_BLOCKS = 8
A_BLK = D_A // A_BLOCKS
LRU_C = 8.0
D_B = D_MODEL
HD_B = 64
H_B = D_B // HD_B
N_B = 128
G_B = 2
CONV_DIM_B = D_B + 2 * G_B * N_B
D_C = 2 * D_MODEL
H_C = 16
DK_C = 128
DV_C = D_C // H_C
HK_C = H_C * DK_C
AB_MAIN = 2 * D_A + D_B + CONV_DIM_B
IN_C = 2 * HK_C + 2 * D_C

LANE = 128
SUBLANE = 8
BF16_ROWS = 16
LOG2_E = 1.4426950408889634
VMEM_LIMIT = 52 * 1024 * 1024

PROJ_MAX_TN = 2048
SSD_CHUNK = 128
HG_CHUNK = 64
HG_LEVELS = 6
HG_BLOCK = 256
HG_UNROLL = 8
SAMPLE_BB = 8

_DN_TR = (((1,), (1,)), ((), ()))
_DN_TL = (((0,), (0,)), ((), ()))


def _dot(a, b):
    return jnp.dot(a, b, preferred_element_type=F32)


def _dot_tr(a, b):
    return lax.dot_general(a, b, _DN_TR, preferred_element_type=F32)


def _dot_tl(a, b):
    return lax.dot_general(a, b, _DN_TL, preferred_element_type=F32)


def _silu(x):
    return x * jax.nn.sigmoid(x)


def _rmsnorm(x, w):
    return x * lax.rsqrt(jnp.mean(x * x, axis=-1, keepdims=True) + EPS) * w


def _params(*sem):
    return pltpu.CompilerParams(dimension_semantics=sem, vmem_limit_bytes=VMEM_LIMIT)


def _in_proj_kernel(x_ref, nw_ref, w_ref, *rest, has_extra, head_major):
    if has_extra:
        wx_ref, o_ref, ox_ref, xn_ref = rest
    else:
        o_ref, xn_ref = rest

    @pl.when(pl.program_id(1) == 0)
    def _():
        xn_ref[...] = _rmsnorm(x_ref[...], nw_ref[...]).astype(BF16)
        if has_extra:
            ox_ref[...] = _dot(xn_ref[...], wx_ref[...])

    acc = _dot(xn_ref[...], w_ref[...])
    if head_major:
        for k in range(acc.shape[1] // LANE):
            o_ref[k] = acc[:, k * LANE:(k + 1) * LANE]
    else:
        o_ref[...] = acc


def _in_proj(x, nw, w, w_extra=None, *, head_major=False):
    m, k = x.shape
    n = w.shape[1]
    tm = min(m, 1024)
    tn = next(c for c in range(PROJ_MAX_TN, 0, -LANE) if n % c == 0)
    assert m % tm == 0 and n % tn == 0 and tn % LANE == 0
    grid = (m // tm, n // tn)
    in_specs = [pl.BlockSpec((tm, k), lambda i, j: (i, 0)),
                pl.BlockSpec((1, k), lambda i, j: (0, 0)),
                pl.BlockSpec((k, tn), lambda i, j: (0, j))]
    args = [x, nw.reshape(1, k), w]
    if head_major:
        out_shape = [jax.ShapeDtypeStruct((n // LANE, m, LANE), F32)]
        out_specs = [pl.BlockSpec((tn // LANE, tm, LANE), lambda i, j: (j, i, 0))]
    else:
        out_shape = [jax.ShapeDtypeStruct((m, n), F32)]
        out_specs = [pl.BlockSpec((tm, tn), lambda i, j: (i, j))]
    if w_extra is not None:
        in_specs.append(pl.BlockSpec((k, LANE), lambda i, j: (0, 0)))
        args.append(w_extra)
        out_shape.append(jax.ShapeDtypeStruct((m, LANE), F32))
        out_specs.append(pl.BlockSpec((tm, LANE), lambda i, j: (i, 0)))
    outs = pl.pallas_call(
        functools.partial(_in_proj_kernel, has_extra=w_extra is not None, head_major=head_major),
        grid=grid, in_specs=in_specs, out_specs=out_specs, out_shape=out_shape,
        scratch_shapes=[pltpu.VMEM((tm, k), BF16)],
        compiler_params=_params("parallel", "arbitrary"),
        name="in_proj",
    )(*args)
    return outs if w_extra is not None else outs[0]


def _out_proj_kernel(mix_ref, h_ref, p_ref, wo_ref, wg_ref, wp_ref, nf_ref, o_ref, *, head_major, final):
    if head_major:
        mix = jnp.concatenate([mix_ref[k] for k in range(mix_ref.shape[0])], axis=-1)
    else:
        mix = mix_ref[...]
    h = h_ref[...] + _dot(mix.astype(BF16), wo_ref[...])
    gate = jax.nn.sigmoid(_dot(h.astype(BF16), wg_ref[...]))
    pe = _dot(p_ref[...].astype(BF16), wp_ref[...])
    h = h + gate * pe
    if final:
        h = _rmsnorm(h, nf_ref[...])
    o_ref[...] = h


def _out_proj(mix, h, p, layer, wo, wg, wp, nf, *, head_major, final):
    m, d = h.shape
    tm = min(m, 512)
    assert m % tm == 0
    if head_major:
        mix_spec = pl.BlockSpec((mix.shape[0], tm, LANE), lambda i: (0, i, 0))
    else:
        mix_spec = pl.BlockSpec((tm, mix.shape[1]), lambda i: (i, 0))
    const = lambda i: (0, 0)
    return pl.pallas_call(
        functools.partial(_out_proj_kernel, head_major=head_major, final=final),
        grid=(m // tm,),
        in_specs=[mix_spec,
                  pl.BlockSpec((tm, d), lambda i: (i, 0)),
                  pl.BlockSpec((None, tm, p.shape[2]), lambda i: (layer, i, 0)),
                  pl.BlockSpec(wo.shape, const), pl.BlockSpec(wg.shape, const),
                  pl.BlockSpec(wp.shape, const), pl.BlockSpec((1, d), const)],
        out_specs=pl.BlockSpec((tm, d), lambda i: (i, 0)),
        out_shape=jax.ShapeDtypeStruct((m, d), F32),
        compiler_params=_params("parallel"),
        name="out_proj",
    )(mix, h, p, wo, wg, wp, nf.reshape(1, d))


def _lru_gates(xc, wr_ref, br_ref, wi_ref, bi_ref, lam_ref):
    xcb = xc.astype(BF16)
    r_parts, i_parts = [], []
    for k in range(A_BLOCKS):
        xk = xcb[:, k * A_BLK:(k + 1) * A_BLK]
        r_parts.append(_dot(xk, wr_ref[k]))
        i_parts.append(_dot(xk, wi_ref[k]))
    r = jax.nn.sigmoid(jnp.concatenate(r_parts, axis=-1) + br_ref[...])
    gi = jax.nn.sigmoid(jnp.concatenate(i_parts, axis=-1) + bi_ref[...])
    log_a = (-LRU_C) * r * jax.nn.softplus(-lam_ref[...])
    a = jnp.exp(log_a)
    u = jnp.sqrt(jnp.tanh(-log_a) * (a * a + 1.0)) * (gi * xc)
    return a, u


def _group_norm_gate(y, bx, z, dexp_ref, bnw_ref):
    y = (y + dexp_ref[...] * bx) * _silu(z)
    gw = D_B // G_B
    parts = []
    for g in range(G_B):
        yg = y[:, g * gw:(g + 1) * gw]
        parts.append(yg * lax.rsqrt(jnp.mean(yg * yg, axis=-1, keepdims=True) + EPS))
    return jnp.concatenate(parts, axis=-1) * bnw_ref[...]


def _ab_prompt_kernel(ax_ref, ag_ref, z_ref, xbc_ref, dt_ref,
                      acw_ref, acb_ref, wr_ref, br_ref, wi_ref, bi_ref, lam_ref,
                      bcw_ref, bcb_ref, dtb_ref, alog_ref, dexp_ref, bnw_ref,
                      mix_ref, ah_ref, ac_ref, bs_ref, bc_ref,
                      xpa_ref, xpb_ref, h_ref, s_ref):
    c = pl.program_id(1)
    last = pl.num_programs(1) - 1
    t = SSD_CHUNK
    ntile = t // SUBLANE

    @pl.when(c == 0)
    def _():
        xpa_ref[...] = jnp.zeros_like(xpa_ref)
        xpb_ref[...] = jnp.zeros_like(xpb_ref)
        h_ref[...] = jnp.zeros_like(h_ref)
        s_ref[...] = jnp.zeros_like(s_ref)

    def tiles(x):
        return [x[i * SUBLANE:(i + 1) * SUBLANE, :] for i in range(ntile)]

    def conv(x, tail_ref, w_ref, b_ref):
        sub = lax.broadcasted_iota(jnp.int32, (SUBLANE, x.shape[1]), 0)
        xt = [tail_ref[...]] + tiles(x)
        acc = [b_ref[...] + w_ref[CONV_W - 1:CONV_W, :] * xt[i + 1] for i in range(ntile)]
        for s in range(1, CONV_W):
            wk = w_ref[CONV_W - 1 - s:CONV_W - s, :]
            rolled = [pltpu.roll(xi, s, 0) for xi in xt]
            for i in range(ntile):
                acc[i] = acc[i] + wk * jnp.where(sub < s, rolled[i], rolled[i + 1])
        tail_ref[...] = xt[ntile]
        return jnp.concatenate(acc, axis=0)

    ax = ax_ref[...]
    xc = conv(ax, xpa_ref, acw_ref, acb_ref)
    a, u = _lru_gates(xc, wr_ref, br_ref, wi_ref, bi_ref, lam_ref)
    sub = lax.broadcasted_iota(jnp.int32, (SUBLANE, D_A), 0)
    at, ut = tiles(a), tiles(u)
    step = 1
    while step < SUBLANE:
        m = sub >= step
        for i in range(ntile):
            ut[i] = jnp.where(m, at[i] * pltpu.roll(ut[i], step, 0) + ut[i], ut[i])
            at[i] = jnp.where(m, at[i] * pltpu.roll(at[i], step, 0), at[i])
        step *= 2
    carry = h_ref[0:1, :]
    hs = []
    for i in range(ntile):
        hs.append(ut[i] + at[i] * carry)
        carry = hs[i][SUBLANE - 1:SUBLANE, :]
    h = jnp.concatenate(hs, axis=0)
    h_ref[0:1, :] = carry
    mix_ref[:, 0:D_A] = (h * _silu(ag_ref[...])).astype(BF16)

    xb = xbc_ref[...]
    xbc = _silu(conv(xb, xpb_ref, bcw_ref, bcb_ref))
    bx = xbc[:, 0:D_B]
    bxb = bx.astype(BF16)
    dt = jax.nn.softplus(dt_ref[...] + dtb_ref[...])
    adt = dt * (-jnp.exp(alog_ref[...]))
    ti = lax.broadcasted_iota(jnp.int32, (t, t), 0)
    si = lax.broadcasted_iota(jnp.int32, (t, t), 1)
    causal = ti >= si
    acs = jnp.dot(causal.astype(F32), adt, preferred_element_type=F32,
                  precision=lax.Precision.HIGHEST)
    a_last = acs[t - 1:t, :]
    wq = jnp.exp(a_last - acs) * dt
    eacs = jnp.exp(acs)
    ealast = jnp.exp(a_last)
    acs_t = acs.T
    dt_t = dt.T
    lane = lax.broadcasted_iota(jnp.int32, (t, LANE), 1)
    rowi = lax.broadcasted_iota(jnp.int32, (LANE, N_B), 0)
    hpg = H_B // G_B
    ys = []
    cb = None
    for j in range(H_B // 2):
        g = (2 * j) // hpg
        bg = xbc[:, D_B + g * N_B:D_B + (g + 1) * N_B]
        cg = xbc[:, D_B + G_B * N_B + g * N_B:D_B + G_B * N_B + (g + 1) * N_B]
        if (2 * j) % hpg == 0:
            cb = _dot_tr(cg.astype(BF16), bg.astype(BF16))
        xpair = bxb[:, j * LANE:(j + 1) * LANE]
        sp = s_ref[j]
        spb = sp.astype(BF16)
        y_h, up_h = [], []
        for hh in range(2):
            hd = 2 * j + hh
            seg = jnp.broadcast_to(acs[:, hd:hd + 1], (t, t)) - jnp.broadcast_to(acs_t[hd:hd + 1, :], (t, t))
            lmat = jnp.exp(jnp.where(causal, seg, -1e30))
            mmat = (cb * lmat * jnp.broadcast_to(dt_t[hd:hd + 1, :], (t, t))).astype(BF16)
            ec = (jnp.broadcast_to(eacs[:, hd:hd + 1], (t, N_B)) * cg).astype(BF16)
            y_h.append(_dot(mmat, xpair) + _dot_tr(ec, spb))
            bw = (bg * jnp.broadcast_to(wq[:, hd:hd + 1], (t, N_B))).astype(BF16)
            up_h.append(_dot_tl(xpair, bw))
        ys.append(jnp.where(lane < HD_B, y_h[0], y_h[1]))
        dec = jnp.where(rowi < HD_B,
                        jnp.broadcast_to(ealast[:, 2 * j:2 * j + 1], (LANE, N_B)),
                        jnp.broadcast_to(ealast[:, 2 * j + 1:2 * j + 2], (LANE, N_B)))
        s_ref[j] = dec * sp + jnp.where(rowi < HD_B, up_h[0], up_h[1])
    y = jnp.concatenate(ys, axis=-1)
    mix_ref[:, D_A:D_A + D_B] = _group_norm_gate(y, bx, z_ref[...], dexp_ref, bnw_ref).astype(BF16)

    @pl.when(c == last)
    def _():
        ah_ref[0] = h[t - 1:t, :]
        ac_ref[0] = ax[t - (CONV_W - 1):t, :]
        bc_ref[0] = xb[t - (CONV_W - 1):t, :]
        for j in range(H_B // 2):
            sj = s_ref[j]
            bs_ref[0, 2 * j] = sj[0:HD_B, :]
            bs_ref[0, 2 * j + 1] = sj[HD_B:2 * HD_B, :]


def _ab_prompt(u, dtr, bsz, seq, w):
    t = SSD_CHUNK
    assert seq % t == 0
    nc = seq // t
    m = bsz * seq
    rows = lambda b, c: b * nc + c
    cvec = lambda b, c: (0, 0)
    c3 = lambda b, c: (0, 0, 0)
    in_specs = [
        pl.BlockSpec((t, D_A), lambda b, c: (rows(b, c), 0)),
        pl.BlockSpec((t, D_A), lambda b, c: (rows(b, c), 1)),
        pl.BlockSpec((t, D_B), lambda b, c: (rows(b, c), 2)),
        pl.BlockSpec((t, CONV_DIM_B), lambda b, c: (rows(b, c), 2)),
        pl.BlockSpec((t, LANE), lambda b, c: (rows(b, c), 0)),
        pl.BlockSpec((CONV_W, D_A), cvec), pl.BlockSpec((1, D_A), cvec),
        pl.BlockSpec((A_BLOCKS, A_BLK, A_BLK), c3), pl.BlockSpec((1, D_A), cvec),
        pl.BlockSpec((A_BLOCKS, A_BLK, A_BLK), c3), pl.BlockSpec((1, D_A), cvec),
        pl.BlockSpec((1, D_A), cvec),
        pl.BlockSpec((CONV_W, CONV_DIM_B), cvec), pl.BlockSpec((1, CONV_DIM_B), cvec),
        pl.BlockSpec((1, LANE), cvec), pl.BlockSpec((1, LANE), cvec),
        pl.BlockSpec((1, D_B), cvec), pl.BlockSpec((1, D_B), cvec),
    ]
    out_shape = [
        jax.ShapeDtypeStruct((m, D_A + D_B), BF16),
        jax.ShapeDtypeStruct((bsz, 1, D_A), F32),
        jax.ShapeDtypeStruct((bsz, CONV_W - 1, D_A), F32),
        jax.ShapeDtypeStruct((bsz, H_B, HD_B, N_B), F32),
        jax.ShapeDtypeStruct((bsz, CONV_W - 1, CONV_DIM_B), F32),
    ]
    out_specs = [
        pl.BlockSpec((t, D_A + D_B), lambda b, c: (rows(b, c), 0)),
        pl.BlockSpec((1, 1, D_A), lambda b, c: (b, 0, 0)),
        pl.BlockSpec((1, CONV_W - 1, D_A), lambda b, c: (b, 0, 0)),
        pl.BlockSpec((1, H_B, HD_B, N_B), lambda b, c: (b, 0, 0, 0)),
        pl.BlockSpec((1, CONV_W - 1, CONV_DIM_B), lambda b, c: (b, 0, 0)),
    ]
    mix, ah, ac, bs, bc = pl.pallas_call(
        _ab_prompt_kernel, grid=(bsz, nc), in_specs=in_specs, out_specs=out_specs, out_shape=out_shape,
        scratch_shapes=[pltpu.VMEM((SUBLANE, D_A), F32), pltpu.VMEM((SUBLANE, CONV_DIM_B), F32),
                        pltpu.VMEM((SUBLANE, D_A), F32), pltpu.VMEM((H_B // 2, 2 * HD_B, N_B), F32)],
        compiler_params=_params("parallel", "arbitrary"),
        name="ab_prompt",
    )(u, u, u, u, dtr, w["acw"], w["acb"], w["wr"], w["br"], w["wi"], w["bi"], w["lam"],
      w["bcw"], w["bcb"], w["dtb"], w["alog"], w["dexp"], w["bnw"])
    return mix, ah.reshape(bsz, D_A), ac, bs, bc


def _ab_sample_rows_kernel(ax_ref, ag_ref, xbc_ref, dt_ref, sah_ref, sac_ref, sbc_ref,
                           acw_ref, acb_ref, wr_ref, br_ref, wi_ref, bi_ref, lam_ref,
                           bcw_ref, bcb_ref, dtb_ref,
                           aout_ref, ah_ref, ac_ref, bc_ref, xact_ref, dts_ref):
    def conv1(x, buf_ref, w_ref, b_ref, nbuf_ref, width):
        y = b_ref[...] + w_ref[CONV_W - 1:CONV_W, :] * x
        for k in range(CONV_W - 1):
            y = y + w_ref[k:k + 1, :] * buf_ref[:, k * width:(k + 1) * width]
        for k in range(CONV_W - 2):
            nbuf_ref[:, k * width:(k + 1) * width] = buf_ref[:, (k + 1) * width:(k + 2) * width]
        nbuf_ref[:, (CONV_W - 2) * width:(CONV_W - 1) * width] = x
        return y

    xc = conv1(ax_ref[...], sac_ref, acw_ref, acb_ref, ac_ref, D_A)
    a, u = _lru_gates(xc, wr_ref, br_ref, wi_ref, bi_ref, lam_ref)
    h = a * sah_ref[...] + u
    ah_ref[...] = h
    aout_ref[...] = h * _silu(ag_ref[...])
    xact_ref[...] = _silu(conv1(xbc_ref[...], sbc_ref, bcw_ref, bcb_ref, bc_ref, CONV_DIM_B))
    dts_ref[...] = jax.nn.softplus(dt_ref[...] + dtb_ref[...])


def _pad_rows_t(x):
    pad = jnp.zeros((LANE - x.shape[0], x.shape[1]), F32)
    return jnp.concatenate([x, pad], axis=0).T


def _ab_sample_state_kernel(s_ref, xact_ref, dts_ref, z_ref, alog_ref, dexp_ref, bnw_ref,
                            so_ref, bout_ref, y_ref):
    bb = SAMPLE_BB
    xact = xact_ref[...]
    bx = xact[:, 0:D_B]
    dts = dts_ref[...]
    dec_t = _pad_rows_t(jnp.exp(dts * (-jnp.exp(alog_ref[...]))))
    dts_t = _pad_rows_t(dts)
    hpg = H_B // G_B
    for j in range(H_B // 2):
        g = (2 * j) // hpg
        xt = _pad_rows_t(bx[:, j * LANE:(j + 1) * LANE])
        dtp = jnp.concatenate([jnp.broadcast_to(dts_t[2 * j:2 * j + 1, :], (HD_B, LANE)),
                               jnp.broadcast_to(dts_t[2 * j + 1:2 * j + 2, :], (HD_B, LANE))], axis=0)
        xdt = xt * dtp
        for i in range(bb):
            brow = jnp.broadcast_to(xact[i:i + 1, D_B + g * N_B:D_B + (g + 1) * N_B], (2 * HD_B, N_B))
            crow = jnp.broadcast_to(
                xact[i:i + 1, D_B + G_B * N_B + g * N_B:D_B + G_B * N_B + (g + 1) * N_B], (2 * SUBLANE, N_B))
            upd = jnp.broadcast_to(xdt[:, i:i + 1], (2 * HD_B, N_B)) * brow
            news = []
            for hh in range(2):
                hd = 2 * j + hh
                dec = jnp.broadcast_to(dec_t[hd:hd + 1, i:i + 1], (HD_B, N_B))
                sn = dec * s_ref[i, hd] + upd[hh * HD_B:(hh + 1) * HD_B, :]
                so_ref[i, hd] = sn
                news.append(sn)
            spair = jnp.concatenate(news, axis=0).astype(BF16)
            yrow = _dot_tr(crow.astype(BF16), spair)
            y_ref[i:i + 1, j * LANE:(j + 1) * LANE] = yrow[0:1, :]
    bout_ref[...] = _group_norm_gate(y_ref[...], bx, z_ref[...], dexp_ref, bnw_ref)


def _ab_sample(u, dtr, s_ah, s_ac, s_bs, s_bc, w):
    bsz = u.shape[0]
    full = lambda shape: pl.BlockSpec(shape, lambda i: tuple(0 for _ in shape))
    cw = CONV_W - 1
    aout, ah, ac, bc, xact, dts = pl.pallas_call(
        _ab_sample_rows_kernel, grid=(1,),
        in_specs=[pl.BlockSpec((bsz, D_A), lambda i: (0, 0)), pl.BlockSpec((bsz, D_A), lambda i: (0, 1)),
                  pl.BlockSpec((bsz, CONV_DIM_B), lambda i: (0, 2)), full((bsz, LANE)),
                  full((bsz, D_A)), full((bsz, cw * D_A)), full((bsz, cw * CONV_DIM_B)),
                  full((CONV_W, D_A)), full((1, D_A)),
                  full((A_BLOCKS, A_BLK, A_BLK)), full((1, D_A)),
                  full((A_BLOCKS, A_BLK, A_BLK)), full((1, D_A)), full((1, D_A)),
                  full((CONV_W, CONV_DIM_B)), full((1, CONV_DIM_B)), full((1, LANE))],
        out_specs=[full((bsz, D_A)), full((bsz, D_A)), full((bsz, cw * D_A)), full((bsz, cw * CONV_DIM_B)),
                   full((bsz, CONV_DIM_B)), full((bsz, LANE))],
        out_shape=[jax.ShapeDtypeStruct((bsz, D_A), F32), jax.ShapeDtypeStruct((bsz, D_A), F32),
                   jax.ShapeDtypeStruct((bsz, cw * D_A), F32), jax.ShapeDtypeStruct((bsz, cw * CONV_DIM_B), F32),
                   jax.ShapeDtypeStruct((bsz, CONV_DIM_B), F32), jax.ShapeDtypeStruct((bsz, LANE), F32)],
        compiler_params=_params("arbitrary"),
        name="ab_sample_rows",
    )(u, u, u, dtr, s_ah, s_ac.reshape(bsz, cw * D_A), s_bc.reshape(bsz, cw * CONV_DIM_B),
      w["acw"], w["acb"], w["wr"], w["br"], w["wi"], w["bi"], w["lam"], w["bcw"], w["bcb"], w["dtb"])

    bb = SAMPLE_BB
    assert bsz % bb == 0
    cvec = lambda i: (0, 0)
    bs, bout = pl.pallas_call(
        _ab_sample_state_kernel, grid=(bsz // bb,),
        in_specs=[pl.BlockSpec((bb, H_B, HD_B, N_B), lambda i: (i, 0, 0, 0)),
                  pl.BlockSpec((bb, CONV_DIM_B), lambda i: (i, 0)),
                  pl.BlockSpec((bb, LANE), lambda i: (i, 0)),
                  pl.BlockSpec((bb, D_B), lambda i: (i, 2)),
                  pl.BlockSpec((1, LANE), cvec), pl.BlockSpec((1, D_B), cvec), pl.BlockSpec((1, D_B), cvec)],
        out_specs=[pl.BlockSpec((bb, H_B, HD_B, N_B), lambda i: (i, 0, 0, 0)),
                   pl.BlockSpec((bb, D_B), lambda i: (i, 0))],
        out_shape=[jax.ShapeDtypeStruct(s_bs.shape, F32), jax.ShapeDtypeStruct((bsz, D_B), F32)],
        scratch_shapes=[pltpu.VMEM((bb, D_B), F32)],
        compiler_params=_params("parallel"),
        name="ab_sample_state",
    )(s_bs, xact, dts, u, w["alog"], w["dexp"], w["bnw"])
    mix = jnp.concatenate([aout, bout], axis=-1)
    return mix, ah, ac.reshape(bsz, cw, D_A), bs, bc.reshape(bsz, cw, CONV_DIM_B)


def _hg_lower_bound(clb, layer):
    mx = jnp.max(clb, axis=0, keepdims=True)
    ex = jnp.exp(clb - mx)
    return jnp.sum(ex[1:layer + 1], axis=0, keepdims=True) / jnp.sum(ex, axis=0, keepdims=True)


def _hg_gates(fx_half, lb):
    f = 0.5 * (1.0 + lb) + (0.5 * (1.0 - lb)) * jnp.tanh(fx_half)
    return f, 1.0 - f


def _hg_out(o, gate_half, cnw):
    return o * lax.rsqrt(jnp.mean(o * o, axis=-1, keepdims=True) + EPS) * cnw * (gate_half + gate_half * jnp.tanh(gate_half))


def _hg_gamma():
    import numpy as np
    q = HG_CHUNK
    t = np.arange(q)[:, None]
    tau = np.arange(q)[None, :]
    mats = [(tau <= t)]
    for l in range(HG_LEVELS):
        w = 1 << l
        ref = (t // (2 * w)) * (2 * w) + w - 1
        upper = (t % (2 * w)) >= w
        mats.append(np.where(upper, (tau > ref) & (tau <= t), (tau > t) & (tau <= ref)))
    gam = np.concatenate(mats, axis=0).astype(np.float32)
    return jnp.asarray(np.concatenate([gam, gam], axis=1), dtype=BF16)


def _hg_level_table():
    import numpy as np
    q = HG_CHUNK
    t = np.arange(q)[:, None]
    s = np.arange(q)[None, :]
    x = t ^ s
    lvl = np.floor(np.log2(np.maximum(x, 1))).astype(np.int32)
    return jnp.asarray(np.where(t > s, lvl, -1).astype(np.int32))


def _c_prompt_kernel(q_ref, f_ref, v_ref, g_ref, clb_ref, cnw_ref, gam_ref, lvl_ref,
                     og_ref, cs_ref, st_ref, *, layer):
    c = pl.program_id(1)
    last = pl.num_programs(1) - 1
    qc = HG_CHUNK

    @pl.when(c == 0)
    def _():
        st_ref[...] = jnp.zeros_like(st_ref)

    gam = gam_ref[...]
    ntile = qc // SUBLANE
    sub = lax.broadcasted_iota(jnp.int32, (SUBLANE, DK_C), 0)
    sub_levels = min(HG_LEVELS, 3)
    sub_upper = [(sub & (1 << l)) != 0 for l in range(sub_levels)]

    def tiles(x):
        return [x[i * SUBLANE:(i + 1) * SUBLANE, :] for i in range(ntile)]

    def gate_split(hd, rows):
        f, kk = _hg_gates(f_ref[hd, rows, :], _hg_lower_bound(clb_ref[hd], layer))
        g = jnp.log(f) * LOG2_E
        g1 = g.astype(BF16)
        g2 = (g - g1.astype(F32)).astype(BF16)
        return kk, jnp.concatenate([g1, g2], axis=0)

    def scores(hd, rows, kk, sums):
        qh = q_ref[hd, rows, :] * (DK_C ** -0.5)
        bcum = sums[0:qc]
        st = st_ref[hd]
        o = _dot_tr((qh * jnp.exp2(bcum)).astype(BF16), st.astype(BF16))
        qt, kt = tiles(qh), tiles(kk)
        prods = []
        for l in range(HG_LEVELS):
            if l < sub_levels:
                sel = [jnp.where(sub_upper[l], qt[i], kt[i]) for i in range(ntile)]
            else:
                sel = [qt[i] if (i >> (l - sub_levels)) & 1 else kt[i] for i in range(ntile)]
            x = (jnp.concatenate(sel, axis=0) * jnp.exp2(sums[(l + 1) * qc:(l + 2) * qc])).astype(BF16)
            half = (1 << l) // BF16_ROWS
            if half == 0:
                p = tiles(_dot_tr(x, x))
                prods.append({i: p[i] for i in range(ntile)})
            else:
                ups = [r for r in range(qc // BF16_ROWS) if (r // half) & 1]
                pu = _dot_tr(jnp.concatenate([x[r * BF16_ROWS:(r + 1) * BF16_ROWS, :] for r in ups], axis=0), x)
                tpr = BF16_ROWS // SUBLANE
                prods.append({r * tpr + k: pu[(n * tpr + k) * SUBLANE:(n * tpr + k + 1) * SUBLANE, :]
                              for n, r in enumerate(ups) for k in range(tpr)})
        return qh, st, o, prods

    def combine(hd, rows, kk, bcum, qh, st, o, prods):
        arows = []
        for i in range(ntile):
            lv = lvl_ref[i * SUBLANE:(i + 1) * SUBLANE, :]
            a = jnp.zeros((SUBLANE, qc), F32)
            for l in range(HG_LEVELS):
                if l < sub_levels or (i >> (l - sub_levels)) & 1:
                    a = jnp.where(lv == l, prods[l][i], a)
            arows.append(a)
        amat = jnp.concatenate(arows, axis=0)
        v = v_ref[hd, rows, :]
        vb = v.astype(BF16)
        o = o + _dot(amat.astype(BF16), vb) + jnp.sum(qh * kk, axis=-1, keepdims=True) * v
        blast = bcum[qc - 1:qc, :]
        kdec = (kk * jnp.exp2(blast - bcum)).astype(BF16)
        st_ref[hd] = st * jnp.exp2(blast) + _dot_tl(vb, kdec)
        return o

    nchunk = HG_BLOCK // qc

    def body(idx, carry):
        hg = idx // nchunk
        rows = pl.ds(pl.multiple_of((idx % nchunk) * qc, qc), qc)
        heads = [hg * HG_UNROLL + k for k in range(HG_UNROLL)]
        gs = [gate_split(hd, rows) for hd in heads]
        sums = _dot(gam, jnp.concatenate([s for _, s in gs], axis=1))
        sums = [sums[:, k * DK_C:(k + 1) * DK_C] for k in range(HG_UNROLL)]
        sc = [scores(hd, rows, gs[k][0], sums[k]) for k, hd in enumerate(heads)]
        outs = [combine(hd, rows, gs[k][0], sums[k][0:qc], *sc[k]) for k, hd in enumerate(heads)]
        for k, hd in enumerate(heads):
            og_ref[hd, rows, :] = _hg_out(outs[k], g_ref[hd, rows, :], cnw_ref[hd]).astype(BF16)
        return carry

    lax.fori_loop(0, (H_C // HG_UNROLL) * nchunk, body, 0)

    @pl.when(c == last)
    def _():
        for hd in range(H_C):
            cs_ref[0, hd] = st_ref[hd].T


def _c_prompt(u, bsz, seq, w, layer):
    tb = HG_BLOCK
    assert seq % tb == 0 and tb % HG_CHUNK == 0 and (1 << HG_LEVELS) == HG_CHUNK
    nc = seq // tb
    m = bsz * seq
    depth = w["clb"].shape[1]

    def part(k):
        return pl.BlockSpec((H_C, tb, LANE), lambda b, c: (k, b * nc + c, 0))

    c2 = lambda b, c: (0, 0)
    c3 = lambda b, c: (0, 0, 0)
    og, cs = pl.pallas_call(
        functools.partial(_c_prompt_kernel, layer=layer), grid=(bsz, nc),
        in_specs=[part(0), part(1), part(2), part(3),
                  pl.BlockSpec((H_C, depth, DK_C), c3), pl.BlockSpec((H_C, 1, DV_C), c3),
                  pl.BlockSpec(w["gam"].shape, c2), pl.BlockSpec(w["lvl"].shape, c2)],
        out_specs=[pl.BlockSpec((H_C, tb, LANE), lambda b, c: (0, b * nc + c, 0)),
                   pl.BlockSpec((1, H_C, DK_C, DV_C), lambda b, c: (b, 0, 0, 0))],
        out_shape=[jax.ShapeDtypeStruct((H_C, m, DV_C), BF16),
                   jax.ShapeDtypeStruct((bsz, H_C, DK_C, DV_C), F32)],
        scratch_shapes=[pltpu.VMEM((H_C, DV_C, DK_C), F32)],
        compiler_params=_params("parallel", "arbitrary"),
        name="c_prompt",
    )(u, u, u, u, w["clb"], w["cnw"], w["gam"], w["lvl"])
    return og, cs


def _c_sample_kernel(q_ref, f_ref, v_ref, g_ref, s_ref, clb_ref, cnw_ref, og_ref, so_ref, *, layer):
    bb = SAMPLE_BB
    lane = lax.broadcasted_iota(jnp.int32, (DK_C, LANE), 1)
    first_rows = lax.broadcasted_iota(jnp.int32, (LANE, DV_C), 0) < bb
    for hd in range(H_C):
        lb = _hg_lower_bound(clb_ref[hd], layer)
        f, kk = _hg_gates(f_ref[hd], lb)
        f_t = _pad_rows_t(f)
        k_t = _pad_rows_t(kk)
        qs = q_ref[hd] * (DK_C ** -0.5)
        v = v_ref[hd]
        vpad = jnp.where(first_rows, jnp.tile(v, (LANE // bb, 1)), 0.0).astype(BF16)
        orows = []
        for i in range(bb):
            fcol = jnp.broadcast_to(f_t[:, i:i + 1], (DK_C, DV_C))
            kv = _dot(jnp.where(lane == i, k_t, 0.0).astype(BF16), vpad)
            sn = fcol * s_ref[i, hd] + kv
            so_ref[i, hd] = sn
            qrow = jnp.broadcast_to(qs[i:i + 1, :], (2 * SUBLANE, DK_C)).astype(BF16)
            orows.append(_dot(qrow, sn.astype(BF16))[0:1, :])
        o = jnp.concatenate(orows, axis=0)
        og_ref[hd] = _hg_out(o, g_ref[hd], cnw_ref[hd])


def _c_sample(u, s_c, w, layer):
    bsz = s_c.shape[0]
    bb = SAMPLE_BB
    assert bsz % bb == 0
    depth = w["clb"].shape[1]

    def part(k):
        return pl.BlockSpec((H_C, bb, LANE), lambda i: (k, i, 0))

    c3 = lambda i: (0, 0, 0)
    og, so = pl.pallas_call(
        functools.partial(_c_sample_kernel, layer=layer), grid=(bsz // bb,),
        in_specs=[part(0), part(1), part(2), part(3),
                  pl.BlockSpec((bb, H_C, DK_C, DV_C), lambda i: (i, 0, 0, 0)),
                  pl.BlockSpec((H_C, depth, DK_C), c3), pl.BlockSpec((H_C, 1, DV_C), c3)],
        out_specs=[pl.BlockSpec((H_C, bb, LANE), lambda i: (0, i, 0)),
                   pl.BlockSpec((bb, H_C, DK_C, DV_C), lambda i: (i, 0, 0, 0))],
        out_shape=[jax.ShapeDtypeStruct((H_C, bsz, DV_C), F32), jax.ShapeDtypeStruct(s_c.shape, F32)],
        compiler_params=_params("parallel"),
        name="c_sample",
    )(u, u, u, u, s_c, w["clb"], w["cnw"])
    return og, so


def _row(v, width=None):
    v = v.astype(F32).reshape(1, -1)
    if width is not None and v.shape[1] < width:
        v = jnp.pad(v, ((0, 0), (0, width - v.shape[1])))
    return v


def kernel(x_prompt, x_sample, p_prompt, p_sample, state_a_h, state_a_conv, state_b_ssm, state_b_conv, state_c,
           norm_w, norm_f, ab_w_in, a_conv_w, a_conv_b, a_w_r, a_b_r, a_w_i, a_b_i, a_lam, b_conv_w, b_conv_b,
           b_dt_bias, b_a_log, b_d, b_norm_w, ab_w_out, c_w_in, c_lb, c_norm_w, c_w_out, ple_proj, ple_gate):
    depth = norm_w.shape[0]
    bp, seq, _ = x_prompt.shape
    bs = x_sample.shape[0]
    hp = x_prompt.reshape(bp * seq, D_MODEL)
    hs = x_sample.reshape(bs, D_MODEL)
    pp = p_prompt.reshape(depth, bp * seq, D_PLE)
    ps = p_sample.reshape(depth, bs, D_PLE)
    gam, lvl = _hg_gamma(), _hg_level_table()
    clb = c_lb.astype(F32).reshape(depth, H_C, DK_C).transpose(1, 0, 2)

    ah_p, ac_p, bs_p, bc_p, c_p = [], [], [], [], []
    ah_s, ac_s, bs_s, bc_s, c_s = [], [], [], [], []
    for i in range(depth):
        j = i // 2
        final = i == depth - 1
        wg = ple_gate[i].astype(BF16)
        wp = ple_proj[i].astype(BF16)
        if i % 2 == 0:
            w_main = ab_w_in[j][:, :AB_MAIN].astype(BF16)
            w_dt = jnp.pad(ab_w_in[j][:, AB_MAIN:], ((0, 0), (0, LANE - H_B))).astype(BF16)
            wo = ab_w_out[j].astype(BF16)
            w = dict(acw=a_conv_w[j].astype(F32), acb=_row(a_conv_b[j]),
                     wr=a_w_r[j].astype(BF16), br=_row(a_b_r[j]), wi=a_w_i[j].astype(BF16), bi=_row(a_b_i[j]),
                     lam=_row(a_lam[j]), bcw=b_conv_w[j].astype(F32), bcb=_row(b_conv_b[j]),
                     dtb=_row(b_dt_bias[j], LANE), alog=_row(b_a_log[j], LANE),
                     dexp=_row(jnp.repeat(b_d[j], HD_B)), bnw=_row(b_norm_w[j]))
            u, dtr = _in_proj(hp, norm_w[i], w_main, w_dt)
            mix, s1, s2, s3, s4 = _ab_prompt(u, dtr, bp, seq, w)
            ah_p.append(s1); ac_p.append(s2); bs_p.append(s3); bc_p.append(s4)
            hp = _out_proj(mix, hp, pp, i, wo, wg, wp, norm_f, head_major=False, final=final)
            u, dtr = _in_proj(hs, norm_w[i], w_main, w_dt)
            mix, s1, s2, s3, s4 = _ab_sample(u, dtr, state_a_h[j], state_a_conv[j], state_b_ssm[j],
                                             state_b_conv[j], w)
            ah_s.append(s1); ac_s.append(s2); bs_s.append(s3); bc_s.append(s4)
            hs = _out_proj(mix, hs, ps, i, wo, wg, wp, norm_f, head_major=False, final=final)
        else:
            col_scale = jnp.concatenate([jnp.ones((HK_C,), F32), jnp.full((HK_C,), 0.5, F32),
                                         jnp.ones((D_C,), F32), jnp.full((D_C,), 0.5, F32)])
            w_in = (c_w_in[j] * col_scale).astype(BF16)
            wo = c_w_out[j].astype(BF16)
            w = dict(clb=clb, cnw=c_norm_w[j].astype(F32).reshape(H_C, 1, DV_C), gam=gam, lvl=lvl)
            u = _in_proj(hp, norm_w[i], w_in, head_major=True)
            og, s1 = _c_prompt(u, bp, seq, w, i)
            c_p.append(s1)
            hp = _out_proj(og, hp, pp, i, wo, wg, wp, norm_f, head_major=True, final=final)
            u = _in_proj(hs, norm_w[i], w_in, head_major=True)
            og, s1 = _c_sample(u, state_c[j], w, i)
            c_s.append(s1)
            hs = _out_proj(og, hs, ps, i, wo, wg, wp, norm_f, head_major=True, final=final)
    return (hp.reshape(bp, seq, D_MODEL), hs.reshape(bs, 1, D_MODEL),
            jnp.stack(ah_p), jnp.stack(ac_p), jnp.stack(bs_p), jnp.stack(bc_p), jnp.stack(c_p),
            jnp.stack(ah_s), jnp.stack(ac_s), jnp.stack(bs_s), jnp.stack(bc_s), jnp.stack(c_s))
```

```python
import functools

import jax
import jax.numpy as jnp
from jax import lax
from jax.experimental import pallas as pl
from jax.experimental.pallas import tpu as pltpu

F32 = jnp.float32
BF16 = jnp.bfloat16

D_MODEL = 1024
D_PLE = 256
EPS = 1e-6
CONV_W = 4
D_A = D_MODEL
A_BLOCKS = 8
A_BLK = D_A // A_BLOCKS
LRU_C = 8.0
D_B = D_MODEL
HD_B = 64
H_B = D_B // HD_B
N_B = 128
G_B = 2
CONV_DIM_B = D_B + 2 * G_B * N_B
D_C = 2 * D_MODEL
H_C = 16
DK_C = 128
DV_C = D_C // H_C
HK_C = H_C * DK_C
AB_MAIN = 2 * D_A + D_B + CONV_DIM_B
IN_C = 2 * HK_C + 2 * D_C

LANE = 128
SUBLANE = 8
BF16_ROWS = 16
LOG2_E = 1.4426950408889634
VMEM_LIMIT = 52 * 1024 * 1024

PROJ_MAX_TN = 2048
SSD_CHUNK = 128
HG_CHUNK = 64
HG_LEVELS = 6
HG_BLOCK = 256
HG_UNROLL = 8
SAMPLE_BB = 8

_DN_TR = (((1,), (1,)), ((), ()))
_DN_TL = (((0,), (0,)), ((), ()))


def _dot(a, b):
    return jnp.dot(a, b, preferred_element_type=F32)


def _dot_tr(a, b):
    return lax.dot_general(a, b, _DN_TR, preferred_element_type=F32)


def _dot_tl(a, b):
    return lax.dot_general(a, b, _DN_TL, preferred_element_type=F32)


def _silu_half(x_half):
    return x_half + x_half * jnp.tanh(x_half)


def _rmsnorm(x, w):
    return x * lax.rsqrt(jnp.mean(x * x, axis=-1, keepdims=True) + EPS) * w


def _params(*sem):
    return pltpu.CompilerParams(dimension_semantics=sem, vmem_limit_bytes=VMEM_LIMIT)


def _in_proj_kernel(x_ref, nw_ref, w_ref, *rest, has_extra, head_major):
    if has_extra:
        wx_ref, o_ref, ox_ref, xn_ref = rest
    else:
        o_ref, xn_ref = rest

    @pl.when(pl.program_id(1) == 0)
    def _():
        xn_ref[...] = _rmsnorm(x_ref[...], nw_ref[...]).astype(BF16)
        if has_extra:
            ox_ref[...] = _dot(xn_ref[...], wx_ref[...])

    acc = _dot(xn_ref[...], w_ref[...])
    if head_major:
        for k in range(acc.shape[1] // LANE):
            o_ref[k] = acc[:, k * LANE:(k + 1) * LANE]
    else:
        o_ref[...] = acc


def _in_proj(x, nw, w, w_extra=None, *, head_major=False):
    m, k = x.shape
    n = w.shape[1]
    tm = min(m, 1024)
    tn = next(c for c in range(PROJ_MAX_TN, 0, -LANE) if n % c == 0)
    assert m % tm == 0 and n % tn == 0 and tn % LANE == 0
    grid = (m // tm, n // tn)
    in_specs = [pl.BlockSpec((tm, k), lambda i, j: (i, 0)),
                pl.BlockSpec((1, k), lambda i, j: (0, 0)),
                pl.BlockSpec((k, tn), lambda i, j: (0, j))]
    args = [x, nw.reshape(1, k), w]
    if head_major:
        out_shape = [jax.ShapeDtypeStruct((n // LANE, m, LANE), F32)]
        out_specs = [pl.BlockSpec((tn // LANE, tm, LANE), lambda i, j: (j, i, 0))]
    else:
        out_shape = [jax.ShapeDtypeStruct((m, n), F32)]
        out_specs = [pl.BlockSpec((tm, tn), lambda i, j: (i, j))]
    if w_extra is not None:
        in_specs.append(pl.BlockSpec((k, LANE), lambda i, j: (0, 0)))
        args.append(w_extra)
        out_shape.append(jax.ShapeDtypeStruct((m, LANE), F32))
        out_specs.append(pl.BlockSpec((tm, LANE), lambda i, j: (i, 0)))
    outs = pl.pallas_call(
        functools.partial(_in_proj_kernel, has_extra=w_extra is not None, head_major=head_major),
        grid=grid, in_specs=in_specs, out_specs=out_specs, out_shape=out_shape,
        scratch_shapes=[pltpu.VMEM((tm, k), BF16)],
        compiler_params=_params("parallel", "arbitrary"),
        name="in_proj",
    )(*args)
    return outs if w_extra is not None else outs[0]


def _residual_ple(h, p_ref, wg_ref, wp_ref, nf_ref, final):
    gate_t = jnp.tanh(_dot(h.astype(BF16), wg_ref[...]))
    pe_half = _dot(p_ref[...].astype(BF16), wp_ref[...])
    h = h + pe_half + pe_half * gate_t
    return _rmsnorm(h, nf_ref[...]) if final else h


def _out_proj_kernel(mix_ref, h_ref, p_ref, wo_ref, wg_ref, wp_ref, nf_ref, o_ref, *, head_major, final):
    if head_major:
        mix = jnp.concatenate([mix_ref[k] for k in range(mix_ref.shape[0])], axis=-1)
    else:
        mix = mix_ref[...]
    h = h_ref[...] + _dot(mix.astype(BF16), wo_ref[...])
    o_ref[...] = _residual_ple(h, p_ref, wg_ref, wp_ref, nf_ref, final)


def _out_proj(mix, h, p, layer, wo, wg, wp, nf, *, head_major, final):
    m, d = h.shape
    tm = min(m, 512)
    assert m % tm == 0
    if head_major:
        mix_spec = pl.BlockSpec((mix.shape[0], tm, LANE), lambda i: (0, i, 0))
    else:
        mix_spec = pl.BlockSpec((tm, mix.shape[1]), lambda i: (i, 0))
    const = lambda i: (0, 0)
    return pl.pallas_call(
        functools.partial(_out_proj_kernel, head_major=head_major, final=final),
        grid=(m // tm,),
        in_specs=[mix_spec,
                  pl.BlockSpec((tm, d), lambda i: (i, 0)),
                  pl.BlockSpec((None, tm, p.shape[2]), lambda i: (layer, i, 0)),
                  pl.BlockSpec(wo.shape, const), pl.BlockSpec(wg.shape, const),
                  pl.BlockSpec(wp.shape, const), pl.BlockSpec((1, d), const)],
        out_specs=pl.BlockSpec((tm, d), lambda i: (i, 0)),
        out_shape=jax.ShapeDtypeStruct((m, d), F32),
        compiler_params=_params("parallel"),
        name="out_proj",
    )(mix, h, p, wo, wg, wp, nf.reshape(1, d))


def _lru_gates(xc, wr_ref, br_ref, wi_ref, bi_ref, lam_ref):
    xcb = xc.astype(BF16)
    r_parts, i_parts = [], []
    for k in range(A_BLOCKS):
        xk = xcb[:, k * A_BLK:(k + 1) * A_BLK]
        r_parts.append(_dot(xk, wr_ref[k]))
        i_parts.append(_dot(xk, wi_ref[k]))
    r = jax.nn.sigmoid(jnp.concatenate(r_parts, axis=-1) + br_ref[...])
    gi = jax.nn.sigmoid(jnp.concatenate(i_parts, axis=-1) + bi_ref[...])
    log_a = (-LRU_C) * r * jax.nn.softplus(-lam_ref[...])
    a = jnp.exp(log_a)
    v = jnp.tanh(-log_a) * (a * a + 1.0)
    u = jnp.where(v > 0.0, v * lax.rsqrt(v), 0.0) * (gi * xc)
    return a, u


def _group_norm_gate(y, bx, z_half, dexp_ref, bnw_ref):
    y = (y + dexp_ref[...] * bx) * _silu_half(z_half)
    gw = D_B // G_B
    parts = []
    for g in range(G_B):
        yg = y[:, g * gw:(g + 1) * gw]
        parts.append(yg * lax.rsqrt(jnp.mean(yg * yg, axis=-1, keepdims=True) + EPS))
    return jnp.concatenate(parts, axis=-1) * bnw_ref[...]


def _ab_prompt_kernel(ax_ref, ag_ref, z_ref, xbc_ref, dt_ref,
                      acw_ref, acb_ref, wr_ref, br_ref, wi_ref, bi_ref, lam_ref,
                      bcw_ref, bcb_ref, dtb_ref, alog_ref, dexp_ref, bnw_ref,
                      hin_ref, p_ref, wo_ref, wg_ref, wp_ref, nf_ref,
                      hout_ref, ah_ref, ac_ref, bs_ref, bc_ref,
                      xpa_ref, xpb_ref, h_ref, s_ref, *, final):
    c = pl.program_id(1)
    last = pl.num_programs(1) - 1
    t = SSD_CHUNK
    ntile = t // SUBLANE

    @pl.when(c == 0)
    def _():
        xpa_ref[...] = jnp.zeros_like(xpa_ref)
        xpb_ref[...] = jnp.zeros_like(xpb_ref)
        h_ref[...] = jnp.zeros_like(h_ref)
        s_ref[...] = jnp.zeros_like(s_ref)

    def tiles(x):
        return [x[i * SUBLANE:(i + 1) * SUBLANE, :] for i in range(ntile)]

    def conv(x, tail_ref, w_ref, b_ref):
        sub = lax.broadcasted_iota(jnp.int32, (SUBLANE, x.shape[1]), 0)
        xt = [tail_ref[...]] + tiles(x)
        taps = [jnp.broadcast_to(w_ref[k:k + 1, :], (SUBLANE, x.shape[1])) for k in range(CONV_W)]
        bias = jnp.broadcast_to(b_ref[...], (SUBLANE, x.shape[1]))
        acc = [bias + taps[CONV_W - 1] * xt[i + 1] for i in range(ntile)]
        for s in range(1, CONV_W):
            wk = taps[CONV_W - 1 - s]
            for i in range(ntile):
                merged = jnp.where(sub >= SUBLANE - s, xt[i], xt[i + 1])
                acc[i] = acc[i] + wk * pltpu.roll(merged, s, 0)
        tail_ref[...] = xt[ntile]
        return jnp.concatenate(acc, axis=0)

    ax = ax_ref[...]
    xc = conv(ax, xpa_ref, acw_ref, acb_ref)
    a, u = _lru_gates(xc, wr_ref, br_ref, wi_ref, bi_ref, lam_ref)
    sub = lax.broadcasted_iota(jnp.int32, (SUBLANE, D_A), 0)
    at, ut = tiles(a), tiles(u)
    step = 1
    while step < SUBLANE:
        m = sub >= step
        for i in range(ntile):
            ut[i] = jnp.where(m, at[i] * pltpu.roll(ut[i], step, 0) + ut[i], ut[i])
            at[i] = jnp.where(m, at[i] * pltpu.roll(at[i], step, 0), at[i])
        step *= 2
    carry = h_ref[0:1, :]
    hs = []
    for i in range(ntile):
        hs.append(ut[i] + at[i] * carry)
        carry = hs[i][SUBLANE - 1:SUBLANE, :]
    h = jnp.concatenate(hs, axis=0)
    h_ref[0:1, :] = carry
    a_out = (h * _silu_half(ag_ref[...])).astype(BF16)

    xb = xbc_ref[...]
    xbc = _silu_half(conv(xb, xpb_ref, bcw_ref, bcb_ref))
    bx = xbc[:, 0:D_B]
    bxb = bx.astype(BF16)
    dt = jax.nn.softplus(dt_ref[...] + dtb_ref[...])
    adt = dt * (-jnp.exp(alog_ref[...]))
    ti = lax.broadcasted_iota(jnp.int32, (t, t), 0)
    si = lax.broadcasted_iota(jnp.int32, (t, t), 1)
    causal = ti >= si
    acs = jnp.dot(causal.astype(F32), adt, preferred_element_type=F32,
                  precision=lax.Precision.HIGHEST)
    a_last = acs[t - 1:t, :]
    wq = jnp.exp(a_last - acs) * dt
    eacs = jnp.exp(acs)
    ealast = jnp.exp(a_last)
    acs_t = acs.T
    dt_t = dt.T
    lane = lax.broadcasted_iota(jnp.int32, (t, LANE), 1)
    rowi = lax.broadcasted_iota(jnp.int32, (LANE, N_B), 0)
    hpg = H_B // G_B
    ys = []
    cb = None
    for j in range(H_B // 2):
        g = (2 * j) // hpg
        bg = xbc[:, D_B + g * N_B:D_B + (g + 1) * N_B]
        cg = xbc[:, D_B + G_B * N_B + g * N_B:D_B + G_B * N_B + (g + 1) * N_B]
        if (2 * j) % hpg == 0:
            cb = _dot_tr(cg.astype(BF16), bg.astype(BF16))
        xpair = bxb[:, j * LANE:(j + 1) * LANE]
        sp = s_ref[j]
        spb = sp.astype(BF16)
        y_h, up_h = [], []
        for hh in range(2):
            hd = 2 * j + hh
            seg = jnp.broadcast_to(acs[:, hd:hd + 1], (t, t)) - jnp.broadcast_to(acs_t[hd:hd + 1, :], (t, t))
            lmat = jnp.exp(jnp.where(causal, seg, -1e30))
            mmat = (cb * lmat * jnp.broadcast_to(dt_t[hd:hd + 1, :], (t, t))).astype(BF16)
            ec = (jnp.broadcast_to(eacs[:, hd:hd + 1], (t, N_B)) * cg).astype(BF16)
            y_h.append(_dot(mmat, xpair) + _dot_tr(ec, spb))
            bw = (bg * jnp.broadcast_to(wq[:, hd:hd + 1], (t, N_B))).astype(BF16)
            up_h.append(_dot_tl(xpair, bw))
        ys.append(jnp.where(lane < HD_B, y_h[0], y_h[1]))
        dec = jnp.where(rowi < HD_B,
                        jnp.broadcast_to(ealast[:, 2 * j:2 * j + 1], (LANE, N_B)),
                        jnp.broadcast_to(ealast[:, 2 * j + 1:2 * j + 2], (LANE, N_B)))
        s_ref[j] = dec * sp + jnp.where(rowi < HD_B, up_h[0], up_h[1])
    y = jnp.concatenate(ys, axis=-1)
    b_out = _group_norm_gate(y, bx, z_ref[...], dexp_ref, bnw_ref).astype(BF16)

    hres = hin_ref[...] + _dot(a_out, wo_ref[0:D_A, :]) + _dot(b_out, wo_ref[D_A:D_A + D_B, :])
    hout_ref[...] = _residual_ple(hres, p_ref, wg_ref, wp_ref, nf_ref, final)

    @pl.when(c == last)
    def _():
        ah_ref[0] = h[t - 1:t, :]
        ac_ref[0] = ax[t - (CONV_W - 1):t, :]
        bc_ref[0] = xb[t - (CONV_W - 1):t, :]
        for j in range(H_B // 2):
            sj = s_ref[j]
            bs_ref[0, 2 * j] = sj[0:HD_B, :]
            bs_ref[0, 2 * j + 1] = sj[HD_B:2 * HD_B, :]


def _ab_prompt(u, dtr, h, p, layer, wo, wg, wp, nf, bsz, seq, w, *, final):
    t = SSD_CHUNK
    assert seq % t == 0
    nc = seq // t
    m = bsz * seq
    rows = lambda b, c: b * nc + c
    cvec = lambda b, c: (0, 0)
    c3 = lambda b, c: (0, 0, 0)
    in_specs = [
        pl.BlockSpec((t, D_A), lambda b, c: (rows(b, c), 0)),
        pl.BlockSpec((t, D_A), lambda b, c: (rows(b, c), 1)),
        pl.BlockSpec((t, D_B), lambda b, c: (rows(b, c), 2)),
        pl.BlockSpec((t, CONV_DIM_B), lambda b, c: (rows(b, c), 2)),
        pl.BlockSpec((t, LANE), lambda b, c: (rows(b, c), 0)),
        pl.BlockSpec((CONV_W, D_A), cvec), pl.BlockSpec((1, D_A), cvec),
        pl.BlockSpec((A_BLOCKS, A_BLK, A_BLK), c3), pl.BlockSpec((1, D_A), cvec),
        pl.BlockSpec((A_BLOCKS, A_BLK, A_BLK), c3), pl.BlockSpec((1, D_A), cvec),
        pl.BlockSpec((1, D_A), cvec),
        pl.BlockSpec((CONV_W, CONV_DIM_B), cvec), pl.BlockSpec((1, CONV_DIM_B), cvec),
        pl.BlockSpec((1, LANE), cvec), pl.BlockSpec((1, LANE), cvec),
        pl.BlockSpec((1, D_B), cvec), pl.BlockSpec((1, D_B), cvec),
        pl.BlockSpec((t, D_MODEL), lambda b, c: (rows(b, c), 0)),
        pl.BlockSpec((None, t, p.shape[2]), lambda b, c: (layer, rows(b, c), 0)),
        pl.BlockSpec(wo.shape, cvec), pl.BlockSpec(wg.shape, cvec), pl.BlockSpec(wp.shape, cvec),
        pl.BlockSpec((1, D_MODEL), cvec),
    ]
    out_shape = [
        jax.ShapeDtypeStruct((m, D_MODEL), F32),
        jax.ShapeDtypeStruct((bsz, 1, D_A), F32),
        jax.ShapeDtypeStruct((bsz, CONV_W - 1, D_A), F32),
        jax.ShapeDtypeStruct((bsz, H_B, HD_B, N_B), F32),
        jax.ShapeDtypeStruct((bsz, CONV_W - 1, CONV_DIM_B), F32),
    ]
    out_specs = [
        pl.BlockSpec((t, D_MODEL), lambda b, c: (rows(b, c), 0)),
        pl.BlockSpec((1, 1, D_A), lambda b, c: (b, 0, 0)),
        pl.BlockSpec((1, CONV_W - 1, D_A), lambda b, c: (b, 0, 0)),
        pl.BlockSpec((1, H_B, HD_B, N_B), lambda b, c: (b, 0, 0, 0)),
        pl.BlockSpec((1, CONV_W - 1, CONV_DIM_B), lambda b, c: (b, 0, 0)),
    ]
    hout, ah, ac, bs, bc = pl.pallas_call(
        functools.partial(_ab_prompt_kernel, final=final),
        grid=(bsz, nc), in_specs=in_specs, out_specs=out_specs, out_shape=out_shape,
        scratch_shapes=[pltpu.VMEM((SUBLANE, D_A), F32), pltpu.VMEM((SUBLANE, CONV_DIM_B), F32),
                        pltpu.VMEM((SUBLANE, D_A), F32), pltpu.VMEM((H_B // 2, 2 * HD_B, N_B), F32)],
        compiler_params=_params("parallel", "arbitrary"),
        name="ab_prompt",
    )(u, u, u, u, dtr, w["acw"], w["acb"], w["wr"], w["br"], w["wi"], w["bi"], w["lam"],
      w["bcw"], w["bcb"], w["dtb"], w["alog"], w["dexp"], w["bnw"],
      h, p, wo, wg, wp, nf.reshape(1, D_MODEL))
    return hout, ah.reshape(bsz, D_A), ac, bs, bc


def _ab_sample_rows_kernel(ax_ref, ag_ref, xbc_ref, dt_ref, sah_ref, sac_ref, sbc_ref,
                           acw_ref, acb_ref, wr_ref, br_ref, wi_ref, bi_ref, lam_ref,
                           bcw_ref, bcb_ref, dtb_ref,
                           aout_ref, ah_ref, ac_ref, bc_ref, xact_ref, dts_ref):
    def conv1(x, buf_ref, w_ref, b_ref, nbuf_ref, width):
        y = b_ref[...] + w_ref[CONV_W - 1:CONV_W, :] * x
        for k in range(CONV_W - 1):
            y = y + w_ref[k:k + 1, :] * buf_ref[:, k * width:(k + 1) * width]
        for k in range(CONV_W - 2):
            nbuf_ref[:, k * width:(k + 1) * width] = buf_ref[:, (k + 1) * width:(k + 2) * width]
        nbuf_ref[:, (CONV_W - 2) * width:(CONV_W - 1) * width] = x
        return y

    xc = conv1(ax_ref[...], sac_ref, acw_ref, acb_ref, ac_ref, D_A)
    a, u = _lru_gates(xc, wr_ref, br_ref, wi_ref, bi_ref, lam_ref)
    h = a * sah_ref[...] + u
    ah_ref[...] = h
    aout_ref[...] = h * _silu_half(ag_ref[...])
    xact_ref[...] = _silu_half(conv1(xbc_ref[...], sbc_ref, bcw_ref, bcb_ref, bc_ref, CONV_DIM_B))
    dts_ref[...] = jax.nn.softplus(dt_ref[...] + dtb_ref[...])


def _pad_rows_t(x):
    pad = jnp.zeros((LANE - x.shape[0], x.shape[1]), F32)
    return jnp.concatenate([x, pad], axis=0).T


def _ab_sample_state_kernel(s_ref, xact_ref, dts_ref, z_ref, alog_ref, dexp_ref, bnw_ref,
                            so_ref, bout_ref, y_ref):
    bb = SAMPLE_BB
    xact = xact_ref[...]
    bx = xact[:, 0:D_B]
    dts = dts_ref[...]
    dec_t = _pad_rows_t(jnp.exp(dts * (-jnp.exp(alog_ref[...]))))
    dts_t = _pad_rows_t(dts)
    hpg = H_B // G_B
    for j in range(H_B // 2):
        g = (2 * j) // hpg
        xt = _pad_rows_t(bx[:, j * LANE:(j + 1) * LANE])
        dtp = jnp.concatenate([jnp.broadcast_to(dts_t[2 * j:2 * j + 1, :], (HD_B, LANE)),
                               jnp.broadcast_to(dts_t[2 * j + 1:2 * j + 2, :], (HD_B, LANE))], axis=0)
        xdt = xt * dtp
        for i in range(bb):
            brow = jnp.broadcast_to(xact[i:i + 1, D_B + g * N_B:D_B + (g + 1) * N_B], (2 * HD_B, N_B))
            crow = jnp.broadcast_to(
                xact[i:i + 1, D_B + G_B * N_B + g * N_B:D_B + G_B * N_B + (g + 1) * N_B], (2 * SUBLANE, N_B))
            upd = jnp.broadcast_to(xdt[:, i:i + 1], (2 * HD_B, N_B)) * brow
            news = []
            for hh in range(2):
                hd = 2 * j + hh
                dec = jnp.broadcast_to(dec_t[hd:hd + 1, i:i + 1], (HD_B, N_B))
                sn = dec * s_ref[i, hd] + upd[hh * HD_B:(hh + 1) * HD_B, :]
                so_ref[i, hd] = sn
                news.append(sn)
            spair = jnp.concatenate(news, axis=0).astype(BF16)
            yrow = _dot_tr(crow.astype(BF16), spair)
            y_ref[i:i + 1, j * LANE:(j + 1) * LANE] = yrow[0:1, :]
    bout_ref[...] = _group_norm_gate(y_ref[...], bx, z_ref[...], dexp_ref, bnw_ref)


def _ab_sample(u, dtr, s_ah, s_ac, s_bs, s_bc, w):
    bsz = u.shape[0]
    full = lambda shape: pl.BlockSpec(shape, lambda i: tuple(0 for _ in shape))
    cw = CONV_W - 1
    aout, ah, ac, bc, xact, dts = pl.pallas_call(
        _ab_sample_rows_kernel, grid=(1,),
        in_specs=[pl.BlockSpec((bsz, D_A), lambda i: (0, 0)), pl.BlockSpec((bsz, D_A), lambda i: (0, 1)),
                  pl.BlockSpec((bsz, CONV_DIM_B), lambda i: (0, 2)), full((bsz, LANE)),
                  full((bsz, D_A)), full((bsz, cw * D_A)), full((bsz, cw * CONV_DIM_B)),
                  full((CONV_W, D_A)), full((1, D_A)),
                  full((A_BLOCKS, A_BLK, A_BLK)), full((1, D_A)),
                  full((A_BLOCKS, A_BLK, A_BLK)), full((1, D_A)), full((1, D_A)),
                  full((CONV_W, CONV_DIM_B)), full((1, CONV_DIM_B)), full((1, LANE))],
        out_specs=[full((bsz, D_A)), full((bsz, D_A)), full((bsz, cw * D_A)), full((bsz, cw * CONV_DIM_B)),
                   full((bsz, CONV_DIM_B)), full((bsz, LANE))],
        out_shape=[jax.ShapeDtypeStruct((bsz, D_A), F32), jax.ShapeDtypeStruct((bsz, D_A), F32),
                   jax.ShapeDtypeStruct((bsz, cw * D_A), F32), jax.ShapeDtypeStruct((bsz, cw * CONV_DIM_B), F32),
                   jax.ShapeDtypeStruct((bsz, CONV_DIM_B), F32), jax.ShapeDtypeStruct((bsz, LANE), F32)],
        compiler_params=_params("arbitrary"),
        name="ab_sample_rows",
    )(u, u, u, dtr, s_ah, s_ac.reshape(bsz, cw * D_A), s_bc.reshape(bsz, cw * CONV_DIM_B),
      w["acw"], w["acb"], w["wr"], w["br"], w["wi"], w["bi"], w["lam"], w["bcw"], w["bcb"], w["dtb"])

    bb = SAMPLE_BB
    assert bsz % bb == 0
    cvec = lambda i: (0, 0)
    bs, bout = pl.pallas_call(
        _ab_sample_state_kernel, grid=(bsz // bb,),
        in_specs=[pl.BlockSpec((bb, H_B, HD_B, N_B), lambda i: (i, 0, 0, 0)),
                  pl.BlockSpec((bb, CONV_DIM_B), lambda i: (i, 0)),
                  pl.BlockSpec((bb, LANE), lambda i: (i, 0)),
                  pl.BlockSpec((bb, D_B), lambda i: (i, 2)),
                  pl.BlockSpec((1, LANE), cvec), pl.BlockSpec((1, D_B), cvec), pl.BlockSpec((1, D_B), cvec)],
        out_specs=[pl.BlockSpec((bb, H_B, HD_B, N_B), lambda i: (i, 0, 0, 0)),
                   pl.BlockSpec((bb, D_B), lambda i: (i, 0))],
        out_shape=[jax.ShapeDtypeStruct(s_bs.shape, F32), jax.ShapeDtypeStruct((bsz, D_B), F32)],
        scratch_shapes=[pltpu.VMEM((bb, D_B), F32)],
        compiler_params=_params("parallel"),
        name="ab_sample_state",
    )(s_bs, xact, dts, u, w["alog"], w["dexp"], w["bnw"])
    mix = jnp.concatenate([aout, bout], axis=-1)
    return mix, ah, ac.reshape(bsz, cw, D_A), bs, bc.reshape(bsz, cw, CONV_DIM_B)


def _hg_lower_bound(clb, layer):
    mx = jnp.max(clb, axis=0, keepdims=True)
    ex = jnp.exp(clb - mx)
    return jnp.sum(ex[1:layer + 1], axis=0, keepdims=True) / jnp.sum(ex, axis=0, keepdims=True)


def _hg_gates(fx_half, lb):
    f = 0.5 * (1.0 + lb) + (0.5 * (1.0 - lb)) * jnp.tanh(fx_half)
    return f, 1.0 - f


def _hg_out(o, gate_half, cnw):
    return o * lax.rsqrt(jnp.mean(o * o, axis=-1, keepdims=True) + EPS) * cnw * (gate_half + gate_half * jnp.tanh(gate_half))


def _hg_gamma():
    import numpy as np
    q = HG_CHUNK
    t = np.arange(q)[:, None]
    tau = np.arange(q)[None, :]
    mats = [(tau <= t)]
    for l in range(HG_LEVELS):
        w = 1 << l
        ref = (t // (2 * w)) * (2 * w) + w - 1
        upper = (t % (2 * w)) >= w
        mats.append(np.where(upper, (tau > ref) & (tau <= t), (tau > t) & (tau <= ref)))
    gam = np.concatenate(mats, axis=0).astype(np.float32)
    return jnp.asarray(np.concatenate([gam, gam], axis=1), dtype=BF16)


def _hg_level_table():
    import numpy as np
    q = HG_CHUNK
    t = np.arange(q)[:, None]
    s = np.arange(q)[None, :]
    x = t ^ s
    lvl = np.floor(np.log2(np.maximum(x, 1))).astype(np.int32)
    return jnp.asarray(np.where(t > s, lvl, -1).astype(np.int32))


def _c_prompt_kernel(q_ref, f_ref, v_ref, g_ref, clb_ref, cnw_ref, gam_ref, lvl_ref,
                     og_ref, cs_ref, st_ref, *, layer):
    c = pl.program_id(1)
    last = pl.num_programs(1) - 1
    qc = HG_CHUNK

    @pl.when(c == 0)
    def _():
        st_ref[...] = jnp.zeros_like(st_ref)

    gam = gam_ref[...]
    ntile = qc // SUBLANE
    sub = lax.broadcasted_iota(jnp.int32, (SUBLANE, DK_C), 0)
    sub_levels = min(HG_LEVELS, 3)
    sub_upper = [(sub & (1 << l)) != 0 for l in range(sub_levels)]

    def tiles(x):
        return [x[i * SUBLANE:(i + 1) * SUBLANE, :] for i in range(ntile)]

    def gate_split(hd, rows):
        f, kk = _hg_gates(f_ref[hd, rows, :], _hg_lower_bound(clb_ref[hd], layer))
        g = jnp.log(f) * LOG2_E
        g1 = g.astype(BF16)
        g2 = (g - g1.astype(F32)).astype(BF16)
        return kk, jnp.concatenate([g1, g2], axis=0)

    def scores(hd, rows, kk, sums):
        qh = q_ref[hd, rows, :] * (DK_C ** -0.5)
        bcum = sums[0:qc]
        st = st_ref[hd]
        o = _dot_tr((qh * jnp.exp2(bcum)).astype(BF16), st.astype(BF16))
        qt, kt = tiles(qh), tiles(kk)
        prods = []
        for l in range(HG_LEVELS):
            if l < sub_levels:
                sel = [jnp.where(sub_upper[l], qt[i], kt[i]) for i in range(ntile)]
            else:
                sel = [qt[i] if (i >> (l - sub_levels)) & 1 else kt[i] for i in range(ntile)]
            x = (jnp.concatenate(sel, axis=0) * jnp.exp2(sums[(l + 1) * qc:(l + 2) * qc])).astype(BF16)
            half = (1 << l) // BF16_ROWS
            if half == 0:
                p = tiles(_dot_tr(x, x))
                prods.append({i: p[i] for i in range(ntile)})
            else:
                ups = [r for r in range(qc // BF16_ROWS) if (r // half) & 1]
                pu = _dot_tr(jnp.concatenate([x[r * BF16_ROWS:(r + 1) * BF16_ROWS, :] for r in ups], axis=0), x)
                tpr = BF16_ROWS // SUBLANE
                prods.append({r * tpr + k: pu[(n * tpr + k) * SUBLANE:(n * tpr + k + 1) * SUBLANE, :]
                              for n, r in enumerate(ups) for k in range(tpr)})
        return qh, st, o, prods

    def combine(hd, rows, kk, bcum, qh, st, o, prods):
        arows = []
        for i in range(ntile):
            lv = lvl_ref[i * SUBLANE:(i + 1) * SUBLANE, :]
            a = jnp.zeros((SUBLANE, qc), F32)
            for l in range(HG_LEVELS):
                if l < sub_levels or (i >> (l - sub_levels)) & 1:
                    a = jnp.where(lv == l, prods[l][i], a)
            arows.append(a)
        amat = jnp.concatenate(arows, axis=0)
        v = v_ref[hd, rows, :]
        vb = v.astype(BF16)
        o = o + _dot(amat.astype(BF16), vb) + jnp.sum(qh * kk, axis=-1, keepdims=True) * v
        blast = bcum[qc - 1:qc, :]
        kdec = (kk * jnp.exp2(blast - bcum)).astype(BF16)
        st_ref[hd] = st * jnp.exp2(blast) + _dot_tl(vb, kdec)
        return o

    nchunk = HG_BLOCK // qc

    def body(idx, carry):
        hg = idx // nchunk
        rows = pl.ds(pl.multiple_of((idx % nchunk) * qc, qc), qc)
        heads = [hg * HG_UNROLL + k for k in range(HG_UNROLL)]
        gs = [gate_split(hd, rows) for hd in heads]
        sums = _dot(gam, jnp.concatenate([s for _, s in gs], axis=1))
        sums = [sums[:, k * DK_C:(k + 1) * DK_C] for k in range(HG_UNROLL)]
        sc = [scores(hd, rows, gs[k][0], sums[k]) for k, hd in enumerate(heads)]
        outs = [combine(hd, rows, gs[k][0], sums[k][0:qc], *sc[k]) for k, hd in enumerate(heads)]
        for k, hd in enumerate(heads):
            og_ref[hd, rows, :] = _hg_out(outs[k], g_ref[hd, rows, :], cnw_ref[hd]).astype(BF16)
        return carry

    lax.fori_loop(0, (H_C // HG_UNROLL) * nchunk, body, 0)

    @pl.when(c == last)
    def _():
        for hd in range(H_C):
            cs_ref[0, hd] = st_ref[hd].T


def _c_prompt(u, bsz, seq, w, layer):
    tb = HG_BLOCK
    assert seq % tb == 0 and tb % HG_CHUNK == 0 and (1 << HG_LEVELS) == HG_CHUNK
    nc = seq // tb
    m = bsz * seq
    depth = w["clb"].shape[1]

    def part(k):
        return pl.BlockSpec((H_C, tb, LANE), lambda b, c: (k, b * nc + c, 0))

    c2 = lambda b, c: (0, 0)
    c3 = lambda b, c: (0, 0, 0)
    og, cs = pl.pallas_call(
        functools.partial(_c_prompt_kernel, layer=layer), grid=(bsz, nc),
        in_specs=[part(0), part(1), part(2), part(3),
                  pl.BlockSpec((H_C, depth, DK_C), c3), pl.BlockSpec((H_C, 1, DV_C), c3),
                  pl.BlockSpec(w["gam"].shape, c2), pl.BlockSpec(w["lvl"].shape, c2)],
        out_specs=[pl.BlockSpec((H_C, tb, LANE), lambda b, c: (0, b * nc + c, 0)),
                   pl.BlockSpec((1, H_C, DK_C, DV_C), lambda b, c: (b, 0, 0, 0))],
        out_shape=[jax.ShapeDtypeStruct((H_C, m, DV_C), BF16),
                   jax.ShapeDtypeStruct((bsz, H_C, DK_C, DV_C), F32)],
        scratch_shapes=[pltpu.VMEM((H_C, DV_C, DK_C), F32)],
        compiler_params=_params("parallel", "arbitrary"),
        name="c_prompt",
    )(u, u, u, u, w["clb"], w["cnw"], w["gam"], w["lvl"])
    return og, cs


def _c_sample_kernel(q_ref, f_ref, v_ref, g_ref, s_ref, clb_ref, cnw_ref, og_ref, so_ref, *, layer):
    bb = SAMPLE_BB
    lane = lax.broadcasted_iota(jnp.int32, (DK_C, LANE), 1)
    first_rows = lax.broadcasted_iota(jnp.int32, (LANE, DV_C), 0) < bb
    for hd in range(H_C):
        lb = _hg_lower_bound(clb_ref[hd], layer)
        f, kk = _hg_gates(f_ref[hd], lb)
        f_t = _pad_rows_t(f)
        k_t = _pad_rows_t(kk)
        qs = q_ref[hd] * (DK_C ** -0.5)
        v = v_ref[hd]
        vpad = jnp.where(first_rows, jnp.tile(v, (LANE // bb, 1)), 0.0).astype(BF16)
        orows = []
        for i in range(bb):
            fcol = jnp.broadcast_to(f_t[:, i:i + 1], (DK_C, DV_C))
            kv = _dot(jnp.where(lane == i, k_t, 0.0).astype(BF16), vpad)
            sn = fcol * s_ref[i, hd] + kv
            so_ref[i, hd] = sn
            qrow = jnp.broadcast_to(qs[i:i + 1, :], (2 * SUBLANE, DK_C)).astype(BF16)
            orows.append(_dot(qrow, sn.astype(BF16))[0:1, :])
        o = jnp.concatenate(orows, axis=0)
        og_ref[hd] = _hg_out(o, g_ref[hd], cnw_ref[hd])


def _c_sample(u, s_c, w, layer):
    bsz = s_c.shape[0]
    bb = SAMPLE_BB
    assert bsz % bb == 0
    depth = w["clb"].shape[1]

    def part(k):
        return pl.BlockSpec((H_C, bb, LANE), lambda i: (k, i, 0))

    c3 = lambda i: (0, 0, 0)
    og, so = pl.pallas_call(
        functools.partial(_c_sample_kernel, layer=layer), grid=(bsz // bb,),
        in_specs=[part(0), part(1), part(2), part(3),
                  pl.BlockSpec((bb, H_C, DK_C, DV_C), lambda i: (i, 0, 0, 0)),
                  pl.BlockSpec((H_C, depth, DK_C), c3), pl.BlockSpec((H_C, 1, DV_C), c3)],
        out_specs=[pl.BlockSpec((H_C, bb, LANE), lambda i: (0, i, 0)),
                   pl.BlockSpec((bb, H_C, DK_C, DV_C), lambda i: (i, 0, 0, 0))],
        out_shape=[jax.ShapeDtypeStruct((H_C, bsz, DV_C), F32), jax.ShapeDtypeStruct(s_c.shape, F32)],
        compiler_params=_params("parallel"),
        name="c_sample",
    )(u, u, u, u, s_c, w["clb"], w["cnw"])
    return og, so


def _row(v, width=None):
    v = v.astype(F32).reshape(1, -1)
    if width is not None and v.shape[1] < width:
        v = jnp.pad(v, ((0, 0), (0, width - v.shape[1])))
    return v


def kernel(x_prompt, x_sample, p_prompt, p_sample, state_a_h, state_a_conv, state_b_ssm, state_b_conv, state_c,
           norm_w, norm_f, ab_w_in, a_conv_w, a_conv_b, a_w_r, a_b_r, a_w_i, a_b_i, a_lam, b_conv_w, b_conv_b,
           b_dt_bias, b_a_log, b_d, b_norm_w, ab_w_out, c_w_in, c_lb, c_norm_w, c_w_out, ple_proj, ple_gate):
    depth = norm_w.shape[0]
    bp, seq, _ = x_prompt.shape
    bs = x_sample.shape[0]
    hp = x_prompt.reshape(bp * seq, D_MODEL)
    hs = x_sample.reshape(bs, D_MODEL)
    pp = p_prompt.reshape(depth, bp * seq, D_PLE)
    ps = p_sample.reshape(depth, bs, D_PLE)
    gam, lvl = _hg_gamma(), _hg_level_table()
    clb = c_lb.astype(F32).reshape(depth, H_C, DK_C).transpose(1, 0, 2)

    ah_p, ac_p, bs_p, bc_p, c_p = [], [], [], [], []
    ah_s, ac_s, bs_s, bc_s, c_s = [], [], [], [], []
    for i in range(depth):
        j = i // 2
        final = i == depth - 1
        wg = (0.5 * ple_gate[i]).astype(BF16)
        wp = (0.5 * ple_proj[i]).astype(BF16)
        if i % 2 == 0:
            col_scale = jnp.concatenate([jnp.ones((D_A,), F32), jnp.full((D_A + D_B,), 0.5, F32),
                                         jnp.ones((CONV_DIM_B,), F32)])
            w_main = (ab_w_in[j][:, :AB_MAIN] * col_scale).astype(BF16)
            w_dt = jnp.pad(ab_w_in[j][:, AB_MAIN:], ((0, 0), (0, LANE - H_B))).astype(BF16)
            wo = ab_w_out[j].astype(BF16)
            w = dict(acw=a_conv_w[j].astype(F32), acb=_row(a_conv_b[j]),
                     wr=a_w_r[j].astype(BF16), br=_row(a_b_r[j]), wi=a_w_i[j].astype(BF16), bi=_row(a_b_i[j]),
                     lam=_row(a_lam[j]), bcw=0.5 * b_conv_w[j].astype(F32), bcb=0.5 * _row(b_conv_b[j]),
                     dtb=_row(b_dt_bias[j], LANE), alog=_row(b_a_log[j], LANE),
                     dexp=_row(jnp.repeat(b_d[j], HD_B)), bnw=_row(b_norm_w[j]))
            u, dtr = _in_proj(hp, norm_w[i], w_main, w_dt)
            hp, s1, s2, s3, s4 = _ab_prompt(u, dtr, hp, pp, i, wo, wg, wp, norm_f, bp, seq, w, final=final)
            ah_p.append(s1); ac_p.append(s2); bs_p.append(s3); bc_p.append(s4)
            u, dtr = _in_proj(hs, norm_w[i], w_main, w_dt)
            mix, s1, s2, s3, s4 = _ab_sample(u, dtr, state_a_h[j], state_a_conv[j], state_b_ssm[j],
                                             state_b_conv[j], w)
            ah_s.append(s1); ac_s.append(s2); bs_s.append(s3); bc_s.append(s4)
            hs = _out_proj(mix, hs, ps, i, wo, wg, wp, norm_f, head_major=False, final=final)
        else:
            col_scale = jnp.concatenate([jnp.ones((HK_C,), F32), jnp.full((HK_C,), 0.5, F32),
                                         jnp.ones((D_C,), F32), jnp.full((D_C,), 0.5, F32)])
            w_in = (c_w_in[j] * col_scale).astype(BF16)
            wo = c_w_out[j].astype(BF16)
            w = dict(clb=clb, cnw=c_norm_w[j].astype(F32).reshape(H_C, 1, DV_C), gam=gam, lvl=lvl)
            u = _in_proj(hp, norm_w[i], w_in, head_major=True)
            og, s1 = _c_prompt(u, bp, seq, w, i)
            c_p.append(s1)
            hp = _out_proj(og, hp, pp, i, wo, wg, wp, norm_f, head_major=True, final=final)
            u = _in_proj(hs, norm_w[i], w_in, head_major=True)
            og, s1 = _c_sample(u, state_c[j], w, i)
            c_s.append(s1)
            hs = _out_proj(og, hs, ps, i, wo, wg, wp, norm_f, head_major=True, final=final)
    return (hp.reshape(bp, seq, D_MODEL), hs.reshape(bs, 1, D_MODEL),
            jnp.stack(ah_p), jnp.stack(ac_p), jnp.stack(bs_p), jnp.stack(bc_p), jnp.stack(c_p),
            jnp.stack(ah_s), jnp.stack(ac_s), jnp.stack(bs_s), jnp.stack(bc_s), jnp.stack(c_s))
```

```python
import functools

import jax
import jax.numpy as jnp
from jax import lax
from jax.experimental import pallas as pl
from jax.experimental.pallas import tpu as pltpu

F32 = jnp.float32
BF16 = jnp.bfloat16

D_MODEL = 1024
D_PLE = 256
EPS = 1e-6
CONV_W = 4
D_A = D_MODEL
A_BLOCKS = 8
A_BLK = D_A // A_BLOCKS
LRU_C = 8.0
D_B = D_MODEL
HD_B = 64
H_B = D_B // HD_B
N_B = 128
G_B = 2
CONV_DIM_B = D_B + 2 * G_B * N_B
D_C = 2 * D_MODEL
H_C = 16
DK_C = 128
DV_C = D_C // H_C
HK_C = H_C * DK_C
AB_MAIN = 2 * D_A + D_B + CONV_DIM_B
IN_C = 2 * HK_C + 2 * D_C

LANE = 128
SUBLANE = 8
BF16_ROWS = 16
LOG2_E = 1.4426950408889634
VMEM_LIMIT = 52 * 1024 * 1024

PROJ_TM = 2048
PROJ_MAX_TN = 1024
SSD_CHUNK = 128
HG_CHUNK = 64
HG_LEVELS = 6
HG_MXU_LEVELS = 3
HG_BLOCK = 512
HG_UNROLL = 16
SAMPLE_BB = 8

_DN_TR = (((1,), (1,)), ((), ()))
_DN_TL = (((0,), (0,)), ((), ()))


def _dot(a, b):
    return jnp.dot(a, b, preferred_element_type=F32)


def _dot_tr(a, b):
    return lax.dot_general(a, b, _DN_TR, preferred_element_type=F32)


def _dot_tl(a, b):
    return lax.dot_general(a, b, _DN_TL, preferred_element_type=F32)


def _silu_half(x_half):
    return x_half + x_half * jnp.tanh(x_half)


def _rmsnorm(x, w):
    return x * lax.rsqrt(jnp.mean(x * x, axis=-1, keepdims=True) + EPS) * w


def _params(*sem):
    return pltpu.CompilerParams(dimension_semantics=sem, vmem_limit_bytes=VMEM_LIMIT)


def _in_proj_kernel(x_ref, nw_ref, w_ref, *rest, has_extra, head_major):
    if has_extra:
        wx_ref, o_ref, ox_ref, xn_ref = rest
    else:
        o_ref, xn_ref = rest

    @pl.when(pl.program_id(1) == 0)
    def _():
        xn_ref[...] = _rmsnorm(x_ref[...], nw_ref[...]).astype(BF16)
        if has_extra:
            ox_ref[...] = _dot(xn_ref[...], wx_ref[...])

    acc = _dot(xn_ref[...], w_ref[...])
    if head_major:
        for k in range(acc.shape[1] // LANE):
            o_ref[k] = acc[:, k * LANE:(k + 1) * LANE]
    else:
        o_ref[...] = acc


def _in_proj(x, nw, w, *, extra_cols=False, head_major=False):
    m, k = x.shape
    n = w.shape[1] - (LANE if extra_cols else 0)
    tm = min(m, PROJ_TM)
    tn = next(c for c in range(PROJ_MAX_TN, 0, -LANE) if n % c == 0)
    assert m % tm == 0 and n % tn == 0 and tn % LANE == 0
    grid = (m // tm, n // tn)
    in_specs = [pl.BlockSpec((tm, k), lambda i, j: (i, 0)),
                pl.BlockSpec((1, k), lambda i, j: (0, 0)),
                pl.BlockSpec((k, tn), lambda i, j: (0, j))]
    args = [x, nw.reshape(1, k), w]
    if head_major:
        out_shape = [jax.ShapeDtypeStruct((n // LANE, m, LANE), F32)]
        out_specs = [pl.BlockSpec((tn // LANE, tm, LANE), lambda i, j: (j, i, 0))]
    else:
        out_shape = [jax.ShapeDtypeStruct((m, n), F32)]
        out_specs = [pl.BlockSpec((tm, tn), lambda i, j: (i, j))]
    if extra_cols:
        in_specs.append(pl.BlockSpec((k, LANE), lambda i, j: (0, n // LANE)))
        args.append(w)
        out_shape.append(jax.ShapeDtypeStruct((m, LANE), F32))
        out_specs.append(pl.BlockSpec((tm, LANE), lambda i, j: (i, 0)))
    outs = pl.pallas_call(
        functools.partial(_in_proj_kernel, has_extra=extra_cols, head_major=head_major),
        grid=grid, in_specs=in_specs, out_specs=out_specs, out_shape=out_shape,
        scratch_shapes=[pltpu.VMEM((tm, k), BF16)],
        compiler_params=_params("parallel", "arbitrary"),
        name="in_proj",
    )(*args)
    return outs if extra_cols else outs[0]


def _residual_ple(h, p_ref, wg_ref, wp_ref, nf_ref, final):
    gate_t = jnp.tanh(_dot(h.astype(BF16), wg_ref[...]))
    pe_half = _dot(p_ref[...].astype(BF16), wp_ref[...])
    h = h + pe_half + pe_half * gate_t
    return _rmsnorm(h, nf_ref[...]) if final else h


def _out_proj_kernel(mix_ref, h_ref, p_ref, wo_ref, wg_ref, wp_ref, nf_ref, o_ref, *, head_major, final):
    if head_major:
        mix = jnp.concatenate([mix_ref[k] for k in range(mix_ref.shape[0])], axis=-1)
    else:
        mix = mix_ref[...]
    h = h_ref[...] + _dot(mix.astype(BF16), wo_ref[...])
    o_ref[...] = _residual_ple(h, p_ref, wg_ref, wp_ref, nf_ref, final)


def _out_proj(mix, h, p, layer, wo, wg, wp, nf, *, head_major, final):
    m, d = h.shape
    tm = min(m, 512)
    assert m % tm == 0
    if head_major:
        mix_spec = pl.BlockSpec((mix.shape[0], tm, LANE), lambda i: (0, i, 0))
    else:
        mix_spec = pl.BlockSpec((tm, mix.shape[1]), lambda i: (i, 0))
    const = lambda i: (0, 0)
    return pl.pallas_call(
        functools.partial(_out_proj_kernel, head_major=head_major, final=final),
        grid=(m // tm,),
        in_specs=[mix_spec,
                  pl.BlockSpec((tm, d), lambda i: (i, 0)),
                  pl.BlockSpec((None, tm, p.shape[2]), lambda i: (layer, i, 0)),
                  pl.BlockSpec(wo.shape, const), pl.BlockSpec(wg.shape, const),
                  pl.BlockSpec(wp.shape, const), pl.BlockSpec((1, d), const)],
        out_specs=pl.BlockSpec((tm, d), lambda i: (i, 0)),
        out_shape=jax.ShapeDtypeStruct((m, d), F32),
        compiler_params=_params("parallel"),
        name="out_proj",
    )(mix, h, p, wo, wg, wp, nf.reshape(1, d))


def _lru_gates(xc, wr_ref, br_ref, wi_ref, bi_ref, lam_ref):
    xcb = xc.astype(BF16)
    r_parts, i_parts = [], []
    for k in range(A_BLOCKS):
        xk = xcb[:, k * A_BLK:(k + 1) * A_BLK]
        r_parts.append(_dot(xk, wr_ref[k]))
        i_parts.append(_dot(xk, wi_ref[k]))
    r = jax.nn.sigmoid(jnp.concatenate(r_parts, axis=-1) + br_ref[...])
    gi = jax.nn.sigmoid(jnp.concatenate(i_parts, axis=-1) + bi_ref[...])
    log_a = (-LRU_C) * r * jax.nn.softplus(-lam_ref[...])
    a = jnp.exp(log_a)
    v = jnp.tanh(-log_a) * (a * a + 1.0)
    u = jnp.where(v > 0.0, v * lax.rsqrt(v), 0.0) * (gi * xc)
    return a, u


def _group_norm_gate(y, bx, z_half, dexp_ref, bnw_ref):
    y = (y + dexp_ref[...] * bx) * _silu_half(z_half)
    gw = D_B // G_B
    parts = []
    for g in range(G_B):
        yg = y[:, g * gw:(g + 1) * gw]
        parts.append(yg * lax.rsqrt(jnp.mean(yg * yg, axis=-1, keepdims=True) + EPS))
    return jnp.concatenate(parts, axis=-1) * bnw_ref[...]


def _ab_prompt_kernel(ax_ref, ag_ref, z_ref, xbc_ref, dt_ref,
                      acw_ref, acb_ref, wr_ref, br_ref, wi_ref, bi_ref, lam_ref,
                      bcw_ref, bcb_ref, dtb_ref, alog_ref, dexp_ref, bnw_ref,
                      hin_ref, p_ref, wo_ref, wg_ref, wp_ref, nf_ref,
                      hout_ref, ah_ref, ac_ref, bs_ref, bc_ref,
                      xpa_ref, xpb_ref, h_ref, s_ref, *, final):
    c = pl.program_id(1)
    last = pl.num_programs(1) - 1
    t = SSD_CHUNK
    ntile = t // SUBLANE

    @pl.when(c == 0)
    def _():
        xpa_ref[...] = jnp.zeros_like(xpa_ref)
        xpb_ref[...] = jnp.zeros_like(xpb_ref)
        h_ref[...] = jnp.zeros_like(h_ref)
        s_ref[...] = jnp.zeros_like(s_ref)

    def tiles(x):
        return [x[i * SUBLANE:(i + 1) * SUBLANE, :] for i in range(ntile)]

    def conv(x, tail_ref, w_ref, b_ref):
        sub = lax.broadcasted_iota(jnp.int32, (SUBLANE, x.shape[1]), 0)
        xt = [tail_ref[...]] + tiles(x)
        taps = [jnp.broadcast_to(w_ref[k:k + 1, :], (SUBLANE, x.shape[1])) for k in range(CONV_W)]
        bias = jnp.broadcast_to(b_ref[...], (SUBLANE, x.shape[1]))
        acc = [bias + taps[CONV_W - 1] * xt[i + 1] for i in range(ntile)]
        for s in range(1, CONV_W):
            wk = taps[CONV_W - 1 - s]
            for i in range(ntile):
                merged = jnp.where(sub >= SUBLANE - s, xt[i], xt[i + 1])
                acc[i] = acc[i] + wk * pltpu.roll(merged, s, 0)
        tail_ref[...] = xt[ntile]
        return jnp.concatenate(acc, axis=0)

    ax = ax_ref[...]
    xc = conv(ax, xpa_ref, acw_ref, acb_ref)
    a, u = _lru_gates(xc, wr_ref, br_ref, wi_ref, bi_ref, lam_ref)
    sub = lax.broadcasted_iota(jnp.int32, (SUBLANE, D_A), 0)
    at, ut = tiles(a), tiles(u)
    step = 1
    while step < SUBLANE:
        m = sub >= step
        for i in range(ntile):
            ut[i] = jnp.where(m, at[i] * pltpu.roll(ut[i], step, 0) + ut[i], ut[i])
            at[i] = jnp.where(m, at[i] * pltpu.roll(at[i], step, 0), at[i])
        step *= 2
    carry = h_ref[0:1, :]
    hs = []
    for i in range(ntile):
        hs.append(ut[i] + at[i] * carry)
        carry = hs[i][SUBLANE - 1:SUBLANE, :]
    h = jnp.concatenate(hs, axis=0)
    h_ref[0:1, :] = carry
    a_out = (h * _silu_half(ag_ref[...])).astype(BF16)

    xb = xbc_ref[...]
    xbc = _silu_half(conv(xb, xpb_ref, bcw_ref, bcb_ref))
    bx = xbc[:, 0:D_B]
    bxb = bx.astype(BF16)
    dt = jax.nn.softplus(dt_ref[...] + dtb_ref[...])
    adt = dt * (-jnp.exp(alog_ref[...]))
    ti = lax.broadcasted_iota(jnp.int32, (t, t), 0)
    si = lax.broadcasted_iota(jnp.int32, (t, t), 1)
    causal = ti >= si
    acs = jnp.dot(causal.astype(F32), adt, preferred_element_type=F32,
                  precision=lax.Precision.HIGHEST)
    a_last = acs[t - 1:t, :]
    wq = jnp.exp(a_last - acs) * dt
    eacs = jnp.exp(acs)
    ealast = jnp.exp(a_last)
    acs_t = acs.T
    dt_t = dt.T
    lane = lax.broadcasted_iota(jnp.int32, (t, LANE), 1)
    rowi = lax.broadcasted_iota(jnp.int32, (LANE, N_B), 0)
    hpg = H_B // G_B
    ys = []
    cb = None
    for j in range(H_B // 2):
        g = (2 * j) // hpg
        bg = xbc[:, D_B + g * N_B:D_B + (g + 1) * N_B]
        cg = xbc[:, D_B + G_B * N_B + g * N_B:D_B + G_B * N_B + (g + 1) * N_B]
        if (2 * j) % hpg == 0:
            cb = _dot_tr(cg.astype(BF16), bg.astype(BF16))
        xpair = bxb[:, j * LANE:(j + 1) * LANE]
        sp = s_ref[j]
        spb = sp.astype(BF16)
        y_h, up_h = [], []
        for hh in range(2):
            hd = 2 * j + hh
            seg = jnp.broadcast_to(acs[:, hd:hd + 1], (t, t)) - jnp.broadcast_to(acs_t[hd:hd + 1, :], (t, t))
            lmat = jnp.exp(jnp.where(causal, seg, -1e30))
            mmat = (cb * lmat * jnp.broadcast_to(dt_t[hd:hd + 1, :], (t, t))).astype(BF16)
            ec = (jnp.broadcast_to(eacs[:, hd:hd + 1], (t, N_B)) * cg).astype(BF16)
            y_h.append(_dot(mmat, xpair) + _dot_tr(ec, spb))
            bw = (bg * jnp.broadcast_to(wq[:, hd:hd + 1], (t, N_B))).astype(BF16)
            up_h.append(_dot_tl(xpair, bw))
        ys.append(jnp.where(lane < HD_B, y_h[0], y_h[1]))
        dec = jnp.where(rowi < HD_B,
                        jnp.broadcast_to(ealast[:, 2 * j:2 * j + 1], (LANE, N_B)),
                        jnp.broadcast_to(ealast[:, 2 * j + 1:2 * j + 2], (LANE, N_B)))
        s_ref[j] = dec * sp + jnp.where(rowi < HD_B, up_h[0], up_h[1])
    y = jnp.concatenate(ys, axis=-1)
    b_out = _group_norm_gate(y, bx, z_ref[...], dexp_ref, bnw_ref).astype(BF16)

    hres = hin_ref[...] + _dot(a_out, wo_ref[0:D_A, :]) + _dot(b_out, wo_ref[D_A:D_A + D_B, :])
    hout_ref[...] = _residual_ple(hres, p_ref, wg_ref, wp_ref, nf_ref, final)

    @pl.when(c == last)
    def _():
        ah_ref[0] = h[t - 1:t, :]
        ac_ref[0] = ax[t - (CONV_W - 1):t, :]
        bc_ref[0] = xb[t - (CONV_W - 1):t, :]
        for j in range(H_B // 2):
            sj = s_ref[j]
            bs_ref[0, 2 * j] = sj[0:HD_B, :]
            bs_ref[0, 2 * j + 1] = sj[HD_B:2 * HD_B, :]


def _ab_prompt(u, dtr, h, p, layer, wo, wg, wp, nf, bsz, seq, w, *, final):
    t = SSD_CHUNK
    assert seq % t == 0
    nc = seq // t
    m = bsz * seq
    rows = lambda b, c: b * nc + c
    cvec = lambda b, c: (0, 0)
    c3 = lambda b, c: (0, 0, 0)
    in_specs = [
        pl.BlockSpec((t, D_A), lambda b, c: (rows(b, c), 0)),
        pl.BlockSpec((t, D_A), lambda b, c: (rows(b, c), 1)),
        pl.BlockSpec((t, D_B), lambda b, c: (rows(b, c), 2)),
        pl.BlockSpec((t, CONV_DIM_B), lambda b, c: (rows(b, c), 2)),
        pl.BlockSpec((t, LANE), lambda b, c: (rows(b, c), 0)),
        pl.BlockSpec((CONV_W, D_A), cvec), pl.BlockSpec((1, D_A), cvec),
        pl.BlockSpec((A_BLOCKS, A_BLK, A_BLK), c3), pl.BlockSpec((1, D_A), cvec),
        pl.BlockSpec((A_BLOCKS, A_BLK, A_BLK), c3), pl.BlockSpec((1, D_A), cvec),
        pl.BlockSpec((1, D_A), cvec),
        pl.BlockSpec((CONV_W, CONV_DIM_B), cvec), pl.BlockSpec((1, CONV_DIM_B), cvec),
        pl.BlockSpec((1, LANE), cvec), pl.BlockSpec((1, LANE), cvec),
        pl.BlockSpec((1, D_B), cvec), pl.BlockSpec((1, D_B), cvec),
        pl.BlockSpec((t, D_MODEL), lambda b, c: (rows(b, c), 0)),
        pl.BlockSpec((None, t, p.shape[2]), lambda b, c: (layer, rows(b, c), 0)),
        pl.BlockSpec(wo.shape, cvec), pl.BlockSpec(wg.shape, cvec), pl.BlockSpec(wp.shape, cvec),
        pl.BlockSpec((1, D_MODEL), cvec),
    ]
    out_shape = [
        jax.ShapeDtypeStruct((m, D_MODEL), F32),
        jax.ShapeDtypeStruct((bsz, 1, D_A), F32),
        jax.ShapeDtypeStruct((bsz, CONV_W - 1, D_A), F32),
        jax.ShapeDtypeStruct((bsz, H_B, HD_B, N_B), F32),
        jax.ShapeDtypeStruct((bsz, CONV_W - 1, CONV_DIM_B), F32),
    ]
    out_specs = [
        pl.BlockSpec((t, D_MODEL), lambda b, c: (rows(b, c), 0)),
        pl.BlockSpec((1, 1, D_A), lambda b, c: (b, 0, 0)),
        pl.BlockSpec((1, CONV_W - 1, D_A), lambda b, c: (b, 0, 0)),
        pl.BlockSpec((1, H_B, HD_B, N_B), lambda b, c: (b, 0, 0, 0)),
        pl.BlockSpec((1, CONV_W - 1, CONV_DIM_B), lambda b, c: (b, 0, 0)),
    ]
    hout, ah, ac, bs, bc = pl.pallas_call(
        functools.partial(_ab_prompt_kernel, final=final),
        grid=(bsz, nc), in_specs=in_specs, out_specs=out_specs, out_shape=out_shape,
        scratch_shapes=[pltpu.VMEM((SUBLANE, D_A), F32), pltpu.VMEM((SUBLANE, CONV_DIM_B), F32),
                        pltpu.VMEM((SUBLANE, D_A), F32), pltpu.VMEM((H_B // 2, 2 * HD_B, N_B), F32)],
        compiler_params=_params("parallel", "arbitrary"),
        name="ab_prompt",
    )(u, u, u, u, dtr, w["acw"], w["acb"], w["wr"], w["br"], w["wi"], w["bi"], w["lam"],
      w["bcw"], w["bcb"], w["dtb"], w["alog"], w["dexp"], w["bnw"],
      h, p, wo, wg, wp, nf.reshape(1, D_MODEL))
    return hout, ah.reshape(bsz, D_A), ac, bs, bc


def _ab_sample_rows_kernel(ax_ref, ag_ref, xbc_ref, dt_ref, sah_ref, sac_ref, sbc_ref,
                           acw_ref, acb_ref, wr_ref, br_ref, wi_ref, bi_ref, lam_ref,
                           bcw_ref, bcb_ref, dtb_ref,
                           aout_ref, ah_ref, ac_ref, bc_ref, xact_ref, dts_ref):
    def conv1(x, buf_ref, w_ref, b_ref, nbuf_ref, width):
        y = b_ref[...] + w_ref[CONV_W - 1:CONV_W, :] * x
        for k in range(CONV_W - 1):
            y = y + w_ref[k:k + 1, :] * buf_ref[:, k * width:(k + 1) * width]
        for k in range(CONV_W - 2):
            nbuf_ref[:, k * width:(k + 1) * width] = buf_ref[:, (k + 1) * width:(k + 2) * width]
        nbuf_ref[:, (CONV_W - 2) * width:(CONV_W - 1) * width] = x
        return y

    xc = conv1(ax_ref[...], sac_ref, acw_ref, acb_ref, ac_ref, D_A)
    a, u = _lru_gates(xc, wr_ref, br_ref, wi_ref, bi_ref, lam_ref)
    h = a * sah_ref[...] + u
    ah_ref[...] = h
    aout_ref[...] = h * _silu_half(ag_ref[...])
    xact_ref[...] = _silu_half(conv1(xbc_ref[...], sbc_ref, bcw_ref, bcb_ref, bc_ref, CONV_DIM_B))
    dts_ref[...] = jax.nn.softplus(dt_ref[...] + dtb_ref[...])


def _pad_rows_t(x):
    pad = jnp.zeros((LANE - x.shape[0], x.shape[1]), F32)
    return jnp.concatenate([x, pad], axis=0).T


def _ab_sample_state_kernel(s_ref, xact_ref, dts_ref, z_ref, alog_ref, dexp_ref, bnw_ref,
                            so_ref, bout_ref, y_ref):
    bb = SAMPLE_BB
    xact = xact_ref[...]
    bx = xact[:, 0:D_B]
    dts = dts_ref[...]
    dec_t = _pad_rows_t(jnp.exp(dts * (-jnp.exp(alog_ref[...]))))
    dts_t = _pad_rows_t(dts)
    hpg = H_B // G_B
    for j in range(H_B // 2):
        g = (2 * j) // hpg
        xt = _pad_rows_t(bx[:, j * LANE:(j + 1) * LANE])
        dtp = jnp.concatenate([jnp.broadcast_to(dts_t[2 * j:2 * j + 1, :], (HD_B, LANE)),
                               jnp.broadcast_to(dts_t[2 * j + 1:2 * j + 2, :], (HD_B, LANE))], axis=0)
        xdt = xt * dtp
        for i in range(bb):
            brow = jnp.broadcast_to(xact[i:i + 1, D_B + g * N_B:D_B + (g + 1) * N_B], (2 * HD_B, N_B))
            crow = jnp.broadcast_to(
                xact[i:i + 1, D_B + G_B * N_B + g * N_B:D_B + G_B * N_B + (g + 1) * N_B], (2 * SUBLANE, N_B))
            upd = jnp.broadcast_to(xdt[:, i:i + 1], (2 * HD_B, N_B)) * brow
            news = []
            for hh in range(2):
                hd = 2 * j + hh
                dec = jnp.broadcast_to(dec_t[hd:hd + 1, i:i + 1], (HD_B, N_B))
                sn = dec * s_ref[i, hd] + upd[hh * HD_B:(hh + 1) * HD_B, :]
                so_ref[i, hd] = sn
                news.append(sn)
            spair = jnp.concatenate(news, axis=0).astype(BF16)
            yrow = _dot_tr(crow.astype(BF16), spair)
            y_ref[i:i + 1, j * LANE:(j + 1) * LANE] = yrow[0:1, :]
    bout_ref[...] = _group_norm_gate(y_ref[...], bx, z_ref[...], dexp_ref, bnw_ref)


def _ab_sample(u, dtr, s_ah, s_ac, s_bs, s_bc, w):
    bsz = u.shape[0]
    full = lambda shape: pl.BlockSpec(shape, lambda i: tuple(0 for _ in shape))
    cw = CONV_W - 1
    aout, ah, ac, bc, xact, dts = pl.pallas_call(
        _ab_sample_rows_kernel, grid=(1,),
        in_specs=[pl.BlockSpec((bsz, D_A), lambda i: (0, 0)), pl.BlockSpec((bsz, D_A), lambda i: (0, 1)),
                  pl.BlockSpec((bsz, CONV_DIM_B), lambda i: (0, 2)), full((bsz, LANE)),
                  full((bsz, D_A)), full((bsz, cw * D_A)), full((bsz, cw * CONV_DIM_B)),
                  full((CONV_W, D_A)), full((1, D_A)),
                  full((A_BLOCKS, A_BLK, A_BLK)), full((1, D_A)),
                  full((A_BLOCKS, A_BLK, A_BLK)), full((1, D_A)), full((1, D_A)),
                  full((CONV_W, CONV_DIM_B)), full((1, CONV_DIM_B)), full((1, LANE))],
        out_specs=[full((bsz, D_A)), full((bsz, D_A)), full((bsz, cw * D_A)), full((bsz, cw * CONV_DIM_B)),
                   full((bsz, CONV_DIM_B)), full((bsz, LANE))],
        out_shape=[jax.ShapeDtypeStruct((bsz, D_A), F32), jax.ShapeDtypeStruct((bsz, D_A), F32),
                   jax.ShapeDtypeStruct((bsz, cw * D_A), F32), jax.ShapeDtypeStruct((bsz, cw * CONV_DIM_B), F32),
                   jax.ShapeDtypeStruct((bsz, CONV_DIM_B), F32), jax.ShapeDtypeStruct((bsz, LANE), F32)],
        compiler_params=_params("arbitrary"),
        name="ab_sample_rows",
    )(u, u, u, dtr, s_ah, s_ac.reshape(bsz, cw * D_A), s_bc.reshape(bsz, cw * CONV_DIM_B),
      w["acw"], w["acb"], w["wr"], w["br"], w["wi"], w["bi"], w["lam"], w["bcw"], w["bcb"], w["dtb"])

    bb = SAMPLE_BB
    assert bsz % bb == 0
    cvec = lambda i: (0, 0)
    bs, bout = pl.pallas_call(
        _ab_sample_state_kernel, grid=(bsz // bb,),
        in_specs=[pl.BlockSpec((bb, H_B, HD_B, N_B), lambda i: (i, 0, 0, 0)),
                  pl.BlockSpec((bb, CONV_DIM_B), lambda i: (i, 0)),
                  pl.BlockSpec((bb, LANE), lambda i: (i, 0)),
                  pl.BlockSpec((bb, D_B), lambda i: (i, 2)),
                  pl.BlockSpec((1, LANE), cvec), pl.BlockSpec((1, D_B), cvec), pl.BlockSpec((1, D_B), cvec)],
        out_specs=[pl.BlockSpec((bb, H_B, HD_B, N_B), lambda i: (i, 0, 0, 0)),
                   pl.BlockSpec((bb, D_B), lambda i: (i, 0))],
        out_shape=[jax.ShapeDtypeStruct(s_bs.shape, F32), jax.ShapeDtypeStruct((bsz, D_B), F32)],
        scratch_shapes=[pltpu.VMEM((bb, D_B), F32)],
        compiler_params=_params("parallel"),
        name="ab_sample_state",
    )(s_bs, xact, dts, u, w["alog"], w["dexp"], w["bnw"])
    mix = jnp.concatenate([aout, bout], axis=-1)
    return mix, ah, ac.reshape(bsz, cw, D_A), bs, bc.reshape(bsz, cw, CONV_DIM_B)


def _hg_lower_bound(clb, layer):
    mx = jnp.max(clb, axis=0, keepdims=True)
    ex = jnp.exp(clb - mx)
    return jnp.sum(ex[1:layer + 1], axis=0, keepdims=True) / jnp.sum(ex, axis=0, keepdims=True)


def _hg_gates(fx_half, lb):
    f = 0.5 * (1.0 + lb) + (0.5 * (1.0 - lb)) * jnp.tanh(fx_half)
    return f, 1.0 - f


def _hg_out(o, gate_half, cnw):
    return o * lax.rsqrt(jnp.mean(o * o, axis=-1, keepdims=True) + EPS) * cnw * (gate_half + gate_half * jnp.tanh(gate_half))


def _hg_gamma():
    import numpy as np
    q = HG_CHUNK
    t = np.arange(q)[:, None]
    tau = np.arange(q)[None, :]
    mats = [(tau <= t)]
    for l in range(1, HG_MXU_LEVELS):
        w = 1 << l
        ref = (t // (2 * w)) * (2 * w) + w - 1
        upper = (t % (2 * w)) >= w
        mats.append(np.where(upper, (tau > ref) & (tau <= t), (tau > t) & (tau <= ref)))
    gam = np.concatenate(mats, axis=0).astype(np.float32)
    return jnp.asarray(np.concatenate([gam, gam], axis=1), dtype=BF16)


def _hg_level_table():
    import numpy as np
    q = HG_CHUNK
    t = np.arange(q)[:, None]
    s = np.arange(q)[None, :]
    x = t ^ s
    lvl = np.floor(np.log2(np.maximum(x, 1))).astype(np.int32)
    return jnp.asarray(np.where(t > s, lvl, -1).astype(np.int32))


def _c_prompt_kernel(q_ref, f_ref, v_ref, g_ref, clb_ref, cnw_ref, gam_ref, lvl_ref,
                     og_ref, cs_ref, st_ref, *, layer):
    c = pl.program_id(1)
    last = pl.num_programs(1) - 1
    qc = HG_CHUNK

    @pl.when(c == 0)
    def _():
        st_ref[...] = jnp.zeros_like(st_ref)

    gam = gam_ref[...]
    ntile = qc // SUBLANE
    sub = lax.broadcasted_iota(jnp.int32, (SUBLANE, DK_C), 0)
    sub_levels = HG_MXU_LEVELS
    sub_upper = [(sub & (1 << l)) != 0 for l in range(sub_levels)]

    def tiles(x):
        return [x[i * SUBLANE:(i + 1) * SUBLANE, :] for i in range(ntile)]

    def gate_split(hd, rows):
        f, kk = _hg_gates(f_ref[hd, rows, :], _hg_lower_bound(clb_ref[hd], layer))
        g = jnp.log(f) * LOG2_E
        g1 = g.astype(BF16)
        g2 = (g - g1.astype(F32)).astype(BF16)
        return (f, kk), jnp.concatenate([g1, g2], axis=0)

    def scores(hd, rows, fk, sums):
        f, kk = fk
        qh = q_ref[hd, rows, :] * (DK_C ** -0.5)
        bcum = sums[0:qc]
        st = st_ref[hd]
        o = _dot_tr((qh * jnp.exp2(bcum)).astype(BF16), st.astype(BF16))
        qt, kt, ft, bt = tiles(qh), tiles(kk), tiles(f), tiles(bcum)
        prods = []
        for l in range(HG_LEVELS):
            if l == 0:
                xt = [jnp.where(sub_upper[0], qt[i] * ft[i], kt[i]) for i in range(ntile)]
            elif l < HG_MXU_LEVELS:
                dec = tiles(jnp.exp2(sums[l * qc:(l + 1) * qc]))
                xt = [jnp.where(sub_upper[l], qt[i], kt[i]) * dec[i] for i in range(ntile)]
            else:
                wt = 1 << (l - HG_MXU_LEVELS)
                xt = []
                for blk in range(0, ntile, 2 * wt):
                    ref = (blk + wt) * SUBLANE - 1
                    bref = jnp.broadcast_to(bcum[ref:ref + 1, :], (SUBLANE, DK_C))
                    xt += [kt[i] * jnp.exp2(bref - bt[i]) for i in range(blk, blk + wt)]
                    xt += [qt[i] * jnp.exp2(bt[i] - bref) for i in range(blk + wt, blk + 2 * wt)]
            x = jnp.concatenate(xt, axis=0).astype(BF16)
            half = (1 << l) // BF16_ROWS
            if half == 0:
                p = tiles(_dot_tr(x, x))
                prods.append({i: p[i] for i in range(ntile)})
            else:
                ups = [r for r in range(qc // BF16_ROWS) if (r // half) & 1]
                pu = _dot_tr(jnp.concatenate([x[r * BF16_ROWS:(r + 1) * BF16_ROWS, :] for r in ups], axis=0), x)
                tpr = BF16_ROWS // SUBLANE
                prods.append({r * tpr + k: pu[(n * tpr + k) * SUBLANE:(n * tpr + k + 1) * SUBLANE, :]
                              for n, r in enumerate(ups) for k in range(tpr)})
        return qh, st, o, prods

    def combine(hd, rows, kk, bcum, qh, st, o, prods):
        arows = []
        for i in range(ntile):
            lv = lvl_ref[i * SUBLANE:(i + 1) * SUBLANE, :]
            a = jnp.zeros((SUBLANE, qc), F32)
            for l in range(HG_LEVELS):
                if l < sub_levels or (i >> (l - sub_levels)) & 1:
                    a = jnp.where(lv == l, prods[l][i], a)
            arows.append(a)
        amat = jnp.concatenate(arows, axis=0)
        v = v_ref[hd, rows, :]
        vb = v.astype(BF16)
        o = o + _dot(amat.astype(BF16), vb) + jnp.sum(qh * kk, axis=-1, keepdims=True) * v
        blast = bcum[qc - 1:qc, :]
        kdec = (kk * jnp.exp2(blast - bcum)).astype(BF16)
        st_ref[hd] = st * jnp.exp2(blast) + _dot_tl(vb, kdec)
        return o

    nchunk = HG_BLOCK // qc

    def body(idx, carry):
        hg = idx // nchunk
        rows = pl.ds(pl.multiple_of((idx % nchunk) * qc, qc), qc)
        heads = [hg * HG_UNROLL + k for k in range(HG_UNROLL)]
        gs = [gate_split(hd, rows) for hd in heads]
        sums = _dot(gam, jnp.concatenate([s for _, s in gs], axis=1))
        sums = [sums[:, k * DK_C:(k + 1) * DK_C] for k in range(HG_UNROLL)]
        sc = [scores(hd, rows, gs[k][0], sums[k]) for k, hd in enumerate(heads)]
        outs = [combine(hd, rows, gs[k][0][1], sums[k][0:qc], *sc[k]) for k, hd in enumerate(heads)]
        for k, hd in enumerate(heads):
            og_ref[hd, rows, :] = _hg_out(outs[k], g_ref[hd, rows, :], cnw_ref[hd]).astype(BF16)
        return carry

    lax.fori_loop(0, (H_C // HG_UNROLL) * nchunk, body, 0)

    @pl.when(c == last)
    def _():
        for hd in range(H_C):
            cs_ref[0, hd] = st_ref[hd].T


def _c_prompt(u, bsz, seq, w, layer):
    tb = HG_BLOCK
    assert seq % tb == 0 and tb % HG_CHUNK == 0 and (1 << HG_LEVELS) == HG_CHUNK
    nc = seq // tb
    m = bsz * seq
    depth = w["clb"].shape[1]

    def part(k):
        return pl.BlockSpec((H_C, tb, LANE), lambda b, c: (k, b * nc + c, 0))

    c2 = lambda b, c: (0, 0)
    c3 = lambda b, c: (0, 0, 0)
    og, cs = pl.pallas_call(
        functools.partial(_c_prompt_kernel, layer=layer), grid=(bsz, nc),
        in_specs=[part(0), part(1), part(2), part(3),
                  pl.BlockSpec((H_C, depth, DK_C), c3), pl.BlockSpec((H_C, 1, DV_C), c3),
                  pl.BlockSpec(w["gam"].shape, c2), pl.BlockSpec(w["lvl"].shape, c2)],
        out_specs=[pl.BlockSpec((H_C, tb, LANE), lambda b, c: (0, b * nc + c, 0)),
                   pl.BlockSpec((1, H_C, DK_C, DV_C), lambda b, c: (b, 0, 0, 0))],
        out_shape=[jax.ShapeDtypeStruct((H_C, m, DV_C), BF16),
                   jax.ShapeDtypeStruct((bsz, H_C, DK_C, DV_C), F32)],
        scratch_shapes=[pltpu.VMEM((H_C, DV_C, DK_C), F32)],
        compiler_params=_params("parallel", "arbitrary"),
        name="c_prompt",
    )(u, u, u, u, w["clb"], w["cnw"], w["gam"], w["lvl"])
    return og, cs


def _c_sample_kernel(q_ref, f_ref, v_ref, g_ref, s_ref, clb_ref, cnw_ref, og_ref, so_ref, *, layer):
    bb = SAMPLE_BB
    lane = lax.broadcasted_iota(jnp.int32, (DK_C, LANE), 1)
    first_rows = lax.broadcasted_iota(jnp.int32, (LANE, DV_C), 0) < bb
    for hd in range(H_C):
        lb = _hg_lower_bound(clb_ref[hd], layer)
        f, kk = _hg_gates(f_ref[hd], lb)
        f_t = _pad_rows_t(f)
        k_t = _pad_rows_t(kk)
        qs = q_ref[hd] * (DK_C ** -0.5)
        v = v_ref[hd]
        vpad = jnp.where(first_rows, jnp.tile(v, (LANE // bb, 1)), 0.0).astype(BF16)
        orows = []
        for i in range(bb):
            fcol = jnp.broadcast_to(f_t[:, i:i + 1], (DK_C, DV_C))
            kv = _dot(jnp.where(lane == i, k_t, 0.0).astype(BF16), vpad)
            sn = fcol * s_ref[i, hd] + kv
            so_ref[i, hd] = sn
            qrow = jnp.broadcast_to(qs[i:i + 1, :], (2 * SUBLANE, DK_C)).astype(BF16)
            orows.append(_dot(qrow, sn.astype(BF16))[0:1, :])
        o = jnp.concatenate(orows, axis=0)
        og_ref[hd] = _hg_out(o, g_ref[hd], cnw_ref[hd])


def _c_sample(u, s_c, w, layer):
    bsz = s_c.shape[0]
    bb = SAMPLE_BB
    assert bsz % bb == 0
    depth = w["clb"].shape[1]

    def part(k):
        return pl.BlockSpec((H_C, bb, LANE), lambda i: (k, i, 0))

    c3 = lambda i: (0, 0, 0)
    og, so = pl.pallas_call(
        functools.partial(_c_sample_kernel, layer=layer), grid=(bsz // bb,),
        in_specs=[part(0), part(1), part(2), part(3),
                  pl.BlockSpec((bb, H_C, DK_C, DV_C), lambda i: (i, 0, 0, 0)),
                  pl.BlockSpec((H_C, depth, DK_C), c3), pl.BlockSpec((H_C, 1, DV_C), c3)],
        out_specs=[pl.BlockSpec((H_C, bb, LANE), lambda i: (0, i, 0)),
                   pl.BlockSpec((bb, H_C, DK_C, DV_C), lambda i: (i, 0, 0, 0))],
        out_shape=[jax.ShapeDtypeStruct((H_C, bsz, DV_C), F32), jax.ShapeDtypeStruct(s_c.shape, F32)],
        compiler_params=_params("parallel"),
        name="c_sample",
    )(u, u, u, u, s_c, w["clb"], w["cnw"])
    return og, so


def _row(v, width=None):
    v = v.astype(F32).reshape(1, -1)
    if width is not None and v.shape[1] < width:
        v = jnp.pad(v, ((0, 0), (0, width - v.shape[1])))
    return v


def kernel(x_prompt, x_sample, p_prompt, p_sample, state_a_h, state_a_conv, state_b_ssm, state_b_conv, state_c,
           norm_w, norm_f, ab_w_in, a_conv_w, a_conv_b, a_w_r, a_b_r, a_w_i, a_b_i, a_lam, b_conv_w, b_conv_b,
           b_dt_bias, b_a_log, b_d, b_norm_w, ab_w_out, c_w_in, c_lb, c_norm_w, c_w_out, ple_proj, ple_gate):
    depth = norm_w.shape[0]
    bp, seq, _ = x_prompt.shape
    bs = x_sample.shape[0]
    hp = x_prompt.reshape(bp * seq, D_MODEL)
    hs = x_sample.reshape(bs, D_MODEL)
    pp = p_prompt.reshape(depth, bp * seq, D_PLE)
    ps = p_sample.reshape(depth, bs, D_PLE)
    gam, lvl = _hg_gamma(), _hg_level_table()
    clb = c_lb.astype(F32).reshape(depth, H_C, DK_C).transpose(1, 0, 2)

    ah_p, ac_p, bs_p, bc_p, c_p = [], [], [], [], []
    ah_s, ac_s, bs_s, bc_s, c_s = [], [], [], [], []
    for i in range(depth):
        j = i // 2
        final = i == depth - 1
        wg = (0.5 * ple_gate[i]).astype(BF16)
        wp = (0.5 * ple_proj[i]).astype(BF16)
        if i % 2 == 0:
            col_scale = jnp.concatenate([jnp.ones((D_A,), F32), jnp.full((D_A + D_B,), 0.5, F32),
                                         jnp.ones((CONV_DIM_B + H_B,), F32)])
            w_ab = jnp.pad(ab_w_in[j] * col_scale, ((0, 0), (0, LANE - H_B))).astype(BF16)
            wo = ab_w_out[j].astype(BF16)
            w = dict(acw=a_conv_w[j].astype(F32), acb=_row(a_conv_b[j]),
                     wr=a_w_r[j].astype(BF16), br=_row(a_b_r[j]), wi=a_w_i[j].astype(BF16), bi=_row(a_b_i[j]),
                     lam=_row(a_lam[j]), bcw=0.5 * b_conv_w[j].astype(F32), bcb=0.5 * _row(b_conv_b[j]),
                     dtb=_row(b_dt_bias[j], LANE), alog=_row(b_a_log[j], LANE),
                     dexp=_row(jnp.repeat(b_d[j], HD_B)), bnw=_row(b_norm_w[j]))
            u, dtr = _in_proj(hp, norm_w[i], w_ab, extra_cols=True)
            hp, s1, s2, s3, s4 = _ab_prompt(u, dtr, hp, pp, i, wo, wg, wp, norm_f, bp, seq, w, final=final)
            ah_p.append(s1); ac_p.append(s2); bs_p.append(s3); bc_p.append(s4)
            u, dtr = _in_proj(hs, norm_w[i], w_ab, extra_cols=True)
            mix, s1, s2, s3, s4 = _ab_sample(u, dtr, state_a_h[j], state_a_conv[j], state_b_ssm[j],
                                             state_b_conv[j], w)
            ah_s.append(s1); ac_s.append(s2); bs_s.append(s3); bc_s.append(s4)
            hs = _out_proj(mix, hs, ps, i, wo, wg, wp, norm_f, head_major=False, final=final)
        else:
            col_scale = jnp.concatenate([jnp.ones((HK_C,), F32), jnp.full((HK_C,), 0.5, F32),
                                         jnp.ones((D_C,), F32), jnp.full((D_C,), 0.5, F32)])
            w_in = (c_w_in[j] * col_scale).astype(BF16)
            wo = c_w_out[j].astype(BF16)
            w = dict(clb=clb, cnw=c_norm_w[j].astype(F32).reshape(H_C, 1, DV_C), gam=gam, lvl=lvl)
            u = _in_proj(hp, norm_w[i], w_in, head_major=True)
            og, s1 = _c_prompt(u, bp, seq, w, i)
            c_p.append(s1)
            hp = _out_proj(og, hp, pp, i, wo, wg, wp, norm_f, head_major=True, final=final)
            u = _in_proj(hs, norm_w[i], w_in, head_major=True)
            og, s1 = _c_sample(u, state_c[j], w, i)
            c_s.append(s1)
            hs = _out_proj(og, hs, ps, i, wo, wg, wp, norm_f, head_major=True, final=final)
    return (hp.reshape(bp, seq, D_MODEL), hs.reshape(bs, 1, D_MODEL),
            jnp.stack(ah_p), jnp.stack(ac_p), jnp.stack(bs_p), jnp.stack(bc_p), jnp.stack(c_p),
            jnp.stack(ah_s), jnp.stack(ac_s), jnp.stack(bs_s), jnp.stack(bc_s), jnp.stack(c_s))
```

```python
import functools

import jax
import jax.numpy as jnp
from jax import lax
from jax.experimental import pallas as pl
from jax.experimental.pallas import tpu as pltpu

F32 = jnp.float32
BF16 = jnp.bfloat16

D_MODEL = 1024
D_PLE = 256
EPS = 1e-6
CONV_W = 4
D_A = D_MODEL
A_BLOCKS = 8
A_BLK = D_A // A_BLOCKS
LRU_C = 8.0
D_B = D_MODEL
HD_B = 64
H_B = D_B // HD_B
N_B = 128
G_B = 2
CONV_DIM_B = D_B + 2 * G_B * N_B
D_C = 2 * D_MODEL
H_C = 16
DK_C = 128
DV_C = D_C // H_C
HK_C = H_C * DK_C
AB_MAIN = 2 * D_A + D_B + CONV_DIM_B
IN_C = 2 * HK_C + 2 * D_C

LANE = 128
SUBLANE = 8
BF16_ROWS = 16
LOG2_E = 1.4426950408889634
VMEM_LIMIT = 52 * 1024 * 1024

PROJ_TM = 1024
PROJ_MAX_TN = 2048
SSD_CHUNK = 128
AB_CHUNKS = 4
HG_CHUNK = 64
HG_LEVELS = 6
HG_MXU_LEVELS = 3
HG_BLOCK = 512
HG_UNROLL = 16
SAMPLE_BB = 8

_DN_TR = (((1,), (1,)), ((), ()))
_DN_TL = (((0,), (0,)), ((), ()))


def _dot(a, b):
    return jnp.dot(a, b, preferred_element_type=F32)


def _dot_tr(a, b):
    return lax.dot_general(a, b, _DN_TR, preferred_element_type=F32)


def _dot_tl(a, b):
    return lax.dot_general(a, b, _DN_TL, preferred_element_type=F32)


def _silu_half(x_half):
    return x_half + x_half * jnp.tanh(x_half)


def _rmsnorm(x, w):
    return x * lax.rsqrt(jnp.mean(x * x, axis=-1, keepdims=True) + EPS) * w


def _params(*sem):
    return pltpu.CompilerParams(dimension_semantics=sem, vmem_limit_bytes=VMEM_LIMIT)


def _in_proj_kernel(x_ref, xnext_ref, nw_ref, w_ref, *rest, has_extra, head_major, w_t, rows_per):
    if has_extra:
        wx_ref, o_ref, ox_ref, xn_ref = rest
    else:
        o_ref, xn_ref = rest
    dot = _dot_tr if w_t else _dot
    i, j = pl.program_id(0), pl.program_id(1)
    slot = i % 2
    tm = x_ref.shape[0]

    @pl.when(jnp.logical_and(i == 0, j == 0))
    def _():
        xn_ref[0] = _rmsnorm(x_ref[...], nw_ref[...]).astype(BF16)

    start = pl.multiple_of(jnp.minimum(j * rows_per, tm - rows_per), BF16_ROWS)
    share = pl.ds(start, rows_per)
    xn_ref[1 - slot, share, :] = _rmsnorm(xnext_ref[share, :], nw_ref[...]).astype(BF16)

    xn = xn_ref[slot]
    if has_extra:
        @pl.when(j == 0)
        def _():
            ox_ref[...] = dot(xn, wx_ref[...])

    acc = dot(xn, w_ref[...])
    if head_major:
        for k in range(acc.shape[1] // LANE):
            o_ref[k] = acc[:, k * LANE:(k + 1) * LANE]
    else:
        o_ref[...] = acc


def _in_proj(x, nw, w, *, extra_cols=False, head_major=False, w_t=False):
    m, k = x.shape
    n = w.shape[0 if w_t else 1] - (LANE if extra_cols else 0)
    tm = min(m, PROJ_TM)
    tn = next(c for c in range(PROJ_MAX_TN, 0, -LANE) if n % c == 0)
    assert m % tm == 0 and n % tn == 0 and tn % LANE == 0 and tm % BF16_ROWS == 0
    ni, nj = m // tm, n // tn
    rows_per = min(tm, -(-tm // (nj * BF16_ROWS)) * BF16_ROWS)
    wspec = (lambda cols, idx: pl.BlockSpec((cols, k), lambda i, j: (idx(j), 0))) if w_t else \
            (lambda cols, idx: pl.BlockSpec((k, cols), lambda i, j: (0, idx(j))))
    in_specs = [pl.BlockSpec((tm, k), lambda i, j: (i, 0)),
                pl.BlockSpec((tm, k), lambda i, j: (jnp.minimum(i + 1, ni - 1), 0)),
                pl.BlockSpec((1, k), lambda i, j: (0, 0)),
                wspec(tn, lambda j: j)]
    args = [x, x, nw.reshape(1, k), w]
    if head_major:
        out_shape = [jax.ShapeDtypeStruct((n // LANE, m, LANE), F32)]
        out_specs = [pl.BlockSpec((tn // LANE, tm, LANE), lambda i, j: (j, i, 0))]
    else:
        out_shape = [jax.ShapeDtypeStruct((m, n), F32)]
        out_specs = [pl.BlockSpec((tm, tn), lambda i, j: (i, j))]
    if extra_cols:
        in_specs.append(wspec(LANE, lambda j: n // LANE))
        args.append(w)
        out_shape.append(jax.ShapeDtypeStruct((m, LANE), F32))
        out_specs.append(pl.BlockSpec((tm, LANE), lambda i, j: (i, 0)))
    outs = pl.pallas_call(
        functools.partial(_in_proj_kernel, has_extra=extra_cols, head_major=head_major, w_t=w_t, rows_per=rows_per),
        grid=(ni, nj), in_specs=in_specs, out_specs=out_specs, out_shape=out_shape,
        scratch_shapes=[pltpu.VMEM((2, tm, k), BF16)],
        compiler_params=_params("arbitrary", "arbitrary"),
        name="in_proj",
    )(*args)
    return outs if extra_cols else outs[0]


def _residual_ple(h, p_ref, wg_ref, wp_ref, nf_ref, final):
    gate_t = jnp.tanh(_dot(h.astype(BF16), wg_ref[...]))
    pe_half = _dot(p_ref[...].astype(BF16), wp_ref[...])
    h = h + pe_half + pe_half * gate_t
    return _rmsnorm(h, nf_ref[...]) if final else h


def _out_proj_kernel(mix_ref, h_ref, p_ref, wo_ref, wg_ref, wp_ref, nf_ref, o_ref, *, head_major, final):
    if head_major:
        mix = jnp.concatenate([mix_ref[k] for k in range(mix_ref.shape[0])], axis=-1)
    else:
        mix = mix_ref[...]
    h = h_ref[...] + _dot(mix.astype(BF16), wo_ref[...])
    o_ref[...] = _residual_ple(h, p_ref, wg_ref, wp_ref, nf_ref, final)


def _out_proj(mix, h, p, layer, wo, wg, wp, nf, *, head_major, final):
    m, d = h.shape
    tm = min(m, 512)
    assert m % tm == 0
    if head_major:
        mix_spec = pl.BlockSpec((mix.shape[0], tm, LANE), lambda i: (0, i, 0))
    else:
        mix_spec = pl.BlockSpec((tm, mix.shape[1]), lambda i: (i, 0))
    const = lambda i: (0, 0)
    return pl.pallas_call(
        functools.partial(_out_proj_kernel, head_major=head_major, final=final),
        grid=(m // tm,),
        in_specs=[mix_spec,
                  pl.BlockSpec((tm, d), lambda i: (i, 0)),
                  pl.BlockSpec((None, tm, p.shape[2]), lambda i: (layer, i, 0)),
                  pl.BlockSpec(wo.shape, const), pl.BlockSpec(wg.shape, const),
                  pl.BlockSpec(wp.shape, const), pl.BlockSpec((1, d), const)],
        out_specs=pl.BlockSpec((tm, d), lambda i: (i, 0)),
        out_shape=jax.ShapeDtypeStruct((m, d), F32),
        compiler_params=_params("parallel"),
        name="out_proj",
    )(mix, h, p, wo, wg, wp, nf.reshape(1, d))


def _lru_gates(xc, wr_ref, br_ref, wi_ref, bi_ref, lam_ref):
    xcb = xc.astype(BF16)
    r_parts, i_parts = [], []
    for k in range(A_BLOCKS):
        xk = xcb[:, k * A_BLK:(k + 1) * A_BLK]
        r_parts.append(_dot(xk, wr_ref[k]))
        i_parts.append(_dot(xk, wi_ref[k]))
    r = jax.nn.sigmoid(jnp.concatenate(r_parts, axis=-1) + br_ref[...])
    gi = jax.nn.sigmoid(jnp.concatenate(i_parts, axis=-1) + bi_ref[...])
    log_a = (-LRU_C) * r * jax.nn.softplus(-lam_ref[...])
    a = jnp.exp(log_a)
    v = jnp.tanh(-log_a) * (a * a + 1.0)
    u = jnp.where(v > 0.0, v * lax.rsqrt(v), 0.0) * (gi * xc)
    return a, u


def _group_norm_gate(y, bx, z_half, dexp_ref, bnw_ref):
    y = (y + dexp_ref[...] * bx) * _silu_half(z_half)
    gw = D_B // G_B
    parts = []
    for g in range(G_B):
        yg = y[:, g * gw:(g + 1) * gw]
        parts.append(yg * lax.rsqrt(jnp.mean(yg * yg, axis=-1, keepdims=True) + EPS))
    return jnp.concatenate(parts, axis=-1) * bnw_ref[...]


def _ab_prompt_kernel(ax_ref, ag_ref, z_ref, xbc_ref, dt_ref,
                      acw_ref, acb_ref, wr_ref, br_ref, wi_ref, bi_ref, lam_ref,
                      bcw_ref, bcb_ref, dtb_ref, alog_ref, dexp_ref, bnw_ref,
                      hin_ref, p_ref, wo_ref, wg_ref, wp_ref, nf_ref,
                      hout_ref, *state_refs, final):
    for k in range(AB_CHUNKS):
        r = pl.ds(k * SSD_CHUNK, SSD_CHUNK)
        _ab_prompt_chunk(ax_ref.at[r], ag_ref.at[r], z_ref.at[r], xbc_ref.at[r], dt_ref.at[r],
                         acw_ref, acb_ref, wr_ref, br_ref, wi_ref, bi_ref, lam_ref,
                         bcw_ref, bcb_ref, dtb_ref, alog_ref, dexp_ref, bnw_ref,
                         hin_ref.at[r], p_ref.at[r], wo_ref, wg_ref, wp_ref, nf_ref,
                         hout_ref.at[r], *state_refs, final=final, first=k == 0, last=k == AB_CHUNKS - 1)


def _ab_prompt_chunk(ax_ref, ag_ref, z_ref, xbc_ref, dt_ref,
                     acw_ref, acb_ref, wr_ref, br_ref, wi_ref, bi_ref, lam_ref,
                     bcw_ref, bcb_ref, dtb_ref, alog_ref, dexp_ref, bnw_ref,
                     hin_ref, p_ref, wo_ref, wg_ref, wp_ref, nf_ref,
                     hout_ref, ah_ref, ac_ref, bs_ref, bc_ref,
                     xpa_ref, xpb_ref, h_ref, s_ref, *, final, first, last):
    c = pl.program_id(1)
    t = SSD_CHUNK
    ntile = t // SUBLANE

    if first:
        @pl.when(c == 0)
        def _():
            xpa_ref[...] = jnp.zeros_like(xpa_ref)
            xpb_ref[...] = jnp.zeros_like(xpb_ref)
            h_ref[...] = jnp.zeros_like(h_ref)
            s_ref[...] = jnp.zeros_like(s_ref)

    def tiles(x):
        return [x[i * SUBLANE:(i + 1) * SUBLANE, :] for i in range(ntile)]

    def conv(x, tail_ref, w_ref, b_ref):
        sub = lax.broadcasted_iota(jnp.int32, (SUBLANE, x.shape[1]), 0)
        xt = [tail_ref[...]] + tiles(x)
        taps = [jnp.broadcast_to(w_ref[k:k + 1, :], (SUBLANE, x.shape[1])) for k in range(CONV_W)]
        bias = jnp.broadcast_to(b_ref[...], (SUBLANE, x.shape[1]))
        acc = [bias + taps[CONV_W - 1] * xt[i + 1] for i in range(ntile)]
        for s in range(1, CONV_W):
            wk = taps[CONV_W - 1 - s]
            for i in range(ntile):
                merged = jnp.where(sub >= SUBLANE - s, xt[i], xt[i + 1])
                acc[i] = acc[i] + wk * pltpu.roll(merged, s, 0)
        tail_ref[...] = xt[ntile]
        return jnp.concatenate(acc, axis=0)

    ax = ax_ref[...]
    xc = conv(ax, xpa_ref, acw_ref, acb_ref)
    a, u = _lru_gates(xc, wr_ref, br_ref, wi_ref, bi_ref, lam_ref)
    sub = lax.broadcasted_iota(jnp.int32, (SUBLANE, D_A), 0)
    at, ut = tiles(a), tiles(u)
    step = 1
    while step < SUBLANE:
        m = sub >= step
        for i in range(ntile):
            ut[i] = jnp.where(m, at[i] * pltpu.roll(ut[i], step, 0) + ut[i], ut[i])
            at[i] = jnp.where(m, at[i] * pltpu.roll(at[i], step, 0), at[i])
        step *= 2
    carry = h_ref[0:1, :]
    hs = []
    for i in range(ntile):
        hs.append(ut[i] + at[i] * carry)
        carry = hs[i][SUBLANE - 1:SUBLANE, :]
    h = jnp.concatenate(hs, axis=0)
    h_ref[0:1, :] = carry
    a_out = (h * _silu_half(ag_ref[...])).astype(BF16)

    xb = xbc_ref[...]
    xbc = _silu_half(conv(xb, xpb_ref, bcw_ref, bcb_ref))
    bx = xbc[:, 0:D_B]
    bxb = bx.astype(BF16)
    dt = jax.nn.softplus(dt_ref[...] + dtb_ref[...])
    adt = dt * (-jnp.exp(alog_ref[...]))
    ti = lax.broadcasted_iota(jnp.int32, (t, t), 0)
    si = lax.broadcasted_iota(jnp.int32, (t, t), 1)
    causal = ti >= si
    acs = jnp.dot(causal.astype(F32), adt, preferred_element_type=F32,
                  precision=lax.Precision.HIGHEST)
    a_last = acs[t - 1:t, :]
    wq = jnp.exp(a_last - acs) * dt
    eacs = jnp.exp(acs)
    ealast = jnp.exp(a_last)
    acs_t = acs.T
    dt_t = dt.T
    lane = lax.broadcasted_iota(jnp.int32, (t, LANE), 1)
    rowi = lax.broadcasted_iota(jnp.int32, (LANE, N_B), 0)
    hpg = H_B // G_B
    ys = []
    cb = None
    for j in range(H_B // 2):
        g = (2 * j) // hpg
        bg = xbc[:, D_B + g * N_B:D_B + (g + 1) * N_B]
        cg = xbc[:, D_B + G_B * N_B + g * N_B:D_B + G_B * N_B + (g + 1) * N_B]
        if (2 * j) % hpg == 0:
            cb = _dot_tr(cg.astype(BF16), bg.astype(BF16))
        xpair = bxb[:, j * LANE:(j + 1) * LANE]
        sp = s_ref[j]
        spb = sp.astype(BF16)
        y_h, up_h = [], []
        for hh in range(2):
            hd = 2 * j + hh
            seg = jnp.broadcast_to(acs[:, hd:hd + 1], (t, t)) - jnp.broadcast_to(acs_t[hd:hd + 1, :], (t, t))
            lmat = jnp.exp(jnp.where(causal, seg, -1e30))
            mmat = (cb * lmat * jnp.broadcast_to(dt_t[hd:hd + 1, :], (t, t))).astype(BF16)
            ec = (jnp.broadcast_to(eacs[:, hd:hd + 1], (t, N_B)) * cg).astype(BF16)
            y_h.append(_dot(mmat, xpair) + _dot_tr(ec, spb))
            bw = (bg * jnp.broadcast_to(wq[:, hd:hd + 1], (t, N_B))).astype(BF16)
            up_h.append(_dot_tl(xpair, bw))
        ys.append(jnp.where(lane < HD_B, y_h[0], y_h[1]))
        dec = jnp.where(rowi < HD_B,
                        jnp.broadcast_to(ealast[:, 2 * j:2 * j + 1], (LANE, N_B)),
                        jnp.broadcast_to(ealast[:, 2 * j + 1:2 * j + 2], (LANE, N_B)))
        s_ref[j] = dec * sp + jnp.where(rowi < HD_B, up_h[0], up_h[1])
    y = jnp.concatenate(ys, axis=-1)
    b_out = _group_norm_gate(y, bx, z_ref[...], dexp_ref, bnw_ref).astype(BF16)

    hres = hin_ref[...] + _dot(a_out, wo_ref[0:D_A, :]) + _dot(b_out, wo_ref[D_A:D_A + D_B, :])
    hout_ref[...] = _residual_ple(hres, p_ref, wg_ref, wp_ref, nf_ref, final)

    if last:
        @pl.when(c == pl.num_programs(1) - 1)
        def _():
            ah_ref[0] = h[t - 1:t, :]
            ac_ref[0] = ax[t - (CONV_W - 1):t, :]
            bc_ref[0] = xb[t - (CONV_W - 1):t, :]
            for j in range(H_B // 2):
                sj = s_ref[j]
                bs_ref[0, 2 * j] = sj[0:HD_B, :]
                bs_ref[0, 2 * j + 1] = sj[HD_B:2 * HD_B, :]


def _ab_prompt(u, dtr, h, p, layer, wo, wg, wp, nf, bsz, seq, w, *, final):
    t = AB_CHUNKS * SSD_CHUNK
    assert seq % t == 0
    nc = seq // t
    m = bsz * seq
    rows = lambda b, c: b * nc + c
    cvec = lambda b, c: (0, 0)
    c3 = lambda b, c: (0, 0, 0)
    in_specs = [
        pl.BlockSpec((t, D_A), lambda b, c: (rows(b, c), 0)),
        pl.BlockSpec((t, D_A), lambda b, c: (rows(b, c), 1)),
        pl.BlockSpec((t, D_B), lambda b, c: (rows(b, c), 2)),
        pl.BlockSpec((t, CONV_DIM_B), lambda b, c: (rows(b, c), 2)),
        pl.BlockSpec((t, LANE), lambda b, c: (rows(b, c), 0)),
        pl.BlockSpec((CONV_W, D_A), cvec), pl.BlockSpec((1, D_A), cvec),
        pl.BlockSpec((A_BLOCKS, A_BLK, A_BLK), c3), pl.BlockSpec((1, D_A), cvec),
        pl.BlockSpec((A_BLOCKS, A_BLK, A_BLK), c3), pl.BlockSpec((1, D_A), cvec),
        pl.BlockSpec((1, D_A), cvec),
        pl.BlockSpec((CONV_W, CONV_DIM_B), cvec), pl.BlockSpec((1, CONV_DIM_B), cvec),
        pl.BlockSpec((1, LANE), cvec), pl.BlockSpec((1, LANE), cvec),
        pl.BlockSpec((1, D_B), cvec), pl.BlockSpec((1, D_B), cvec),
        pl.BlockSpec((t, D_MODEL), lambda b, c: (rows(b, c), 0)),
        pl.BlockSpec((None, t, p.shape[2]), lambda b, c: (layer, rows(b, c), 0)),
        pl.BlockSpec(wo.shape, cvec), pl.BlockSpec(wg.shape, cvec), pl.BlockSpec(wp.shape, cvec),
        pl.BlockSpec((1, D_MODEL), cvec),
    ]
    out_shape = [
        jax.ShapeDtypeStruct((m, D_MODEL), F32),
        jax.ShapeDtypeStruct((bsz, 1, D_A), F32),
        jax.ShapeDtypeStruct((bsz, CONV_W - 1, D_A), F32),
        jax.ShapeDtypeStruct((bsz, H_B, HD_B, N_B), F32),
        jax.ShapeDtypeStruct((bsz, CONV_W - 1, CONV_DIM_B), F32),
    ]
    out_specs = [
        pl.BlockSpec((t, D_MODEL), lambda b, c: (rows(b, c), 0)),
        pl.BlockSpec((1, 1, D_A), lambda b, c: (b, 0, 0)),
        pl.BlockSpec((1, CONV_W - 1, D_A), lambda b, c: (b, 0, 0)),
        pl.BlockSpec((1, H_B, HD_B, N_B), lambda b, c: (b, 0, 0, 0)),
        pl.BlockSpec((1, CONV_W - 1, CONV_DIM_B), lambda b, c: (b, 0, 0)),
    ]
    hout, ah, ac, bs, bc = pl.pallas_call(
        functools.partial(_ab_prompt_kernel, final=final),
        grid=(bsz, nc), in_specs=in_specs, out_specs=out_specs, out_shape=out_shape,
        scratch_shapes=[pltpu.VMEM((SUBLANE, D_A), F32), pltpu.VMEM((SUBLANE, CONV_DIM_B), F32),
                        pltpu.VMEM((SUBLANE, D_A), F32), pltpu.VMEM((H_B // 2, 2 * HD_B, N_B), F32)],
        compiler_params=_params("parallel", "arbitrary"),
        name="ab_prompt",
    )(u, u, u, u, dtr, w["acw"], w["acb"], w["wr"], w["br"], w["wi"], w["bi"], w["lam"],
      w["bcw"], w["bcb"], w["dtb"], w["alog"], w["dexp"], w["bnw"],
      h, p, wo, wg, wp, nf.reshape(1, D_MODEL))
    return hout, ah.reshape(bsz, D_A), ac, bs, bc


def _ab_sample_rows_kernel(ax_ref, ag_ref, xbc_ref, dt_ref, sah_ref, sac_ref, sbc_ref,
                           acw_ref, acb_ref, wr_ref, br_ref, wi_ref, bi_ref, lam_ref,
                           bcw_ref, bcb_ref, dtb_ref,
                           aout_ref, ah_ref, ac_ref, bc_ref, xact_ref, dts_ref):
    def conv1(x, buf_ref, w_ref, b_ref, nbuf_ref, width):
        y = b_ref[...] + w_ref[CONV_W - 1:CONV_W, :] * x
        for k in range(CONV_W - 1):
            y = y + w_ref[k:k + 1, :] * buf_ref[:, k * width:(k + 1) * width]
        for k in range(CONV_W - 2):
            nbuf_ref[:, k * width:(k + 1) * width] = buf_ref[:, (k + 1) * width:(k + 2) * width]
        nbuf_ref[:, (CONV_W - 2) * width:(CONV_W - 1) * width] = x
        return y

    xc = conv1(ax_ref[...], sac_ref, acw_ref, acb_ref, ac_ref, D_A)
    a, u = _lru_gates(xc, wr_ref, br_ref, wi_ref, bi_ref, lam_ref)
    h = a * sah_ref[...] + u
    ah_ref[...] = h
    aout_ref[...] = h * _silu_half(ag_ref[...])
    xact_ref[...] = _silu_half(conv1(xbc_ref[...], sbc_ref, bcw_ref, bcb_ref, bc_ref, CONV_DIM_B))
    dts_ref[...] = jax.nn.softplus(dt_ref[...] + dtb_ref[...])


def _pad_rows_t(x):
    pad = jnp.zeros((LANE - x.shape[0], x.shape[1]), F32)
    return jnp.concatenate([x, pad], axis=0).T


def _ab_sample_state_kernel(s_ref, xact_ref, dts_ref, z_ref, alog_ref, dexp_ref, bnw_ref,
                            so_ref, bout_ref, y_ref):
    bb = SAMPLE_BB
    xact = xact_ref[...]
    bx = xact[:, 0:D_B]
    dts = dts_ref[...]
    dec_t = _pad_rows_t(jnp.exp(dts * (-jnp.exp(alog_ref[...]))))
    dts_t = _pad_rows_t(dts)
    hpg = H_B // G_B
    for j in range(H_B // 2):
        g = (2 * j) // hpg
        xt = _pad_rows_t(bx[:, j * LANE:(j + 1) * LANE])
        dtp = jnp.concatenate([jnp.broadcast_to(dts_t[2 * j:2 * j + 1, :], (HD_B, LANE)),
                               jnp.broadcast_to(dts_t[2 * j + 1:2 * j + 2, :], (HD_B, LANE))], axis=0)
        xdt = xt * dtp
        for i in range(bb):
            brow = jnp.broadcast_to(xact[i:i + 1, D_B + g * N_B:D_B + (g + 1) * N_B], (2 * HD_B, N_B))
            crow = jnp.broadcast_to(
                xact[i:i + 1, D_B + G_B * N_B + g * N_B:D_B + G_B * N_B + (g + 1) * N_B], (2 * SUBLANE, N_B))
            upd = jnp.broadcast_to(xdt[:, i:i + 1], (2 * HD_B, N_B)) * brow
            news = []
            for hh in range(2):
                hd = 2 * j + hh
                dec = jnp.broadcast_to(dec_t[hd:hd + 1, i:i + 1], (HD_B, N_B))
                sn = dec * s_ref[i, hd] + upd[hh * HD_B:(hh + 1) * HD_B, :]
                so_ref[i, hd] = sn
                news.append(sn)
            spair = jnp.concatenate(news, axis=0).astype(BF16)
            yrow = _dot_tr(crow.astype(BF16), spair)
            y_ref[i:i + 1, j * LANE:(j + 1) * LANE] = yrow[0:1, :]
    bout_ref[...] = _group_norm_gate(y_ref[...], bx, z_ref[...], dexp_ref, bnw_ref)


def _ab_sample(u, dtr, s_ah, s_ac, s_bs, s_bc, w):
    bsz = u.shape[0]
    full = lambda shape: pl.BlockSpec(shape, lambda i: tuple(0 for _ in shape))
    cw = CONV_W - 1
    aout, ah, ac, bc, xact, dts = pl.pallas_call(
        _ab_sample_rows_kernel, grid=(1,),
        in_specs=[pl.BlockSpec((bsz, D_A), lambda i: (0, 0)), pl.BlockSpec((bsz, D_A), lambda i: (0, 1)),
                  pl.BlockSpec((bsz, CONV_DIM_B), lambda i: (0, 2)), full((bsz, LANE)),
                  full((bsz, D_A)), full((bsz, cw * D_A)), full((bsz, cw * CONV_DIM_B)),
                  full((CONV_W, D_A)), full((1, D_A)),
                  full((A_BLOCKS, A_BLK, A_BLK)), full((1, D_A)),
                  full((A_BLOCKS, A_BLK, A_BLK)), full((1, D_A)), full((1, D_A)),
                  full((CONV_W, CONV_DIM_B)), full((1, CONV_DIM_B)), full((1, LANE))],
        out_specs=[full((bsz, D_A)), full((bsz, D_A)), full((bsz, cw * D_A)), full((bsz, cw * CONV_DIM_B)),
                   full((bsz, CONV_DIM_B)), full((bsz, LANE))],
        out_shape=[jax.ShapeDtypeStruct((bsz, D_A), F32), jax.ShapeDtypeStruct((bsz, D_A), F32),
                   jax.ShapeDtypeStruct((bsz, cw * D_A), F32), jax.ShapeDtypeStruct((bsz, cw * CONV_DIM_B), F32),
                   jax.ShapeDtypeStruct((bsz, CONV_DIM_B), F32), jax.ShapeDtypeStruct((bsz, LANE), F32)],
        compiler_params=_params("arbitrary"),
        name="ab_sample_rows",
    )(u, u, u, dtr, s_ah, s_ac.reshape(bsz, cw * D_A), s_bc.reshape(bsz, cw * CONV_DIM_B),
      w["acw"], w["acb"], w["wr"], w["br"], w["wi"], w["bi"], w["lam"], w["bcw"], w["bcb"], w["dtb"])

    bb = SAMPLE_BB
    assert bsz % bb == 0
    cvec = lambda i: (0, 0)
    bs, bout = pl.pallas_call(
        _ab_sample_state_kernel, grid=(bsz // bb,),
        in_specs=[pl.BlockSpec((bb, H_B, HD_B, N_B), lambda i: (i, 0, 0, 0)),
                  pl.BlockSpec((bb, CONV_DIM_B), lambda i: (i, 0)),
                  pl.BlockSpec((bb, LANE), lambda i: (i, 0)),
                  pl.BlockSpec((bb, D_B), lambda i: (i, 2)),
                  pl.BlockSpec((1, LANE), cvec), pl.BlockSpec((1, D_B), cvec), pl.BlockSpec((1, D_B), cvec)],
        out_specs=[pl.BlockSpec((bb, H_B, HD_B, N_B), lambda i: (i, 0, 0, 0)),
                   pl.BlockSpec((bb, D_B), lambda i: (i, 0))],
        out_shape=[jax.ShapeDtypeStruct(s_bs.shape, F32), jax.ShapeDtypeStruct((bsz, D_B), F32)],
        scratch_shapes=[pltpu.VMEM((bb, D_B), F32)],
        compiler_params=_params("parallel"),
        name="ab_sample_state",
    )(s_bs, xact, dts, u, w["alog"], w["dexp"], w["bnw"])
    mix = jnp.concatenate([aout, bout], axis=-1)
    return mix, ah, ac.reshape(bsz, cw, D_A), bs, bc.reshape(bsz, cw, CONV_DIM_B)


def _hg_lower_bound(clb, layer):
    mx = jnp.max(clb, axis=0, keepdims=True)
    ex = jnp.exp(clb - mx)
    return jnp.sum(ex[1:layer + 1], axis=0, keepdims=True) / jnp.sum(ex, axis=0, keepdims=True)


def _hg_gates(fx_half, lb):
    f = 0.5 * (1.0 + lb) + (0.5 * (1.0 - lb)) * jnp.tanh(fx_half)
    return f, 1.0 - f


def _hg_out(o, gate_half, cnw):
    return o * lax.rsqrt(jnp.mean(o * o, axis=-1, keepdims=True) + EPS) * cnw * (gate_half + gate_half * jnp.tanh(gate_half))


def _hg_gamma():
    import numpy as np
    q = HG_CHUNK
    t = np.arange(q)[:, None]
    tau = np.arange(q)[None, :]
    mats = [(tau <= t)]
    for l in range(1, HG_MXU_LEVELS):
        w = 1 << l
        ref = (t // (2 * w)) * (2 * w) + w - 1
        upper = (t % (2 * w)) >= w
        mats.append(np.where(upper, (tau > ref) & (tau <= t), (tau > t) & (tau <= ref)))
    gam = np.concatenate(mats, axis=0).astype(np.float32)
    return jnp.asarray(np.concatenate([gam, gam], axis=1), dtype=BF16)


def _hg_level_table():
    import numpy as np
    q = HG_CHUNK
    t = np.arange(q)[:, None]
    s = np.arange(q)[None, :]
    x = t ^ s
    lvl = np.floor(np.log2(np.maximum(x, 1))).astype(np.int32)
    return jnp.asarray(np.where(t > s, lvl, -1).astype(np.int32))


def _c_prompt_kernel(q_ref, f_ref, v_ref, g_ref, clb_ref, cnw_ref, gam_ref, lvl_ref,
                     og_ref, cs_ref, st_ref, *, layer):
    c = pl.program_id(1)
    last = pl.num_programs(1) - 1
    qc = HG_CHUNK

    @pl.when(c == 0)
    def _():
        st_ref[...] = jnp.zeros_like(st_ref)

    gam = gam_ref[...]
    ntile = qc // SUBLANE
    sub = lax.broadcasted_iota(jnp.int32, (SUBLANE, DK_C), 0)
    sub_levels = HG_MXU_LEVELS
    sub_upper = [(sub & (1 << l)) != 0 for l in range(sub_levels)]

    def tiles(x):
        return [x[i * SUBLANE:(i + 1) * SUBLANE, :] for i in range(ntile)]

    def gate_split(hd, rows):
        f, kk = _hg_gates(f_ref[hd, rows, :], _hg_lower_bound(clb_ref[hd], layer))
        g = jnp.log(f) * LOG2_E
        g1 = g.astype(BF16)
        g2 = (g - g1.astype(F32)).astype(BF16)
        return (f, kk), jnp.concatenate([g1, g2], axis=0)

    def scores(hd, rows, fk, sums):
        f, kk = fk
        qh = q_ref[hd, rows, :] * (DK_C ** -0.5)
        bcum = sums[0:qc]
        st = st_ref[hd]
        o = _dot_tr((qh * jnp.exp2(bcum)).astype(BF16), st.astype(BF16))
        qt, kt, ft, bt = tiles(qh), tiles(kk), tiles(f), tiles(bcum)
        prods = []
        for l in range(HG_LEVELS):
            if l == 0:
                xt = [jnp.where(sub_upper[0], qt[i] * ft[i], kt[i]) for i in range(ntile)]
            elif l < HG_MXU_LEVELS:
                dec = tiles(jnp.exp2(sums[l * qc:(l + 1) * qc]))
                xt = [jnp.where(sub_upper[l], qt[i], kt[i]) * dec[i] for i in range(ntile)]
            else:
                wt = 1 << (l - HG_MXU_LEVELS)
                xt = []
                for blk in range(0, ntile, 2 * wt):
                    ref = (blk + wt) * SUBLANE - 1
                    bref = jnp.broadcast_to(bcum[ref:ref + 1, :], (SUBLANE, DK_C))
                    xt += [kt[i] * jnp.exp2(bref - bt[i]) for i in range(blk, blk + wt)]
                    xt += [qt[i] * jnp.exp2(bt[i] - bref) for i in range(blk + wt, blk + 2 * wt)]
            x = jnp.concatenate(xt, axis=0).astype(BF16)
            half = (1 << l) // BF16_ROWS
            if half == 0:
                p = tiles(_dot_tr(x, x))
                prods.append({i: p[i] for i in range(ntile)})
            else:
                ups = [r for r in range(qc // BF16_ROWS) if (r // half) & 1]
                pu = _dot_tr(jnp.concatenate([x[r * BF16_ROWS:(r + 1) * BF16_ROWS, :] for r in ups], axis=0), x)
                tpr = BF16_ROWS // SUBLANE
                prods.append({r * tpr + k: pu[(n * tpr + k) * SUBLANE:(n * tpr + k + 1) * SUBLANE, :]
                              for n, r in enumerate(ups) for k in range(tpr)})
        return qh, st, o, prods

    def combine(hd, rows, kk, bcum, qh, st, o, prods):
        arows = []
        for i in range(ntile):
            lv = lvl_ref[i * SUBLANE:(i + 1) * SUBLANE, :]
            a = jnp.zeros((SUBLANE, qc), F32)
            for l in range(HG_LEVELS):
                if l < sub_levels or (i >> (l - sub_levels)) & 1:
                    a = jnp.where(lv == l, prods[l][i], a)
            arows.append(a)
        amat = jnp.concatenate(arows, axis=0)
        v = v_ref[hd, rows, :]
        vb = v.astype(BF16)
        o = o + _dot(amat.astype(BF16), vb) + jnp.sum(qh * kk, axis=-1, keepdims=True) * v
        blast = bcum[qc - 1:qc, :]
        kdec = (kk * jnp.exp2(blast - bcum)).astype(BF16)
        st_ref[hd] = st * jnp.exp2(blast) + _dot_tl(vb, kdec)
        return o

    nchunk = HG_BLOCK // qc

    def body(idx, carry):
        hg = idx // nchunk
        rows = pl.ds(pl.multiple_of((idx % nchunk) * qc, qc), qc)
        heads = [hg * HG_UNROLL + k for k in range(HG_UNROLL)]
        gs = [gate_split(hd, rows) for hd in heads]
        sums = _dot(gam, jnp.concatenate([s for _, s in gs], axis=1))
        sums = [sums[:, k * DK_C:(k + 1) * DK_C] for k in range(HG_UNROLL)]
        sc = [scores(hd, rows, gs[k][0], sums[k]) for k, hd in enumerate(heads)]
        outs = [combine(hd, rows, gs[k][0][1], sums[k][0:qc], *sc[k]) for k, hd in enumerate(heads)]
        for k, hd in enumerate(heads):
            og_ref[hd, rows, :] = _hg_out(outs[k], g_ref[hd, rows, :], cnw_ref[hd]).astype(BF16)
        return carry

    lax.fori_loop(0, (H_C // HG_UNROLL) * nchunk, body, 0)

    @pl.when(c == last)
    def _():
        for hd in range(H_C):
            cs_ref[0, hd] = st_ref[hd].T


def _c_prompt(u, bsz, seq, w, layer):
    tb = HG_BLOCK
    assert seq % tb == 0 and tb % HG_CHUNK == 0 and (1 << HG_LEVELS) == HG_CHUNK
    nc = seq // tb
    m = bsz * seq
    depth = w["clb"].shape[1]

    def part(k):
        return pl.BlockSpec((H_C, tb, LANE), lambda b, c: (k, b * nc + c, 0))

    c2 = lambda b, c: (0, 0)
    c3 = lambda b, c: (0, 0, 0)
    og, cs = pl.pallas_call(
        functools.partial(_c_prompt_kernel, layer=layer), grid=(bsz, nc),
        in_specs=[part(0), part(1), part(2), part(3),
                  pl.BlockSpec((H_C, depth, DK_C), c3), pl.BlockSpec((H_C, 1, DV_C), c3),
                  pl.BlockSpec(w["gam"].shape, c2), pl.BlockSpec(w["lvl"].shape, c2)],
        out_specs=[pl.BlockSpec((H_C, tb, LANE), lambda b, c: (0, b * nc + c, 0)),
                   pl.BlockSpec((1, H_C, DK_C, DV_C), lambda b, c: (b, 0, 0, 0))],
        out_shape=[jax.ShapeDtypeStruct((H_C, m, DV_C), BF16),
                   jax.ShapeDtypeStruct((bsz, H_C, DK_C, DV_C), F32)],
        scratch_shapes=[pltpu.VMEM((H_C, DV_C, DK_C), F32)],
        compiler_params=_params("parallel", "arbitrary"),
        name="c_prompt",
    )(u, u, u, u, w["clb"], w["cnw"], w["gam"], w["lvl"])
    return og, cs


def _c_sample_kernel(q_ref, f_ref, v_ref, g_ref, s_ref, clb_ref, cnw_ref, og_ref, so_ref, *, layer):
    bb = SAMPLE_BB
    lane = lax.broadcasted_iota(jnp.int32, (DK_C, LANE), 1)
    first_rows = lax.broadcasted_iota(jnp.int32, (LANE, DV_C), 0) < bb
    for hd in range(H_C):
        lb = _hg_lower_bound(clb_ref[hd], layer)
        f, kk = _hg_gates(f_ref[hd], lb)
        f_t = _pad_rows_t(f)
        k_t = _pad_rows_t(kk)
        qs = q_ref[hd] * (DK_C ** -0.5)
        v = v_ref[hd]
        vpad = jnp.where(first_rows, jnp.tile(v, (LANE // bb, 1)), 0.0).astype(BF16)
        orows = []
        for i in range(bb):
            fcol = jnp.broadcast_to(f_t[:, i:i + 1], (DK_C, DV_C))
            kv = _dot(jnp.where(lane == i, k_t, 0.0).astype(BF16), vpad)
            sn = fcol * s_ref[i, hd] + kv
            so_ref[i, hd] = sn
            qrow = jnp.broadcast_to(qs[i:i + 1, :], (2 * SUBLANE, DK_C)).astype(BF16)
            orows.append(_dot(qrow, sn.astype(BF16))[0:1, :])
        o = jnp.concatenate(orows, axis=0)
        og_ref[hd] = _hg_out(o, g_ref[hd], cnw_ref[hd])


def _c_sample(u, s_c, w, layer):
    bsz = s_c.shape[0]
    bb = SAMPLE_BB
    assert bsz % bb == 0
    depth = w["clb"].shape[1]

    def part(k):
        return pl.BlockSpec((H_C, bb, LANE), lambda i: (k, i, 0))

    c3 = lambda i: (0, 0, 0)
    og, so = pl.pallas_call(
        functools.partial(_c_sample_kernel, layer=layer), grid=(bsz // bb,),
        in_specs=[part(0), part(1), part(2), part(3),
                  pl.BlockSpec((bb, H_C, DK_C, DV_C), lambda i: (i, 0, 0, 0)),
                  pl.BlockSpec((H_C, depth, DK_C), c3), pl.BlockSpec((H_C, 1, DV_C), c3)],
        out_specs=[pl.BlockSpec((H_C, bb, LANE), lambda i: (0, i, 0)),
                   pl.BlockSpec((bb, H_C, DK_C, DV_C), lambda i: (i, 0, 0, 0))],
        out_shape=[jax.ShapeDtypeStruct((H_C, bsz, DV_C), F32), jax.ShapeDtypeStruct(s_c.shape, F32)],
        compiler_params=_params("parallel"),
        name="c_sample",
    )(u, u, u, u, s_c, w["clb"], w["cnw"])
    return og, so


def _row(v, width=None):
    v = v.astype(F32).reshape(1, -1)
    if width is not None and v.shape[1] < width:
        v = jnp.pad(v, ((0, 0), (0, width - v.shape[1])))
    return v


def kernel(x_prompt, x_sample, p_prompt, p_sample, state_a_h, state_a_conv, state_b_ssm, state_b_conv, state_c,
           norm_w, norm_f, ab_w_in, a_conv_w, a_conv_b, a_w_r, a_b_r, a_w_i, a_b_i, a_lam, b_conv_w, b_conv_b,
           b_dt_bias, b_a_log, b_d, b_norm_w, ab_w_out, c_w_in, c_lb, c_norm_w, c_w_out, ple_proj, ple_gate):
    depth = norm_w.shape[0]
    bp, seq, _ = x_prompt.shape
    bs = x_sample.shape[0]
    hp = x_prompt.reshape(bp * seq, D_MODEL)
    hs = x_sample.reshape(bs, D_MODEL)
    pp = p_prompt.reshape(depth, bp * seq, D_PLE)
    ps = p_sample.reshape(depth, bs, D_PLE)
    gam, lvl = _hg_gamma(), _hg_level_table()
    clb = c_lb.astype(F32).reshape(depth, H_C, DK_C).transpose(1, 0, 2)

    ah_p, ac_p, bs_p, bc_p, c_p = [], [], [], [], []
    ah_s, ac_s, bs_s, bc_s, c_s = [], [], [], [], []
    for i in range(depth):
        j = i // 2
        final = i == depth - 1
        wg = (0.5 * ple_gate[i]).astype(BF16)
        wp = (0.5 * ple_proj[i]).astype(BF16)
        if i % 2 == 0:
            col_scale = jnp.concatenate([jnp.ones((D_A,), F32), jnp.full((D_A + D_B,), 0.5, F32),
                                         jnp.ones((CONV_DIM_B + H_B,), F32)])
            w_ab = jnp.pad(ab_w_in[j].T * col_scale[:, None], ((0, LANE - H_B), (0, 0))).astype(BF16)
            wo = ab_w_out[j].astype(BF16)
            w = dict(acw=a_conv_w[j].astype(F32), acb=_row(a_conv_b[j]),
                     wr=a_w_r[j].astype(BF16), br=_row(a_b_r[j]), wi=a_w_i[j].astype(BF16), bi=_row(a_b_i[j]),
                     lam=_row(a_lam[j]), bcw=0.5 * b_conv_w[j].astype(F32), bcb=0.5 * _row(b_conv_b[j]),
                     dtb=_row(b_dt_bias[j], LANE), alog=_row(b_a_log[j], LANE),
                     dexp=_row(jnp.repeat(b_d[j], HD_B)), bnw=_row(b_norm_w[j]))
            u, dtr = _in_proj(hp, norm_w[i], w_ab, extra_cols=True, w_t=True)
            hp, s1, s2, s3, s4 = _ab_prompt(u, dtr, hp, pp, i, wo, wg, wp, norm_f, bp, seq, w, final=final)
            ah_p.append(s1); ac_p.append(s2); bs_p.append(s3); bc_p.append(s4)
            u, dtr = _in_proj(hs, norm_w[i], w_ab, extra_cols=True, w_t=True)
            mix, s1, s2, s3, s4 = _ab_sample(u, dtr, state_a_h[j], state_a_conv[j], state_b_ssm[j],
                                             state_b_conv[j], w)
            ah_s.append(s1); ac_s.append(s2); bs_s.append(s3); bc_s.append(s4)
            hs = _out_proj(mix, hs, ps, i, wo, wg, wp, norm_f, head_major=False, final=final)
        else:
            col_scale = jnp.concatenate([jnp.ones((HK_C,), F32), jnp.full((HK_C,), 0.5, F32),
                                         jnp.ones((D_C,), F32), jnp.full((D_C,), 0.5, F32)])
            w_in = (c_w_in[j] * col_scale).astype(BF16)
            wo = c_w_out[j].astype(BF16)
            w = dict(clb=clb, cnw=c_norm_w[j].astype(F32).reshape(H_C, 1, DV_C), gam=gam, lvl=lvl)
            u = _in_proj(hp, norm_w[i], w_in, head_major=True)
            og, s1 = _c_prompt(u, bp, seq, w, i)
            c_p.append(s1)
            hp = _out_proj(og, hp, pp, i, wo, wg, wp, norm_f, head_major=True, final=final)
            u = _in_proj(hs, norm_w[i], w_in, head_major=True)
            og, s1 = _c_sample(u, state_c[j], w, i)
            c_s.append(s1)
            hs = _out_proj(og, hs, ps, i, wo, wg, wp, norm_f, head_major=True, final=final)
    return (hp.reshape(bp, seq, D_MODEL), hs.reshape(bs, 1, D_MODEL),
            jnp.stack(ah_p), jnp.stack(ac_p), jnp.stack(bs_p), jnp.stack(bc_p), jnp.stack(c_p),
            jnp.stack(ah_s), jnp.stack(ac_s), jnp.stack(bs_s), jnp.stack(bc_s), jnp.stack(c_s))
```

```python
import functools

import jax
import jax.numpy as jnp
from jax import lax
from jax.experimental import pallas as pl
from jax.experimental.pallas import tpu as pltpu

F32 = jnp.float32
BF16 = jnp.bfloat16

D_MODEL = 1024
D_PLE = 256
EPS = 1e-6
CONV_W = 4
D_A = D_MODEL
A_BLOCKS = 8
A_BLK = D_A // A_BLOCKS
LRU_C = 8.0
D_B = D_MODEL
HD_B = 64
H_B = D_B // HD_B
N_B = 128
G_B = 2
CONV_DIM_B = D_B + 2 * G_B * N_B
D_C = 2 * D_MODEL
H_C = 16
DK_C = 128
DV_C = D_C // H_C
HK_C = H_C * DK_C
AB_MAIN = 2 * D_A + D_B + CONV_DIM_B
IN_C = 2 * HK_C + 2 * D_C

LANE = 128
SUBLANE = 8
BF16_ROWS = 16
LOG2_E = 1.4426950408889634
VMEM_LIMIT = 52 * 1024 * 1024

PROJ_TM = 2048
PROJ_MAX_TN = 1024
SSD_CHUNK = 128
AB_CHUNKS = 1
HG_CHUNK = 64
HG_LEVELS = 6
HG_MXU_LEVELS = 3
HG_BLOCK = 512
HG_UNROLL = 16
SAMPLE_BB = 8

_DN_TR = (((1,), (1,)), ((), ()))
_DN_TL = (((0,), (0,)), ((), ()))


def _dot(a, b):
    return jnp.dot(a, b, preferred_element_type=F32)


def _dot_tr(a, b):
    return lax.dot_general(a, b, _DN_TR, preferred_element_type=F32)


def _dot_tl(a, b):
    return lax.dot_general(a, b, _DN_TL, preferred_element_type=F32)


def _silu_half(x_half):
    return x_half + x_half * jnp.tanh(x_half)


def _rmsnorm(x, w):
    return x * lax.rsqrt(jnp.mean(x * x, axis=-1, keepdims=True) + EPS) * w


def _params(*sem):
    return pltpu.CompilerParams(dimension_semantics=sem, vmem_limit_bytes=VMEM_LIMIT)


def _in_proj_kernel(x_ref, nw_ref, w_ref, *rest, has_extra, head_major, w_t):
    if has_extra:
        wx_ref, o_ref, ox_ref, xn_ref = rest
    else:
        o_ref, xn_ref = rest
    dot = _dot_tr if w_t else _dot

    @pl.when(pl.program_id(1) == 0)
    def _():
        xn_ref[...] = _rmsnorm(x_ref[...], nw_ref[...]).astype(BF16)
        if has_extra:
            ox_ref[...] = dot(xn_ref[...], wx_ref[...])

    acc = dot(xn_ref[...], w_ref[...])
    if head_major:
        for k in range(acc.shape[1] // LANE):
            o_ref[k] = acc[:, k * LANE:(k + 1) * LANE]
    else:
        o_ref[...] = acc


def _in_proj(x, nw, w, *, extra_cols=False, head_major=False, w_t=False):
    m, k = x.shape
    n = w.shape[0 if w_t else 1] - (LANE if extra_cols else 0)
    tm = min(m, PROJ_TM)
    tn = next(c for c in range(PROJ_MAX_TN, 0, -LANE) if n % c == 0)
    assert m % tm == 0 and n % tn == 0 and tn % LANE == 0
    grid = (m // tm, n // tn)
    wspec = (lambda cols, idx: pl.BlockSpec((cols, k), lambda i, j: (idx(j), 0))) if w_t else \
            (lambda cols, idx: pl.BlockSpec((k, cols), lambda i, j: (0, idx(j))))
    in_specs = [pl.BlockSpec((tm, k), lambda i, j: (i, 0)),
                pl.BlockSpec((1, k), lambda i, j: (0, 0)),
                wspec(tn, lambda j: j)]
    args = [x, nw.reshape(1, k), w]
    if head_major:
        out_shape = [jax.ShapeDtypeStruct((n // LANE, m, LANE), F32)]
        out_specs = [pl.BlockSpec((tn // LANE, tm, LANE), lambda i, j: (j, i, 0))]
    else:
        out_shape = [jax.ShapeDtypeStruct((m, n), F32)]
        out_specs = [pl.BlockSpec((tm, tn), lambda i, j: (i, j))]
    if extra_cols:
        in_specs.append(wspec(LANE, lambda j: n // LANE))
        args.append(w)
        out_shape.append(jax.ShapeDtypeStruct((m, LANE), F32))
        out_specs.append(pl.BlockSpec((tm, LANE), lambda i, j: (i, 0)))
    outs = pl.pallas_call(
        functools.partial(_in_proj_kernel, has_extra=extra_cols, head_major=head_major, w_t=w_t),
        grid=grid, in_specs=in_specs, out_specs=out_specs, out_shape=out_shape,
        scratch_shapes=[pltpu.VMEM((tm, k), BF16)],
        compiler_params=_params("parallel", "arbitrary"),
        name="in_proj",
    )(*args)
    return outs if extra_cols else outs[0]


def _residual_ple(h, p_ref, wg_ref, wp_ref, nf_ref, final):
    gate_t = jnp.tanh(_dot(h.astype(BF16), wg_ref[...]))
    pe_half = _dot(p_ref[...].astype(BF16), wp_ref[...])
    h = h + pe_half + pe_half * gate_t
    return _rmsnorm(h, nf_ref[...]) if final else h


def _out_proj_kernel(mix_ref, h_ref, p_ref, wo_ref, wg_ref, wp_ref, nf_ref, o_ref, *, head_major, final):
    if head_major:
        mix = jnp.concatenate([mix_ref[k] for k in range(mix_ref.shape[0])], axis=-1)
    else:
        mix = mix_ref[...]
    h = h_ref[...] + _dot(mix.astype(BF16), wo_ref[...])
    o_ref[...] = _residual_ple(h, p_ref, wg_ref, wp_ref, nf_ref, final)


def _out_proj(mix, h, p, layer, wo, wg, wp, nf, *, head_major, final):
    m, d = h.shape
    tm = min(m, 512)
    assert m % tm == 0
    if head_major:
        mix_spec = pl.BlockSpec((mix.shape[0], tm, LANE), lambda i: (0, i, 0))
    else:
        mix_spec = pl.BlockSpec((tm, mix.shape[1]), lambda i: (i, 0))
    const = lambda i: (0, 0)
    return pl.pallas_call(
        functools.partial(_out_proj_kernel, head_major=head_major, final=final),
        grid=(m // tm,),
        in_specs=[mix_spec,
                  pl.BlockSpec((tm, d), lambda i: (i, 0)),
                  pl.BlockSpec((None, tm, p.shape[2]), lambda i: (layer, i, 0)),
                  pl.BlockSpec(wo.shape, const), pl.BlockSpec(wg.shape, const),
                  pl.BlockSpec(wp.shape, const), pl.BlockSpec((1, d), const)],
        out_specs=pl.BlockSpec((tm, d), lambda i: (i, 0)),
        out_shape=jax.ShapeDtypeStruct((m, d), F32),
        compiler_params=_params("parallel"),
        name="out_proj",
    )(mix, h, p, wo, wg, wp, nf.reshape(1, d))


def _lru_gates(xc, wr_ref, br_ref, wi_ref, bi_ref, lam_ref):
    xcb = xc.astype(BF16)
    r_parts, i_parts = [], []
    for k in range(A_BLOCKS):
        xk = xcb[:, k * A_BLK:(k + 1) * A_BLK]
        r_parts.append(_dot(xk, wr_ref[k]))
        i_parts.append(_dot(xk, wi_ref[k]))
    r = jax.nn.sigmoid(jnp.concatenate(r_parts, axis=-1) + br_ref[...])
    gi = jax.nn.sigmoid(jnp.concatenate(i_parts, axis=-1) + bi_ref[...])
    log_a = (-LRU_C) * r * jax.nn.softplus(-lam_ref[...])
    a = jnp.exp(log_a)
    v = jnp.tanh(-log_a) * (a * a + 1.0)
    u = jnp.where(v > 0.0, v * lax.rsqrt(v), 0.0) * (gi * xc)
    return a, u


def _group_norm_gate(y, bx, z_half, dexp_ref, bnw_ref):
    y = (y + dexp_ref[...] * bx) * _silu_half(z_half)
    gw = D_B // G_B
    parts = []
    for g in range(G_B):
        yg = y[:, g * gw:(g + 1) * gw]
        parts.append(yg * lax.rsqrt(jnp.mean(yg * yg, axis=-1, keepdims=True) + EPS))
    return jnp.concatenate(parts, axis=-1) * bnw_ref[...]


def _ab_prompt_kernel(ax_ref, ag_ref, z_ref, xbc_ref, dt_ref,
                      acw_ref, acb_ref, wr_ref, br_ref, wi_ref, bi_ref, lam_ref,
                      bcw_ref, bcb_ref, dtb_ref, alog_ref, dexp_ref, bnw_ref,
                      hin_ref, p_ref, wo_ref, wg_ref, wp_ref, nf_ref,
                      hout_ref, *state_refs, final):
    for k in range(AB_CHUNKS):
        r = pl.ds(k * SSD_CHUNK, SSD_CHUNK)
        _ab_prompt_chunk(ax_ref.at[r], ag_ref.at[r], z_ref.at[r], xbc_ref.at[r], dt_ref.at[r],
                         acw_ref, acb_ref, wr_ref, br_ref, wi_ref, bi_ref, lam_ref,
                         bcw_ref, bcb_ref, dtb_ref, alog_ref, dexp_ref, bnw_ref,
                         hin_ref.at[r], p_ref.at[r], wo_ref, wg_ref, wp_ref, nf_ref,
                         hout_ref.at[r], *state_refs, final=final, first=k == 0, last=k == AB_CHUNKS - 1)


def _ab_prompt_chunk(ax_ref, ag_ref, z_ref, xbc_ref, dt_ref,
                     acw_ref, acb_ref, wr_ref, br_ref, wi_ref, bi_ref, lam_ref,
                     bcw_ref, bcb_ref, dtb_ref, alog_ref, dexp_ref, bnw_ref,
                     hin_ref, p_ref, wo_ref, wg_ref, wp_ref, nf_ref,
                     hout_ref, ah_ref, ac_ref, bs_ref, bc_ref,
                     xpa_ref, xpb_ref, h_ref, s_ref, *, final, first, last):
    c = pl.program_id(1)
    t = SSD_CHUNK
    ntile = t // SUBLANE

    if first:
        @pl.when(c == 0)
        def _():
            xpa_ref[...] = jnp.zeros_like(xpa_ref)
            xpb_ref[...] = jnp.zeros_like(xpb_ref)
            h_ref[...] = jnp.zeros_like(h_ref)
            s_ref[...] = jnp.zeros_like(s_ref)

    def tiles(x):
        return [x[i * SUBLANE:(i + 1) * SUBLANE, :] for i in range(ntile)]

    def conv(x, tail_ref, w_ref, b_ref):
        sub = lax.broadcasted_iota(jnp.int32, (SUBLANE, x.shape[1]), 0)
        xt = [tail_ref[...]] + tiles(x)
        taps = [jnp.broadcast_to(w_ref[k:k + 1, :], (SUBLANE, x.shape[1])) for k in range(CONV_W)]
        bias = jnp.broadcast_to(b_ref[...], (SUBLANE, x.shape[1]))
        acc = [bias + taps[CONV_W - 1] * xt[i + 1] for i in range(ntile)]
        for s in range(1, CONV_W):
            wk = taps[CONV_W - 1 - s]
            for i in range(ntile):
                merged = jnp.where(sub >= SUBLANE - s, xt[i], xt[i + 1])
                acc[i] = acc[i] + wk * pltpu.roll(merged, s, 0)
        tail_ref[...] = xt[ntile]
        return jnp.concatenate(acc, axis=0)

    ax = ax_ref[...]
    xc = conv(ax, xpa_ref, acw_ref, acb_ref)
    a, u = _lru_gates(xc, wr_ref, br_ref, wi_ref, bi_ref, lam_ref)
    sub = lax.broadcasted_iota(jnp.int32, (SUBLANE, D_A), 0)
    at, ut = tiles(a), tiles(u)
    step = 1
    while step < SUBLANE:
        m = sub >= step
        for i in range(ntile):
            ut[i] = jnp.where(m, at[i] * pltpu.roll(ut[i], step, 0) + ut[i], ut[i])
            at[i] = jnp.where(m, at[i] * pltpu.roll(at[i], step, 0), at[i])
        step *= 2
    carry = h_ref[0:1, :]
    hs = []
    for i in range(ntile):
        hs.append(ut[i] + at[i] * carry)
        carry = hs[i][SUBLANE - 1:SUBLANE, :]
    h = jnp.concatenate(hs, axis=0)
    h_ref[0:1, :] = carry
    a_out = (h * _silu_half(ag_ref[...])).astype(BF16)

    xb = xbc_ref[...]
    xbc = _silu_half(conv(xb, xpb_ref, bcw_ref, bcb_ref))
    bx = xbc[:, 0:D_B]
    bxb = bx.astype(BF16)
    dt = jax.nn.softplus(dt_ref[...] + dtb_ref[...])
    adt = dt * (-jnp.exp(alog_ref[...]))
    ti = lax.broadcasted_iota(jnp.int32, (t, t), 0)
    si = lax.broadcasted_iota(jnp.int32, (t, t), 1)
    causal = ti >= si
    acs = jnp.dot(causal.astype(F32), adt, preferred_element_type=F32,
                  precision=lax.Precision.HIGHEST)
    a_last = acs[t - 1:t, :]
    wq = jnp.exp(a_last - acs) * dt
    eacs = jnp.exp(acs)
    ealast = jnp.exp(a_last)
    acs_t = acs.T
    dt_t = dt.T
    lane = lax.broadcasted_iota(jnp.int32, (t, LANE), 1)
    rowi = lax.broadcasted_iota(jnp.int32, (LANE, N_B), 0)
    hpg = H_B // G_B
    ys = []
    cb = None
    for j in range(H_B // 2):
        g = (2 * j) // hpg
        bg = xbc[:, D_B + g * N_B:D_B + (g + 1) * N_B]
        cg = xbc[:, D_B + G_B * N_B + g * N_B:D_B + G_B * N_B + (g + 1) * N_B]
        if (2 * j) % hpg == 0:
            cb = _dot_tr(cg.astype(BF16), bg.astype(BF16))
        xpair = bxb[:, j * LANE:(j + 1) * LANE]
        sp = s_ref[j]
        spb = sp.astype(BF16)
        y_h, up_h = [], []
        for hh in range(2):
            hd = 2 * j + hh
            seg = jnp.broadcast_to(acs[:, hd:hd + 1], (t, t)) - jnp.broadcast_to(acs_t[hd:hd + 1, :], (t, t))
            lmat = jnp.exp(jnp.where(causal, seg, -1e30))
            mmat = (cb * lmat * jnp.broadcast_to(dt_t[hd:hd + 1, :], (t, t))).astype(BF16)
            ec = (jnp.broadcast_to(eacs[:, hd:hd + 1], (t, N_B)) * cg).astype(BF16)
            y_h.append(_dot(mmat, xpair) + _dot_tr(ec, spb))
            bw = (bg * jnp.broadcast_to(wq[:, hd:hd + 1], (t, N_B))).astype(BF16)
            up_h.append(_dot_tl(xpair, bw))
        ys.append(jnp.where(lane < HD_B, y_h[0], y_h[1]))
        dec = jnp.where(rowi < HD_B,
                        jnp.broadcast_to(ealast[:, 2 * j:2 * j + 1], (LANE, N_B)),
                        jnp.broadcast_to(ealast[:, 2 * j + 1:2 * j + 2], (LANE, N_B)))
        s_ref[j] = dec * sp + jnp.where(rowi < HD_B, up_h[0], up_h[1])
    y = jnp.concatenate(ys, axis=-1)
    b_out = _group_norm_gate(y, bx, z_ref[...], dexp_ref, bnw_ref).astype(BF16)

    hres = hin_ref[...] + _dot(a_out, wo_ref[0:D_A, :]) + _dot(b_out, wo_ref[D_A:D_A + D_B, :])
    hout_ref[...] = _residual_ple(hres, p_ref, wg_ref, wp_ref, nf_ref, final)

    if last:
        @pl.when(c == pl.num_programs(1) - 1)
        def _():
            ah_ref[0] = h[t - 1:t, :]
            ac_ref[0] = ax[t - (CONV_W - 1):t, :]
            bc_ref[0] = xb[t - (CONV_W - 1):t, :]
            for j in range(H_B // 2):
                sj = s_ref[j]
                bs_ref[0, 2 * j] = sj[0:HD_B, :]
                bs_ref[0, 2 * j + 1] = sj[HD_B:2 * HD_B, :]


def _ab_prompt(u, dtr, h, p, layer, wo, wg, wp, nf, bsz, seq, w, *, final):
    t = AB_CHUNKS * SSD_CHUNK
    assert seq % t == 0
    nc = seq // t
    m = bsz * seq
    rows = lambda b, c: b * nc + c
    cvec = lambda b, c: (0, 0)
    c3 = lambda b, c: (0, 0, 0)
    in_specs = [
        pl.BlockSpec((t, D_A), lambda b, c: (rows(b, c), 0)),
        pl.BlockSpec((t, D_A), lambda b, c: (rows(b, c), 1)),
        pl.BlockSpec((t, D_B), lambda b, c: (rows(b, c), 2)),
        pl.BlockSpec((t, CONV_DIM_B), lambda b, c: (rows(b, c), 2)),
        pl.BlockSpec((t, LANE), lambda b, c: (rows(b, c), 0)),
        pl.BlockSpec((CONV_W, D_A), cvec), pl.BlockSpec((1, D_A), cvec),
        pl.BlockSpec((A_BLOCKS, A_BLK, A_BLK), c3), pl.BlockSpec((1, D_A), cvec),
        pl.BlockSpec((A_BLOCKS, A_BLK, A_BLK), c3), pl.BlockSpec((1, D_A), cvec),
        pl.BlockSpec((1, D_A), cvec),
        pl.BlockSpec((CONV_W, CONV_DIM_B), cvec), pl.BlockSpec((1, CONV_DIM_B), cvec),
        pl.BlockSpec((1, LANE), cvec), pl.BlockSpec((1, LANE), cvec),
        pl.BlockSpec((1, D_B), cvec), pl.BlockSpec((1, D_B), cvec),
        pl.BlockSpec((t, D_MODEL), lambda b, c: (rows(b, c), 0)),
        pl.BlockSpec((None, t, p.shape[2]), lambda b, c: (layer, rows(b, c), 0)),
        pl.BlockSpec(wo.shape, cvec), pl.BlockSpec(wg.shape, cvec), pl.BlockSpec(wp.shape, cvec),
        pl.BlockSpec((1, D_MODEL), cvec),
    ]
    out_shape = [
        jax.ShapeDtypeStruct((m, D_MODEL), F32),
        jax.ShapeDtypeStruct((bsz, 1, D_A), F32),
        jax.ShapeDtypeStruct((bsz, CONV_W - 1, D_A), F32),
        jax.ShapeDtypeStruct((bsz, H_B, HD_B, N_B), F32),
        jax.ShapeDtypeStruct((bsz, CONV_W - 1, CONV_DIM_B), F32),
    ]
    out_specs = [
        pl.BlockSpec((t, D_MODEL), lambda b, c: (rows(b, c), 0)),
        pl.BlockSpec((1, 1, D_A), lambda b, c: (b, 0, 0)),
        pl.BlockSpec((1, CONV_W - 1, D_A), lambda b, c: (b, 0, 0)),
        pl.BlockSpec((1, H_B, HD_B, N_B), lambda b, c: (b, 0, 0, 0)),
        pl.BlockSpec((1, CONV_W - 1, CONV_DIM_B), lambda b, c: (b, 0, 0)),
    ]
    hout, ah, ac, bs, bc = pl.pallas_call(
        functools.partial(_ab_prompt_kernel, final=final),
        grid=(bsz, nc), in_specs=in_specs, out_specs=out_specs, out_shape=out_shape,
        scratch_shapes=[pltpu.VMEM((SUBLANE, D_A), F32), pltpu.VMEM((SUBLANE, CONV_DIM_B), F32),
                        pltpu.VMEM((SUBLANE, D_A), F32), pltpu.VMEM((H_B // 2, 2 * HD_B, N_B), F32)],
        compiler_params=_params("parallel", "arbitrary"),
        name="ab_prompt",
    )(u, u, u, u, dtr, w["acw"], w["acb"], w["wr"], w["br"], w["wi"], w["bi"], w["lam"],
      w["bcw"], w["bcb"], w["dtb"], w["alog"], w["dexp"], w["bnw"],
      h, p, wo, wg, wp, nf.reshape(1, D_MODEL))
    return hout, ah.reshape(bsz, D_A), ac, bs, bc


def _ab_sample_rows_kernel(ax_ref, ag_ref, xbc_ref, dt_ref, sah_ref, sac_ref, sbc_ref,
                           acw_ref, acb_ref, wr_ref, br_ref, wi_ref, bi_ref, lam_ref,
                           bcw_ref, bcb_ref, dtb_ref,
                           aout_ref, ah_ref, ac_ref, bc_ref, xact_ref, dts_ref):
    def conv1(x, buf_ref, w_ref, b_ref, nbuf_ref, width):
        y = b_ref[...] + w_ref[CONV_W - 1:CONV_W, :] * x
        for k in range(CONV_W - 1):
            y = y + w_ref[k:k + 1, :] * buf_ref[:, k * width:(k + 1) * width]
        for k in range(CONV_W - 2):
            nbuf_ref[:, k * width:(k + 1) * width] = buf_ref[:, (k + 1) * width:(k + 2) * width]
        nbuf_ref[:, (CONV_W - 2) * width:(CONV_W - 1) * width] = x
        return y

    xc = conv1(ax_ref[...], sac_ref, acw_ref, acb_ref, ac_ref, D_A)
    a, u = _lru_gates(xc, wr_ref, br_ref, wi_ref, bi_ref, lam_ref)
    h = a * sah_ref[...] + u
    ah_ref[...] = h
    aout_ref[...] = h * _silu_half(ag_ref[...])
    xact_ref[...] = _silu_half(conv1(xbc_ref[...], sbc_ref, bcw_ref, bcb_ref, bc_ref, CONV_DIM_B))
    dts_ref[...] = jax.nn.softplus(dt_ref[...] + dtb_ref[...])


def _pad_rows_t(x):
    pad = jnp.zeros((LANE - x.shape[0], x.shape[1]), F32)
    return jnp.concatenate([x, pad], axis=0).T


def _ab_sample_state_kernel(s_ref, xact_ref, dts_ref, z_ref, alog_ref, dexp_ref, bnw_ref,
                            so_ref, bout_ref, y_ref):
    bb = SAMPLE_BB
    xact = xact_ref[...]
    bx = xact[:, 0:D_B]
    dts = dts_ref[...]
    dec_t = _pad_rows_t(jnp.exp(dts * (-jnp.exp(alog_ref[...]))))
    dts_t = _pad_rows_t(dts)
    hpg = H_B // G_B
    for j in range(H_B // 2):
        g = (2 * j) // hpg
        xt = _pad_rows_t(bx[:, j * LANE:(j + 1) * LANE])
        dtp = jnp.concatenate([jnp.broadcast_to(dts_t[2 * j:2 * j + 1, :], (HD_B, LANE)),
                               jnp.broadcast_to(dts_t[2 * j + 1:2 * j + 2, :], (HD_B, LANE))], axis=0)
        xdt = xt * dtp
        for i in range(bb):
            brow = jnp.broadcast_to(xact[i:i + 1, D_B + g * N_B:D_B + (g + 1) * N_B], (2 * HD_B, N_B))
            crow = jnp.broadcast_to(
                xact[i:i + 1, D_B + G_B * N_B + g * N_B:D_B + G_B * N_B + (g + 1) * N_B], (2 * SUBLANE, N_B))
            upd = jnp.broadcast_to(xdt[:, i:i + 1], (2 * HD_B, N_B)) * brow
            news = []
            for hh in range(2):
                hd = 2 * j + hh
                dec = jnp.broadcast_to(dec_t[hd:hd + 1, i:i + 1], (HD_B, N_B))
                sn = dec * s_ref[i, hd] + upd[hh * HD_B:(hh + 1) * HD_B, :]
                so_ref[i, hd] = sn
                news.append(sn)
            spair = jnp.concatenate(news, axis=0).astype(BF16)
            yrow = _dot_tr(crow.astype(BF16), spair)
            y_ref[i:i + 1, j * LANE:(j + 1) * LANE] = yrow[0:1, :]
    bout_ref[...] = _group_norm_gate(y_ref[...], bx, z_ref[...], dexp_ref, bnw_ref)


def _ab_sample(u, dtr, s_ah, s_ac, s_bs, s_bc, w):
    bsz = u.shape[0]
    full = lambda shape: pl.BlockSpec(shape, lambda i: tuple(0 for _ in shape))
    cw = CONV_W - 1
    aout, ah, ac, bc, xact, dts = pl.pallas_call(
        _ab_sample_rows_kernel, grid=(1,),
        in_specs=[pl.BlockSpec((bsz, D_A), lambda i: (0, 0)), pl.BlockSpec((bsz, D_A), lambda i: (0, 1)),
                  pl.BlockSpec((bsz, CONV_DIM_B), lambda i: (0, 2)), full((bsz, LANE)),
                  full((bsz, D_A)), full((bsz, cw * D_A)), full((bsz, cw * CONV_DIM_B)),
                  full((CONV_W, D_A)), full((1, D_A)),
                  full((A_BLOCKS, A_BLK, A_BLK)), full((1, D_A)),
                  full((A_BLOCKS, A_BLK, A_BLK)), full((1, D_A)), full((1, D_A)),
                  full((CONV_W, CONV_DIM_B)), full((1, CONV_DIM_B)), full((1, LANE))],
        out_specs=[full((bsz, D_A)), full((bsz, D_A)), full((bsz, cw * D_A)), full((bsz, cw * CONV_DIM_B)),
                   full((bsz, CONV_DIM_B)), full((bsz, LANE))],
        out_shape=[jax.ShapeDtypeStruct((bsz, D_A), F32), jax.ShapeDtypeStruct((bsz, D_A), F32),
                   jax.ShapeDtypeStruct((bsz, cw * D_A), F32), jax.ShapeDtypeStruct((bsz, cw * CONV_DIM_B), F32),
                   jax.ShapeDtypeStruct((bsz, CONV_DIM_B), F32), jax.ShapeDtypeStruct((bsz, LANE), F32)],
        compiler_params=_params("arbitrary"),
        name="ab_sample_rows",
    )(u, u, u, dtr, s_ah, s_ac.reshape(bsz, cw * D_A), s_bc.reshape(bsz, cw * CONV_DIM_B),
      w["acw"], w["acb"], w["wr"], w["br"], w["wi"], w["bi"], w["lam"], w["bcw"], w["bcb"], w["dtb"])

    bb = SAMPLE_BB
    assert bsz % bb == 0
    cvec = lambda i: (0, 0)
    bs, bout = pl.pallas_call(
        _ab_sample_state_kernel, grid=(bsz // bb,),
        in_specs=[pl.BlockSpec((bb, H_B, HD_B, N_B), lambda i: (i, 0, 0, 0)),
                  pl.BlockSpec((bb, CONV_DIM_B), lambda i: (i, 0)),
                  pl.BlockSpec((bb, LANE), lambda i: (i, 0)),
                  pl.BlockSpec((bb, D_B), lambda i: (i, 2)),
                  pl.BlockSpec((1, LANE), cvec), pl.BlockSpec((1, D_B), cvec), pl.BlockSpec((1, D_B), cvec)],
        out_specs=[pl.BlockSpec((bb, H_B, HD_B, N_B), lambda i: (i, 0, 0, 0)),
                   pl.BlockSpec((bb, D_B), lambda i: (i, 0))],
        out_shape=[jax.ShapeDtypeStruct(s_bs.shape, F32), jax.ShapeDtypeStruct((bsz, D_B), F32)],
        scratch_shapes=[pltpu.VMEM((bb, D_B), F32)],
        compiler_params=_params("parallel"),
        name="ab_sample_state",
    )(s_bs, xact, dts, u, w["alog"], w["dexp"], w["bnw"])
    mix = jnp.concatenate([aout, bout], axis=-1)
    return mix, ah, ac.reshape(bsz, cw, D_A), bs, bc.reshape(bsz, cw, CONV_DIM_B)


def _hg_lower_bound(clb, layer):
    mx = jnp.max(clb, axis=0, keepdims=True)
    ex = jnp.exp(clb - mx)
    return jnp.sum(ex[1:layer + 1], axis=0, keepdims=True) / jnp.sum(ex, axis=0, keepdims=True)


def _hg_gates(fx_half, lb):
    f = 0.5 * (1.0 + lb) + (0.5 * (1.0 - lb)) * jnp.tanh(fx_half)
    return f, 1.0 - f


def _hg_out(o, gate_half, cnw):
    return o * lax.rsqrt(jnp.mean(o * o, axis=-1, keepdims=True) + EPS) * cnw * (gate_half + gate_half * jnp.tanh(gate_half))


def _hg_gamma():
    import numpy as np
    q = HG_CHUNK
    t = np.arange(q)[:, None]
    tau = np.arange(q)[None, :]
    mats = [(tau <= t)]
    for l in range(1, HG_MXU_LEVELS):
        w = 1 << l
        ref = (t // (2 * w)) * (2 * w) + w - 1
        upper = (t % (2 * w)) >= w
        mats.append(np.where(upper, (tau > ref) & (tau <= t), (tau > t) & (tau <= ref)))
    gam = np.concatenate(mats, axis=0).astype(np.float32)
    return jnp.asarray(np.concatenate([gam, gam], axis=1), dtype=BF16)


def _hg_level_table():
    import numpy as np
    q = HG_CHUNK
    t = np.arange(q)[:, None]
    s = np.arange(q)[None, :]
    x = t ^ s
    lvl = np.floor(np.log2(np.maximum(x, 1))).astype(np.int32)
    return jnp.asarray(np.where(t > s, lvl, -1).astype(np.int32))


def _c_prompt_kernel(q_ref, f_ref, v_ref, g_ref, clb_ref, cnw_ref, gam_ref, lvl_ref,
                     og_ref, cs_ref, st_ref, *, layer):
    c = pl.program_id(1)
    last = pl.num_programs(1) - 1
    qc = HG_CHUNK

    @pl.when(c == 0)
    def _():
        st_ref[...] = jnp.zeros_like(st_ref)

    gam = gam_ref[...]
    ntile = qc // SUBLANE
    sub = lax.broadcasted_iota(jnp.int32, (SUBLANE, DK_C), 0)
    sub_levels = HG_MXU_LEVELS
    sub_upper = [(sub & (1 << l)) != 0 for l in range(sub_levels)]

    def tiles(x):
        return [x[i * SUBLANE:(i + 1) * SUBLANE, :] for i in range(ntile)]

    def gate_split(hd, rows):
        f, kk = _hg_gates(f_ref[hd, rows, :], _hg_lower_bound(clb_ref[hd], layer))
        g = jnp.log(f) * LOG2_E
        g1 = g.astype(BF16)
        g2 = (g - g1.astype(F32)).astype(BF16)
        return (f, kk), jnp.concatenate([g1, g2], axis=0)

    def scores(hd, rows, fk, sums):
        f, kk = fk
        qh = q_ref[hd, rows, :] * (DK_C ** -0.5)
        bcum = sums[0:qc]
        st = st_ref[hd]
        o = _dot_tr((qh * jnp.exp2(bcum)).astype(BF16), st.astype(BF16))
        qt, kt, ft, bt = tiles(qh), tiles(kk), tiles(f), tiles(bcum)
        prods = []
        for l in range(HG_LEVELS):
            if l == 0:
                xt = [jnp.where(sub_upper[0], qt[i] * ft[i], kt[i]) for i in range(ntile)]
            elif l < HG_MXU_LEVELS:
                dec = tiles(jnp.exp2(sums[l * qc:(l + 1) * qc]))
                xt = [jnp.where(sub_upper[l], qt[i], kt[i]) * dec[i] for i in range(ntile)]
            else:
                wt = 1 << (l - HG_MXU_LEVELS)
                xt = []
                for blk in range(0, ntile, 2 * wt):
                    ref = (blk + wt) * SUBLANE - 1
                    bref = jnp.broadcast_to(bcum[ref:ref + 1, :], (SUBLANE, DK_C))
                    xt += [kt[i] * jnp.exp2(bref - bt[i]) for i in range(blk, blk + wt)]
                    xt += [qt[i] * jnp.exp2(bt[i] - bref) for i in range(blk + wt, blk + 2 * wt)]
            x = jnp.concatenate(xt, axis=0).astype(BF16)
            half = (1 << l) // BF16_ROWS
            if half == 0:
                p = tiles(_dot_tr(x, x))
                prods.append({i: p[i] for i in range(ntile)})
            else:
                ups = [r for r in range(qc // BF16_ROWS) if (r // half) & 1]
                pu = _dot_tr(jnp.concatenate([x[r * BF16_ROWS:(r + 1) * BF16_ROWS, :] for r in ups], axis=0), x)
                tpr = BF16_ROWS // SUBLANE
                prods.append({r * tpr + k: pu[(n * tpr + k) * SUBLANE:(n * tpr + k + 1) * SUBLANE, :]
                              for n, r in enumerate(ups) for k in range(tpr)})
        return qh, st, o, prods

    def level_masks():
        masks = {}
        for i in range(ntile):
            lv = lvl_ref[i * SUBLANE:(i + 1) * SUBLANE, :]
            for l in range(HG_LEVELS):
                if l < sub_levels or (i >> (l - sub_levels)) & 1:
                    masks[i, l] = lv == l
        return masks

    def combine(hd, rows, kk, bcum, qh, st, o, prods, masks):
        arows = []
        for i in range(ntile):
            a = jnp.zeros((SUBLANE, qc), F32)
            for l in range(HG_LEVELS):
                if (i, l) in masks:
                    a = jnp.where(masks[i, l], prods[l][i], a)
            arows.append(a)
        amat = jnp.concatenate(arows, axis=0)
        v = v_ref[hd, rows, :]
        vb = v.astype(BF16)
        o = o + _dot(amat.astype(BF16), vb) + jnp.sum(qh * kk, axis=-1, keepdims=True) * v
        blast = bcum[qc - 1:qc, :]
        kdec = (kk * jnp.exp2(blast - bcum)).astype(BF16)
        st_ref[hd] = st * jnp.exp2(blast) + _dot_tl(vb, kdec)
        return o

    nchunk = HG_BLOCK // qc

    def body(idx, carry):
        hg = idx // nchunk
        rows = pl.ds(pl.multiple_of((idx % nchunk) * qc, qc), qc)
        heads = [hg * HG_UNROLL + k for k in range(HG_UNROLL)]
        gs = [gate_split(hd, rows) for hd in heads]
        sums = _dot(gam, jnp.concatenate([s for _, s in gs], axis=1))
        sums = [sums[:, k * DK_C:(k + 1) * DK_C] for k in range(HG_UNROLL)]
        sc = [scores(hd, rows, gs[k][0], sums[k]) for k, hd in enumerate(heads)]
        masks = level_masks()
        outs = [combine(hd, rows, gs[k][0][1], sums[k][0:qc], *sc[k], masks) for k, hd in enumerate(heads)]
        for k, hd in enumerate(heads):
            og_ref[hd, rows, :] = _hg_out(outs[k], g_ref[hd, rows, :], cnw_ref[hd]).astype(BF16)
        return carry

    lax.fori_loop(0, (H_C // HG_UNROLL) * nchunk, body, 0)

    @pl.when(c == last)
    def _():
        for hd in range(H_C):
            cs_ref[0, hd] = st_ref[hd].T


def _c_prompt(u, bsz, seq, w, layer):
    tb = HG_BLOCK
    assert seq % tb == 0 and tb % HG_CHUNK == 0 and (1 << HG_LEVELS) == HG_CHUNK
    nc = seq // tb
    m = bsz * seq
    depth = w["clb"].shape[1]

    def part(k):
        return pl.BlockSpec((H_C, tb, LANE), lambda b, c: (k, b * nc + c, 0))

    c2 = lambda b, c: (0, 0)
    c3 = lambda b, c: (0, 0, 0)
    og, cs = pl.pallas_call(
        functools.partial(_c_prompt_kernel, layer=layer), grid=(bsz, nc),
        in_specs=[part(0), part(1), part(2), part(3),
                  pl.BlockSpec((H_C, depth, DK_C), c3), pl.BlockSpec((H_C, 1, DV_C), c3),
                  pl.BlockSpec(w["gam"].shape, c2), pl.BlockSpec(w["lvl"].shape, c2)],
        out_specs=[pl.BlockSpec((H_C, tb, LANE), lambda b, c: (0, b * nc + c, 0)),
                   pl.BlockSpec((1, H_C, DK_C, DV_C), lambda b, c: (b, 0, 0, 0))],
        out_shape=[jax.ShapeDtypeStruct((H_C, m, DV_C), BF16),
                   jax.ShapeDtypeStruct((bsz, H_C, DK_C, DV_C), F32)],
        scratch_shapes=[pltpu.VMEM((H_C, DV_C, DK_C), F32)],
        compiler_params=_params("parallel", "arbitrary"),
        name="c_prompt",
    )(u, u, u, u, w["clb"], w["cnw"], w["gam"], w["lvl"])
    return og, cs


def _c_sample_kernel(q_ref, f_ref, v_ref, g_ref, s_ref, clb_ref, cnw_ref, og_ref, so_ref, *, layer):
    bb = SAMPLE_BB
    lane = lax.broadcasted_iota(jnp.int32, (DK_C, LANE), 1)
    first_rows = lax.broadcasted_iota(jnp.int32, (LANE, DV_C), 0) < bb
    for hd in range(H_C):
        lb = _hg_lower_bound(clb_ref[hd], layer)
        f, kk = _hg_gates(f_ref[hd], lb)
        f_t = _pad_rows_t(f)
        k_t = _pad_rows_t(kk)
        qs = q_ref[hd] * (DK_C ** -0.5)
        v = v_ref[hd]
        vpad = jnp.where(first_rows, jnp.tile(v, (LANE // bb, 1)), 0.0).astype(BF16)
        orows = []
        for i in range(bb):
            fcol = jnp.broadcast_to(f_t[:, i:i + 1], (DK_C, DV_C))
            kv = _dot(jnp.where(lane == i, k_t, 0.0).astype(BF16), vpad)
            sn = fcol * s_ref[i, hd] + kv
            so_ref[i, hd] = sn
            qrow = jnp.broadcast_to(qs[i:i + 1, :], (2 * SUBLANE, DK_C)).astype(BF16)
            orows.append(_dot(qrow, sn.astype(BF16))[0:1, :])
        o = jnp.concatenate(orows, axis=0)
        og_ref[hd] = _hg_out(o, g_ref[hd], cnw_ref[hd])


def _c_sample(u, s_c, w, layer):
    bsz = s_c.shape[0]
    bb = SAMPLE_BB
    assert bsz % bb == 0
    depth = w["clb"].shape[1]

    def part(k):
        return pl.BlockSpec((H_C, bb, LANE), lambda i: (k, i, 0))

    c3 = lambda i: (0, 0, 0)
    og, so = pl.pallas_call(
        functools.partial(_c_sample_kernel, layer=layer), grid=(bsz // bb,),
        in_specs=[part(0), part(1), part(2), part(3),
                  pl.BlockSpec((bb, H_C, DK_C, DV_C), lambda i: (i, 0, 0, 0)),
                  pl.BlockSpec((H_C, depth, DK_C), c3), pl.BlockSpec((H_C, 1, DV_C), c3)],
        out_specs=[pl.BlockSpec((H_C, bb, LANE), lambda i: (0, i, 0)),
                   pl.BlockSpec((bb, H_C, DK_C, DV_C), lambda i: (i, 0, 0, 0))],
        out_shape=[jax.ShapeDtypeStruct((H_C, bsz, DV_C), F32), jax.ShapeDtypeStruct(s_c.shape, F32)],
        compiler_params=_params("parallel"),
        name="c_sample",
    )(u, u, u, u, s_c, w["clb"], w["cnw"])
    return og, so


def _row(v, width=None):
    v = v.astype(F32).reshape(1, -1)
    if width is not None and v.shape[1] < width:
        v = jnp.pad(v, ((0, 0), (0, width - v.shape[1])))
    return v


def kernel(x_prompt, x_sample, p_prompt, p_sample, state_a_h, state_a_conv, state_b_ssm, state_b_conv, state_c,
           norm_w, norm_f, ab_w_in, a_conv_w, a_conv_b, a_w_r, a_b_r, a_w_i, a_b_i, a_lam, b_conv_w, b_conv_b,
           b_dt_bias, b_a_log, b_d, b_norm_w, ab_w_out, c_w_in, c_lb, c_norm_w, c_w_out, ple_proj, ple_gate):
    depth = norm_w.shape[0]
    bp, seq, _ = x_prompt.shape
    bs = x_sample.shape[0]
    hp = x_prompt.reshape(bp * seq, D_MODEL)
    hs = x_sample.reshape(bs, D_MODEL)
    pp = p_prompt.reshape(depth, bp * seq, D_PLE)
    ps = p_sample.reshape(depth, bs, D_PLE)
    gam, lvl = _hg_gamma(), _hg_level_table()
    clb = c_lb.astype(F32).reshape(depth, H_C, DK_C).transpose(1, 0, 2)

    ah_p, ac_p, bs_p, bc_p, c_p = [], [], [], [], []
    ah_s, ac_s, bs_s, bc_s, c_s = [], [], [], [], []
    for i in range(depth):
        j = i // 2
        final = i == depth - 1
        wg = (0.5 * ple_gate[i]).astype(BF16)
        wp = (0.5 * ple_proj[i]).astype(BF16)
        if i % 2 == 0:
            col_scale = jnp.concatenate([jnp.ones((D_A,), F32), jnp.full((D_A + D_B,), 0.5, F32),
                                         jnp.ones((CONV_DIM_B + H_B,), F32)])
            w_ab = jnp.pad(ab_w_in[j].T * col_scale[:, None], ((0, LANE - H_B), (0, 0))).astype(BF16)
            wo = ab_w_out[j].astype(BF16)
            w = dict(acw=a_conv_w[j].astype(F32), acb=_row(a_conv_b[j]),
                     wr=a_w_r[j].astype(BF16), br=_row(a_b_r[j]), wi=a_w_i[j].astype(BF16), bi=_row(a_b_i[j]),
                     lam=_row(a_lam[j]), bcw=0.5 * b_conv_w[j].astype(F32), bcb=0.5 * _row(b_conv_b[j]),
                     dtb=_row(b_dt_bias[j], LANE), alog=_row(b_a_log[j], LANE),
                     dexp=_row(jnp.repeat(b_d[j], HD_B)), bnw=_row(b_norm_w[j]))
            u, dtr = _in_proj(hp, norm_w[i], w_ab, extra_cols=True, w_t=True)
            hp, s1, s2, s3, s4 = _ab_prompt(u, dtr, hp, pp, i, wo, wg, wp, norm_f, bp, seq, w, final=final)
            ah_p.append(s1); ac_p.append(s2); bs_p.append(s3); bc_p.append(s4)
            u, dtr = _in_proj(hs, norm_w[i], w_ab, extra_cols=True, w_t=True)
            mix, s1, s2, s3, s4 = _ab_sample(u, dtr, state_a_h[j], state_a_conv[j], state_b_ssm[j],
                                             state_b_conv[j], w)
            ah_s.append(s1); ac_s.append(s2); bs_s.append(s3); bc_s.append(s4)
            hs = _out_proj(mix, hs, ps, i, wo, wg, wp, norm_f, head_major=False, final=final)
        else:
            col_scale = jnp.concatenate([jnp.ones((HK_C,), F32), jnp.full((HK_C,), 0.5, F32),
                                         jnp.ones((D_C,), F32), jnp.full((D_C,), 0.5, F32)])
            w_in = (c_w_in[j] * col_scale).astype(BF16)
            wo = c_w_out[j].astype(BF16)
            w = dict(clb=clb, cnw=c_norm_w[j].astype(F32).reshape(H_C, 1, DV_C), gam=gam, lvl=lvl)
            u = _in_proj(hp, norm_w[i], w_in, head_major=True)
            og, s1 = _c_prompt(u, bp, seq, w, i)
            c_p.append(s1)
            hp = _out_proj(og, hp, pp, i, wo, wg, wp, norm_f, head_major=True, final=final)
            u = _in_proj(hs, norm_w[i], w_in, head_major=True)
            og, s1 = _c_sample(u, state_c[j], w, i)
            c_s.append(s1)
            hs = _out_proj(og, hs, ps, i, wo, wg, wp, norm_f, head_major=True, final=final)
    return (hp.reshape(bp, seq, D_MODEL), hs.reshape(bs, 1, D_MODEL),
            jnp.stack(ah_p), jnp.stack(ac_p), jnp.stack(bs_p), jnp.stack(bc_p), jnp.stack(c_p),
            jnp.stack(ah_s), jnp.stack(ac_s), jnp.stack(bs_s), jnp.stack(bc_s), jnp.stack(c_s))
```

```python
import functools

import jax
import jax.numpy as jnp
from jax import lax
from jax.experimental import pallas as pl
from jax.experimental.pallas import tpu as pltpu

F32 = jnp.float32
BF16 = jnp.bfloat16

D_MODEL = 1024
D_PLE = 256
EPS = 1e-6
CONV_W = 4
D_A = D_MODEL
A_BLOCKS = 8
A_BLK = D_A // A_BLOCKS
LRU_C = 8.0
D_B = D_MODEL
HD_B = 64
H_B = D_B // HD_B
N_B = 128
G_B = 2
CONV_DIM_B = D_B + 2 * G_B * N_B
D_C = 2 * D_MODEL
H_C = 16
DK_C = 128
DV_C = D_C // H_C
HK_C = H_C * DK_C
AB_MAIN = 2 * D_A + D_B + CONV_DIM_B
IN_C = 2 * HK_C + 2 * D_C

LANE = 128
SUBLANE = 8
BF16_ROWS = 16
LOG2_E = 1.4426950408889634
VMEM_LIMIT = 52 * 1024 * 1024

PROJ_TM = 2048
PROJ_MAX_TN = 1024
SSD_CHUNK = 128
AB_CHUNKS = 1
HG_CHUNK = 64
HG_LEVELS = 6
HG_MXU_LEVELS = 3
HG_BLOCK = 512
HG_UNROLL = 16
SAMPLE_BB = 8

_DN_TR = (((1,), (1,)), ((), ()))
_DN_TL = (((0,), (0,)), ((), ()))


def _dot(a, b):
    return jnp.dot(a, b, preferred_element_type=F32)


def _dot_tr(a, b):
    return lax.dot_general(a, b, _DN_TR, preferred_element_type=F32)


def _dot_tl(a, b):
    return lax.dot_general(a, b, _DN_TL, preferred_element_type=F32)


def _silu_half(x_half):
    return x_half + x_half * jnp.tanh(x_half)


def _rmsnorm(x, w):
    return x * lax.rsqrt(jnp.mean(x * x, axis=-1, keepdims=True) + EPS) * w


def _params(*sem):
    return pltpu.CompilerParams(dimension_semantics=sem, vmem_limit_bytes=VMEM_LIMIT)


def _in_proj_kernel(x_ref, nw_ref, w_ref, *rest, has_extra, head_major, w_t):
    if has_extra:
        wx_ref, o_ref, ox_ref, xn_ref = rest
    else:
        o_ref, xn_ref = rest
    dot = _dot_tr if w_t else _dot

    @pl.when(pl.program_id(1) == 0)
    def _():
        xn_ref[...] = _rmsnorm(x_ref[...], nw_ref[...]).astype(BF16)
        if has_extra:
            ox_ref[...] = dot(xn_ref[...], wx_ref[...])

    acc = dot(xn_ref[...], w_ref[...])
    if head_major:
        for k in range(acc.shape[1] // LANE):
            o_ref[k] = acc[:, k * LANE:(k + 1) * LANE].astype(o_ref.dtype)
    else:
        o_ref[...] = acc


def _in_proj(x, nw, w, *, extra_cols=False, head_major=False, w_t=False, head_major_dtype=F32):
    m, k = x.shape
    n = w.shape[0 if w_t else 1] - (LANE if extra_cols else 0)
    tm = min(m, PROJ_TM)
    tn = next(c for c in range(PROJ_MAX_TN, 0, -LANE) if n % c == 0)
    assert m % tm == 0 and n % tn == 0 and tn % LANE == 0
    grid = (m // tm, n // tn)
    wspec = (lambda cols, idx: pl.BlockSpec((cols, k), lambda i, j: (idx(j), 0))) if w_t else \
            (lambda cols, idx: pl.BlockSpec((k, cols), lambda i, j: (0, idx(j))))
    in_specs = [pl.BlockSpec((tm, k), lambda i, j: (i, 0)),
                pl.BlockSpec((1, k), lambda i, j: (0, 0)),
                wspec(tn, lambda j: j)]
    args = [x, nw.reshape(1, k), w]
    if head_major:
        out_shape = [jax.ShapeDtypeStruct((n // LANE, m, LANE), head_major_dtype)]
        out_specs = [pl.BlockSpec((tn // LANE, tm, LANE), lambda i, j: (j, i, 0))]
    else:
        out_shape = [jax.ShapeDtypeStruct((m, n), F32)]
        out_specs = [pl.BlockSpec((tm, tn), lambda i, j: (i, j))]
    if extra_cols:
        in_specs.append(wspec(LANE, lambda j: n // LANE))
        args.append(w)
        out_shape.append(jax.ShapeDtypeStruct((m, LANE), F32))
        out_specs.append(pl.BlockSpec((tm, LANE), lambda i, j: (i, 0)))
    outs = pl.pallas_call(
        functools.partial(_in_proj_kernel, has_extra=extra_cols, head_major=head_major, w_t=w_t),
        grid=grid, in_specs=in_specs, out_specs=out_specs, out_shape=out_shape,
        scratch_shapes=[pltpu.VMEM((tm, k), BF16)],
        compiler_params=_params("parallel", "arbitrary"),
        name="in_proj",
    )(*args)
    return outs if extra_cols else outs[0]


def _residual_ple(h, p_ref, wg_ref, wp_ref, nf_ref, final):
    gate_t = jnp.tanh(_dot(h.astype(BF16), wg_ref[...]))
    pe_half = _dot(p_ref[...].astype(BF16), wp_ref[...])
    h = h + pe_half + pe_half * gate_t
    return _rmsnorm(h, nf_ref[...]) if final else h


def _out_proj_kernel(mix_ref, h_ref, p_ref, wo_ref, wg_ref, wp_ref, nf_ref, o_ref, *, head_major, final):
    if head_major:
        mix = jnp.concatenate([mix_ref[k] for k in range(mix_ref.shape[0])], axis=-1)
    else:
        mix = mix_ref[...]
    h = h_ref[...] + _dot(mix.astype(BF16), wo_ref[...])
    o_ref[...] = _residual_ple(h, p_ref, wg_ref, wp_ref, nf_ref, final)


def _out_proj(mix, h, p, layer, wo, wg, wp, nf, *, head_major, final):
    m, d = h.shape
    tm = min(m, 512)
    assert m % tm == 0
    if head_major:
        mix_spec = pl.BlockSpec((mix.shape[0], tm, LANE), lambda i: (0, i, 0))
    else:
        mix_spec = pl.BlockSpec((tm, mix.shape[1]), lambda i: (i, 0))
    const = lambda i: (0, 0)
    return pl.pallas_call(
        functools.partial(_out_proj_kernel, head_major=head_major, final=final),
        grid=(m // tm,),
        in_specs=[mix_spec,
                  pl.BlockSpec((tm, d), lambda i: (i, 0)),
                  pl.BlockSpec((None, tm, p.shape[2]), lambda i: (layer, i, 0)),
                  pl.BlockSpec(wo.shape, const), pl.BlockSpec(wg.shape, const),
                  pl.BlockSpec(wp.shape, const), pl.BlockSpec((1, d), const)],
        out_specs=pl.BlockSpec((tm, d), lambda i: (i, 0)),
        out_shape=jax.ShapeDtypeStruct((m, d), F32),
        compiler_params=_params("parallel"),
        name="out_proj",
    )(mix, h, p, wo, wg, wp, nf.reshape(1, d))


def _lru_gates(xc, wr_ref, br_ref, wi_ref, bi_ref, lam_ref):
    xcb = xc.astype(BF16)
    r_parts, i_parts = [], []
    for k in range(A_BLOCKS):
        xk = xcb[:, k * A_BLK:(k + 1) * A_BLK]
        r_parts.append(_dot(xk, wr_ref[k]))
        i_parts.append(_dot(xk, wi_ref[k]))
    r = jax.nn.sigmoid(jnp.concatenate(r_parts, axis=-1) + br_ref[...])
    gi = jax.nn.sigmoid(jnp.concatenate(i_parts, axis=-1) + bi_ref[...])
    log_a = (-LRU_C) * r * jax.nn.softplus(-lam_ref[...])
    a = jnp.exp(log_a)
    v = jnp.tanh(-log_a) * (a * a + 1.0)
    u = jnp.where(v > 0.0, v * lax.rsqrt(v), 0.0) * (gi * xc)
    return a, u


def _group_norm_gate(y, bx, z_half, dexp_ref, bnw_ref):
    y = (y + dexp_ref[...] * bx) * _silu_half(z_half)
    gw = D_B // G_B
    parts = []
    for g in range(G_B):
        yg = y[:, g * gw:(g + 1) * gw]
        parts.append(yg * lax.rsqrt(jnp.mean(yg * yg, axis=-1, keepdims=True) + EPS))
    return jnp.concatenate(parts, axis=-1) * bnw_ref[...]


def _ab_prompt_kernel(ax_ref, ag_ref, z_ref, xbc_ref, dt_ref,
                      acw_ref, acb_ref, wr_ref, br_ref, wi_ref, bi_ref, lam_ref,
                      bcw_ref, bcb_ref, dtb_ref, alog_ref, dexp_ref, bnw_ref,
                      hin_ref, p_ref, wo_ref, wg_ref, wp_ref, nf_ref,
                      hout_ref, *state_refs, final):
    for k in range(AB_CHUNKS):
        r = pl.ds(k * SSD_CHUNK, SSD_CHUNK)
        _ab_prompt_chunk(ax_ref.at[r], ag_ref.at[r], z_ref.at[r], xbc_ref.at[r], dt_ref.at[r],
                         acw_ref, acb_ref, wr_ref, br_ref, wi_ref, bi_ref, lam_ref,
                         bcw_ref, bcb_ref, dtb_ref, alog_ref, dexp_ref, bnw_ref,
                         hin_ref.at[r], p_ref.at[r], wo_ref, wg_ref, wp_ref, nf_ref,
                         hout_ref.at[r], *state_refs, final=final, first=k == 0, last=k == AB_CHUNKS - 1)


def _ab_prompt_chunk(ax_ref, ag_ref, z_ref, xbc_ref, dt_ref,
                     acw_ref, acb_ref, wr_ref, br_ref, wi_ref, bi_ref, lam_ref,
                     bcw_ref, bcb_ref, dtb_ref, alog_ref, dexp_ref, bnw_ref,
                     hin_ref, p_ref, wo_ref, wg_ref, wp_ref, nf_ref,
                     hout_ref, ah_ref, ac_ref, bs_ref, bc_ref,
                     xpa_ref, xpb_ref, h_ref, s_ref, *, final, first, last):
    c = pl.program_id(1)
    t = SSD_CHUNK
    ntile = t // SUBLANE

    if first:
        @pl.when(c == 0)
        def _():
            xpa_ref[...] = jnp.zeros_like(xpa_ref)
            xpb_ref[...] = jnp.zeros_like(xpb_ref)
            h_ref[...] = jnp.zeros_like(h_ref)
            s_ref[...] = jnp.zeros_like(s_ref)

    def tiles(x):
        return [x[i * SUBLANE:(i + 1) * SUBLANE, :] for i in range(ntile)]

    def conv(x, tail_ref, w_ref, b_ref):
        sub = lax.broadcasted_iota(jnp.int32, (SUBLANE, x.shape[1]), 0)
        xt = [tail_ref[...]] + tiles(x)
        taps = [jnp.broadcast_to(w_ref[k:k + 1, :], (SUBLANE, x.shape[1])) for k in range(CONV_W)]
        bias = jnp.broadcast_to(b_ref[...], (SUBLANE, x.shape[1]))
        acc = [bias + taps[CONV_W - 1] * xt[i + 1] for i in range(ntile)]
        for s in range(1, CONV_W):
            wk = taps[CONV_W - 1 - s]
            for i in range(ntile):
                merged = jnp.where(sub >= SUBLANE - s, xt[i], xt[i + 1])
                acc[i] = acc[i] + wk * pltpu.roll(merged, s, 0)
        tail_ref[...] = xt[ntile]
        return jnp.concatenate(acc, axis=0)

    ax = ax_ref[...]
    xc = conv(ax, xpa_ref, acw_ref, acb_ref)
    a, u = _lru_gates(xc, wr_ref, br_ref, wi_ref, bi_ref, lam_ref)
    sub = lax.broadcasted_iota(jnp.int32, (SUBLANE, D_A), 0)
    at, ut = tiles(a), tiles(u)
    step = 1
    while step < SUBLANE:
        m = sub >= step
        for i in range(ntile):
            ut[i] = jnp.where(m, at[i] * pltpu.roll(ut[i], step, 0) + ut[i], ut[i])
            at[i] = jnp.where(m, at[i] * pltpu.roll(at[i], step, 0), at[i])
        step *= 2
    carry = h_ref[0:1, :]
    hs = []
    for i in range(ntile):
        hs.append(ut[i] + at[i] * carry)
        carry = hs[i][SUBLANE - 1:SUBLANE, :]
    h = jnp.concatenate(hs, axis=0)
    h_ref[0:1, :] = carry
    a_out = (h * _silu_half(ag_ref[...])).astype(BF16)

    xb = xbc_ref[...]
    xbc = _silu_half(conv(xb, xpb_ref, bcw_ref, bcb_ref))
    bx = xbc[:, 0:D_B]
    bxb = bx.astype(BF16)
    dt = jax.nn.softplus(dt_ref[...] + dtb_ref[...])
    adt = dt * (-jnp.exp(alog_ref[...]))
    ti = lax.broadcasted_iota(jnp.int32, (t, t), 0)
    si = lax.broadcasted_iota(jnp.int32, (t, t), 1)
    causal = ti >= si
    acs = jnp.dot(causal.astype(F32), adt, preferred_element_type=F32,
                  precision=lax.Precision.HIGHEST)
    a_last = acs[t - 1:t, :]
    wq = jnp.exp(a_last - acs) * dt
    eacs = jnp.exp(acs)
    ealast = jnp.exp(a_last)
    acs_t = acs.T
    dt_t = dt.T
    lane = lax.broadcasted_iota(jnp.int32, (t, LANE), 1)
    rowi = lax.broadcasted_iota(jnp.int32, (LANE, N_B), 0)
    hpg = H_B // G_B
    ys = []
    cb = None
    for j in range(H_B // 2):
        g = (2 * j) // hpg
        bg = xbc[:, D_B + g * N_B:D_B + (g + 1) * N_B]
        cg = xbc[:, D_B + G_B * N_B + g * N_B:D_B + G_B * N_B + (g + 1) * N_B]
        if (2 * j) % hpg == 0:
            cb = _dot_tr(cg.astype(BF16), bg.astype(BF16))
        xpair = bxb[:, j * LANE:(j + 1) * LANE]
        sp = s_ref[j]
        spb = sp.astype(BF16)
        y_h, up_h = [], []
        for hh in range(2):
            hd = 2 * j + hh
            seg = jnp.broadcast_to(acs[:, hd:hd + 1], (t, t)) - jnp.broadcast_to(acs_t[hd:hd + 1, :], (t, t))
            lmat = jnp.exp(jnp.where(causal, seg, -1e30))
            mmat = (cb * lmat * jnp.broadcast_to(dt_t[hd:hd + 1, :], (t, t))).astype(BF16)
            ec = (jnp.broadcast_to(eacs[:, hd:hd + 1], (t, N_B)) * cg).astype(BF16)
            y_h.append(_dot(mmat, xpair) + _dot_tr(ec, spb))
            bw = (bg * jnp.broadcast_to(wq[:, hd:hd + 1], (t, N_B))).astype(BF16)
            up_h.append(_dot_tl(xpair, bw))
        ys.append(jnp.where(lane < HD_B, y_h[0], y_h[1]))
        dec = jnp.where(rowi < HD_B,
                        jnp.broadcast_to(ealast[:, 2 * j:2 * j + 1], (LANE, N_B)),
                        jnp.broadcast_to(ealast[:, 2 * j + 1:2 * j + 2], (LANE, N_B)))
        s_ref[j] = dec * sp + jnp.where(rowi < HD_B, up_h[0], up_h[1])
    y = jnp.concatenate(ys, axis=-1)
    b_out = _group_norm_gate(y, bx, z_ref[...], dexp_ref, bnw_ref).astype(BF16)

    hres = hin_ref[...] + _dot(a_out, wo_ref[0:D_A, :]) + _dot(b_out, wo_ref[D_A:D_A + D_B, :])
    hout_ref[...] = _residual_ple(hres, p_ref, wg_ref, wp_ref, nf_ref, final)

    if last:
        @pl.when(c == pl.num_programs(1) - 1)
        def _():
            ah_ref[0] = h[t - 1:t, :]
            ac_ref[0] = ax[t - (CONV_W - 1):t, :]
            bc_ref[0] = xb[t - (CONV_W - 1):t, :]
            for j in range(H_B // 2):
                sj = s_ref[j]
                bs_ref[0, 2 * j] = sj[0:HD_B, :]
                bs_ref[0, 2 * j + 1] = sj[HD_B:2 * HD_B, :]


def _ab_prompt(u, dtr, h, p, layer, wo, wg, wp, nf, bsz, seq, w, *, final):
    t = AB_CHUNKS * SSD_CHUNK
    assert seq % t == 0
    nc = seq // t
    m = bsz * seq
    rows = lambda b, c: b * nc + c
    cvec = lambda b, c: (0, 0)
    c3 = lambda b, c: (0, 0, 0)
    in_specs = [
        pl.BlockSpec((t, D_A), lambda b, c: (rows(b, c), 0)),
        pl.BlockSpec((t, D_A), lambda b, c: (rows(b, c), 1)),
        pl.BlockSpec((t, D_B), lambda b, c: (rows(b, c), 2)),
        pl.BlockSpec((t, CONV_DIM_B), lambda b, c: (rows(b, c), 2)),
        pl.BlockSpec((t, LANE), lambda b, c: (rows(b, c), 0)),
        pl.BlockSpec((CONV_W, D_A), cvec), pl.BlockSpec((1, D_A), cvec),
        pl.BlockSpec((A_BLOCKS, A_BLK, A_BLK), c3), pl.BlockSpec((1, D_A), cvec),
        pl.BlockSpec((A_BLOCKS, A_BLK, A_BLK), c3), pl.BlockSpec((1, D_A), cvec),
        pl.BlockSpec((1, D_A), cvec),
        pl.BlockSpec((CONV_W, CONV_DIM_B), cvec), pl.BlockSpec((1, CONV_DIM_B), cvec),
        pl.BlockSpec((1, LANE), cvec), pl.BlockSpec((1, LANE), cvec),
        pl.BlockSpec((1, D_B), cvec), pl.BlockSpec((1, D_B), cvec),
        pl.BlockSpec((t, D_MODEL), lambda b, c: (rows(b, c), 0)),
        pl.BlockSpec((None, t, p.shape[2]), lambda b, c: (layer, rows(b, c), 0)),
        pl.BlockSpec(wo.shape, cvec), pl.BlockSpec(wg.shape, cvec), pl.BlockSpec(wp.shape, cvec),
        pl.BlockSpec((1, D_MODEL), cvec),
    ]
    out_shape = [
        jax.ShapeDtypeStruct((m, D_MODEL), F32),
        jax.ShapeDtypeStruct((bsz, 1, D_A), F32),
        jax.ShapeDtypeStruct((bsz, CONV_W - 1, D_A), F32),
        jax.ShapeDtypeStruct((bsz, H_B, HD_B, N_B), F32),
        jax.ShapeDtypeStruct((bsz, CONV_W - 1, CONV_DIM_B), F32),
    ]
    out_specs = [
        pl.BlockSpec((t, D_MODEL), lambda b, c: (rows(b, c), 0)),
        pl.BlockSpec((1, 1, D_A), lambda b, c: (b, 0, 0)),
        pl.BlockSpec((1, CONV_W - 1, D_A), lambda b, c: (b, 0, 0)),
        pl.BlockSpec((1, H_B, HD_B, N_B), lambda b, c: (b, 0, 0, 0)),
        pl.BlockSpec((1, CONV_W - 1, CONV_DIM_B), lambda b, c: (b, 0, 0)),
    ]
    hout, ah, ac, bs, bc = pl.pallas_call(
        functools.partial(_ab_prompt_kernel, final=final),
        grid=(bsz, nc), in_specs=in_specs, out_specs=out_specs, out_shape=out_shape,
        scratch_shapes=[pltpu.VMEM((SUBLANE, D_A), F32), pltpu.VMEM((SUBLANE, CONV_DIM_B), F32),
                        pltpu.VMEM((SUBLANE, D_A), F32), pltpu.VMEM((H_B // 2, 2 * HD_B, N_B), F32)],
        compiler_params=_params("parallel", "arbitrary"),
        name="ab_prompt",
    )(u, u, u, u, dtr, w["acw"], w["acb"], w["wr"], w["br"], w["wi"], w["bi"], w["lam"],
      w["bcw"], w["bcb"], w["dtb"], w["alog"], w["dexp"], w["bnw"],
      h, p, wo, wg, wp, nf.reshape(1, D_MODEL))
    return hout, ah.reshape(bsz, D_A), ac, bs, bc


def _ab_sample_rows_kernel(ax_ref, ag_ref, xbc_ref, dt_ref, sah_ref, sac_ref, sbc_ref,
                           acw_ref, acb_ref, wr_ref, br_ref, wi_ref, bi_ref, lam_ref,
                           bcw_ref, bcb_ref, dtb_ref,
                           aout_ref, ah_ref, ac_ref, bc_ref, xact_ref, dts_ref):
    def conv1(x, buf_ref, w_ref, b_ref, nbuf_ref, width):
        y = b_ref[...] + w_ref[CONV_W - 1:CONV_W, :] * x
        for k in range(CONV_W - 1):
            y = y + w_ref[k:k + 1, :] * buf_ref[:, k * width:(k + 1) * width]
        for k in range(CONV_W - 2):
            nbuf_ref[:, k * width:(k + 1) * width] = buf_ref[:, (k + 1) * width:(k + 2) * width]
        nbuf_ref[:, (CONV_W - 2) * width:(CONV_W - 1) * width] = x
        return y

    xc = conv1(ax_ref[...], sac_ref, acw_ref, acb_ref, ac_ref, D_A)
    a, u = _lru_gates(xc, wr_ref, br_ref, wi_ref, bi_ref, lam_ref)
    h = a * sah_ref[...] + u
    ah_ref[...] = h
    aout_ref[...] = h * _silu_half(ag_ref[...])
    xact_ref[...] = _silu_half(conv1(xbc_ref[...], sbc_ref, bcw_ref, bcb_ref, bc_ref, CONV_DIM_B))
    dts_ref[...] = jax.nn.softplus(dt_ref[...] + dtb_ref[...])


def _pad_rows_t(x):
    pad = jnp.zeros((LANE - x.shape[0], x.shape[1]), F32)
    return jnp.concatenate([x, pad], axis=0).T


def _ab_sample_state_kernel(s_ref, xact_ref, dts_ref, z_ref, alog_ref, dexp_ref, bnw_ref,
                            so_ref, bout_ref, y_ref):
    bb = SAMPLE_BB
    xact = xact_ref[...]
    bx = xact[:, 0:D_B]
    dts = dts_ref[...]
    dec_t = _pad_rows_t(jnp.exp(dts * (-jnp.exp(alog_ref[...]))))
    dts_t = _pad_rows_t(dts)
    hpg = H_B // G_B
    for j in range(H_B // 2):
        g = (2 * j) // hpg
        xt = _pad_rows_t(bx[:, j * LANE:(j + 1) * LANE])
        dtp = jnp.concatenate([jnp.broadcast_to(dts_t[2 * j:2 * j + 1, :], (HD_B, LANE)),
                               jnp.broadcast_to(dts_t[2 * j + 1:2 * j + 2, :], (HD_B, LANE))], axis=0)
        xdt = xt * dtp
        for i in range(bb):
            brow = jnp.broadcast_to(xact[i:i + 1, D_B + g * N_B:D_B + (g + 1) * N_B], (2 * HD_B, N_B))
            crow = jnp.broadcast_to(
                xact[i:i + 1, D_B + G_B * N_B + g * N_B:D_B + G_B * N_B + (g + 1) * N_B], (2 * SUBLANE, N_B))
            upd = jnp.broadcast_to(xdt[:, i:i + 1], (2 * HD_B, N_B)) * brow
            news = []
            for hh in range(2):
                hd = 2 * j + hh
                dec = jnp.broadcast_to(dec_t[hd:hd + 1, i:i + 1], (HD_B, N_B))
                sn = dec * s_ref[i, hd] + upd[hh * HD_B:(hh + 1) * HD_B, :]
                so_ref[i, hd] = sn
                news.append(sn)
            spair = jnp.concatenate(news, axis=0).astype(BF16)
            yrow = _dot_tr(crow.astype(BF16), spair)
            y_ref[i:i + 1, j * LANE:(j + 1) * LANE] = yrow[0:1, :]
    bout_ref[...] = _group_norm_gate(y_ref[...], bx, z_ref[...], dexp_ref, bnw_ref)


def _ab_sample(u, dtr, s_ah, s_ac, s_bs, s_bc, w):
    bsz = u.shape[0]
    full = lambda shape: pl.BlockSpec(shape, lambda i: tuple(0 for _ in shape))
    cw = CONV_W - 1
    aout, ah, ac, bc, xact, dts = pl.pallas_call(
        _ab_sample_rows_kernel, grid=(1,),
        in_specs=[pl.BlockSpec((bsz, D_A), lambda i: (0, 0)), pl.BlockSpec((bsz, D_A), lambda i: (0, 1)),
                  pl.BlockSpec((bsz, CONV_DIM_B), lambda i: (0, 2)), full((bsz, LANE)),
                  full((bsz, D_A)), full((bsz, cw * D_A)), full((bsz, cw * CONV_DIM_B)),
                  full((CONV_W, D_A)), full((1, D_A)),
                  full((A_BLOCKS, A_BLK, A_BLK)), full((1, D_A)),
                  full((A_BLOCKS, A_BLK, A_BLK)), full((1, D_A)), full((1, D_A)),
                  full((CONV_W, CONV_DIM_B)), full((1, CONV_DIM_B)), full((1, LANE))],
        out_specs=[full((bsz, D_A)), full((bsz, D_A)), full((bsz, cw * D_A)), full((bsz, cw * CONV_DIM_B)),
                   full((bsz, CONV_DIM_B)), full((bsz, LANE))],
        out_shape=[jax.ShapeDtypeStruct((bsz, D_A), F32), jax.ShapeDtypeStruct((bsz, D_A), F32),
                   jax.ShapeDtypeStruct((bsz, cw * D_A), F32), jax.ShapeDtypeStruct((bsz, cw * CONV_DIM_B), F32),
                   jax.ShapeDtypeStruct((bsz, CONV_DIM_B), F32), jax.ShapeDtypeStruct((bsz, LANE), F32)],
        compiler_params=_params("arbitrary"),
        name="ab_sample_rows",
    )(u, u, u, dtr, s_ah, s_ac.reshape(bsz, cw * D_A), s_bc.reshape(bsz, cw * CONV_DIM_B),
      w["acw"], w["acb"], w["wr"], w["br"], w["wi"], w["bi"], w["lam"], w["bcw"], w["bcb"], w["dtb"])

    bb = SAMPLE_BB
    assert bsz % bb == 0
    cvec = lambda i: (0, 0)
    bs, bout = pl.pallas_call(
        _ab_sample_state_kernel, grid=(bsz // bb,),
        in_specs=[pl.BlockSpec((bb, H_B, HD_B, N_B), lambda i: (i, 0, 0, 0)),
                  pl.BlockSpec((bb, CONV_DIM_B), lambda i: (i, 0)),
                  pl.BlockSpec((bb, LANE), lambda i: (i, 0)),
                  pl.BlockSpec((bb, D_B), lambda i: (i, 2)),
                  pl.BlockSpec((1, LANE), cvec), pl.BlockSpec((1, D_B), cvec), pl.BlockSpec((1, D_B), cvec)],
        out_specs=[pl.BlockSpec((bb, H_B, HD_B, N_B), lambda i: (i, 0, 0, 0)),
                   pl.BlockSpec((bb, D_B), lambda i: (i, 0))],
        out_shape=[jax.ShapeDtypeStruct(s_bs.shape, F32), jax.ShapeDtypeStruct((bsz, D_B), F32)],
        scratch_shapes=[pltpu.VMEM((bb, D_B), F32)],
        compiler_params=_params("parallel"),
        name="ab_sample_state",
    )(s_bs, xact, dts, u, w["alog"], w["dexp"], w["bnw"])
    mix = jnp.concatenate([aout, bout], axis=-1)
    return mix, ah, ac.reshape(bsz, cw, D_A), bs, bc.reshape(bsz, cw, CONV_DIM_B)


def _hg_lower_bound(clb, layer):
    mx = jnp.max(clb, axis=0, keepdims=True)
    ex = jnp.exp(clb - mx)
    return jnp.sum(ex[1:layer + 1], axis=0, keepdims=True) / jnp.sum(ex, axis=0, keepdims=True)


def _hg_gates(fx_half, lb):
    f = 0.5 * (1.0 + lb) + (0.5 * (1.0 - lb)) * jnp.tanh(fx_half)
    return f, 1.0 - f


def _hg_out(o, gate_half, cnw):
    return o * lax.rsqrt(jnp.mean(o * o, axis=-1, keepdims=True) + EPS) * cnw * (gate_half + gate_half * jnp.tanh(gate_half))


def _hg_gamma():
    import numpy as np
    q = HG_CHUNK
    t = np.arange(q)[:, None]
    tau = np.arange(q)[None, :]
    mats = [(tau <= t)]
    for l in range(1, HG_MXU_LEVELS):
        w = 1 << l
        ref = (t // (2 * w)) * (2 * w) + w - 1
        upper = (t % (2 * w)) >= w
        mats.append(np.where(upper, (tau > ref) & (tau <= t), (tau > t) & (tau <= ref)))
    gam = np.concatenate(mats, axis=0).astype(np.float32)
    return jnp.asarray(np.concatenate([gam, gam], axis=1), dtype=BF16)


def _hg_level_table():
    import numpy as np
    q = HG_CHUNK
    t = np.arange(q)[:, None]
    s = np.arange(q)[None, :]
    x = t ^ s
    lvl = np.floor(np.log2(np.maximum(x, 1))).astype(np.int32)
    return jnp.asarray(np.where(t > s, lvl, -1).astype(np.int32))


def _c_prompt_kernel(q_ref, f_ref, v_ref, g_ref, clb_ref, cnw_ref, gam_ref, lvl_ref,
                     og_ref, cs_ref, st_ref, *, layer):
    c = pl.program_id(1)
    last = pl.num_programs(1) - 1
    qc = HG_CHUNK

    @pl.when(c == 0)
    def _():
        st_ref[...] = jnp.zeros_like(st_ref)

    gam = gam_ref[...]
    ntile = qc // SUBLANE
    sub = lax.broadcasted_iota(jnp.int32, (SUBLANE, DK_C), 0)
    sub_levels = HG_MXU_LEVELS
    sub_upper = [(sub & (1 << l)) != 0 for l in range(sub_levels)]

    def tiles(x):
        return [x[i * SUBLANE:(i + 1) * SUBLANE, :] for i in range(ntile)]

    def gate_split(hd, rows):
        f, kk = _hg_gates(f_ref[hd, rows, :].astype(F32), _hg_lower_bound(clb_ref[hd], layer))
        g = jnp.log(f) * LOG2_E
        g1 = g.astype(BF16)
        g2 = (g - g1.astype(F32)).astype(BF16)
        return (f, kk), jnp.concatenate([g1, g2], axis=0)

    def scores(hd, rows, fk, sums):
        f, kk = fk
        qh = q_ref[hd, rows, :].astype(F32) * (DK_C ** -0.5)
        bcum = sums[0:qc]
        st = st_ref[hd]
        o = _dot_tr((qh * jnp.exp2(bcum)).astype(BF16), st.astype(BF16))
        qt, kt, ft, bt = tiles(qh), tiles(kk), tiles(f), tiles(bcum)
        prods = []
        for l in range(HG_LEVELS):
            if l == 0:
                xt = [jnp.where(sub_upper[0], qt[i] * ft[i], kt[i]) for i in range(ntile)]
            elif l < HG_MXU_LEVELS:
                dec = tiles(jnp.exp2(sums[l * qc:(l + 1) * qc]))
                xt = [jnp.where(sub_upper[l], qt[i], kt[i]) * dec[i] for i in range(ntile)]
            else:
                wt = 1 << (l - HG_MXU_LEVELS)
                xt = []
                for blk in range(0, ntile, 2 * wt):
                    ref = (blk + wt) * SUBLANE - 1
                    bref = jnp.broadcast_to(bcum[ref:ref + 1, :], (SUBLANE, DK_C))
                    xt += [kt[i] * jnp.exp2(bref - bt[i]) for i in range(blk, blk + wt)]
                    xt += [qt[i] * jnp.exp2(bt[i] - bref) for i in range(blk + wt, blk + 2 * wt)]
            x = jnp.concatenate(xt, axis=0).astype(BF16)
            half = (1 << l) // BF16_ROWS
            if half == 0:
                p = tiles(_dot_tr(x, x))
                prods.append({i: p[i] for i in range(ntile)})
            else:
                ups = [r for r in range(qc // BF16_ROWS) if (r // half) & 1]
                pu = _dot_tr(jnp.concatenate([x[r * BF16_ROWS:(r + 1) * BF16_ROWS, :] for r in ups], axis=0), x)
                tpr = BF16_ROWS // SUBLANE
                prods.append({r * tpr + k: pu[(n * tpr + k) * SUBLANE:(n * tpr + k + 1) * SUBLANE, :]
                              for n, r in enumerate(ups) for k in range(tpr)})
        return qh, st, o, prods

    def level_masks():
        masks = {}
        for i in range(ntile):
            lv = lvl_ref[i * SUBLANE:(i + 1) * SUBLANE, :]
            for l in range(HG_LEVELS):
                if l < sub_levels or (i >> (l - sub_levels)) & 1:
                    masks[i, l] = lv == l
        return masks

    def combine(hd, rows, kk, bcum, qh, st, o, prods, masks):
        arows = []
        for i in range(ntile):
            a = jnp.zeros((SUBLANE, qc), F32)
            for l in range(HG_LEVELS):
                if (i, l) in masks:
                    a = jnp.where(masks[i, l], prods[l][i], a)
            arows.append(a)
        amat = jnp.concatenate(arows, axis=0)
        vb = v_ref[hd, rows, :].astype(BF16)
        o = o + _dot(amat.astype(BF16), vb) + jnp.sum(qh * kk, axis=-1, keepdims=True) * vb.astype(F32)
        blast = bcum[qc - 1:qc, :]
        kdec = (kk * jnp.exp2(blast - bcum)).astype(BF16)
        st_ref[hd] = st * jnp.exp2(blast) + _dot_tl(vb, kdec)
        return o

    nchunk = HG_BLOCK // qc

    def body(idx, carry):
        hg = idx // nchunk
        rows = pl.ds(pl.multiple_of((idx % nchunk) * qc, qc), qc)
        heads = [hg * HG_UNROLL + k for k in range(HG_UNROLL)]
        gs = [gate_split(hd, rows) for hd in heads]
        sums = _dot(gam, jnp.concatenate([s for _, s in gs], axis=1))
        sums = [sums[:, k * DK_C:(k + 1) * DK_C] for k in range(HG_UNROLL)]
        sc = [scores(hd, rows, gs[k][0], sums[k]) for k, hd in enumerate(heads)]
        masks = level_masks()
        outs = [combine(hd, rows, gs[k][0][1], sums[k][0:qc], *sc[k], masks) for k, hd in enumerate(heads)]
        for k, hd in enumerate(heads):
            og_ref[hd, rows, :] = _hg_out(outs[k], g_ref[hd, rows, :].astype(F32), cnw_ref[hd]).astype(BF16)
        return carry

    lax.fori_loop(0, (H_C // HG_UNROLL) * nchunk, body, 0)

    @pl.when(c == last)
    def _():
        for hd in range(H_C):
            cs_ref[0, hd] = st_ref[hd].T


def _c_prompt(u, bsz, seq, w, layer):
    tb = HG_BLOCK
    assert seq % tb == 0 and tb % HG_CHUNK == 0 and (1 << HG_LEVELS) == HG_CHUNK
    nc = seq // tb
    m = bsz * seq
    depth = w["clb"].shape[1]

    def part(k):
        return pl.BlockSpec((H_C, tb, LANE), lambda b, c: (k, b * nc + c, 0))

    c2 = lambda b, c: (0, 0)
    c3 = lambda b, c: (0, 0, 0)
    og, cs = pl.pallas_call(
        functools.partial(_c_prompt_kernel, layer=layer), grid=(bsz, nc),
        in_specs=[part(0), part(1), part(2), part(3),
                  pl.BlockSpec((H_C, depth, DK_C), c3), pl.BlockSpec((H_C, 1, DV_C), c3),
                  pl.BlockSpec(w["gam"].shape, c2), pl.BlockSpec(w["lvl"].shape, c2)],
        out_specs=[pl.BlockSpec((H_C, tb, LANE), lambda b, c: (0, b * nc + c, 0)),
                   pl.BlockSpec((1, H_C, DK_C, DV_C), lambda b, c: (b, 0, 0, 0))],
        out_shape=[jax.ShapeDtypeStruct((H_C, m, DV_C), BF16),
                   jax.ShapeDtypeStruct((bsz, H_C, DK_C, DV_C), F32)],
        scratch_shapes=[pltpu.VMEM((H_C, DV_C, DK_C), F32)],
        compiler_params=_params("parallel", "arbitrary"),
        name="c_prompt",
    )(u, u, u, u, w["clb"], w["cnw"], w["gam"], w["lvl"])
    return og, cs


def _c_sample_kernel(q_ref, f_ref, v_ref, g_ref, s_ref, clb_ref, cnw_ref, og_ref, so_ref, *, layer):
    bb = SAMPLE_BB
    lane = lax.broadcasted_iota(jnp.int32, (DK_C, LANE), 1)
    first_rows = lax.broadcasted_iota(jnp.int32, (LANE, DV_C), 0) < bb
    for hd in range(H_C):
        lb = _hg_lower_bound(clb_ref[hd], layer)
        f, kk = _hg_gates(f_ref[hd], lb)
        f_t = _pad_rows_t(f)
        k_t = _pad_rows_t(kk)
        qs = q_ref[hd] * (DK_C ** -0.5)
        v = v_ref[hd]
        vpad = jnp.where(first_rows, jnp.tile(v, (LANE // bb, 1)), 0.0).astype(BF16)
        orows = []
        for i in range(bb):
            fcol = jnp.broadcast_to(f_t[:, i:i + 1], (DK_C, DV_C))
            kv = _dot(jnp.where(lane == i, k_t, 0.0).astype(BF16), vpad)
            sn = fcol * s_ref[i, hd] + kv
            so_ref[i, hd] = sn
            qrow = jnp.broadcast_to(qs[i:i + 1, :], (2 * SUBLANE, DK_C)).astype(BF16)
            orows.append(_dot(qrow, sn.astype(BF16))[0:1, :])
        o = jnp.concatenate(orows, axis=0)
        og_ref[hd] = _hg_out(o, g_ref[hd], cnw_ref[hd])


def _c_sample(u, s_c, w, layer):
    bsz = s_c.shape[0]
    bb = SAMPLE_BB
    assert bsz % bb == 0
    depth = w["clb"].shape[1]

    def part(k):
        return pl.BlockSpec((H_C, bb, LANE), lambda i: (k, i, 0))

    c3 = lambda i: (0, 0, 0)
    og, so = pl.pallas_call(
        functools.partial(_c_sample_kernel, layer=layer), grid=(bsz // bb,),
        in_specs=[part(0), part(1), part(2), part(3),
                  pl.BlockSpec((bb, H_C, DK_C, DV_C), lambda i: (i, 0, 0, 0)),
                  pl.BlockSpec((H_C, depth, DK_C), c3), pl.BlockSpec((H_C, 1, DV_C), c3)],
        out_specs=[pl.BlockSpec((H_C, bb, LANE), lambda i: (0, i, 0)),
                   pl.BlockSpec((bb, H_C, DK_C, DV_C), lambda i: (i, 0, 0, 0))],
        out_shape=[jax.ShapeDtypeStruct((H_C, bsz, DV_C), F32), jax.ShapeDtypeStruct(s_c.shape, F32)],
        compiler_params=_params("parallel"),
        name="c_sample",
    )(u, u, u, u, s_c, w["clb"], w["cnw"])
    return og, so


def _row(v, width=None):
    v = v.astype(F32).reshape(1, -1)
    if width is not None and v.shape[1] < width:
        v = jnp.pad(v, ((0, 0), (0, width - v.shape[1])))
    return v


def kernel(x_prompt, x_sample, p_prompt, p_sample, state_a_h, state_a_conv, state_b_ssm, state_b_conv, state_c,
           norm_w, norm_f, ab_w_in, a_conv_w, a_conv_b, a_w_r, a_b_r, a_w_i, a_b_i, a_lam, b_conv_w, b_conv_b,
           b_dt_bias, b_a_log, b_d, b_norm_w, ab_w_out, c_w_in, c_lb, c_norm_w, c_w_out, ple_proj, ple_gate):
    depth = norm_w.shape[0]
    bp, seq, _ = x_prompt.shape
    bs = x_sample.shape[0]
    hp = x_prompt.reshape(bp * seq, D_MODEL)
    hs = x_sample.reshape(bs, D_MODEL)
    pp = p_prompt.reshape(depth, bp * seq, D_PLE)
    ps = p_sample.reshape(depth, bs, D_PLE)
    gam, lvl = _hg_gamma(), _hg_level_table()
    clb = c_lb.astype(F32).reshape(depth, H_C, DK_C).transpose(1, 0, 2)

    ah_p, ac_p, bs_p, bc_p, c_p = [], [], [], [], []
    ah_s, ac_s, bs_s, bc_s, c_s = [], [], [], [], []
    for i in range(depth):
        j = i // 2
        final = i == depth - 1
        wg = (0.5 * ple_gate[i]).astype(BF16)
        wp = (0.5 * ple_proj[i]).astype(BF16)
        if i % 2 == 0:
            col_scale = jnp.concatenate([jnp.ones((D_A,), F32), jnp.full((D_A + D_B,), 0.5, F32),
                                         jnp.ones((CONV_DIM_B + H_B,), F32)])
            w_ab = jnp.pad(ab_w_in[j].T * col_scale[:, None], ((0, LANE - H_B), (0, 0))).astype(BF16)
            wo = ab_w_out[j].astype(BF16)
            w = dict(acw=a_conv_w[j].astype(F32), acb=_row(a_conv_b[j]),
                     wr=a_w_r[j].astype(BF16), br=_row(a_b_r[j]), wi=a_w_i[j].astype(BF16), bi=_row(a_b_i[j]),
                     lam=_row(a_lam[j]), bcw=0.5 * b_conv_w[j].astype(F32), bcb=0.5 * _row(b_conv_b[j]),
                     dtb=_row(b_dt_bias[j], LANE), alog=_row(b_a_log[j], LANE),
                     dexp=_row(jnp.repeat(b_d[j], HD_B)), bnw=_row(b_norm_w[j]))
            u, dtr = _in_proj(hp, norm_w[i], w_ab, extra_cols=True, w_t=True)
            hp, s1, s2, s3, s4 = _ab_prompt(u, dtr, hp, pp, i, wo, wg, wp, norm_f, bp, seq, w, final=final)
            ah_p.append(s1); ac_p.append(s2); bs_p.append(s3); bc_p.append(s4)
            u, dtr = _in_proj(hs, norm_w[i], w_ab, extra_cols=True, w_t=True)
            mix, s1, s2, s3, s4 = _ab_sample(u, dtr, state_a_h[j], state_a_conv[j], state_b_ssm[j],
                                             state_b_conv[j], w)
            ah_s.append(s1); ac_s.append(s2); bs_s.append(s3); bc_s.append(s4)
            hs = _out_proj(mix, hs, ps, i, wo, wg, wp, norm_f, head_major=False, final=final)
        else:
            col_scale = jnp.concatenate([jnp.ones((HK_C,), F32), jnp.full((HK_C,), 0.5, F32),
                                         jnp.ones((D_C,), F32), jnp.full((D_C,), 0.5, F32)])
            w_in = (c_w_in[j] * col_scale).astype(BF16)
            wo = c_w_out[j].astype(BF16)
            w = dict(clb=clb, cnw=c_norm_w[j].astype(F32).reshape(H_C, 1, DV_C), gam=gam, lvl=lvl)
            u = _in_proj(hp, norm_w[i], w_in, head_major=True, head_major_dtype=BF16)
            og, s1 = _c_prompt(u, bp, seq, w, i)
            c_p.append(s1)
            hp = _out_proj(og, hp, pp, i, wo, wg, wp, norm_f, head_major=True, final=final)
            u = _in_proj(hs, norm_w[i], w_in, head_major=True)
            og, s1 = _c_sample(u, state_c[j], w, i)
            c_s.append(s1)
            hs = _out_proj(og, hs, ps, i, wo, wg, wp, norm_f, head_major=True, final=final)
    return (hp.reshape(bp, seq, D_MODEL), hs.reshape(bs, 1, D_MODEL),
            jnp.stack(ah_p), jnp.stack(ac_p), jnp.stack(bs_p), jnp.stack(bc_p), jnp.stack(c_p),
            jnp.stack(ah_s), jnp.stack(ac_s), jnp.stack(bs_s), jnp.stack(bc_s), jnp.stack(c_s))
```

```python
import functools

import jax
import jax.numpy as jnp
from jax import lax
from jax.experimental import pallas as pl
from jax.experimental.pallas import tpu as pltpu

F32 = jnp.float32
BF16 = jnp.bfloat16

D_MODEL = 1024
D_PLE = 256
EPS = 1e-6
CONV_W = 4
D_A = D_MODEL
A_BLOCKS = 8
A_BLK = D_A // A_BLOCKS
LRU_C = 8.0
D_B = D_MODEL
HD_B = 64
H_B = D_B // HD_B
N_B = 128
G_B = 2
CONV_DIM_B = D_B + 2 * G_B * N_B
D_C = 2 * D_MODEL
H_C = 16
DK_C = 128
DV_C = D_C // H_C
HK_C = H_C * DK_C
AB_MAIN = 2 * D_A + D_B + CONV_DIM_B
IN_C = 2 * HK_C + 2 * D_C

LANE = 128
SUBLANE = 8
BF16_ROWS = 16
LOG2_E = 1.4426950408889634
VMEM_LIMIT = 52 * 1024 * 1024

PROJ_TM = 2048
PROJ_MAX_TN = 1024
SSD_CHUNK = 128
AB_CHUNKS = 2
HG_CHUNK = 64
HG_LEVELS = 6
HG_MXU_LEVELS = 3
HG_F_MIN = 1e-30
HG_BLOCK = 512
HG_UNROLL = 16
SAMPLE_BB = 8

_DN_TR = (((1,), (1,)), ((), ()))
_DN_TL = (((0,), (0,)), ((), ()))


def _dot(a, b):
    return jnp.dot(a, b, preferred_element_type=F32)


def _dot_tr(a, b):
    return lax.dot_general(a, b, _DN_TR, preferred_element_type=F32)


def _dot_tl(a, b):
    return lax.dot_general(a, b, _DN_TL, preferred_element_type=F32)


def _silu_half(x_half):
    return x_half + x_half * jnp.tanh(x_half)


def _rmsnorm(x, w):
    return x * lax.rsqrt(jnp.mean(x * x, axis=-1, keepdims=True) + EPS) * w


def _params(*sem):
    return pltpu.CompilerParams(dimension_semantics=sem, vmem_limit_bytes=VMEM_LIMIT)


def _in_proj_kernel(x_ref, nw_ref, w_ref, *rest, has_extra, head_major, w_t):
    if has_extra:
        wx_ref, o_ref, ox_ref, xn_ref = rest
    else:
        o_ref, xn_ref = rest
    dot = _dot_tr if w_t else _dot

    @pl.when(pl.program_id(1) == 0)
    def _():
        xn_ref[...] = _rmsnorm(x_ref[...], nw_ref[...]).astype(BF16)
        if has_extra:
            ox_ref[...] = dot(xn_ref[...], wx_ref[...])

    acc = dot(xn_ref[...], w_ref[...])
    if head_major:
        for k in range(acc.shape[1] // LANE):
            o_ref[k] = acc[:, k * LANE:(k + 1) * LANE].astype(o_ref.dtype)
    else:
        o_ref[...] = acc


def _in_proj(x, nw, w, w_extra=None, *, head_major=False, w_t=False, head_major_dtype=F32):
    m, k = x.shape
    extra_cols = w_extra is not None
    n = w.shape[0 if w_t else 1]
    tm = min(m, PROJ_TM)
    tn = next(c for c in range(PROJ_MAX_TN, 0, -LANE) if n % c == 0)
    assert m % tm == 0 and n % tn == 0 and tn % LANE == 0
    grid = (m // tm, n // tn)
    wspec = (lambda cols, idx: pl.BlockSpec((cols, k), lambda i, j: (idx(j), 0))) if w_t else \
            (lambda cols, idx: pl.BlockSpec((k, cols), lambda i, j: (0, idx(j))))
    in_specs = [pl.BlockSpec((tm, k), lambda i, j: (i, 0)),
                pl.BlockSpec((1, k), lambda i, j: (0, 0)),
                wspec(tn, lambda j: j)]
    args = [x, nw.reshape(1, k), w]
    if head_major:
        out_shape = [jax.ShapeDtypeStruct((n // LANE, m, LANE), head_major_dtype)]
        out_specs = [pl.BlockSpec((tn // LANE, tm, LANE), lambda i, j: (j, i, 0))]
    else:
        out_shape = [jax.ShapeDtypeStruct((m, n), F32)]
        out_specs = [pl.BlockSpec((tm, tn), lambda i, j: (i, j))]
    if extra_cols:
        in_specs.append(wspec(LANE, lambda j: 0))
        args.append(w_extra)
        out_shape.append(jax.ShapeDtypeStruct((m, LANE), F32))
        out_specs.append(pl.BlockSpec((tm, LANE), lambda i, j: (i, 0)))
    outs = pl.pallas_call(
        functools.partial(_in_proj_kernel, has_extra=extra_cols, head_major=head_major, w_t=w_t),
        grid=grid, in_specs=in_specs, out_specs=out_specs, out_shape=out_shape,
        scratch_shapes=[pltpu.VMEM((tm, k), BF16)],
        compiler_params=_params("parallel", "arbitrary"),
        name="in_proj",
    )(*args)
    return outs if extra_cols else outs[0]


def _residual_ple(h, p_ref, wg_ref, wp_ref, nf_ref, final):
    gate_t = jnp.tanh(_dot(h.astype(BF16), wg_ref[...]))
    pe_half = _dot(p_ref[...].astype(BF16), wp_ref[...])
    h = h + pe_half + pe_half * gate_t
    return _rmsnorm(h, nf_ref[...]) if final else h


def _out_proj_kernel(mix_ref, h_ref, p_ref, wo_ref, wg_ref, wp_ref, nf_ref, o_ref, *, head_major, final):
    if head_major:
        mix = jnp.concatenate([mix_ref[k] for k in range(mix_ref.shape[0])], axis=-1)
    else:
        mix = mix_ref[...]
    h = h_ref[...] + _dot(mix.astype(BF16), wo_ref[...])
    o_ref[...] = _residual_ple(h, p_ref, wg_ref, wp_ref, nf_ref, final)


def _out_proj(mix, h, p, layer, wo, wg, wp, nf, *, head_major, final):
    m, d = h.shape
    tm = min(m, 512)
    assert m % tm == 0
    if head_major:
        mix_spec = pl.BlockSpec((mix.shape[0], tm, LANE), lambda i: (0, i, 0))
    else:
        mix_spec = pl.BlockSpec((tm, mix.shape[1]), lambda i: (i, 0))
    const = lambda i: (0, 0)
    return pl.pallas_call(
        functools.partial(_out_proj_kernel, head_major=head_major, final=final),
        grid=(m // tm,),
        in_specs=[mix_spec,
                  pl.BlockSpec((tm, d), lambda i: (i, 0)),
                  pl.BlockSpec((None, tm, p.shape[2]), lambda i: (layer, i, 0)),
                  pl.BlockSpec(wo.shape, const), pl.BlockSpec(wg.shape, const),
                  pl.BlockSpec(wp.shape, const), pl.BlockSpec((1, d), const)],
        out_specs=pl.BlockSpec((tm, d), lambda i: (i, 0)),
        out_shape=jax.ShapeDtypeStruct((m, d), F32),
        compiler_params=_params("parallel"),
        name="out_proj",
    )(mix, h, p, wo, wg, wp, nf.reshape(1, d))


def _lru_gates(xc, wr_ref, br_ref, wi_ref, bi_ref, lam_ref):
    xcb = xc.astype(BF16)
    r_parts, i_parts = [], []
    for k in range(A_BLOCKS):
        xk = xcb[:, k * A_BLK:(k + 1) * A_BLK]
        r_parts.append(_dot(xk, wr_ref[k]))
        i_parts.append(_dot(xk, wi_ref[k]))
    r = jax.nn.sigmoid(jnp.concatenate(r_parts, axis=-1) + br_ref[...])
    gi = jax.nn.sigmoid(jnp.concatenate(i_parts, axis=-1) + bi_ref[...])
    log_a = (-LRU_C) * r * jax.nn.softplus(-lam_ref[...])
    a = jnp.exp(log_a)
    v = jnp.tanh(-log_a) * (a * a + 1.0)
    u = jnp.where(v > 0.0, v * lax.rsqrt(v), 0.0) * (gi * xc)
    return a, u


def _group_norm_gate(y, bx, z_half, dexp_ref, bnw_ref):
    y = (y + dexp_ref[...] * bx) * _silu_half(z_half)
    gw = D_B // G_B
    parts = []
    for g in range(G_B):
        yg = y[:, g * gw:(g + 1) * gw]
        parts.append(yg * lax.rsqrt(jnp.mean(yg * yg, axis=-1, keepdims=True) + EPS))
    return jnp.concatenate(parts, axis=-1) * bnw_ref[...]


def _ab_prompt_kernel(ax_ref, ag_ref, z_ref, xbc_ref, dt_ref,
                      acw_ref, acb_ref, wr_ref, br_ref, wi_ref, bi_ref, lam_ref,
                      bcw_ref, bcb_ref, dtb_ref, alog_ref, dexp_ref, bnw_ref,
                      hin_ref, p_ref, wo_ref, wg_ref, wp_ref, nf_ref,
                      hout_ref, *state_refs, final):
    for k in range(AB_CHUNKS):
        r = pl.ds(k * SSD_CHUNK, SSD_CHUNK)
        _ab_prompt_chunk(ax_ref.at[r], ag_ref.at[r], z_ref.at[r], xbc_ref.at[r], dt_ref.at[r],
                         acw_ref, acb_ref, wr_ref, br_ref, wi_ref, bi_ref, lam_ref,
                         bcw_ref, bcb_ref, dtb_ref, alog_ref, dexp_ref, bnw_ref,
                         hin_ref.at[r], p_ref.at[r], wo_ref, wg_ref, wp_ref, nf_ref,
                         hout_ref.at[r], *state_refs, final=final, first=k == 0, last=k == AB_CHUNKS - 1)


def _ab_prompt_chunk(ax_ref, ag_ref, z_ref, xbc_ref, dt_ref,
                     acw_ref, acb_ref, wr_ref, br_ref, wi_ref, bi_ref, lam_ref,
                     bcw_ref, bcb_ref, dtb_ref, alog_ref, dexp_ref, bnw_ref,
                     hin_ref, p_ref, wo_ref, wg_ref, wp_ref, nf_ref,
                     hout_ref, ah_ref, ac_ref, bs_ref, bc_ref,
                     xpa_ref, xpb_ref, h_ref, s_ref, *, final, first, last):
    c = pl.program_id(1)
    t = SSD_CHUNK
    ntile = t // SUBLANE

    if first:
        @pl.when(c == 0)
        def _():
            xpa_ref[...] = jnp.zeros_like(xpa_ref)
            xpb_ref[...] = jnp.zeros_like(xpb_ref)
            h_ref[...] = jnp.zeros_like(h_ref)
            s_ref[...] = jnp.zeros_like(s_ref)

    def tiles(x):
        return [x[i * SUBLANE:(i + 1) * SUBLANE, :] for i in range(ntile)]

    def conv(x, tail_ref, w_ref, b_ref):
        sub = lax.broadcasted_iota(jnp.int32, (SUBLANE, x.shape[1]), 0)
        xt = [tail_ref[...]] + tiles(x)
        taps = [jnp.broadcast_to(w_ref[k:k + 1, :], (SUBLANE, x.shape[1])) for k in range(CONV_W)]
        bias = jnp.broadcast_to(b_ref[...], (SUBLANE, x.shape[1]))
        acc = [bias + taps[CONV_W - 1] * xt[i + 1] for i in range(ntile)]
        for s in range(1, CONV_W):
            wk = taps[CONV_W - 1 - s]
            for i in range(ntile):
                merged = jnp.where(sub >= SUBLANE - s, xt[i], xt[i + 1])
                acc[i] = acc[i] + wk * pltpu.roll(merged, s, 0)
        tail_ref[...] = xt[ntile]
        return jnp.concatenate(acc, axis=0)

    ax = ax_ref[...]
    xc = conv(ax, xpa_ref, acw_ref, acb_ref)
    a, u = _lru_gates(xc, wr_ref, br_ref, wi_ref, bi_ref, lam_ref)
    sub = lax.broadcasted_iota(jnp.int32, (SUBLANE, D_A), 0)
    at, ut = tiles(a), tiles(u)
    step = 1
    while step < SUBLANE:
        m = sub >= step
        for i in range(ntile):
            ut[i] = jnp.where(m, at[i] * pltpu.roll(ut[i], step, 0) + ut[i], ut[i])
            at[i] = jnp.where(m, at[i] * pltpu.roll(at[i], step, 0), at[i])
        step *= 2
    carry = h_ref[0:1, :]
    hs = []
    for i in range(ntile):
        hs.append(ut[i] + at[i] * carry)
        carry = hs[i][SUBLANE - 1:SUBLANE, :]
    h = jnp.concatenate(hs, axis=0)
    h_ref[0:1, :] = carry
    a_out = (h * _silu_half(ag_ref[...])).astype(BF16)

    xb = xbc_ref[...]
    xbc = _silu_half(conv(xb, xpb_ref, bcw_ref, bcb_ref))
    bx = xbc[:, 0:D_B]
    bxb = bx.astype(BF16)
    dt = jax.nn.softplus(dt_ref[...] + dtb_ref[...])
    adt = dt * (-jnp.exp(alog_ref[...]))
    ti = lax.broadcasted_iota(jnp.int32, (t, t), 0)
    si = lax.broadcasted_iota(jnp.int32, (t, t), 1)
    causal = ti >= si
    acs = jnp.dot(causal.astype(F32), adt, preferred_element_type=F32,
                  precision=lax.Precision.HIGHEST)
    a_last = acs[t - 1:t, :]
    wq = jnp.exp(a_last - acs) * dt
    eacs = jnp.exp(acs)
    ealast = jnp.exp(a_last)
    acs_t = acs.T
    dt_t = dt.T
    lane = lax.broadcasted_iota(jnp.int32, (t, LANE), 1)
    rowi = lax.broadcasted_iota(jnp.int32, (LANE, N_B), 0)
    hpg = H_B // G_B
    ys = []
    cb = None
    for j in range(H_B // 2):
        g = (2 * j) // hpg
        bg = xbc[:, D_B + g * N_B:D_B + (g + 1) * N_B]
        cg = xbc[:, D_B + G_B * N_B + g * N_B:D_B + G_B * N_B + (g + 1) * N_B]
        if (2 * j) % hpg == 0:
            cb = _dot_tr(cg.astype(BF16), bg.astype(BF16))
        xpair = bxb[:, j * LANE:(j + 1) * LANE]
        sp = s_ref[j]
        spb = sp.astype(BF16)
        y_h, up_h = [], []
        for hh in range(2):
            hd = 2 * j + hh
            seg = jnp.broadcast_to(acs[:, hd:hd + 1], (t, t)) - jnp.broadcast_to(acs_t[hd:hd + 1, :], (t, t))
            lmat = jnp.exp(jnp.where(causal, seg, -1e30))
            mmat = (cb * lmat * jnp.broadcast_to(dt_t[hd:hd + 1, :], (t, t))).astype(BF16)
            ec = (jnp.broadcast_to(eacs[:, hd:hd + 1], (t, N_B)) * cg).astype(BF16)
            y_h.append(_dot(mmat, xpair) + _dot_tr(ec, spb))
            bw = (bg * jnp.broadcast_to(wq[:, hd:hd + 1], (t, N_B))).astype(BF16)
            up_h.append(_dot_tl(xpair, bw))
        ys.append(jnp.where(lane < HD_B, y_h[0], y_h[1]))
        dec = jnp.where(rowi < HD_B,
                        jnp.broadcast_to(ealast[:, 2 * j:2 * j + 1], (LANE, N_B)),
                        jnp.broadcast_to(ealast[:, 2 * j + 1:2 * j + 2], (LANE, N_B)))
        s_ref[j] = dec * sp + jnp.where(rowi < HD_B, up_h[0], up_h[1])
    y = jnp.concatenate(ys, axis=-1)
    b_out = _group_norm_gate(y, bx, z_ref[...], dexp_ref, bnw_ref).astype(BF16)

    hres = hin_ref[...] + _dot(a_out, wo_ref[0:D_A, :]) + _dot(b_out, wo_ref[D_A:D_A + D_B, :])
    hout_ref[...] = _residual_ple(hres, p_ref, wg_ref, wp_ref, nf_ref, final)

    if last:
        @pl.when(c == pl.num_programs(1) - 1)
        def _():
            ah_ref[0] = h[t - 1:t, :]
            ac_ref[0] = ax[t - (CONV_W - 1):t, :]
            bc_ref[0] = xb[t - (CONV_W - 1):t, :]
            for j in range(H_B // 2):
                sj = s_ref[j]
                bs_ref[0, 2 * j] = sj[0:HD_B, :]
                bs_ref[0, 2 * j + 1] = sj[HD_B:2 * HD_B, :]


def _ab_prompt(u, dtr, h, p, layer, wo, wg, wp, nf, bsz, seq, w, *, final):
    t = AB_CHUNKS * SSD_CHUNK
    assert seq % t == 0
    nc = seq // t
    m = bsz * seq
    rows = lambda b, c: b * nc + c
    cvec = lambda b, c: (0, 0)
    c3 = lambda b, c: (0, 0, 0)
    in_specs = [
        pl.BlockSpec((t, D_A), lambda b, c: (rows(b, c), 0)),
        pl.BlockSpec((t, D_A), lambda b, c: (rows(b, c), 1)),
        pl.BlockSpec((t, D_B), lambda b, c: (rows(b, c), 2)),
        pl.BlockSpec((t, CONV_DIM_B), lambda b, c: (rows(b, c), 2)),
        pl.BlockSpec((t, LANE), lambda b, c: (rows(b, c), 0)),
        pl.BlockSpec((CONV_W, D_A), cvec), pl.BlockSpec((1, D_A), cvec),
        pl.BlockSpec((A_BLOCKS, A_BLK, A_BLK), c3), pl.BlockSpec((1, D_A), cvec),
        pl.BlockSpec((A_BLOCKS, A_BLK, A_BLK), c3), pl.BlockSpec((1, D_A), cvec),
        pl.BlockSpec((1, D_A), cvec),
        pl.BlockSpec((CONV_W, CONV_DIM_B), cvec), pl.BlockSpec((1, CONV_DIM_B), cvec),
        pl.BlockSpec((1, LANE), cvec), pl.BlockSpec((1, LANE), cvec),
        pl.BlockSpec((1, D_B), cvec), pl.BlockSpec((1, D_B), cvec),
        pl.BlockSpec((t, D_MODEL), lambda b, c: (rows(b, c), 0)),
        pl.BlockSpec((None, t, p.shape[2]), lambda b, c: (layer, rows(b, c), 0)),
        pl.BlockSpec(wo.shape, cvec), pl.BlockSpec(wg.shape, cvec), pl.BlockSpec(wp.shape, cvec),
        pl.BlockSpec((1, D_MODEL), cvec),
    ]
    out_shape = [
        jax.ShapeDtypeStruct((m, D_MODEL), F32),
        jax.ShapeDtypeStruct((bsz, 1, D_A), F32),
        jax.ShapeDtypeStruct((bsz, CONV_W - 1, D_A), F32),
        jax.ShapeDtypeStruct((bsz, H_B, HD_B, N_B), F32),
        jax.ShapeDtypeStruct((bsz, CONV_W - 1, CONV_DIM_B), F32),
    ]
    out_specs = [
        pl.BlockSpec((t, D_MODEL), lambda b, c: (rows(b, c), 0)),
        pl.BlockSpec((1, 1, D_A), lambda b, c: (b, 0, 0)),
        pl.BlockSpec((1, CONV_W - 1, D_A), lambda b, c: (b, 0, 0)),
        pl.BlockSpec((1, H_B, HD_B, N_B), lambda b, c: (b, 0, 0, 0)),
        pl.BlockSpec((1, CONV_W - 1, CONV_DIM_B), lambda b, c: (b, 0, 0)),
    ]
    hout, ah, ac, bs, bc = pl.pallas_call(
        functools.partial(_ab_prompt_kernel, final=final),
        grid=(bsz, nc), in_specs=in_specs, out_specs=out_specs, out_shape=out_shape,
        scratch_shapes=[pltpu.VMEM((SUBLANE, D_A), F32), pltpu.VMEM((SUBLANE, CONV_DIM_B), F32),
                        pltpu.VMEM((SUBLANE, D_A), F32), pltpu.VMEM((H_B // 2, 2 * HD_B, N_B), F32)],
        compiler_params=_params("parallel", "arbitrary"),
        name="ab_prompt",
    )(u, u, u, u, dtr, w["acw"], w["acb"], w["wr"], w["br"], w["wi"], w["bi"], w["lam"],
      w["bcw"], w["bcb"], w["dtb"], w["alog"], w["dexp"], w["bnw"],
      h, p, wo, wg, wp, nf.reshape(1, D_MODEL))
    return hout, ah.reshape(bsz, D_A), ac, bs, bc


def _ab_sample_rows_kernel(ax_ref, ag_ref, xbc_ref, dt_ref, sah_ref, sac_ref, sbc_ref,
                           acw_ref, acb_ref, wr_ref, br_ref, wi_ref, bi_ref, lam_ref,
                           bcw_ref, bcb_ref, dtb_ref,
                           aout_ref, ah_ref, ac_ref, bc_ref, xact_ref, dts_ref):
    def conv1(x, buf_ref, w_ref, b_ref, nbuf_ref, width):
        y = b_ref[...] + w_ref[CONV_W - 1:CONV_W, :] * x
        for k in range(CONV_W - 1):
            y = y + w_ref[k:k + 1, :] * buf_ref[:, k * width:(k + 1) * width]
        for k in range(CONV_W - 2):
            nbuf_ref[:, k * width:(k + 1) * width] = buf_ref[:, (k + 1) * width:(k + 2) * width]
        nbuf_ref[:, (CONV_W - 2) * width:(CONV_W - 1) * width] = x
        return y

    xc = conv1(ax_ref[...], sac_ref, acw_ref, acb_ref, ac_ref, D_A)
    a, u = _lru_gates(xc, wr_ref, br_ref, wi_ref, bi_ref, lam_ref)
    h = a * sah_ref[...] + u
    ah_ref[...] = h
    aout_ref[...] = h * _silu_half(ag_ref[...])
    xact_ref[...] = _silu_half(conv1(xbc_ref[...], sbc_ref, bcw_ref, bcb_ref, bc_ref, CONV_DIM_B))
    dts_ref[...] = jax.nn.softplus(dt_ref[...] + dtb_ref[...])


def _pad_rows_t(x):
    pad = jnp.zeros((LANE - x.shape[0], x.shape[1]), F32)
    return jnp.concatenate([x, pad], axis=0).T


def _ab_sample_state_kernel(s_ref, xact_ref, dts_ref, z_ref, alog_ref, dexp_ref, bnw_ref,
                            so_ref, bout_ref, y_ref):
    bb = SAMPLE_BB
    xact = xact_ref[...]
    bx = xact[:, 0:D_B]
    dts = dts_ref[...]
    dec_t = _pad_rows_t(jnp.exp(dts * (-jnp.exp(alog_ref[...]))))
    dts_t = _pad_rows_t(dts)
    hpg = H_B // G_B
    for j in range(H_B // 2):
        g = (2 * j) // hpg
        xt = _pad_rows_t(bx[:, j * LANE:(j + 1) * LANE])
        dtp = jnp.concatenate([jnp.broadcast_to(dts_t[2 * j:2 * j + 1, :], (HD_B, LANE)),
                               jnp.broadcast_to(dts_t[2 * j + 1:2 * j + 2, :], (HD_B, LANE))], axis=0)
        xdt = xt * dtp
        for i in range(bb):
            brow = jnp.broadcast_to(xact[i:i + 1, D_B + g * N_B:D_B + (g + 1) * N_B], (2 * HD_B, N_B))
            crow = jnp.broadcast_to(
                xact[i:i + 1, D_B + G_B * N_B + g * N_B:D_B + G_B * N_B + (g + 1) * N_B], (2 * SUBLANE, N_B))
            upd = jnp.broadcast_to(xdt[:, i:i + 1], (2 * HD_B, N_B)) * brow
            news = []
            for hh in range(2):
                hd = 2 * j + hh
                dec = jnp.broadcast_to(dec_t[hd:hd + 1, i:i + 1], (HD_B, N_B))
                sn = dec * s_ref[i, hd] + upd[hh * HD_B:(hh + 1) * HD_B, :]
                so_ref[i, hd] = sn
                news.append(sn)
            spair = jnp.concatenate(news, axis=0).astype(BF16)
            yrow = _dot_tr(crow.astype(BF16), spair)
            y_ref[i:i + 1, j * LANE:(j + 1) * LANE] = yrow[0:1, :]
    bout_ref[...] = _group_norm_gate(y_ref[...], bx, z_ref[...], dexp_ref, bnw_ref)


def _ab_sample(u, dtr, s_ah, s_ac, s_bs, s_bc, w):
    bsz = u.shape[0]
    full = lambda shape: pl.BlockSpec(shape, lambda i: tuple(0 for _ in shape))
    cw = CONV_W - 1
    aout, ah, ac, bc, xact, dts = pl.pallas_call(
        _ab_sample_rows_kernel, grid=(1,),
        in_specs=[pl.BlockSpec((bsz, D_A), lambda i: (0, 0)), pl.BlockSpec((bsz, D_A), lambda i: (0, 1)),
                  pl.BlockSpec((bsz, CONV_DIM_B), lambda i: (0, 2)), full((bsz, LANE)),
                  full((bsz, D_A)), full((bsz, cw * D_A)), full((bsz, cw * CONV_DIM_B)),
                  full((CONV_W, D_A)), full((1, D_A)),
                  full((A_BLOCKS, A_BLK, A_BLK)), full((1, D_A)),
                  full((A_BLOCKS, A_BLK, A_BLK)), full((1, D_A)), full((1, D_A)),
                  full((CONV_W, CONV_DIM_B)), full((1, CONV_DIM_B)), full((1, LANE))],
        out_specs=[full((bsz, D_A)), full((bsz, D_A)), full((bsz, cw * D_A)), full((bsz, cw * CONV_DIM_B)),
                   full((bsz, CONV_DIM_B)), full((bsz, LANE))],
        out_shape=[jax.ShapeDtypeStruct((bsz, D_A), F32), jax.ShapeDtypeStruct((bsz, D_A), F32),
                   jax.ShapeDtypeStruct((bsz, cw * D_A), F32), jax.ShapeDtypeStruct((bsz, cw * CONV_DIM_B), F32),
                   jax.ShapeDtypeStruct((bsz, CONV_DIM_B), F32), jax.ShapeDtypeStruct((bsz, LANE), F32)],
        compiler_params=_params("arbitrary"),
        name="ab_sample_rows",
    )(u, u, u, dtr, s_ah, s_ac.reshape(bsz, cw * D_A), s_bc.reshape(bsz, cw * CONV_DIM_B),
      w["acw"], w["acb"], w["wr"], w["br"], w["wi"], w["bi"], w["lam"], w["bcw"], w["bcb"], w["dtb"])

    bb = SAMPLE_BB
    assert bsz % bb == 0
    cvec = lambda i: (0, 0)
    bs, bout = pl.pallas_call(
        _ab_sample_state_kernel, grid=(bsz // bb,),
        in_specs=[pl.BlockSpec((bb, H_B, HD_B, N_B), lambda i: (i, 0, 0, 0)),
                  pl.BlockSpec((bb, CONV_DIM_B), lambda i: (i, 0)),
                  pl.BlockSpec((bb, LANE), lambda i: (i, 0)),
                  pl.BlockSpec((bb, D_B), lambda i: (i, 2)),
                  pl.BlockSpec((1, LANE), cvec), pl.BlockSpec((1, D_B), cvec), pl.BlockSpec((1, D_B), cvec)],
        out_specs=[pl.BlockSpec((bb, H_B, HD_B, N_B), lambda i: (i, 0, 0, 0)),
                   pl.BlockSpec((bb, D_B), lambda i: (i, 0))],
        out_shape=[jax.ShapeDtypeStruct(s_bs.shape, F32), jax.ShapeDtypeStruct((bsz, D_B), F32)],
        scratch_shapes=[pltpu.VMEM((bb, D_B), F32)],
        compiler_params=_params("parallel"),
        name="ab_sample_state",
    )(s_bs, xact, dts, u, w["alog"], w["dexp"], w["bnw"])
    mix = jnp.concatenate([aout, bout], axis=-1)
    return mix, ah, ac.reshape(bsz, cw, D_A), bs, bc.reshape(bsz, cw, CONV_DIM_B)


def _hg_lower_bound(clb, layer):
    mx = jnp.max(clb, axis=0, keepdims=True)
    ex = jnp.exp(clb - mx)
    return jnp.sum(ex[1:layer + 1], axis=0, keepdims=True) / jnp.sum(ex, axis=0, keepdims=True)


def _hg_gates(fx_half, lb):
    f = 0.5 * (1.0 + lb) + (0.5 * (1.0 - lb)) * jnp.tanh(fx_half)
    return f, 1.0 - f


def _hg_out(o, gate_half, cnw):
    return o * lax.rsqrt(jnp.mean(o * o, axis=-1, keepdims=True) + EPS) * cnw * (gate_half + gate_half * jnp.tanh(gate_half))


def _hg_gamma():
    import numpy as np
    q = HG_CHUNK
    t = np.arange(q)[:, None]
    tau = np.arange(q)[None, :]
    mats = [(tau <= t)]
    for l in range(1, HG_MXU_LEVELS):
        w = 1 << l
        ref = (t // (2 * w)) * (2 * w) + w - 1
        upper = (t % (2 * w)) >= w
        mats.append(np.where(upper, (tau > ref) & (tau <= t), (tau > t) & (tau <= ref)))
    gam = np.concatenate(mats, axis=0).astype(np.float32)
    return jnp.asarray(np.concatenate([gam, gam], axis=1), dtype=BF16)


def _hg_level_table():
    import numpy as np
    q = HG_CHUNK
    t = np.arange(q)[:, None]
    s = np.arange(q)[None, :]
    x = t ^ s
    lvl = np.floor(np.log2(np.maximum(x, 1))).astype(np.int32)
    return jnp.asarray(np.where(t > s, lvl, -1).astype(np.int32))


def _c_prompt_kernel(q_ref, f_ref, v_ref, g_ref, clb_ref, cnw_ref, gam_ref, lvl_ref,
                     og_ref, cs_ref, st_ref, *, layer):
    c = pl.program_id(1)
    last = pl.num_programs(1) - 1
    qc = HG_CHUNK

    @pl.when(c == 0)
    def _():
        st_ref[...] = jnp.zeros_like(st_ref)

    gam = gam_ref[...]
    ntile = qc // SUBLANE
    sub = lax.broadcasted_iota(jnp.int32, (SUBLANE, DK_C), 0)
    sub_levels = HG_MXU_LEVELS
    sub_upper = [(sub & (1 << l)) != 0 for l in range(sub_levels)]

    def tiles(x):
        return [x[i * SUBLANE:(i + 1) * SUBLANE, :] for i in range(ntile)]

    def gate_split(hd, rows):
        f, kk = _hg_gates(f_ref[hd, rows, :].astype(F32), _hg_lower_bound(clb_ref[hd], layer))
        g = jnp.log(jnp.maximum(f, HG_F_MIN)) * LOG2_E
        g1 = g.astype(BF16)
        g2 = (g - g1.astype(F32)).astype(BF16)
        return (f, kk), jnp.concatenate([g1, g2], axis=0)

    def scores(hd, rows, fk, sums):
        f, kk = fk
        qh = q_ref[hd, rows, :].astype(F32) * (DK_C ** -0.5)
        bcum = sums[0:qc]
        st = st_ref[hd]
        o = _dot_tr((qh * jnp.exp2(bcum)).astype(BF16), st.astype(BF16))
        qt, kt, ft, bt = tiles(qh), tiles(kk), tiles(f), tiles(bcum)
        prods = []
        for l in range(HG_LEVELS):
            if l == 0:
                xt = [jnp.where(sub_upper[0], qt[i] * ft[i], kt[i]) for i in range(ntile)]
            elif l < HG_MXU_LEVELS:
                dec = tiles(jnp.exp2(sums[l * qc:(l + 1) * qc]))
                xt = [jnp.where(sub_upper[l], qt[i], kt[i]) * dec[i] for i in range(ntile)]
            else:
                wt = 1 << (l - HG_MXU_LEVELS)
                xt = []
                for blk in range(0, ntile, 2 * wt):
                    ref = (blk + wt) * SUBLANE - 1
                    bref = jnp.broadcast_to(bcum[ref:ref + 1, :], (SUBLANE, DK_C))
                    xt += [kt[i] * jnp.exp2(bref - bt[i]) for i in range(blk, blk + wt)]
                    xt += [qt[i] * jnp.exp2(bt[i] - bref) for i in range(blk + wt, blk + 2 * wt)]
            x = jnp.concatenate(xt, axis=0).astype(BF16)
            half = (1 << l) // BF16_ROWS
            if half == 0:
                p = tiles(_dot_tr(x, x))
                prods.append({i: p[i] for i in range(ntile)})
            else:
                ups = [r for r in range(qc // BF16_ROWS) if (r // half) & 1]
                pu = _dot_tr(jnp.concatenate([x[r * BF16_ROWS:(r + 1) * BF16_ROWS, :] for r in ups], axis=0), x)
                tpr = BF16_ROWS // SUBLANE
                prods.append({r * tpr + k: pu[(n * tpr + k) * SUBLANE:(n * tpr + k + 1) * SUBLANE, :]
                              for n, r in enumerate(ups) for k in range(tpr)})
        return qh, st, o, prods

    def level_masks():
        masks = {}
        for i in range(ntile):
            lv = lvl_ref[i * SUBLANE:(i + 1) * SUBLANE, :]
            for l in range(HG_LEVELS):
                if l < sub_levels or (i >> (l - sub_levels)) & 1:
                    masks[i, l] = lv == l
        return masks

    def combine(hd, rows, kk, bcum, qh, st, o, prods, masks):
        arows = []
        for i in range(ntile):
            a = jnp.zeros((SUBLANE, qc), F32)
            for l in range(HG_LEVELS):
                if (i, l) in masks:
                    a = jnp.where(masks[i, l], prods[l][i], a)
            arows.append(a)
        amat = jnp.concatenate(arows, axis=0)
        vb = v_ref[hd, rows, :].astype(BF16)
        o = o + _dot(amat.astype(BF16), vb) + jnp.sum(qh * kk, axis=-1, keepdims=True) * vb.astype(F32)
        blast = bcum[qc - 1:qc, :]
        kdec = (kk * jnp.exp2(blast - bcum)).astype(BF16)
        st_ref[hd] = st * jnp.exp2(blast) + _dot_tl(vb, kdec)
        return o

    nchunk = HG_BLOCK // qc

    def body(idx, carry):
        hg = idx // nchunk
        rows = pl.ds(pl.multiple_of((idx % nchunk) * qc, qc), qc)
        heads = [hg * HG_UNROLL + k for k in range(HG_UNROLL)]
        gs = [gate_split(hd, rows) for hd in heads]
        sums = _dot(gam, jnp.concatenate([s for _, s in gs], axis=1))
        sums = [sums[:, k * DK_C:(k + 1) * DK_C] for k in range(HG_UNROLL)]
        sc = [scores(hd, rows, gs[k][0], sums[k]) for k, hd in enumerate(heads)]
        masks = level_masks()
        outs = [combine(hd, rows, gs[k][0][1], sums[k][0:qc], *sc[k], masks) for k, hd in enumerate(heads)]
        for k, hd in enumerate(heads):
            og_ref[hd, rows, :] = _hg_out(outs[k], g_ref[hd, rows, :].astype(F32), cnw_ref[hd]).astype(BF16)
        return carry

    lax.fori_loop(0, (H_C // HG_UNROLL) * nchunk, body, 0)

    @pl.when(c == last)
    def _():
        for hd in range(H_C):
            cs_ref[0, hd] = st_ref[hd].T


def _c_prompt(u, bsz, seq, w, layer):
    tb = HG_BLOCK
    assert seq % tb == 0 and tb % HG_CHUNK == 0 and (1 << HG_LEVELS) == HG_CHUNK
    nc = seq // tb
    m = bsz * seq
    depth = w["clb"].shape[1]

    def part(k):
        return pl.BlockSpec((H_C, tb, LANE), lambda b, c: (k, b * nc + c, 0))

    c2 = lambda b, c: (0, 0)
    c3 = lambda b, c: (0, 0, 0)
    og, cs = pl.pallas_call(
        functools.partial(_c_prompt_kernel, layer=layer), grid=(bsz, nc),
        in_specs=[part(0), part(1), part(2), part(3),
                  pl.BlockSpec((H_C, depth, DK_C), c3), pl.BlockSpec((H_C, 1, DV_C), c3),
                  pl.BlockSpec(w["gam"].shape, c2), pl.BlockSpec(w["lvl"].shape, c2)],
        out_specs=[pl.BlockSpec((H_C, tb, LANE), lambda b, c: (0, b * nc + c, 0)),
                   pl.BlockSpec((1, H_C, DK_C, DV_C), lambda b, c: (b, 0, 0, 0))],
        out_shape=[jax.ShapeDtypeStruct((H_C, m, DV_C), BF16),
                   jax.ShapeDtypeStruct((bsz, H_C, DK_C, DV_C), F32)],
        scratch_shapes=[pltpu.VMEM((H_C, DV_C, DK_C), F32)],
        compiler_params=_params("parallel", "arbitrary"),
        name="c_prompt",
    )(u, u, u, u, w["clb"], w["cnw"], w["gam"], w["lvl"])
    return og, cs


def _c_sample_kernel(q_ref, f_ref, v_ref, g_ref, s_ref, clb_ref, cnw_ref, og_ref, so_ref, *, layer):
    bb = SAMPLE_BB
    lane = lax.broadcasted_iota(jnp.int32, (DK_C, LANE), 1)
    first_rows = lax.broadcasted_iota(jnp.int32, (LANE, DV_C), 0) < bb
    for hd in range(H_C):
        lb = _hg_lower_bound(clb_ref[hd], layer)
        f, kk = _hg_gates(f_ref[hd], lb)
        f_t = _pad_rows_t(f)
        k_t = _pad_rows_t(kk)
        qs = q_ref[hd] * (DK_C ** -0.5)
        v = v_ref[hd]
        vpad = jnp.where(first_rows, jnp.tile(v, (LANE // bb, 1)), 0.0).astype(BF16)
        orows = []
        for i in range(bb):
            fcol = jnp.broadcast_to(f_t[:, i:i + 1], (DK_C, DV_C))
            kv = _dot(jnp.where(lane == i, k_t, 0.0).astype(BF16), vpad)
            sn = fcol * s_ref[i, hd] + kv
            so_ref[i, hd] = sn
            qrow = jnp.broadcast_to(qs[i:i + 1, :], (2 * SUBLANE, DK_C)).astype(BF16)
            orows.append(_dot(qrow, sn.astype(BF16))[0:1, :])
        o = jnp.concatenate(orows, axis=0)
        og_ref[hd] = _hg_out(o, g_ref[hd], cnw_ref[hd])


def _c_sample(u, s_c, w, layer):
    bsz = s_c.shape[0]
    bb = SAMPLE_BB
    assert bsz % bb == 0
    depth = w["clb"].shape[1]

    def part(k):
        return pl.BlockSpec((H_C, bb, LANE), lambda i: (k, i, 0))

    c3 = lambda i: (0, 0, 0)
    og, so = pl.pallas_call(
        functools.partial(_c_sample_kernel, layer=layer), grid=(bsz // bb,),
        in_specs=[part(0), part(1), part(2), part(3),
                  pl.BlockSpec((bb, H_C, DK_C, DV_C), lambda i: (i, 0, 0, 0)),
                  pl.BlockSpec((H_C, depth, DK_C), c3), pl.BlockSpec((H_C, 1, DV_C), c3)],
        out_specs=[pl.BlockSpec((H_C, bb, LANE), lambda i: (0, i, 0)),
                   pl.BlockSpec((bb, H_C, DK_C, DV_C), lambda i: (i, 0, 0, 0))],
        out_shape=[jax.ShapeDtypeStruct((H_C, bsz, DV_C), F32), jax.ShapeDtypeStruct(s_c.shape, F32)],
        compiler_params=_params("parallel"),
        name="c_sample",
    )(u, u, u, u, s_c, w["clb"], w["cnw"])
    return og, so


def _row(v, width=None):
    v = v.astype(F32).reshape(1, -1)
    if width is not None and v.shape[1] < width:
        v = jnp.pad(v, ((0, 0), (0, width - v.shape[1])))
    return v


def kernel(x_prompt, x_sample, p_prompt, p_sample, state_a_h, state_a_conv, state_b_ssm, state_b_conv, state_c,
           norm_w, norm_f, ab_w_in, a_conv_w, a_conv_b, a_w_r, a_b_r, a_w_i, a_b_i, a_lam, b_conv_w, b_conv_b,
           b_dt_bias, b_a_log, b_d, b_norm_w, ab_w_out, c_w_in, c_lb, c_norm_w, c_w_out, ple_proj, ple_gate):
    depth = norm_w.shape[0]
    bp, seq, _ = x_prompt.shape
    bs = x_sample.shape[0]
    hp = x_prompt.reshape(bp * seq, D_MODEL)
    hs = x_sample.reshape(bs, D_MODEL)
    pp = p_prompt.reshape(depth, bp * seq, D_PLE)
    ps = p_sample.reshape(depth, bs, D_PLE)
    gam, lvl = _hg_gamma(), _hg_level_table()
    clb = c_lb.astype(F32).reshape(depth, H_C, DK_C).transpose(1, 0, 2)

    ah_p, ac_p, bs_p, bc_p, c_p = [], [], [], [], []
    ah_s, ac_s, bs_s, bc_s, c_s = [], [], [], [], []
    for i in range(depth):
        j = i // 2
        final = i == depth - 1
        wg = (0.5 * ple_gate[i]).astype(BF16)
        wp = (0.5 * ple_proj[i]).astype(BF16)
        if i % 2 == 0:
            col_scale = jnp.concatenate([jnp.ones((D_A,), F32), jnp.full((D_A + D_B,), 0.5, F32),
                                         jnp.ones((CONV_DIM_B,), F32)])
            w_ab_t = ab_w_in[j].T
            w_ab = (w_ab_t[:AB_MAIN] * col_scale[:, None]).astype(BF16)
            w_dt = jnp.pad(w_ab_t[AB_MAIN:], ((0, LANE - H_B), (0, 0))).astype(BF16)
            wo = ab_w_out[j].astype(BF16)
            w = dict(acw=a_conv_w[j].astype(F32), acb=_row(a_conv_b[j]),
                     wr=a_w_r[j].astype(BF16), br=_row(a_b_r[j]), wi=a_w_i[j].astype(BF16), bi=_row(a_b_i[j]),
                     lam=_row(a_lam[j]), bcw=0.5 * b_conv_w[j].astype(F32), bcb=0.5 * _row(b_conv_b[j]),
                     dtb=_row(b_dt_bias[j], LANE), alog=_row(b_a_log[j], LANE),
                     dexp=_row(jnp.repeat(b_d[j], HD_B)), bnw=_row(b_norm_w[j]))
            u, dtr = _in_proj(hp, norm_w[i], w_ab, w_dt, w_t=True)
            hp, s1, s2, s3, s4 = _ab_prompt(u, dtr, hp, pp, i, wo, wg, wp, norm_f, bp, seq, w, final=final)
            ah_p.append(s1); ac_p.append(s2); bs_p.append(s3); bc_p.append(s4)
            u, dtr = _in_proj(hs, norm_w[i], w_ab, w_dt, w_t=True)
            mix, s1, s2, s3, s4 = _ab_sample(u, dtr, state_a_h[j], state_a_conv[j], state_b_ssm[j],
                                             state_b_conv[j], w)
            ah_s.append(s1); ac_s.append(s2); bs_s.append(s3); bc_s.append(s4)
            hs = _out_proj(mix, hs, ps, i, wo, wg, wp, norm_f, head_major=False, final=final)
        else:
            col_scale = jnp.concatenate([jnp.ones((HK_C,), F32), jnp.full((HK_C,), 0.5, F32),
                                         jnp.ones((D_C,), F32), jnp.full((D_C,), 0.5, F32)])
            w_in = (c_w_in[j] * col_scale).astype(BF16)
            wo = c_w_out[j].astype(BF16)
            w = dict(clb=clb, cnw=c_norm_w[j].astype(F32).reshape(H_C, 1, DV_C), gam=gam, lvl=lvl)
            u = _in_proj(hp, norm_w[i], w_in, head_major=True, head_major_dtype=BF16)
            og, s1 = _c_prompt(u, bp, seq, w, i)
            c_p.append(s1)
            hp = _out_proj(og, hp, pp, i, wo, wg, wp, norm_f, head_major=True, final=final)
            u = _in_proj(hs, norm_w[i], w_in, head_major=True)
            og, s1 = _c_sample(u, state_c[j], w, i)
            c_s.append(s1)
            hs = _out_proj(og, hs, ps, i, wo, wg, wp, norm_f, head_major=True, final=final)
    return (hp.reshape(bp, seq, D_MODEL), hs.reshape(bs, 1, D_MODEL),
            jnp.stack(ah_p), jnp.stack(ac_p), jnp.stack(bs_p), jnp.stack(bc_p), jnp.stack(c_p),
            jnp.stack(ah_s), jnp.stack(ac_s), jnp.stack(bs_s), jnp.stack(bc_s), jnp.stack(c_s))
```

```python
import functools

import jax
import jax.numpy as jnp
from jax import lax
from jax.experimental import pallas as pl
from jax.experimental.pallas import tpu as pltpu

F32 = jnp.float32
BF16 = jnp.bfloat16

D_MODEL = 1024
D_PLE = 256
EPS = 1e-6
CONV_W = 4
D_A = D_MODEL
A_BLOCKS = 8
A_BLK = D_A // A_BLOCKS
LRU_C = 8.0
D_B = D_MODEL
HD_B = 64
H_B = D_B // HD_B
N_B = 128
G_B = 2
CONV_DIM_B = D_B + 2 * G_B * N_B
D_C = 2 * D_MODEL
H_C = 16
DK_C = 128
DV_C = D_C // H_C
HK_C = H_C * DK_C
AB_MAIN = 2 * D_A + D_B + CONV_DIM_B
IN_C = 2 * HK_C + 2 * D_C

LANE = 128
SUBLANE = 8
BF16_ROWS = 16
LOG2_E = 1.4426950408889634
VMEM_LIMIT = 56 * 1024 * 1024

PROJ_TM = 2048
PROJ_VMEM_BUDGET = 46 * 1024 * 1024
SSD_CHUNK = 128
AB_CHUNKS = 2
HG_CHUNK = 128
HG_LEVELS = 7
HG_MXU_LEVELS = 3
HG_F_MIN = 1e-30
HG_BLOCK = 512
HG_UNROLL = 16
SAMPLE_BB = 8

_DN_TR = (((1,), (1,)), ((), ()))
_DN_TL = (((0,), (0,)), ((), ()))


def _dot(a, b):
    return jnp.dot(a, b, preferred_element_type=F32)


def _dot_tr(a, b):
    return lax.dot_general(a, b, _DN_TR, preferred_element_type=F32)


def _dot_tl(a, b):
    return lax.dot_general(a, b, _DN_TL, preferred_element_type=F32)


def _silu_half(x_half):
    return x_half + x_half * jnp.tanh(x_half)


def _rmsnorm(x, w):
    return x * lax.rsqrt(jnp.mean(x * x, axis=-1, keepdims=True) + EPS) * w


def _params(*sem):
    return pltpu.CompilerParams(dimension_semantics=sem, vmem_limit_bytes=VMEM_LIMIT)


def _in_proj_kernel(x_ref, nw_ref, w_ref, *rest, has_extra, head_major, w_t):
    if has_extra:
        wx_ref, o_ref, ox_ref, xn_ref = rest
    else:
        o_ref, xn_ref = rest
    dot = _dot_tr if w_t else _dot

    @pl.when(pl.program_id(1) == 0)
    def _():
        xn_ref[...] = _rmsnorm(x_ref[...], nw_ref[...]).astype(BF16)
        if has_extra:
            ox_ref[...] = dot(xn_ref[...], wx_ref[...])

    acc = dot(xn_ref[...], w_ref[...])
    if head_major:
        for k in range(acc.shape[1] // LANE):
            o_ref[k] = acc[:, k * LANE:(k + 1) * LANE].astype(o_ref.dtype)
    else:
        o_ref[...] = acc


def _in_proj(x, nw, w, w_extra=None, *, head_major=False, w_t=False, head_major_dtype=F32):
    m, k = x.shape
    extra_cols = w_extra is not None
    n = w.shape[0 if w_t else 1]
    tm = min(m, PROJ_TM)
    out_bytes = jnp.dtype(head_major_dtype if head_major else F32).itemsize

    def vmem(c):
        extra = 2 * k * LANE * 2 + 2 * tm * LANE * 4 if extra_cols else 0
        return 2 * tm * k * 4 + tm * k * 2 + 2 * k * c * 2 + 2 * tm * c * out_bytes + extra

    tn = next(c for c in range(n, 0, -LANE) if n % c == 0 and vmem(c) <= PROJ_VMEM_BUDGET)
    assert m % tm == 0 and n % tn == 0 and tn % LANE == 0
    grid = (m // tm, n // tn)
    wspec = (lambda cols, idx: pl.BlockSpec((cols, k), lambda i, j: (idx(j), 0))) if w_t else \
            (lambda cols, idx: pl.BlockSpec((k, cols), lambda i, j: (0, idx(j))))
    in_specs = [pl.BlockSpec((tm, k), lambda i, j: (i, 0)),
                pl.BlockSpec((1, k), lambda i, j: (0, 0)),
                wspec(tn, lambda j: j)]
    args = [x, nw.reshape(1, k), w]
    if head_major:
        out_shape = [jax.ShapeDtypeStruct((n // LANE, m, LANE), head_major_dtype)]
        out_specs = [pl.BlockSpec((tn // LANE, tm, LANE), lambda i, j: (j, i, 0))]
    else:
        out_shape = [jax.ShapeDtypeStruct((m, n), F32)]
        out_specs = [pl.BlockSpec((tm, tn), lambda i, j: (i, j))]
    if extra_cols:
        in_specs.append(wspec(LANE, lambda j: 0))
        args.append(w_extra)
        out_shape.append(jax.ShapeDtypeStruct((m, LANE), F32))
        out_specs.append(pl.BlockSpec((tm, LANE), lambda i, j: (i, 0)))
    outs = pl.pallas_call(
        functools.partial(_in_proj_kernel, has_extra=extra_cols, head_major=head_major, w_t=w_t),
        grid=grid, in_specs=in_specs, out_specs=out_specs, out_shape=out_shape,
        scratch_shapes=[pltpu.VMEM((tm, k), BF16)],
        compiler_params=_params("parallel", "arbitrary"),
        name="in_proj",
    )(*args)
    return outs if extra_cols else outs[0]


def _residual_ple(h, p_ref, wg_ref, wp_ref, nf_ref, final):
    gate_t = jnp.tanh(_dot(h.astype(BF16), wg_ref[...]))
    pe_half = _dot(p_ref[...].astype(BF16), wp_ref[...])
    h = h + pe_half + pe_half * gate_t
    return _rmsnorm(h, nf_ref[...]) if final else h


def _out_proj_kernel(mix_ref, h_ref, p_ref, wo_ref, wg_ref, wp_ref, nf_ref, o_ref, *, head_major, final):
    if head_major:
        mix = jnp.concatenate([mix_ref[k] for k in range(mix_ref.shape[0])], axis=-1)
    else:
        mix = mix_ref[...]
    h = h_ref[...] + _dot(mix.astype(BF16), wo_ref[...])
    o_ref[...] = _residual_ple(h, p_ref, wg_ref, wp_ref, nf_ref, final)


def _out_proj(mix, h, p, layer, wo, wg, wp, nf, *, head_major, final):
    m, d = h.shape
    tm = min(m, 512)
    assert m % tm == 0
    if head_major:
        mix_spec = pl.BlockSpec((mix.shape[0], tm, LANE), lambda i: (0, i, 0))
    else:
        mix_spec = pl.BlockSpec((tm, mix.shape[1]), lambda i: (i, 0))
    const = lambda i: (0, 0)
    return pl.pallas_call(
        functools.partial(_out_proj_kernel, head_major=head_major, final=final),
        grid=(m // tm,),
        in_specs=[mix_spec,
                  pl.BlockSpec((tm, d), lambda i: (i, 0)),
                  pl.BlockSpec((None, tm, p.shape[2]), lambda i: (layer, i, 0)),
                  pl.BlockSpec(wo.shape, const), pl.BlockSpec(wg.shape, const),
                  pl.BlockSpec(wp.shape, const), pl.BlockSpec((1, d), const)],
        out_specs=pl.BlockSpec((tm, d), lambda i: (i, 0)),
        out_shape=jax.ShapeDtypeStruct((m, d), F32),
        compiler_params=_params("parallel"),
        name="out_proj",
    )(mix, h, p, wo, wg, wp, nf.reshape(1, d))


def _lru_gates(xc, wr_ref, br_ref, wi_ref, bi_ref, lam_ref):
    xcb = xc.astype(BF16)
    r_parts, i_parts = [], []
    for k in range(A_BLOCKS):
        xk = xcb[:, k * A_BLK:(k + 1) * A_BLK]
        r_parts.append(_dot(xk, wr_ref[k]))
        i_parts.append(_dot(xk, wi_ref[k]))
    r = jax.nn.sigmoid(jnp.concatenate(r_parts, axis=-1) + br_ref[...])
    gi = jax.nn.sigmoid(jnp.concatenate(i_parts, axis=-1) + bi_ref[...])
    log_a = (-LRU_C) * r * jax.nn.softplus(-lam_ref[...])
    a = jnp.exp(log_a)
    v = jnp.tanh(-log_a) * (a * a + 1.0)
    u = jnp.where(v > 0.0, v * lax.rsqrt(v), 0.0) * (gi * xc)
    return a, u


def _group_norm_gate(y, bx, z_half, dexp_ref, bnw_ref):
    y = (y + dexp_ref[...] * bx) * _silu_half(z_half)
    gw = D_B // G_B
    parts = []
    for g in range(G_B):
        yg = y[:, g * gw:(g + 1) * gw]
        parts.append(yg * lax.rsqrt(jnp.mean(yg * yg, axis=-1, keepdims=True) + EPS))
    return jnp.concatenate(parts, axis=-1) * bnw_ref[...]


def _ab_prompt_kernel(ax_ref, ag_ref, z_ref, xbc_ref, dt_ref,
                      acw_ref, acb_ref, wr_ref, br_ref, wi_ref, bi_ref, lam_ref,
                      bcw_ref, bcb_ref, dtb_ref, alog_ref, dexp_ref, bnw_ref,
                      hin_ref, p_ref, wo_ref, wg_ref, wp_ref, nf_ref,
                      hout_ref, *state_refs, final):
    for k in range(AB_CHUNKS):
        r = pl.ds(k * SSD_CHUNK, SSD_CHUNK)
        _ab_prompt_chunk(ax_ref.at[r], ag_ref.at[r], z_ref.at[r], xbc_ref.at[r], dt_ref.at[r],
                         acw_ref, acb_ref, wr_ref, br_ref, wi_ref, bi_ref, lam_ref,
                         bcw_ref, bcb_ref, dtb_ref, alog_ref, dexp_ref, bnw_ref,
                         hin_ref.at[r], p_ref.at[r], wo_ref, wg_ref, wp_ref, nf_ref,
                         hout_ref.at[r], *state_refs, final=final, first=k == 0, last=k == AB_CHUNKS - 1)


def _ab_prompt_chunk(ax_ref, ag_ref, z_ref, xbc_ref, dt_ref,
                     acw_ref, acb_ref, wr_ref, br_ref, wi_ref, bi_ref, lam_ref,
                     bcw_ref, bcb_ref, dtb_ref, alog_ref, dexp_ref, bnw_ref,
                     hin_ref, p_ref, wo_ref, wg_ref, wp_ref, nf_ref,
                     hout_ref, ah_ref, ac_ref, bs_ref, bc_ref,
                     xpa_ref, xpb_ref, h_ref, s_ref, *, final, first, last):
    c = pl.program_id(1)
    t = SSD_CHUNK
    ntile = t // SUBLANE

    if first:
        @pl.when(c == 0)
        def _():
            xpa_ref[...] = jnp.zeros_like(xpa_ref)
            xpb_ref[...] = jnp.zeros_like(xpb_ref)
            h_ref[...] = jnp.zeros_like(h_ref)
            s_ref[...] = jnp.zeros_like(s_ref)

    def tiles(x):
        return [x[i * SUBLANE:(i + 1) * SUBLANE, :] for i in range(ntile)]

    def conv(x, tail_ref, w_ref, b_ref):
        sub = lax.broadcasted_iota(jnp.int32, (SUBLANE, x.shape[1]), 0)
        xt = [tail_ref[...]] + tiles(x)
        taps = [jnp.broadcast_to(w_ref[k:k + 1, :], (SUBLANE, x.shape[1])) for k in range(CONV_W)]
        bias = jnp.broadcast_to(b_ref[...], (SUBLANE, x.shape[1]))
        acc = [bias + taps[CONV_W - 1] * xt[i + 1] for i in range(ntile)]
        for s in range(1, CONV_W):
            wk = taps[CONV_W - 1 - s]
            for i in range(ntile):
                merged = jnp.where(sub >= SUBLANE - s, xt[i], xt[i + 1])
                acc[i] = acc[i] + wk * pltpu.roll(merged, s, 0)
        tail_ref[...] = xt[ntile]
        return jnp.concatenate(acc, axis=0)

    ax = ax_ref[...]
    xc = conv(ax, xpa_ref, acw_ref, acb_ref)
    a, u = _lru_gates(xc, wr_ref, br_ref, wi_ref, bi_ref, lam_ref)
    sub = lax.broadcasted_iota(jnp.int32, (SUBLANE, D_A), 0)
    at, ut = tiles(a), tiles(u)
    step = 1
    while step < SUBLANE:
        m = sub >= step
        for i in range(ntile):
            ut[i] = jnp.where(m, at[i] * pltpu.roll(ut[i], step, 0) + ut[i], ut[i])
            at[i] = jnp.where(m, at[i] * pltpu.roll(at[i], step, 0), at[i])
        step *= 2
    carry = h_ref[0:1, :]
    hs = []
    for i in range(ntile):
        hs.append(ut[i] + at[i] * carry)
        carry = hs[i][SUBLANE - 1:SUBLANE, :]
    h = jnp.concatenate(hs, axis=0)
    h_ref[0:1, :] = carry
    a_out = (h * _silu_half(ag_ref[...])).astype(BF16)

    xb = xbc_ref[...]
    xbc = _silu_half(conv(xb, xpb_ref, bcw_ref, bcb_ref))
    bx = xbc[:, 0:D_B]
    bxb = bx.astype(BF16)
    dt = jax.nn.softplus(dt_ref[...] + dtb_ref[...])
    adt = dt * (-jnp.exp(alog_ref[...]))
    ti = lax.broadcasted_iota(jnp.int32, (t, t), 0)
    si = lax.broadcasted_iota(jnp.int32, (t, t), 1)
    causal = ti >= si
    acs = jnp.dot(causal.astype(F32), adt, preferred_element_type=F32,
                  precision=lax.Precision.HIGHEST)
    a_last = acs[t - 1:t, :]
    wq = jnp.exp(a_last - acs) * dt
    eacs = jnp.exp(acs)
    ealast = jnp.exp(a_last)
    acs_t = acs.T
    dt_t = dt.T
    lane = lax.broadcasted_iota(jnp.int32, (t, LANE), 1)
    rowi = lax.broadcasted_iota(jnp.int32, (LANE, N_B), 0)
    hpg = H_B // G_B
    ys = []
    cb = None
    for j in range(H_B // 2):
        g = (2 * j) // hpg
        bg = xbc[:, D_B + g * N_B:D_B + (g + 1) * N_B]
        cg = xbc[:, D_B + G_B * N_B + g * N_B:D_B + G_B * N_B + (g + 1) * N_B]
        if (2 * j) % hpg == 0:
            cb = _dot_tr(cg.astype(BF16), bg.astype(BF16))
        xpair = bxb[:, j * LANE:(j + 1) * LANE]
        sp = s_ref[j]
        spb = sp.astype(BF16)
        y_h, up_h = [], []
        for hh in range(2):
            hd = 2 * j + hh
            seg = jnp.broadcast_to(acs[:, hd:hd + 1], (t, t)) - jnp.broadcast_to(acs_t[hd:hd + 1, :], (t, t))
            lmat = jnp.exp(jnp.where(causal, seg, -1e30))
            mmat = (cb * lmat * jnp.broadcast_to(dt_t[hd:hd + 1, :], (t, t))).astype(BF16)
            ec = (jnp.broadcast_to(eacs[:, hd:hd + 1], (t, N_B)) * cg).astype(BF16)
            y_h.append(_dot(mmat, xpair) + _dot_tr(ec, spb))
            bw = (bg * jnp.broadcast_to(wq[:, hd:hd + 1], (t, N_B))).astype(BF16)
            up_h.append(_dot_tl(xpair, bw))
        ys.append(jnp.where(lane < HD_B, y_h[0], y_h[1]))
        dec = jnp.where(rowi < HD_B,
                        jnp.broadcast_to(ealast[:, 2 * j:2 * j + 1], (LANE, N_B)),
                        jnp.broadcast_to(ealast[:, 2 * j + 1:2 * j + 2], (LANE, N_B)))
        s_ref[j] = dec * sp + jnp.where(rowi < HD_B, up_h[0], up_h[1])
    y = jnp.concatenate(ys, axis=-1)
    b_out = _group_norm_gate(y, bx, z_ref[...], dexp_ref, bnw_ref).astype(BF16)

    hres = hin_ref[...] + _dot(a_out, wo_ref[0:D_A, :]) + _dot(b_out, wo_ref[D_A:D_A + D_B, :])
    hout_ref[...] = _residual_ple(hres, p_ref, wg_ref, wp_ref, nf_ref, final)

    if last:
        @pl.when(c == pl.num_programs(1) - 1)
        def _():
            ah_ref[0] = h[t - 1:t, :]
            ac_ref[0] = ax[t - (CONV_W - 1):t, :]
            bc_ref[0] = xb[t - (CONV_W - 1):t, :]
            for j in range(H_B // 2):
                sj = s_ref[j]
                bs_ref[0, 2 * j] = sj[0:HD_B, :]
                bs_ref[0, 2 * j + 1] = sj[HD_B:2 * HD_B, :]


def _ab_prompt(u, dtr, h, p, layer, wo, wg, wp, nf, bsz, seq, w, *, final):
    t = AB_CHUNKS * SSD_CHUNK
    assert seq % t == 0
    nc = seq // t
    m = bsz * seq
    rows = lambda b, c: b * nc + c
    cvec = lambda b, c: (0, 0)
    c3 = lambda b, c: (0, 0, 0)
    in_specs = [
        pl.BlockSpec((t, D_A), lambda b, c: (rows(b, c), 0)),
        pl.BlockSpec((t, D_A), lambda b, c: (rows(b, c), 1)),
        pl.BlockSpec((t, D_B), lambda b, c: (rows(b, c), 2)),
        pl.BlockSpec((t, CONV_DIM_B), lambda b, c: (rows(b, c), 2)),
        pl.BlockSpec((t, LANE), lambda b, c: (rows(b, c), 0)),
        pl.BlockSpec((CONV_W, D_A), cvec), pl.BlockSpec((1, D_A), cvec),
        pl.BlockSpec((A_BLOCKS, A_BLK, A_BLK), c3), pl.BlockSpec((1, D_A), cvec),
        pl.BlockSpec((A_BLOCKS, A_BLK, A_BLK), c3), pl.BlockSpec((1, D_A), cvec),
        pl.BlockSpec((1, D_A), cvec),
        pl.BlockSpec((CONV_W, CONV_DIM_B), cvec), pl.BlockSpec((1, CONV_DIM_B), cvec),
        pl.BlockSpec((1, LANE), cvec), pl.BlockSpec((1, LANE), cvec),
        pl.BlockSpec((1, D_B), cvec), pl.BlockSpec((1, D_B), cvec),
        pl.BlockSpec((t, D_MODEL), lambda b, c: (rows(b, c), 0)),
        pl.BlockSpec((None, t, p.shape[2]), lambda b, c: (layer, rows(b, c), 0)),
        pl.BlockSpec(wo.shape, cvec), pl.BlockSpec(wg.shape, cvec), pl.BlockSpec(wp.shape, cvec),
        pl.BlockSpec((1, D_MODEL), cvec),
    ]
    out_shape = [
        jax.ShapeDtypeStruct((m, D_MODEL), F32),
        jax.ShapeDtypeStruct((bsz, 1, D_A), F32),
        jax.ShapeDtypeStruct((bsz, CONV_W - 1, D_A), F32),
        jax.ShapeDtypeStruct((bsz, H_B, HD_B, N_B), F32),
        jax.ShapeDtypeStruct((bsz, CONV_W - 1, CONV_DIM_B), F32),
    ]
    out_specs = [
        pl.BlockSpec((t, D_MODEL), lambda b, c: (rows(b, c), 0)),
        pl.BlockSpec((1, 1, D_A), lambda b, c: (b, 0, 0)),
        pl.BlockSpec((1, CONV_W - 1, D_A), lambda b, c: (b, 0, 0)),
        pl.BlockSpec((1, H_B, HD_B, N_B), lambda b, c: (b, 0, 0, 0)),
        pl.BlockSpec((1, CONV_W - 1, CONV_DIM_B), lambda b, c: (b, 0, 0)),
    ]
    hout, ah, ac, bs, bc = pl.pallas_call(
        functools.partial(_ab_prompt_kernel, final=final),
        grid=(bsz, nc), in_specs=in_specs, out_specs=out_specs, out_shape=out_shape,
        scratch_shapes=[pltpu.VMEM((SUBLANE, D_A), F32), pltpu.VMEM((SUBLANE, CONV_DIM_B), F32),
                        pltpu.VMEM((SUBLANE, D_A), F32), pltpu.VMEM((H_B // 2, 2 * HD_B, N_B), F32)],
        compiler_params=_params("parallel", "arbitrary"),
        name="ab_prompt",
    )(u, u, u, u, dtr, w["acw"], w["acb"], w["wr"], w["br"], w["wi"], w["bi"], w["lam"],
      w["bcw"], w["bcb"], w["dtb"], w["alog"], w["dexp"], w["bnw"],
      h, p, wo, wg, wp, nf.reshape(1, D_MODEL))
    return hout, ah.reshape(bsz, D_A), ac, bs, bc


def _ab_sample_rows_kernel(ax_ref, ag_ref, xbc_ref, dt_ref, sah_ref, sac_ref, sbc_ref,
                           acw_ref, acb_ref, wr_ref, br_ref, wi_ref, bi_ref, lam_ref,
                           bcw_ref, bcb_ref, dtb_ref,
                           aout_ref, ah_ref, ac_ref, bc_ref, xact_ref, dts_ref):
    def conv1(x, buf_ref, w_ref, b_ref, nbuf_ref, width):
        y = b_ref[...] + w_ref[CONV_W - 1:CONV_W, :] * x
        for k in range(CONV_W - 1):
            y = y + w_ref[k:k + 1, :] * buf_ref[:, k * width:(k + 1) * width]
        for k in range(CONV_W - 2):
            nbuf_ref[:, k * width:(k + 1) * width] = buf_ref[:, (k + 1) * width:(k + 2) * width]
        nbuf_ref[:, (CONV_W - 2) * width:(CONV_W - 1) * width] = x
        return y

    xc = conv1(ax_ref[...], sac_ref, acw_ref, acb_ref, ac_ref, D_A)
    a, u = _lru_gates(xc, wr_ref, br_ref, wi_ref, bi_ref, lam_ref)
    h = a * sah_ref[...] + u
    ah_ref[...] = h
    aout_ref[...] = h * _silu_half(ag_ref[...])
    xact_ref[...] = _silu_half(conv1(xbc_ref[...], sbc_ref, bcw_ref, bcb_ref, bc_ref, CONV_DIM_B))
    dts_ref[...] = jax.nn.softplus(dt_ref[...] + dtb_ref[...])


def _pad_rows_t(x):
    pad = jnp.zeros((LANE - x.shape[0], x.shape[1]), F32)
    return jnp.concatenate([x, pad], axis=0).T


def _ab_sample_state_kernel(s_ref, xact_ref, dts_ref, z_ref, alog_ref, dexp_ref, bnw_ref,
                            so_ref, bout_ref, y_ref):
    bb = SAMPLE_BB
    xact = xact_ref[...]
    bx = xact[:, 0:D_B]
    dts = dts_ref[...]
    dec_t = _pad_rows_t(jnp.exp(dts * (-jnp.exp(alog_ref[...]))))
    dts_t = _pad_rows_t(dts)
    hpg = H_B // G_B
    for j in range(H_B // 2):
        g = (2 * j) // hpg
        xt = _pad_rows_t(bx[:, j * LANE:(j + 1) * LANE])
        dtp = jnp.concatenate([jnp.broadcast_to(dts_t[2 * j:2 * j + 1, :], (HD_B, LANE)),
                               jnp.broadcast_to(dts_t[2 * j + 1:2 * j + 2, :], (HD_B, LANE))], axis=0)
        xdt = xt * dtp
        for i in range(bb):
            brow = jnp.broadcast_to(xact[i:i + 1, D_B + g * N_B:D_B + (g + 1) * N_B], (2 * HD_B, N_B))
            crow = jnp.broadcast_to(
                xact[i:i + 1, D_B + G_B * N_B + g * N_B:D_B + G_B * N_B + (g + 1) * N_B], (2 * SUBLANE, N_B))
            upd = jnp.broadcast_to(xdt[:, i:i + 1], (2 * HD_B, N_B)) * brow
            news = []
            for hh in range(2):
                hd = 2 * j + hh
                dec = jnp.broadcast_to(dec_t[hd:hd + 1, i:i + 1], (HD_B, N_B))
                sn = dec * s_ref[i, hd] + upd[hh * HD_B:(hh + 1) * HD_B, :]
                so_ref[i, hd] = sn
                news.append(sn)
            spair = jnp.concatenate(news, axis=0).astype(BF16)
            yrow = _dot_tr(crow.astype(BF16), spair)
            y_ref[i:i + 1, j * LANE:(j + 1) * LANE] = yrow[0:1, :]
    bout_ref[...] = _group_norm_gate(y_ref[...], bx, z_ref[...], dexp_ref, bnw_ref)


def _ab_sample(u, dtr, s_ah, s_ac, s_bs, s_bc, w):
    bsz = u.shape[0]
    full = lambda shape: pl.BlockSpec(shape, lambda i: tuple(0 for _ in shape))
    cw = CONV_W - 1
    aout, ah, ac, bc, xact, dts = pl.pallas_call(
        _ab_sample_rows_kernel, grid=(1,),
        in_specs=[pl.BlockSpec((bsz, D_A), lambda i: (0, 0)), pl.BlockSpec((bsz, D_A), lambda i: (0, 1)),
                  pl.BlockSpec((bsz, CONV_DIM_B), lambda i: (0, 2)), full((bsz, LANE)),
                  full((bsz, D_A)), full((bsz, cw * D_A)), full((bsz, cw * CONV_DIM_B)),
                  full((CONV_W, D_A)), full((1, D_A)),
                  full((A_BLOCKS, A_BLK, A_BLK)), full((1, D_A)),
                  full((A_BLOCKS, A_BLK, A_BLK)), full((1, D_A)), full((1, D_A)),
                  full((CONV_W, CONV_DIM_B)), full((1, CONV_DIM_B)), full((1, LANE))],
        out_specs=[full((bsz, D_A)), full((bsz, D_A)), full((bsz, cw * D_A)), full((bsz, cw * CONV_DIM_B)),
                   full((bsz, CONV_DIM_B)), full((bsz, LANE))],
        out_shape=[jax.ShapeDtypeStruct((bsz, D_A), F32), jax.ShapeDtypeStruct((bsz, D_A), F32),
                   jax.ShapeDtypeStruct((bsz, cw * D_A), F32), jax.ShapeDtypeStruct((bsz, cw * CONV_DIM_B), F32),
                   jax.ShapeDtypeStruct((bsz, CONV_DIM_B), F32), jax.ShapeDtypeStruct((bsz, LANE), F32)],
        compiler_params=_params("arbitrary"),
        name="ab_sample_rows",
    )(u, u, u, dtr, s_ah, s_ac.reshape(bsz, cw * D_A), s_bc.reshape(bsz, cw * CONV_DIM_B),
      w["acw"], w["acb"], w["wr"], w["br"], w["wi"], w["bi"], w["lam"], w["bcw"], w["bcb"], w["dtb"])

    bb = SAMPLE_BB
    assert bsz % bb == 0
    cvec = lambda i: (0, 0)
    bs, bout = pl.pallas_call(
        _ab_sample_state_kernel, grid=(bsz // bb,),
        in_specs=[pl.BlockSpec((bb, H_B, HD_B, N_B), lambda i: (i, 0, 0, 0)),
                  pl.BlockSpec((bb, CONV_DIM_B), lambda i: (i, 0)),
                  pl.BlockSpec((bb, LANE), lambda i: (i, 0)),
                  pl.BlockSpec((bb, D_B), lambda i: (i, 2)),
                  pl.BlockSpec((1, LANE), cvec), pl.BlockSpec((1, D_B), cvec), pl.BlockSpec((1, D_B), cvec)],
        out_specs=[pl.BlockSpec((bb, H_B, HD_B, N_B), lambda i: (i, 0, 0, 0)),
                   pl.BlockSpec((bb, D_B), lambda i: (i, 0))],
        out_shape=[jax.ShapeDtypeStruct(s_bs.shape, F32), jax.ShapeDtypeStruct((bsz, D_B), F32)],
        scratch_shapes=[pltpu.VMEM((bb, D_B), F32)],
        compiler_params=_params("parallel"),
        name="ab_sample_state",
    )(s_bs, xact, dts, u, w["alog"], w["dexp"], w["bnw"])
    mix = jnp.concatenate([aout, bout], axis=-1)
    return mix, ah, ac.reshape(bsz, cw, D_A), bs, bc.reshape(bsz, cw, CONV_DIM_B)


def _hg_lower_bound(clb, layer):
    mx = jnp.max(clb, axis=0, keepdims=True)
    ex = jnp.exp(clb - mx)
    return jnp.sum(ex[1:layer + 1], axis=0, keepdims=True) / jnp.sum(ex, axis=0, keepdims=True)


def _hg_gates(fx_half, lb):
    f = 0.5 * (1.0 + lb) + (0.5 * (1.0 - lb)) * jnp.tanh(fx_half)
    return f, 1.0 - f


def _hg_out(o, gate_half, cnw):
    return o * lax.rsqrt(jnp.mean(o * o, axis=-1, keepdims=True) + EPS) * cnw * (gate_half + gate_half * jnp.tanh(gate_half))


def _hg_gamma():
    import numpy as np
    q = HG_CHUNK
    t = np.arange(q)[:, None]
    tau = np.arange(q)[None, :]
    mats = [(tau <= t)]
    for l in range(1, HG_MXU_LEVELS):
        w = 1 << l
        ref = (t // (2 * w)) * (2 * w) + w - 1
        upper = (t % (2 * w)) >= w
        mats.append(np.where(upper, (tau > ref) & (tau <= t), (tau > t) & (tau <= ref)))
    gam = np.concatenate(mats, axis=0).astype(np.float32)
    return jnp.asarray(np.concatenate([gam, gam], axis=1), dtype=BF16)


def _hg_level_table():
    import numpy as np
    q = HG_CHUNK
    t = np.arange(q)[:, None]
    s = np.arange(q)[None, :]
    x = t ^ s
    lvl = np.floor(np.log2(np.maximum(x, 1))).astype(np.int32)
    return jnp.asarray(np.where(t > s, lvl, -1).astype(np.int32))


def _c_prompt_kernel(q_ref, f_ref, v_ref, g_ref, clb_ref, cnw_ref, gam_ref, lvl_ref,
                     og_ref, cs_ref, st_ref, *, layer):
    c = pl.program_id(1)
    last = pl.num_programs(1) - 1
    qc = HG_CHUNK

    @pl.when(c == 0)
    def _():
        st_ref[...] = jnp.zeros_like(st_ref)

    gam = gam_ref[...]
    ntile = qc // SUBLANE
    sub = lax.broadcasted_iota(jnp.int32, (SUBLANE, DK_C), 0)
    sub_levels = HG_MXU_LEVELS
    sub_upper = [(sub & (1 << l)) != 0 for l in range(sub_levels)]

    def tiles(x):
        return [x[i * SUBLANE:(i + 1) * SUBLANE, :] for i in range(ntile)]

    def gate_split(hd, rows):
        f, kk = _hg_gates(f_ref[hd, rows, :].astype(F32), _hg_lower_bound(clb_ref[hd], layer))
        g = jnp.log(jnp.maximum(f, HG_F_MIN)) * LOG2_E
        g1 = g.astype(BF16)
        g2 = (g - g1.astype(F32)).astype(BF16)
        return (f, kk), jnp.concatenate([g1, g2], axis=0)

    def scores(hd, rows, fk, sums):
        f, kk = fk
        qh = q_ref[hd, rows, :].astype(F32) * (DK_C ** -0.5)
        bcum = sums[0:qc]
        st = st_ref[hd]
        o = _dot_tr((qh * jnp.exp2(bcum)).astype(BF16), st.astype(BF16))
        qt, kt, ft, bt = tiles(qh), tiles(kk), tiles(f), tiles(bcum)
        prods = []
        for l in range(HG_LEVELS):
            if l == 0:
                xt = [jnp.where(sub_upper[0], qt[i] * ft[i], kt[i]) for i in range(ntile)]
            elif l < HG_MXU_LEVELS:
                dec = tiles(jnp.exp2(sums[l * qc:(l + 1) * qc]))
                xt = [jnp.where(sub_upper[l], qt[i], kt[i]) * dec[i] for i in range(ntile)]
            else:
                wt = 1 << (l - HG_MXU_LEVELS)
                xt = []
                for blk in range(0, ntile, 2 * wt):
                    ref = (blk + wt) * SUBLANE - 1
                    bref = jnp.broadcast_to(bcum[ref:ref + 1, :], (SUBLANE, DK_C))
                    xt += [kt[i] * jnp.exp2(bref - bt[i]) for i in range(blk, blk + wt)]
                    xt += [qt[i] * jnp.exp2(bt[i] - bref) for i in range(blk + wt, blk + 2 * wt)]
            x = jnp.concatenate(xt, axis=0).astype(BF16)
            half = (1 << l) // BF16_ROWS
            if half == 0:
                p = tiles(_dot_tr(x, x))
                prods.append({i: p[i] for i in range(ntile)})
            else:
                ups = [r for r in range(qc // BF16_ROWS) if (r // half) & 1]
                pu = _dot_tr(jnp.concatenate([x[r * BF16_ROWS:(r + 1) * BF16_ROWS, :] for r in ups], axis=0), x)
                tpr = BF16_ROWS // SUBLANE
                prods.append({r * tpr + k: pu[(n * tpr + k) * SUBLANE:(n * tpr + k + 1) * SUBLANE, :]
                              for n, r in enumerate(ups) for k in range(tpr)})
        return qh, st, o, prods

    def level_masks():
        masks = {}
        for i in range(ntile):
            lv = lvl_ref[i * SUBLANE:(i + 1) * SUBLANE, :]
            for l in range(HG_LEVELS):
                if l < sub_levels or (i >> (l - sub_levels)) & 1:
                    masks[i, l] = lv == l
        return masks

    def combine(hd, rows, kk, bcum, qh, st, o, prods, masks):
        arows = []
        for i in range(ntile):
            a = jnp.zeros((SUBLANE, qc), F32)
            for l in range(HG_LEVELS):
                if (i, l) in masks:
                    a = jnp.where(masks[i, l], prods[l][i], a)
            arows.append(a)
        amat = jnp.concatenate(arows, axis=0)
        vb = v_ref[hd, rows, :].astype(BF16)
        o = o + _dot(amat.astype(BF16), vb) + jnp.sum(qh * kk, axis=-1, keepdims=True) * vb.astype(F32)
        blast = bcum[qc - 1:qc, :]
        kdec = (kk * jnp.exp2(blast - bcum)).astype(BF16)
        st_ref[hd] = st * jnp.exp2(blast) + _dot_tl(vb, kdec)
        return o

    nchunk = HG_BLOCK // qc

    def body(idx, carry):
        hg = idx // nchunk
        rows = pl.ds(pl.multiple_of((idx % nchunk) * qc, qc), qc)
        heads = [hg * HG_UNROLL + k for k in range(HG_UNROLL)]
        gs = [gate_split(hd, rows) for hd in heads]
        sums = _dot(gam, jnp.concatenate([s for _, s in gs], axis=1))
        sums = [sums[:, k * DK_C:(k + 1) * DK_C] for k in range(HG_UNROLL)]
        sc = [scores(hd, rows, gs[k][0], sums[k]) for k, hd in enumerate(heads)]
        masks = level_masks()
        outs = [combine(hd, rows, gs[k][0][1], sums[k][0:qc], *sc[k], masks) for k, hd in enumerate(heads)]
        for k, hd in enumerate(heads):
            og_ref[hd, rows, :] = _hg_out(outs[k], g_ref[hd, rows, :].astype(F32), cnw_ref[hd]).astype(BF16)
        return carry

    lax.fori_loop(0, (H_C // HG_UNROLL) * nchunk, body, 0)

    @pl.when(c == last)
    def _():
        for hd in range(H_C):
            cs_ref[0, hd] = st_ref[hd].T


def _c_prompt(u, bsz, seq, w, layer):
    tb = HG_BLOCK
    assert seq % tb == 0 and tb % HG_CHUNK == 0 and (1 << HG_LEVELS) == HG_CHUNK
    nc = seq // tb
    m = bsz * seq
    depth = w["clb"].shape[1]

    def part(k):
        return pl.BlockSpec((H_C, tb, LANE), lambda b, c: (k, b * nc + c, 0))

    c2 = lambda b, c: (0, 0)
    c3 = lambda b, c: (0, 0, 0)
    og, cs = pl.pallas_call(
        functools.partial(_c_prompt_kernel, layer=layer), grid=(bsz, nc),
        in_specs=[part(0), part(1), part(2), part(3),
                  pl.BlockSpec((H_C, depth, DK_C), c3), pl.BlockSpec((H_C, 1, DV_C), c3),
                  pl.BlockSpec(w["gam"].shape, c2), pl.BlockSpec(w["lvl"].shape, c2)],
        out_specs=[pl.BlockSpec((H_C, tb, LANE), lambda b, c: (0, b * nc + c, 0)),
                   pl.BlockSpec((1, H_C, DK_C, DV_C), lambda b, c: (b, 0, 0, 0))],
        out_shape=[jax.ShapeDtypeStruct((H_C, m, DV_C), BF16),
                   jax.ShapeDtypeStruct((bsz, H_C, DK_C, DV_C), F32)],
        scratch_shapes=[pltpu.VMEM((H_C, DV_C, DK_C), F32)],
        compiler_params=_params("parallel", "arbitrary"),
        name="c_prompt",
    )(u, u, u, u, w["clb"], w["cnw"], w["gam"], w["lvl"])
    return og, cs


def _c_sample_kernel(q_ref, f_ref, v_ref, g_ref, s_ref, clb_ref, cnw_ref, og_ref, so_ref, *, layer):
    bb = SAMPLE_BB
    lane = lax.broadcasted_iota(jnp.int32, (DK_C, LANE), 1)
    first_rows = lax.broadcasted_iota(jnp.int32, (LANE, DV_C), 0) < bb
    for hd in range(H_C):
        lb = _hg_lower_bound(clb_ref[hd], layer)
        f, kk = _hg_gates(f_ref[hd], lb)
        f_t = _pad_rows_t(f)
        k_t = _pad_rows_t(kk)
        qs = q_ref[hd] * (DK_C ** -0.5)
        v = v_ref[hd]
        vpad = jnp.where(first_rows, jnp.tile(v, (LANE // bb, 1)), 0.0).astype(BF16)
        orows = []
        for i in range(bb):
            fcol = jnp.broadcast_to(f_t[:, i:i + 1], (DK_C, DV_C))
            kv = _dot(jnp.where(lane == i, k_t, 0.0).astype(BF16), vpad)
            sn = fcol * s_ref[i, hd] + kv
            so_ref[i, hd] = sn
            qrow = jnp.broadcast_to(qs[i:i + 1, :], (2 * SUBLANE, DK_C)).astype(BF16)
            orows.append(_dot(qrow, sn.astype(BF16))[0:1, :])
        o = jnp.concatenate(orows, axis=0)
        og_ref[hd] = _hg_out(o, g_ref[hd], cnw_ref[hd])


def _c_sample(u, s_c, w, layer):
    bsz = s_c.shape[0]
    bb = SAMPLE_BB
    assert bsz % bb == 0
    depth = w["clb"].shape[1]

    def part(k):
        return pl.BlockSpec((H_C, bb, LANE), lambda i: (k, i, 0))

    c3 = lambda i: (0, 0, 0)
    og, so = pl.pallas_call(
        functools.partial(_c_sample_kernel, layer=layer), grid=(bsz // bb,),
        in_specs=[part(0), part(1), part(2), part(3),
                  pl.BlockSpec((bb, H_C, DK_C, DV_C), lambda i: (i, 0, 0, 0)),
                  pl.BlockSpec((H_C, depth, DK_C), c3), pl.BlockSpec((H_C, 1, DV_C), c3)],
        out_specs=[pl.BlockSpec((H_C, bb, LANE), lambda i: (0, i, 0)),
                   pl.BlockSpec((bb, H_C, DK_C, DV_C), lambda i: (i, 0, 0, 0))],
        out_shape=[jax.ShapeDtypeStruct((H_C, bsz, DV_C), F32), jax.ShapeDtypeStruct(s_c.shape, F32)],
        compiler_params=_params("parallel"),
        name="c_sample",
    )(u, u, u, u, s_c, w["clb"], w["cnw"])
    return og, so


def _row(v, width=None):
    v = v.astype(F32).reshape(1, -1)
    if width is not None and v.shape[1] < width:
        v = jnp.pad(v, ((0, 0), (0, width - v.shape[1])))
    return v


def kernel(x_prompt, x_sample, p_prompt, p_sample, state_a_h, state_a_conv, state_b_ssm, state_b_conv, state_c,
           norm_w, norm_f, ab_w_in, a_conv_w, a_conv_b, a_w_r, a_b_r, a_w_i, a_b_i, a_lam, b_conv_w, b_conv_b,
           b_dt_bias, b_a_log, b_d, b_norm_w, ab_w_out, c_w_in, c_lb, c_norm_w, c_w_out, ple_proj, ple_gate):
    depth = norm_w.shape[0]
    bp, seq, _ = x_prompt.shape
    bs = x_sample.shape[0]
    hp = x_prompt.reshape(bp * seq, D_MODEL)
    hs = x_sample.reshape(bs, D_MODEL)
    pp = p_prompt.reshape(depth, bp * seq, D_PLE)
    ps = p_sample.reshape(depth, bs, D_PLE)
    gam, lvl = _hg_gamma(), _hg_level_table()
    clb = c_lb.astype(F32).reshape(depth, H_C, DK_C).transpose(1, 0, 2)

    ah_p, ac_p, bs_p, bc_p, c_p = [], [], [], [], []
    ah_s, ac_s, bs_s, bc_s, c_s = [], [], [], [], []
    for i in range(depth):
        j = i // 2
        final = i == depth - 1
        wg = (0.5 * ple_gate[i]).astype(BF16)
        wp = (0.5 * ple_proj[i]).astype(BF16)
        if i % 2 == 0:
            col_scale = jnp.concatenate([jnp.ones((D_A,), F32), jnp.full((D_A + D_B,), 0.5, F32),
                                         jnp.ones((CONV_DIM_B,), F32)])
            w_ab_t = ab_w_in[j].T
            w_ab = (w_ab_t[:AB_MAIN] * col_scale[:, None]).astype(BF16)
            w_dt = jnp.pad(w_ab_t[AB_MAIN:], ((0, LANE - H_B), (0, 0))).astype(BF16)
            wo = ab_w_out[j].astype(BF16)
            w = dict(acw=a_conv_w[j].astype(F32), acb=_row(a_conv_b[j]),
                     wr=a_w_r[j].astype(BF16), br=_row(a_b_r[j]), wi=a_w_i[j].astype(BF16), bi=_row(a_b_i[j]),
                     lam=_row(a_lam[j]), bcw=0.5 * b_conv_w[j].astype(F32), bcb=0.5 * _row(b_conv_b[j]),
                     dtb=_row(b_dt_bias[j], LANE), alog=_row(b_a_log[j], LANE),
                     dexp=_row(jnp.repeat(b_d[j], HD_B)), bnw=_row(b_norm_w[j]))
            u, dtr = _in_proj(hp, norm_w[i], w_ab, w_dt, w_t=True)
            hp, s1, s2, s3, s4 = _ab_prompt(u, dtr, hp, pp, i, wo, wg, wp, norm_f, bp, seq, w, final=final)
            ah_p.append(s1); ac_p.append(s2); bs_p.append(s3); bc_p.append(s4)
            u, dtr = _in_proj(hs, norm_w[i], w_ab, w_dt, w_t=True)
            mix, s1, s2, s3, s4 = _ab_sample(u, dtr, state_a_h[j], state_a_conv[j], state_b_ssm[j],
                                             state_b_conv[j], w)
            ah_s.append(s1); ac_s.append(s2); bs_s.append(s3); bc_s.append(s4)
            hs = _out_proj(mix, hs, ps, i, wo, wg, wp, norm_f, head_major=False, final=final)
        else:
            col_scale = jnp.concatenate([jnp.ones((HK_C,), F32), jnp.full((HK_C,), 0.5, F32),
                                         jnp.ones((D_C,), F32), jnp.full((D_C,), 0.5, F32)])
            w_in = (c_w_in[j] * col_scale).astype(BF16)
            wo = c_w_out[j].astype(BF16)
            w = dict(clb=clb, cnw=c_norm_w[j].astype(F32).reshape(H_C, 1, DV_C), gam=gam, lvl=lvl)
            u = _in_proj(hp, norm_w[i], w_in, head_major=True, head_major_dtype=BF16)
            og, s1 = _c_prompt(u, bp, seq, w, i)
            c_p.append(s1)
            hp = _out_proj(og, hp, pp, i, wo, wg, wp, norm_f, head_major=True, final=final)
            u = _in_proj(hs, norm_w[i], w_in, head_major=True)
            og, s1 = _c_sample(u, state_c[j], w, i)
            c_s.append(s1)
            hs = _out_proj(og, hs, ps, i, wo, wg, wp, norm_f, head_major=True, final=final)
    return (hp.reshape(bp, seq, D_MODEL), hs.reshape(bs, 1, D_MODEL),
            jnp.stack(ah_p), jnp.stack(ac_p), jnp.stack(bs_p), jnp.stack(bc_p), jnp.stack(c_p),
            jnp.stack(ah_s), jnp.stack(ac_s), jnp.stack(bs_s), jnp.stack(bc_s), jnp.stack(c_s))
```

```python
import functools

import jax
import jax.numpy as jnp
from jax import lax
from jax.experimental import pallas as pl
from jax.experimental.pallas import tpu as pltpu

F32 = jnp.float32
BF16 = jnp.bfloat16

D_MODEL = 1024
D_PLE = 256
EPS = 1e-6
CONV_W = 4
D_A = D_MODEL
A_BLOCKS = 8
A_BLK = D_A // A_BLOCKS
LRU_C = 8.0
D_B = D_MODEL
HD_B = 64
H_B = D_B // HD_B
N_B = 128
G_B = 2
CONV_DIM_B = D_B + 2 * G_B * N_B
D_C = 2 * D_MODEL
H_C = 16
DK_C = 128
DV_C = D_C // H_C
HK_C = H_C * DK_C
AB_MAIN = 2 * D_A + D_B + CONV_DIM_B
IN_C = 2 * HK_C + 2 * D_C

LANE = 128
SUBLANE = 8
BF16_ROWS = 16
LOG2_E = 1.4426950408889634
VMEM_LIMIT = 56 * 1024 * 1024

PROJ_TM = 2048
PROJ_VMEM_BUDGET = 46 * 1024 * 1024
SSD_CHUNK = 128
AB_CHUNKS = 2
HG_CHUNK = 64
HG_LEVELS = 6
HG_MXU_LEVELS = 3
HG_F_MIN = 1e-30
HG_BLOCK = 512
HG_UNROLL = 16
SAMPLE_BB = 8

_DN_TR = (((1,), (1,)), ((), ()))
_DN_TL = (((0,), (0,)), ((), ()))


def _dot(a, b):
    return jnp.dot(a, b, preferred_element_type=F32)


def _dot_tr(a, b):
    return lax.dot_general(a, b, _DN_TR, preferred_element_type=F32)


def _dot_tl(a, b):
    return lax.dot_general(a, b, _DN_TL, preferred_element_type=F32)


def _silu_half(x_half):
    return x_half + x_half * jnp.tanh(x_half)


def _rmsnorm(x, w):
    return x * lax.rsqrt(jnp.mean(x * x, axis=-1, keepdims=True) + EPS) * w


def _params(*sem):
    return pltpu.CompilerParams(dimension_semantics=sem, vmem_limit_bytes=VMEM_LIMIT)


def _in_proj_kernel(x_ref, nw_ref, w_ref, *rest, has_extra, head_major, w_t):
    if has_extra:
        wx_ref, o_ref, ox_ref, xn_ref = rest
    else:
        o_ref, xn_ref = rest
    dot = _dot_tr if w_t else _dot

    @pl.when(pl.program_id(1) == 0)
    def _():
        xn_ref[...] = _rmsnorm(x_ref[...], nw_ref[...]).astype(BF16)
        if has_extra:
            ox_ref[...] = dot(xn_ref[...], wx_ref[...])

    acc = dot(xn_ref[...], w_ref[...])
    if head_major:
        for k in range(acc.shape[1] // LANE):
            o_ref[k] = acc[:, k * LANE:(k + 1) * LANE].astype(o_ref.dtype)
    else:
        o_ref[...] = acc


def _in_proj(x, nw, w, w_extra=None, *, head_major=False, w_t=False, head_major_dtype=F32):
    m, k = x.shape
    extra_cols = w_extra is not None
    n = w.shape[0 if w_t else 1]
    tm = min(m, PROJ_TM)
    out_bytes = jnp.dtype(head_major_dtype if head_major else F32).itemsize

    def vmem(c):
        extra = 2 * k * LANE * 2 + 2 * tm * LANE * 4 if extra_cols else 0
        return 2 * tm * k * 4 + tm * k * 2 + 2 * k * c * 2 + 2 * tm * c * out_bytes + extra

    tn = next(c for c in range(n, 0, -LANE) if n % c == 0 and vmem(c) <= PROJ_VMEM_BUDGET)
    assert m % tm == 0 and n % tn == 0 and tn % LANE == 0
    grid = (m // tm, n // tn)
    wspec = (lambda cols, idx: pl.BlockSpec((cols, k), lambda i, j: (idx(j), 0))) if w_t else \
            (lambda cols, idx: pl.BlockSpec((k, cols), lambda i, j: (0, idx(j))))
    in_specs = [pl.BlockSpec((tm, k), lambda i, j: (i, 0)),
                pl.BlockSpec((1, k), lambda i, j: (0, 0)),
                wspec(tn, lambda j: j)]
    args = [x, nw.reshape(1, k), w]
    if head_major:
        out_shape = [jax.ShapeDtypeStruct((n // LANE, m, LANE), head_major_dtype)]
        out_specs = [pl.BlockSpec((tn // LANE, tm, LANE), lambda i, j: (j, i, 0))]
    else:
        out_shape = [jax.ShapeDtypeStruct((m, n), F32)]
        out_specs = [pl.BlockSpec((tm, tn), lambda i, j: (i, j))]
    if extra_cols:
        in_specs.append(wspec(LANE, lambda j: 0))
        args.append(w_extra)
        out_shape.append(jax.ShapeDtypeStruct((m, LANE), F32))
        out_specs.append(pl.BlockSpec((tm, LANE), lambda i, j: (i, 0)))
    outs = pl.pallas_call(
        functools.partial(_in_proj_kernel, has_extra=extra_cols, head_major=head_major, w_t=w_t),
        grid=grid, in_specs=in_specs, out_specs=out_specs, out_shape=out_shape,
        scratch_shapes=[pltpu.VMEM((tm, k), BF16)],
        compiler_params=_params("parallel", "arbitrary"),
        name="in_proj",
    )(*args)
    return outs if extra_cols else outs[0]


def _residual_ple(h, p_ref, wg_ref, wp_ref, nf_ref, final):
    gate_t = jnp.tanh(_dot(h.astype(BF16), wg_ref[...]))
    pe_half = _dot(p_ref[...].astype(BF16), wp_ref[...])
    h = h + pe_half + pe_half * gate_t
    return _rmsnorm(h, nf_ref[...]) if final else h


def _out_proj_kernel(mix_ref, h_ref, p_ref, wo_ref, wg_ref, wp_ref, nf_ref, o_ref, *, head_major, final):
    if head_major:
        mix = jnp.concatenate([mix_ref[k] for k in range(mix_ref.shape[0])], axis=-1)
    else:
        mix = mix_ref[...]
    h = h_ref[...] + _dot(mix.astype(BF16), wo_ref[...])
    o_ref[...] = _residual_ple(h, p_ref, wg_ref, wp_ref, nf_ref, final)


def _out_proj(mix, h, p, layer, wo, wg, wp, nf, *, head_major, final):
    m, d = h.shape
    tm = min(m, 512)
    assert m % tm == 0
    if head_major:
        mix_spec = pl.BlockSpec((mix.shape[0], tm, LANE), lambda i: (0, i, 0))
    else:
        mix_spec = pl.BlockSpec((tm, mix.shape[1]), lambda i: (i, 0))
    const = lambda i: (0, 0)
    return pl.pallas_call(
        functools.partial(_out_proj_kernel, head_major=head_major, final=final),
        grid=(m // tm,),
        in_specs=[mix_spec,
                  pl.BlockSpec((tm, d), lambda i: (i, 0)),
                  pl.BlockSpec((None, tm, p.shape[2]), lambda i: (layer, i, 0)),
                  pl.BlockSpec(wo.shape, const), pl.BlockSpec(wg.shape, const),
                  pl.BlockSpec(wp.shape, const), pl.BlockSpec((1, d), const)],
        out_specs=pl.BlockSpec((tm, d), lambda i: (i, 0)),
        out_shape=jax.ShapeDtypeStruct((m, d), F32),
        compiler_params=_params("parallel"),
        name="out_proj",
    )(mix, h, p, wo, wg, wp, nf.reshape(1, d))


def _lru_gates(xc, wr_ref, br_ref, wi_ref, bi_ref, lam_ref):
    xcb = xc.astype(BF16)
    r_parts, i_parts = [], []
    for k in range(A_BLOCKS):
        xk = xcb[:, k * A_BLK:(k + 1) * A_BLK]
        r_parts.append(_dot(xk, wr_ref[k]))
        i_parts.append(_dot(xk, wi_ref[k]))
    r = jax.nn.sigmoid(jnp.concatenate(r_parts, axis=-1) + br_ref[...])
    gi = jax.nn.sigmoid(jnp.concatenate(i_parts, axis=-1) + bi_ref[...])
    log_a = (-LRU_C) * r * jax.nn.softplus(-lam_ref[...])
    a = jnp.exp(log_a)
    v = jnp.tanh(-log_a) * (a * a + 1.0)
    u = jnp.where(v > 0.0, v * lax.rsqrt(v), 0.0) * (gi * xc)
    return a, u


def _group_norm_gate(y, bx, z_half, dexp_ref, bnw_ref):
    y = (y + dexp_ref[...] * bx) * _silu_half(z_half)
    gw = D_B // G_B
    parts = []
    for g in range(G_B):
        yg = y[:, g * gw:(g + 1) * gw]
        parts.append(yg * lax.rsqrt(jnp.mean(yg * yg, axis=-1, keepdims=True) + EPS))
    return jnp.concatenate(parts, axis=-1) * bnw_ref[...]


def _ab_prompt_kernel(ax_ref, ag_ref, z_ref, xbc_ref, dt_ref,
                      acw_ref, acb_ref, wr_ref, br_ref, wi_ref, bi_ref, lam_ref,
                      bcw_ref, bcb_ref, dtb_ref, alog_ref, dexp_ref, bnw_ref,
                      hin_ref, p_ref, wo_ref, wg_ref, wp_ref, nf_ref,
                      hout_ref, *state_refs, final):
    for k in range(AB_CHUNKS):
        r = pl.ds(k * SSD_CHUNK, SSD_CHUNK)
        _ab_prompt_chunk(ax_ref.at[r], ag_ref.at[r], z_ref.at[r], xbc_ref.at[r], dt_ref.at[r],
                         acw_ref, acb_ref, wr_ref, br_ref, wi_ref, bi_ref, lam_ref,
                         bcw_ref, bcb_ref, dtb_ref, alog_ref, dexp_ref, bnw_ref,
                         hin_ref.at[r], p_ref.at[r], wo_ref, wg_ref, wp_ref, nf_ref,
                         hout_ref.at[r], *state_refs, final=final, first=k == 0, last=k == AB_CHUNKS - 1)


def _ab_prompt_chunk(ax_ref, ag_ref, z_ref, xbc_ref, dt_ref,
                     acw_ref, acb_ref, wr_ref, br_ref, wi_ref, bi_ref, lam_ref,
                     bcw_ref, bcb_ref, dtb_ref, alog_ref, dexp_ref, bnw_ref,
                     hin_ref, p_ref, wo_ref, wg_ref, wp_ref, nf_ref,
                     hout_ref, ah_ref, ac_ref, bs_ref, bc_ref,
                     xpa_ref, xpb_ref, h_ref, s_ref, *, final, first, last):
    c = pl.program_id(1)
    t = SSD_CHUNK
    ntile = t // SUBLANE

    if first:
        @pl.when(c == 0)
        def _():
            xpa_ref[...] = jnp.zeros_like(xpa_ref)
            xpb_ref[...] = jnp.zeros_like(xpb_ref)
            h_ref[...] = jnp.zeros_like(h_ref)
            s_ref[...] = jnp.zeros_like(s_ref)

    def tiles(x):
        return [x[i * SUBLANE:(i + 1) * SUBLANE, :] for i in range(ntile)]

    def conv(x, tail_ref, w_ref, b_ref):
        sub = lax.broadcasted_iota(jnp.int32, (SUBLANE, x.shape[1]), 0)
        xt = [tail_ref[...]] + tiles(x)
        taps = [jnp.broadcast_to(w_ref[k:k + 1, :], (SUBLANE, x.shape[1])) for k in range(CONV_W)]
        bias = jnp.broadcast_to(b_ref[...], (SUBLANE, x.shape[1]))
        acc = [bias + taps[CONV_W - 1] * xt[i + 1] for i in range(ntile)]
        for s in range(1, CONV_W):
            wk = taps[CONV_W - 1 - s]
            for i in range(ntile):
                merged = jnp.where(sub >= SUBLANE - s, xt[i], xt[i + 1])
                acc[i] = acc[i] + wk * pltpu.roll(merged, s, 0)
        tail_ref[...] = xt[ntile]
        return jnp.concatenate(acc, axis=0)

    ax = ax_ref[...]
    xc = conv(ax, xpa_ref, acw_ref, acb_ref)
    a, u = _lru_gates(xc, wr_ref, br_ref, wi_ref, bi_ref, lam_ref)
    sub = lax.broadcasted_iota(jnp.int32, (SUBLANE, D_A), 0)
    at, ut = tiles(a), tiles(u)
    step = 1
    while step < SUBLANE:
        m = sub >= step
        for i in range(ntile):
            ut[i] = jnp.where(m, at[i] * pltpu.roll(ut[i], step, 0) + ut[i], ut[i])
            at[i] = jnp.where(m, at[i] * pltpu.roll(at[i], step, 0), at[i])
        step *= 2
    carry = h_ref[0:1, :]
    hs = []
    for i in range(ntile):
        hs.append(ut[i] + at[i] * carry)
        carry = hs[i][SUBLANE - 1:SUBLANE, :]
    h = jnp.concatenate(hs, axis=0)
    h_ref[0:1, :] = carry
    a_out = (h * _silu_half(ag_ref[...])).astype(BF16)

    xb = xbc_ref[...]
    xbc = _silu_half(conv(xb, xpb_ref, bcw_ref, bcb_ref))
    bx = xbc[:, 0:D_B]
    bxb = bx.astype(BF16)
    dt = jax.nn.softplus(dt_ref[...] + dtb_ref[...])
    adt = dt * (-jnp.exp(alog_ref[...]))
    ti = lax.broadcasted_iota(jnp.int32, (t, t), 0)
    si = lax.broadcasted_iota(jnp.int32, (t, t), 1)
    causal = ti >= si
    acs = jnp.dot(causal.astype(F32), adt, preferred_element_type=F32,
                  precision=lax.Precision.HIGHEST)
    a_last = acs[t - 1:t, :]
    wq = jnp.exp(a_last - acs) * dt
    eacs = jnp.exp(acs)
    ealast = jnp.exp(a_last)
    acs_t = acs.T
    dt_t = dt.T
    lane = lax.broadcasted_iota(jnp.int32, (t, LANE), 1)
    rowi = lax.broadcasted_iota(jnp.int32, (LANE, N_B), 0)
    hpg = H_B // G_B
    ys = []
    cb = None
    for j in range(H_B // 2):
        g = (2 * j) // hpg
        bg = xbc[:, D_B + g * N_B:D_B + (g + 1) * N_B]
        cg = xbc[:, D_B + G_B * N_B + g * N_B:D_B + G_B * N_B + (g + 1) * N_B]
        if (2 * j) % hpg == 0:
            cb = _dot_tr(cg.astype(BF16), bg.astype(BF16))
        xpair = bxb[:, j * LANE:(j + 1) * LANE]
        sp = s_ref[j]
        spb = sp.astype(BF16)
        y_h, up_h = [], []
        for hh in range(2):
            hd = 2 * j + hh
            seg = jnp.broadcast_to(acs[:, hd:hd + 1], (t, t)) - jnp.broadcast_to(acs_t[hd:hd + 1, :], (t, t))
            lmat = jnp.exp(jnp.where(causal, seg, -1e30))
            mmat = (cb * lmat * jnp.broadcast_to(dt_t[hd:hd + 1, :], (t, t))).astype(BF16)
            ec = (jnp.broadcast_to(eacs[:, hd:hd + 1], (t, N_B)) * cg).astype(BF16)
            y_h.append(_dot(mmat, xpair) + _dot_tr(ec, spb))
            bw = (bg * jnp.broadcast_to(wq[:, hd:hd + 1], (t, N_B))).astype(BF16)
            up_h.append(_dot_tl(xpair, bw))
        ys.append(jnp.where(lane < HD_B, y_h[0], y_h[1]))
        dec = jnp.where(rowi < HD_B,
                        jnp.broadcast_to(ealast[:, 2 * j:2 * j + 1], (LANE, N_B)),
                        jnp.broadcast_to(ealast[:, 2 * j + 1:2 * j + 2], (LANE, N_B)))
        s_ref[j] = dec * sp + jnp.where(rowi < HD_B, up_h[0], up_h[1])
    y = jnp.concatenate(ys, axis=-1)
    b_out = _group_norm_gate(y, bx, z_ref[...], dexp_ref, bnw_ref).astype(BF16)

    hres = hin_ref[...] + _dot(a_out, wo_ref[0:D_A, :]) + _dot(b_out, wo_ref[D_A:D_A + D_B, :])
    hout_ref[...] = _residual_ple(hres, p_ref, wg_ref, wp_ref, nf_ref, final)

    if last:
        @pl.when(c == pl.num_programs(1) - 1)
        def _():
            ah_ref[0] = h[t - 1:t, :]
            ac_ref[0] = ax[t - (CONV_W - 1):t, :]
            bc_ref[0] = xb[t - (CONV_W - 1):t, :]
            for j in range(H_B // 2):
                sj = s_ref[j]
                bs_ref[0, 2 * j] = sj[0:HD_B, :]
                bs_ref[0, 2 * j + 1] = sj[HD_B:2 * HD_B, :]


def _ab_prompt(u, dtr, h, p, layer, wo, wg, wp, nf, bsz, seq, w, *, final):
    t = AB_CHUNKS * SSD_CHUNK
    assert seq % t == 0
    nc = seq // t
    m = bsz * seq
    rows = lambda b, c: b * nc + c
    cvec = lambda b, c: (0, 0)
    c3 = lambda b, c: (0, 0, 0)
    in_specs = [
        pl.BlockSpec((t, D_A), lambda b, c: (rows(b, c), 0)),
        pl.BlockSpec((t, D_A), lambda b, c: (rows(b, c), 1)),
        pl.BlockSpec((t, D_B), lambda b, c: (rows(b, c), 2)),
        pl.BlockSpec((t, CONV_DIM_B), lambda b, c: (rows(b, c), 2)),
        pl.BlockSpec((t, LANE), lambda b, c: (rows(b, c), 0)),
        pl.BlockSpec((CONV_W, D_A), cvec), pl.BlockSpec((1, D_A), cvec),
        pl.BlockSpec((A_BLOCKS, A_BLK, A_BLK), c3), pl.BlockSpec((1, D_A), cvec),
        pl.BlockSpec((A_BLOCKS, A_BLK, A_BLK), c3), pl.BlockSpec((1, D_A), cvec),
        pl.BlockSpec((1, D_A), cvec),
        pl.BlockSpec((CONV_W, CONV_DIM_B), cvec), pl.BlockSpec((1, CONV_DIM_B), cvec),
        pl.BlockSpec((1, LANE), cvec), pl.BlockSpec((1, LANE), cvec),
        pl.BlockSpec((1, D_B), cvec), pl.BlockSpec((1, D_B), cvec),
        pl.BlockSpec((t, D_MODEL), lambda b, c: (rows(b, c), 0)),
        pl.BlockSpec((None, t, p.shape[2]), lambda b, c: (layer, rows(b, c), 0)),
        pl.BlockSpec(wo.shape, cvec), pl.BlockSpec(wg.shape, cvec), pl.BlockSpec(wp.shape, cvec),
        pl.BlockSpec((1, D_MODEL), cvec),
    ]
    out_shape = [
        jax.ShapeDtypeStruct((m, D_MODEL), F32),
        jax.ShapeDtypeStruct((bsz, 1, D_A), F32),
        jax.ShapeDtypeStruct((bsz, CONV_W - 1, D_A), F32),
        jax.ShapeDtypeStruct((bsz, H_B, HD_B, N_B), F32),
        jax.ShapeDtypeStruct((bsz, CONV_W - 1, CONV_DIM_B), F32),
    ]
    out_specs = [
        pl.BlockSpec((t, D_MODEL), lambda b, c: (rows(b, c), 0)),
        pl.BlockSpec((1, 1, D_A), lambda b, c: (b, 0, 0)),
        pl.BlockSpec((1, CONV_W - 1, D_A), lambda b, c: (b, 0, 0)),
        pl.BlockSpec((1, H_B, HD_B, N_B), lambda b, c: (b, 0, 0, 0)),
        pl.BlockSpec((1, CONV_W - 1, CONV_DIM_B), lambda b, c: (b, 0, 0)),
    ]
    hout, ah, ac, bs, bc = pl.pallas_call(
        functools.partial(_ab_prompt_kernel, final=final),
        grid=(bsz, nc), in_specs=in_specs, out_specs=out_specs, out_shape=out_shape,
        scratch_shapes=[pltpu.VMEM((SUBLANE, D_A), F32), pltpu.VMEM((SUBLANE, CONV_DIM_B), F32),
                        pltpu.VMEM((SUBLANE, D_A), F32), pltpu.VMEM((H_B // 2, 2 * HD_B, N_B), F32)],
        compiler_params=_params("parallel", "arbitrary"),
        name="ab_prompt",
    )(u, u, u, u, dtr, w["acw"], w["acb"], w["wr"], w["br"], w["wi"], w["bi"], w["lam"],
      w["bcw"], w["bcb"], w["dtb"], w["alog"], w["dexp"], w["bnw"],
      h, p, wo, wg, wp, nf.reshape(1, D_MODEL))
    return hout, ah.reshape(bsz, D_A), ac, bs, bc


def _ab_sample_rows_kernel(ax_ref, ag_ref, xbc_ref, dt_ref, sah_ref, sac_ref, sbc_ref,
                           acw_ref, acb_ref, wr_ref, br_ref, wi_ref, bi_ref, lam_ref,
                           bcw_ref, bcb_ref, dtb_ref,
                           aout_ref, ah_ref, ac_ref, bc_ref, xact_ref, dts_ref):
    def conv1(x, buf_ref, w_ref, b_ref, nbuf_ref, width):
        y = b_ref[...] + w_ref[CONV_W - 1:CONV_W, :] * x
        for k in range(CONV_W - 1):
            y = y + w_ref[k:k + 1, :] * buf_ref[:, k * width:(k + 1) * width]
        for k in range(CONV_W - 2):
            nbuf_ref[:, k * width:(k + 1) * width] = buf_ref[:, (k + 1) * width:(k + 2) * width]
        nbuf_ref[:, (CONV_W - 2) * width:(CONV_W - 1) * width] = x
        return y

    xc = conv1(ax_ref[...], sac_ref, acw_ref, acb_ref, ac_ref, D_A)
    a, u = _lru_gates(xc, wr_ref, br_ref, wi_ref, bi_ref, lam_ref)
    h = a * sah_ref[...] + u
    ah_ref[...] = h
    aout_ref[...] = h * _silu_half(ag_ref[...])
    xact_ref[...] = _silu_half(conv1(xbc_ref[...], sbc_ref, bcw_ref, bcb_ref, bc_ref, CONV_DIM_B))
    dts_ref[...] = jax.nn.softplus(dt_ref[...] + dtb_ref[...])


def _pad_rows_t(x):
    pad = jnp.zeros((LANE - x.shape[0], x.shape[1]), F32)
    return jnp.concatenate([x, pad], axis=0).T


def _ab_sample_state_kernel(s_ref, xact_ref, dts_ref, z_ref, alog_ref, dexp_ref, bnw_ref,
                            so_ref, bout_ref, y_ref):
    bb = SAMPLE_BB
    xact = xact_ref[...]
    bx = xact[:, 0:D_B]
    dts = dts_ref[...]
    dec_t = _pad_rows_t(jnp.exp(dts * (-jnp.exp(alog_ref[...]))))
    dts_t = _pad_rows_t(dts)
    hpg = H_B // G_B
    for j in range(H_B // 2):
        g = (2 * j) // hpg
        xt = _pad_rows_t(bx[:, j * LANE:(j + 1) * LANE])
        dtp = jnp.concatenate([jnp.broadcast_to(dts_t[2 * j:2 * j + 1, :], (HD_B, LANE)),
                               jnp.broadcast_to(dts_t[2 * j + 1:2 * j + 2, :], (HD_B, LANE))], axis=0)
        xdt = xt * dtp
        for i in range(bb):
            brow = jnp.broadcast_to(xact[i:i + 1, D_B + g * N_B:D_B + (g + 1) * N_B], (2 * HD_B, N_B))
            crow = jnp.broadcast_to(
                xact[i:i + 1, D_B + G_B * N_B + g * N_B:D_B + G_B * N_B + (g + 1) * N_B], (2 * SUBLANE, N_B))
            upd = jnp.broadcast_to(xdt[:, i:i + 1], (2 * HD_B, N_B)) * brow
            news = []
            for hh in range(2):
                hd = 2 * j + hh
                dec = jnp.broadcast_to(dec_t[hd:hd + 1, i:i + 1], (HD_B, N_B))
                sn = dec * s_ref[i, hd] + upd[hh * HD_B:(hh + 1) * HD_B, :]
                so_ref[i, hd] = sn
                news.append(sn)
            spair = jnp.concatenate(news, axis=0).astype(BF16)
            yrow = _dot_tr(crow.astype(BF16), spair)
            y_ref[i:i + 1, j * LANE:(j + 1) * LANE] = yrow[0:1, :]
    bout_ref[...] = _group_norm_gate(y_ref[...], bx, z_ref[...], dexp_ref, bnw_ref)


def _ab_sample(u, dtr, s_ah, s_ac, s_bs, s_bc, w):
    bsz = u.shape[0]
    full = lambda shape: pl.BlockSpec(shape, lambda i: tuple(0 for _ in shape))
    cw = CONV_W - 1
    aout, ah, ac, bc, xact, dts = pl.pallas_call(
        _ab_sample_rows_kernel, grid=(1,),
        in_specs=[pl.BlockSpec((bsz, D_A), lambda i: (0, 0)), pl.BlockSpec((bsz, D_A), lambda i: (0, 1)),
                  pl.BlockSpec((bsz, CONV_DIM_B), lambda i: (0, 2)), full((bsz, LANE)),
                  full((bsz, D_A)), full((bsz, cw * D_A)), full((bsz, cw * CONV_DIM_B)),
                  full((CONV_W, D_A)), full((1, D_A)),
                  full((A_BLOCKS, A_BLK, A_BLK)), full((1, D_A)),
                  full((A_BLOCKS, A_BLK, A_BLK)), full((1, D_A)), full((1, D_A)),
                  full((CONV_W, CONV_DIM_B)), full((1, CONV_DIM_B)), full((1, LANE))],
        out_specs=[full((bsz, D_A)), full((bsz, D_A)), full((bsz, cw * D_A)), full((bsz, cw * CONV_DIM_B)),
                   full((bsz, CONV_DIM_B)), full((bsz, LANE))],
        out_shape=[jax.ShapeDtypeStruct((bsz, D_A), F32), jax.ShapeDtypeStruct((bsz, D_A), F32),
                   jax.ShapeDtypeStruct((bsz, cw * D_A), F32), jax.ShapeDtypeStruct((bsz, cw * CONV_DIM_B), F32),
                   jax.ShapeDtypeStruct((bsz, CONV_DIM_B), F32), jax.ShapeDtypeStruct((bsz, LANE), F32)],
        compiler_params=_params("arbitrary"),
        name="ab_sample_rows",
    )(u, u, u, dtr, s_ah, s_ac.reshape(bsz, cw * D_A), s_bc.reshape(bsz, cw * CONV_DIM_B),
      w["acw"], w["acb"], w["wr"], w["br"], w["wi"], w["bi"], w["lam"], w["bcw"], w["bcb"], w["dtb"])

    bb = SAMPLE_BB
    assert bsz % bb == 0
    cvec = lambda i: (0, 0)
    bs, bout = pl.pallas_call(
        _ab_sample_state_kernel, grid=(bsz // bb,),
        in_specs=[pl.BlockSpec((bb, H_B, HD_B, N_B), lambda i: (i, 0, 0, 0)),
                  pl.BlockSpec((bb, CONV_DIM_B), lambda i: (i, 0)),
                  pl.BlockSpec((bb, LANE), lambda i: (i, 0)),
                  pl.BlockSpec((bb, D_B), lambda i: (i, 2)),
                  pl.BlockSpec((1, LANE), cvec), pl.BlockSpec((1, D_B), cvec), pl.BlockSpec((1, D_B), cvec)],
        out_specs=[pl.BlockSpec((bb, H_B, HD_B, N_B), lambda i: (i, 0, 0, 0)),
                   pl.BlockSpec((bb, D_B), lambda i: (i, 0))],
        out_shape=[jax.ShapeDtypeStruct(s_bs.shape, F32), jax.ShapeDtypeStruct((bsz, D_B), F32)],
        scratch_shapes=[pltpu.VMEM((bb, D_B), F32)],
        compiler_params=_params("parallel"),
        name="ab_sample_state",
    )(s_bs, xact, dts, u, w["alog"], w["dexp"], w["bnw"])
    mix = jnp.concatenate([aout, bout], axis=-1)
    return mix, ah, ac.reshape(bsz, cw, D_A), bs, bc.reshape(bsz, cw, CONV_DIM_B)


def _hg_lower_bound(clb, layer):
    mx = jnp.max(clb, axis=0, keepdims=True)
    ex = jnp.exp(clb - mx)
    return jnp.sum(ex[1:layer + 1], axis=0, keepdims=True) / jnp.sum(ex, axis=0, keepdims=True)


def _hg_gates(fx_half, lb):
    f = 0.5 * (1.0 + lb) + (0.5 * (1.0 - lb)) * jnp.tanh(fx_half)
    return f, 1.0 - f


def _hg_out(o, gate_half, cnw):
    return o * lax.rsqrt(jnp.mean(o * o, axis=-1, keepdims=True) + EPS) * cnw * (gate_half + gate_half * jnp.tanh(gate_half))


def _hg_gamma():
    import numpy as np
    q = HG_CHUNK
    t = np.arange(q)[:, None]
    tau = np.arange(q)[None, :]
    mats = [(tau <= t)]
    for l in range(1, HG_MXU_LEVELS):
        w = 1 << l
        ref = (t // (2 * w)) * (2 * w) + w - 1
        upper = (t % (2 * w)) >= w
        mats.append(np.where(upper, (tau > ref) & (tau <= t), (tau > t) & (tau <= ref)))
    gam = np.concatenate(mats, axis=0).astype(np.float32)
    return jnp.asarray(np.concatenate([gam, gam], axis=1), dtype=BF16)


def _hg_level_table():
    import numpy as np
    q = HG_CHUNK
    t = np.arange(q)[:, None]
    s = np.arange(q)[None, :]
    x = t ^ s
    lvl = np.floor(np.log2(np.maximum(x, 1))).astype(np.int32)
    return jnp.asarray(np.where(t > s, lvl, -1).astype(np.int32))


def _c_prompt_kernel(q_ref, f_ref, v_ref, g_ref, clb_ref, cnw_ref, gam_ref, lvl_ref,
                     og_ref, cs_ref, st_ref, *, layer):
    c = pl.program_id(1)
    last = pl.num_programs(1) - 1
    qc = HG_CHUNK

    @pl.when(c == 0)
    def _():
        st_ref[...] = jnp.zeros_like(st_ref)

    gam = gam_ref[...]
    ntile = qc // SUBLANE
    sub = lax.broadcasted_iota(jnp.int32, (SUBLANE, DK_C), 0)
    sub_levels = HG_MXU_LEVELS
    sub_upper = [(sub & (1 << l)) != 0 for l in range(sub_levels)]

    def tiles(x):
        return [x[i * SUBLANE:(i + 1) * SUBLANE, :] for i in range(ntile)]

    def gate_split(hd, rows):
        f, kk = _hg_gates(f_ref[hd, rows, :].astype(F32), _hg_lower_bound(clb_ref[hd], layer))
        g = jnp.log(jnp.maximum(f, HG_F_MIN)) * LOG2_E
        g1 = g.astype(BF16)
        g2 = (g - g1.astype(F32)).astype(BF16)
        return (f, kk), jnp.concatenate([g1, g2], axis=0)

    def scores(hd, rows, fk, sums):
        f, kk = fk
        qh = q_ref[hd, rows, :].astype(F32) * (DK_C ** -0.5)
        bcum = sums[0:qc]
        st = st_ref[hd]
        o = _dot_tr((qh * jnp.exp2(bcum)).astype(BF16), st.astype(BF16))
        qt, kt, ft, bt = tiles(qh), tiles(kk), tiles(f), tiles(bcum)
        prods = []
        for l in range(HG_LEVELS):
            if l == 0:
                xt = [jnp.where(sub_upper[0], qt[i] * ft[i], kt[i]) for i in range(ntile)]
            elif l < HG_MXU_LEVELS:
                dec = tiles(jnp.exp2(sums[l * qc:(l + 1) * qc]))
                xt = [jnp.where(sub_upper[l], qt[i], kt[i]) * dec[i] for i in range(ntile)]
            else:
                wt = 1 << (l - HG_MXU_LEVELS)
                xt = []
                for blk in range(0, ntile, 2 * wt):
                    ref = (blk + wt) * SUBLANE - 1
                    bref = jnp.broadcast_to(bcum[ref:ref + 1, :], (SUBLANE, DK_C))
                    xt += [kt[i] * jnp.exp2(bref - bt[i]) for i in range(blk, blk + wt)]
                    xt += [qt[i] * jnp.exp2(bt[i] - bref) for i in range(blk + wt, blk + 2 * wt)]
            x = jnp.concatenate(xt, axis=0).astype(BF16)
            half = (1 << l) // BF16_ROWS
            if half == 0:
                p = tiles(_dot_tr(x, x))
                prods.append({i: p[i] for i in range(ntile)})
            else:
                ups = [r for r in range(qc // BF16_ROWS) if (r // half) & 1]
                pu = _dot_tr(jnp.concatenate([x[r * BF16_ROWS:(r + 1) * BF16_ROWS, :] for r in ups], axis=0), x)
                tpr = BF16_ROWS // SUBLANE
                prods.append({r * tpr + k: pu[(n * tpr + k) * SUBLANE:(n * tpr + k + 1) * SUBLANE, :]
                              for n, r in enumerate(ups) for k in range(tpr)})
        return qh, st, o, prods

    def level_masks():
        masks = {}
        for i in range(ntile):
            lv = lvl_ref[i * SUBLANE:(i + 1) * SUBLANE, :]
            for l in range(HG_LEVELS):
                if l < sub_levels or (i >> (l - sub_levels)) & 1:
                    masks[i, l] = lv == l
        return masks

    def combine(hd, rows, kk, bcum, qh, st, o, prods, masks):
        arows = []
        for i in range(ntile):
            a = jnp.zeros((SUBLANE, qc), F32)
            for l in range(HG_LEVELS):
                if (i, l) in masks:
                    a = jnp.where(masks[i, l], prods[l][i], a)
            arows.append(a)
        amat = jnp.concatenate(arows, axis=0)
        vb = v_ref[hd, rows, :].astype(BF16)
        o = o + _dot(amat.astype(BF16), vb) + jnp.sum(qh * kk, axis=-1, keepdims=True) * vb.astype(F32)
        blast = bcum[qc - 1:qc, :]
        kdec = (kk * jnp.exp2(blast - bcum)).astype(BF16)
        st_ref[hd] = st * jnp.exp2(blast) + _dot_tl(vb, kdec)
        return o

    nchunk = HG_BLOCK // qc

    def body(idx, carry):
        hg = idx // nchunk
        rows = pl.ds(pl.multiple_of((idx % nchunk) * qc, qc), qc)
        heads = [hg * HG_UNROLL + k for k in range(HG_UNROLL)]
        gs = [gate_split(hd, rows) for hd in heads]
        sums = _dot(gam, jnp.concatenate([s for _, s in gs], axis=1))
        sums = [sums[:, k * DK_C:(k + 1) * DK_C] for k in range(HG_UNROLL)]
        sc = [scores(hd, rows, gs[k][0], sums[k]) for k, hd in enumerate(heads)]
        masks = level_masks()
        outs = [combine(hd, rows, gs[k][0][1], sums[k][0:qc], *sc[k], masks) for k, hd in enumerate(heads)]
        for k, hd in enumerate(heads):
            og_ref[hd, rows, :] = _hg_out(outs[k], g_ref[hd, rows, :].astype(F32), cnw_ref[hd]).astype(BF16)
        return carry

    lax.fori_loop(0, (H_C // HG_UNROLL) * nchunk, body, 0)

    @pl.when(c == last)
    def _():
        for hd in range(H_C):
            cs_ref[0, hd] = st_ref[hd].T


def _c_prompt(u, bsz, seq, w, layer):
    tb = HG_BLOCK
    assert seq % tb == 0 and tb % HG_CHUNK == 0 and (1 << HG_LEVELS) == HG_CHUNK
    nc = seq // tb
    m = bsz * seq
    depth = w["clb"].shape[1]

    def part(k):
        return pl.BlockSpec((H_C, tb, LANE), lambda b, c: (k, b * nc + c, 0))

    c2 = lambda b, c: (0, 0)
    c3 = lambda b, c: (0, 0, 0)
    og, cs = pl.pallas_call(
        functools.partial(_c_prompt_kernel, layer=layer), grid=(bsz, nc),
        in_specs=[part(0), part(1), part(2), part(3),
                  pl.BlockSpec((H_C, depth, DK_C), c3), pl.BlockSpec((H_C, 1, DV_C), c3),
                  pl.BlockSpec(w["gam"].shape, c2), pl.BlockSpec(w["lvl"].shape, c2)],
        out_specs=[pl.BlockSpec((H_C, tb, LANE), lambda b, c: (0, b * nc + c, 0)),
                   pl.BlockSpec((1, H_C, DK_C, DV_C), lambda b, c: (b, 0, 0, 0))],
        out_shape=[jax.ShapeDtypeStruct((H_C, m, DV_C), BF16),
                   jax.ShapeDtypeStruct((bsz, H_C, DK_C, DV_C), F32)],
        scratch_shapes=[pltpu.VMEM((H_C, DV_C, DK_C), F32)],
        compiler_params=_params("parallel", "arbitrary"),
        name="c_prompt",
    )(u, u, u, u, w["clb"], w["cnw"], w["gam"], w["lvl"])
    return og, cs


def _c_sample_kernel(q_ref, f_ref, v_ref, g_ref, s_ref, clb_ref, cnw_ref, og_ref, so_ref, *, layer):
    bb = SAMPLE_BB
    lane = lax.broadcasted_iota(jnp.int32, (DK_C, LANE), 1)
    first_rows = lax.broadcasted_iota(jnp.int32, (LANE, DV_C), 0) < bb
    for hd in range(H_C):
        lb = _hg_lower_bound(clb_ref[hd], layer)
        f, kk = _hg_gates(f_ref[hd], lb)
        f_t = _pad_rows_t(f)
        k_t = _pad_rows_t(kk)
        qs = q_ref[hd] * (DK_C ** -0.5)
        v = v_ref[hd]
        vpad = jnp.where(first_rows, jnp.tile(v, (LANE // bb, 1)), 0.0).astype(BF16)
        orows = []
        for i in range(bb):
            fcol = jnp.broadcast_to(f_t[:, i:i + 1], (DK_C, DV_C))
            kv = _dot(jnp.where(lane == i, k_t, 0.0).astype(BF16), vpad)
            sn = fcol * s_ref[i, hd] + kv
            so_ref[i, hd] = sn
            qrow = jnp.broadcast_to(qs[i:i + 1, :], (2 * SUBLANE, DK_C)).astype(BF16)
            orows.append(_dot(qrow, sn.astype(BF16))[0:1, :])
        o = jnp.concatenate(orows, axis=0)
        og_ref[hd] = _hg_out(o, g_ref[hd], cnw_ref[hd])


def _c_sample(u, s_c, w, layer):
    bsz = s_c.shape[0]
    bb = SAMPLE_BB
    assert bsz % bb == 0
    depth = w["clb"].shape[1]

    def part(k):
        return pl.BlockSpec((H_C, bb, LANE), lambda i: (k, i, 0))

    c3 = lambda i: (0, 0, 0)
    og, so = pl.pallas_call(
        functools.partial(_c_sample_kernel, layer=layer), grid=(bsz // bb,),
        in_specs=[part(0), part(1), part(2), part(3),
                  pl.BlockSpec((bb, H_C, DK_C, DV_C), lambda i: (i, 0, 0, 0)),
                  pl.BlockSpec((H_C, depth, DK_C), c3), pl.BlockSpec((H_C, 1, DV_C), c3)],
        out_specs=[pl.BlockSpec((H_C, bb, LANE), lambda i: (0, i, 0)),
                   pl.BlockSpec((bb, H_C, DK_C, DV_C), lambda i: (i, 0, 0, 0))],
        out_shape=[jax.ShapeDtypeStruct((H_C, bsz, DV_C), F32), jax.ShapeDtypeStruct(s_c.shape, F32)],
        compiler_params=_params("parallel"),
        name="c_sample",
    )(u, u, u, u, s_c, w["clb"], w["cnw"])
    return og, so


def _row(v, width=None):
    v = v.astype(F32).reshape(1, -1)
    if width is not None and v.shape[1] < width:
        v = jnp.pad(v, ((0, 0), (0, width - v.shape[1])))
    return v


def kernel(x_prompt, x_sample, p_prompt, p_sample, state_a_h, state_a_conv, state_b_ssm, state_b_conv, state_c,
           norm_w, norm_f, ab_w_in, a_conv_w, a_conv_b, a_w_r, a_b_r, a_w_i, a_b_i, a_lam, b_conv_w, b_conv_b,
           b_dt_bias, b_a_log, b_d, b_norm_w, ab_w_out, c_w_in, c_lb, c_norm_w, c_w_out, ple_proj, ple_gate):
    depth = norm_w.shape[0]
    bp, seq, _ = x_prompt.shape
    bs = x_sample.shape[0]
    hp = x_prompt.reshape(bp * seq, D_MODEL)
    hs = x_sample.reshape(bs, D_MODEL)
    pp = p_prompt.reshape(depth, bp * seq, D_PLE)
    ps = p_sample.reshape(depth, bs, D_PLE)
    gam, lvl = _hg_gamma(), _hg_level_table()
    clb = c_lb.astype(F32).reshape(depth, H_C, DK_C).transpose(1, 0, 2)

    ah_p, ac_p, bs_p, bc_p, c_p = [], [], [], [], []
    ah_s, ac_s, bs_s, bc_s, c_s = [], [], [], [], []
    for i in range(depth):
        j = i // 2
        final = i == depth - 1
        wg = (0.5 * ple_gate[i]).astype(BF16)
        wp = (0.5 * ple_proj[i]).astype(BF16)
        if i % 2 == 0:
            col_scale = jnp.concatenate([jnp.ones((D_A,), F32), jnp.full((D_A + D_B,), 0.5, F32),
                                         jnp.ones((CONV_DIM_B,), F32)])
            w_ab_t = ab_w_in[j].T
            w_ab = (w_ab_t[:AB_MAIN] * col_scale[:, None]).astype(BF16)
            w_dt = jnp.pad(w_ab_t[AB_MAIN:], ((0, LANE - H_B), (0, 0))).astype(BF16)
            wo = ab_w_out[j].astype(BF16)
            w = dict(acw=a_conv_w[j].astype(F32), acb=_row(a_conv_b[j]),
                     wr=a_w_r[j].astype(BF16), br=_row(a_b_r[j]), wi=a_w_i[j].astype(BF16), bi=_row(a_b_i[j]),
                     lam=_row(a_lam[j]), bcw=0.5 * b_conv_w[j].astype(F32), bcb=0.5 * _row(b_conv_b[j]),
                     dtb=_row(b_dt_bias[j], LANE), alog=_row(b_a_log[j], LANE),
                     dexp=_row(jnp.repeat(b_d[j], HD_B)), bnw=_row(b_norm_w[j]))
            u, dtr = _in_proj(hp, norm_w[i], w_ab, w_dt, w_t=True)
            hp, s1, s2, s3, s4 = _ab_prompt(u, dtr, hp, pp, i, wo, wg, wp, norm_f, bp, seq, w, final=final)
            ah_p.append(s1); ac_p.append(s2); bs_p.append(s3); bc_p.append(s4)
            u, dtr = _in_proj(hs, norm_w[i], w_ab, w_dt, w_t=True)
            mix, s1, s2, s3, s4 = _ab_sample(u, dtr, state_a_h[j], state_a_conv[j], state_b_ssm[j],
                                             state_b_conv[j], w)
            ah_s.append(s1); ac_s.append(s2); bs_s.append(s3); bc_s.append(s4)
            hs = _out_proj(mix, hs, ps, i, wo, wg, wp, norm_f, head_major=False, final=final)
        else:
            col_scale = jnp.concatenate([jnp.ones((HK_C,), F32), jnp.full((HK_C,), 0.5, F32),
                                         jnp.ones((D_C,), F32), jnp.full((D_C,), 0.5, F32)])
            w_in = (c_w_in[j] * col_scale).astype(BF16)
            wo = c_w_out[j].astype(BF16)
            w = dict(clb=clb, cnw=c_norm_w[j].astype(F32).reshape(H_C, 1, DV_C), gam=gam, lvl=lvl)
            u = _in_proj(hp, norm_w[i], w_in, head_major=True, head_major_dtype=BF16)
            og, s1 = _c_prompt(u, bp, seq, w, i)
            c_p.append(s1)
            hp = _out_proj(og, hp, pp, i, wo, wg, wp, norm_f, head_major=True, final=final)
            u = _in_proj(hs, norm_w[i], w_in, head_major=True)
            og, s1 = _c_sample(u, state_c[j], w, i)
            c_s.append(s1)
            hs = _out_proj(og, hs, ps, i, wo, wg, wp, norm_f, head_major=True, final=final)
    return (hp.reshape(bp, seq, D_MODEL), hs.reshape(bs, 1, D_MODEL),
            jnp.stack(ah_p), jnp.stack(ac_p), jnp.stack(bs_p), jnp.stack(bc_p), jnp.stack(c_p),
            jnp.stack(ah_s), jnp.stack(ac_s), jnp.stack(bs_s), jnp.stack(bc_s), jnp.stack(c_s))
```

```python
import functools

import jax
import jax.numpy as jnp
from jax import lax
from jax.experimental import pallas as pl
from jax.experimental.pallas import tpu as pltpu

F32 = jnp.float32
BF16 = jnp.bfloat16

D_MODEL = 1024
D_PLE = 256
EPS = 1e-6
CONV_W = 4
D_A = D_MODEL
A_BLOCKS = 8
A_BLK = D_A // A_BLOCKS
LRU_C = 8.0
D_B = D_MODEL
HD_B = 64
H_B = D_B // HD_B
N_B = 128
G_B = 2
CONV_DIM_B = D_B + 2 * G_B * N_B
D_C = 2 * D_MODEL
H_C = 16
DK_C = 128
DV_C = D_C // H_C
HK_C = H_C * DK_C
AB_MAIN = 2 * D_A + D_B + CONV_DIM_B
IN_C = 2 * HK_C + 2 * D_C

LANE = 128
SUBLANE = 8
BF16_ROWS = 16
LOG2_E = 1.4426950408889634
VMEM_LIMIT = 56 * 1024 * 1024

PROJ_TM = 2048
PROJ_VMEM_BUDGET = 46 * 1024 * 1024
SSD_CHUNK = 128
AB_CHUNKS = 2
HG_CHUNK = 64
HG_LEVELS = 6
HG_MXU_LEVELS = 3
HG_F_MIN = 1e-30
HG_BLOCK = 512
HG_UNROLL = 16
SAMPLE_BB = 8
SAMPLE_UNROLL = 4

_DN_TR = (((1,), (1,)), ((), ()))
_DN_TL = (((0,), (0,)), ((), ()))


def _dot(a, b):
    return jnp.dot(a, b, preferred_element_type=F32)


def _dot_tr(a, b):
    return lax.dot_general(a, b, _DN_TR, preferred_element_type=F32)


def _dot_tl(a, b):
    return lax.dot_general(a, b, _DN_TL, preferred_element_type=F32)


def _silu_half(x_half):
    return x_half + x_half * jnp.tanh(x_half)


def _rmsnorm(x, w):
    return x * lax.rsqrt(jnp.mean(x * x, axis=-1, keepdims=True) + EPS) * w


def _params(*sem):
    return pltpu.CompilerParams(dimension_semantics=sem, vmem_limit_bytes=VMEM_LIMIT)


def _in_proj_kernel(x_ref, nw_ref, w_ref, *rest, has_extra, head_major, w_t):
    if has_extra:
        wx_ref, o_ref, ox_ref, xn_ref = rest
    else:
        o_ref, xn_ref = rest
    dot = _dot_tr if w_t else _dot

    @pl.when(pl.program_id(1) == 0)
    def _():
        xn_ref[...] = _rmsnorm(x_ref[...], nw_ref[...]).astype(BF16)
        if has_extra:
            ox_ref[...] = dot(xn_ref[...], wx_ref[...])

    acc = dot(xn_ref[...], w_ref[...])
    if head_major:
        for k in range(acc.shape[1] // LANE):
            o_ref[k] = acc[:, k * LANE:(k + 1) * LANE].astype(o_ref.dtype)
    else:
        o_ref[...] = acc


def _in_proj(x, nw, w, w_extra=None, *, head_major=False, w_t=False, head_major_dtype=F32):
    m, k = x.shape
    extra_cols = w_extra is not None
    n = w.shape[0 if w_t else 1]
    tm = min(m, PROJ_TM)
    out_bytes = jnp.dtype(head_major_dtype if head_major else F32).itemsize

    def vmem(c):
        extra = 2 * k * LANE * 2 + 2 * tm * LANE * 4 if extra_cols else 0
        return 2 * tm * k * 4 + tm * k * 2 + 2 * k * c * 2 + 2 * tm * c * out_bytes + extra

    tn = next(c for c in range(n, 0, -LANE) if n % c == 0 and vmem(c) <= PROJ_VMEM_BUDGET)
    assert m % tm == 0 and n % tn == 0 and tn % LANE == 0
    grid = (m // tm, n // tn)
    wspec = (lambda cols, idx: pl.BlockSpec((cols, k), lambda i, j: (idx(j), 0))) if w_t else \
            (lambda cols, idx: pl.BlockSpec((k, cols), lambda i, j: (0, idx(j))))
    in_specs = [pl.BlockSpec((tm, k), lambda i, j: (i, 0)),
                pl.BlockSpec((1, k), lambda i, j: (0, 0)),
                wspec(tn, lambda j: j)]
    args = [x, nw.reshape(1, k), w]
    if head_major:
        out_shape = [jax.ShapeDtypeStruct((n // LANE, m, LANE), head_major_dtype)]
        out_specs = [pl.BlockSpec((tn // LANE, tm, LANE), lambda i, j: (j, i, 0))]
    else:
        out_shape = [jax.ShapeDtypeStruct((m, n), F32)]
        out_specs = [pl.BlockSpec((tm, tn), lambda i, j: (i, j))]
    if extra_cols:
        in_specs.append(wspec(LANE, lambda j: 0))
        args.append(w_extra)
        out_shape.append(jax.ShapeDtypeStruct((m, LANE), F32))
        out_specs.append(pl.BlockSpec((tm, LANE), lambda i, j: (i, 0)))
    outs = pl.pallas_call(
        functools.partial(_in_proj_kernel, has_extra=extra_cols, head_major=head_major, w_t=w_t),
        grid=grid, in_specs=in_specs, out_specs=out_specs, out_shape=out_shape,
        scratch_shapes=[pltpu.VMEM((tm, k), BF16)],
        compiler_params=_params("parallel", "arbitrary"),
        name="in_proj",
    )(*args)
    return outs if extra_cols else outs[0]


def _residual_ple(h, p_ref, wg_ref, wp_ref, nf_ref, final):
    gate_t = jnp.tanh(_dot(h.astype(BF16), wg_ref[...]))
    pe_half = _dot(p_ref[...].astype(BF16), wp_ref[...])
    h = h + pe_half + pe_half * gate_t
    return _rmsnorm(h, nf_ref[...]) if final else h


def _out_proj_kernel(mix_ref, h_ref, p_ref, wo_ref, wg_ref, wp_ref, nf_ref, o_ref, *, head_major, final):
    if head_major:
        mix = jnp.concatenate([mix_ref[k] for k in range(mix_ref.shape[0])], axis=-1)
    else:
        mix = mix_ref[...]
    h = h_ref[...] + _dot(mix.astype(BF16), wo_ref[...])
    o_ref[...] = _residual_ple(h, p_ref, wg_ref, wp_ref, nf_ref, final)


def _out_proj(mix, h, p, layer, wo, wg, wp, nf, *, head_major, final):
    m, d = h.shape
    tm = min(m, 512)
    assert m % tm == 0
    if head_major:
        mix_spec = pl.BlockSpec((mix.shape[0], tm, LANE), lambda i: (0, i, 0))
    else:
        mix_spec = pl.BlockSpec((tm, mix.shape[1]), lambda i: (i, 0))
    const = lambda i: (0, 0)
    return pl.pallas_call(
        functools.partial(_out_proj_kernel, head_major=head_major, final=final),
        grid=(m // tm,),
        in_specs=[mix_spec,
                  pl.BlockSpec((tm, d), lambda i: (i, 0)),
                  pl.BlockSpec((None, tm, p.shape[2]), lambda i: (layer, i, 0)),
                  pl.BlockSpec(wo.shape, const), pl.BlockSpec(wg.shape, const),
                  pl.BlockSpec(wp.shape, const), pl.BlockSpec((1, d), const)],
        out_specs=pl.BlockSpec((tm, d), lambda i: (i, 0)),
        out_shape=jax.ShapeDtypeStruct((m, d), F32),
        compiler_params=_params("parallel"),
        name="out_proj",
    )(mix, h, p, wo, wg, wp, nf.reshape(1, d))


def _lru_gates(xc, wr_ref, br_ref, wi_ref, bi_ref, lam_ref):
    xcb = xc.astype(BF16)
    r_parts, i_parts = [], []
    for k in range(A_BLOCKS):
        xk = xcb[:, k * A_BLK:(k + 1) * A_BLK]
        r_parts.append(_dot(xk, wr_ref[k]))
        i_parts.append(_dot(xk, wi_ref[k]))
    r = jax.nn.sigmoid(jnp.concatenate(r_parts, axis=-1) + br_ref[...])
    gi = jax.nn.sigmoid(jnp.concatenate(i_parts, axis=-1) + bi_ref[...])
    log_a = (-LRU_C) * r * jax.nn.softplus(-lam_ref[...])
    a = jnp.exp(log_a)
    v = jnp.tanh(-log_a) * (a * a + 1.0)
    u = jnp.where(v > 0.0, v * lax.rsqrt(v), 0.0) * (gi * xc)
    return a, u


def _group_norm_gate(y, bx, z_half, dexp_ref, bnw_ref):
    y = (y + dexp_ref[...] * bx) * _silu_half(z_half)
    gw = D_B // G_B
    parts = []
    for g in range(G_B):
        yg = y[:, g * gw:(g + 1) * gw]
        parts.append(yg * lax.rsqrt(jnp.mean(yg * yg, axis=-1, keepdims=True) + EPS))
    return jnp.concatenate(parts, axis=-1) * bnw_ref[...]


def _ab_prompt_kernel(ax_ref, ag_ref, z_ref, xbc_ref, dt_ref,
                      acw_ref, acb_ref, wr_ref, br_ref, wi_ref, bi_ref, lam_ref,
                      bcw_ref, bcb_ref, dtb_ref, alog_ref, dexp_ref, bnw_ref,
                      hin_ref, p_ref, wo_ref, wg_ref, wp_ref, nf_ref,
                      hout_ref, *state_refs, final):
    for k in range(AB_CHUNKS):
        r = pl.ds(k * SSD_CHUNK, SSD_CHUNK)
        _ab_prompt_chunk(ax_ref.at[r], ag_ref.at[r], z_ref.at[r], xbc_ref.at[r], dt_ref.at[r],
                         acw_ref, acb_ref, wr_ref, br_ref, wi_ref, bi_ref, lam_ref,
                         bcw_ref, bcb_ref, dtb_ref, alog_ref, dexp_ref, bnw_ref,
                         hin_ref.at[r], p_ref.at[r], wo_ref, wg_ref, wp_ref, nf_ref,
                         hout_ref.at[r], *state_refs, final=final, first=k == 0, last=k == AB_CHUNKS - 1)


def _ab_prompt_chunk(ax_ref, ag_ref, z_ref, xbc_ref, dt_ref,
                     acw_ref, acb_ref, wr_ref, br_ref, wi_ref, bi_ref, lam_ref,
                     bcw_ref, bcb_ref, dtb_ref, alog_ref, dexp_ref, bnw_ref,
                     hin_ref, p_ref, wo_ref, wg_ref, wp_ref, nf_ref,
                     hout_ref, ah_ref, ac_ref, bs_ref, bc_ref,
                     xpa_ref, xpb_ref, h_ref, s_ref, *, final, first, last):
    c = pl.program_id(1)
    t = SSD_CHUNK
    ntile = t // SUBLANE

    if first:
        @pl.when(c == 0)
        def _():
            xpa_ref[...] = jnp.zeros_like(xpa_ref)
            xpb_ref[...] = jnp.zeros_like(xpb_ref)
            h_ref[...] = jnp.zeros_like(h_ref)
            s_ref[...] = jnp.zeros_like(s_ref)

    def tiles(x):
        return [x[i * SUBLANE:(i + 1) * SUBLANE, :] for i in range(ntile)]

    def conv(x, tail_ref, w_ref, b_ref):
        sub = lax.broadcasted_iota(jnp.int32, (SUBLANE, x.shape[1]), 0)
        xt = [tail_ref[...]] + tiles(x)
        taps = [jnp.broadcast_to(w_ref[k:k + 1, :], (SUBLANE, x.shape[1])) for k in range(CONV_W)]
        bias = jnp.broadcast_to(b_ref[...], (SUBLANE, x.shape[1]))
        acc = [bias + taps[CONV_W - 1] * xt[i + 1] for i in range(ntile)]
        for s in range(1, CONV_W):
            wk = taps[CONV_W - 1 - s]
            for i in range(ntile):
                merged = jnp.where(sub >= SUBLANE - s, xt[i], xt[i + 1])
                acc[i] = acc[i] + wk * pltpu.roll(merged, s, 0)
        tail_ref[...] = xt[ntile]
        return jnp.concatenate(acc, axis=0)

    ax = ax_ref[...]
    xc = conv(ax, xpa_ref, acw_ref, acb_ref)
    a, u = _lru_gates(xc, wr_ref, br_ref, wi_ref, bi_ref, lam_ref)
    sub = lax.broadcasted_iota(jnp.int32, (SUBLANE, D_A), 0)
    at, ut = tiles(a), tiles(u)
    step = 1
    while step < SUBLANE:
        m = sub >= step
        for i in range(ntile):
            ut[i] = jnp.where(m, at[i] * pltpu.roll(ut[i], step, 0) + ut[i], ut[i])
            at[i] = jnp.where(m, at[i] * pltpu.roll(at[i], step, 0), at[i])
        step *= 2
    carry = h_ref[0:1, :]
    hs = []
    for i in range(ntile):
        hs.append(ut[i] + at[i] * carry)
        carry = hs[i][SUBLANE - 1:SUBLANE, :]
    h = jnp.concatenate(hs, axis=0)
    h_ref[0:1, :] = carry
    a_out = (h * _silu_half(ag_ref[...])).astype(BF16)

    xb = xbc_ref[...]
    xbc = _silu_half(conv(xb, xpb_ref, bcw_ref, bcb_ref))
    bx = xbc[:, 0:D_B]
    bxb = bx.astype(BF16)
    dt = jax.nn.softplus(dt_ref[...] + dtb_ref[...])
    adt = dt * (-jnp.exp(alog_ref[...]))
    ti = lax.broadcasted_iota(jnp.int32, (t, t), 0)
    si = lax.broadcasted_iota(jnp.int32, (t, t), 1)
    causal = ti >= si
    acs = jnp.dot(causal.astype(F32), adt, preferred_element_type=F32,
                  precision=lax.Precision.HIGHEST)
    a_last = acs[t - 1:t, :]
    wq = jnp.exp(a_last - acs) * dt
    eacs = jnp.exp(acs)
    ealast = jnp.exp(a_last)
    acs_t = acs.T
    dt_t = dt.T
    lane = lax.broadcasted_iota(jnp.int32, (t, LANE), 1)
    rowi = lax.broadcasted_iota(jnp.int32, (LANE, N_B), 0)
    hpg = H_B // G_B
    ys = []
    cb = None
    for j in range(H_B // 2):
        g = (2 * j) // hpg
        bg = xbc[:, D_B + g * N_B:D_B + (g + 1) * N_B]
        cg = xbc[:, D_B + G_B * N_B + g * N_B:D_B + G_B * N_B + (g + 1) * N_B]
        if (2 * j) % hpg == 0:
            cb = _dot_tr(cg.astype(BF16), bg.astype(BF16))
        xpair = bxb[:, j * LANE:(j + 1) * LANE]
        sp = s_ref[j]
        spb = sp.astype(BF16)
        y_h, up_h = [], []
        for hh in range(2):
            hd = 2 * j + hh
            seg = jnp.broadcast_to(acs[:, hd:hd + 1], (t, t)) - jnp.broadcast_to(acs_t[hd:hd + 1, :], (t, t))
            lmat = jnp.exp(jnp.where(causal, seg, -1e30))
            mmat = (cb * lmat * jnp.broadcast_to(dt_t[hd:hd + 1, :], (t, t))).astype(BF16)
            ec = (jnp.broadcast_to(eacs[:, hd:hd + 1], (t, N_B)) * cg).astype(BF16)
            y_h.append(_dot(mmat, xpair) + _dot_tr(ec, spb))
            bw = (bg * jnp.broadcast_to(wq[:, hd:hd + 1], (t, N_B))).astype(BF16)
            up_h.append(_dot_tl(xpair, bw))
        ys.append(jnp.where(lane < HD_B, y_h[0], y_h[1]))
        dec = jnp.where(rowi < HD_B,
                        jnp.broadcast_to(ealast[:, 2 * j:2 * j + 1], (LANE, N_B)),
                        jnp.broadcast_to(ealast[:, 2 * j + 1:2 * j + 2], (LANE, N_B)))
        s_ref[j] = dec * sp + jnp.where(rowi < HD_B, up_h[0], up_h[1])
    y = jnp.concatenate(ys, axis=-1)
    b_out = _group_norm_gate(y, bx, z_ref[...], dexp_ref, bnw_ref).astype(BF16)

    hres = hin_ref[...] + _dot(a_out, wo_ref[0:D_A, :]) + _dot(b_out, wo_ref[D_A:D_A + D_B, :])
    hout_ref[...] = _residual_ple(hres, p_ref, wg_ref, wp_ref, nf_ref, final)

    if last:
        @pl.when(c == pl.num_programs(1) - 1)
        def _():
            ah_ref[0] = h[t - 1:t, :]
            ac_ref[0] = ax[t - (CONV_W - 1):t, :]
            bc_ref[0] = xb[t - (CONV_W - 1):t, :]
            for j in range(H_B // 2):
                sj = s_ref[j]
                bs_ref[0, 2 * j] = sj[0:HD_B, :]
                bs_ref[0, 2 * j + 1] = sj[HD_B:2 * HD_B, :]


def _ab_prompt(u, dtr, h, p, layer, wo, wg, wp, nf, bsz, seq, w, *, final):
    t = AB_CHUNKS * SSD_CHUNK
    assert seq % t == 0
    nc = seq // t
    m = bsz * seq
    rows = lambda b, c: b * nc + c
    cvec = lambda b, c: (0, 0)
    c3 = lambda b, c: (0, 0, 0)
    in_specs = [
        pl.BlockSpec((t, D_A), lambda b, c: (rows(b, c), 0)),
        pl.BlockSpec((t, D_A), lambda b, c: (rows(b, c), 1)),
        pl.BlockSpec((t, D_B), lambda b, c: (rows(b, c), 2)),
        pl.BlockSpec((t, CONV_DIM_B), lambda b, c: (rows(b, c), 2)),
        pl.BlockSpec((t, LANE), lambda b, c: (rows(b, c), 0)),
        pl.BlockSpec((CONV_W, D_A), cvec), pl.BlockSpec((1, D_A), cvec),
        pl.BlockSpec((A_BLOCKS, A_BLK, A_BLK), c3), pl.BlockSpec((1, D_A), cvec),
        pl.BlockSpec((A_BLOCKS, A_BLK, A_BLK), c3), pl.BlockSpec((1, D_A), cvec),
        pl.BlockSpec((1, D_A), cvec),
        pl.BlockSpec((CONV_W, CONV_DIM_B), cvec), pl.BlockSpec((1, CONV_DIM_B), cvec),
        pl.BlockSpec((1, LANE), cvec), pl.BlockSpec((1, LANE), cvec),
        pl.BlockSpec((1, D_B), cvec), pl.BlockSpec((1, D_B), cvec),
        pl.BlockSpec((t, D_MODEL), lambda b, c: (rows(b, c), 0)),
        pl.BlockSpec((None, t, p.shape[2]), lambda b, c: (layer, rows(b, c), 0)),
        pl.BlockSpec(wo.shape, cvec), pl.BlockSpec(wg.shape, cvec), pl.BlockSpec(wp.shape, cvec),
        pl.BlockSpec((1, D_MODEL), cvec),
    ]
    out_shape = [
        jax.ShapeDtypeStruct((m, D_MODEL), F32),
        jax.ShapeDtypeStruct((bsz, 1, D_A), F32),
        jax.ShapeDtypeStruct((bsz, CONV_W - 1, D_A), F32),
        jax.ShapeDtypeStruct((bsz, H_B, HD_B, N_B), F32),
        jax.ShapeDtypeStruct((bsz, CONV_W - 1, CONV_DIM_B), F32),
    ]
    out_specs = [
        pl.BlockSpec((t, D_MODEL), lambda b, c: (rows(b, c), 0)),
        pl.BlockSpec((1, 1, D_A), lambda b, c: (b, 0, 0)),
        pl.BlockSpec((1, CONV_W - 1, D_A), lambda b, c: (b, 0, 0)),
        pl.BlockSpec((1, H_B, HD_B, N_B), lambda b, c: (b, 0, 0, 0)),
        pl.BlockSpec((1, CONV_W - 1, CONV_DIM_B), lambda b, c: (b, 0, 0)),
    ]
    hout, ah, ac, bs, bc = pl.pallas_call(
        functools.partial(_ab_prompt_kernel, final=final),
        grid=(bsz, nc), in_specs=in_specs, out_specs=out_specs, out_shape=out_shape,
        scratch_shapes=[pltpu.VMEM((SUBLANE, D_A), F32), pltpu.VMEM((SUBLANE, CONV_DIM_B), F32),
                        pltpu.VMEM((SUBLANE, D_A), F32), pltpu.VMEM((H_B // 2, 2 * HD_B, N_B), F32)],
        compiler_params=_params("parallel", "arbitrary"),
        name="ab_prompt",
    )(u, u, u, u, dtr, w["acw"], w["acb"], w["wr"], w["br"], w["wi"], w["bi"], w["lam"],
      w["bcw"], w["bcb"], w["dtb"], w["alog"], w["dexp"], w["bnw"],
      h, p, wo, wg, wp, nf.reshape(1, D_MODEL))
    return hout, ah.reshape(bsz, D_A), ac, bs, bc


def _ab_sample_rows_kernel(ax_ref, ag_ref, xbc_ref, dt_ref, sah_ref, sac_ref, sbc_ref,
                           acw_ref, acb_ref, wr_ref, br_ref, wi_ref, bi_ref, lam_ref,
                           bcw_ref, bcb_ref, dtb_ref,
                           aout_ref, ah_ref, ac_ref, bc_ref, xact_ref, dts_ref):
    def conv1(x, buf_ref, w_ref, b_ref, nbuf_ref, width):
        y = b_ref[...] + w_ref[CONV_W - 1:CONV_W, :] * x
        for k in range(CONV_W - 1):
            y = y + w_ref[k:k + 1, :] * buf_ref[:, k * width:(k + 1) * width]
        for k in range(CONV_W - 2):
            nbuf_ref[:, k * width:(k + 1) * width] = buf_ref[:, (k + 1) * width:(k + 2) * width]
        nbuf_ref[:, (CONV_W - 2) * width:(CONV_W - 1) * width] = x
        return y

    xc = conv1(ax_ref[...], sac_ref, acw_ref, acb_ref, ac_ref, D_A)
    a, u = _lru_gates(xc, wr_ref, br_ref, wi_ref, bi_ref, lam_ref)
    h = a * sah_ref[...] + u
    ah_ref[...] = h
    aout_ref[...] = h * _silu_half(ag_ref[...])
    xact_ref[...] = _silu_half(conv1(xbc_ref[...], sbc_ref, bcw_ref, bcb_ref, bc_ref, CONV_DIM_B))
    dts_ref[...] = jax.nn.softplus(dt_ref[...] + dtb_ref[...])


def _pad_rows_t(x):
    pad = jnp.zeros((LANE - x.shape[0], x.shape[1]), F32)
    return jnp.concatenate([x, pad], axis=0).T


def _ab_sample_state_kernel(s_ref, xact_ref, dts_ref, z_ref, alog_ref, dexp_ref, bnw_ref,
                            so_ref, bout_ref, y_ref):
    bb = SAMPLE_BB
    xact = xact_ref[...]
    bx = xact[:, 0:D_B]
    dts = dts_ref[...]
    dec_t = _pad_rows_t(jnp.exp(dts * (-jnp.exp(alog_ref[...]))))
    dts_t = _pad_rows_t(dts)
    hpg = H_B // G_B
    for j in range(H_B // 2):
        g = (2 * j) // hpg
        xt = _pad_rows_t(bx[:, j * LANE:(j + 1) * LANE])
        dtp = jnp.concatenate([jnp.broadcast_to(dts_t[2 * j:2 * j + 1, :], (HD_B, LANE)),
                               jnp.broadcast_to(dts_t[2 * j + 1:2 * j + 2, :], (HD_B, LANE))], axis=0)
        xdt = xt * dtp
        for i in range(bb):
            brow = jnp.broadcast_to(xact[i:i + 1, D_B + g * N_B:D_B + (g + 1) * N_B], (2 * HD_B, N_B))
            crow = jnp.broadcast_to(
                xact[i:i + 1, D_B + G_B * N_B + g * N_B:D_B + G_B * N_B + (g + 1) * N_B], (2 * SUBLANE, N_B))
            upd = jnp.broadcast_to(xdt[:, i:i + 1], (2 * HD_B, N_B)) * brow
            news = []
            for hh in range(2):
                hd = 2 * j + hh
                dec = jnp.broadcast_to(dec_t[hd:hd + 1, i:i + 1], (HD_B, N_B))
                sn = dec * s_ref[i, hd] + upd[hh * HD_B:(hh + 1) * HD_B, :]
                so_ref[i, hd] = sn
                news.append(sn)
            spair = jnp.concatenate(news, axis=0).astype(BF16)
            yrow = _dot_tr(crow.astype(BF16), spair)
            y_ref[i:i + 1, j * LANE:(j + 1) * LANE] = yrow[0:1, :]
    bout_ref[...] = _group_norm_gate(y_ref[...], bx, z_ref[...], dexp_ref, bnw_ref)


def _ab_sample(u, dtr, s_ah, s_ac, s_bs, s_bc, w):
    bsz = u.shape[0]
    full = lambda shape: pl.BlockSpec(shape, lambda i: tuple(0 for _ in shape))
    cw = CONV_W - 1
    aout, ah, ac, bc, xact, dts = pl.pallas_call(
        _ab_sample_rows_kernel, grid=(1,),
        in_specs=[pl.BlockSpec((bsz, D_A), lambda i: (0, 0)), pl.BlockSpec((bsz, D_A), lambda i: (0, 1)),
                  pl.BlockSpec((bsz, CONV_DIM_B), lambda i: (0, 2)), full((bsz, LANE)),
                  full((bsz, D_A)), full((bsz, cw * D_A)), full((bsz, cw * CONV_DIM_B)),
                  full((CONV_W, D_A)), full((1, D_A)),
                  full((A_BLOCKS, A_BLK, A_BLK)), full((1, D_A)),
                  full((A_BLOCKS, A_BLK, A_BLK)), full((1, D_A)), full((1, D_A)),
                  full((CONV_W, CONV_DIM_B)), full((1, CONV_DIM_B)), full((1, LANE))],
        out_specs=[full((bsz, D_A)), full((bsz, D_A)), full((bsz, cw * D_A)), full((bsz, cw * CONV_DIM_B)),
                   full((bsz, CONV_DIM_B)), full((bsz, LANE))],
        out_shape=[jax.ShapeDtypeStruct((bsz, D_A), F32), jax.ShapeDtypeStruct((bsz, D_A), F32),
                   jax.ShapeDtypeStruct((bsz, cw * D_A), F32), jax.ShapeDtypeStruct((bsz, cw * CONV_DIM_B), F32),
                   jax.ShapeDtypeStruct((bsz, CONV_DIM_B), F32), jax.ShapeDtypeStruct((bsz, LANE), F32)],
        compiler_params=_params("arbitrary"),
        name="ab_sample_rows",
    )(u, u, u, dtr, s_ah, s_ac.reshape(bsz, cw * D_A), s_bc.reshape(bsz, cw * CONV_DIM_B),
      w["acw"], w["acb"], w["wr"], w["br"], w["wi"], w["bi"], w["lam"], w["bcw"], w["bcb"], w["dtb"])

    bb = SAMPLE_BB
    assert bsz % bb == 0
    cvec = lambda i: (0, 0)
    bs, bout = pl.pallas_call(
        _ab_sample_state_kernel, grid=(bsz // bb,),
        in_specs=[pl.BlockSpec((bb, H_B, HD_B, N_B), lambda i: (i, 0, 0, 0)),
                  pl.BlockSpec((bb, CONV_DIM_B), lambda i: (i, 0)),
                  pl.BlockSpec((bb, LANE), lambda i: (i, 0)),
                  pl.BlockSpec((bb, D_B), lambda i: (i, 2)),
                  pl.BlockSpec((1, LANE), cvec), pl.BlockSpec((1, D_B), cvec), pl.BlockSpec((1, D_B), cvec)],
        out_specs=[pl.BlockSpec((bb, H_B, HD_B, N_B), lambda i: (i, 0, 0, 0)),
                   pl.BlockSpec((bb, D_B), lambda i: (i, 0))],
        out_shape=[jax.ShapeDtypeStruct(s_bs.shape, F32), jax.ShapeDtypeStruct((bsz, D_B), F32)],
        scratch_shapes=[pltpu.VMEM((bb, D_B), F32)],
        compiler_params=_params("parallel"),
        name="ab_sample_state",
    )(s_bs, xact, dts, u, w["alog"], w["dexp"], w["bnw"])
    mix = jnp.concatenate([aout, bout], axis=-1)
    return mix, ah, ac.reshape(bsz, cw, D_A), bs, bc.reshape(bsz, cw, CONV_DIM_B)


def _hg_lower_bound(clb, layer):
    mx = jnp.max(clb, axis=0, keepdims=True)
    ex = jnp.exp(clb - mx)
    return jnp.sum(ex[1:layer + 1], axis=0, keepdims=True) / jnp.sum(ex, axis=0, keepdims=True)


def _hg_gates(fx_half, lb):
    f = 0.5 * (1.0 + lb) + (0.5 * (1.0 - lb)) * jnp.tanh(fx_half)
    return f, 1.0 - f


def _hg_out(o, gate_half, cnw):
    return o * lax.rsqrt(jnp.mean(o * o, axis=-1, keepdims=True) + EPS) * cnw * (gate_half + gate_half * jnp.tanh(gate_half))


def _hg_gamma():
    import numpy as np
    q = HG_CHUNK
    t = np.arange(q)[:, None]
    tau = np.arange(q)[None, :]
    mats = [(tau <= t)]
    for l in range(1, HG_MXU_LEVELS):
        w = 1 << l
        ref = (t // (2 * w)) * (2 * w) + w - 1
        upper = (t % (2 * w)) >= w
        mats.append(np.where(upper, (tau > ref) & (tau <= t), (tau > t) & (tau <= ref)))
    gam = np.concatenate(mats, axis=0).astype(np.float32)
    return jnp.asarray(np.concatenate([gam, gam], axis=1), dtype=BF16)


def _hg_level_table():
    import numpy as np
    q = HG_CHUNK
    t = np.arange(q)[:, None]
    s = np.arange(q)[None, :]
    x = t ^ s
    lvl = np.floor(np.log2(np.maximum(x, 1))).astype(np.int32)
    return jnp.asarray(np.where(t > s, lvl, -1).astype(np.int32))


def _c_prompt_kernel(q_ref, f_ref, v_ref, g_ref, clb_ref, cnw_ref, gam_ref, lvl_ref,
                     og_ref, cs_ref, st_ref, *, layer):
    c = pl.program_id(1)
    last = pl.num_programs(1) - 1
    qc = HG_CHUNK

    @pl.when(c == 0)
    def _():
        st_ref[...] = jnp.zeros_like(st_ref)

    gam = gam_ref[...]
    ntile = qc // SUBLANE
    sub = lax.broadcasted_iota(jnp.int32, (SUBLANE, DK_C), 0)
    sub_levels = HG_MXU_LEVELS
    sub_upper = [(sub & (1 << l)) != 0 for l in range(sub_levels)]

    def tiles(x):
        return [x[i * SUBLANE:(i + 1) * SUBLANE, :] for i in range(ntile)]

    def gate_split(hd, rows):
        f, kk = _hg_gates(f_ref[hd, rows, :].astype(F32), _hg_lower_bound(clb_ref[hd], layer))
        g = jnp.log(jnp.maximum(f, HG_F_MIN)) * LOG2_E
        g1 = g.astype(BF16)
        g2 = (g - g1.astype(F32)).astype(BF16)
        return (f, kk), jnp.concatenate([g1, g2], axis=0)

    def scores(hd, rows, fk, sums):
        f, kk = fk
        qh = q_ref[hd, rows, :].astype(F32) * (DK_C ** -0.5)
        bcum = sums[0:qc]
        st = st_ref[hd]
        o = _dot_tr((qh * jnp.exp2(bcum)).astype(BF16), st.astype(BF16))
        qt, kt, ft, bt = tiles(qh), tiles(kk), tiles(f), tiles(bcum)
        prods = []
        for l in range(HG_LEVELS):
            if l == 0:
                xt = [jnp.where(sub_upper[0], qt[i] * ft[i], kt[i]) for i in range(ntile)]
            elif l < HG_MXU_LEVELS:
                dec = tiles(jnp.exp2(sums[l * qc:(l + 1) * qc]))
                xt = [jnp.where(sub_upper[l], qt[i], kt[i]) * dec[i] for i in range(ntile)]
            else:
                wt = 1 << (l - HG_MXU_LEVELS)
                xt = []
                for blk in range(0, ntile, 2 * wt):
                    ref = (blk + wt) * SUBLANE - 1
                    bref = jnp.broadcast_to(bcum[ref:ref + 1, :], (SUBLANE, DK_C))
                    xt += [kt[i] * jnp.exp2(bref - bt[i]) for i in range(blk, blk + wt)]
                    xt += [qt[i] * jnp.exp2(bt[i] - bref) for i in range(blk + wt, blk + 2 * wt)]
            x = jnp.concatenate(xt, axis=0).astype(BF16)
            half = (1 << l) // BF16_ROWS
            if half == 0:
                p = tiles(_dot_tr(x, x))
                prods.append({i: p[i] for i in range(ntile)})
            else:
                ups = [r for r in range(qc // BF16_ROWS) if (r // half) & 1]
                pu = _dot_tr(jnp.concatenate([x[r * BF16_ROWS:(r + 1) * BF16_ROWS, :] for r in ups], axis=0), x)
                tpr = BF16_ROWS // SUBLANE
                prods.append({r * tpr + k: pu[(n * tpr + k) * SUBLANE:(n * tpr + k + 1) * SUBLANE, :]
                              for n, r in enumerate(ups) for k in range(tpr)})
        return qh, st, o, prods

    def level_masks():
        masks = {}
        for i in range(ntile):
            lv = lvl_ref[i * SUBLANE:(i + 1) * SUBLANE, :]
            for l in range(HG_LEVELS):
                if l < sub_levels or (i >> (l - sub_levels)) & 1:
                    masks[i, l] = lv == l
        return masks

    def combine(hd, rows, kk, bcum, qh, st, o, prods, masks):
        arows = []
        for i in range(ntile):
            a = jnp.zeros((SUBLANE, qc), F32)
            for l in range(HG_LEVELS):
                if (i, l) in masks:
                    a = jnp.where(masks[i, l], prods[l][i], a)
            arows.append(a)
        amat = jnp.concatenate(arows, axis=0)
        vb = v_ref[hd, rows, :].astype(BF16)
        o = o + _dot(amat.astype(BF16), vb) + jnp.sum(qh * kk, axis=-1, keepdims=True) * vb.astype(F32)
        blast = bcum[qc - 1:qc, :]
        kdec = (kk * jnp.exp2(blast - bcum)).astype(BF16)
        st_ref[hd] = st * jnp.exp2(blast) + _dot_tl(vb, kdec)
        return o

    nchunk = HG_BLOCK // qc

    def body(idx, carry):
        hg = idx // nchunk
        rows = pl.ds(pl.multiple_of((idx % nchunk) * qc, qc), qc)
        heads = [hg * HG_UNROLL + k for k in range(HG_UNROLL)]
        gs = [gate_split(hd, rows) for hd in heads]
        sums = _dot(gam, jnp.concatenate([s for _, s in gs], axis=1))
        sums = [sums[:, k * DK_C:(k + 1) * DK_C] for k in range(HG_UNROLL)]
        sc = [scores(hd, rows, gs[k][0], sums[k]) for k, hd in enumerate(heads)]
        masks = level_masks()
        outs = [combine(hd, rows, gs[k][0][1], sums[k][0:qc], *sc[k], masks) for k, hd in enumerate(heads)]
        for k, hd in enumerate(heads):
            og_ref[hd, rows, :] = _hg_out(outs[k], g_ref[hd, rows, :].astype(F32), cnw_ref[hd]).astype(BF16)
        return carry

    lax.fori_loop(0, (H_C // HG_UNROLL) * nchunk, body, 0)

    @pl.when(c == last)
    def _():
        for hd in range(H_C):
            cs_ref[0, hd] = st_ref[hd].T


def _c_prompt(u, bsz, seq, w, layer):
    tb = HG_BLOCK
    assert seq % tb == 0 and tb % HG_CHUNK == 0 and (1 << HG_LEVELS) == HG_CHUNK
    nc = seq // tb
    m = bsz * seq
    depth = w["clb"].shape[1]

    def part(k):
        return pl.BlockSpec((H_C, tb, LANE), lambda b, c: (k, b * nc + c, 0))

    c2 = lambda b, c: (0, 0)
    c3 = lambda b, c: (0, 0, 0)
    og, cs = pl.pallas_call(
        functools.partial(_c_prompt_kernel, layer=layer), grid=(bsz, nc),
        in_specs=[part(0), part(1), part(2), part(3),
                  pl.BlockSpec((H_C, depth, DK_C), c3), pl.BlockSpec((H_C, 1, DV_C), c3),
                  pl.BlockSpec(w["gam"].shape, c2), pl.BlockSpec(w["lvl"].shape, c2)],
        out_specs=[pl.BlockSpec((H_C, tb, LANE), lambda b, c: (0, b * nc + c, 0)),
                   pl.BlockSpec((1, H_C, DK_C, DV_C), lambda b, c: (b, 0, 0, 0))],
        out_shape=[jax.ShapeDtypeStruct((H_C, m, DV_C), BF16),
                   jax.ShapeDtypeStruct((bsz, H_C, DK_C, DV_C), F32)],
        scratch_shapes=[pltpu.VMEM((H_C, DV_C, DK_C), F32)],
        compiler_params=_params("parallel", "arbitrary"),
        name="c_prompt",
    )(u, u, u, u, w["clb"], w["cnw"], w["gam"], w["lvl"])
    return og, cs


def _c_sample_kernel(q_ref, f_ref, v_ref, g_ref, s_ref, clb_ref, cnw_ref, og_ref, so_ref, *, layer):
    bb = SAMPLE_BB
    lane = lax.broadcasted_iota(jnp.int32, (DK_C, LANE), 1)
    first_rows = lax.broadcasted_iota(jnp.int32, (LANE, DV_C), 0) < bb

    def head(hd, carry):
        lb = _hg_lower_bound(clb_ref[hd], layer)
        f, kk = _hg_gates(f_ref[hd], lb)
        f_t = _pad_rows_t(f)
        k_t = _pad_rows_t(kk)
        qs = q_ref[hd] * (DK_C ** -0.5)
        v = v_ref[hd]
        vpad = jnp.where(first_rows, jnp.tile(v, (LANE // bb, 1)), 0.0).astype(BF16)
        orows = []
        for i in range(bb):
            fcol = jnp.broadcast_to(f_t[:, i:i + 1], (DK_C, DV_C))
            kv = _dot(jnp.where(lane == i, k_t, 0.0).astype(BF16), vpad)
            sn = fcol * s_ref[i, hd] + kv
            so_ref[i, hd] = sn
            qrow = jnp.broadcast_to(qs[i:i + 1, :], (2 * SUBLANE, DK_C)).astype(BF16)
            orows.append(_dot(qrow, sn.astype(BF16))[0:1, :])
        o = jnp.concatenate(orows, axis=0)
        og_ref[hd] = _hg_out(o, g_ref[hd], cnw_ref[hd])
        return carry

    lax.fori_loop(0, H_C, head, 0, unroll=SAMPLE_UNROLL)


def _c_sample(u, s_c, w, layer):
    bsz = s_c.shape[0]
    bb = SAMPLE_BB
    assert bsz % bb == 0
    depth = w["clb"].shape[1]

    def part(k):
        return pl.BlockSpec((H_C, bb, LANE), lambda i: (k, i, 0))

    c3 = lambda i: (0, 0, 0)
    og, so = pl.pallas_call(
        functools.partial(_c_sample_kernel, layer=layer), grid=(bsz // bb,),
        in_specs=[part(0), part(1), part(2), part(3),
                  pl.BlockSpec((bb, H_C, DK_C, DV_C), lambda i: (i, 0, 0, 0)),
                  pl.BlockSpec((H_C, depth, DK_C), c3), pl.BlockSpec((H_C, 1, DV_C), c3)],
        out_specs=[pl.BlockSpec((H_C, bb, LANE), lambda i: (0, i, 0)),
                   pl.BlockSpec((bb, H_C, DK_C, DV_C), lambda i: (i, 0, 0, 0))],
        out_shape=[jax.ShapeDtypeStruct((H_C, bsz, DV_C), F32), jax.ShapeDtypeStruct(s_c.shape, F32)],
        compiler_params=_params("parallel"),
        name="c_sample",
    )(u, u, u, u, s_c, w["clb"], w["cnw"])
    return og, so


def _row(v, width=None):
    v = v.astype(F32).reshape(1, -1)
    if width is not None and v.shape[1] < width:
        v = jnp.pad(v, ((0, 0), (0, width - v.shape[1])))
    return v


def kernel(x_prompt, x_sample, p_prompt, p_sample, state_a_h, state_a_conv, state_b_ssm, state_b_conv, state_c,
           norm_w, norm_f, ab_w_in, a_conv_w, a_conv_b, a_w_r, a_b_r, a_w_i, a_b_i, a_lam, b_conv_w, b_conv_b,
           b_dt_bias, b_a_log, b_d, b_norm_w, ab_w_out, c_w_in, c_lb, c_norm_w, c_w_out, ple_proj, ple_gate):
    depth = norm_w.shape[0]
    bp, seq, _ = x_prompt.shape
    bs = x_sample.shape[0]
    hp = x_prompt.reshape(bp * seq, D_MODEL)
    hs = x_sample.reshape(bs, D_MODEL)
    pp = p_prompt.reshape(depth, bp * seq, D_PLE)
    ps = p_sample.reshape(depth, bs, D_PLE)
    gam, lvl = _hg_gamma(), _hg_level_table()
    clb = c_lb.astype(F32).reshape(depth, H_C, DK_C).transpose(1, 0, 2)

    ah_p, ac_p, bs_p, bc_p, c_p = [], [], [], [], []
    ah_s, ac_s, bs_s, bc_s, c_s = [], [], [], [], []
    for i in range(depth):
        j = i // 2
        final = i == depth - 1
        wg = (0.5 * ple_gate[i]).astype(BF16)
        wp = (0.5 * ple_proj[i]).astype(BF16)
        if i % 2 == 0:
            col_scale = jnp.concatenate([jnp.ones((D_A,), F32), jnp.full((D_A + D_B,), 0.5, F32),
                                         jnp.ones((CONV_DIM_B,), F32)])
            w_ab_t = ab_w_in[j].T
            w_ab = (w_ab_t[:AB_MAIN] * col_scale[:, None]).astype(BF16)
            w_dt = jnp.pad(w_ab_t[AB_MAIN:], ((0, LANE - H_B), (0, 0))).astype(BF16)
            wo = ab_w_out[j].astype(BF16)
            w = dict(acw=a_conv_w[j].astype(F32), acb=_row(a_conv_b[j]),
                     wr=a_w_r[j].astype(BF16), br=_row(a_b_r[j]), wi=a_w_i[j].astype(BF16), bi=_row(a_b_i[j]),
                     lam=_row(a_lam[j]), bcw=0.5 * b_conv_w[j].astype(F32), bcb=0.5 * _row(b_conv_b[j]),
                     dtb=_row(b_dt_bias[j], LANE), alog=_row(b_a_log[j], LANE),
                     dexp=_row(jnp.repeat(b_d[j], HD_B)), bnw=_row(b_norm_w[j]))
            u, dtr = _in_proj(hp, norm_w[i], w_ab, w_dt, w_t=True)
            hp, s1, s2, s3, s4 = _ab_prompt(u, dtr, hp, pp, i, wo, wg, wp, norm_f, bp, seq, w, final=final)
            ah_p.append(s1); ac_p.append(s2); bs_p.append(s3); bc_p.append(s4)
            u, dtr = _in_proj(hs, norm_w[i], w_ab, w_dt, w_t=True)
            mix, s1, s2, s3, s4 = _ab_sample(u, dtr, state_a_h[j], state_a_conv[j], state_b_ssm[j],
                                             state_b_conv[j], w)
            ah_s.append(s1); ac_s.append(s2); bs_s.append(s3); bc_s.append(s4)
            hs = _out_proj(mix, hs, ps, i, wo, wg, wp, norm_f, head_major=False, final=final)
        else:
            col_scale = jnp.concatenate([jnp.ones((HK_C,), F32), jnp.full((HK_C,), 0.5, F32),
                                         jnp.ones((D_C,), F32), jnp.full((D_C,), 0.5, F32)])
            w_in = (c_w_in[j] * col_scale).astype(BF16)
            wo = c_w_out[j].astype(BF16)
            w = dict(clb=clb, cnw=c_norm_w[j].astype(F32).reshape(H_C, 1, DV_C), gam=gam, lvl=lvl)
            u = _in_proj(hp, norm_w[i], w_in, head_major=True, head_major_dtype=BF16)
            og, s1 = _c_prompt(u, bp, seq, w, i)
            c_p.append(s1)
            hp = _out_proj(og, hp, pp, i, wo, wg, wp, norm_f, head_major=True, final=final)
            u = _in_proj(hs, norm_w[i], w_in, head_major=True)
            og, s1 = _c_sample(u, state_c[j], w, i)
            c_s.append(s1)
            hs = _out_proj(og, hs, ps, i, wo, wg, wp, norm_f, head_major=True, final=final)
    return (hp.reshape(bp, seq, D_MODEL), hs.reshape(bs, 1, D_MODEL),
            jnp.stack(ah_p), jnp.stack(ac_p), jnp.stack(bs_p), jnp.stack(bc_p), jnp.stack(c_p),
            jnp.stack(ah_s), jnp.stack(ac_s), jnp.stack(bs_s), jnp.stack(bc_s), jnp.stack(c_s))
```

```python
import functools

import jax
import jax.numpy as jnp
from jax import lax
from jax.experimental import pallas as pl
from jax.experimental.pallas import tpu as pltpu

F32 = jnp.float32
BF16 = jnp.bfloat16

D_MODEL = 1024
D_PLE = 256
EPS = 1e-6
CONV_W = 4
D_A = D_MODEL
A_BLOCKS = 8
A_BLK = D_A // A_BLOCKS
LRU_C = 8.0
D_B = D_MODEL
HD_B = 64
H_B = D_B // HD_B
N_B = 128
G_B = 2
CONV_DIM_B = D_B + 2 * G_B * N_B
D_C = 2 * D_MODEL
H_C = 16
DK_C = 128
DV_C = D_C // H_C
HK_C = H_C * DK_C
AB_MAIN = 2 * D_A + D_B + CONV_DIM_B
IN_C = 2 * HK_C + 2 * D_C

LANE = 128
SUBLANE = 8
BF16_ROWS = 16
LOG2_E = 1.4426950408889634
VMEM_LIMIT = 56 * 1024 * 1024

PROJ_TM = 2048
PROJ_VMEM_BUDGET = 46 * 1024 * 1024
OUT_PROJ_TM = 1024
SSD_CHUNK = 128
AB_CHUNKS = 2
HG_CHUNK = 64
HG_LEVELS = 6
HG_MXU_LEVELS = 3
HG_F_MIN = 1e-30
HG_BLOCK = 512
HG_UNROLL = 16
SAMPLE_BB = 8
SAMPLE_UNROLL = 4

_DN_TR = (((1,), (1,)), ((), ()))
_DN_TL = (((0,), (0,)), ((), ()))


def _dot(a, b):
    return jnp.dot(a, b, preferred_element_type=F32)


def _dot_tr(a, b):
    return lax.dot_general(a, b, _DN_TR, preferred_element_type=F32)


def _dot_tl(a, b):
    return lax.dot_general(a, b, _DN_TL, preferred_element_type=F32)


def _silu_half(x_half):
    return x_half + x_half * jnp.tanh(x_half)


def _rmsnorm(x, w):
    return x * lax.rsqrt(jnp.mean(x * x, axis=-1, keepdims=True) + EPS) * w


def _params(*sem):
    return pltpu.CompilerParams(dimension_semantics=sem, vmem_limit_bytes=VMEM_LIMIT)


def _in_proj_kernel(x_ref, nw_ref, w_ref, *rest, has_extra, head_major, w_t):
    if has_extra:
        wx_ref, o_ref, ox_ref, xn_ref = rest
    else:
        o_ref, xn_ref = rest
    dot = _dot_tr if w_t else _dot

    @pl.when(pl.program_id(1) == 0)
    def _():
        xn_ref[...] = _rmsnorm(x_ref[...], nw_ref[...]).astype(BF16)
        if has_extra:
            ox_ref[...] = dot(xn_ref[...], wx_ref[...])

    acc = dot(xn_ref[...], w_ref[...])
    if head_major:
        for k in range(acc.shape[1] // LANE):
            o_ref[k] = acc[:, k * LANE:(k + 1) * LANE].astype(o_ref.dtype)
    else:
        o_ref[...] = acc


def _in_proj(x, nw, w, w_extra=None, *, head_major=False, w_t=False, head_major_dtype=F32):
    m, k = x.shape
    extra_cols = w_extra is not None
    n = w.shape[0 if w_t else 1]
    tm = min(m, PROJ_TM)
    out_bytes = jnp.dtype(head_major_dtype if head_major else F32).itemsize

    def vmem(c):
        extra = 2 * k * LANE * 2 + 2 * tm * LANE * 4 if extra_cols else 0
        return 2 * tm * k * 4 + tm * k * 2 + 2 * k * c * 2 + 2 * tm * c * out_bytes + extra

    tn = next(c for c in range(n, 0, -LANE) if n % c == 0 and vmem(c) <= PROJ_VMEM_BUDGET)
    assert m % tm == 0 and n % tn == 0 and tn % LANE == 0
    grid = (m // tm, n // tn)
    wspec = (lambda cols, idx: pl.BlockSpec((cols, k), lambda i, j: (idx(j), 0))) if w_t else \
            (lambda cols, idx: pl.BlockSpec((k, cols), lambda i, j: (0, idx(j))))
    in_specs = [pl.BlockSpec((tm, k), lambda i, j: (i, 0)),
                pl.BlockSpec((1, k), lambda i, j: (0, 0)),
                wspec(tn, lambda j: j)]
    args = [x, nw.reshape(1, k), w]
    if head_major:
        out_shape = [jax.ShapeDtypeStruct((n // LANE, m, LANE), head_major_dtype)]
        out_specs = [pl.BlockSpec((tn // LANE, tm, LANE), lambda i, j: (j, i, 0))]
    else:
        out_shape = [jax.ShapeDtypeStruct((m, n), F32)]
        out_specs = [pl.BlockSpec((tm, tn), lambda i, j: (i, j))]
    if extra_cols:
        in_specs.append(wspec(LANE, lambda j: 0))
        args.append(w_extra)
        out_shape.append(jax.ShapeDtypeStruct((m, LANE), F32))
        out_specs.append(pl.BlockSpec((tm, LANE), lambda i, j: (i, 0)))
    outs = pl.pallas_call(
        functools.partial(_in_proj_kernel, has_extra=extra_cols, head_major=head_major, w_t=w_t),
        grid=grid, in_specs=in_specs, out_specs=out_specs, out_shape=out_shape,
        scratch_shapes=[pltpu.VMEM((tm, k), BF16)],
        compiler_params=_params("parallel", "arbitrary"),
        name="in_proj",
    )(*args)
    return outs if extra_cols else outs[0]


def _residual_ple(h, p_ref, wg_ref, wp_ref, nf_ref, final):
    gate_t = jnp.tanh(_dot(h.astype(BF16), wg_ref[...]))
    pe_half = _dot(p_ref[...].astype(BF16), wp_ref[...])
    h = h + pe_half + pe_half * gate_t
    return _rmsnorm(h, nf_ref[...]) if final else h


def _out_proj_kernel(mix_ref, h_ref, p_ref, wo_ref, wg_ref, wp_ref, nf_ref, o_ref, *, head_major, final):
    if head_major:
        mix = jnp.concatenate([mix_ref[k] for k in range(mix_ref.shape[0])], axis=-1)
    else:
        mix = mix_ref[...]
    h = h_ref[...] + _dot(mix.astype(BF16), wo_ref[...])
    o_ref[...] = _residual_ple(h, p_ref, wg_ref, wp_ref, nf_ref, final)


def _out_proj(mix, h, p, layer, wo, wg, wp, nf, *, head_major, final):
    m, d = h.shape
    tm = min(m, OUT_PROJ_TM)
    assert m % tm == 0
    if head_major:
        mix_spec = pl.BlockSpec((mix.shape[0], tm, LANE), lambda i: (0, i, 0))
    else:
        mix_spec = pl.BlockSpec((tm, mix.shape[1]), lambda i: (i, 0))
    const = lambda i: (0, 0)
    return pl.pallas_call(
        functools.partial(_out_proj_kernel, head_major=head_major, final=final),
        grid=(m // tm,),
        in_specs=[mix_spec,
                  pl.BlockSpec((tm, d), lambda i: (i, 0)),
                  pl.BlockSpec((None, tm, p.shape[2]), lambda i: (layer, i, 0)),
                  pl.BlockSpec(wo.shape, const), pl.BlockSpec(wg.shape, const),
                  pl.BlockSpec(wp.shape, const), pl.BlockSpec((1, d), const)],
        out_specs=pl.BlockSpec((tm, d), lambda i: (i, 0)),
        out_shape=jax.ShapeDtypeStruct((m, d), F32),
        compiler_params=_params("parallel"),
        name="out_proj",
    )(mix, h, p, wo, wg, wp, nf.reshape(1, d))


def _lru_gates(xc, wr_ref, br_ref, wi_ref, bi_ref, lam_ref):
    xcb = xc.astype(BF16)
    r_parts, i_parts = [], []
    for k in range(A_BLOCKS):
        xk = xcb[:, k * A_BLK:(k + 1) * A_BLK]
        r_parts.append(_dot(xk, wr_ref[k]))
        i_parts.append(_dot(xk, wi_ref[k]))
    tr = jnp.tanh(jnp.concatenate(r_parts, axis=-1) + br_ref[...])
    gi = jax.nn.sigmoid(jnp.concatenate(i_parts, axis=-1) + bi_ref[...])
    half_rate = (-0.5 * LRU_C) * jax.nn.softplus(-lam_ref[...])
    log_a = half_rate + half_rate * tr
    a = jnp.exp(log_a)
    v = jnp.tanh(-log_a) * (a * a + 1.0)
    u = jnp.where(v > 0.0, v * lax.rsqrt(v), 0.0) * (gi * xc)
    return a, u


def _group_norm_gate(y, bx, z_half, dexp_ref, bnw_ref):
    y = (y + dexp_ref[...] * bx) * _silu_half(z_half)
    gw = D_B // G_B
    parts = []
    for g in range(G_B):
        yg = y[:, g * gw:(g + 1) * gw]
        parts.append(yg * lax.rsqrt(jnp.mean(yg * yg, axis=-1, keepdims=True) + EPS))
    return jnp.concatenate(parts, axis=-1) * bnw_ref[...]


def _ab_prompt_kernel(ax_ref, ag_ref, z_ref, xbc_ref, dt_ref,
                      acw_ref, acb_ref, wr_ref, br_ref, wi_ref, bi_ref, lam_ref,
                      bcw_ref, bcb_ref, dtb_ref, alog_ref, dexp_ref, bnw_ref,
                      hin_ref, p_ref, wo_ref, wg_ref, wp_ref, nf_ref,
                      hout_ref, *state_refs, final):
    for k in range(AB_CHUNKS):
        r = pl.ds(k * SSD_CHUNK, SSD_CHUNK)
        _ab_prompt_chunk(ax_ref.at[r], ag_ref.at[r], z_ref.at[r], xbc_ref.at[r], dt_ref.at[r],
                         acw_ref, acb_ref, wr_ref, br_ref, wi_ref, bi_ref, lam_ref,
                         bcw_ref, bcb_ref, dtb_ref, alog_ref, dexp_ref, bnw_ref,
                         hin_ref.at[r], p_ref.at[r], wo_ref, wg_ref, wp_ref, nf_ref,
                         hout_ref.at[r], *state_refs, final=final, first=k == 0, last=k == AB_CHUNKS - 1)


def _ab_prompt_chunk(ax_ref, ag_ref, z_ref, xbc_ref, dt_ref,
                     acw_ref, acb_ref, wr_ref, br_ref, wi_ref, bi_ref, lam_ref,
                     bcw_ref, bcb_ref, dtb_ref, alog_ref, dexp_ref, bnw_ref,
                     hin_ref, p_ref, wo_ref, wg_ref, wp_ref, nf_ref,
                     hout_ref, ah_ref, ac_ref, bs_ref, bc_ref,
                     xpa_ref, xpb_ref, h_ref, s_ref, *, final, first, last):
    c = pl.program_id(1)
    t = SSD_CHUNK
    ntile = t // SUBLANE

    if first:
        @pl.when(c == 0)
        def _():
            xpa_ref[...] = jnp.zeros_like(xpa_ref)
            xpb_ref[...] = jnp.zeros_like(xpb_ref)
            h_ref[...] = jnp.zeros_like(h_ref)
            s_ref[...] = jnp.zeros_like(s_ref)

    def tiles(x):
        return [x[i * SUBLANE:(i + 1) * SUBLANE, :] for i in range(ntile)]

    def conv(x, tail_ref, w_ref, b_ref):
        sub = lax.broadcasted_iota(jnp.int32, (SUBLANE, x.shape[1]), 0)
        xt = [tail_ref[...]] + tiles(x)
        taps = [jnp.broadcast_to(w_ref[k:k + 1, :], (SUBLANE, x.shape[1])) for k in range(CONV_W)]
        bias = jnp.broadcast_to(b_ref[...], (SUBLANE, x.shape[1]))
        acc = [bias + taps[CONV_W - 1] * xt[i + 1] for i in range(ntile)]
        for s in range(1, CONV_W):
            wk = taps[CONV_W - 1 - s]
            for i in range(ntile):
                merged = jnp.where(sub >= SUBLANE - s, xt[i], xt[i + 1])
                acc[i] = acc[i] + wk * pltpu.roll(merged, s, 0)
        tail_ref[...] = xt[ntile]
        return jnp.concatenate(acc, axis=0)

    ax = ax_ref[...]
    xc = conv(ax, xpa_ref, acw_ref, acb_ref)
    a, u = _lru_gates(xc, wr_ref, br_ref, wi_ref, bi_ref, lam_ref)
    sub = lax.broadcasted_iota(jnp.int32, (SUBLANE, D_A), 0)
    at, ut = tiles(a), tiles(u)
    step = 1
    while step < SUBLANE:
        m = sub >= step
        for i in range(ntile):
            ut[i] = jnp.where(m, at[i] * pltpu.roll(ut[i], step, 0) + ut[i], ut[i])
            at[i] = jnp.where(m, at[i] * pltpu.roll(at[i], step, 0), at[i])
        step *= 2
    carry = h_ref[0:1, :]
    hs = []
    for i in range(ntile):
        hs.append(ut[i] + at[i] * carry)
        carry = hs[i][SUBLANE - 1:SUBLANE, :]
    h = jnp.concatenate(hs, axis=0)
    h_ref[0:1, :] = carry
    a_out = (h * _silu_half(ag_ref[...])).astype(BF16)

    xb = xbc_ref[...]
    xbc = _silu_half(conv(xb, xpb_ref, bcw_ref, bcb_ref))
    bx = xbc[:, 0:D_B]
    bxb = bx.astype(BF16)
    dt = jax.nn.softplus(dt_ref[...] + dtb_ref[...])
    adt = dt * (-LOG2_E * jnp.exp(alog_ref[...]))
    ti = lax.broadcasted_iota(jnp.int32, (t, t), 0)
    si = lax.broadcasted_iota(jnp.int32, (t, t), 1)
    causal = ti >= si
    acs = jnp.dot(causal.astype(F32), adt, preferred_element_type=F32,
                  precision=lax.Precision.HIGHEST)
    a_last = acs[t - 1:t, :]
    wq = jnp.exp2(a_last - acs) * dt
    eacs = jnp.exp2(acs)
    ealast = jnp.exp2(a_last)
    acs_t = acs.T
    dt_t = dt.T
    lane = lax.broadcasted_iota(jnp.int32, (t, LANE), 1)
    rowi = lax.broadcasted_iota(jnp.int32, (LANE, N_B), 0)
    hpg = H_B // G_B
    ys = []
    cb = None
    for j in range(H_B // 2):
        g = (2 * j) // hpg
        bg = xbc[:, D_B + g * N_B:D_B + (g + 1) * N_B]
        cg = xbc[:, D_B + G_B * N_B + g * N_B:D_B + G_B * N_B + (g + 1) * N_B]
        if (2 * j) % hpg == 0:
            cb = _dot_tr(cg.astype(BF16), bg.astype(BF16))
        xpair = bxb[:, j * LANE:(j + 1) * LANE]
        sp = s_ref[j]
        spb = sp.astype(BF16)
        y_h, up_h = [], []
        for hh in range(2):
            hd = 2 * j + hh
            seg = jnp.broadcast_to(acs[:, hd:hd + 1], (t, t)) - jnp.broadcast_to(acs_t[hd:hd + 1, :], (t, t))
            lmat = jnp.exp2(jnp.where(causal, seg, -1e30))
            mmat = (cb * lmat * jnp.broadcast_to(dt_t[hd:hd + 1, :], (t, t))).astype(BF16)
            ec = (jnp.broadcast_to(eacs[:, hd:hd + 1], (t, N_B)) * cg).astype(BF16)
            y_h.append(_dot(mmat, xpair) + _dot_tr(ec, spb))
            bw = (bg * jnp.broadcast_to(wq[:, hd:hd + 1], (t, N_B))).astype(BF16)
            up_h.append(_dot_tl(xpair, bw))
        ys.append(jnp.where(lane < HD_B, y_h[0], y_h[1]))
        dec = jnp.where(rowi < HD_B,
                        jnp.broadcast_to(ealast[:, 2 * j:2 * j + 1], (LANE, N_B)),
                        jnp.broadcast_to(ealast[:, 2 * j + 1:2 * j + 2], (LANE, N_B)))
        s_ref[j] = dec * sp + jnp.where(rowi < HD_B, up_h[0], up_h[1])
    y = jnp.concatenate(ys, axis=-1)
    b_out = _group_norm_gate(y, bx, z_ref[...], dexp_ref, bnw_ref).astype(BF16)

    hres = hin_ref[...] + _dot(a_out, wo_ref[0:D_A, :]) + _dot(b_out, wo_ref[D_A:D_A + D_B, :])
    hout_ref[...] = _residual_ple(hres, p_ref, wg_ref, wp_ref, nf_ref, final)

    if last:
        @pl.when(c == pl.num_programs(1) - 1)
        def _():
            ah_ref[0] = h[t - 1:t, :]
            ac_ref[0] = ax[t - (CONV_W - 1):t, :]
            bc_ref[0] = xb[t - (CONV_W - 1):t, :]
            for j in range(H_B // 2):
                sj = s_ref[j]
                bs_ref[0, 2 * j] = sj[0:HD_B, :]
                bs_ref[0, 2 * j + 1] = sj[HD_B:2 * HD_B, :]


def _ab_prompt(u, dtr, h, p, layer, wo, wg, wp, nf, bsz, seq, w, *, final):
    t = AB_CHUNKS * SSD_CHUNK
    assert seq % t == 0
    nc = seq // t
    m = bsz * seq
    rows = lambda b, c: b * nc + c
    cvec = lambda b, c: (0, 0)
    c3 = lambda b, c: (0, 0, 0)
    in_specs = [
        pl.BlockSpec((t, D_A), lambda b, c: (rows(b, c), 0)),
        pl.BlockSpec((t, D_A), lambda b, c: (rows(b, c), 1)),
        pl.BlockSpec((t, D_B), lambda b, c: (rows(b, c), 2)),
        pl.BlockSpec((t, CONV_DIM_B), lambda b, c: (rows(b, c), 2)),
        pl.BlockSpec((t, LANE), lambda b, c: (rows(b, c), 0)),
        pl.BlockSpec((CONV_W, D_A), cvec), pl.BlockSpec((1, D_A), cvec),
        pl.BlockSpec((A_BLOCKS, A_BLK, A_BLK), c3), pl.BlockSpec((1, D_A), cvec),
        pl.BlockSpec((A_BLOCKS, A_BLK, A_BLK), c3), pl.BlockSpec((1, D_A), cvec),
        pl.BlockSpec((1, D_A), cvec),
        pl.BlockSpec((CONV_W, CONV_DIM_B), cvec), pl.BlockSpec((1, CONV_DIM_B), cvec),
        pl.BlockSpec((1, LANE), cvec), pl.BlockSpec((1, LANE), cvec),
        pl.BlockSpec((1, D_B), cvec), pl.BlockSpec((1, D_B), cvec),
        pl.BlockSpec((t, D_MODEL), lambda b, c: (rows(b, c), 0)),
        pl.BlockSpec((None, t, p.shape[2]), lambda b, c: (layer, rows(b, c), 0)),
        pl.BlockSpec(wo.shape, cvec), pl.BlockSpec(wg.shape, cvec), pl.BlockSpec(wp.shape, cvec),
        pl.BlockSpec((1, D_MODEL), cvec),
    ]
    out_shape = [
        jax.ShapeDtypeStruct((m, D_MODEL), F32),
        jax.ShapeDtypeStruct((bsz, 1, D_A), F32),
        jax.ShapeDtypeStruct((bsz, CONV_W - 1, D_A), F32),
        jax.ShapeDtypeStruct((bsz, H_B, HD_B, N_B), F32),
        jax.ShapeDtypeStruct((bsz, CONV_W - 1, CONV_DIM_B), F32),
    ]
    out_specs = [
        pl.BlockSpec((t, D_MODEL), lambda b, c: (rows(b, c), 0)),
        pl.BlockSpec((1, 1, D_A), lambda b, c: (b, 0, 0)),
        pl.BlockSpec((1, CONV_W - 1, D_A), lambda b, c: (b, 0, 0)),
        pl.BlockSpec((1, H_B, HD_B, N_B), lambda b, c: (b, 0, 0, 0)),
        pl.BlockSpec((1, CONV_W - 1, CONV_DIM_B), lambda b, c: (b, 0, 0)),
    ]
    hout, ah, ac, bs, bc = pl.pallas_call(
        functools.partial(_ab_prompt_kernel, final=final),
        grid=(bsz, nc), in_specs=in_specs, out_specs=out_specs, out_shape=out_shape,
        scratch_shapes=[pltpu.VMEM((SUBLANE, D_A), F32), pltpu.VMEM((SUBLANE, CONV_DIM_B), F32),
                        pltpu.VMEM((SUBLANE, D_A), F32), pltpu.VMEM((H_B // 2, 2 * HD_B, N_B), F32)],
        compiler_params=_params("parallel", "arbitrary"),
        name="ab_prompt",
    )(u, u, u, u, dtr, w["acw"], w["acb"], w["wr"], w["br"], w["wi"], w["bi"], w["lam"],
      w["bcw"], w["bcb"], w["dtb"], w["alog"], w["dexp"], w["bnw"],
      h, p, wo, wg, wp, nf.reshape(1, D_MODEL))
    return hout, ah.reshape(bsz, D_A), ac, bs, bc


def _ab_sample_rows_kernel(ax_ref, ag_ref, xbc_ref, dt_ref, sah_ref, sac_ref, sbc_ref,
                           acw_ref, acb_ref, wr_ref, br_ref, wi_ref, bi_ref, lam_ref,
                           bcw_ref, bcb_ref, dtb_ref,
                           aout_ref, ah_ref, ac_ref, bc_ref, xact_ref, dts_ref):
    def conv1(x, buf_ref, w_ref, b_ref, nbuf_ref, width):
        y = b_ref[...] + w_ref[CONV_W - 1:CONV_W, :] * x
        for k in range(CONV_W - 1):
            y = y + w_ref[k:k + 1, :] * buf_ref[:, k * width:(k + 1) * width]
        for k in range(CONV_W - 2):
            nbuf_ref[:, k * width:(k + 1) * width] = buf_ref[:, (k + 1) * width:(k + 2) * width]
        nbuf_ref[:, (CONV_W - 2) * width:(CONV_W - 1) * width] = x
        return y

    xc = conv1(ax_ref[...], sac_ref, acw_ref, acb_ref, ac_ref, D_A)
    a, u = _lru_gates(xc, wr_ref, br_ref, wi_ref, bi_ref, lam_ref)
    h = a * sah_ref[...] + u
    ah_ref[...] = h
    aout_ref[...] = h * _silu_half(ag_ref[...])
    xact_ref[...] = _silu_half(conv1(xbc_ref[...], sbc_ref, bcw_ref, bcb_ref, bc_ref, CONV_DIM_B))
    dts_ref[...] = jax.nn.softplus(dt_ref[...] + dtb_ref[...])


def _pad_rows_t(x):
    pad = jnp.zeros((LANE - x.shape[0], x.shape[1]), F32)
    return jnp.concatenate([x, pad], axis=0).T


def _ab_sample_state_kernel(s_ref, xact_ref, dts_ref, z_ref, alog_ref, dexp_ref, bnw_ref,
                            so_ref, bout_ref, y_ref):
    bb = SAMPLE_BB
    xact = xact_ref[...]
    bx = xact[:, 0:D_B]
    dts = dts_ref[...]
    dec_t = _pad_rows_t(jnp.exp(dts * (-jnp.exp(alog_ref[...]))))
    dts_t = _pad_rows_t(dts)
    hpg = H_B // G_B
    for j in range(H_B // 2):
        g = (2 * j) // hpg
        xt = _pad_rows_t(bx[:, j * LANE:(j + 1) * LANE])
        dtp = jnp.concatenate([jnp.broadcast_to(dts_t[2 * j:2 * j + 1, :], (HD_B, LANE)),
                               jnp.broadcast_to(dts_t[2 * j + 1:2 * j + 2, :], (HD_B, LANE))], axis=0)
        xdt = xt * dtp
        for i in range(bb):
            brow = jnp.broadcast_to(xact[i:i + 1, D_B + g * N_B:D_B + (g + 1) * N_B], (2 * HD_B, N_B))
            crow = jnp.broadcast_to(
                xact[i:i + 1, D_B + G_B * N_B + g * N_B:D_B + G_B * N_B + (g + 1) * N_B], (2 * SUBLANE, N_B))
            upd = jnp.broadcast_to(xdt[:, i:i + 1], (2 * HD_B, N_B)) * brow
            news = []
            for hh in range(2):
                hd = 2 * j + hh
                dec = jnp.broadcast_to(dec_t[hd:hd + 1, i:i + 1], (HD_B, N_B))
                sn = dec * s_ref[i, hd] + upd[hh * HD_B:(hh + 1) * HD_B, :]
                so_ref[i, hd] = sn
                news.append(sn)
            spair = jnp.concatenate(news, axis=0).astype(BF16)
            yrow = _dot_tr(crow.astype(BF16), spair)
            y_ref[i:i + 1, j * LANE:(j + 1) * LANE] = yrow[0:1, :]
    bout_ref[...] = _group_norm_gate(y_ref[...], bx, z_ref[...], dexp_ref, bnw_ref)


def _ab_sample(u, dtr, s_ah, s_ac, s_bs, s_bc, w):
    bsz = u.shape[0]
    full = lambda shape: pl.BlockSpec(shape, lambda i: tuple(0 for _ in shape))
    cw = CONV_W - 1
    aout, ah, ac, bc, xact, dts = pl.pallas_call(
        _ab_sample_rows_kernel, grid=(1,),
        in_specs=[pl.BlockSpec((bsz, D_A), lambda i: (0, 0)), pl.BlockSpec((bsz, D_A), lambda i: (0, 1)),
                  pl.BlockSpec((bsz, CONV_DIM_B), lambda i: (0, 2)), full((bsz, LANE)),
                  full((bsz, D_A)), full((bsz, cw * D_A)), full((bsz, cw * CONV_DIM_B)),
                  full((CONV_W, D_A)), full((1, D_A)),
                  full((A_BLOCKS, A_BLK, A_BLK)), full((1, D_A)),
                  full((A_BLOCKS, A_BLK, A_BLK)), full((1, D_A)), full((1, D_A)),
                  full((CONV_W, CONV_DIM_B)), full((1, CONV_DIM_B)), full((1, LANE))],
        out_specs=[full((bsz, D_A)), full((bsz, D_A)), full((bsz, cw * D_A)), full((bsz, cw * CONV_DIM_B)),
                   full((bsz, CONV_DIM_B)), full((bsz, LANE))],
        out_shape=[jax.ShapeDtypeStruct((bsz, D_A), F32), jax.ShapeDtypeStruct((bsz, D_A), F32),
                   jax.ShapeDtypeStruct((bsz, cw * D_A), F32), jax.ShapeDtypeStruct((bsz, cw * CONV_DIM_B), F32),
                   jax.ShapeDtypeStruct((bsz, CONV_DIM_B), F32), jax.ShapeDtypeStruct((bsz, LANE), F32)],
        compiler_params=_params("arbitrary"),
        name="ab_sample_rows",
    )(u, u, u, dtr, s_ah, s_ac.reshape(bsz, cw * D_A), s_bc.reshape(bsz, cw * CONV_DIM_B),
      w["acw"], w["acb"], w["wr"], w["br"], w["wi"], w["bi"], w["lam"], w["bcw"], w["bcb"], w["dtb"])

    bb = SAMPLE_BB
    assert bsz % bb == 0
    cvec = lambda i: (0, 0)
    bs, bout = pl.pallas_call(
        _ab_sample_state_kernel, grid=(bsz // bb,),
        in_specs=[pl.BlockSpec((bb, H_B, HD_B, N_B), lambda i: (i, 0, 0, 0)),
                  pl.BlockSpec((bb, CONV_DIM_B), lambda i: (i, 0)),
                  pl.BlockSpec((bb, LANE), lambda i: (i, 0)),
                  pl.BlockSpec((bb, D_B), lambda i: (i, 2)),
                  pl.BlockSpec((1, LANE), cvec), pl.BlockSpec((1, D_B), cvec), pl.BlockSpec((1, D_B), cvec)],
        out_specs=[pl.BlockSpec((bb, H_B, HD_B, N_B), lambda i: (i, 0, 0, 0)),
                   pl.BlockSpec((bb, D_B), lambda i: (i, 0))],
        out_shape=[jax.ShapeDtypeStruct(s_bs.shape, F32), jax.ShapeDtypeStruct((bsz, D_B), F32)],
        scratch_shapes=[pltpu.VMEM((bb, D_B), F32)],
        compiler_params=_params("parallel"),
        name="ab_sample_state",
    )(s_bs, xact, dts, u, w["alog"], w["dexp"], w["bnw"])
    mix = jnp.concatenate([aout, bout], axis=-1)
    return mix, ah, ac.reshape(bsz, cw, D_A), bs, bc.reshape(bsz, cw, CONV_DIM_B)


def _hg_lower_bound(clb, layer):
    mx = jnp.max(clb, axis=0, keepdims=True)
    ex = jnp.exp(clb - mx)
    return jnp.sum(ex[1:layer + 1], axis=0, keepdims=True) / jnp.sum(ex, axis=0, keepdims=True)


def _hg_gates(fx_half, lb):
    f = 0.5 * (1.0 + lb) + (0.5 * (1.0 - lb)) * jnp.tanh(fx_half)
    return f, 1.0 - f


def _hg_out(o, gate_half, cnw):
    return o * lax.rsqrt(jnp.mean(o * o, axis=-1, keepdims=True) + EPS) * cnw * (gate_half + gate_half * jnp.tanh(gate_half))


def _hg_gamma():
    import numpy as np
    q = HG_CHUNK
    t = np.arange(q)[:, None]
    tau = np.arange(q)[None, :]
    mats = [(tau <= t)]
    for l in range(1, HG_MXU_LEVELS):
        w = 1 << l
        ref = (t // (2 * w)) * (2 * w) + w - 1
        upper = (t % (2 * w)) >= w
        mats.append(np.where(upper, (tau > ref) & (tau <= t), (tau > t) & (tau <= ref)))
    gam = np.concatenate(mats, axis=0).astype(np.float32)
    return jnp.asarray(np.concatenate([gam, gam], axis=1), dtype=BF16)


def _hg_level_table():
    import numpy as np
    q = HG_CHUNK
    t = np.arange(q)[:, None]
    s = np.arange(q)[None, :]
    x = t ^ s
    lvl = np.floor(np.log2(np.maximum(x, 1))).astype(np.int32)
    return jnp.asarray(np.where(t > s, lvl, -1).astype(np.int32))


def _c_prompt_kernel(q_ref, f_ref, v_ref, g_ref, clb_ref, cnw_ref, gam_ref, lvl_ref,
                     og_ref, cs_ref, st_ref, *, layer):
    c = pl.program_id(1)
    last = pl.num_programs(1) - 1
    qc = HG_CHUNK

    @pl.when(c == 0)
    def _():
        st_ref[...] = jnp.zeros_like(st_ref)

    gam = gam_ref[...]
    ntile = qc // SUBLANE
    sub = lax.broadcasted_iota(jnp.int32, (SUBLANE, DK_C), 0)
    sub_levels = HG_MXU_LEVELS
    sub_upper = [(sub & (1 << l)) != 0 for l in range(sub_levels)]

    def tiles(x):
        return [x[i * SUBLANE:(i + 1) * SUBLANE, :] for i in range(ntile)]

    def gate_split(hd, rows):
        f, kk = _hg_gates(f_ref[hd, rows, :].astype(F32), _hg_lower_bound(clb_ref[hd], layer))
        g = jnp.log(jnp.maximum(f, HG_F_MIN)) * LOG2_E
        g1 = g.astype(BF16)
        g2 = (g - g1.astype(F32)).astype(BF16)
        return (f, kk), jnp.concatenate([g1, g2], axis=0)

    def scores(hd, rows, fk, sums):
        f, kk = fk
        qh = q_ref[hd, rows, :].astype(F32) * (DK_C ** -0.5)
        bcum = sums[0:qc]
        st = st_ref[hd]
        o = _dot_tr((qh * jnp.exp2(bcum)).astype(BF16), st.astype(BF16))
        qt, kt, ft, bt = tiles(qh), tiles(kk), tiles(f), tiles(bcum)
        prods = []
        for l in range(HG_LEVELS):
            if l == 0:
                xt = [jnp.where(sub_upper[0], qt[i] * ft[i], kt[i]) for i in range(ntile)]
            elif l < HG_MXU_LEVELS:
                dec = tiles(jnp.exp2(sums[l * qc:(l + 1) * qc]))
                xt = [jnp.where(sub_upper[l], qt[i], kt[i]) * dec[i] for i in range(ntile)]
            else:
                wt = 1 << (l - HG_MXU_LEVELS)
                xt = []
                for blk in range(0, ntile, 2 * wt):
                    ref = (blk + wt) * SUBLANE - 1
                    bref = jnp.broadcast_to(bcum[ref:ref + 1, :], (SUBLANE, DK_C))
                    xt += [kt[i] * jnp.exp2(bref - bt[i]) for i in range(blk, blk + wt)]
                    xt += [qt[i] * jnp.exp2(bt[i] - bref) for i in range(blk + wt, blk + 2 * wt)]
            x = jnp.concatenate(xt, axis=0).astype(BF16)
            half = (1 << l) // BF16_ROWS
            if half == 0:
                p = tiles(_dot_tr(x, x))
                prods.append({i: p[i] for i in range(ntile)})
            else:
                ups = [r for r in range(qc // BF16_ROWS) if (r // half) & 1]
                pu = _dot_tr(jnp.concatenate([x[r * BF16_ROWS:(r + 1) * BF16_ROWS, :] for r in ups], axis=0), x)
                tpr = BF16_ROWS // SUBLANE
                prods.append({r * tpr + k: pu[(n * tpr + k) * SUBLANE:(n * tpr + k + 1) * SUBLANE, :]
                              for n, r in enumerate(ups) for k in range(tpr)})
        return qh, st, o, prods

    def level_masks():
        masks = {}
        for i in range(ntile):
            lv = lvl_ref[i * SUBLANE:(i + 1) * SUBLANE, :]
            for l in range(HG_LEVELS):
                if l < sub_levels or (i >> (l - sub_levels)) & 1:
                    masks[i, l] = lv == l
        return masks

    def combine(hd, rows, kk, bcum, qh, st, o, prods, masks):
        arows = []
        for i in range(ntile):
            a = jnp.zeros((SUBLANE, qc), F32)
            for l in range(HG_LEVELS):
                if (i, l) in masks:
                    a = jnp.where(masks[i, l], prods[l][i], a)
            arows.append(a)
        amat = jnp.concatenate(arows, axis=0)
        vb = v_ref[hd, rows, :].astype(BF16)
        o = o + _dot(amat.astype(BF16), vb) + jnp.sum(qh * kk, axis=-1, keepdims=True) * vb.astype(F32)
        blast = bcum[qc - 1:qc, :]
        kdec = (kk * jnp.exp2(blast - bcum)).astype(BF16)
        st_ref[hd] = st * jnp.exp2(blast) + _dot_tl(vb, kdec)
        return o

    nchunk = HG_BLOCK // qc

    def body(idx, carry):
        hg = idx // nchunk
        rows = pl.ds(pl.multiple_of((idx % nchunk) * qc, qc), qc)
        heads = [hg * HG_UNROLL + k for k in range(HG_UNROLL)]
        gs = [gate_split(hd, rows) for hd in heads]
        sums = _dot(gam, jnp.concatenate([s for _, s in gs], axis=1))
        sums = [sums[:, k * DK_C:(k + 1) * DK_C] for k in range(HG_UNROLL)]
        sc = [scores(hd, rows, gs[k][0], sums[k]) for k, hd in enumerate(heads)]
        masks = level_masks()
        outs = [combine(hd, rows, gs[k][0][1], sums[k][0:qc], *sc[k], masks) for k, hd in enumerate(heads)]
        for k, hd in enumerate(heads):
            og_ref[hd, rows, :] = _hg_out(outs[k], g_ref[hd, rows, :].astype(F32), cnw_ref[hd]).astype(BF16)
        return carry

    lax.fori_loop(0, (H_C // HG_UNROLL) * nchunk, body, 0)

    @pl.when(c == last)
    def _():
        for hd in range(H_C):
            cs_ref[0, hd] = st_ref[hd].T


def _c_prompt(u, bsz, seq, w, layer):
    tb = HG_BLOCK
    assert seq % tb == 0 and tb % HG_CHUNK == 0 and (1 << HG_LEVELS) == HG_CHUNK
    nc = seq // tb
    m = bsz * seq
    depth = w["clb"].shape[1]

    def part(k):
        return pl.BlockSpec((H_C, tb, LANE), lambda b, c: (k, b * nc + c, 0))

    c2 = lambda b, c: (0, 0)
    c3 = lambda b, c: (0, 0, 0)
    og, cs = pl.pallas_call(
        functools.partial(_c_prompt_kernel, layer=layer), grid=(bsz, nc),
        in_specs=[part(0), part(1), part(2), part(3),
                  pl.BlockSpec((H_C, depth, DK_C), c3), pl.BlockSpec((H_C, 1, DV_C), c3),
                  pl.BlockSpec(w["gam"].shape, c2), pl.BlockSpec(w["lvl"].shape, c2)],
        out_specs=[pl.BlockSpec((H_C, tb, LANE), lambda b, c: (0, b * nc + c, 0)),
                   pl.BlockSpec((1, H_C, DK_C, DV_C), lambda b, c: (b, 0, 0, 0))],
        out_shape=[jax.ShapeDtypeStruct((H_C, m, DV_C), BF16),
                   jax.ShapeDtypeStruct((bsz, H_C, DK_C, DV_C), F32)],
        scratch_shapes=[pltpu.VMEM((H_C, DV_C, DK_C), F32)],
        compiler_params=_params("parallel", "arbitrary"),
        name="c_prompt",
    )(u, u, u, u, w["clb"], w["cnw"], w["gam"], w["lvl"])
    return og, cs


def _c_sample_kernel(q_ref, f_ref, v_ref, g_ref, s_ref, clb_ref, cnw_ref, og_ref, so_ref, *, layer):
    bb = SAMPLE_BB
    lane = lax.broadcasted_iota(jnp.int32, (DK_C, LANE), 1)
    first_rows = lax.broadcasted_iota(jnp.int32, (LANE, DV_C), 0) < bb

    def head(hd, carry):
        lb = _hg_lower_bound(clb_ref[hd], layer)
        f, kk = _hg_gates(f_ref[hd], lb)
        f_t = _pad_rows_t(f)
        k_t = _pad_rows_t(kk)
        qs = q_ref[hd] * (DK_C ** -0.5)
        v = v_ref[hd]
        vpad = jnp.where(first_rows, jnp.tile(v, (LANE // bb, 1)), 0.0).astype(BF16)
        orows = []
        for i in range(bb):
            fcol = jnp.broadcast_to(f_t[:, i:i + 1], (DK_C, DV_C))
            kv = _dot(jnp.where(lane == i, k_t, 0.0).astype(BF16), vpad)
            sn = fcol * s_ref[i, hd] + kv
            so_ref[i, hd] = sn
            qrow = jnp.broadcast_to(qs[i:i + 1, :], (2 * SUBLANE, DK_C)).astype(BF16)
            orows.append(_dot(qrow, sn.astype(BF16))[0:1, :])
        o = jnp.concatenate(orows, axis=0)
        og_ref[hd] = _hg_out(o, g_ref[hd], cnw_ref[hd])
        return carry

    lax.fori_loop(0, H_C, head, 0, unroll=SAMPLE_UNROLL)


def _c_sample(u, s_c, w, layer):
    bsz = s_c.shape[0]
    bb = SAMPLE_BB
    assert bsz % bb == 0
    depth = w["clb"].shape[1]

    def part(k):
        return pl.BlockSpec((H_C, bb, LANE), lambda i: (k, i, 0))

    c3 = lambda i: (0, 0, 0)
    og, so = pl.pallas_call(
        functools.partial(_c_sample_kernel, layer=layer), grid=(bsz // bb,),
        in_specs=[part(0), part(1), part(2), part(3),
                  pl.BlockSpec((bb, H_C, DK_C, DV_C), lambda i: (i, 0, 0, 0)),
                  pl.BlockSpec((H_C, depth, DK_C), c3), pl.BlockSpec((H_C, 1, DV_C), c3)],
        out_specs=[pl.BlockSpec((H_C, bb, LANE), lambda i: (0, i, 0)),
                   pl.BlockSpec((bb, H_C, DK_C, DV_C), lambda i: (i, 0, 0, 0))],
        out_shape=[jax.ShapeDtypeStruct((H_C, bsz, DV_C), F32), jax.ShapeDtypeStruct(s_c.shape, F32)],
        compiler_params=_params("parallel"),
        name="c_sample",
    )(u, u, u, u, s_c, w["clb"], w["cnw"])
    return og, so


def _row(v, width=None):
    v = v.astype(F32).reshape(1, -1)
    if width is not None and v.shape[1] < width:
        v = jnp.pad(v, ((0, 0), (0, width - v.shape[1])))
    return v


def kernel(x_prompt, x_sample, p_prompt, p_sample, state_a_h, state_a_conv, state_b_ssm, state_b_conv, state_c,
           norm_w, norm_f, ab_w_in, a_conv_w, a_conv_b, a_w_r, a_b_r, a_w_i, a_b_i, a_lam, b_conv_w, b_conv_b,
           b_dt_bias, b_a_log, b_d, b_norm_w, ab_w_out, c_w_in, c_lb, c_norm_w, c_w_out, ple_proj, ple_gate):
    depth = norm_w.shape[0]
    bp, seq, _ = x_prompt.shape
    bs = x_sample.shape[0]
    hp = x_prompt.reshape(bp * seq, D_MODEL)
    hs = x_sample.reshape(bs, D_MODEL)
    pp = p_prompt.reshape(depth, bp * seq, D_PLE)
    ps = p_sample.reshape(depth, bs, D_PLE)
    gam, lvl = _hg_gamma(), _hg_level_table()
    clb = c_lb.astype(F32).reshape(depth, H_C, DK_C).transpose(1, 0, 2)

    ah_p, ac_p, bs_p, bc_p, c_p = [], [], [], [], []
    ah_s, ac_s, bs_s, bc_s, c_s = [], [], [], [], []
    for i in range(depth):
        j = i // 2
        final = i == depth - 1
        wg = (0.5 * ple_gate[i]).astype(BF16)
        wp = (0.5 * ple_proj[i]).astype(BF16)
        if i % 2 == 0:
            col_scale = jnp.concatenate([jnp.ones((D_A,), F32), jnp.full((D_A + D_B,), 0.5, F32),
                                         jnp.ones((CONV_DIM_B,), F32)])
            w_ab_t = ab_w_in[j].T
            w_ab = (w_ab_t[:AB_MAIN] * col_scale[:, None]).astype(BF16)
            w_dt = jnp.pad(w_ab_t[AB_MAIN:], ((0, LANE - H_B), (0, 0))).astype(BF16)
            wo = ab_w_out[j].astype(BF16)
            w = dict(acw=a_conv_w[j].astype(F32), acb=_row(a_conv_b[j]),
                     wr=(0.5 * a_w_r[j]).astype(BF16), br=0.5 * _row(a_b_r[j]),
                     wi=a_w_i[j].astype(BF16), bi=_row(a_b_i[j]),
                     lam=_row(a_lam[j]), bcw=0.5 * b_conv_w[j].astype(F32), bcb=0.5 * _row(b_conv_b[j]),
                     dtb=_row(b_dt_bias[j], LANE), alog=_row(b_a_log[j], LANE),
                     dexp=_row(jnp.repeat(b_d[j], HD_B)), bnw=_row(b_norm_w[j]))
            u, dtr = _in_proj(hp, norm_w[i], w_ab, w_dt, w_t=True)
            hp, s1, s2, s3, s4 = _ab_prompt(u, dtr, hp, pp, i, wo, wg, wp, norm_f, bp, seq, w, final=final)
            ah_p.append(s1); ac_p.append(s2); bs_p.append(s3); bc_p.append(s4)
            u, dtr = _in_proj(hs, norm_w[i], w_ab, w_dt, w_t=True)
            mix, s1, s2, s3, s4 = _ab_sample(u, dtr, state_a_h[j], state_a_conv[j], state_b_ssm[j],
                                             state_b_conv[j], w)
            ah_s.append(s1); ac_s.append(s2); bs_s.append(s3); bc_s.append(s4)
            hs = _out_proj(mix, hs, ps, i, wo, wg, wp, norm_f, head_major=False, final=final)
        else:
            col_scale = jnp.concatenate([jnp.ones((HK_C,), F32), jnp.full((HK_C,), 0.5, F32),
                                         jnp.ones((D_C,), F32), jnp.full((D_C,), 0.5, F32)])
            w_in = (c_w_in[j] * col_scale).astype(BF16)
            wo = c_w_out[j].astype(BF16)
            w = dict(clb=clb, cnw=c_norm_w[j].astype(F32).reshape(H_C, 1, DV_C), gam=gam, lvl=lvl)
            u = _in_proj(hp, norm_w[i], w_in, head_major=True, head_major_dtype=BF16)
            og, s1 = _c_prompt(u, bp, seq, w, i)
            c_p.append(s1)
            hp = _out_proj(og, hp, pp, i, wo, wg, wp, norm_f, head_major=True, final=final)
            u = _in_proj(hs, norm_w[i], w_in, head_major=True)
            og, s1 = _c_sample(u, state_c[j], w, i)
            c_s.append(s1)
            hs = _out_proj(og, hs, ps, i, wo, wg, wp, norm_f, head_major=True, final=final)
    return (hp.reshape(bp, seq, D_MODEL), hs.reshape(bs, 1, D_MODEL),
            jnp.stack(ah_p), jnp.stack(ac_p), jnp.stack(bs_p), jnp.stack(bc_p), jnp.stack(c_p),
            jnp.stack(ah_s), jnp.stack(ac_s), jnp.stack(bs_s), jnp.stack(bc_s), jnp.stack(c_s))
```

```python
import functools

import jax
import jax.numpy as jnp
from jax import lax
from jax.experimental import pallas as pl
from jax.experimental.pallas import tpu as pltpu

F32 = jnp.float32
BF16 = jnp.bfloat16

D_MODEL = 1024
D_PLE = 256
EPS = 1e-6
CONV_W = 4
D_A = D_MODEL
A_BLOCKS = 8
A_BLK = D_A // A_BLOCKS
LRU_C = 8.0
D_B = D_MODEL
HD_B = 64
H_B = D_B // HD_B
N_B = 128
G_B = 2
CONV_DIM_B = D_B + 2 * G_B * N_B
D_C = 2 * D_MODEL
H_C = 16
DK_C = 128
DV_C = D_C // H_C
HK_C = H_C * DK_C
AB_MAIN = 2 * D_A + D_B + CONV_DIM_B
IN_C = 2 * HK_C + 2 * D_C

LANE = 128
SUBLANE = 8
BF16_ROWS = 16
LOG2_E = 1.4426950408889634
VMEM_LIMIT = 56 * 1024 * 1024

PROJ_TM = 2048
PROJ_VMEM_BUDGET = 46 * 1024 * 1024
OUT_PROJ_TM = 1024
SSD_CHUNK = 128
AB_CHUNKS = 2
HG_CHUNK = 64
HG_LEVELS = 6
HG_MXU_LEVELS = 3
HG_F_MIN = 1e-30
HG_BLOCK = 512
HG_UNROLL = 16
SAMPLE_BB = 8
SAMPLE_UNROLL = 4
FUSED_HEAD_SPLIT = 2

_DN_TR = (((1,), (1,)), ((), ()))
_DN_TL = (((0,), (0,)), ((), ()))


def _dot(a, b):
    return jnp.dot(a, b, preferred_element_type=F32)


def _dot_tr(a, b):
    return lax.dot_general(a, b, _DN_TR, preferred_element_type=F32)


def _dot_tl(a, b):
    return lax.dot_general(a, b, _DN_TL, preferred_element_type=F32)


def _silu_half(x_half):
    return x_half + x_half * jnp.tanh(x_half)


def _rmsnorm(x, w):
    return x * lax.rsqrt(jnp.mean(x * x, axis=-1, keepdims=True) + EPS) * w


def _params(*sem):
    return pltpu.CompilerParams(dimension_semantics=sem, vmem_limit_bytes=VMEM_LIMIT)


def _in_proj_kernel(x_ref, nw_ref, w_ref, *rest, has_extra, head_major, w_t):
    if has_extra:
        wx_ref, o_ref, ox_ref, xn_ref = rest
    else:
        o_ref, xn_ref = rest
    dot = _dot_tr if w_t else _dot

    @pl.when(pl.program_id(1) == 0)
    def _():
        xn_ref[...] = _rmsnorm(x_ref[...], nw_ref[...]).astype(BF16)
        if has_extra:
            ox_ref[...] = dot(xn_ref[...], wx_ref[...])

    acc = dot(xn_ref[...], w_ref[...])
    if head_major:
        for k in range(acc.shape[1] // LANE):
            o_ref[k] = acc[:, k * LANE:(k + 1) * LANE].astype(o_ref.dtype)
    else:
        o_ref[...] = acc


def _in_proj(x, nw, w, w_extra=None, *, head_major=False, w_t=False, head_major_dtype=F32):
    m, k = x.shape
    extra_cols = w_extra is not None
    n = w.shape[0 if w_t else 1]
    tm = min(m, PROJ_TM)
    out_bytes = jnp.dtype(head_major_dtype if head_major else F32).itemsize

    def vmem(c):
        extra = 2 * k * LANE * 2 + 2 * tm * LANE * 4 if extra_cols else 0
        return 2 * tm * k * 4 + tm * k * 2 + 2 * k * c * 2 + 2 * tm * c * out_bytes + extra

    tn = next(c for c in range(n, 0, -LANE) if n % c == 0 and vmem(c) <= PROJ_VMEM_BUDGET)
    assert m % tm == 0 and n % tn == 0 and tn % LANE == 0
    grid = (m // tm, n // tn)
    wspec = (lambda cols, idx: pl.BlockSpec((cols, k), lambda i, j: (idx(j), 0))) if w_t else \
            (lambda cols, idx: pl.BlockSpec((k, cols), lambda i, j: (0, idx(j))))
    in_specs = [pl.BlockSpec((tm, k), lambda i, j: (i, 0)),
                pl.BlockSpec((1, k), lambda i, j: (0, 0)),
                wspec(tn, lambda j: j)]
    args = [x, nw.reshape(1, k), w]
    if head_major:
        out_shape = [jax.ShapeDtypeStruct((n // LANE, m, LANE), head_major_dtype)]
        out_specs = [pl.BlockSpec((tn // LANE, tm, LANE), lambda i, j: (j, i, 0))]
    else:
        out_shape = [jax.ShapeDtypeStruct((m, n), F32)]
        out_specs = [pl.BlockSpec((tm, tn), lambda i, j: (i, j))]
    if extra_cols:
        in_specs.append(wspec(LANE, lambda j: 0))
        args.append(w_extra)
        out_shape.append(jax.ShapeDtypeStruct((m, LANE), F32))
        out_specs.append(pl.BlockSpec((tm, LANE), lambda i, j: (i, 0)))
    outs = pl.pallas_call(
        functools.partial(_in_proj_kernel, has_extra=extra_cols, head_major=head_major, w_t=w_t),
        grid=grid, in_specs=in_specs, out_specs=out_specs, out_shape=out_shape,
        scratch_shapes=[pltpu.VMEM((tm, k), BF16)],
        compiler_params=_params("parallel", "arbitrary"),
        name="in_proj",
    )(*args)
    return outs if extra_cols else outs[0]


def _residual_ple(h, p_ref, wg_ref, wp_ref, nf_ref, final):
    gate_t = jnp.tanh(_dot(h.astype(BF16), wg_ref[...]))
    pe_half = _dot(p_ref[...].astype(BF16), wp_ref[...])
    h = h + pe_half + pe_half * gate_t
    return _rmsnorm(h, nf_ref[...]) if final else h


def _out_proj_kernel(mix_ref, h_ref, p_ref, wo_ref, wg_ref, wp_ref, nf_ref, o_ref, *, head_major, final):
    if head_major:
        mix = jnp.concatenate([mix_ref[k] for k in range(mix_ref.shape[0])], axis=-1)
    else:
        mix = mix_ref[...]
    h = h_ref[...] + _dot(mix.astype(BF16), wo_ref[...])
    o_ref[...] = _residual_ple(h, p_ref, wg_ref, wp_ref, nf_ref, final)


def _out_proj(mix, h, p, layer, wo, wg, wp, nf, *, head_major, final):
    m, d = h.shape
    tm = min(m, OUT_PROJ_TM)
    assert m % tm == 0
    if head_major:
        mix_spec = pl.BlockSpec((mix.shape[0], tm, LANE), lambda i: (0, i, 0))
    else:
        mix_spec = pl.BlockSpec((tm, mix.shape[1]), lambda i: (i, 0))
    const = lambda i: (0, 0)
    return pl.pallas_call(
        functools.partial(_out_proj_kernel, head_major=head_major, final=final),
        grid=(m // tm,),
        in_specs=[mix_spec,
                  pl.BlockSpec((tm, d), lambda i: (i, 0)),
                  pl.BlockSpec((None, tm, p.shape[2]), lambda i: (layer, i, 0)),
                  pl.BlockSpec(wo.shape, const), pl.BlockSpec(wg.shape, const),
                  pl.BlockSpec(wp.shape, const), pl.BlockSpec((1, d), const)],
        out_specs=pl.BlockSpec((tm, d), lambda i: (i, 0)),
        out_shape=jax.ShapeDtypeStruct((m, d), F32),
        compiler_params=_params("parallel"),
        name="out_proj",
    )(mix, h, p, wo, wg, wp, nf.reshape(1, d))


def _lru_gates(xc, wr_ref, br_ref, wi_ref, bi_ref, lam_ref):
    xcb = xc.astype(BF16)
    r_parts, i_parts = [], []
    for k in range(A_BLOCKS):
        xk = xcb[:, k * A_BLK:(k + 1) * A_BLK]
        r_parts.append(_dot(xk, wr_ref[k]))
        i_parts.append(_dot(xk, wi_ref[k]))
    tr = jnp.tanh(jnp.concatenate(r_parts, axis=-1) + br_ref[...])
    gi = jax.nn.sigmoid(jnp.concatenate(i_parts, axis=-1) + bi_ref[...])
    half_rate = (-0.5 * LRU_C) * jax.nn.softplus(-lam_ref[...])
    log_a = half_rate + half_rate * tr
    a = jnp.exp(log_a)
    v = jnp.tanh(-log_a) * (a * a + 1.0)
    u = jnp.where(v > 0.0, v * lax.rsqrt(v), 0.0) * (gi * xc)
    return a, u


def _group_norm_gate(y, bx, z_half, dexp_ref, bnw_ref):
    y = (y + dexp_ref[...] * bx) * _silu_half(z_half)
    gw = D_B // G_B
    parts = []
    for g in range(G_B):
        yg = y[:, g * gw:(g + 1) * gw]
        parts.append(yg * lax.rsqrt(jnp.mean(yg * yg, axis=-1, keepdims=True) + EPS))
    return jnp.concatenate(parts, axis=-1) * bnw_ref[...]


def _ab_prompt_kernel(ax_ref, ag_ref, z_ref, xbc_ref, dt_ref,
                      acw_ref, acb_ref, wr_ref, br_ref, wi_ref, bi_ref, lam_ref,
                      bcw_ref, bcb_ref, dtb_ref, alog_ref, dexp_ref, bnw_ref,
                      hin_ref, p_ref, wo_ref, wg_ref, wp_ref, nf_ref,
                      hout_ref, *state_refs, final):
    for k in range(AB_CHUNKS):
        r = pl.ds(k * SSD_CHUNK, SSD_CHUNK)
        _ab_prompt_chunk(ax_ref.at[r], ag_ref.at[r], z_ref.at[r], xbc_ref.at[r], dt_ref.at[r],
                         acw_ref, acb_ref, wr_ref, br_ref, wi_ref, bi_ref, lam_ref,
                         bcw_ref, bcb_ref, dtb_ref, alog_ref, dexp_ref, bnw_ref,
                         hin_ref.at[r], p_ref.at[r], wo_ref, wg_ref, wp_ref, nf_ref,
                         hout_ref.at[r], *state_refs, final=final, first=k == 0, last=k == AB_CHUNKS - 1)


def _ab_prompt_chunk(ax_ref, ag_ref, z_ref, xbc_ref, dt_ref,
                     acw_ref, acb_ref, wr_ref, br_ref, wi_ref, bi_ref, lam_ref,
                     bcw_ref, bcb_ref, dtb_ref, alog_ref, dexp_ref, bnw_ref,
                     hin_ref, p_ref, wo_ref, wg_ref, wp_ref, nf_ref,
                     hout_ref, ah_ref, ac_ref, bs_ref, bc_ref,
                     xpa_ref, xpb_ref, h_ref, s_ref, *, final, first, last):
    c = pl.program_id(1)
    t = SSD_CHUNK
    ntile = t // SUBLANE

    if first:
        @pl.when(c == 0)
        def _():
            xpa_ref[...] = jnp.zeros_like(xpa_ref)
            xpb_ref[...] = jnp.zeros_like(xpb_ref)
            h_ref[...] = jnp.zeros_like(h_ref)
            s_ref[...] = jnp.zeros_like(s_ref)

    def tiles(x):
        return [x[i * SUBLANE:(i + 1) * SUBLANE, :] for i in range(ntile)]

    def conv(x, tail_ref, w_ref, b_ref):
        sub = lax.broadcasted_iota(jnp.int32, (SUBLANE, x.shape[1]), 0)
        xt = [tail_ref[...]] + tiles(x)
        taps = [jnp.broadcast_to(w_ref[k:k + 1, :], (SUBLANE, x.shape[1])) for k in range(CONV_W)]
        bias = jnp.broadcast_to(b_ref[...], (SUBLANE, x.shape[1]))
        acc = [bias + taps[CONV_W - 1] * xt[i + 1] for i in range(ntile)]
        for s in range(1, CONV_W):
            wk = taps[CONV_W - 1 - s]
            for i in range(ntile):
                merged = jnp.where(sub >= SUBLANE - s, xt[i], xt[i + 1])
                acc[i] = acc[i] + wk * pltpu.roll(merged, s, 0)
        tail_ref[...] = xt[ntile]
        return jnp.concatenate(acc, axis=0)

    ax = ax_ref[...]
    xc = conv(ax, xpa_ref, acw_ref, acb_ref)
    a, u = _lru_gates(xc, wr_ref, br_ref, wi_ref, bi_ref, lam_ref)
    sub = lax.broadcasted_iota(jnp.int32, (SUBLANE, D_A), 0)
    at, ut = tiles(a), tiles(u)
    step = 1
    while step < SUBLANE:
        m = sub >= step
        for i in range(ntile):
            ut[i] = jnp.where(m, at[i] * pltpu.roll(ut[i], step, 0) + ut[i], ut[i])
            at[i] = jnp.where(m, at[i] * pltpu.roll(at[i], step, 0), at[i])
        step *= 2
    carry = h_ref[0:1, :]
    hs = []
    for i in range(ntile):
        hs.append(ut[i] + at[i] * carry)
        carry = hs[i][SUBLANE - 1:SUBLANE, :]
    h = jnp.concatenate(hs, axis=0)
    h_ref[0:1, :] = carry
    a_out = (h * _silu_half(ag_ref[...])).astype(BF16)

    xb = xbc_ref[...]
    xbc = _silu_half(conv(xb, xpb_ref, bcw_ref, bcb_ref))
    bx = xbc[:, 0:D_B]
    bxb = bx.astype(BF16)
    dt = jax.nn.softplus(dt_ref[...] + dtb_ref[...])
    adt = dt * (-LOG2_E * jnp.exp(alog_ref[...]))
    ti = lax.broadcasted_iota(jnp.int32, (t, t), 0)
    si = lax.broadcasted_iota(jnp.int32, (t, t), 1)
    causal = ti >= si
    acs = jnp.dot(causal.astype(F32), adt, preferred_element_type=F32,
                  precision=lax.Precision.HIGHEST)
    a_last = acs[t - 1:t, :]
    wq = jnp.exp2(a_last - acs) * dt
    eacs = jnp.exp2(acs)
    ealast = jnp.exp2(a_last)
    acs_t = acs.T
    dt_t = dt.T
    lane = lax.broadcasted_iota(jnp.int32, (t, LANE), 1)
    rowi = lax.broadcasted_iota(jnp.int32, (LANE, N_B), 0)
    hpg = H_B // G_B
    ys = []
    cb = None
    for j in range(H_B // 2):
        g = (2 * j) // hpg
        bg = xbc[:, D_B + g * N_B:D_B + (g + 1) * N_B]
        cg = xbc[:, D_B + G_B * N_B + g * N_B:D_B + G_B * N_B + (g + 1) * N_B]
        if (2 * j) % hpg == 0:
            cb = _dot_tr(cg.astype(BF16), bg.astype(BF16))
        xpair = bxb[:, j * LANE:(j + 1) * LANE]
        sp = s_ref[j]
        spb = sp.astype(BF16)
        y_h, up_h = [], []
        for hh in range(2):
            hd = 2 * j + hh
            seg = jnp.broadcast_to(acs[:, hd:hd + 1], (t, t)) - jnp.broadcast_to(acs_t[hd:hd + 1, :], (t, t))
            lmat = jnp.exp2(jnp.where(causal, seg, -1e30))
            mmat = (cb * lmat * jnp.broadcast_to(dt_t[hd:hd + 1, :], (t, t))).astype(BF16)
            ec = (jnp.broadcast_to(eacs[:, hd:hd + 1], (t, N_B)) * cg).astype(BF16)
            y_h.append(_dot(mmat, xpair) + _dot_tr(ec, spb))
            bw = (bg * jnp.broadcast_to(wq[:, hd:hd + 1], (t, N_B))).astype(BF16)
            up_h.append(_dot_tl(xpair, bw))
        ys.append(jnp.where(lane < HD_B, y_h[0], y_h[1]))
        dec = jnp.where(rowi < HD_B,
                        jnp.broadcast_to(ealast[:, 2 * j:2 * j + 1], (LANE, N_B)),
                        jnp.broadcast_to(ealast[:, 2 * j + 1:2 * j + 2], (LANE, N_B)))
        s_ref[j] = dec * sp + jnp.where(rowi < HD_B, up_h[0], up_h[1])
    y = jnp.concatenate(ys, axis=-1)
    b_out = _group_norm_gate(y, bx, z_ref[...], dexp_ref, bnw_ref).astype(BF16)

    hres = hin_ref[...] + _dot(a_out, wo_ref[0:D_A, :]) + _dot(b_out, wo_ref[D_A:D_A + D_B, :])
    hout_ref[...] = _residual_ple(hres, p_ref, wg_ref, wp_ref, nf_ref, final)

    if last:
        @pl.when(c == pl.num_programs(1) - 1)
        def _():
            ah_ref[0] = h[t - 1:t, :]
            ac_ref[0] = ax[t - (CONV_W - 1):t, :]
            bc_ref[0] = xb[t - (CONV_W - 1):t, :]
            for j in range(H_B // 2):
                sj = s_ref[j]
                bs_ref[0, 2 * j] = sj[0:HD_B, :]
                bs_ref[0, 2 * j + 1] = sj[HD_B:2 * HD_B, :]


def _ab_prompt(u, dtr, h, p, layer, wo, wg, wp, nf, bsz, seq, w, *, final):
    t = AB_CHUNKS * SSD_CHUNK
    assert seq % t == 0
    nc = seq // t
    m = bsz * seq
    rows = lambda b, c: b * nc + c
    cvec = lambda b, c: (0, 0)
    c3 = lambda b, c: (0, 0, 0)
    in_specs = [
        pl.BlockSpec((t, D_A), lambda b, c: (rows(b, c), 0)),
        pl.BlockSpec((t, D_A), lambda b, c: (rows(b, c), 1)),
        pl.BlockSpec((t, D_B), lambda b, c: (rows(b, c), 2)),
        pl.BlockSpec((t, CONV_DIM_B), lambda b, c: (rows(b, c), 2)),
        pl.BlockSpec((t, LANE), lambda b, c: (rows(b, c), 0)),
        pl.BlockSpec((CONV_W, D_A), cvec), pl.BlockSpec((1, D_A), cvec),
        pl.BlockSpec((A_BLOCKS, A_BLK, A_BLK), c3), pl.BlockSpec((1, D_A), cvec),
        pl.BlockSpec((A_BLOCKS, A_BLK, A_BLK), c3), pl.BlockSpec((1, D_A), cvec),
        pl.BlockSpec((1, D_A), cvec),
        pl.BlockSpec((CONV_W, CONV_DIM_B), cvec), pl.BlockSpec((1, CONV_DIM_B), cvec),
        pl.BlockSpec((1, LANE), cvec), pl.BlockSpec((1, LANE), cvec),
        pl.BlockSpec((1, D_B), cvec), pl.BlockSpec((1, D_B), cvec),
        pl.BlockSpec((t, D_MODEL), lambda b, c: (rows(b, c), 0)),
        pl.BlockSpec((None, t, p.shape[2]), lambda b, c: (layer, rows(b, c), 0)),
        pl.BlockSpec(wo.shape, cvec), pl.BlockSpec(wg.shape, cvec), pl.BlockSpec(wp.shape, cvec),
        pl.BlockSpec((1, D_MODEL), cvec),
    ]
    out_shape = [
        jax.ShapeDtypeStruct((m, D_MODEL), F32),
        jax.ShapeDtypeStruct((bsz, 1, D_A), F32),
        jax.ShapeDtypeStruct((bsz, CONV_W - 1, D_A), F32),
        jax.ShapeDtypeStruct((bsz, H_B, HD_B, N_B), F32),
        jax.ShapeDtypeStruct((bsz, CONV_W - 1, CONV_DIM_B), F32),
    ]
    out_specs = [
        pl.BlockSpec((t, D_MODEL), lambda b, c: (rows(b, c), 0)),
        pl.BlockSpec((1, 1, D_A), lambda b, c: (b, 0, 0)),
        pl.BlockSpec((1, CONV_W - 1, D_A), lambda b, c: (b, 0, 0)),
        pl.BlockSpec((1, H_B, HD_B, N_B), lambda b, c: (b, 0, 0, 0)),
        pl.BlockSpec((1, CONV_W - 1, CONV_DIM_B), lambda b, c: (b, 0, 0)),
    ]
    hout, ah, ac, bs, bc = pl.pallas_call(
        functools.partial(_ab_prompt_kernel, final=final),
        grid=(bsz, nc), in_specs=in_specs, out_specs=out_specs, out_shape=out_shape,
        scratch_shapes=[pltpu.VMEM((SUBLANE, D_A), F32), pltpu.VMEM((SUBLANE, CONV_DIM_B), F32),
                        pltpu.VMEM((SUBLANE, D_A), F32), pltpu.VMEM((H_B // 2, 2 * HD_B, N_B), F32)],
        compiler_params=_params("parallel", "arbitrary"),
        name="ab_prompt",
    )(u, u, u, u, dtr, w["acw"], w["acb"], w["wr"], w["br"], w["wi"], w["bi"], w["lam"],
      w["bcw"], w["bcb"], w["dtb"], w["alog"], w["dexp"], w["bnw"],
      h, p, wo, wg, wp, nf.reshape(1, D_MODEL))
    return hout, ah.reshape(bsz, D_A), ac, bs, bc


def _ab_sample_rows_kernel(ax_ref, ag_ref, xbc_ref, dt_ref, sah_ref, sac_ref, sbc_ref,
                           acw_ref, acb_ref, wr_ref, br_ref, wi_ref, bi_ref, lam_ref,
                           bcw_ref, bcb_ref, dtb_ref,
                           aout_ref, ah_ref, ac_ref, bc_ref, xact_ref, dts_ref):
    def conv1(x, buf_ref, w_ref, b_ref, nbuf_ref, width):
        y = b_ref[...] + w_ref[CONV_W - 1:CONV_W, :] * x
        for k in range(CONV_W - 1):
            y = y + w_ref[k:k + 1, :] * buf_ref[:, k * width:(k + 1) * width]
        for k in range(CONV_W - 2):
            nbuf_ref[:, k * width:(k + 1) * width] = buf_ref[:, (k + 1) * width:(k + 2) * width]
        nbuf_ref[:, (CONV_W - 2) * width:(CONV_W - 1) * width] = x
        return y

    xc = conv1(ax_ref[...], sac_ref, acw_ref, acb_ref, ac_ref, D_A)
    a, u = _lru_gates(xc, wr_ref, br_ref, wi_ref, bi_ref, lam_ref)
    h = a * sah_ref[...] + u
    ah_ref[...] = h
    aout_ref[...] = h * _silu_half(ag_ref[...])
    xact_ref[...] = _silu_half(conv1(xbc_ref[...], sbc_ref, bcw_ref, bcb_ref, bc_ref, CONV_DIM_B))
    dts_ref[...] = jax.nn.softplus(dt_ref[...] + dtb_ref[...])


def _pad_rows_t(x):
    pad = jnp.zeros((LANE - x.shape[0], x.shape[1]), F32)
    return jnp.concatenate([x, pad], axis=0).T


def _ab_sample_state_kernel(s_ref, xact_ref, dts_ref, z_ref, alog_ref, dexp_ref, bnw_ref,
                            so_ref, bout_ref, y_ref):
    bb = SAMPLE_BB
    xact = xact_ref[...]
    bx = xact[:, 0:D_B]
    dts = dts_ref[...]
    dec_t = _pad_rows_t(jnp.exp(dts * (-jnp.exp(alog_ref[...]))))
    dts_t = _pad_rows_t(dts)
    hpg = H_B // G_B
    for j in range(H_B // 2):
        g = (2 * j) // hpg
        xt = _pad_rows_t(bx[:, j * LANE:(j + 1) * LANE])
        dtp = jnp.concatenate([jnp.broadcast_to(dts_t[2 * j:2 * j + 1, :], (HD_B, LANE)),
                               jnp.broadcast_to(dts_t[2 * j + 1:2 * j + 2, :], (HD_B, LANE))], axis=0)
        xdt = xt * dtp
        for i in range(bb):
            brow = jnp.broadcast_to(xact[i:i + 1, D_B + g * N_B:D_B + (g + 1) * N_B], (2 * HD_B, N_B))
            crow = jnp.broadcast_to(
                xact[i:i + 1, D_B + G_B * N_B + g * N_B:D_B + G_B * N_B + (g + 1) * N_B], (2 * SUBLANE, N_B))
            upd = jnp.broadcast_to(xdt[:, i:i + 1], (2 * HD_B, N_B)) * brow
            news = []
            for hh in range(2):
                hd = 2 * j + hh
                dec = jnp.broadcast_to(dec_t[hd:hd + 1, i:i + 1], (HD_B, N_B))
                sn = dec * s_ref[i, hd] + upd[hh * HD_B:(hh + 1) * HD_B, :]
                so_ref[i, hd] = sn
                news.append(sn)
            spair = jnp.concatenate(news, axis=0).astype(BF16)
            yrow = _dot_tr(crow.astype(BF16), spair)
            y_ref[i:i + 1, j * LANE:(j + 1) * LANE] = yrow[0:1, :]
    bout_ref[...] = _group_norm_gate(y_ref[...], bx, z_ref[...], dexp_ref, bnw_ref)


def _ab_sample(u, dtr, s_ah, s_ac, s_bs, s_bc, w):
    bsz = u.shape[0]
    full = lambda shape: pl.BlockSpec(shape, lambda i: tuple(0 for _ in shape))
    cw = CONV_W - 1
    aout, ah, ac, bc, xact, dts = pl.pallas_call(
        _ab_sample_rows_kernel, grid=(1,),
        in_specs=[pl.BlockSpec((bsz, D_A), lambda i: (0, 0)), pl.BlockSpec((bsz, D_A), lambda i: (0, 1)),
                  pl.BlockSpec((bsz, CONV_DIM_B), lambda i: (0, 2)), full((bsz, LANE)),
                  full((bsz, D_A)), full((bsz, cw * D_A)), full((bsz, cw * CONV_DIM_B)),
                  full((CONV_W, D_A)), full((1, D_A)),
                  full((A_BLOCKS, A_BLK, A_BLK)), full((1, D_A)),
                  full((A_BLOCKS, A_BLK, A_BLK)), full((1, D_A)), full((1, D_A)),
                  full((CONV_W, CONV_DIM_B)), full((1, CONV_DIM_B)), full((1, LANE))],
        out_specs=[full((bsz, D_A)), full((bsz, D_A)), full((bsz, cw * D_A)), full((bsz, cw * CONV_DIM_B)),
                   full((bsz, CONV_DIM_B)), full((bsz, LANE))],
        out_shape=[jax.ShapeDtypeStruct((bsz, D_A), F32), jax.ShapeDtypeStruct((bsz, D_A), F32),
                   jax.ShapeDtypeStruct((bsz, cw * D_A), F32), jax.ShapeDtypeStruct((bsz, cw * CONV_DIM_B), F32),
                   jax.ShapeDtypeStruct((bsz, CONV_DIM_B), F32), jax.ShapeDtypeStruct((bsz, LANE), F32)],
        compiler_params=_params("arbitrary"),
        name="ab_sample_rows",
    )(u, u, u, dtr, s_ah, s_ac.reshape(bsz, cw * D_A), s_bc.reshape(bsz, cw * CONV_DIM_B),
      w["acw"], w["acb"], w["wr"], w["br"], w["wi"], w["bi"], w["lam"], w["bcw"], w["bcb"], w["dtb"])

    bb = SAMPLE_BB
    assert bsz % bb == 0
    cvec = lambda i: (0, 0)
    bs, bout = pl.pallas_call(
        _ab_sample_state_kernel, grid=(bsz // bb,),
        in_specs=[pl.BlockSpec((bb, H_B, HD_B, N_B), lambda i: (i, 0, 0, 0)),
                  pl.BlockSpec((bb, CONV_DIM_B), lambda i: (i, 0)),
                  pl.BlockSpec((bb, LANE), lambda i: (i, 0)),
                  pl.BlockSpec((bb, D_B), lambda i: (i, 2)),
                  pl.BlockSpec((1, LANE), cvec), pl.BlockSpec((1, D_B), cvec), pl.BlockSpec((1, D_B), cvec)],
        out_specs=[pl.BlockSpec((bb, H_B, HD_B, N_B), lambda i: (i, 0, 0, 0)),
                   pl.BlockSpec((bb, D_B), lambda i: (i, 0))],
        out_shape=[jax.ShapeDtypeStruct(s_bs.shape, F32), jax.ShapeDtypeStruct((bsz, D_B), F32)],
        scratch_shapes=[pltpu.VMEM((bb, D_B), F32)],
        compiler_params=_params("parallel"),
        name="ab_sample_state",
    )(s_bs, xact, dts, u, w["alog"], w["dexp"], w["bnw"])
    mix = jnp.concatenate([aout, bout], axis=-1)
    return mix, ah, ac.reshape(bsz, cw, D_A), bs, bc.reshape(bsz, cw, CONV_DIM_B)


def _hg_lower_bound(clb, layer):
    mx = jnp.max(clb, axis=0, keepdims=True)
    ex = jnp.exp(clb - mx)
    return jnp.sum(ex[1:layer + 1], axis=0, keepdims=True) / jnp.sum(ex, axis=0, keepdims=True)


def _hg_gates(fx_half, lb):
    f = 0.5 * (1.0 + lb) + (0.5 * (1.0 - lb)) * jnp.tanh(fx_half)
    return f, 1.0 - f


def _hg_out(o, gate_half, cnw):
    return o * lax.rsqrt(jnp.mean(o * o, axis=-1, keepdims=True) + EPS) * cnw * (gate_half + gate_half * jnp.tanh(gate_half))


def _hg_gamma():
    import numpy as np
    q = HG_CHUNK
    t = np.arange(q)[:, None]
    tau = np.arange(q)[None, :]
    mats = [(tau <= t)]
    for l in range(1, HG_MXU_LEVELS):
        w = 1 << l
        ref = (t // (2 * w)) * (2 * w) + w - 1
        upper = (t % (2 * w)) >= w
        mats.append(np.where(upper, (tau > ref) & (tau <= t), (tau > t) & (tau <= ref)))
    gam = np.concatenate(mats, axis=0).astype(np.float32)
    return jnp.asarray(np.concatenate([gam, gam], axis=1), dtype=BF16)


def _hg_level_table():
    import numpy as np
    q = HG_CHUNK
    t = np.arange(q)[:, None]
    s = np.arange(q)[None, :]
    x = t ^ s
    lvl = np.floor(np.log2(np.maximum(x, 1))).astype(np.int32)
    return jnp.asarray(np.where(t > s, lvl, -1).astype(np.int32))


def _c_prompt_kernel(q_ref, f_ref, v_ref, g_ref, clb_ref, cnw_ref, gam_ref, lvl_ref,
                     og_ref, cs_ref, st_ref, *, layer):
    c = pl.program_id(1)
    last = pl.num_programs(1) - 1
    qc = HG_CHUNK

    @pl.when(c == 0)
    def _():
        st_ref[...] = jnp.zeros_like(st_ref)

    gam = gam_ref[...]
    ntile = qc // SUBLANE
    sub = lax.broadcasted_iota(jnp.int32, (SUBLANE, DK_C), 0)
    sub_levels = HG_MXU_LEVELS
    sub_upper = [(sub & (1 << l)) != 0 for l in range(sub_levels)]

    def tiles(x):
        return [x[i * SUBLANE:(i + 1) * SUBLANE, :] for i in range(ntile)]

    def gate_split(hd, rows):
        f, kk = _hg_gates(f_ref[hd, rows, :].astype(F32), _hg_lower_bound(clb_ref[hd], layer))
        g = jnp.log(jnp.maximum(f, HG_F_MIN)) * LOG2_E
        g1 = g.astype(BF16)
        g2 = (g - g1.astype(F32)).astype(BF16)
        return (f, kk), jnp.concatenate([g1, g2], axis=0)

    def scores(hd, rows, fk, sums):
        f, kk = fk
        qh = q_ref[hd, rows, :].astype(F32) * (DK_C ** -0.5)
        bcum = sums[0:qc]
        st = st_ref[hd]
        o = _dot_tr((qh * jnp.exp2(bcum)).astype(BF16), st.astype(BF16))
        qt, kt, ft, bt = tiles(qh), tiles(kk), tiles(f), tiles(bcum)
        prods = []
        for l in range(HG_LEVELS):
            if l == 0:
                xt = [jnp.where(sub_upper[0], qt[i] * ft[i], kt[i]) for i in range(ntile)]
            elif l < HG_MXU_LEVELS:
                dec = tiles(jnp.exp2(sums[l * qc:(l + 1) * qc]))
                xt = [jnp.where(sub_upper[l], qt[i], kt[i]) * dec[i] for i in range(ntile)]
            else:
                wt = 1 << (l - HG_MXU_LEVELS)
                xt = []
                for blk in range(0, ntile, 2 * wt):
                    ref = (blk + wt) * SUBLANE - 1
                    bref = jnp.broadcast_to(bcum[ref:ref + 1, :], (SUBLANE, DK_C))
                    xt += [kt[i] * jnp.exp2(bref - bt[i]) for i in range(blk, blk + wt)]
                    xt += [qt[i] * jnp.exp2(bt[i] - bref) for i in range(blk + wt, blk + 2 * wt)]
            x = jnp.concatenate(xt, axis=0).astype(BF16)
            half = (1 << l) // BF16_ROWS
            if half == 0:
                p = tiles(_dot_tr(x, x))
                prods.append({i: p[i] for i in range(ntile)})
            else:
                ups = [r for r in range(qc // BF16_ROWS) if (r // half) & 1]
                pu = _dot_tr(jnp.concatenate([x[r * BF16_ROWS:(r + 1) * BF16_ROWS, :] for r in ups], axis=0), x)
                tpr = BF16_ROWS // SUBLANE
                prods.append({r * tpr + k: pu[(n * tpr + k) * SUBLANE:(n * tpr + k + 1) * SUBLANE, :]
                              for n, r in enumerate(ups) for k in range(tpr)})
        return qh, st, o, prods

    def level_masks():
        masks = {}
        for i in range(ntile):
            lv = lvl_ref[i * SUBLANE:(i + 1) * SUBLANE, :]
            for l in range(HG_LEVELS):
                if l < sub_levels or (i >> (l - sub_levels)) & 1:
                    masks[i, l] = lv == l
        return masks

    def combine(hd, rows, kk, bcum, qh, st, o, prods, masks):
        arows = []
        for i in range(ntile):
            a = jnp.zeros((SUBLANE, qc), F32)
            for l in range(HG_LEVELS):
                if (i, l) in masks:
                    a = jnp.where(masks[i, l], prods[l][i], a)
            arows.append(a)
        amat = jnp.concatenate(arows, axis=0)
        vb = v_ref[hd, rows, :].astype(BF16)
        o = o + _dot(amat.astype(BF16), vb) + jnp.sum(qh * kk, axis=-1, keepdims=True) * vb.astype(F32)
        blast = bcum[qc - 1:qc, :]
        kdec = (kk * jnp.exp2(blast - bcum)).astype(BF16)
        st_ref[hd] = st * jnp.exp2(blast) + _dot_tl(vb, kdec)
        return o

    nchunk = HG_BLOCK // qc

    def body(idx, carry):
        hg = idx // nchunk
        rows = pl.ds(pl.multiple_of((idx % nchunk) * qc, qc), qc)
        heads = [hg * HG_UNROLL + k for k in range(HG_UNROLL)]
        gs = [gate_split(hd, rows) for hd in heads]
        sums = _dot(gam, jnp.concatenate([s for _, s in gs], axis=1))
        sums = [sums[:, k * DK_C:(k + 1) * DK_C] for k in range(HG_UNROLL)]
        sc = [scores(hd, rows, gs[k][0], sums[k]) for k, hd in enumerate(heads)]
        masks = level_masks()
        outs = [combine(hd, rows, gs[k][0][1], sums[k][0:qc], *sc[k], masks) for k, hd in enumerate(heads)]
        for k, hd in enumerate(heads):
            og_ref[hd, rows, :] = _hg_out(outs[k], g_ref[hd, rows, :].astype(F32), cnw_ref[hd]).astype(BF16)
        return carry

    lax.fori_loop(0, (H_C // HG_UNROLL) * nchunk, body, 0)

    @pl.when(c == last)
    def _():
        for hd in range(H_C):
            cs_ref[0, hd] = st_ref[hd].T


def _c_prompt(u, bsz, seq, w, layer):
    tb = HG_BLOCK
    assert seq % tb == 0 and tb % HG_CHUNK == 0 and (1 << HG_LEVELS) == HG_CHUNK
    nc = seq // tb
    m = bsz * seq
    depth = w["clb"].shape[1]

    def part(k):
        return pl.BlockSpec((H_C, tb, LANE), lambda b, c: (k, b * nc + c, 0))

    c2 = lambda b, c: (0, 0)
    c3 = lambda b, c: (0, 0, 0)
    og, cs = pl.pallas_call(
        functools.partial(_c_prompt_kernel, layer=layer), grid=(bsz, nc),
        in_specs=[part(0), part(1), part(2), part(3),
                  pl.BlockSpec((H_C, depth, DK_C), c3), pl.BlockSpec((H_C, 1, DV_C), c3),
                  pl.BlockSpec(w["gam"].shape, c2), pl.BlockSpec(w["lvl"].shape, c2)],
        out_specs=[pl.BlockSpec((H_C, tb, LANE), lambda b, c: (0, b * nc + c, 0)),
                   pl.BlockSpec((1, H_C, DK_C, DV_C), lambda b, c: (b, 0, 0, 0))],
        out_shape=[jax.ShapeDtypeStruct((H_C, m, DV_C), BF16),
                   jax.ShapeDtypeStruct((bsz, H_C, DK_C, DV_C), F32)],
        scratch_shapes=[pltpu.VMEM((H_C, DV_C, DK_C), F32)],
        compiler_params=_params("parallel", "arbitrary"),
        name="c_prompt",
    )(u, u, u, u, w["clb"], w["cnw"], w["gam"], w["lvl"])
    return og, cs


def _c_sample_head(hd, q_ref, f_ref, v_ref, g_ref, s_ref, clb_ref, cnw_ref, og_ref, so_ref, layer):
    bb = SAMPLE_BB
    lane = lax.broadcasted_iota(jnp.int32, (DK_C, LANE), 1)
    first_rows = lax.broadcasted_iota(jnp.int32, (LANE, DV_C), 0) < bb
    lb = _hg_lower_bound(clb_ref[hd], layer)
    f, kk = _hg_gates(f_ref[hd], lb)
    f_t = _pad_rows_t(f)
    k_t = _pad_rows_t(kk)
    qs = q_ref[hd] * (DK_C ** -0.5)
    v = v_ref[hd]
    vpad = jnp.where(first_rows, jnp.tile(v, (LANE // bb, 1)), 0.0).astype(BF16)
    orows = []
    for i in range(bb):
        fcol = jnp.broadcast_to(f_t[:, i:i + 1], (DK_C, DV_C))
        kv = _dot(jnp.where(lane == i, k_t, 0.0).astype(BF16), vpad)
        sn = fcol * s_ref[i, hd] + kv
        so_ref[i, hd] = sn
        qrow = jnp.broadcast_to(qs[i:i + 1, :], (2 * SUBLANE, DK_C)).astype(BF16)
        orows.append(_dot(qrow, sn.astype(BF16))[0:1, :])
    o = jnp.concatenate(orows, axis=0)
    og_ref[hd] = _hg_out(o, g_ref[hd], cnw_ref[hd])


def _c_sample_kernel(q_ref, f_ref, v_ref, g_ref, s_ref, clb_ref, cnw_ref, og_ref, so_ref, *, layer):
    def head(hd, carry):
        _c_sample_head(hd, q_ref, f_ref, v_ref, g_ref, s_ref, clb_ref, cnw_ref, og_ref, so_ref, layer)
        return carry

    lax.fori_loop(0, H_C, head, 0, unroll=SAMPLE_UNROLL)


def _c_sample(u, s_c, w, layer):
    bsz = s_c.shape[0]
    bb = SAMPLE_BB
    assert bsz % bb == 0
    depth = w["clb"].shape[1]

    def part(k):
        return pl.BlockSpec((H_C, bb, LANE), lambda i: (k, i, 0))

    c3 = lambda i: (0, 0, 0)
    og, so = pl.pallas_call(
        functools.partial(_c_sample_kernel, layer=layer), grid=(bsz // bb,),
        in_specs=[part(0), part(1), part(2), part(3),
                  pl.BlockSpec((bb, H_C, DK_C, DV_C), lambda i: (i, 0, 0, 0)),
                  pl.BlockSpec((H_C, depth, DK_C), c3), pl.BlockSpec((H_C, 1, DV_C), c3)],
        out_specs=[pl.BlockSpec((H_C, bb, LANE), lambda i: (0, i, 0)),
                   pl.BlockSpec((bb, H_C, DK_C, DV_C), lambda i: (i, 0, 0, 0))],
        out_shape=[jax.ShapeDtypeStruct((H_C, bsz, DV_C), F32), jax.ShapeDtypeStruct(s_c.shape, F32)],
        compiler_params=_params("parallel"),
        name="c_sample",
    )(u, u, u, u, s_c, w["clb"], w["cnw"])
    return og, so


def _out_proj_c_sample_kernel(mix_ref, h_ref, p_ref, wo_ref, wg_ref, wp_ref, nf_ref,
                              q_ref, f_ref, v_ref, g_ref, s_ref, clb_ref, cnw_ref,
                              o_ref, og_ref, so_ref, *, final, layer):
    for hd in range(q_ref.shape[0]):
        _c_sample_head(hd, q_ref, f_ref, v_ref, g_ref, s_ref, clb_ref, cnw_ref, og_ref, so_ref, layer)
    _out_proj_kernel(mix_ref, h_ref, p_ref, wo_ref, wg_ref, wp_ref, nf_ref, o_ref, head_major=True, final=final)


def _out_proj_c_sample(mix, h, p, layer, wo, wg, wp, nf, u_s, s_c, w, *, final):
    m, d = h.shape
    bsz = s_c.shape[0]
    bb, split = SAMPLE_BB, FUSED_HEAD_SPLIT
    steps = (bsz // bb) * split
    hp = H_C // split
    tm = m // steps
    assert bsz % bb == 0 and H_C % split == 0 and m % steps == 0 and tm % BF16_ROWS == 0
    depth = w["clb"].shape[1]
    const = lambda i: (0, 0)

    def part(k):
        return pl.BlockSpec((hp, bb, LANE), lambda i: (k * split + i % split, i // split, 0))

    state = pl.BlockSpec((bb, hp, DK_C, DV_C), lambda i: (i // split, i % split, 0, 0))
    hrow = lambda i: (i % split, 0, 0)
    hout, og, so = pl.pallas_call(
        functools.partial(_out_proj_c_sample_kernel, final=final, layer=layer),
        grid=(steps,),
        in_specs=[pl.BlockSpec((mix.shape[0], tm, LANE), lambda i: (0, i, 0)),
                  pl.BlockSpec((tm, d), lambda i: (i, 0)),
                  pl.BlockSpec((None, tm, p.shape[2]), lambda i: (layer, i, 0)),
                  pl.BlockSpec(wo.shape, const), pl.BlockSpec(wg.shape, const),
                  pl.BlockSpec(wp.shape, const), pl.BlockSpec((1, d), const),
                  part(0), part(1), part(2), part(3), state,
                  pl.BlockSpec((hp, depth, DK_C), hrow), pl.BlockSpec((hp, 1, DV_C), hrow)],
        out_specs=[pl.BlockSpec((tm, d), lambda i: (i, 0)),
                   pl.BlockSpec((hp, bb, LANE), lambda i: (i % split, i // split, 0)),
                   state],
        out_shape=[jax.ShapeDtypeStruct((m, d), F32),
                   jax.ShapeDtypeStruct((H_C, bsz, DV_C), F32), jax.ShapeDtypeStruct(s_c.shape, F32)],
        compiler_params=_params("arbitrary"),
        name="out_proj_c_sample",
    )(mix, h, p, wo, wg, wp, nf.reshape(1, d), u_s, u_s, u_s, u_s, s_c, w["clb"], w["cnw"])
    return hout, og, so


def _row(v, width=None):
    v = v.astype(F32).reshape(1, -1)
    if width is not None and v.shape[1] < width:
        v = jnp.pad(v, ((0, 0), (0, width - v.shape[1])))
    return v


def kernel(x_prompt, x_sample, p_prompt, p_sample, state_a_h, state_a_conv, state_b_ssm, state_b_conv, state_c,
           norm_w, norm_f, ab_w_in, a_conv_w, a_conv_b, a_w_r, a_b_r, a_w_i, a_b_i, a_lam, b_conv_w, b_conv_b,
           b_dt_bias, b_a_log, b_d, b_norm_w, ab_w_out, c_w_in, c_lb, c_norm_w, c_w_out, ple_proj, ple_gate):
    depth = norm_w.shape[0]
    bp, seq, _ = x_prompt.shape
    bs = x_sample.shape[0]
    hp = x_prompt.reshape(bp * seq, D_MODEL)
    hs = x_sample.reshape(bs, D_MODEL)
    pp = p_prompt.reshape(depth, bp * seq, D_PLE)
    ps = p_sample.reshape(depth, bs, D_PLE)
    gam, lvl = _hg_gamma(), _hg_level_table()
    clb = c_lb.astype(F32).reshape(depth, H_C, DK_C).transpose(1, 0, 2)

    ah_p, ac_p, bs_p, bc_p, c_p = [], [], [], [], []
    ah_s, ac_s, bs_s, bc_s, c_s = [], [], [], [], []
    for i in range(depth):
        j = i // 2
        final = i == depth - 1
        wg = (0.5 * ple_gate[i]).astype(BF16)
        wp = (0.5 * ple_proj[i]).astype(BF16)
        if i % 2 == 0:
            col_scale = jnp.concatenate([jnp.ones((D_A,), F32), jnp.full((D_A + D_B,), 0.5, F32),
                                         jnp.ones((CONV_DIM_B,), F32)])
            w_ab_t = ab_w_in[j].T
            w_ab = (w_ab_t[:AB_MAIN] * col_scale[:, None]).astype(BF16)
            w_dt = jnp.pad(w_ab_t[AB_MAIN:], ((0, LANE - H_B), (0, 0))).astype(BF16)
            wo = ab_w_out[j].astype(BF16)
            w = dict(acw=a_conv_w[j].astype(F32), acb=_row(a_conv_b[j]),
                     wr=(0.5 * a_w_r[j]).astype(BF16), br=0.5 * _row(a_b_r[j]),
                     wi=a_w_i[j].astype(BF16), bi=_row(a_b_i[j]),
                     lam=_row(a_lam[j]), bcw=0.5 * b_conv_w[j].astype(F32), bcb=0.5 * _row(b_conv_b[j]),
                     dtb=_row(b_dt_bias[j], LANE), alog=_row(b_a_log[j], LANE),
                     dexp=_row(jnp.repeat(b_d[j], HD_B)), bnw=_row(b_norm_w[j]))
            u, dtr = _in_proj(hp, norm_w[i], w_ab, w_dt, w_t=True)
            hp, s1, s2, s3, s4 = _ab_prompt(u, dtr, hp, pp, i, wo, wg, wp, norm_f, bp, seq, w, final=final)
            ah_p.append(s1); ac_p.append(s2); bs_p.append(s3); bc_p.append(s4)
            u, dtr = _in_proj(hs, norm_w[i], w_ab, w_dt, w_t=True)
            mix, s1, s2, s3, s4 = _ab_sample(u, dtr, state_a_h[j], state_a_conv[j], state_b_ssm[j],
                                             state_b_conv[j], w)
            ah_s.append(s1); ac_s.append(s2); bs_s.append(s3); bc_s.append(s4)
            hs = _out_proj(mix, hs, ps, i, wo, wg, wp, norm_f, head_major=False, final=final)
        else:
            col_scale = jnp.concatenate([jnp.ones((HK_C,), F32), jnp.full((HK_C,), 0.5, F32),
                                         jnp.ones((D_C,), F32), jnp.full((D_C,), 0.5, F32)])
            w_in = (c_w_in[j] * col_scale).astype(BF16)
            wo = c_w_out[j].astype(BF16)
            w = dict(clb=clb, cnw=c_norm_w[j].astype(F32).reshape(H_C, 1, DV_C), gam=gam, lvl=lvl)
            u = _in_proj(hp, norm_w[i], w_in, head_major=True, head_major_dtype=BF16)
            og, s1 = _c_prompt(u, bp, seq, w, i)
            c_p.append(s1)
            u = _in_proj(hs, norm_w[i], w_in, head_major=True)
            steps = (bs // SAMPLE_BB) * FUSED_HEAD_SPLIT
            if bs % SAMPLE_BB == 0 and (bp * seq) % (steps * BF16_ROWS) == 0 and (bp * seq) // steps <= OUT_PROJ_TM:
                hp, og, s1 = _out_proj_c_sample(og, hp, pp, i, wo, wg, wp, norm_f, u, state_c[j], w, final=final)
            else:
                hp = _out_proj(og, hp, pp, i, wo, wg, wp, norm_f, head_major=True, final=final)
                og, s1 = _c_sample(u, state_c[j], w, i)
            c_s.append(s1)
            hs = _out_proj(og, hs, ps, i, wo, wg, wp, norm_f, head_major=True, final=final)
    return (hp.reshape(bp, seq, D_MODEL), hs.reshape(bs, 1, D_MODEL),
            jnp.stack(ah_p), jnp.stack(ac_p), jnp.stack(bs_p), jnp.stack(bc_p), jnp.stack(c_p),
            jnp.stack(ah_s), jnp.stack(ac_s), jnp.stack(bs_s), jnp.stack(bc_s), jnp.stack(c_s))
```

```python
import functools

import jax
import jax.numpy as jnp
from jax import lax
from jax.experimental import pallas as pl
from jax.experimental.pallas import tpu as pltpu

F32 = jnp.float32
BF16 = jnp.bfloat16

D_MODEL = 1024
D_PLE = 256
EPS = 1e-6
CONV_W = 4
D_A = D_MODEL
A_BLOCKS = 8
A_BLK = D_A // A_BLOCKS
LRU_C = 8.0
D_B = D_MODEL
HD_B = 64
H_B = D_B // HD_B
N_B = 128
G_B = 2
CONV_DIM_B = D_B + 2 * G_B * N_B
D_C = 2 * D_MODEL
H_C = 16
DK_C = 128
DV_C = D_C // H_C
HK_C = H_C * DK_C
AB_MAIN = 2 * D_A + D_B + CONV_DIM_B
IN_C = 2 * HK_C + 2 * D_C

LANE = 128
SUBLANE = 8
BF16_ROWS = 16
LOG2_E = 1.4426950408889634
VMEM_LIMIT = 56 * 1024 * 1024

PROJ_TM = 2048
PROJ_VMEM_BUDGET = 46 * 1024 * 1024
OUT_PROJ_TM = 1024
SSD_CHUNK = 128
AB_CHUNKS = 2
HG_CHUNK = 64
HG_LEVELS = 6
HG_MXU_LEVELS = 3
HG_F_MIN = 1e-30
HG_BLOCK = 512
HG_UNROLL = 16
SAMPLE_BB = 8
SAMPLE_UNROLL = 4
FUSED_HEAD_SPLIT = 2

_DN_TR = (((1,), (1,)), ((), ()))
_DN_TL = (((0,), (0,)), ((), ()))


def _dot(a, b):
    return jnp.dot(a, b, preferred_element_type=F32)


def _dot_tr(a, b):
    return lax.dot_general(a, b, _DN_TR, preferred_element_type=F32)


def _dot_tl(a, b):
    return lax.dot_general(a, b, _DN_TL, preferred_element_type=F32)


def _silu_half(x_half):
    return x_half + x_half * jnp.tanh(x_half)


def _rmsnorm(x, w):
    return x * lax.rsqrt(jnp.mean(x * x, axis=-1, keepdims=True) + EPS) * w


def _params(*sem):
    return pltpu.CompilerParams(dimension_semantics=sem, vmem_limit_bytes=VMEM_LIMIT)


def _in_proj_kernel(x_ref, nw_ref, w_ref, *rest, has_extra, head_major, w_t):
    if has_extra:
        wx_ref, o_ref, ox_ref, xn_ref = rest
    else:
        o_ref, xn_ref = rest
    dot = _dot_tr if w_t else _dot

    @pl.when(pl.program_id(1) == 0)
    def _():
        xn_ref[...] = _rmsnorm(x_ref[...], nw_ref[...]).astype(BF16)
        if has_extra:
            ox_ref[...] = dot(xn_ref[...], wx_ref[...])

    acc = dot(xn_ref[...], w_ref[...])
    if head_major:
        for k in range(acc.shape[1] // LANE):
            o_ref[k] = acc[:, k * LANE:(k + 1) * LANE].astype(o_ref.dtype)
    else:
        o_ref[...] = acc


def _in_proj(x, nw, w, w_extra=None, *, head_major=False, w_t=False, head_major_dtype=F32):
    m, k = x.shape
    extra_cols = w_extra is not None
    n = w.shape[0 if w_t else 1]
    tm = min(m, PROJ_TM)
    out_bytes = jnp.dtype(head_major_dtype if head_major else F32).itemsize

    def vmem(c):
        extra = 2 * k * LANE * 2 + 2 * tm * LANE * 4 if extra_cols else 0
        return 2 * tm * k * 4 + tm * k * 2 + 2 * k * c * 2 + 2 * tm * c * out_bytes + extra

    tn = next(c for c in range(n, 0, -LANE) if n % c == 0 and vmem(c) <= PROJ_VMEM_BUDGET)
    assert m % tm == 0 and n % tn == 0 and tn % LANE == 0
    grid = (m // tm, n // tn)
    wspec = (lambda cols, idx: pl.BlockSpec((cols, k), lambda i, j: (idx(j), 0))) if w_t else \
            (lambda cols, idx: pl.BlockSpec((k, cols), lambda i, j: (0, idx(j))))
    in_specs = [pl.BlockSpec((tm, k), lambda i, j: (i, 0)),
                pl.BlockSpec((1, k), lambda i, j: (0, 0)),
                wspec(tn, lambda j: j)]
    args = [x, nw.reshape(1, k), w]
    if head_major:
        out_shape = [jax.ShapeDtypeStruct((n // LANE, m, LANE), head_major_dtype)]
        out_specs = [pl.BlockSpec((tn // LANE, tm, LANE), lambda i, j: (j, i, 0))]
    else:
        out_shape = [jax.ShapeDtypeStruct((m, n), F32)]
        out_specs = [pl.BlockSpec((tm, tn), lambda i, j: (i, j))]
    if extra_cols:
        in_specs.append(wspec(LANE, lambda j: 0))
        args.append(w_extra)
        out_shape.append(jax.ShapeDtypeStruct((m, LANE), F32))
        out_specs.append(pl.BlockSpec((tm, LANE), lambda i, j: (i, 0)))
    outs = pl.pallas_call(
        functools.partial(_in_proj_kernel, has_extra=extra_cols, head_major=head_major, w_t=w_t),
        grid=grid, in_specs=in_specs, out_specs=out_specs, out_shape=out_shape,
        scratch_shapes=[pltpu.VMEM((tm, k), BF16)],
        compiler_params=_params("parallel", "arbitrary"),
        name="in_proj",
    )(*args)
    return outs if extra_cols else outs[0]


def _residual_ple(h, p_ref, wg_ref, wp_ref, nf_ref, final):
    gate_t = jnp.tanh(_dot(h.astype(BF16), wg_ref[...]))
    pe_half = _dot(p_ref[...].astype(BF16), wp_ref[...])
    h = h + pe_half + pe_half * gate_t
    return _rmsnorm(h, nf_ref[...]) if final else h


def _out_proj_kernel(mix_ref, h_ref, p_ref, wo_ref, wg_ref, wp_ref, nf_ref, o_ref, *, head_major, final):
    if head_major:
        mix = jnp.concatenate([mix_ref[k] for k in range(mix_ref.shape[0])], axis=-1)
    else:
        mix = mix_ref[...]
    h = h_ref[...] + _dot(mix.astype(BF16), wo_ref[...])
    o_ref[...] = _residual_ple(h, p_ref, wg_ref, wp_ref, nf_ref, final)


def _out_proj(mix, h, p, layer, wo, wg, wp, nf, *, head_major, final):
    m, d = h.shape
    tm = min(m, OUT_PROJ_TM)
    assert m % tm == 0
    if head_major:
        mix_spec = pl.BlockSpec((mix.shape[0], tm, LANE), lambda i: (0, i, 0))
    else:
        mix_spec = pl.BlockSpec((tm, mix.shape[1]), lambda i: (i, 0))
    const = lambda i: (0, 0)
    return pl.pallas_call(
        functools.partial(_out_proj_kernel, head_major=head_major, final=final),
        grid=(m // tm,),
        in_specs=[mix_spec,
                  pl.BlockSpec((tm, d), lambda i: (i, 0)),
                  pl.BlockSpec((None, tm, p.shape[2]), lambda i: (layer, i, 0)),
                  pl.BlockSpec(wo.shape, const), pl.BlockSpec(wg.shape, const),
                  pl.BlockSpec(wp.shape, const), pl.BlockSpec((1, d), const)],
        out_specs=pl.BlockSpec((tm, d), lambda i: (i, 0)),
        out_shape=jax.ShapeDtypeStruct((m, d), F32),
        compiler_params=_params("parallel"),
        name="out_proj",
    )(mix, h, p, wo, wg, wp, nf.reshape(1, d))


def _lru_gates(xc, wr_ref, br_ref, wi_ref, bi_ref, lam_ref):
    xcb = xc.astype(BF16)
    r_parts, i_parts = [], []
    for k in range(A_BLOCKS):
        xk = xcb[:, k * A_BLK:(k + 1) * A_BLK]
        r_parts.append(_dot(xk, wr_ref[k]))
        i_parts.append(_dot(xk, wi_ref[k]))
    tr = jnp.tanh(jnp.concatenate(r_parts, axis=-1) + br_ref[...])
    gi = jax.nn.sigmoid(jnp.concatenate(i_parts, axis=-1) + bi_ref[...])
    half_rate = (-0.5 * LRU_C) * jax.nn.softplus(-lam_ref[...])
    log_a = half_rate + half_rate * tr
    a = jnp.exp(log_a)
    v = jnp.tanh(-log_a) * (a * a + 1.0)
    u = jnp.where(v > 0.0, v * lax.rsqrt(v), 0.0) * (gi * xc)
    return a, u


def _group_norm_gate(y, bx, z_half, dexp_ref, bnw_ref):
    y = (y + dexp_ref[...] * bx) * _silu_half(z_half)
    gw = D_B // G_B
    parts = []
    for g in range(G_B):
        yg = y[:, g * gw:(g + 1) * gw]
        parts.append(yg * lax.rsqrt(jnp.mean(yg * yg, axis=-1, keepdims=True) + EPS))
    return jnp.concatenate(parts, axis=-1) * bnw_ref[...]


def _ab_prompt_kernel(ax_ref, ag_ref, z_ref, xbc_ref, dt_ref,
                      acw_ref, acb_ref, wr_ref, br_ref, wi_ref, bi_ref, lam_ref,
                      bcw_ref, bcb_ref, dtb_ref, alog_ref, dexp_ref, bnw_ref,
                      hin_ref, p_ref, wo_ref, wg_ref, wp_ref, nf_ref,
                      *rest, final, fused):
    rest = list(rest)
    sample_in = [rest.pop(0) for _ in range(6)] if fused else None
    hout_ref = rest.pop(0)
    state_out = [rest.pop(0) for _ in range(4)]
    sample_out = [rest.pop(0) for _ in range(2)] if fused else None
    state_refs = state_out + rest
    for k in range(AB_CHUNKS):
        if fused and k == AB_CHUNKS - 1:
            _ab_sample_pairs(*sample_in, *sample_out, 0)
        r = pl.ds(k * SSD_CHUNK, SSD_CHUNK)
        _ab_prompt_chunk(ax_ref.at[r], ag_ref.at[r], z_ref.at[r], xbc_ref.at[r], dt_ref.at[r],
                         acw_ref, acb_ref, wr_ref, br_ref, wi_ref, bi_ref, lam_ref,
                         bcw_ref, bcb_ref, dtb_ref, alog_ref, dexp_ref, bnw_ref,
                         hin_ref.at[r], p_ref.at[r], wo_ref, wg_ref, wp_ref, nf_ref,
                         hout_ref.at[r], *state_refs, final=final, first=k == 0, last=k == AB_CHUNKS - 1)


def _ab_prompt_chunk(ax_ref, ag_ref, z_ref, xbc_ref, dt_ref,
                     acw_ref, acb_ref, wr_ref, br_ref, wi_ref, bi_ref, lam_ref,
                     bcw_ref, bcb_ref, dtb_ref, alog_ref, dexp_ref, bnw_ref,
                     hin_ref, p_ref, wo_ref, wg_ref, wp_ref, nf_ref,
                     hout_ref, ah_ref, ac_ref, bs_ref, bc_ref,
                     xpa_ref, xpb_ref, h_ref, s_ref, *, final, first, last):
    c = pl.program_id(1)
    t = SSD_CHUNK
    ntile = t // SUBLANE

    if first:
        @pl.when(c == 0)
        def _():
            xpa_ref[...] = jnp.zeros_like(xpa_ref)
            xpb_ref[...] = jnp.zeros_like(xpb_ref)
            h_ref[...] = jnp.zeros_like(h_ref)
            s_ref[...] = jnp.zeros_like(s_ref)

    def tiles(x):
        return [x[i * SUBLANE:(i + 1) * SUBLANE, :] for i in range(ntile)]

    def conv(x, tail_ref, w_ref, b_ref):
        sub = lax.broadcasted_iota(jnp.int32, (SUBLANE, x.shape[1]), 0)
        xt = [tail_ref[...]] + tiles(x)
        taps = [jnp.broadcast_to(w_ref[k:k + 1, :], (SUBLANE, x.shape[1])) for k in range(CONV_W)]
        bias = jnp.broadcast_to(b_ref[...], (SUBLANE, x.shape[1]))
        acc = [bias + taps[CONV_W - 1] * xt[i + 1] for i in range(ntile)]
        for s in range(1, CONV_W):
            wk = taps[CONV_W - 1 - s]
            for i in range(ntile):
                merged = jnp.where(sub >= SUBLANE - s, xt[i], xt[i + 1])
                acc[i] = acc[i] + wk * pltpu.roll(merged, s, 0)
        tail_ref[...] = xt[ntile]
        return jnp.concatenate(acc, axis=0)

    ax = ax_ref[...]
    xc = conv(ax, xpa_ref, acw_ref, acb_ref)
    a, u = _lru_gates(xc, wr_ref, br_ref, wi_ref, bi_ref, lam_ref)
    sub = lax.broadcasted_iota(jnp.int32, (SUBLANE, D_A), 0)
    at, ut = tiles(a), tiles(u)
    step = 1
    while step < SUBLANE:
        m = sub >= step
        for i in range(ntile):
            ut[i] = jnp.where(m, at[i] * pltpu.roll(ut[i], step, 0) + ut[i], ut[i])
            at[i] = jnp.where(m, at[i] * pltpu.roll(at[i], step, 0), at[i])
        step *= 2
    carry = h_ref[0:1, :]
    hs = []
    for i in range(ntile):
        hs.append(ut[i] + at[i] * carry)
        carry = hs[i][SUBLANE - 1:SUBLANE, :]
    h = jnp.concatenate(hs, axis=0)
    h_ref[0:1, :] = carry
    a_out = (h * _silu_half(ag_ref[...])).astype(BF16)

    xb = xbc_ref[...]
    xbc = _silu_half(conv(xb, xpb_ref, bcw_ref, bcb_ref))
    bx = xbc[:, 0:D_B]
    bxb = bx.astype(BF16)
    dt = jax.nn.softplus(dt_ref[...] + dtb_ref[...])
    adt = dt * (-LOG2_E * jnp.exp(alog_ref[...]))
    ti = lax.broadcasted_iota(jnp.int32, (t, t), 0)
    si = lax.broadcasted_iota(jnp.int32, (t, t), 1)
    causal = ti >= si
    acs = jnp.dot(causal.astype(F32), adt, preferred_element_type=F32,
                  precision=lax.Precision.HIGHEST)
    a_last = acs[t - 1:t, :]
    wq = jnp.exp2(a_last - acs) * dt
    eacs = jnp.exp2(acs)
    ealast = jnp.exp2(a_last)
    acs_t = acs.T
    dt_t = dt.T
    lane = lax.broadcasted_iota(jnp.int32, (t, LANE), 1)
    rowi = lax.broadcasted_iota(jnp.int32, (LANE, N_B), 0)
    hpg = H_B // G_B
    ys = []
    cb = None
    for j in range(H_B // 2):
        g = (2 * j) // hpg
        bg = xbc[:, D_B + g * N_B:D_B + (g + 1) * N_B]
        cg = xbc[:, D_B + G_B * N_B + g * N_B:D_B + G_B * N_B + (g + 1) * N_B]
        if (2 * j) % hpg == 0:
            cb = _dot_tr(cg.astype(BF16), bg.astype(BF16))
        xpair = bxb[:, j * LANE:(j + 1) * LANE]
        sp = s_ref[j]
        spb = sp.astype(BF16)
        y_h, up_h = [], []
        for hh in range(2):
            hd = 2 * j + hh
            seg = jnp.broadcast_to(acs[:, hd:hd + 1], (t, t)) - jnp.broadcast_to(acs_t[hd:hd + 1, :], (t, t))
            lmat = jnp.exp2(jnp.where(causal, seg, -1e30))
            mmat = (cb * lmat * jnp.broadcast_to(dt_t[hd:hd + 1, :], (t, t))).astype(BF16)
            ec = (jnp.broadcast_to(eacs[:, hd:hd + 1], (t, N_B)) * cg).astype(BF16)
            y_h.append(_dot(mmat, xpair) + _dot_tr(ec, spb))
            bw = (bg * jnp.broadcast_to(wq[:, hd:hd + 1], (t, N_B))).astype(BF16)
            up_h.append(_dot_tl(xpair, bw))
        ys.append(jnp.where(lane < HD_B, y_h[0], y_h[1]))
        dec = jnp.where(rowi < HD_B,
                        jnp.broadcast_to(ealast[:, 2 * j:2 * j + 1], (LANE, N_B)),
                        jnp.broadcast_to(ealast[:, 2 * j + 1:2 * j + 2], (LANE, N_B)))
        s_ref[j] = dec * sp + jnp.where(rowi < HD_B, up_h[0], up_h[1])
    y = jnp.concatenate(ys, axis=-1)
    b_out = _group_norm_gate(y, bx, z_ref[...], dexp_ref, bnw_ref).astype(BF16)

    hres = hin_ref[...] + _dot(a_out, wo_ref[0:D_A, :]) + _dot(b_out, wo_ref[D_A:D_A + D_B, :])
    hout_ref[...] = _residual_ple(hres, p_ref, wg_ref, wp_ref, nf_ref, final)

    if last:
        @pl.when(c == pl.num_programs(1) - 1)
        def _():
            ah_ref[0] = h[t - 1:t, :]
            ac_ref[0] = ax[t - (CONV_W - 1):t, :]
            bc_ref[0] = xb[t - (CONV_W - 1):t, :]
            for j in range(H_B // 2):
                sj = s_ref[j]
                bs_ref[0, 2 * j] = sj[0:HD_B, :]
                bs_ref[0, 2 * j + 1] = sj[HD_B:2 * HD_B, :]


def _ab_prompt(u, dtr, h, p, layer, wo, wg, wp, nf, bsz, seq, w, *, final, sample=None):
    t = AB_CHUNKS * SSD_CHUNK
    assert seq % t == 0
    nc = seq // t
    m = bsz * seq
    rows = lambda b, c: b * nc + c
    cvec = lambda b, c: (0, 0)
    c3 = lambda b, c: (0, 0, 0)
    in_specs = [
        pl.BlockSpec((t, D_A), lambda b, c: (rows(b, c), 0)),
        pl.BlockSpec((t, D_A), lambda b, c: (rows(b, c), 1)),
        pl.BlockSpec((t, D_B), lambda b, c: (rows(b, c), 2)),
        pl.BlockSpec((t, CONV_DIM_B), lambda b, c: (rows(b, c), 2)),
        pl.BlockSpec((t, LANE), lambda b, c: (rows(b, c), 0)),
        pl.BlockSpec((CONV_W, D_A), cvec), pl.BlockSpec((1, D_A), cvec),
        pl.BlockSpec((A_BLOCKS, A_BLK, A_BLK), c3), pl.BlockSpec((1, D_A), cvec),
        pl.BlockSpec((A_BLOCKS, A_BLK, A_BLK), c3), pl.BlockSpec((1, D_A), cvec),
        pl.BlockSpec((1, D_A), cvec),
        pl.BlockSpec((CONV_W, CONV_DIM_B), cvec), pl.BlockSpec((1, CONV_DIM_B), cvec),
        pl.BlockSpec((1, LANE), cvec), pl.BlockSpec((1, LANE), cvec),
        pl.BlockSpec((1, D_B), cvec), pl.BlockSpec((1, D_B), cvec),
        pl.BlockSpec((t, D_MODEL), lambda b, c: (rows(b, c), 0)),
        pl.BlockSpec((None, t, p.shape[2]), lambda b, c: (layer, rows(b, c), 0)),
        pl.BlockSpec(wo.shape, cvec), pl.BlockSpec(wg.shape, cvec), pl.BlockSpec(wp.shape, cvec),
        pl.BlockSpec((1, D_MODEL), cvec),
    ]
    out_shape = [
        jax.ShapeDtypeStruct((m, D_MODEL), F32),
        jax.ShapeDtypeStruct((bsz, 1, D_A), F32),
        jax.ShapeDtypeStruct((bsz, CONV_W - 1, D_A), F32),
        jax.ShapeDtypeStruct((bsz, H_B, HD_B, N_B), F32),
        jax.ShapeDtypeStruct((bsz, CONV_W - 1, CONV_DIM_B), F32),
    ]
    out_specs = [
        pl.BlockSpec((t, D_MODEL), lambda b, c: (rows(b, c), 0)),
        pl.BlockSpec((1, 1, D_A), lambda b, c: (b, 0, 0)),
        pl.BlockSpec((1, CONV_W - 1, D_A), lambda b, c: (b, 0, 0)),
        pl.BlockSpec((1, H_B, HD_B, N_B), lambda b, c: (b, 0, 0, 0)),
        pl.BlockSpec((1, CONV_W - 1, CONV_DIM_B), lambda b, c: (b, 0, 0)),
    ]
    args = [u, u, u, u, dtr, w["acw"], w["acb"], w["wr"], w["br"], w["wi"], w["bi"], w["lam"],
            w["bcw"], w["bcb"], w["dtb"], w["alog"], w["dexp"], w["bnw"],
            h, p, wo, wg, wp, nf.reshape(1, D_MODEL)]
    fused = sample is not None
    if fused:
        s_bs, xact, dts = sample
        sbsz, bb = s_bs.shape[0], SAMPLE_BB
        split = (bsz * nc) // (sbsz // bb)
        hps = H_B // split
        assert sbsz % bb == 0 and (bsz * nc) % (sbsz // bb) == 0 and H_B % split == 0
        assert hps % 2 == 0 and (H_B // G_B) % hps == 0
        regroup = lambda v: jnp.pad(v[:, :H_B].reshape(-1, split, hps).transpose(1, 0, 2),
                                    ((0, 0), (0, 0), (0, LANE - hps)))
        blk = lambda b, c: rows(b, c) // split
        grp = lambda b, c: rows(b, c) % split
        bc_group = lambda b, c: (grp(b, c) * hps) // (H_B // G_B)
        state = pl.BlockSpec((bb, hps, HD_B, N_B), lambda b, c: (blk(b, c), grp(b, c), 0, 0))
        in_specs += [state,
                     pl.BlockSpec((bb, hps * HD_B), lambda b, c: (blk(b, c), grp(b, c))),
                     pl.BlockSpec((bb, N_B), lambda b, c: (blk(b, c), D_B // N_B + bc_group(b, c))),
                     pl.BlockSpec((bb, N_B), lambda b, c: (blk(b, c), D_B // N_B + G_B + bc_group(b, c))),
                     pl.BlockSpec((None, bb, LANE), lambda b, c: (grp(b, c), blk(b, c), 0)),
                     pl.BlockSpec((None, 1, LANE), lambda b, c: (grp(b, c), 0, 0))]
        args += [s_bs, xact, xact, xact, regroup(dts), regroup(w["alog"])]
        out_shape += [jax.ShapeDtypeStruct(s_bs.shape, F32), jax.ShapeDtypeStruct((sbsz, D_B), F32)]
        out_specs += [state, pl.BlockSpec((bb, hps * HD_B), lambda b, c: (blk(b, c), grp(b, c)))]
    outs = pl.pallas_call(
        functools.partial(_ab_prompt_kernel, final=final, fused=fused),
        grid=(bsz, nc), in_specs=in_specs, out_specs=out_specs, out_shape=out_shape,
        scratch_shapes=[pltpu.VMEM((SUBLANE, D_A), F32), pltpu.VMEM((SUBLANE, CONV_DIM_B), F32),
                        pltpu.VMEM((SUBLANE, D_A), F32), pltpu.VMEM((H_B // 2, 2 * HD_B, N_B), F32)],
        compiler_params=_params("arbitrary" if fused else "parallel", "arbitrary"),
        name="ab_prompt",
    )(*args)
    hout, ah, ac, bs, bc = outs[:5]
    return (hout, ah.reshape(bsz, D_A), ac, bs, bc) + tuple(outs[5:])


def _ab_sample_rows_kernel(ax_ref, ag_ref, xbc_ref, dt_ref, sah_ref, sac_ref, sbc_ref,
                           acw_ref, acb_ref, wr_ref, br_ref, wi_ref, bi_ref, lam_ref,
                           bcw_ref, bcb_ref, dtb_ref,
                           aout_ref, ah_ref, ac_ref, bc_ref, xact_ref, dts_ref):
    def conv1(x, buf_ref, w_ref, b_ref, nbuf_ref, width):
        y = b_ref[...] + w_ref[CONV_W - 1:CONV_W, :] * x
        for k in range(CONV_W - 1):
            y = y + w_ref[k:k + 1, :] * buf_ref[:, k * width:(k + 1) * width]
        for k in range(CONV_W - 2):
            nbuf_ref[:, k * width:(k + 1) * width] = buf_ref[:, (k + 1) * width:(k + 2) * width]
        nbuf_ref[:, (CONV_W - 2) * width:(CONV_W - 1) * width] = x
        return y

    xc = conv1(ax_ref[...], sac_ref, acw_ref, acb_ref, ac_ref, D_A)
    a, u = _lru_gates(xc, wr_ref, br_ref, wi_ref, bi_ref, lam_ref)
    h = a * sah_ref[...] + u
    ah_ref[...] = h
    aout_ref[...] = h * _silu_half(ag_ref[...])
    xact_ref[...] = _silu_half(conv1(xbc_ref[...], sbc_ref, bcw_ref, bcb_ref, bc_ref, CONV_DIM_B))
    dts_ref[...] = jax.nn.softplus(dt_ref[...] + dtb_ref[...])


def _pad_rows_t(x):
    pad = jnp.zeros((LANE - x.shape[0], x.shape[1]), F32)
    return jnp.concatenate([x, pad], axis=0).T


def _ab_sample_pairs(s_ref, bx_ref, b_ref, c_ref, dts_ref, alog_ref, so_ref, y_ref, head0):
    bb = SAMPLE_BB
    dts = dts_ref[...]
    dec_t = _pad_rows_t(jnp.exp(dts * (-jnp.exp(alog_ref[...]))))
    dts_t = _pad_rows_t(dts)
    for j in range(bx_ref.shape[1] // LANE):
        h0 = head0 + 2 * j
        xt = _pad_rows_t(bx_ref[:, j * LANE:(j + 1) * LANE])
        dtp = jnp.concatenate([jnp.broadcast_to(dts_t[h0:h0 + 1, :], (HD_B, LANE)),
                               jnp.broadcast_to(dts_t[h0 + 1:h0 + 2, :], (HD_B, LANE))], axis=0)
        xdt = xt * dtp
        for i in range(bb):
            brow = jnp.broadcast_to(b_ref[i:i + 1, :], (2 * HD_B, N_B))
            crow = jnp.broadcast_to(c_ref[i:i + 1, :], (2 * SUBLANE, N_B))
            upd = jnp.broadcast_to(xdt[:, i:i + 1], (2 * HD_B, N_B)) * brow
            news = []
            for hh in range(2):
                dec = jnp.broadcast_to(dec_t[h0 + hh:h0 + hh + 1, i:i + 1], (HD_B, N_B))
                sn = dec * s_ref[i, 2 * j + hh] + upd[hh * HD_B:(hh + 1) * HD_B, :]
                so_ref[i, 2 * j + hh] = sn
                news.append(sn)
            spair = jnp.concatenate(news, axis=0).astype(BF16)
            yrow = _dot_tr(crow.astype(BF16), spair)
            y_ref[i:i + 1, j * LANE:(j + 1) * LANE] = yrow[0:1, :]


def _ab_sample_state_kernel(s_ref, xact_ref, dts_ref, alog_ref, so_ref, y_ref):
    hpg = H_B // G_B
    gw = D_B // G_B
    for g in range(G_B):
        heads = pl.ds(g * hpg, hpg)
        _ab_sample_pairs(s_ref.at[:, heads], xact_ref.at[:, pl.ds(g * gw, gw)],
                         xact_ref.at[:, pl.ds(D_B + g * N_B, N_B)],
                         xact_ref.at[:, pl.ds(D_B + G_B * N_B + g * N_B, N_B)],
                         dts_ref, alog_ref, so_ref.at[:, heads], y_ref.at[:, pl.ds(g * gw, gw)], g * hpg)


def _ab_sample_finish_kernel(y_ref, xact_ref, z_ref, dexp_ref, bnw_ref, bout_ref):
    bout_ref[...] = _group_norm_gate(y_ref[...], xact_ref[:, 0:D_B], z_ref[...], dexp_ref, bnw_ref)


def _ab_sample_rows(u, dtr, s_ah, s_ac, s_bc, w):
    bsz = u.shape[0]
    full = lambda shape: pl.BlockSpec(shape, lambda i: tuple(0 for _ in shape))
    cw = CONV_W - 1
    aout, ah, ac, bc, xact, dts = pl.pallas_call(
        _ab_sample_rows_kernel, grid=(1,),
        in_specs=[pl.BlockSpec((bsz, D_A), lambda i: (0, 0)), pl.BlockSpec((bsz, D_A), lambda i: (0, 1)),
                  pl.BlockSpec((bsz, CONV_DIM_B), lambda i: (0, 2)), full((bsz, LANE)),
                  full((bsz, D_A)), full((bsz, cw * D_A)), full((bsz, cw * CONV_DIM_B)),
                  full((CONV_W, D_A)), full((1, D_A)),
                  full((A_BLOCKS, A_BLK, A_BLK)), full((1, D_A)),
                  full((A_BLOCKS, A_BLK, A_BLK)), full((1, D_A)), full((1, D_A)),
                  full((CONV_W, CONV_DIM_B)), full((1, CONV_DIM_B)), full((1, LANE))],
        out_specs=[full((bsz, D_A)), full((bsz, D_A)), full((bsz, cw * D_A)), full((bsz, cw * CONV_DIM_B)),
                   full((bsz, CONV_DIM_B)), full((bsz, LANE))],
        out_shape=[jax.ShapeDtypeStruct((bsz, D_A), F32), jax.ShapeDtypeStruct((bsz, D_A), F32),
                   jax.ShapeDtypeStruct((bsz, cw * D_A), F32), jax.ShapeDtypeStruct((bsz, cw * CONV_DIM_B), F32),
                   jax.ShapeDtypeStruct((bsz, CONV_DIM_B), F32), jax.ShapeDtypeStruct((bsz, LANE), F32)],
        compiler_params=_params("arbitrary"),
        name="ab_sample_rows",
    )(u, u, u, dtr, s_ah, s_ac.reshape(bsz, cw * D_A), s_bc.reshape(bsz, cw * CONV_DIM_B),
      w["acw"], w["acb"], w["wr"], w["br"], w["wi"], w["bi"], w["lam"], w["bcw"], w["bcb"], w["dtb"])
    return aout, ah, ac.reshape(bsz, cw, D_A), bc.reshape(bsz, cw, CONV_DIM_B), xact, dts


def _ab_sample_state(s_bs, xact, dts, w):
    bsz = xact.shape[0]
    bb = SAMPLE_BB
    assert bsz % bb == 0
    return pl.pallas_call(
        _ab_sample_state_kernel, grid=(bsz // bb,),
        in_specs=[pl.BlockSpec((bb, H_B, HD_B, N_B), lambda i: (i, 0, 0, 0)),
                  pl.BlockSpec((bb, CONV_DIM_B), lambda i: (i, 0)),
                  pl.BlockSpec((bb, LANE), lambda i: (i, 0)),
                  pl.BlockSpec((1, LANE), lambda i: (0, 0))],
        out_specs=[pl.BlockSpec((bb, H_B, HD_B, N_B), lambda i: (i, 0, 0, 0)),
                   pl.BlockSpec((bb, D_B), lambda i: (i, 0))],
        out_shape=[jax.ShapeDtypeStruct(s_bs.shape, F32), jax.ShapeDtypeStruct((bsz, D_B), F32)],
        compiler_params=_params("parallel"),
        name="ab_sample_state",
    )(s_bs, xact, dts, w["alog"])


def _ab_sample_finish(y, xact, u, w):
    bsz = y.shape[0]
    full = lambda shape: pl.BlockSpec(shape, lambda i: (0, 0))
    return pl.pallas_call(
        _ab_sample_finish_kernel, grid=(1,),
        in_specs=[full((bsz, D_B)), full((bsz, CONV_DIM_B)),
                  pl.BlockSpec((bsz, D_B), lambda i: (0, 2)),
                  full((1, D_B)), full((1, D_B))],
        out_specs=full((bsz, D_B)),
        out_shape=jax.ShapeDtypeStruct((bsz, D_B), F32),
        compiler_params=_params("arbitrary"),
        name="ab_sample_finish",
    )(y, xact, u, w["dexp"], w["bnw"])


def _hg_lower_bound(clb, layer):
    mx = jnp.max(clb, axis=0, keepdims=True)
    ex = jnp.exp(clb - mx)
    return jnp.sum(ex[1:layer + 1], axis=0, keepdims=True) / jnp.sum(ex, axis=0, keepdims=True)


def _hg_gates(fx_half, lb):
    f = 0.5 * (1.0 + lb) + (0.5 * (1.0 - lb)) * jnp.tanh(fx_half)
    return f, 1.0 - f


def _hg_out(o, gate_half, cnw):
    return o * lax.rsqrt(jnp.mean(o * o, axis=-1, keepdims=True) + EPS) * cnw * (gate_half + gate_half * jnp.tanh(gate_half))


def _hg_gamma():
    import numpy as np
    q = HG_CHUNK
    t = np.arange(q)[:, None]
    tau = np.arange(q)[None, :]
    mats = [(tau <= t)]
    for l in range(1, HG_MXU_LEVELS):
        w = 1 << l
        ref = (t // (2 * w)) * (2 * w) + w - 1
        upper = (t % (2 * w)) >= w
        mats.append(np.where(upper, (tau > ref) & (tau <= t), (tau > t) & (tau <= ref)))
    gam = np.concatenate(mats, axis=0).astype(np.float32)
    return jnp.asarray(np.concatenate([gam, gam], axis=1), dtype=BF16)


def _hg_level_table():
    import numpy as np
    q = HG_CHUNK
    t = np.arange(q)[:, None]
    s = np.arange(q)[None, :]
    x = t ^ s
    lvl = np.floor(np.log2(np.maximum(x, 1))).astype(np.int32)
    return jnp.asarray(np.where(t > s, lvl, -1).astype(np.int32))


def _c_prompt_kernel(q_ref, f_ref, v_ref, g_ref, clb_ref, cnw_ref, gam_ref, lvl_ref,
                     og_ref, cs_ref, st_ref, *, layer):
    c = pl.program_id(1)
    last = pl.num_programs(1) - 1
    qc = HG_CHUNK

    @pl.when(c == 0)
    def _():
        st_ref[...] = jnp.zeros_like(st_ref)

    gam = gam_ref[...]
    ntile = qc // SUBLANE
    sub = lax.broadcasted_iota(jnp.int32, (SUBLANE, DK_C), 0)
    sub_levels = HG_MXU_LEVELS
    sub_upper = [(sub & (1 << l)) != 0 for l in range(sub_levels)]

    def tiles(x):
        return [x[i * SUBLANE:(i + 1) * SUBLANE, :] for i in range(ntile)]

    def gate_split(hd, rows):
        f, kk = _hg_gates(f_ref[hd, rows, :].astype(F32), _hg_lower_bound(clb_ref[hd], layer))
        g = jnp.log(jnp.maximum(f, HG_F_MIN)) * LOG2_E
        g1 = g.astype(BF16)
        g2 = (g - g1.astype(F32)).astype(BF16)
        return (f, kk), jnp.concatenate([g1, g2], axis=0)

    def scores(hd, rows, fk, sums):
        f, kk = fk
        qh = q_ref[hd, rows, :].astype(F32) * (DK_C ** -0.5)
        bcum = sums[0:qc]
        st = st_ref[hd]
        o = _dot_tr((qh * jnp.exp2(bcum)).astype(BF16), st.astype(BF16))
        qt, kt, ft, bt = tiles(qh), tiles(kk), tiles(f), tiles(bcum)
        prods = []
        for l in range(HG_LEVELS):
            if l == 0:
                xt = [jnp.where(sub_upper[0], qt[i] * ft[i], kt[i]) for i in range(ntile)]
            elif l < HG_MXU_LEVELS:
                dec = tiles(jnp.exp2(sums[l * qc:(l + 1) * qc]))
                xt = [jnp.where(sub_upper[l], qt[i], kt[i]) * dec[i] for i in range(ntile)]
            else:
                wt = 1 << (l - HG_MXU_LEVELS)
                xt = []
                for blk in range(0, ntile, 2 * wt):
                    ref = (blk + wt) * SUBLANE - 1
                    bref = jnp.broadcast_to(bcum[ref:ref + 1, :], (SUBLANE, DK_C))
                    xt += [kt[i] * jnp.exp2(bref - bt[i]) for i in range(blk, blk + wt)]
                    xt += [qt[i] * jnp.exp2(bt[i] - bref) for i in range(blk + wt, blk + 2 * wt)]
            x = jnp.concatenate(xt, axis=0).astype(BF16)
            half = (1 << l) // BF16_ROWS
            if half == 0:
                p = tiles(_dot_tr(x, x))
                prods.append({i: p[i] for i in range(ntile)})
            else:
                ups = [r for r in range(qc // BF16_ROWS) if (r // half) & 1]
                pu = _dot_tr(jnp.concatenate([x[r * BF16_ROWS:(r + 1) * BF16_ROWS, :] for r in ups], axis=0), x)
                tpr = BF16_ROWS // SUBLANE
                prods.append({r * tpr + k: pu[(n * tpr + k) * SUBLANE:(n * tpr + k + 1) * SUBLANE, :]
                              for n, r in enumerate(ups) for k in range(tpr)})
        return qh, st, o, prods

    def level_masks():
        masks = {}
        for i in range(ntile):
            lv = lvl_ref[i * SUBLANE:(i + 1) * SUBLANE, :]
            for l in range(HG_LEVELS):
                if l < sub_levels or (i >> (l - sub_levels)) & 1:
                    masks[i, l] = lv == l
        return masks

    def combine(hd, rows, kk, bcum, qh, st, o, prods, masks):
        arows = []
        for i in range(ntile):
            a = jnp.zeros((SUBLANE, qc), F32)
            for l in range(HG_LEVELS):
                if (i, l) in masks:
                    a = jnp.where(masks[i, l], prods[l][i], a)
            arows.append(a)
        amat = jnp.concatenate(arows, axis=0)
        vb = v_ref[hd, rows, :].astype(BF16)
        o = o + _dot(amat.astype(BF16), vb) + jnp.sum(qh * kk, axis=-1, keepdims=True) * vb.astype(F32)
        blast = bcum[qc - 1:qc, :]
        kdec = (kk * jnp.exp2(blast - bcum)).astype(BF16)
        st_ref[hd] = st * jnp.exp2(blast) + _dot_tl(vb, kdec)
        return o

    nchunk = HG_BLOCK // qc

    def body(idx, carry):
        hg = idx // nchunk
        rows = pl.ds(pl.multiple_of((idx % nchunk) * qc, qc), qc)
        heads = [hg * HG_UNROLL + k for k in range(HG_UNROLL)]
        gs = [gate_split(hd, rows) for hd in heads]
        sums = _dot(gam, jnp.concatenate([s for _, s in gs], axis=1))
        sums = [sums[:, k * DK_C:(k + 1) * DK_C] for k in range(HG_UNROLL)]
        sc = [scores(hd, rows, gs[k][0], sums[k]) for k, hd in enumerate(heads)]
        masks = level_masks()
        outs = [combine(hd, rows, gs[k][0][1], sums[k][0:qc], *sc[k], masks) for k, hd in enumerate(heads)]
        for k, hd in enumerate(heads):
            og_ref[hd, rows, :] = _hg_out(outs[k], g_ref[hd, rows, :].astype(F32), cnw_ref[hd]).astype(BF16)
        return carry

    lax.fori_loop(0, (H_C // HG_UNROLL) * nchunk, body, 0)

    @pl.when(c == last)
    def _():
        for hd in range(H_C):
            cs_ref[0, hd] = st_ref[hd].T


def _c_prompt(u, bsz, seq, w, layer):
    tb = HG_BLOCK
    assert seq % tb == 0 and tb % HG_CHUNK == 0 and (1 << HG_LEVELS) == HG_CHUNK
    nc = seq // tb
    m = bsz * seq
    depth = w["clb"].shape[1]

    def part(k):
        return pl.BlockSpec((H_C, tb, LANE), lambda b, c: (k, b * nc + c, 0))

    c2 = lambda b, c: (0, 0)
    c3 = lambda b, c: (0, 0, 0)
    og, cs = pl.pallas_call(
        functools.partial(_c_prompt_kernel, layer=layer), grid=(bsz, nc),
        in_specs=[part(0), part(1), part(2), part(3),
                  pl.BlockSpec((H_C, depth, DK_C), c3), pl.BlockSpec((H_C, 1, DV_C), c3),
                  pl.BlockSpec(w["gam"].shape, c2), pl.BlockSpec(w["lvl"].shape, c2)],
        out_specs=[pl.BlockSpec((H_C, tb, LANE), lambda b, c: (0, b * nc + c, 0)),
                   pl.BlockSpec((1, H_C, DK_C, DV_C), lambda b, c: (b, 0, 0, 0))],
        out_shape=[jax.ShapeDtypeStruct((H_C, m, DV_C), BF16),
                   jax.ShapeDtypeStruct((bsz, H_C, DK_C, DV_C), F32)],
        scratch_shapes=[pltpu.VMEM((H_C, DV_C, DK_C), F32)],
        compiler_params=_params("parallel", "arbitrary"),
        name="c_prompt",
    )(u, u, u, u, w["clb"], w["cnw"], w["gam"], w["lvl"])
    return og, cs


def _c_sample_head(hd, q_ref, f_ref, v_ref, g_ref, s_ref, clb_ref, cnw_ref, og_ref, so_ref, layer):
    bb = SAMPLE_BB
    lane = lax.broadcasted_iota(jnp.int32, (DK_C, LANE), 1)
    first_rows = lax.broadcasted_iota(jnp.int32, (LANE, DV_C), 0) < bb
    lb = _hg_lower_bound(clb_ref[hd], layer)
    f, kk = _hg_gates(f_ref[hd], lb)
    f_t = _pad_rows_t(f)
    k_t = _pad_rows_t(kk)
    qs = q_ref[hd] * (DK_C ** -0.5)
    v = v_ref[hd]
    vpad = jnp.where(first_rows, jnp.tile(v, (LANE // bb, 1)), 0.0).astype(BF16)
    orows = []
    for i in range(bb):
        fcol = jnp.broadcast_to(f_t[:, i:i + 1], (DK_C, DV_C))
        kv = _dot(jnp.where(lane == i, k_t, 0.0).astype(BF16), vpad)
        sn = fcol * s_ref[i, hd] + kv
        so_ref[i, hd] = sn
        qrow = jnp.broadcast_to(qs[i:i + 1, :], (2 * SUBLANE, DK_C)).astype(BF16)
        orows.append(_dot(qrow, sn.astype(BF16))[0:1, :])
    o = jnp.concatenate(orows, axis=0)
    og_ref[hd] = _hg_out(o, g_ref[hd], cnw_ref[hd])


def _c_sample_kernel(q_ref, f_ref, v_ref, g_ref, s_ref, clb_ref, cnw_ref, og_ref, so_ref, *, layer):
    def head(hd, carry):
        _c_sample_head(hd, q_ref, f_ref, v_ref, g_ref, s_ref, clb_ref, cnw_ref, og_ref, so_ref, layer)
        return carry

    lax.fori_loop(0, H_C, head, 0, unroll=SAMPLE_UNROLL)


def _c_sample(u, s_c, w, layer):
    bsz = s_c.shape[0]
    bb = SAMPLE_BB
    assert bsz % bb == 0
    depth = w["clb"].shape[1]

    def part(k):
        return pl.BlockSpec((H_C, bb, LANE), lambda i: (k, i, 0))

    c3 = lambda i: (0, 0, 0)
    og, so = pl.pallas_call(
        functools.partial(_c_sample_kernel, layer=layer), grid=(bsz // bb,),
        in_specs=[part(0), part(1), part(2), part(3),
                  pl.BlockSpec((bb, H_C, DK_C, DV_C), lambda i: (i, 0, 0, 0)),
                  pl.BlockSpec((H_C, depth, DK_C), c3), pl.BlockSpec((H_C, 1, DV_C), c3)],
        out_specs=[pl.BlockSpec((H_C, bb, LANE), lambda i: (0, i, 0)),
                   pl.BlockSpec((bb, H_C, DK_C, DV_C), lambda i: (i, 0, 0, 0))],
        out_shape=[jax.ShapeDtypeStruct((H_C, bsz, DV_C), F32), jax.ShapeDtypeStruct(s_c.shape, F32)],
        compiler_params=_params("parallel"),
        name="c_sample",
    )(u, u, u, u, s_c, w["clb"], w["cnw"])
    return og, so


def _out_proj_c_sample_kernel(mix_ref, h_ref, p_ref, wo_ref, wg_ref, wp_ref, nf_ref,
                              q_ref, f_ref, v_ref, g_ref, s_ref, clb_ref, cnw_ref,
                              o_ref, og_ref, so_ref, *, final, layer):
    for hd in range(q_ref.shape[0]):
        _c_sample_head(hd, q_ref, f_ref, v_ref, g_ref, s_ref, clb_ref, cnw_ref, og_ref, so_ref, layer)
    _out_proj_kernel(mix_ref, h_ref, p_ref, wo_ref, wg_ref, wp_ref, nf_ref, o_ref, head_major=True, final=final)


def _out_proj_c_sample(mix, h, p, layer, wo, wg, wp, nf, u_s, s_c, w, *, final):
    m, d = h.shape
    bsz = s_c.shape[0]
    bb, split = SAMPLE_BB, FUSED_HEAD_SPLIT
    steps = (bsz // bb) * split
    hp = H_C // split
    tm = m // steps
    assert bsz % bb == 0 and H_C % split == 0 and m % steps == 0 and tm % BF16_ROWS == 0
    depth = w["clb"].shape[1]
    const = lambda i: (0, 0)

    def part(k):
        return pl.BlockSpec((hp, bb, LANE), lambda i: (k * split + i % split, i // split, 0))

    state = pl.BlockSpec((bb, hp, DK_C, DV_C), lambda i: (i // split, i % split, 0, 0))
    hrow = lambda i: (i % split, 0, 0)
    hout, og, so = pl.pallas_call(
        functools.partial(_out_proj_c_sample_kernel, final=final, layer=layer),
        grid=(steps,),
        in_specs=[pl.BlockSpec((mix.shape[0], tm, LANE), lambda i: (0, i, 0)),
                  pl.BlockSpec((tm, d), lambda i: (i, 0)),
                  pl.BlockSpec((None, tm, p.shape[2]), lambda i: (layer, i, 0)),
                  pl.BlockSpec(wo.shape, const), pl.BlockSpec(wg.shape, const),
                  pl.BlockSpec(wp.shape, const), pl.BlockSpec((1, d), const),
                  part(0), part(1), part(2), part(3), state,
                  pl.BlockSpec((hp, depth, DK_C), hrow), pl.BlockSpec((hp, 1, DV_C), hrow)],
        out_specs=[pl.BlockSpec((tm, d), lambda i: (i, 0)),
                   pl.BlockSpec((hp, bb, LANE), lambda i: (i % split, i // split, 0)),
                   state],
        out_shape=[jax.ShapeDtypeStruct((m, d), F32),
                   jax.ShapeDtypeStruct((H_C, bsz, DV_C), F32), jax.ShapeDtypeStruct(s_c.shape, F32)],
        compiler_params=_params("arbitrary"),
        name="out_proj_c_sample",
    )(mix, h, p, wo, wg, wp, nf.reshape(1, d), u_s, u_s, u_s, u_s, s_c, w["clb"], w["cnw"])
    return hout, og, so


def _row(v, width=None):
    v = v.astype(F32).reshape(1, -1)
    if width is not None and v.shape[1] < width:
        v = jnp.pad(v, ((0, 0), (0, width - v.shape[1])))
    return v


def kernel(x_prompt, x_sample, p_prompt, p_sample, state_a_h, state_a_conv, state_b_ssm, state_b_conv, state_c,
           norm_w, norm_f, ab_w_in, a_conv_w, a_conv_b, a_w_r, a_b_r, a_w_i, a_b_i, a_lam, b_conv_w, b_conv_b,
           b_dt_bias, b_a_log, b_d, b_norm_w, ab_w_out, c_w_in, c_lb, c_norm_w, c_w_out, ple_proj, ple_gate):
    depth = norm_w.shape[0]
    bp, seq, _ = x_prompt.shape
    bs = x_sample.shape[0]
    hp = x_prompt.reshape(bp * seq, D_MODEL)
    hs = x_sample.reshape(bs, D_MODEL)
    pp = p_prompt.reshape(depth, bp * seq, D_PLE)
    ps = p_sample.reshape(depth, bs, D_PLE)
    gam, lvl = _hg_gamma(), _hg_level_table()
    clb = c_lb.astype(F32).reshape(depth, H_C, DK_C).transpose(1, 0, 2)

    ah_p, ac_p, bs_p, bc_p, c_p = [], [], [], [], []
    ah_s, ac_s, bs_s, bc_s, c_s = [], [], [], [], []
    for i in range(depth):
        j = i // 2
        final = i == depth - 1
        wg = (0.5 * ple_gate[i]).astype(BF16)
        wp = (0.5 * ple_proj[i]).astype(BF16)
        if i % 2 == 0:
            col_scale = jnp.concatenate([jnp.ones((D_A,), F32), jnp.full((D_A + D_B,), 0.5, F32),
                                         jnp.ones((CONV_DIM_B,), F32)])
            w_ab_t = ab_w_in[j].T
            w_ab = (w_ab_t[:AB_MAIN] * col_scale[:, None]).astype(BF16)
            w_dt = jnp.pad(w_ab_t[AB_MAIN:], ((0, LANE - H_B), (0, 0))).astype(BF16)
            wo = ab_w_out[j].astype(BF16)
            w = dict(acw=a_conv_w[j].astype(F32), acb=_row(a_conv_b[j]),
                     wr=(0.5 * a_w_r[j]).astype(BF16), br=0.5 * _row(a_b_r[j]),
                     wi=a_w_i[j].astype(BF16), bi=_row(a_b_i[j]),
                     lam=_row(a_lam[j]), bcw=0.5 * b_conv_w[j].astype(F32), bcb=0.5 * _row(b_conv_b[j]),
                     dtb=_row(b_dt_bias[j], LANE), alog=_row(b_a_log[j], LANE),
                     dexp=_row(jnp.repeat(b_d[j], HD_B)), bnw=_row(b_norm_w[j]))
            us, dtrs = _in_proj(hs, norm_w[i], w_ab, w_dt, w_t=True)
            aout, s1, s2, s4, xact, dts = _ab_sample_rows(us, dtrs, state_a_h[j], state_a_conv[j], state_b_conv[j], w)
            ah_s.append(s1); ac_s.append(s2); bc_s.append(s4)
            u, dtr = _in_proj(hp, norm_w[i], w_ab, w_dt, w_t=True)
            grid_steps = bp * (seq // (AB_CHUNKS * SSD_CHUNK))
            sample_blocks = bs // SAMPLE_BB
            split = grid_steps // sample_blocks if bs % SAMPLE_BB == 0 and grid_steps % sample_blocks == 0 else 0
            if split and H_B % split == 0 and (H_B // split) % 2 == 0 and (H_B // G_B) % (H_B // split) == 0:
                hp, s1, s2, s3, s4, bs_new, y = _ab_prompt(u, dtr, hp, pp, i, wo, wg, wp, norm_f, bp, seq, w,
                                                          final=final, sample=(state_b_ssm[j], xact, dts))
            else:
                hp, s1, s2, s3, s4 = _ab_prompt(u, dtr, hp, pp, i, wo, wg, wp, norm_f, bp, seq, w, final=final)
                bs_new, y = _ab_sample_state(state_b_ssm[j], xact, dts, w)
            ah_p.append(s1); ac_p.append(s2); bs_p.append(s3); bc_p.append(s4)
            bs_s.append(bs_new)
            mix = jnp.concatenate([aout, _ab_sample_finish(y, xact, us, w)], axis=-1)
            hs = _out_proj(mix, hs, ps, i, wo, wg, wp, norm_f, head_major=False, final=final)
        else:
            col_scale = jnp.concatenate([jnp.ones((HK_C,), F32), jnp.full((HK_C,), 0.5, F32),
                                         jnp.ones((D_C,), F32), jnp.full((D_C,), 0.5, F32)])
            w_in = (c_w_in[j] * col_scale).astype(BF16)
            wo = c_w_out[j].astype(BF16)
            w = dict(clb=clb, cnw=c_norm_w[j].astype(F32).reshape(H_C, 1, DV_C), gam=gam, lvl=lvl)
            u = _in_proj(hp, norm_w[i], w_in, head_major=True, head_major_dtype=BF16)
            og, s1 = _c_prompt(u, bp, seq, w, i)
            c_p.append(s1)
            u = _in_proj(hs, norm_w[i], w_in, head_major=True)
            steps = (bs // SAMPLE_BB) * FUSED_HEAD_SPLIT
            if bs % SAMPLE_BB == 0 and (bp * seq) % (steps * BF16_ROWS) == 0 and (bp * seq) // steps <= OUT_PROJ_TM:
                hp, og, s1 = _out_proj_c_sample(og, hp, pp, i, wo, wg, wp, norm_f, u, state_c[j], w, final=final)
            else:
                hp = _out_proj(og, hp, pp, i, wo, wg, wp, norm_f, head_major=True, final=final)
                og, s1 = _c_sample(u, state_c[j], w, i)
            c_s.append(s1)
            hs = _out_proj(og, hs, ps, i, wo, wg, wp, norm_f, head_major=True, final=final)
    return (hp.reshape(bp, seq, D_MODEL), hs.reshape(bs, 1, D_MODEL),
            jnp.stack(ah_p), jnp.stack(ac_p), jnp.stack(bs_p), jnp.stack(bc_p), jnp.stack(c_p),
            jnp.stack(ah_s), jnp.stack(ac_s), jnp.stack(bs_s), jnp.stack(bc_s), jnp.stack(c_s))
```

```python
import functools

import jax
import jax.numpy as jnp
from jax import lax
from jax.experimental import pallas as pl
from jax.experimental.pallas import tpu as pltpu

F32 = jnp.float32
BF16 = jnp.bfloat16

D_MODEL = 1024
D_PLE = 256
EPS = 1e-6
CONV_W = 4
D_A = D_MODEL
A_BLOCKS = 8
A_BLK = D_A // A_BLOCKS
LRU_C = 8.0
D_B = D_MODEL
HD_B = 64
H_B = D_B // HD_B
N_B = 128
G_B = 2
CONV_DIM_B = D_B + 2 * G_B * N_B
D_C = 2 * D_MODEL
H_C = 16
DK_C = 128
DV_C = D_C // H_C
HK_C = H_C * DK_C
AB_MAIN = 2 * D_A + D_B + CONV_DIM_B
IN_C = 2 * HK_C + 2 * D_C

LANE = 128
SUBLANE = 8
BF16_ROWS = 16
LOG2_E = 1.4426950408889634
VMEM_LIMIT = 56 * 1024 * 1024

PROJ_TM = 2048
PROJ_VMEM_BUDGET = 46 * 1024 * 1024
HOST_VMEM_EXTRA = 6 * 1024 * 1024
OUT_PROJ_TM = 1024
SSD_CHUNK = 128
AB_CHUNKS = 2
HG_CHUNK = 64
HG_LEVELS = 6
HG_MXU_LEVELS = 3
HG_F_MIN = 1e-30
HG_BLOCK = 512
HG_UNROLL = 16
SAMPLE_BB = 8
SAMPLE_UNROLL = 4
FUSED_HEAD_SPLIT = 2

_DN_TR = (((1,), (1,)), ((), ()))
_DN_TL = (((0,), (0,)), ((), ()))


def _dot(a, b):
    return jnp.dot(a, b, preferred_element_type=F32)


def _dot_tr(a, b):
    return lax.dot_general(a, b, _DN_TR, preferred_element_type=F32)


def _dot_tl(a, b):
    return lax.dot_general(a, b, _DN_TL, preferred_element_type=F32)


def _silu_half(x_half):
    return x_half + x_half * jnp.tanh(x_half)


def _rmsnorm(x, w):
    return x * lax.rsqrt(jnp.mean(x * x, axis=-1, keepdims=True) + EPS) * w


def _params(*sem, vmem_limit=VMEM_LIMIT):
    return pltpu.CompilerParams(dimension_semantics=sem, vmem_limit_bytes=vmem_limit)


def _in_proj_kernel(x_ref, nw_ref, w_ref, *rest, has_extra, head_major, w_t, fused):
    rest = list(rest)
    wx_ref = rest.pop(0) if has_extra else None
    sample_in = [rest.pop(0) for _ in range(6)] if fused else None
    o_ref = rest.pop(0)
    ox_ref = rest.pop(0) if has_extra else None
    sample_out = [rest.pop(0) for _ in range(2)] if fused else None
    xn_ref, = rest
    dot = _dot_tr if w_t else _dot

    @pl.when(pl.program_id(1) == 0)
    def _():
        xn_ref[...] = _rmsnorm(x_ref[...], nw_ref[...]).astype(BF16)
        if has_extra:
            ox_ref[...] = dot(xn_ref[...], wx_ref[...])

    if fused:
        _ab_sample_pairs(*sample_in, *sample_out, 0)
    acc = dot(xn_ref[...], w_ref[...])
    if head_major:
        for k in range(acc.shape[1] // LANE):
            o_ref[k] = acc[:, k * LANE:(k + 1) * LANE].astype(o_ref.dtype)
    else:
        o_ref[...] = acc


def _sample_ssd_split(sbsz, steps):
    if sbsz % SAMPLE_BB or steps % (sbsz // SAMPLE_BB):
        return 0
    split = steps // (sbsz // SAMPLE_BB)
    if H_B % split or (H_B // split) % 2 or (H_B // G_B) % (H_B // split):
        return 0
    return split


def _sample_ssd_operands(sample, alog, split, step):
    s_bs, xact, dts = sample
    sbsz, bb = s_bs.shape[0], SAMPLE_BB
    hps = H_B // split
    regroup = lambda v: jnp.pad(v[:, :H_B].reshape(-1, split, hps).transpose(1, 0, 2),
                                ((0, 0), (0, 0), (0, LANE - hps)))
    blk = lambda *g: step(*g) // split
    grp = lambda *g: step(*g) % split
    bc_group = lambda *g: (grp(*g) * hps) // (H_B // G_B)
    state = pl.BlockSpec((bb, hps, HD_B, N_B), lambda *g: (blk(*g), grp(*g), 0, 0))
    xs = pl.BlockSpec((bb, hps * HD_B), lambda *g: (blk(*g), grp(*g)))
    in_specs = [state, xs,
                pl.BlockSpec((bb, N_B), lambda *g: (blk(*g), D_B // N_B + bc_group(*g))),
                pl.BlockSpec((bb, N_B), lambda *g: (blk(*g), D_B // N_B + G_B + bc_group(*g))),
                pl.BlockSpec((None, bb, LANE), lambda *g: (grp(*g), blk(*g), 0)),
                pl.BlockSpec((None, 1, LANE), lambda *g: (grp(*g), 0, 0))]
    args = [s_bs, xact, xact, xact, regroup(dts), regroup(alog)]
    out_shape = [jax.ShapeDtypeStruct(s_bs.shape, F32), jax.ShapeDtypeStruct((sbsz, D_B), F32)]
    return in_specs, args, out_shape, [state, xs]


def _in_proj(x, nw, w, w_extra=None, *, head_major=False, w_t=False, head_major_dtype=F32, sample=None, alog=None):
    m, k = x.shape
    extra_cols = w_extra is not None
    n = w.shape[0 if w_t else 1]
    tm = min(m, PROJ_TM)
    out_bytes = jnp.dtype(head_major_dtype if head_major else F32).itemsize

    def split_of(c):
        return _sample_ssd_split(sample[0].shape[0], (m // tm) * (n // c)) if sample is not None else 0

    def vmem(c):
        extra = 2 * k * LANE * 2 + 2 * tm * LANE * 4 if extra_cols else 0
        ssd = 4 * SAMPLE_BB * (H_B // split_of(c)) * HD_B * N_B * 4 if split_of(c) else 0
        return 2 * tm * k * 4 + tm * k * 2 + 2 * k * c * 2 + 2 * tm * c * out_bytes + extra + ssd

    headroom = HOST_VMEM_EXTRA if sample is not None else 0
    fits = [c for c in range(n, 0, -LANE) if n % c == 0 and vmem(c) <= PROJ_VMEM_BUDGET + headroom]
    tn = next((c for c in fits if split_of(c)), fits[0])
    split = split_of(tn)
    assert m % tm == 0 and n % tn == 0 and tn % LANE == 0
    grid = (m // tm, n // tn)
    wspec = (lambda cols, idx: pl.BlockSpec((cols, k), lambda i, j: (idx(j), 0))) if w_t else \
            (lambda cols, idx: pl.BlockSpec((k, cols), lambda i, j: (0, idx(j))))
    in_specs = [pl.BlockSpec((tm, k), lambda i, j: (i, 0)),
                pl.BlockSpec((1, k), lambda i, j: (0, 0)),
                wspec(tn, lambda j: j)]
    args = [x, nw.reshape(1, k), w]
    if head_major:
        out_shape = [jax.ShapeDtypeStruct((n // LANE, m, LANE), head_major_dtype)]
        out_specs = [pl.BlockSpec((tn // LANE, tm, LANE), lambda i, j: (j, i, 0))]
    else:
        out_shape = [jax.ShapeDtypeStruct((m, n), F32)]
        out_specs = [pl.BlockSpec((tm, tn), lambda i, j: (i, j))]
    if extra_cols:
        in_specs.append(wspec(LANE, lambda j: 0))
        args.append(w_extra)
        out_shape.append(jax.ShapeDtypeStruct((m, LANE), F32))
        out_specs.append(pl.BlockSpec((tm, LANE), lambda i, j: (i, 0)))
    if split:
        s_in, s_args, s_shape, s_out = _sample_ssd_operands(sample, alog, split, lambda i, j: i * grid[1] + j)
        in_specs += s_in
        args += s_args
        out_shape += s_shape
        out_specs += s_out
    outs = pl.pallas_call(
        functools.partial(_in_proj_kernel, has_extra=extra_cols, head_major=head_major, w_t=w_t, fused=bool(split)),
        grid=grid, in_specs=in_specs, out_specs=out_specs, out_shape=out_shape,
        scratch_shapes=[pltpu.VMEM((tm, k), BF16)],
        compiler_params=_params("arbitrary" if split else "parallel", "arbitrary", vmem_limit=VMEM_LIMIT + headroom),
        name="in_proj",
    )(*args)
    if sample is not None and not split:
        outs = list(outs) + list(_ab_sample_state(*sample, dict(alog=alog)))
    return outs if len(outs) > 1 else outs[0]


def _residual_ple(h, p_ref, wg_ref, wp_ref, nf_ref, final):
    gate_t = jnp.tanh(_dot(h.astype(BF16), wg_ref[...]))
    pe_half = _dot(p_ref[...].astype(BF16), wp_ref[...])
    h = h + pe_half + pe_half * gate_t
    return _rmsnorm(h, nf_ref[...]) if final else h


def _out_proj_kernel(mix_ref, h_ref, p_ref, wo_ref, wg_ref, wp_ref, nf_ref, o_ref, *, head_major, final):
    if head_major:
        mix = jnp.concatenate([mix_ref[k] for k in range(mix_ref.shape[0])], axis=-1)
    else:
        mix = mix_ref[...]
    h = h_ref[...] + _dot(mix.astype(BF16), wo_ref[...])
    o_ref[...] = _residual_ple(h, p_ref, wg_ref, wp_ref, nf_ref, final)


def _out_proj(mix, h, p, layer, wo, wg, wp, nf, *, head_major, final):
    m, d = h.shape
    tm = min(m, OUT_PROJ_TM)
    assert m % tm == 0
    if head_major:
        mix_spec = pl.BlockSpec((mix.shape[0], tm, LANE), lambda i: (0, i, 0))
    else:
        mix_spec = pl.BlockSpec((tm, mix.shape[1]), lambda i: (i, 0))
    const = lambda i: (0, 0)
    return pl.pallas_call(
        functools.partial(_out_proj_kernel, head_major=head_major, final=final),
        grid=(m // tm,),
        in_specs=[mix_spec,
                  pl.BlockSpec((tm, d), lambda i: (i, 0)),
                  pl.BlockSpec((None, tm, p.shape[2]), lambda i: (layer, i, 0)),
                  pl.BlockSpec(wo.shape, const), pl.BlockSpec(wg.shape, const),
                  pl.BlockSpec(wp.shape, const), pl.BlockSpec((1, d), const)],
        out_specs=pl.BlockSpec((tm, d), lambda i: (i, 0)),
        out_shape=jax.ShapeDtypeStruct((m, d), F32),
        compiler_params=_params("parallel"),
        name="out_proj",
    )(mix, h, p, wo, wg, wp, nf.reshape(1, d))


def _lru_gates(xc, wr_ref, br_ref, wi_ref, bi_ref, lam_ref):
    xcb = xc.astype(BF16)
    r_parts, i_parts = [], []
    for k in range(A_BLOCKS):
        xk = xcb[:, k * A_BLK:(k + 1) * A_BLK]
        r_parts.append(_dot(xk, wr_ref[k]))
        i_parts.append(_dot(xk, wi_ref[k]))
    tr = jnp.tanh(jnp.concatenate(r_parts, axis=-1) + br_ref[...])
    gi = jax.nn.sigmoid(jnp.concatenate(i_parts, axis=-1) + bi_ref[...])
    half_rate = (-0.5 * LRU_C) * jax.nn.softplus(-lam_ref[...])
    log_a = half_rate + half_rate * tr
    a = jnp.exp(log_a)
    v = jnp.tanh(-log_a) * (a * a + 1.0)
    u = jnp.where(v > 0.0, v * lax.rsqrt(v), 0.0) * (gi * xc)
    return a, u


def _group_norm_gate(y, bx, z_half, dexp_ref, bnw_ref):
    y = (y + dexp_ref[...] * bx) * _silu_half(z_half)
    gw = D_B // G_B
    parts = []
    for g in range(G_B):
        yg = y[:, g * gw:(g + 1) * gw]
        parts.append(yg * lax.rsqrt(jnp.mean(yg * yg, axis=-1, keepdims=True) + EPS))
    return jnp.concatenate(parts, axis=-1) * bnw_ref[...]


def _ab_prompt_kernel(ax_ref, ag_ref, z_ref, xbc_ref, dt_ref,
                      acw_ref, acb_ref, wr_ref, br_ref, wi_ref, bi_ref, lam_ref,
                      bcw_ref, bcb_ref, dtb_ref, alog_ref, dexp_ref, bnw_ref,
                      hin_ref, p_ref, wo_ref, wg_ref, wp_ref, nf_ref,
                      hout_ref, *state_refs, final):
    for k in range(AB_CHUNKS):
        r = pl.ds(k * SSD_CHUNK, SSD_CHUNK)
        _ab_prompt_chunk(ax_ref.at[r], ag_ref.at[r], z_ref.at[r], xbc_ref.at[r], dt_ref.at[r],
                         acw_ref, acb_ref, wr_ref, br_ref, wi_ref, bi_ref, lam_ref,
                         bcw_ref, bcb_ref, dtb_ref, alog_ref, dexp_ref, bnw_ref,
                         hin_ref.at[r], p_ref.at[r], wo_ref, wg_ref, wp_ref, nf_ref,
                         hout_ref.at[r], *state_refs, final=final, first=k == 0, last=k == AB_CHUNKS - 1)


def _ab_prompt_chunk(ax_ref, ag_ref, z_ref, xbc_ref, dt_ref,
                     acw_ref, acb_ref, wr_ref, br_ref, wi_ref, bi_ref, lam_ref,
                     bcw_ref, bcb_ref, dtb_ref, alog_ref, dexp_ref, bnw_ref,
                     hin_ref, p_ref, wo_ref, wg_ref, wp_ref, nf_ref,
                     hout_ref, ah_ref, ac_ref, bs_ref, bc_ref,
                     xpa_ref, xpb_ref, h_ref, s_ref, *, final, first, last):
    c = pl.program_id(1)
    t = SSD_CHUNK
    ntile = t // SUBLANE

    if first:
        @pl.when(c == 0)
        def _():
            xpa_ref[...] = jnp.zeros_like(xpa_ref)
            xpb_ref[...] = jnp.zeros_like(xpb_ref)
            h_ref[...] = jnp.zeros_like(h_ref)
            s_ref[...] = jnp.zeros_like(s_ref)

    def tiles(x):
        return [x[i * SUBLANE:(i + 1) * SUBLANE, :] for i in range(ntile)]

    def conv(x, tail_ref, w_ref, b_ref):
        sub = lax.broadcasted_iota(jnp.int32, (SUBLANE, x.shape[1]), 0)
        xt = [tail_ref[...]] + tiles(x)
        taps = [jnp.broadcast_to(w_ref[k:k + 1, :], (SUBLANE, x.shape[1])) for k in range(CONV_W)]
        bias = jnp.broadcast_to(b_ref[...], (SUBLANE, x.shape[1]))
        acc = [bias + taps[CONV_W - 1] * xt[i + 1] for i in range(ntile)]
        for s in range(1, CONV_W):
            wk = taps[CONV_W - 1 - s]
            for i in range(ntile):
                merged = jnp.where(sub >= SUBLANE - s, xt[i], xt[i + 1])
                acc[i] = acc[i] + wk * pltpu.roll(merged, s, 0)
        tail_ref[...] = xt[ntile]
        return jnp.concatenate(acc, axis=0)

    ax = ax_ref[...]
    xc = conv(ax, xpa_ref, acw_ref, acb_ref)
    a, u = _lru_gates(xc, wr_ref, br_ref, wi_ref, bi_ref, lam_ref)
    sub = lax.broadcasted_iota(jnp.int32, (SUBLANE, D_A), 0)
    at, ut = tiles(a), tiles(u)
    step = 1
    while step < SUBLANE:
        m = sub >= step
        for i in range(ntile):
            ut[i] = jnp.where(m, at[i] * pltpu.roll(ut[i], step, 0) + ut[i], ut[i])
            at[i] = jnp.where(m, at[i] * pltpu.roll(at[i], step, 0), at[i])
        step *= 2
    carry = h_ref[0:1, :]
    hs = []
    for i in range(ntile):
        hs.append(ut[i] + at[i] * carry)
        carry = hs[i][SUBLANE - 1:SUBLANE, :]
    h = jnp.concatenate(hs, axis=0)
    h_ref[0:1, :] = carry
    a_out = (h * _silu_half(ag_ref[...])).astype(BF16)

    xb = xbc_ref[...]
    xbc = _silu_half(conv(xb, xpb_ref, bcw_ref, bcb_ref))
    bx = xbc[:, 0:D_B]
    bxb = bx.astype(BF16)
    dt = jax.nn.softplus(dt_ref[...] + dtb_ref[...])
    adt = dt * (-LOG2_E * jnp.exp(alog_ref[...]))
    ti = lax.broadcasted_iota(jnp.int32, (t, t), 0)
    si = lax.broadcasted_iota(jnp.int32, (t, t), 1)
    causal = ti >= si
    acs = jnp.dot(causal.astype(F32), adt, preferred_element_type=F32,
                  precision=lax.Precision.HIGHEST)
    a_last = acs[t - 1:t, :]
    wq = jnp.exp2(a_last - acs) * dt
    eacs = jnp.exp2(acs)
    ealast = jnp.exp2(a_last)
    acs_t = acs.T
    dt_t = dt.T
    lane = lax.broadcasted_iota(jnp.int32, (t, LANE), 1)
    rowi = lax.broadcasted_iota(jnp.int32, (LANE, N_B), 0)
    hpg = H_B // G_B
    ys = []
    cb = None
    for j in range(H_B // 2):
        g = (2 * j) // hpg
        bg = xbc[:, D_B + g * N_B:D_B + (g + 1) * N_B]
        cg = xbc[:, D_B + G_B * N_B + g * N_B:D_B + G_B * N_B + (g + 1) * N_B]
        if (2 * j) % hpg == 0:
            cb = _dot_tr(cg.astype(BF16), bg.astype(BF16))
        xpair = bxb[:, j * LANE:(j + 1) * LANE]
        sp = s_ref[j]
        spb = sp.astype(BF16)
        y_h, up_h = [], []
        for hh in range(2):
            hd = 2 * j + hh
            seg = jnp.broadcast_to(acs[:, hd:hd + 1], (t, t)) - jnp.broadcast_to(acs_t[hd:hd + 1, :], (t, t))
            lmat = jnp.exp2(jnp.where(causal, seg, -1e30))
            mmat = (cb * lmat * jnp.broadcast_to(dt_t[hd:hd + 1, :], (t, t))).astype(BF16)
            ec = (jnp.broadcast_to(eacs[:, hd:hd + 1], (t, N_B)) * cg).astype(BF16)
            y_h.append(_dot(mmat, xpair) + _dot_tr(ec, spb))
            bw = (bg * jnp.broadcast_to(wq[:, hd:hd + 1], (t, N_B))).astype(BF16)
            up_h.append(_dot_tl(xpair, bw))
        ys.append(jnp.where(lane < HD_B, y_h[0], y_h[1]))
        dec = jnp.where(rowi < HD_B,
                        jnp.broadcast_to(ealast[:, 2 * j:2 * j + 1], (LANE, N_B)),
                        jnp.broadcast_to(ealast[:, 2 * j + 1:2 * j + 2], (LANE, N_B)))
        s_ref[j] = dec * sp + jnp.where(rowi < HD_B, up_h[0], up_h[1])
    y = jnp.concatenate(ys, axis=-1)
    b_out = _group_norm_gate(y, bx, z_ref[...], dexp_ref, bnw_ref).astype(BF16)

    hres = hin_ref[...] + _dot(a_out, wo_ref[0:D_A, :]) + _dot(b_out, wo_ref[D_A:D_A + D_B, :])
    hout_ref[...] = _residual_ple(hres, p_ref, wg_ref, wp_ref, nf_ref, final)

    if last:
        @pl.when(c == pl.num_programs(1) - 1)
        def _():
            ah_ref[0] = h[t - 1:t, :]
            ac_ref[0] = ax[t - (CONV_W - 1):t, :]
            bc_ref[0] = xb[t - (CONV_W - 1):t, :]
            for j in range(H_B // 2):
                sj = s_ref[j]
                bs_ref[0, 2 * j] = sj[0:HD_B, :]
                bs_ref[0, 2 * j + 1] = sj[HD_B:2 * HD_B, :]


def _ab_prompt(u, dtr, h, p, layer, wo, wg, wp, nf, bsz, seq, w, *, final):
    t = AB_CHUNKS * SSD_CHUNK
    assert seq % t == 0
    nc = seq // t
    m = bsz * seq
    rows = lambda b, c: b * nc + c
    cvec = lambda b, c: (0, 0)
    c3 = lambda b, c: (0, 0, 0)
    in_specs = [
        pl.BlockSpec((t, D_A), lambda b, c: (rows(b, c), 0)),
        pl.BlockSpec((t, D_A), lambda b, c: (rows(b, c), 1)),
        pl.BlockSpec((t, D_B), lambda b, c: (rows(b, c), 2)),
        pl.BlockSpec((t, CONV_DIM_B), lambda b, c: (rows(b, c), 2)),
        pl.BlockSpec((t, LANE), lambda b, c: (rows(b, c), 0)),
        pl.BlockSpec((CONV_W, D_A), cvec), pl.BlockSpec((1, D_A), cvec),
        pl.BlockSpec((A_BLOCKS, A_BLK, A_BLK), c3), pl.BlockSpec((1, D_A), cvec),
        pl.BlockSpec((A_BLOCKS, A_BLK, A_BLK), c3), pl.BlockSpec((1, D_A), cvec),
        pl.BlockSpec((1, D_A), cvec),
        pl.BlockSpec((CONV_W, CONV_DIM_B), cvec), pl.BlockSpec((1, CONV_DIM_B), cvec),
        pl.BlockSpec((1, LANE), cvec), pl.BlockSpec((1, LANE), cvec),
        pl.BlockSpec((1, D_B), cvec), pl.BlockSpec((1, D_B), cvec),
        pl.BlockSpec((t, D_MODEL), lambda b, c: (rows(b, c), 0)),
        pl.BlockSpec((None, t, p.shape[2]), lambda b, c: (layer, rows(b, c), 0)),
        pl.BlockSpec(wo.shape, cvec), pl.BlockSpec(wg.shape, cvec), pl.BlockSpec(wp.shape, cvec),
        pl.BlockSpec((1, D_MODEL), cvec),
    ]
    out_shape = [
        jax.ShapeDtypeStruct((m, D_MODEL), F32),
        jax.ShapeDtypeStruct((bsz, 1, D_A), F32),
        jax.ShapeDtypeStruct((bsz, CONV_W - 1, D_A), F32),
        jax.ShapeDtypeStruct((bsz, H_B, HD_B, N_B), F32),
        jax.ShapeDtypeStruct((bsz, CONV_W - 1, CONV_DIM_B), F32),
    ]
    out_specs = [
        pl.BlockSpec((t, D_MODEL), lambda b, c: (rows(b, c), 0)),
        pl.BlockSpec((1, 1, D_A), lambda b, c: (b, 0, 0)),
        pl.BlockSpec((1, CONV_W - 1, D_A), lambda b, c: (b, 0, 0)),
        pl.BlockSpec((1, H_B, HD_B, N_B), lambda b, c: (b, 0, 0, 0)),
        pl.BlockSpec((1, CONV_W - 1, CONV_DIM_B), lambda b, c: (b, 0, 0)),
    ]
    hout, ah, ac, bs, bc = pl.pallas_call(
        functools.partial(_ab_prompt_kernel, final=final),
        grid=(bsz, nc), in_specs=in_specs, out_specs=out_specs, out_shape=out_shape,
        scratch_shapes=[pltpu.VMEM((SUBLANE, D_A), F32), pltpu.VMEM((SUBLANE, CONV_DIM_B), F32),
                        pltpu.VMEM((SUBLANE, D_A), F32), pltpu.VMEM((H_B // 2, 2 * HD_B, N_B), F32)],
        compiler_params=_params("parallel", "arbitrary"),
        name="ab_prompt",
    )(u, u, u, u, dtr, w["acw"], w["acb"], w["wr"], w["br"], w["wi"], w["bi"], w["lam"],
      w["bcw"], w["bcb"], w["dtb"], w["alog"], w["dexp"], w["bnw"],
      h, p, wo, wg, wp, nf.reshape(1, D_MODEL))
    return hout, ah.reshape(bsz, D_A), ac, bs, bc


def _ab_sample_rows_kernel(ax_ref, ag_ref, xbc_ref, dt_ref, sah_ref, sac_ref, sbc_ref,
                           acw_ref, acb_ref, wr_ref, br_ref, wi_ref, bi_ref, lam_ref,
                           bcw_ref, bcb_ref, dtb_ref,
                           aout_ref, ah_ref, ac_ref, bc_ref, xact_ref, dts_ref):
    def conv1(x, buf_ref, w_ref, b_ref, nbuf_ref, width):
        y = b_ref[...] + w_ref[CONV_W - 1:CONV_W, :] * x
        for k in range(CONV_W - 1):
            y = y + w_ref[k:k + 1, :] * buf_ref[:, k * width:(k + 1) * width]
        for k in range(CONV_W - 2):
            nbuf_ref[:, k * width:(k + 1) * width] = buf_ref[:, (k + 1) * width:(k + 2) * width]
        nbuf_ref[:, (CONV_W - 2) * width:(CONV_W - 1) * width] = x
        return y

    xc = conv1(ax_ref[...], sac_ref, acw_ref, acb_ref, ac_ref, D_A)
    a, u = _lru_gates(xc, wr_ref, br_ref, wi_ref, bi_ref, lam_ref)
    h = a * sah_ref[...] + u
    ah_ref[...] = h
    aout_ref[...] = h * _silu_half(ag_ref[...])
    xact_ref[...] = _silu_half(conv1(xbc_ref[...], sbc_ref, bcw_ref, bcb_ref, bc_ref, CONV_DIM_B))
    dts_ref[...] = jax.nn.softplus(dt_ref[...] + dtb_ref[...])


def _pad_rows_t(x):
    pad = jnp.zeros((LANE - x.shape[0], x.shape[1]), F32)
    return jnp.concatenate([x, pad], axis=0).T


def _ab_sample_pairs(s_ref, bx_ref, b_ref, c_ref, dts_ref, alog_ref, so_ref, y_ref, head0):
    bb = SAMPLE_BB
    dts = dts_ref[...]
    dec_t = _pad_rows_t(jnp.exp(dts * (-jnp.exp(alog_ref[...]))))
    dts_t = _pad_rows_t(dts)
    for j in range(bx_ref.shape[1] // LANE):
        h0 = head0 + 2 * j
        xt = _pad_rows_t(bx_ref[:, j * LANE:(j + 1) * LANE])
        dtp = jnp.concatenate([jnp.broadcast_to(dts_t[h0:h0 + 1, :], (HD_B, LANE)),
                               jnp.broadcast_to(dts_t[h0 + 1:h0 + 2, :], (HD_B, LANE))], axis=0)
        xdt = xt * dtp
        for i in range(bb):
            brow = jnp.broadcast_to(b_ref[i:i + 1, :], (2 * HD_B, N_B))
            crow = jnp.broadcast_to(c_ref[i:i + 1, :], (2 * SUBLANE, N_B))
            upd = jnp.broadcast_to(xdt[:, i:i + 1], (2 * HD_B, N_B)) * brow
            news = []
            for hh in range(2):
                dec = jnp.broadcast_to(dec_t[h0 + hh:h0 + hh + 1, i:i + 1], (HD_B, N_B))
                sn = dec * s_ref[i, 2 * j + hh] + upd[hh * HD_B:(hh + 1) * HD_B, :]
                so_ref[i, 2 * j + hh] = sn
                news.append(sn)
            spair = jnp.concatenate(news, axis=0).astype(BF16)
            yrow = _dot_tr(crow.astype(BF16), spair)
            y_ref[i:i + 1, j * LANE:(j + 1) * LANE] = yrow[0:1, :]


def _ab_sample_state_kernel(s_ref, xact_ref, dts_ref, alog_ref, so_ref, y_ref):
    hpg = H_B // G_B
    gw = D_B // G_B
    for g in range(G_B):
        heads = pl.ds(g * hpg, hpg)
        _ab_sample_pairs(s_ref.at[:, heads], xact_ref.at[:, pl.ds(g * gw, gw)],
                         xact_ref.at[:, pl.ds(D_B + g * N_B, N_B)],
                         xact_ref.at[:, pl.ds(D_B + G_B * N_B + g * N_B, N_B)],
                         dts_ref, alog_ref, so_ref.at[:, heads], y_ref.at[:, pl.ds(g * gw, gw)], g * hpg)


def _ab_sample_finish_kernel(y_ref, xact_ref, z_ref, dexp_ref, bnw_ref, bout_ref):
    bout_ref[...] = _group_norm_gate(y_ref[...], xact_ref[:, 0:D_B], z_ref[...], dexp_ref, bnw_ref)


def _ab_sample_rows(u, dtr, s_ah, s_ac, s_bc, w):
    bsz = u.shape[0]
    full = lambda shape: pl.BlockSpec(shape, lambda i: tuple(0 for _ in shape))
    cw = CONV_W - 1
    aout, ah, ac, bc, xact, dts = pl.pallas_call(
        _ab_sample_rows_kernel, grid=(1,),
        in_specs=[pl.BlockSpec((bsz, D_A), lambda i: (0, 0)), pl.BlockSpec((bsz, D_A), lambda i: (0, 1)),
                  pl.BlockSpec((bsz, CONV_DIM_B), lambda i: (0, 2)), full((bsz, LANE)),
                  full((bsz, D_A)), full((bsz, cw * D_A)), full((bsz, cw * CONV_DIM_B)),
                  full((CONV_W, D_A)), full((1, D_A)),
                  full((A_BLOCKS, A_BLK, A_BLK)), full((1, D_A)),
                  full((A_BLOCKS, A_BLK, A_BLK)), full((1, D_A)), full((1, D_A)),
                  full((CONV_W, CONV_DIM_B)), full((1, CONV_DIM_B)), full((1, LANE))],
        out_specs=[full((bsz, D_A)), full((bsz, D_A)), full((bsz, cw * D_A)), full((bsz, cw * CONV_DIM_B)),
                   full((bsz, CONV_DIM_B)), full((bsz, LANE))],
        out_shape=[jax.ShapeDtypeStruct((bsz, D_A), F32), jax.ShapeDtypeStruct((bsz, D_A), F32),
                   jax.ShapeDtypeStruct((bsz, cw * D_A), F32), jax.ShapeDtypeStruct((bsz, cw * CONV_DIM_B), F32),
                   jax.ShapeDtypeStruct((bsz, CONV_DIM_B), F32), jax.ShapeDtypeStruct((bsz, LANE), F32)],
        compiler_params=_params("arbitrary"),
        name="ab_sample_rows",
    )(u, u, u, dtr, s_ah, s_ac.reshape(bsz, cw * D_A), s_bc.reshape(bsz, cw * CONV_DIM_B),
      w["acw"], w["acb"], w["wr"], w["br"], w["wi"], w["bi"], w["lam"], w["bcw"], w["bcb"], w["dtb"])
    return aout, ah, ac.reshape(bsz, cw, D_A), bc.reshape(bsz, cw, CONV_DIM_B), xact, dts


def _ab_sample_state(s_bs, xact, dts, w):
    bsz = xact.shape[0]
    bb = SAMPLE_BB
    assert bsz % bb == 0
    return pl.pallas_call(
        _ab_sample_state_kernel, grid=(bsz // bb,),
        in_specs=[pl.BlockSpec((bb, H_B, HD_B, N_B), lambda i: (i, 0, 0, 0)),
                  pl.BlockSpec((bb, CONV_DIM_B), lambda i: (i, 0)),
                  pl.BlockSpec((bb, LANE), lambda i: (i, 0)),
                  pl.BlockSpec((1, LANE), lambda i: (0, 0))],
        out_specs=[pl.BlockSpec((bb, H_B, HD_B, N_B), lambda i: (i, 0, 0, 0)),
                   pl.BlockSpec((bb, D_B), lambda i: (i, 0))],
        out_shape=[jax.ShapeDtypeStruct(s_bs.shape, F32), jax.ShapeDtypeStruct((bsz, D_B), F32)],
        compiler_params=_params("parallel"),
        name="ab_sample_state",
    )(s_bs, xact, dts, w["alog"])


def _ab_sample_finish(y, xact, u, w):
    bsz = y.shape[0]
    full = lambda shape: pl.BlockSpec(shape, lambda i: (0, 0))
    return pl.pallas_call(
        _ab_sample_finish_kernel, grid=(1,),
        in_specs=[full((bsz, D_B)), full((bsz, CONV_DIM_B)),
                  pl.BlockSpec((bsz, D_B), lambda i: (0, 2)),
                  full((1, D_B)), full((1, D_B))],
        out_specs=full((bsz, D_B)),
        out_shape=jax.ShapeDtypeStruct((bsz, D_B), F32),
        compiler_params=_params("arbitrary"),
        name="ab_sample_finish",
    )(y, xact, u, w["dexp"], w["bnw"])


def _hg_lower_bound(clb, layer):
    mx = jnp.max(clb, axis=0, keepdims=True)
    ex = jnp.exp(clb - mx)
    return jnp.sum(ex[1:layer + 1], axis=0, keepdims=True) / jnp.sum(ex, axis=0, keepdims=True)


def _hg_gates(fx_half, lb):
    f = 0.5 * (1.0 + lb) + (0.5 * (1.0 - lb)) * jnp.tanh(fx_half)
    return f, 1.0 - f


def _hg_out(o, gate_half, cnw):
    return o * lax.rsqrt(jnp.mean(o * o, axis=-1, keepdims=True) + EPS) * cnw * (gate_half + gate_half * jnp.tanh(gate_half))


def _hg_gamma():
    import numpy as np
    q = HG_CHUNK
    t = np.arange(q)[:, None]
    tau = np.arange(q)[None, :]
    mats = [(tau <= t)]
    for l in range(1, HG_MXU_LEVELS):
        w = 1 << l
        ref = (t // (2 * w)) * (2 * w) + w - 1
        upper = (t % (2 * w)) >= w
        mats.append(np.where(upper, (tau > ref) & (tau <= t), (tau > t) & (tau <= ref)))
    gam = np.concatenate(mats, axis=0).astype(np.float32)
    return jnp.asarray(np.concatenate([gam, gam], axis=1), dtype=BF16)


def _hg_level_table():
    import numpy as np
    q = HG_CHUNK
    t = np.arange(q)[:, None]
    s = np.arange(q)[None, :]
    x = t ^ s
    lvl = np.floor(np.log2(np.maximum(x, 1))).astype(np.int32)
    return jnp.asarray(np.where(t > s, lvl, -1).astype(np.int32))


def _c_prompt_kernel(q_ref, f_ref, v_ref, g_ref, clb_ref, cnw_ref, gam_ref, lvl_ref,
                     og_ref, cs_ref, st_ref, *, layer):
    c = pl.program_id(1)
    last = pl.num_programs(1) - 1
    qc = HG_CHUNK

    @pl.when(c == 0)
    def _():
        st_ref[...] = jnp.zeros_like(st_ref)

    gam = gam_ref[...]
    ntile = qc // SUBLANE
    sub = lax.broadcasted_iota(jnp.int32, (SUBLANE, DK_C), 0)
    sub_levels = HG_MXU_LEVELS
    sub_upper = [(sub & (1 << l)) != 0 for l in range(sub_levels)]

    def tiles(x):
        return [x[i * SUBLANE:(i + 1) * SUBLANE, :] for i in range(ntile)]

    def gate_split(hd, rows):
        f, kk = _hg_gates(f_ref[hd, rows, :].astype(F32), _hg_lower_bound(clb_ref[hd], layer))
        g = jnp.log(jnp.maximum(f, HG_F_MIN)) * LOG2_E
        g1 = g.astype(BF16)
        g2 = (g - g1.astype(F32)).astype(BF16)
        return (f, kk), jnp.concatenate([g1, g2], axis=0)

    def scores(hd, rows, fk, sums):
        f, kk = fk
        qh = q_ref[hd, rows, :].astype(F32) * (DK_C ** -0.5)
        bcum = sums[0:qc]
        st = st_ref[hd]
        o = _dot_tr((qh * jnp.exp2(bcum)).astype(BF16), st.astype(BF16))
        qt, kt, ft, bt = tiles(qh), tiles(kk), tiles(f), tiles(bcum)
        prods = []
        for l in range(HG_LEVELS):
            if l == 0:
                xt = [jnp.where(sub_upper[0], qt[i] * ft[i], kt[i]) for i in range(ntile)]
            elif l < HG_MXU_LEVELS:
                dec = tiles(jnp.exp2(sums[l * qc:(l + 1) * qc]))
                xt = [jnp.where(sub_upper[l], qt[i], kt[i]) * dec[i] for i in range(ntile)]
            else:
                wt = 1 << (l - HG_MXU_LEVELS)
                xt = []
                for blk in range(0, ntile, 2 * wt):
                    ref = (blk + wt) * SUBLANE - 1
                    bref = jnp.broadcast_to(bcum[ref:ref + 1, :], (SUBLANE, DK_C))
                    xt += [kt[i] * jnp.exp2(bref - bt[i]) for i in range(blk, blk + wt)]
                    xt += [qt[i] * jnp.exp2(bt[i] - bref) for i in range(blk + wt, blk + 2 * wt)]
            x = jnp.concatenate(xt, axis=0).astype(BF16)
            half = (1 << l) // BF16_ROWS
            if half == 0:
                p = tiles(_dot_tr(x, x))
                prods.append({i: p[i] for i in range(ntile)})
            else:
                ups = [r for r in range(qc // BF16_ROWS) if (r // half) & 1]
                pu = _dot_tr(jnp.concatenate([x[r * BF16_ROWS:(r + 1) * BF16_ROWS, :] for r in ups], axis=0), x)
                tpr = BF16_ROWS // SUBLANE
                prods.append({r * tpr + k: pu[(n * tpr + k) * SUBLANE:(n * tpr + k + 1) * SUBLANE, :]
                              for n, r in enumerate(ups) for k in range(tpr)})
        return qh, st, o, prods

    def level_masks():
        masks = {}
        for i in range(ntile):
            lv = lvl_ref[i * SUBLANE:(i + 1) * SUBLANE, :]
            for l in range(HG_LEVELS):
                if l < sub_levels or (i >> (l - sub_levels)) & 1:
                    masks[i, l] = lv == l
        return masks

    def combine(hd, rows, kk, bcum, qh, st, o, prods, masks):
        arows = []
        for i in range(ntile):
            a = jnp.zeros((SUBLANE, qc), F32)
            for l in range(HG_LEVELS):
                if (i, l) in masks:
                    a = jnp.where(masks[i, l], prods[l][i], a)
            arows.append(a)
        amat = jnp.concatenate(arows, axis=0)
        vb = v_ref[hd, rows, :].astype(BF16)
        o = o + _dot(amat.astype(BF16), vb) + jnp.sum(qh * kk, axis=-1, keepdims=True) * vb.astype(F32)
        blast = bcum[qc - 1:qc, :]
        kdec = (kk * jnp.exp2(blast - bcum)).astype(BF16)
        st_ref[hd] = st * jnp.exp2(blast) + _dot_tl(vb, kdec)
        return o

    nchunk = HG_BLOCK // qc

    def body(idx, carry):
        hg = idx // nchunk
        rows = pl.ds(pl.multiple_of((idx % nchunk) * qc, qc), qc)
        heads = [hg * HG_UNROLL + k for k in range(HG_UNROLL)]
        gs = [gate_split(hd, rows) for hd in heads]
        sums = _dot(gam, jnp.concatenate([s for _, s in gs], axis=1))
        sums = [sums[:, k * DK_C:(k + 1) * DK_C] for k in range(HG_UNROLL)]
        sc = [scores(hd, rows, gs[k][0], sums[k]) for k, hd in enumerate(heads)]
        masks = level_masks()
        outs = [combine(hd, rows, gs[k][0][1], sums[k][0:qc], *sc[k], masks) for k, hd in enumerate(heads)]
        for k, hd in enumerate(heads):
            og_ref[hd, rows, :] = _hg_out(outs[k], g_ref[hd, rows, :].astype(F32), cnw_ref[hd]).astype(BF16)
        return carry

    lax.fori_loop(0, (H_C // HG_UNROLL) * nchunk, body, 0)

    @pl.when(c == last)
    def _():
        for hd in range(H_C):
            cs_ref[0, hd] = st_ref[hd].T


def _c_prompt(u, bsz, seq, w, layer):
    tb = HG_BLOCK
    assert seq % tb == 0 and tb % HG_CHUNK == 0 and (1 << HG_LEVELS) == HG_CHUNK
    nc = seq // tb
    m = bsz * seq
    depth = w["clb"].shape[1]

    def part(k):
        return pl.BlockSpec((H_C, tb, LANE), lambda b, c: (k, b * nc + c, 0))

    c2 = lambda b, c: (0, 0)
    c3 = lambda b, c: (0, 0, 0)
    og, cs = pl.pallas_call(
        functools.partial(_c_prompt_kernel, layer=layer), grid=(bsz, nc),
        in_specs=[part(0), part(1), part(2), part(3),
                  pl.BlockSpec((H_C, depth, DK_C), c3), pl.BlockSpec((H_C, 1, DV_C), c3),
                  pl.BlockSpec(w["gam"].shape, c2), pl.BlockSpec(w["lvl"].shape, c2)],
        out_specs=[pl.BlockSpec((H_C, tb, LANE), lambda b, c: (0, b * nc + c, 0)),
                   pl.BlockSpec((1, H_C, DK_C, DV_C), lambda b, c: (b, 0, 0, 0))],
        out_shape=[jax.ShapeDtypeStruct((H_C, m, DV_C), BF16),
                   jax.ShapeDtypeStruct((bsz, H_C, DK_C, DV_C), F32)],
        scratch_shapes=[pltpu.VMEM((H_C, DV_C, DK_C), F32)],
        compiler_params=_params("parallel", "arbitrary"),
        name="c_prompt",
    )(u, u, u, u, w["clb"], w["cnw"], w["gam"], w["lvl"])
    return og, cs


def _c_sample_head(hd, q_ref, f_ref, v_ref, g_ref, s_ref, clb_ref, cnw_ref, og_ref, so_ref, layer):
    bb = SAMPLE_BB
    lane = lax.broadcasted_iota(jnp.int32, (DK_C, LANE), 1)
    first_rows = lax.broadcasted_iota(jnp.int32, (LANE, DV_C), 0) < bb
    lb = _hg_lower_bound(clb_ref[hd], layer)
    f, kk = _hg_gates(f_ref[hd], lb)
    f_t = _pad_rows_t(f)
    k_t = _pad_rows_t(kk)
    qs = q_ref[hd] * (DK_C ** -0.5)
    v = v_ref[hd]
    vpad = jnp.where(first_rows, jnp.tile(v, (LANE // bb, 1)), 0.0).astype(BF16)
    orows = []
    for i in range(bb):
        fcol = jnp.broadcast_to(f_t[:, i:i + 1], (DK_C, DV_C))
        kv = _dot(jnp.where(lane == i, k_t, 0.0).astype(BF16), vpad)
        sn = fcol * s_ref[i, hd] + kv
        so_ref[i, hd] = sn
        qrow = jnp.broadcast_to(qs[i:i + 1, :], (2 * SUBLANE, DK_C)).astype(BF16)
        orows.append(_dot(qrow, sn.astype(BF16))[0:1, :])
    o = jnp.concatenate(orows, axis=0)
    og_ref[hd] = _hg_out(o, g_ref[hd], cnw_ref[hd])


def _c_sample_kernel(q_ref, f_ref, v_ref, g_ref, s_ref, clb_ref, cnw_ref, og_ref, so_ref, *, layer):
    def head(hd, carry):
        _c_sample_head(hd, q_ref, f_ref, v_ref, g_ref, s_ref, clb_ref, cnw_ref, og_ref, so_ref, layer)
        return carry

    lax.fori_loop(0, H_C, head, 0, unroll=SAMPLE_UNROLL)


def _c_sample(u, s_c, w, layer):
    bsz = s_c.shape[0]
    bb = SAMPLE_BB
    assert bsz % bb == 0
    depth = w["clb"].shape[1]

    def part(k):
        return pl.BlockSpec((H_C, bb, LANE), lambda i: (k, i, 0))

    c3 = lambda i: (0, 0, 0)
    og, so = pl.pallas_call(
        functools.partial(_c_sample_kernel, layer=layer), grid=(bsz // bb,),
        in_specs=[part(0), part(1), part(2), part(3),
                  pl.BlockSpec((bb, H_C, DK_C, DV_C), lambda i: (i, 0, 0, 0)),
                  pl.BlockSpec((H_C, depth, DK_C), c3), pl.BlockSpec((H_C, 1, DV_C), c3)],
        out_specs=[pl.BlockSpec((H_C, bb, LANE), lambda i: (0, i, 0)),
                   pl.BlockSpec((bb, H_C, DK_C, DV_C), lambda i: (i, 0, 0, 0))],
        out_shape=[jax.ShapeDtypeStruct((H_C, bsz, DV_C), F32), jax.ShapeDtypeStruct(s_c.shape, F32)],
        compiler_params=_params("parallel"),
        name="c_sample",
    )(u, u, u, u, s_c, w["clb"], w["cnw"])
    return og, so


def _out_proj_c_sample_kernel(mix_ref, h_ref, p_ref, wo_ref, wg_ref, wp_ref, nf_ref,
                              q_ref, f_ref, v_ref, g_ref, s_ref, clb_ref, cnw_ref,
                              o_ref, og_ref, so_ref, *, final, layer):
    for hd in range(q_ref.shape[0]):
        _c_sample_head(hd, q_ref, f_ref, v_ref, g_ref, s_ref, clb_ref, cnw_ref, og_ref, so_ref, layer)
    _out_proj_kernel(mix_ref, h_ref, p_ref, wo_ref, wg_ref, wp_ref, nf_ref, o_ref, head_major=True, final=final)


def _out_proj_c_sample(mix, h, p, layer, wo, wg, wp, nf, u_s, s_c, w, *, final):
    m, d = h.shape
    bsz = s_c.shape[0]
    bb, split = SAMPLE_BB, FUSED_HEAD_SPLIT
    steps = (bsz // bb) * split
    hp = H_C // split
    tm = m // steps
    assert bsz % bb == 0 and H_C % split == 0 and m % steps == 0 and tm % BF16_ROWS == 0
    depth = w["clb"].shape[1]
    const = lambda i: (0, 0)

    def part(k):
        return pl.BlockSpec((hp, bb, LANE), lambda i: (k * split + i % split, i // split, 0))

    state = pl.BlockSpec((bb, hp, DK_C, DV_C), lambda i: (i // split, i % split, 0, 0))
    hrow = lambda i: (i % split, 0, 0)
    hout, og, so = pl.pallas_call(
        functools.partial(_out_proj_c_sample_kernel, final=final, layer=layer),
        grid=(steps,),
        in_specs=[pl.BlockSpec((mix.shape[0], tm, LANE), lambda i: (0, i, 0)),
                  pl.BlockSpec((tm, d), lambda i: (i, 0)),
                  pl.BlockSpec((None, tm, p.shape[2]), lambda i: (layer, i, 0)),
                  pl.BlockSpec(wo.shape, const), pl.BlockSpec(wg.shape, const),
                  pl.BlockSpec(wp.shape, const), pl.BlockSpec((1, d), const),
                  part(0), part(1), part(2), part(3), state,
                  pl.BlockSpec((hp, depth, DK_C), hrow), pl.BlockSpec((hp, 1, DV_C), hrow)],
        out_specs=[pl.BlockSpec((tm, d), lambda i: (i, 0)),
                   pl.BlockSpec((hp, bb, LANE), lambda i: (i % split, i // split, 0)),
                   state],
        out_shape=[jax.ShapeDtypeStruct((m, d), F32),
                   jax.ShapeDtypeStruct((H_C, bsz, DV_C), F32), jax.ShapeDtypeStruct(s_c.shape, F32)],
        compiler_params=_params("arbitrary"),
        name="out_proj_c_sample",
    )(mix, h, p, wo, wg, wp, nf.reshape(1, d), u_s, u_s, u_s, u_s, s_c, w["clb"], w["cnw"])
    return hout, og, so


def _row(v, width=None):
    v = v.astype(F32).reshape(1, -1)
    if width is not None and v.shape[1] < width:
        v = jnp.pad(v, ((0, 0), (0, width - v.shape[1])))
    return v


def kernel(x_prompt, x_sample, p_prompt, p_sample, state_a_h, state_a_conv, state_b_ssm, state_b_conv, state_c,
           norm_w, norm_f, ab_w_in, a_conv_w, a_conv_b, a_w_r, a_b_r, a_w_i, a_b_i, a_lam, b_conv_w, b_conv_b,
           b_dt_bias, b_a_log, b_d, b_norm_w, ab_w_out, c_w_in, c_lb, c_norm_w, c_w_out, ple_proj, ple_gate):
    depth = norm_w.shape[0]
    bp, seq, _ = x_prompt.shape
    bs = x_sample.shape[0]
    hp = x_prompt.reshape(bp * seq, D_MODEL)
    hs = x_sample.reshape(bs, D_MODEL)
    pp = p_prompt.reshape(depth, bp * seq, D_PLE)
    ps = p_sample.reshape(depth, bs, D_PLE)
    gam, lvl = _hg_gamma(), _hg_level_table()
    clb = c_lb.astype(F32).reshape(depth, H_C, DK_C).transpose(1, 0, 2)

    def c_in_weights(jc):
        col_scale = jnp.concatenate([jnp.ones((HK_C,), F32), jnp.full((HK_C,), 0.5, F32),
                                     jnp.ones((D_C,), F32), jnp.full((D_C,), 0.5, F32)])
        return (c_w_in[jc] * col_scale).astype(BF16)

    ah_p, ac_p, bs_p, bc_p, c_p = [], [], [], [], []
    ah_s, ac_s, bs_s, bc_s, c_s = [], [], [], [], []
    next_u = None
    for i in range(depth):
        j = i // 2
        final = i == depth - 1
        wg = (0.5 * ple_gate[i]).astype(BF16)
        wp = (0.5 * ple_proj[i]).astype(BF16)
        if i % 2 == 0:
            col_scale = jnp.concatenate([jnp.ones((D_A,), F32), jnp.full((D_A + D_B,), 0.5, F32),
                                         jnp.ones((CONV_DIM_B,), F32)])
            w_ab_t = ab_w_in[j].T
            w_ab = (w_ab_t[:AB_MAIN] * col_scale[:, None]).astype(BF16)
            w_dt = jnp.pad(w_ab_t[AB_MAIN:], ((0, LANE - H_B), (0, 0))).astype(BF16)
            wo = ab_w_out[j].astype(BF16)
            w = dict(acw=a_conv_w[j].astype(F32), acb=_row(a_conv_b[j]),
                     wr=(0.5 * a_w_r[j]).astype(BF16), br=0.5 * _row(a_b_r[j]),
                     wi=a_w_i[j].astype(BF16), bi=_row(a_b_i[j]),
                     lam=_row(a_lam[j]), bcw=0.5 * b_conv_w[j].astype(F32), bcb=0.5 * _row(b_conv_b[j]),
                     dtb=_row(b_dt_bias[j], LANE), alog=_row(b_a_log[j], LANE),
                     dexp=_row(jnp.repeat(b_d[j], HD_B)), bnw=_row(b_norm_w[j]))
            us, dtrs = _in_proj(hs, norm_w[i], w_ab, w_dt, w_t=True)
            aout, s1, s2, s4, xact, dts = _ab_sample_rows(us, dtrs, state_a_h[j], state_a_conv[j], state_b_conv[j], w)
            ah_s.append(s1); ac_s.append(s2); bc_s.append(s4)
            u, dtr = _in_proj(hp, norm_w[i], w_ab, w_dt, w_t=True)
            hp, s1, s2, s3, s4 = _ab_prompt(u, dtr, hp, pp, i, wo, wg, wp, norm_f, bp, seq, w, final=final)
            ah_p.append(s1); ac_p.append(s2); bs_p.append(s3); bc_p.append(s4)
            if i + 1 < depth:
                next_u, bs_new, y = _in_proj(hp, norm_w[i + 1], c_in_weights((i + 1) // 2), head_major=True,
                                             head_major_dtype=BF16, sample=(state_b_ssm[j], xact, dts), alog=w["alog"])
            else:
                bs_new, y = _ab_sample_state(state_b_ssm[j], xact, dts, w)
            bs_s.append(bs_new)
            mix = jnp.concatenate([aout, _ab_sample_finish(y, xact, us, w)], axis=-1)
            hs = _out_proj(mix, hs, ps, i, wo, wg, wp, norm_f, head_major=False, final=final)
        else:
            w_in = c_in_weights(j)
            wo = c_w_out[j].astype(BF16)
            w = dict(clb=clb, cnw=c_norm_w[j].astype(F32).reshape(H_C, 1, DV_C), gam=gam, lvl=lvl)
            u = next_u if next_u is not None else _in_proj(hp, norm_w[i], w_in, head_major=True,
                                                           head_major_dtype=BF16)
            next_u = None
            og, s1 = _c_prompt(u, bp, seq, w, i)
            c_p.append(s1)
            u = _in_proj(hs, norm_w[i], w_in, head_major=True)
            steps = (bs // SAMPLE_BB) * FUSED_HEAD_SPLIT
            if bs % SAMPLE_BB == 0 and (bp * seq) % (steps * BF16_ROWS) == 0 and (bp * seq) // steps <= OUT_PROJ_TM:
                hp, og, s1 = _out_proj_c_sample(og, hp, pp, i, wo, wg, wp, norm_f, u, state_c[j], w, final=final)
            else:
                hp = _out_proj(og, hp, pp, i, wo, wg, wp, norm_f, head_major=True, final=final)
                og, s1 = _c_sample(u, state_c[j], w, i)
            c_s.append(s1)
            hs = _out_proj(og, hs, ps, i, wo, wg, wp, norm_f, head_major=True, final=final)
    return (hp.reshape(bp, seq, D_MODEL), hs.reshape(bs, 1, D_MODEL),
            jnp.stack(ah_p), jnp.stack(ac_p), jnp.stack(bs_p), jnp.stack(bc_p), jnp.stack(c_p),
            jnp.stack(ah_s), jnp.stack(ac_s), jnp.stack(bs_s), jnp.stack(bc_s), jnp.stack(c_s))
```

```python
import functools

import jax
import jax.numpy as jnp
from jax import lax
from jax.experimental import pallas as pl
from jax.experimental.pallas import tpu as pltpu

F32 = jnp.float32
BF16 = jnp.bfloat16

D_MODEL = 1024
D_PLE = 256
EPS = 1e-6
CONV_W = 4
D_A = D_MODEL
A_BLOCKS = 8
A_BLK = D_A // A_BLOCKS
LRU_C = 8.0
D_B = D_MODEL
HD_B = 64
H_B = D_B // HD_B
N_B = 128
G_B = 2
CONV_DIM_B = D_B + 2 * G_B * N_B
D_C = 2 * D_MODEL
H_C = 16
DK_C = 128
DV_C = D_C // H_C
HK_C = H_C * DK_C
AB_MAIN = 2 * D_A + D_B + CONV_DIM_B
IN_C = 2 * HK_C + 2 * D_C

LANE = 128
SUBLANE = 8
BF16_ROWS = 16
LOG2_E = 1.4426950408889634
VMEM_LIMIT = 56 * 1024 * 1024

PROJ_TM = 2048
PROJ_VMEM_BUDGET = 46 * 1024 * 1024
PROJ_MIN_STEPS = 4
HOST_VMEM_EXTRA = 6 * 1024 * 1024
OUT_PROJ_TM = 1024
SSD_CHUNK = 128
AB_CHUNKS = 2
HG_CHUNK = 64
HG_LEVELS = 6
HG_MXU_LEVELS = 3
HG_F_MIN = 1e-30
HG_BLOCK = 512
HG_UNROLL = 16
SAMPLE_BB = 8
SAMPLE_UNROLL = 4
FUSED_HEAD_SPLIT = 2

_DN_TR = (((1,), (1,)), ((), ()))
_DN_TL = (((0,), (0,)), ((), ()))


def _dot(a, b):
    return jnp.dot(a, b, preferred_element_type=F32)


def _dot_tr(a, b):
    return lax.dot_general(a, b, _DN_TR, preferred_element_type=F32)


def _dot_tl(a, b):
    return lax.dot_general(a, b, _DN_TL, preferred_element_type=F32)


def _silu_half(x_half):
    return x_half + x_half * jnp.tanh(x_half)


def _rmsnorm(x, w):
    return x * lax.rsqrt(jnp.mean(x * x, axis=-1, keepdims=True) + EPS) * w


def _params(*sem, vmem_limit=VMEM_LIMIT):
    return pltpu.CompilerParams(dimension_semantics=sem, vmem_limit_bytes=vmem_limit)


def _in_proj_kernel(x_ref, nw_ref, w_ref, *rest, has_extra, head_major, w_t, fused, prep):
    rest = list(rest)
    wx_ref = rest.pop(0) if has_extra else None
    sample_in = [rest.pop(0) for _ in range(6)] if fused else None
    prep_in = [rest.pop(0) for _ in range(2)] if prep else None
    o_ref = rest.pop(0)
    ox_ref = rest.pop(0) if has_extra else None
    sample_out = [rest.pop(0) for _ in range(2)] if fused else None
    prep_ref = rest.pop(0) if prep else None
    xn_ref, = rest
    dot = _dot_tr if w_t else _dot

    @pl.when(pl.program_id(1) == 0)
    def _():
        xn_ref[...] = _rmsnorm(x_ref[...], nw_ref[...]).astype(BF16)
        if has_extra:
            ox_ref[...] = dot(xn_ref[...], wx_ref[...])

    if fused:
        _ab_sample_pairs(*sample_in, *sample_out, 0)
    if prep:
        prep_ref[...] = (prep_in[0][...] * prep_in[1][...]).astype(BF16)
    acc = dot(xn_ref[...], w_ref[...])
    if head_major:
        for k in range(acc.shape[1] // LANE):
            o_ref[k] = acc[:, k * LANE:(k + 1) * LANE].astype(o_ref.dtype)
    else:
        o_ref[...] = acc


def _sample_ssd_split(sbsz, steps):
    if sbsz % SAMPLE_BB or steps % (sbsz // SAMPLE_BB):
        return 0
    split = steps // (sbsz // SAMPLE_BB)
    if H_B % split or (H_B // split) % 2 or (H_B // G_B) % (H_B // split):
        return 0
    return split


def _sample_ssd_operands(sample, alog, split, step):
    s_bs, xact, dts = sample
    sbsz, bb = s_bs.shape[0], SAMPLE_BB
    hps = H_B // split
    regroup = lambda v: jnp.pad(v[:, :H_B].reshape(-1, split, hps).transpose(1, 0, 2),
                                ((0, 0), (0, 0), (0, LANE - hps)))
    blk = lambda *g: step(*g) // split
    grp = lambda *g: step(*g) % split
    bc_group = lambda *g: (grp(*g) * hps) // (H_B // G_B)
    state = pl.BlockSpec((bb, hps, HD_B, N_B), lambda *g: (blk(*g), grp(*g), 0, 0))
    xs = pl.BlockSpec((bb, hps * HD_B), lambda *g: (blk(*g), grp(*g)))
    in_specs = [state, xs,
                pl.BlockSpec((bb, N_B), lambda *g: (blk(*g), D_B // N_B + bc_group(*g))),
                pl.BlockSpec((bb, N_B), lambda *g: (blk(*g), D_B // N_B + G_B + bc_group(*g))),
                pl.BlockSpec((None, bb, LANE), lambda *g: (grp(*g), blk(*g), 0)),
                pl.BlockSpec((None, 1, LANE), lambda *g: (grp(*g), 0, 0))]
    args = [s_bs, xact, xact, xact, regroup(dts), regroup(alog)]
    out_shape = [jax.ShapeDtypeStruct(s_bs.shape, F32), jax.ShapeDtypeStruct((sbsz, D_B), F32)]
    return in_specs, args, out_shape, [state, xs]


def _in_proj(x, nw, w, w_extra=None, *, head_major=False, w_t=False, head_major_dtype=F32, sample=None, alog=None,
             prep=None):
    m, k = x.shape
    extra_cols = w_extra is not None
    n = w.shape[0 if w_t else 1]
    tm = min(m, PROJ_TM)
    out_bytes = jnp.dtype(head_major_dtype if head_major else F32).itemsize

    def split_of(c):
        return _sample_ssd_split(sample[0].shape[0], (m // tm) * (n // c)) if sample is not None else 0

    def vmem(c):
        extra = 2 * k * LANE * 2 + 2 * tm * LANE * 4 if extra_cols else 0
        ssd = 4 * SAMPLE_BB * (H_B // split_of(c)) * HD_B * N_B * 4 if split_of(c) else 0
        return 2 * tm * k * 4 + tm * k * 2 + 2 * k * c * 2 + 2 * tm * c * out_bytes + extra + ssd

    headroom = HOST_VMEM_EXTRA if sample is not None else 0
    fits = [c for c in range(n, 0, -LANE) if n % c == 0 and vmem(c) <= PROJ_VMEM_BUDGET + headroom]
    if m // tm == 1:
        fits = [c for c in fits if n // c >= PROJ_MIN_STEPS] or fits
    tn = next((c for c in fits if split_of(c)), fits[0])
    split = split_of(tn)
    assert m % tm == 0 and n % tn == 0 and tn % LANE == 0
    grid = (m // tm, n // tn)
    wspec = (lambda cols, idx: pl.BlockSpec((cols, k), lambda i, j: (idx(j), 0))) if w_t else \
            (lambda cols, idx: pl.BlockSpec((k, cols), lambda i, j: (0, idx(j))))
    in_specs = [pl.BlockSpec((tm, k), lambda i, j: (i, 0)),
                pl.BlockSpec((1, k), lambda i, j: (0, 0)),
                wspec(tn, lambda j: j)]
    args = [x, nw.reshape(1, k), w]
    if head_major:
        out_shape = [jax.ShapeDtypeStruct((n // LANE, m, LANE), head_major_dtype)]
        out_specs = [pl.BlockSpec((tn // LANE, tm, LANE), lambda i, j: (j, i, 0))]
    else:
        out_shape = [jax.ShapeDtypeStruct((m, n), F32)]
        out_specs = [pl.BlockSpec((tm, tn), lambda i, j: (i, j))]
    if extra_cols:
        in_specs.append(wspec(LANE, lambda j: 0))
        args.append(w_extra)
        out_shape.append(jax.ShapeDtypeStruct((m, LANE), F32))
        out_specs.append(pl.BlockSpec((tm, LANE), lambda i, j: (i, 0)))
    step = lambda i, j: i * grid[1] + j
    if split:
        s_in, s_args, s_shape, s_out = _sample_ssd_operands(sample, alog, split, step)
        in_specs += s_in
        args += s_args
        out_shape += s_shape
        out_specs += s_out
    steps = grid[0] * grid[1]
    prep_here = prep is not None and prep[0].shape[1] % (steps * LANE) == 0
    if prep_here:
        k2, n2 = prep[0].shape
        in_specs += [pl.BlockSpec((k2, n2 // steps), lambda i, j: (0, step(i, j))),
                     pl.BlockSpec((1, n2 // steps), lambda i, j: (0, step(i, j)))]
        args += [prep[0], prep[1].reshape(1, n2)]
        out_shape.append(jax.ShapeDtypeStruct((k2, n2), BF16))
        out_specs.append(pl.BlockSpec((k2, n2 // steps), lambda i, j: (0, step(i, j))))
        headroom = HOST_VMEM_EXTRA
    outs = pl.pallas_call(
        functools.partial(_in_proj_kernel, has_extra=extra_cols, head_major=head_major, w_t=w_t, fused=bool(split),
                          prep=prep_here),
        grid=grid, in_specs=in_specs, out_specs=out_specs, out_shape=out_shape,
        scratch_shapes=[pltpu.VMEM((tm, k), BF16)],
        compiler_params=_params("arbitrary" if split or prep_here else "parallel", "arbitrary",
                                vmem_limit=VMEM_LIMIT + headroom),
        name="in_proj",
    )(*args)
    outs = list(outs)
    if sample is not None and not split:
        prep_out = [outs.pop()] if prep_here else []
        outs += list(_ab_sample_state(*sample, dict(alog=alog))) + prep_out
    if prep is not None and not prep_here:
        outs.append((prep[0] * prep[1]).astype(BF16))
    return outs if len(outs) > 1 else outs[0]


def _residual_ple(h, p_ref, wg_ref, wp_ref, nf_ref, final):
    gate_t = jnp.tanh(_dot(h.astype(BF16), wg_ref[...]))
    pe_half = _dot(p_ref[...].astype(BF16), wp_ref[...])
    h = h + pe_half + pe_half * gate_t
    return _rmsnorm(h, nf_ref[...]) if final else h


def _out_proj_kernel(mix_ref, h_ref, p_ref, wo_ref, wg_ref, wp_ref, nf_ref, o_ref, *, head_major, final):
    if head_major:
        mix = jnp.concatenate([mix_ref[k] for k in range(mix_ref.shape[0])], axis=-1)
    else:
        mix = mix_ref[...]
    h = h_ref[...] + _dot(mix.astype(BF16), wo_ref[...])
    o_ref[...] = _residual_ple(h, p_ref, wg_ref, wp_ref, nf_ref, final)


def _out_proj(mix, h, p, layer, wo, wg, wp, nf, *, head_major, final):
    m, d = h.shape
    tm = min(m, OUT_PROJ_TM)
    assert m % tm == 0
    if head_major:
        mix_spec = pl.BlockSpec((mix.shape[0], tm, LANE), lambda i: (0, i, 0))
    else:
        mix_spec = pl.BlockSpec((tm, mix.shape[1]), lambda i: (i, 0))
    const = lambda i: (0, 0)
    return pl.pallas_call(
        functools.partial(_out_proj_kernel, head_major=head_major, final=final),
        grid=(m // tm,),
        in_specs=[mix_spec,
                  pl.BlockSpec((tm, d), lambda i: (i, 0)),
                  pl.BlockSpec((None, tm, p.shape[2]), lambda i: (layer, i, 0)),
                  pl.BlockSpec(wo.shape, const), pl.BlockSpec(wg.shape, const),
                  pl.BlockSpec(wp.shape, const), pl.BlockSpec((1, d), const)],
        out_specs=pl.BlockSpec((tm, d), lambda i: (i, 0)),
        out_shape=jax.ShapeDtypeStruct((m, d), F32),
        compiler_params=_params("parallel"),
        name="out_proj",
    )(mix, h, p, wo, wg, wp, nf.reshape(1, d))


def _lru_gates(xc, wr_ref, br_ref, wi_ref, bi_ref, lam_ref):
    xcb = xc.astype(BF16)
    r_parts, i_parts = [], []
    for k in range(A_BLOCKS):
        xk = xcb[:, k * A_BLK:(k + 1) * A_BLK]
        r_parts.append(_dot(xk, wr_ref[k]))
        i_parts.append(_dot(xk, wi_ref[k]))
    tr = jnp.tanh(jnp.concatenate(r_parts, axis=-1) + br_ref[...])
    gi = jax.nn.sigmoid(jnp.concatenate(i_parts, axis=-1) + bi_ref[...])
    half_rate = (-0.5 * LRU_C) * jax.nn.softplus(-lam_ref[...])
    log_a = half_rate + half_rate * tr
    a = jnp.exp(log_a)
    v = jnp.tanh(-log_a) * (a * a + 1.0)
    u = jnp.where(v > 0.0, v * lax.rsqrt(v), 0.0) * (gi * xc)
    return a, u


def _group_norm_gate(y, bx, z_half, dexp_ref, bnw_ref):
    y = (y + dexp_ref[...] * bx) * _silu_half(z_half)
    gw = D_B // G_B
    parts = []
    for g in range(G_B):
        yg = y[:, g * gw:(g + 1) * gw]
        parts.append(yg * lax.rsqrt(jnp.mean(yg * yg, axis=-1, keepdims=True) + EPS))
    return jnp.concatenate(parts, axis=-1) * bnw_ref[...]


def _ab_prompt_kernel(ax_ref, ag_ref, z_ref, xbc_ref, dt_ref,
                      acw_ref, acb_ref, wr_ref, br_ref, wi_ref, bi_ref, lam_ref,
                      bcw_ref, bcb_ref, dtb_ref, alog_ref, dexp_ref, bnw_ref,
                      hin_ref, p_ref, wo_ref, wg_ref, wp_ref, nf_ref,
                      hout_ref, *state_refs, final):
    for k in range(AB_CHUNKS):
        r = pl.ds(k * SSD_CHUNK, SSD_CHUNK)
        _ab_prompt_chunk(ax_ref.at[r], ag_ref.at[r], z_ref.at[r], xbc_ref.at[r], dt_ref.at[r],
                         acw_ref, acb_ref, wr_ref, br_ref, wi_ref, bi_ref, lam_ref,
                         bcw_ref, bcb_ref, dtb_ref, alog_ref, dexp_ref, bnw_ref,
                         hin_ref.at[r], p_ref.at[r], wo_ref, wg_ref, wp_ref, nf_ref,
                         hout_ref.at[r], *state_refs, final=final, first=k == 0, last=k == AB_CHUNKS - 1)


def _ab_prompt_chunk(ax_ref, ag_ref, z_ref, xbc_ref, dt_ref,
                     acw_ref, acb_ref, wr_ref, br_ref, wi_ref, bi_ref, lam_ref,
                     bcw_ref, bcb_ref, dtb_ref, alog_ref, dexp_ref, bnw_ref,
                     hin_ref, p_ref, wo_ref, wg_ref, wp_ref, nf_ref,
                     hout_ref, ah_ref, ac_ref, bs_ref, bc_ref,
                     xpa_ref, xpb_ref, h_ref, s_ref, *, final, first, last):
    c = pl.program_id(1)
    t = SSD_CHUNK
    ntile = t // SUBLANE

    if first:
        @pl.when(c == 0)
        def _():
            xpa_ref[...] = jnp.zeros_like(xpa_ref)
            xpb_ref[...] = jnp.zeros_like(xpb_ref)
            h_ref[...] = jnp.zeros_like(h_ref)
            s_ref[...] = jnp.zeros_like(s_ref)

    def tiles(x):
        return [x[i * SUBLANE:(i + 1) * SUBLANE, :] for i in range(ntile)]

    def conv(x, tail_ref, w_ref, b_ref):
        sub = lax.broadcasted_iota(jnp.int32, (SUBLANE, x.shape[1]), 0)
        xt = [tail_ref[...]] + tiles(x)
        taps = [jnp.broadcast_to(w_ref[k:k + 1, :], (SUBLANE, x.shape[1])) for k in range(CONV_W)]
        bias = jnp.broadcast_to(b_ref[...], (SUBLANE, x.shape[1]))
        acc = [bias + taps[CONV_W - 1] * xt[i + 1] for i in range(ntile)]
        for s in range(1, CONV_W):
            wk = taps[CONV_W - 1 - s]
            for i in range(ntile):
                merged = jnp.where(sub >= SUBLANE - s, xt[i], xt[i + 1])
                acc[i] = acc[i] + wk * pltpu.roll(merged, s, 0)
        tail_ref[...] = xt[ntile]
        return jnp.concatenate(acc, axis=0)

    ax = ax_ref[...]
    xc = conv(ax, xpa_ref, acw_ref, acb_ref)
    a, u = _lru_gates(xc, wr_ref, br_ref, wi_ref, bi_ref, lam_ref)
    sub = lax.broadcasted_iota(jnp.int32, (SUBLANE, D_A), 0)
    at, ut = tiles(a), tiles(u)
    step = 1
    while step < SUBLANE:
        m = sub >= step
        for i in range(ntile):
            ut[i] = jnp.where(m, at[i] * pltpu.roll(ut[i], step, 0) + ut[i], ut[i])
            at[i] = jnp.where(m, at[i] * pltpu.roll(at[i], step, 0), at[i])
        step *= 2
    carry = h_ref[0:1, :]
    hs = []
    for i in range(ntile):
        hs.append(ut[i] + at[i] * carry)
        carry = hs[i][SUBLANE - 1:SUBLANE, :]
    h = jnp.concatenate(hs, axis=0)
    h_ref[0:1, :] = carry
    a_out = (h * _silu_half(ag_ref[...])).astype(BF16)

    xb = xbc_ref[...]
    xbc = _silu_half(conv(xb, xpb_ref, bcw_ref, bcb_ref))
    bx = xbc[:, 0:D_B]
    bxb = bx.astype(BF16)
    dt = jax.nn.softplus(dt_ref[...] + dtb_ref[...])
    adt = dt * (-LOG2_E * jnp.exp(alog_ref[...]))
    ti = lax.broadcasted_iota(jnp.int32, (t, t), 0)
    si = lax.broadcasted_iota(jnp.int32, (t, t), 1)
    causal = ti >= si
    acs = jnp.dot(causal.astype(F32), adt, preferred_element_type=F32,
                  precision=lax.Precision.HIGHEST)
    a_last = acs[t - 1:t, :]
    wq = jnp.exp2(a_last - acs) * dt
    eacs = jnp.exp2(acs)
    ealast = jnp.exp2(a_last)
    acs_t = acs.T
    dt_t = dt.T
    lane = lax.broadcasted_iota(jnp.int32, (t, LANE), 1)
    rowi = lax.broadcasted_iota(jnp.int32, (LANE, N_B), 0)
    hpg = H_B // G_B
    ys = []
    cb = None
    for j in range(H_B // 2):
        g = (2 * j) // hpg
        bg = xbc[:, D_B + g * N_B:D_B + (g + 1) * N_B]
        cg = xbc[:, D_B + G_B * N_B + g * N_B:D_B + G_B * N_B + (g + 1) * N_B]
        if (2 * j) % hpg == 0:
            cb = _dot_tr(cg.astype(BF16), bg.astype(BF16))
        xpair = bxb[:, j * LANE:(j + 1) * LANE]
        sp = s_ref[j]
        spb = sp.astype(BF16)
        y_h, up_h = [], []
        for hh in range(2):
            hd = 2 * j + hh
            seg = jnp.broadcast_to(acs[:, hd:hd + 1], (t, t)) - jnp.broadcast_to(acs_t[hd:hd + 1, :], (t, t))
            lmat = jnp.exp2(jnp.where(causal, seg, -1e30))
            mmat = (cb * lmat * jnp.broadcast_to(dt_t[hd:hd + 1, :], (t, t))).astype(BF16)
            ec = (jnp.broadcast_to(eacs[:, hd:hd + 1], (t, N_B)) * cg).astype(BF16)
            y_h.append(_dot(mmat, xpair) + _dot_tr(ec, spb))
            bw = (bg * jnp.broadcast_to(wq[:, hd:hd + 1], (t, N_B))).astype(BF16)
            up_h.append(_dot_tl(xpair, bw))
        ys.append(jnp.where(lane < HD_B, y_h[0], y_h[1]))
        dec = jnp.where(rowi < HD_B,
                        jnp.broadcast_to(ealast[:, 2 * j:2 * j + 1], (LANE, N_B)),
                        jnp.broadcast_to(ealast[:, 2 * j + 1:2 * j + 2], (LANE, N_B)))
        s_ref[j] = dec * sp + jnp.where(rowi < HD_B, up_h[0], up_h[1])
    y = jnp.concatenate(ys, axis=-1)
    b_out = _group_norm_gate(y, bx, z_ref[...], dexp_ref, bnw_ref).astype(BF16)

    hres = hin_ref[...] + _dot(a_out, wo_ref[0:D_A, :]) + _dot(b_out, wo_ref[D_A:D_A + D_B, :])
    hout_ref[...] = _residual_ple(hres, p_ref, wg_ref, wp_ref, nf_ref, final)

    if last:
        @pl.when(c == pl.num_programs(1) - 1)
        def _():
            ah_ref[0] = h[t - 1:t, :]
            ac_ref[0] = ax[t - (CONV_W - 1):t, :]
            bc_ref[0] = xb[t - (CONV_W - 1):t, :]
            for j in range(H_B // 2):
                sj = s_ref[j]
                bs_ref[0, 2 * j] = sj[0:HD_B, :]
                bs_ref[0, 2 * j + 1] = sj[HD_B:2 * HD_B, :]


def _ab_prompt(u, dtr, h, p, layer, wo, wg, wp, nf, bsz, seq, w, *, final):
    t = AB_CHUNKS * SSD_CHUNK
    assert seq % t == 0
    nc = seq // t
    m = bsz * seq
    rows = lambda b, c: b * nc + c
    cvec = lambda b, c: (0, 0)
    c3 = lambda b, c: (0, 0, 0)
    in_specs = [
        pl.BlockSpec((t, D_A), lambda b, c: (rows(b, c), 0)),
        pl.BlockSpec((t, D_A), lambda b, c: (rows(b, c), 1)),
        pl.BlockSpec((t, D_B), lambda b, c: (rows(b, c), 2)),
        pl.BlockSpec((t, CONV_DIM_B), lambda b, c: (rows(b, c), 2)),
        pl.BlockSpec((t, LANE), lambda b, c: (rows(b, c), 0)),
        pl.BlockSpec((CONV_W, D_A), cvec), pl.BlockSpec((1, D_A), cvec),
        pl.BlockSpec((A_BLOCKS, A_BLK, A_BLK), c3), pl.BlockSpec((1, D_A), cvec),
        pl.BlockSpec((A_BLOCKS, A_BLK, A_BLK), c3), pl.BlockSpec((1, D_A), cvec),
        pl.BlockSpec((1, D_A), cvec),
        pl.BlockSpec((CONV_W, CONV_DIM_B), cvec), pl.BlockSpec((1, CONV_DIM_B), cvec),
        pl.BlockSpec((1, LANE), cvec), pl.BlockSpec((1, LANE), cvec),
        pl.BlockSpec((1, D_B), cvec), pl.BlockSpec((1, D_B), cvec),
        pl.BlockSpec((t, D_MODEL), lambda b, c: (rows(b, c), 0)),
        pl.BlockSpec((None, t, p.shape[2]), lambda b, c: (layer, rows(b, c), 0)),
        pl.BlockSpec(wo.shape, cvec), pl.BlockSpec(wg.shape, cvec), pl.BlockSpec(wp.shape, cvec),
        pl.BlockSpec((1, D_MODEL), cvec),
    ]
    out_shape = [
        jax.ShapeDtypeStruct((m, D_MODEL), F32),
        jax.ShapeDtypeStruct((bsz, 1, D_A), F32),
        jax.ShapeDtypeStruct((bsz, CONV_W - 1, D_A), F32),
        jax.ShapeDtypeStruct((bsz, H_B, HD_B, N_B), F32),
        jax.ShapeDtypeStruct((bsz, CONV_W - 1, CONV_DIM_B), F32),
    ]
    out_specs = [
        pl.BlockSpec((t, D_MODEL), lambda b, c: (rows(b, c), 0)),
        pl.BlockSpec((1, 1, D_A), lambda b, c: (b, 0, 0)),
        pl.BlockSpec((1, CONV_W - 1, D_A), lambda b, c: (b, 0, 0)),
        pl.BlockSpec((1, H_B, HD_B, N_B), lambda b, c: (b, 0, 0, 0)),
        pl.BlockSpec((1, CONV_W - 1, CONV_DIM_B), lambda b, c: (b, 0, 0)),
    ]
    hout, ah, ac, bs, bc = pl.pallas_call(
        functools.partial(_ab_prompt_kernel, final=final),
        grid=(bsz, nc), in_specs=in_specs, out_specs=out_specs, out_shape=out_shape,
        scratch_shapes=[pltpu.VMEM((SUBLANE, D_A), F32), pltpu.VMEM((SUBLANE, CONV_DIM_B), F32),
                        pltpu.VMEM((SUBLANE, D_A), F32), pltpu.VMEM((H_B // 2, 2 * HD_B, N_B), F32)],
        compiler_params=_params("parallel", "arbitrary"),
        name="ab_prompt",
    )(u, u, u, u, dtr, w["acw"], w["acb"], w["wr"], w["br"], w["wi"], w["bi"], w["lam"],
      w["bcw"], w["bcb"], w["dtb"], w["alog"], w["dexp"], w["bnw"],
      h, p, wo, wg, wp, nf.reshape(1, D_MODEL))
    return hout, ah.reshape(bsz, D_A), ac, bs, bc


def _ab_sample_rows_kernel(ax_ref, ag_ref, xbc_ref, dt_ref, sah_ref, sac_ref, sbc_ref,
                           acw_ref, acb_ref, wr_ref, br_ref, wi_ref, bi_ref, lam_ref,
                           bcw_ref, bcb_ref, dtb_ref,
                           aout_ref, ah_ref, ac_ref, bc_ref, xact_ref, dts_ref):
    def conv1(x, buf_ref, w_ref, b_ref, nbuf_ref, width):
        y = b_ref[...] + w_ref[CONV_W - 1:CONV_W, :] * x
        for k in range(CONV_W - 1):
            y = y + w_ref[k:k + 1, :] * buf_ref[:, k * width:(k + 1) * width]
        for k in range(CONV_W - 2):
            nbuf_ref[:, k * width:(k + 1) * width] = buf_ref[:, (k + 1) * width:(k + 2) * width]
        nbuf_ref[:, (CONV_W - 2) * width:(CONV_W - 1) * width] = x
        return y

    xc = conv1(ax_ref[...], sac_ref, acw_ref, acb_ref, ac_ref, D_A)
    a, u = _lru_gates(xc, wr_ref, br_ref, wi_ref, bi_ref, lam_ref)
    h = a * sah_ref[...] + u
    ah_ref[...] = h
    aout_ref[...] = h * _silu_half(ag_ref[...])
    xact_ref[...] = _silu_half(conv1(xbc_ref[...], sbc_ref, bcw_ref, bcb_ref, bc_ref, CONV_DIM_B))
    dts_ref[...] = jax.nn.softplus(dt_ref[...] + dtb_ref[...])


def _pad_rows_t(x):
    pad = jnp.zeros((LANE - x.shape[0], x.shape[1]), F32)
    return jnp.concatenate([x, pad], axis=0).T


def _ab_sample_pairs(s_ref, bx_ref, b_ref, c_ref, dts_ref, alog_ref, so_ref, y_ref, head0):
    bb = SAMPLE_BB
    dts = dts_ref[...]
    dec_t = _pad_rows_t(jnp.exp(dts * (-jnp.exp(alog_ref[...]))))
    dts_t = _pad_rows_t(dts)
    for j in range(bx_ref.shape[1] // LANE):
        h0 = head0 + 2 * j
        xt = _pad_rows_t(bx_ref[:, j * LANE:(j + 1) * LANE])
        dtp = jnp.concatenate([jnp.broadcast_to(dts_t[h0:h0 + 1, :], (HD_B, LANE)),
                               jnp.broadcast_to(dts_t[h0 + 1:h0 + 2, :], (HD_B, LANE))], axis=0)
        xdt = xt * dtp
        for i in range(bb):
            brow = jnp.broadcast_to(b_ref[i:i + 1, :], (2 * HD_B, N_B))
            crow = jnp.broadcast_to(c_ref[i:i + 1, :].astype(BF16).astype(F32), (2 * HD_B, N_B))
            upd = jnp.broadcast_to(xdt[:, i:i + 1], (2 * HD_B, N_B)) * brow
            news = []
            for hh in range(2):
                dec = jnp.broadcast_to(dec_t[h0 + hh:h0 + hh + 1, i:i + 1], (HD_B, N_B))
                sn = dec * s_ref[i, 2 * j + hh] + upd[hh * HD_B:(hh + 1) * HD_B, :]
                so_ref[i, 2 * j + hh] = sn
                news.append(sn)
            prod = jnp.concatenate(news, axis=0).astype(BF16).astype(F32) * crow
            y_ref[i:i + 1, j * LANE:(j + 1) * LANE] = jnp.sum(prod.T, axis=0, keepdims=True)


def _ab_sample_state_kernel(s_ref, xact_ref, dts_ref, alog_ref, so_ref, y_ref):
    hpg = H_B // G_B
    gw = D_B // G_B
    for g in range(G_B):
        heads = pl.ds(g * hpg, hpg)
        _ab_sample_pairs(s_ref.at[:, heads], xact_ref.at[:, pl.ds(g * gw, gw)],
                         xact_ref.at[:, pl.ds(D_B + g * N_B, N_B)],
                         xact_ref.at[:, pl.ds(D_B + G_B * N_B + g * N_B, N_B)],
                         dts_ref, alog_ref, so_ref.at[:, heads], y_ref.at[:, pl.ds(g * gw, gw)], g * hpg)


def _ab_sample_finish_kernel(y_ref, xact_ref, z_ref, dexp_ref, bnw_ref, bout_ref):
    bout_ref[...] = _group_norm_gate(y_ref[...], xact_ref[:, 0:D_B], z_ref[...], dexp_ref, bnw_ref)


def _ab_sample_rows(u, dtr, s_ah, s_ac, s_bc, w):
    bsz = u.shape[0]
    full = lambda shape: pl.BlockSpec(shape, lambda i: tuple(0 for _ in shape))
    cw = CONV_W - 1
    aout, ah, ac, bc, xact, dts = pl.pallas_call(
        _ab_sample_rows_kernel, grid=(1,),
        in_specs=[pl.BlockSpec((bsz, D_A), lambda i: (0, 0)), pl.BlockSpec((bsz, D_A), lambda i: (0, 1)),
                  pl.BlockSpec((bsz, CONV_DIM_B), lambda i: (0, 2)), full((bsz, LANE)),
                  full((bsz, D_A)), full((bsz, cw * D_A)), full((bsz, cw * CONV_DIM_B)),
                  full((CONV_W, D_A)), full((1, D_A)),
                  full((A_BLOCKS, A_BLK, A_BLK)), full((1, D_A)),
                  full((A_BLOCKS, A_BLK, A_BLK)), full((1, D_A)), full((1, D_A)),
                  full((CONV_W, CONV_DIM_B)), full((1, CONV_DIM_B)), full((1, LANE))],
        out_specs=[full((bsz, D_A)), full((bsz, D_A)), full((bsz, cw * D_A)), full((bsz, cw * CONV_DIM_B)),
                   full((bsz, CONV_DIM_B)), full((bsz, LANE))],
        out_shape=[jax.ShapeDtypeStruct((bsz, D_A), F32), jax.ShapeDtypeStruct((bsz, D_A), F32),
                   jax.ShapeDtypeStruct((bsz, cw * D_A), F32), jax.ShapeDtypeStruct((bsz, cw * CONV_DIM_B), F32),
                   jax.ShapeDtypeStruct((bsz, CONV_DIM_B), F32), jax.ShapeDtypeStruct((bsz, LANE), F32)],
        compiler_params=_params("arbitrary"),
        name="ab_sample_rows",
    )(u, u, u, dtr, s_ah, s_ac.reshape(bsz, cw * D_A), s_bc.reshape(bsz, cw * CONV_DIM_B),
      w["acw"], w["acb"], w["wr"], w["br"], w["wi"], w["bi"], w["lam"], w["bcw"], w["bcb"], w["dtb"])
    return aout, ah, ac.reshape(bsz, cw, D_A), bc.reshape(bsz, cw, CONV_DIM_B), xact, dts


def _ab_sample_state(s_bs, xact, dts, w):
    bsz = xact.shape[0]
    bb = SAMPLE_BB
    assert bsz % bb == 0
    return pl.pallas_call(
        _ab_sample_state_kernel, grid=(bsz // bb,),
        in_specs=[pl.BlockSpec((bb, H_B, HD_B, N_B), lambda i: (i, 0, 0, 0)),
                  pl.BlockSpec((bb, CONV_DIM_B), lambda i: (i, 0)),
                  pl.BlockSpec((bb, LANE), lambda i: (i, 0)),
                  pl.BlockSpec((1, LANE), lambda i: (0, 0))],
        out_specs=[pl.BlockSpec((bb, H_B, HD_B, N_B), lambda i: (i, 0, 0, 0)),
                   pl.BlockSpec((bb, D_B), lambda i: (i, 0))],
        out_shape=[jax.ShapeDtypeStruct(s_bs.shape, F32), jax.ShapeDtypeStruct((bsz, D_B), F32)],
        compiler_params=_params("parallel"),
        name="ab_sample_state",
    )(s_bs, xact, dts, w["alog"])


def _ab_sample_finish(y, xact, u, w):
    bsz = y.shape[0]
    full = lambda shape: pl.BlockSpec(shape, lambda i: (0, 0))
    return pl.pallas_call(
        _ab_sample_finish_kernel, grid=(1,),
        in_specs=[full((bsz, D_B)), full((bsz, CONV_DIM_B)),
                  pl.BlockSpec((bsz, D_B), lambda i: (0, 2)),
                  full((1, D_B)), full((1, D_B))],
        out_specs=full((bsz, D_B)),
        out_shape=jax.ShapeDtypeStruct((bsz, D_B), F32),
        compiler_params=_params("arbitrary"),
        name="ab_sample_finish",
    )(y, xact, u, w["dexp"], w["bnw"])


def _hg_lower_bound(clb, layer):
    mx = jnp.max(clb, axis=0, keepdims=True)
    ex = jnp.exp(clb - mx)
    return jnp.sum(ex[1:layer + 1], axis=0, keepdims=True) / jnp.sum(ex, axis=0, keepdims=True)


def _hg_gates(fx_half, lb):
    f = 0.5 * (1.0 + lb) + (0.5 * (1.0 - lb)) * jnp.tanh(fx_half)
    return f, 1.0 - f


def _hg_out(o, gate_half, cnw):
    return o * lax.rsqrt(jnp.mean(o * o, axis=-1, keepdims=True) + EPS) * cnw * (gate_half + gate_half * jnp.tanh(gate_half))


def _hg_gamma():
    import numpy as np
    q = HG_CHUNK
    t = np.arange(q)[:, None]
    tau = np.arange(q)[None, :]
    mats = [(tau <= t)]
    for l in range(1, HG_MXU_LEVELS):
        w = 1 << l
        ref = (t // (2 * w)) * (2 * w) + w - 1
        upper = (t % (2 * w)) >= w
        mats.append(np.where(upper, (tau > ref) & (tau <= t), (tau > t) & (tau <= ref)))
    gam = np.concatenate(mats, axis=0).astype(np.float32)
    return jnp.asarray(np.concatenate([gam, gam], axis=1), dtype=BF16)


def _hg_level_table():
    import numpy as np
    q = HG_CHUNK
    t = np.arange(q)[:, None]
    s = np.arange(q)[None, :]
    x = t ^ s
    lvl = np.floor(np.log2(np.maximum(x, 1))).astype(np.int32)
    return jnp.asarray(np.where(t > s, lvl, -1).astype(np.int32))


def _c_prompt_kernel(q_ref, f_ref, v_ref, g_ref, clb_ref, cnw_ref, gam_ref, lvl_ref,
                     og_ref, cs_ref, st_ref, *, layer):
    c = pl.program_id(1)
    last = pl.num_programs(1) - 1
    qc = HG_CHUNK

    @pl.when(c == 0)
    def _():
        st_ref[...] = jnp.zeros_like(st_ref)

    gam = gam_ref[...]
    ntile = qc // SUBLANE
    sub = lax.broadcasted_iota(jnp.int32, (SUBLANE, DK_C), 0)
    sub_levels = HG_MXU_LEVELS
    sub_upper = [(sub & (1 << l)) != 0 for l in range(sub_levels)]

    def tiles(x):
        return [x[i * SUBLANE:(i + 1) * SUBLANE, :] for i in range(ntile)]

    def gate_split(hd, rows):
        f, kk = _hg_gates(f_ref[hd, rows, :].astype(F32), _hg_lower_bound(clb_ref[hd], layer))
        g = jnp.log(jnp.maximum(f, HG_F_MIN)) * LOG2_E
        g1 = g.astype(BF16)
        g2 = (g - g1.astype(F32)).astype(BF16)
        return (f, kk), jnp.concatenate([g1, g2], axis=0)

    def scores(hd, rows, fk, sums):
        f, kk = fk
        qh = q_ref[hd, rows, :].astype(F32) * (DK_C ** -0.5)
        bcum = sums[0:qc]
        st = st_ref[hd]
        o = _dot_tr((qh * jnp.exp2(bcum)).astype(BF16), st.astype(BF16))
        qt, kt, ft, bt = tiles(qh), tiles(kk), tiles(f), tiles(bcum)
        prods = []
        for l in range(HG_LEVELS):
            if l == 0:
                xt = [jnp.where(sub_upper[0], qt[i] * ft[i], kt[i]) for i in range(ntile)]
            elif l < HG_MXU_LEVELS:
                dec = tiles(jnp.exp2(sums[l * qc:(l + 1) * qc]))
                xt = [jnp.where(sub_upper[l], qt[i], kt[i]) * dec[i] for i in range(ntile)]
            else:
                wt = 1 << (l - HG_MXU_LEVELS)
                xt = []
                for blk in range(0, ntile, 2 * wt):
                    ref = (blk + wt) * SUBLANE - 1
                    bref = jnp.broadcast_to(bcum[ref:ref + 1, :], (SUBLANE, DK_C))
                    xt += [kt[i] * jnp.exp2(bref - bt[i]) for i in range(blk, blk + wt)]
                    xt += [qt[i] * jnp.exp2(bt[i] - bref) for i in range(blk + wt, blk + 2 * wt)]
            x = jnp.concatenate(xt, axis=0).astype(BF16)
            half = (1 << l) // BF16_ROWS
            if half == 0:
                p = tiles(_dot_tr(x, x))
                prods.append({i: p[i] for i in range(ntile)})
            else:
                ups = [r for r in range(qc // BF16_ROWS) if (r // half) & 1]
                pu = _dot_tr(jnp.concatenate([x[r * BF16_ROWS:(r + 1) * BF16_ROWS, :] for r in ups], axis=0), x)
                tpr = BF16_ROWS // SUBLANE
                prods.append({r * tpr + k: pu[(n * tpr + k) * SUBLANE:(n * tpr + k + 1) * SUBLANE, :]
                              for n, r in enumerate(ups) for k in range(tpr)})
        return qh, st, o, prods

    def level_masks():
        masks = {}
        for i in range(ntile):
            lv = lvl_ref[i * SUBLANE:(i + 1) * SUBLANE, :]
            for l in range(HG_LEVELS):
                if l < sub_levels or (i >> (l - sub_levels)) & 1:
                    masks[i, l] = lv == l
        return masks

    def combine(hd, rows, kk, bcum, qh, st, o, prods, masks):
        arows = []
        for i in range(ntile):
            a = jnp.zeros((SUBLANE, qc), F32)
            for l in range(HG_LEVELS):
                if (i, l) in masks:
                    a = jnp.where(masks[i, l], prods[l][i], a)
            arows.append(a)
        amat = jnp.concatenate(arows, axis=0)
        vb = v_ref[hd, rows, :].astype(BF16)
        o = o + _dot(amat.astype(BF16), vb) + jnp.sum(qh * kk, axis=-1, keepdims=True) * vb.astype(F32)
        blast = bcum[qc - 1:qc, :]
        kdec = (kk * jnp.exp2(blast - bcum)).astype(BF16)
        st_ref[hd] = st * jnp.exp2(blast) + _dot_tl(vb, kdec)
        return o

    nchunk = HG_BLOCK // qc

    def body(idx, carry):
        hg = idx // nchunk
        rows = pl.ds(pl.multiple_of((idx % nchunk) * qc, qc), qc)
        heads = [hg * HG_UNROLL + k for k in range(HG_UNROLL)]
        gs = [gate_split(hd, rows) for hd in heads]
        sums = _dot(gam, jnp.concatenate([s for _, s in gs], axis=1))
        sums = [sums[:, k * DK_C:(k + 1) * DK_C] for k in range(HG_UNROLL)]
        sc = [scores(hd, rows, gs[k][0], sums[k]) for k, hd in enumerate(heads)]
        masks = level_masks()
        outs = [combine(hd, rows, gs[k][0][1], sums[k][0:qc], *sc[k], masks) for k, hd in enumerate(heads)]
        for k, hd in enumerate(heads):
            og_ref[hd, rows, :] = _hg_out(outs[k], g_ref[hd, rows, :].astype(F32), cnw_ref[hd]).astype(BF16)
        return carry

    lax.fori_loop(0, (H_C // HG_UNROLL) * nchunk, body, 0)

    @pl.when(c == last)
    def _():
        for hd in range(H_C):
            cs_ref[0, hd] = st_ref[hd].T


def _c_prompt(u, bsz, seq, w, layer):
    tb = HG_BLOCK
    assert seq % tb == 0 and tb % HG_CHUNK == 0 and (1 << HG_LEVELS) == HG_CHUNK
    nc = seq // tb
    m = bsz * seq
    depth = w["clb"].shape[1]

    def part(k):
        return pl.BlockSpec((H_C, tb, LANE), lambda b, c: (k, b * nc + c, 0))

    c2 = lambda b, c: (0, 0)
    c3 = lambda b, c: (0, 0, 0)
    og, cs = pl.pallas_call(
        functools.partial(_c_prompt_kernel, layer=layer), grid=(bsz, nc),
        in_specs=[part(0), part(1), part(2), part(3),
                  pl.BlockSpec((H_C, depth, DK_C), c3), pl.BlockSpec((H_C, 1, DV_C), c3),
                  pl.BlockSpec(w["gam"].shape, c2), pl.BlockSpec(w["lvl"].shape, c2)],
        out_specs=[pl.BlockSpec((H_C, tb, LANE), lambda b, c: (0, b * nc + c, 0)),
                   pl.BlockSpec((1, H_C, DK_C, DV_C), lambda b, c: (b, 0, 0, 0))],
        out_shape=[jax.ShapeDtypeStruct((H_C, m, DV_C), BF16),
                   jax.ShapeDtypeStruct((bsz, H_C, DK_C, DV_C), F32)],
        scratch_shapes=[pltpu.VMEM((H_C, DV_C, DK_C), F32)],
        compiler_params=_params("parallel", "arbitrary"),
        name="c_prompt",
    )(u, u, u, u, w["clb"], w["cnw"], w["gam"], w["lvl"])
    return og, cs


def _c_sample_head(hd, q_ref, f_ref, v_ref, g_ref, s_ref, clb_ref, cnw_ref, og_ref, so_ref, layer):
    bb = SAMPLE_BB
    lane = lax.broadcasted_iota(jnp.int32, (DK_C, LANE), 1)
    first_rows = lax.broadcasted_iota(jnp.int32, (LANE, DV_C), 0) < bb
    lb = _hg_lower_bound(clb_ref[hd], layer)
    f, kk = _hg_gates(f_ref[hd], lb)
    f_t = _pad_rows_t(f)
    k_t = _pad_rows_t(kk)
    qs = q_ref[hd] * (DK_C ** -0.5)
    v = v_ref[hd]
    vpad = jnp.where(first_rows, jnp.tile(v, (LANE // bb, 1)), 0.0).astype(BF16)
    orows = []
    for i in range(bb):
        fcol = jnp.broadcast_to(f_t[:, i:i + 1], (DK_C, DV_C))
        kv = _dot(jnp.where(lane == i, k_t, 0.0).astype(BF16), vpad)
        sn = fcol * s_ref[i, hd] + kv
        so_ref[i, hd] = sn
        qrow = jnp.broadcast_to(qs[i:i + 1, :], (2 * SUBLANE, DK_C)).astype(BF16)
        orows.append(_dot(qrow, sn.astype(BF16))[0:1, :])
    o = jnp.concatenate(orows, axis=0)
    og_ref[hd] = _hg_out(o, g_ref[hd], cnw_ref[hd])


def _c_sample_kernel(q_ref, f_ref, v_ref, g_ref, s_ref, clb_ref, cnw_ref, og_ref, so_ref, *, layer):
    def head(hd, carry):
        _c_sample_head(hd, q_ref, f_ref, v_ref, g_ref, s_ref, clb_ref, cnw_ref, og_ref, so_ref, layer)
        return carry

    lax.fori_loop(0, H_C, head, 0, unroll=SAMPLE_UNROLL)


def _c_sample(u, s_c, w, layer):
    bsz = s_c.shape[0]
    bb = SAMPLE_BB
    assert bsz % bb == 0
    depth = w["clb"].shape[1]

    def part(k):
        return pl.BlockSpec((H_C, bb, LANE), lambda i: (k, i, 0))

    c3 = lambda i: (0, 0, 0)
    og, so = pl.pallas_call(
        functools.partial(_c_sample_kernel, layer=layer), grid=(bsz // bb,),
        in_specs=[part(0), part(1), part(2), part(3),
                  pl.BlockSpec((bb, H_C, DK_C, DV_C), lambda i: (i, 0, 0, 0)),
                  pl.BlockSpec((H_C, depth, DK_C), c3), pl.BlockSpec((H_C, 1, DV_C), c3)],
        out_specs=[pl.BlockSpec((H_C, bb, LANE), lambda i: (0, i, 0)),
                   pl.BlockSpec((bb, H_C, DK_C, DV_C), lambda i: (i, 0, 0, 0))],
        out_shape=[jax.ShapeDtypeStruct((H_C, bsz, DV_C), F32), jax.ShapeDtypeStruct(s_c.shape, F32)],
        compiler_params=_params("parallel"),
        name="c_sample",
    )(u, u, u, u, s_c, w["clb"], w["cnw"])
    return og, so


def _out_proj_c_sample_kernel(mix_ref, h_ref, p_ref, wo_ref, wg_ref, wp_ref, nf_ref,
                              q_ref, f_ref, v_ref, g_ref, s_ref, clb_ref, cnw_ref,
                              o_ref, og_ref, so_ref, *, final, layer):
    for hd in range(q_ref.shape[0]):
        _c_sample_head(hd, q_ref, f_ref, v_ref, g_ref, s_ref, clb_ref, cnw_ref, og_ref, so_ref, layer)
    _out_proj_kernel(mix_ref, h_ref, p_ref, wo_ref, wg_ref, wp_ref, nf_ref, o_ref, head_major=True, final=final)


def _out_proj_c_sample(mix, h, p, layer, wo, wg, wp, nf, u_s, s_c, w, *, final):
    m, d = h.shape
    bsz = s_c.shape[0]
    bb, split = SAMPLE_BB, FUSED_HEAD_SPLIT
    steps = (bsz // bb) * split
    hp = H_C // split
    tm = m // steps
    assert bsz % bb == 0 and H_C % split == 0 and m % steps == 0 and tm % BF16_ROWS == 0
    depth = w["clb"].shape[1]
    const = lambda i: (0, 0)

    def part(k):
        return pl.BlockSpec((hp, bb, LANE), lambda i: (k * split + i % split, i // split, 0))

    state = pl.BlockSpec((bb, hp, DK_C, DV_C), lambda i: (i // split, i % split, 0, 0))
    hrow = lambda i: (i % split, 0, 0)
    hout, og, so = pl.pallas_call(
        functools.partial(_out_proj_c_sample_kernel, final=final, layer=layer),
        grid=(steps,),
        in_specs=[pl.BlockSpec((mix.shape[0], tm, LANE), lambda i: (0, i, 0)),
                  pl.BlockSpec((tm, d), lambda i: (i, 0)),
                  pl.BlockSpec((None, tm, p.shape[2]), lambda i: (layer, i, 0)),
                  pl.BlockSpec(wo.shape, const), pl.BlockSpec(wg.shape, const),
                  pl.BlockSpec(wp.shape, const), pl.BlockSpec((1, d), const),
                  part(0), part(1), part(2), part(3), state,
                  pl.BlockSpec((hp, depth, DK_C), hrow), pl.BlockSpec((hp, 1, DV_C), hrow)],
        out_specs=[pl.BlockSpec((tm, d), lambda i: (i, 0)),
                   pl.BlockSpec((hp, bb, LANE), lambda i: (i % split, i // split, 0)),
                   state],
        out_shape=[jax.ShapeDtypeStruct((m, d), F32),
                   jax.ShapeDtypeStruct((H_C, bsz, DV_C), F32), jax.ShapeDtypeStruct(s_c.shape, F32)],
        compiler_params=_params("arbitrary"),
        name="out_proj_c_sample",
    )(mix, h, p, wo, wg, wp, nf.reshape(1, d), u_s, u_s, u_s, u_s, s_c, w["clb"], w["cnw"])
    return hout, og, so


def _row(v, width=None):
    v = v.astype(F32).reshape(1, -1)
    if width is not None and v.shape[1] < width:
        v = jnp.pad(v, ((0, 0), (0, width - v.shape[1])))
    return v


def kernel(x_prompt, x_sample, p_prompt, p_sample, state_a_h, state_a_conv, state_b_ssm, state_b_conv, state_c,
           norm_w, norm_f, ab_w_in, a_conv_w, a_conv_b, a_w_r, a_b_r, a_w_i, a_b_i, a_lam, b_conv_w, b_conv_b,
           b_dt_bias, b_a_log, b_d, b_norm_w, ab_w_out, c_w_in, c_lb, c_norm_w, c_w_out, ple_proj, ple_gate):
    depth = norm_w.shape[0]
    bp, seq, _ = x_prompt.shape
    bs = x_sample.shape[0]
    hp = x_prompt.reshape(bp * seq, D_MODEL)
    hs = x_sample.reshape(bs, D_MODEL)
    pp = p_prompt.reshape(depth, bp * seq, D_PLE)
    ps = p_sample.reshape(depth, bs, D_PLE)
    gam, lvl = _hg_gamma(), _hg_level_table()
    clb = c_lb.astype(F32).reshape(depth, H_C, DK_C).transpose(1, 0, 2)

    c_col_scale = jnp.concatenate([jnp.ones((HK_C,), F32), jnp.full((HK_C,), 0.5, F32),
                                   jnp.ones((D_C,), F32), jnp.full((D_C,), 0.5, F32)])

    ah_p, ac_p, bs_p, bc_p, c_p = [], [], [], [], []
    ah_s, ac_s, bs_s, bc_s, c_s = [], [], [], [], []
    next_u = next_w_in = None
    for i in range(depth):
        j = i // 2
        final = i == depth - 1
        wg = (0.5 * ple_gate[i]).astype(BF16)
        wp = (0.5 * ple_proj[i]).astype(BF16)
        if i % 2 == 0:
            col_scale = jnp.concatenate([jnp.ones((D_A,), F32), jnp.full((D_A + D_B,), 0.5, F32),
                                         jnp.ones((CONV_DIM_B,), F32)])
            w_ab_t = ab_w_in[j].T
            w_ab = (w_ab_t[:AB_MAIN] * col_scale[:, None]).astype(BF16)
            w_dt = jnp.pad(w_ab_t[AB_MAIN:], ((0, LANE - H_B), (0, 0))).astype(BF16)
            wo = ab_w_out[j].astype(BF16)
            w = dict(acw=a_conv_w[j].astype(F32), acb=_row(a_conv_b[j]),
                     wr=(0.5 * a_w_r[j]).astype(BF16), br=0.5 * _row(a_b_r[j]),
                     wi=a_w_i[j].astype(BF16), bi=_row(a_b_i[j]),
                     lam=_row(a_lam[j]), bcw=0.5 * b_conv_w[j].astype(F32), bcb=0.5 * _row(b_conv_b[j]),
                     dtb=_row(b_dt_bias[j], LANE), alog=_row(b_a_log[j], LANE),
                     dexp=_row(jnp.repeat(b_d[j], HD_B)), bnw=_row(b_norm_w[j]))
            us, dtrs = _in_proj(hs, norm_w[i], w_ab, w_dt, w_t=True)
            aout, s1, s2, s4, xact, dts = _ab_sample_rows(us, dtrs, state_a_h[j], state_a_conv[j], state_b_conv[j], w)
            ah_s.append(s1); ac_s.append(s2); bc_s.append(s4)
            if i + 1 < depth:
                u, dtr, next_w_in = _in_proj(hp, norm_w[i], w_ab, w_dt, w_t=True,
                                             prep=(c_w_in[(i + 1) // 2], c_col_scale))
            else:
                u, dtr = _in_proj(hp, norm_w[i], w_ab, w_dt, w_t=True)
            hp, s1, s2, s3, s4 = _ab_prompt(u, dtr, hp, pp, i, wo, wg, wp, norm_f, bp, seq, w, final=final)
            ah_p.append(s1); ac_p.append(s2); bs_p.append(s3); bc_p.append(s4)
            if i + 1 < depth:
                next_u, bs_new, y = _in_proj(hp, norm_w[i + 1], next_w_in, head_major=True, head_major_dtype=BF16,
                                             sample=(state_b_ssm[j], xact, dts), alog=w["alog"])
            else:
                bs_new, y = _ab_sample_state(state_b_ssm[j], xact, dts, w)
            bs_s.append(bs_new)
            mix = jnp.concatenate([aout, _ab_sample_finish(y, xact, us, w)], axis=-1)
            hs = _out_proj(mix, hs, ps, i, wo, wg, wp, norm_f, head_major=False, final=final)
        else:
            w_in = next_w_in if next_w_in is not None else (c_w_in[j] * c_col_scale).astype(BF16)
            wo = c_w_out[j].astype(BF16)
            w = dict(clb=clb, cnw=c_norm_w[j].astype(F32).reshape(H_C, 1, DV_C), gam=gam, lvl=lvl)
            u = next_u if next_u is not None else _in_proj(hp, norm_w[i], w_in, head_major=True,
                                                           head_major_dtype=BF16)
            next_u = next_w_in = None
            og, s1 = _c_prompt(u, bp, seq, w, i)
            c_p.append(s1)
            u = _in_proj(hs, norm_w[i], w_in, head_major=True)
            steps = (bs // SAMPLE_BB) * FUSED_HEAD_SPLIT
            if bs % SAMPLE_BB == 0 and (bp * seq) % (steps * BF16_ROWS) == 0 and (bp * seq) // steps <= OUT_PROJ_TM:
                hp, og, s1 = _out_proj_c_sample(og, hp, pp, i, wo, wg, wp, norm_f, u, state_c[j], w, final=final)
            else:
                hp = _out_proj(og, hp, pp, i, wo, wg, wp, norm_f, head_major=True, final=final)
                og, s1 = _c_sample(u, state_c[j], w, i)
            c_s.append(s1)
            hs = _out_proj(og, hs, ps, i, wo, wg, wp, norm_f, head_major=True, final=final)
    return (hp.reshape(bp, seq, D_MODEL), hs.reshape(bs, 1, D_MODEL),
            jnp.stack(ah_p), jnp.stack(ac_p), jnp.stack(bs_p), jnp.stack(bc_p), jnp.stack(c_p),
            jnp.stack(ah_s), jnp.stack(ac_s), jnp.stack(bs_s), jnp.stack(bc_s), jnp.stack(c_s))
```

```python
import functools

import jax
import jax.numpy as jnp
from jax import lax
from jax.experimental import pallas as pl
from jax.experimental.pallas import tpu as pltpu

F32 = jnp.float32
BF16 = jnp.bfloat16

D_MODEL = 1024
D_PLE = 256
EPS = 1e-6
CONV_W = 4
D_A = D_MODEL
A_BLOCKS = 8
A_BLK = D_A // A_BLOCKS
LRU_C = 8.0
D_B = D_MODEL
HD_B = 64
H_B = D_B // HD_B
N_B = 128
G_B = 2
CONV_DIM_B = D_B + 2 * G_B * N_B
D_C = 2 * D_MODEL
H_C = 16
DK_C = 128
DV_C = D_C // H_C
HK_C = H_C * DK_C
AB_MAIN = 2 * D_A + D_B + CONV_DIM_B
IN_C = 2 * HK_C + 2 * D_C

LANE = 128
SUBLANE = 8
BF16_ROWS = 16
LOG2_E = 1.4426950408889634
VMEM_LIMIT = 56 * 1024 * 1024

PROJ_TM = 2048
PROJ_VMEM_BUDGET = 46 * 1024 * 1024
PROJ_MIN_STEPS = 4
HOST_VMEM_EXTRA = 6 * 1024 * 1024
OUT_PROJ_TM = 1024
SSD_CHUNK = 128
AB_CHUNKS = 2
HG_CHUNK = 64
HG_LEVELS = 6
HG_MXU_LEVELS = 3
HG_F_MIN = 1e-30
HG_BLOCK = 512
HG_UNROLL = 16
SAMPLE_BB = 8
SAMPLE_UNROLL = 4
FUSED_HEAD_SPLIT = 2

_DN_TR = (((1,), (1,)), ((), ()))
_DN_TL = (((0,), (0,)), ((), ()))


def _dot(a, b):
    return jnp.dot(a, b, preferred_element_type=F32)


def _dot_tr(a, b):
    return lax.dot_general(a, b, _DN_TR, preferred_element_type=F32)


def _dot_tl(a, b):
    return lax.dot_general(a, b, _DN_TL, preferred_element_type=F32)


def _silu_half(x_half):
    return x_half + x_half * jnp.tanh(x_half)


def _rmsnorm(x, w):
    return x * lax.rsqrt(jnp.mean(x * x, axis=-1, keepdims=True) + EPS) * w


def _params(*sem, vmem_limit=VMEM_LIMIT):
    return pltpu.CompilerParams(dimension_semantics=sem, vmem_limit_bytes=vmem_limit)


def _in_proj_kernel(x_ref, nw_ref, w_ref, *rest, has_extra, head_major, w_t, fused, prep):
    rest = list(rest)
    wx_ref = rest.pop(0) if has_extra else None
    sample_in = [rest.pop(0) for _ in range(6)] if fused else None
    prep_in = [rest.pop(0) for _ in range(2)] if prep else None
    o_ref = rest.pop(0)
    ox_ref = rest.pop(0) if has_extra else None
    sample_out = [rest.pop(0) for _ in range(2)] if fused else None
    prep_ref = rest.pop(0) if prep else None
    xn_ref, = rest
    dot = _dot_tr if w_t else _dot

    @pl.when(pl.program_id(1) == 0)
    def _():
        xn_ref[...] = _rmsnorm(x_ref[...], nw_ref[...]).astype(BF16)
        if has_extra:
            ox_ref[...] = dot(xn_ref[...], wx_ref[...])

    if fused:
        _ab_sample_pairs(*sample_in, *sample_out, 0)
    if prep:
        prep_ref[...] = (prep_in[0][...] * prep_in[1][...]).astype(BF16)
    acc = dot(xn_ref[...], w_ref[...])
    if head_major:
        for k in range(acc.shape[1] // LANE):
            o_ref[k] = acc[:, k * LANE:(k + 1) * LANE].astype(o_ref.dtype)
    else:
        o_ref[...] = acc


def _sample_ssd_split(sbsz, steps):
    if sbsz % SAMPLE_BB or steps % (sbsz // SAMPLE_BB):
        return 0
    split = steps // (sbsz // SAMPLE_BB)
    if H_B % split or (H_B // split) % 2 or (H_B // G_B) % (H_B // split):
        return 0
    return split


def _sample_ssd_operands(sample, alog, split, step):
    s_bs, xact, dts = sample
    sbsz, bb = s_bs.shape[0], SAMPLE_BB
    hps = H_B // split
    regroup = lambda v: jnp.pad(v[:, :H_B].reshape(-1, split, hps).transpose(1, 0, 2),
                                ((0, 0), (0, 0), (0, LANE - hps)))
    blk = lambda *g: step(*g) // split
    grp = lambda *g: step(*g) % split
    bc_group = lambda *g: (grp(*g) * hps) // (H_B // G_B)
    state = pl.BlockSpec((bb, hps, HD_B, N_B), lambda *g: (blk(*g), grp(*g), 0, 0))
    xs = pl.BlockSpec((bb, hps * HD_B), lambda *g: (blk(*g), grp(*g)))
    in_specs = [state, xs,
                pl.BlockSpec((bb, N_B), lambda *g: (blk(*g), D_B // N_B + bc_group(*g))),
                pl.BlockSpec((bb, N_B), lambda *g: (blk(*g), D_B // N_B + G_B + bc_group(*g))),
                pl.BlockSpec((None, bb, LANE), lambda *g: (grp(*g), blk(*g), 0)),
                pl.BlockSpec((None, 1, LANE), lambda *g: (grp(*g), 0, 0))]
    args = [s_bs, xact, xact, xact, regroup(dts), regroup(alog)]
    out_shape = [jax.ShapeDtypeStruct(s_bs.shape, F32), jax.ShapeDtypeStruct((sbsz, D_B), F32)]
    return in_specs, args, out_shape, [state, xs]


def _in_proj(x, nw, w, w_extra=None, *, head_major=False, w_t=False, head_major_dtype=F32, sample=None, alog=None,
             prep=None):
    m, k = x.shape
    extra_cols = w_extra is not None
    n = w.shape[0 if w_t else 1]
    tm = min(m, PROJ_TM)
    out_bytes = jnp.dtype(head_major_dtype if head_major else F32).itemsize

    def split_of(c):
        return _sample_ssd_split(sample[0].shape[0], (m // tm) * (n // c)) if sample is not None else 0

    def vmem(c):
        extra = 2 * k * LANE * 2 + 2 * tm * LANE * 4 if extra_cols else 0
        ssd = 4 * SAMPLE_BB * (H_B // split_of(c)) * HD_B * N_B * 4 if split_of(c) else 0
        return 2 * tm * k * 4 + tm * k * 2 + 2 * k * c * 2 + 2 * tm * c * out_bytes + extra + ssd

    headroom = HOST_VMEM_EXTRA if sample is not None else 0
    fits = [c for c in range(n, 0, -LANE) if n % c == 0 and vmem(c) <= PROJ_VMEM_BUDGET + headroom]
    if m // tm == 1:
        fits = [c for c in fits if n // c >= PROJ_MIN_STEPS] or fits
    tn = next((c for c in fits if split_of(c)), fits[0])
    split = split_of(tn)
    assert m % tm == 0 and n % tn == 0 and tn % LANE == 0
    grid = (m // tm, n // tn)
    wspec = (lambda cols, idx: pl.BlockSpec((cols, k), lambda i, j: (idx(j), 0))) if w_t else \
            (lambda cols, idx: pl.BlockSpec((k, cols), lambda i, j: (0, idx(j))))
    in_specs = [pl.BlockSpec((tm, k), lambda i, j: (i, 0)),
                pl.BlockSpec((1, k), lambda i, j: (0, 0)),
                wspec(tn, lambda j: j)]
    args = [x, nw.reshape(1, k), w]
    if head_major:
        out_shape = [jax.ShapeDtypeStruct((n // LANE, m, LANE), head_major_dtype)]
        out_specs = [pl.BlockSpec((tn // LANE, tm, LANE), lambda i, j: (j, i, 0))]
    else:
        out_shape = [jax.ShapeDtypeStruct((m, n), F32)]
        out_specs = [pl.BlockSpec((tm, tn), lambda i, j: (i, j))]
    if extra_cols:
        in_specs.append(wspec(LANE, lambda j: 0))
        args.append(w_extra)
        out_shape.append(jax.ShapeDtypeStruct((m, LANE), F32))
        out_specs.append(pl.BlockSpec((tm, LANE), lambda i, j: (i, 0)))
    step = lambda i, j: i * grid[1] + j
    if split:
        s_in, s_args, s_shape, s_out = _sample_ssd_operands(sample, alog, split, step)
        in_specs += s_in
        args += s_args
        out_shape += s_shape
        out_specs += s_out
    steps = grid[0] * grid[1]
    prep_here = prep is not None and prep[0].shape[0] % (steps * BF16_ROWS) == 0
    if prep_here:
        k2, n2 = prep[0].shape
        in_specs += [pl.BlockSpec((k2 // steps, n2), lambda i, j: (step(i, j), 0)),
                     pl.BlockSpec((1, n2), lambda i, j: (0, 0))]
        args += [prep[0], prep[1].reshape(1, n2)]
        out_shape.append(jax.ShapeDtypeStruct((k2, n2), BF16))
        out_specs.append(pl.BlockSpec((k2 // steps, n2), lambda i, j: (step(i, j), 0)))
        headroom = HOST_VMEM_EXTRA
    outs = pl.pallas_call(
        functools.partial(_in_proj_kernel, has_extra=extra_cols, head_major=head_major, w_t=w_t, fused=bool(split),
                          prep=prep_here),
        grid=grid, in_specs=in_specs, out_specs=out_specs, out_shape=out_shape,
        scratch_shapes=[pltpu.VMEM((tm, k), BF16)],
        compiler_params=_params("arbitrary" if split or prep_here else "parallel", "arbitrary",
                                vmem_limit=VMEM_LIMIT + headroom),
        name="in_proj",
    )(*args)
    outs = list(outs)
    if sample is not None and not split:
        prep_out = [outs.pop()] if prep_here else []
        outs += list(_ab_sample_state(*sample, dict(alog=alog))) + prep_out
    if prep is not None and not prep_here:
        outs.append((prep[0] * prep[1]).astype(BF16))
    return outs if len(outs) > 1 else outs[0]


def _residual_ple(h, p_ref, wg_ref, wp_ref, nf_ref, final):
    gate_t = jnp.tanh(_dot(h.astype(BF16), wg_ref[...]))
    pe_half = _dot(p_ref[...].astype(BF16), wp_ref[...])
    h = h + pe_half + pe_half * gate_t
    return _rmsnorm(h, nf_ref[...]) if final else h


def _out_proj_kernel(mix_ref, h_ref, p_ref, wo_ref, wg_ref, wp_ref, nf_ref, o_ref, *, head_major, final):
    if head_major:
        mix = jnp.concatenate([mix_ref[k] for k in range(mix_ref.shape[0])], axis=-1)
    else:
        mix = mix_ref[...]
    h = h_ref[...] + _dot(mix.astype(BF16), wo_ref[...])
    o_ref[...] = _residual_ple(h, p_ref, wg_ref, wp_ref, nf_ref, final)


def _out_proj(mix, h, p, layer, wo, wg, wp, nf, *, head_major, final):
    m, d = h.shape
    tm = min(m, OUT_PROJ_TM)
    assert m % tm == 0
    if head_major:
        mix_spec = pl.BlockSpec((mix.shape[0], tm, LANE), lambda i: (0, i, 0))
    else:
        mix_spec = pl.BlockSpec((tm, mix.shape[1]), lambda i: (i, 0))
    const = lambda i: (0, 0)
    return pl.pallas_call(
        functools.partial(_out_proj_kernel, head_major=head_major, final=final),
        grid=(m // tm,),
        in_specs=[mix_spec,
                  pl.BlockSpec((tm, d), lambda i: (i, 0)),
                  pl.BlockSpec((None, tm, p.shape[2]), lambda i: (layer, i, 0)),
                  pl.BlockSpec(wo.shape, const), pl.BlockSpec(wg.shape, const),
                  pl.BlockSpec(wp.shape, const), pl.BlockSpec((1, d), const)],
        out_specs=pl.BlockSpec((tm, d), lambda i: (i, 0)),
        out_shape=jax.ShapeDtypeStruct((m, d), F32),
        compiler_params=_params("parallel"),
        name="out_proj",
    )(mix, h, p, wo, wg, wp, nf.reshape(1, d))


def _lru_gates(xc, wr_ref, br_ref, wi_ref, bi_ref, lam_ref):
    xcb = xc.astype(BF16)
    r_parts, i_parts = [], []
    for k in range(A_BLOCKS):
        xk = xcb[:, k * A_BLK:(k + 1) * A_BLK]
        r_parts.append(_dot(xk, wr_ref[k]))
        i_parts.append(_dot(xk, wi_ref[k]))
    tr = jnp.tanh(jnp.concatenate(r_parts, axis=-1) + br_ref[...])
    gi = jax.nn.sigmoid(jnp.concatenate(i_parts, axis=-1) + bi_ref[...])
    half_rate = (-0.5 * LRU_C) * jax.nn.softplus(-lam_ref[...])
    log_a = half_rate + half_rate * tr
    a = jnp.exp(log_a)
    v = jnp.tanh(-log_a) * (a * a + 1.0)
    u = jnp.where(v > 0.0, v * lax.rsqrt(v), 0.0) * (gi * xc)
    return a, u


def _group_norm_gate(y, bx, z_half, dexp_ref, bnw_ref):
    y = (y + dexp_ref[...] * bx) * _silu_half(z_half)
    gw = D_B // G_B
    parts = []
    for g in range(G_B):
        yg = y[:, g * gw:(g + 1) * gw]
        parts.append(yg * lax.rsqrt(jnp.mean(yg * yg, axis=-1, keepdims=True) + EPS))
    return jnp.concatenate(parts, axis=-1) * bnw_ref[...]


def _ab_prompt_kernel(ax_ref, ag_ref, z_ref, xbc_ref, dt_ref,
                      acw_ref, acb_ref, wr_ref, br_ref, wi_ref, bi_ref, lam_ref,
                      bcw_ref, bcb_ref, dtb_ref, alog_ref, dexp_ref, bnw_ref,
                      hin_ref, p_ref, wo_ref, wg_ref, wp_ref, nf_ref,
                      hout_ref, *state_refs, final):
    for k in range(AB_CHUNKS):
        r = pl.ds(k * SSD_CHUNK, SSD_CHUNK)
        _ab_prompt_chunk(ax_ref.at[r], ag_ref.at[r], z_ref.at[r], xbc_ref.at[r], dt_ref.at[r],
                         acw_ref, acb_ref, wr_ref, br_ref, wi_ref, bi_ref, lam_ref,
                         bcw_ref, bcb_ref, dtb_ref, alog_ref, dexp_ref, bnw_ref,
                         hin_ref.at[r], p_ref.at[r], wo_ref, wg_ref, wp_ref, nf_ref,
                         hout_ref.at[r], *state_refs, final=final, first=k == 0, last=k == AB_CHUNKS - 1)


def _ab_prompt_chunk(ax_ref, ag_ref, z_ref, xbc_ref, dt_ref,
                     acw_ref, acb_ref, wr_ref, br_ref, wi_ref, bi_ref, lam_ref,
                     bcw_ref, bcb_ref, dtb_ref, alog_ref, dexp_ref, bnw_ref,
                     hin_ref, p_ref, wo_ref, wg_ref, wp_ref, nf_ref,
                     hout_ref, ah_ref, ac_ref, bs_ref, bc_ref,
                     xpa_ref, xpb_ref, h_ref, s_ref, *, final, first, last):
    c = pl.program_id(1)
    t = SSD_CHUNK
    ntile = t // SUBLANE

    if first:
        @pl.when(c == 0)
        def _():
            xpa_ref[...] = jnp.zeros_like(xpa_ref)
            xpb_ref[...] = jnp.zeros_like(xpb_ref)
            h_ref[...] = jnp.zeros_like(h_ref)
            s_ref[...] = jnp.zeros_like(s_ref)

    def tiles(x):
        return [x[i * SUBLANE:(i + 1) * SUBLANE, :] for i in range(ntile)]

    def conv(x, tail_ref, w_ref, b_ref):
        sub = lax.broadcasted_iota(jnp.int32, (SUBLANE, x.shape[1]), 0)
        xt = [tail_ref[...]] + tiles(x)
        taps = [jnp.broadcast_to(w_ref[k:k + 1, :], (SUBLANE, x.shape[1])) for k in range(CONV_W)]
        bias = jnp.broadcast_to(b_ref[...], (SUBLANE, x.shape[1]))
        acc = [bias + taps[CONV_W - 1] * xt[i + 1] for i in range(ntile)]
        for s in range(1, CONV_W):
            wk = taps[CONV_W - 1 - s]
            for i in range(ntile):
                merged = jnp.where(sub >= SUBLANE - s, xt[i], xt[i + 1])
                acc[i] = acc[i] + wk * pltpu.roll(merged, s, 0)
        tail_ref[...] = xt[ntile]
        return jnp.concatenate(acc, axis=0)

    ax = ax_ref[...]
    xc = conv(ax, xpa_ref, acw_ref, acb_ref)
    a, u = _lru_gates(xc, wr_ref, br_ref, wi_ref, bi_ref, lam_ref)
    sub = lax.broadcasted_iota(jnp.int32, (SUBLANE, D_A), 0)
    at, ut = tiles(a), tiles(u)
    step = 1
    while step < SUBLANE:
        m = sub >= step
        for i in range(ntile):
            ut[i] = jnp.where(m, at[i] * pltpu.roll(ut[i], step, 0) + ut[i], ut[i])
            at[i] = jnp.where(m, at[i] * pltpu.roll(at[i], step, 0), at[i])
        step *= 2
    carry = h_ref[0:1, :]
    hs = []
    for i in range(ntile):
        hs.append(ut[i] + at[i] * carry)
        carry = hs[i][SUBLANE - 1:SUBLANE, :]
    h = jnp.concatenate(hs, axis=0)
    h_ref[0:1, :] = carry
    a_out = (h * _silu_half(ag_ref[...])).astype(BF16)

    xb = xbc_ref[...]
    xbc = _silu_half(conv(xb, xpb_ref, bcw_ref, bcb_ref))
    bx = xbc[:, 0:D_B]
    bxb = bx.astype(BF16)
    dt = jax.nn.softplus(dt_ref[...] + dtb_ref[...])
    adt = dt * (-LOG2_E * jnp.exp(alog_ref[...]))
    ti = lax.broadcasted_iota(jnp.int32, (t, t), 0)
    si = lax.broadcasted_iota(jnp.int32, (t, t), 1)
    causal = ti >= si
    acs = jnp.dot(causal.astype(F32), adt, preferred_element_type=F32,
                  precision=lax.Precision.HIGHEST)
    a_last = acs[t - 1:t, :]
    wq = jnp.exp2(a_last - acs) * dt
    eacs = jnp.exp2(acs)
    ealast = jnp.exp2(a_last)
    acs_t = acs.T
    dt_t = dt.T
    lane = lax.broadcasted_iota(jnp.int32, (t, LANE), 1)
    rowi = lax.broadcasted_iota(jnp.int32, (LANE, N_B), 0)
    hpg = H_B // G_B
    ys = []
    cb = None
    for j in range(H_B // 2):
        g = (2 * j) // hpg
        bg = xbc[:, D_B + g * N_B:D_B + (g + 1) * N_B]
        cg = xbc[:, D_B + G_B * N_B + g * N_B:D_B + G_B * N_B + (g + 1) * N_B]
        if (2 * j) % hpg == 0:
            cb = _dot_tr(cg.astype(BF16), bg.astype(BF16))
        xpair = bxb[:, j * LANE:(j + 1) * LANE]
        sp = s_ref[j]
        spb = sp.astype(BF16)
        y_h, up_h = [], []
        for hh in range(2):
            hd = 2 * j + hh
            seg = jnp.broadcast_to(acs[:, hd:hd + 1], (t, t)) - jnp.broadcast_to(acs_t[hd:hd + 1, :], (t, t))
            lmat = jnp.exp2(jnp.where(causal, seg, -1e30))
            mmat = (cb * lmat * jnp.broadcast_to(dt_t[hd:hd + 1, :], (t, t))).astype(BF16)
            ec = (jnp.broadcast_to(eacs[:, hd:hd + 1], (t, N_B)) * cg).astype(BF16)
            y_h.append(_dot(mmat, xpair) + _dot_tr(ec, spb))
            bw = (bg * jnp.broadcast_to(wq[:, hd:hd + 1], (t, N_B))).astype(BF16)
            up_h.append(_dot_tl(xpair, bw))
        ys.append(jnp.where(lane < HD_B, y_h[0], y_h[1]))
        dec = jnp.where(rowi < HD_B,
                        jnp.broadcast_to(ealast[:, 2 * j:2 * j + 1], (LANE, N_B)),
                        jnp.broadcast_to(ealast[:, 2 * j + 1:2 * j + 2], (LANE, N_B)))
        s_ref[j] = dec * sp + jnp.where(rowi < HD_B, up_h[0], up_h[1])
    y = jnp.concatenate(ys, axis=-1)
    b_out = _group_norm_gate(y, bx, z_ref[...], dexp_ref, bnw_ref).astype(BF16)

    hres = hin_ref[...] + _dot(a_out, wo_ref[0:D_A, :]) + _dot(b_out, wo_ref[D_A:D_A + D_B, :])
    hout_ref[...] = _residual_ple(hres, p_ref, wg_ref, wp_ref, nf_ref, final)

    if last:
        @pl.when(c == pl.num_programs(1) - 1)
        def _():
            ah_ref[0] = h[t - 1:t, :]
            ac_ref[0] = ax[t - (CONV_W - 1):t, :]
            bc_ref[0] = xb[t - (CONV_W - 1):t, :]
            for j in range(H_B // 2):
                sj = s_ref[j]
                bs_ref[0, 2 * j] = sj[0:HD_B, :]
                bs_ref[0, 2 * j + 1] = sj[HD_B:2 * HD_B, :]


def _ab_prompt(u, dtr, h, p, layer, wo, wg, wp, nf, bsz, seq, w, *, final):
    t = AB_CHUNKS * SSD_CHUNK
    assert seq % t == 0
    nc = seq // t
    m = bsz * seq
    rows = lambda b, c: b * nc + c
    cvec = lambda b, c: (0, 0)
    c3 = lambda b, c: (0, 0, 0)
    in_specs = [
        pl.BlockSpec((t, D_A), lambda b, c: (rows(b, c), 0)),
        pl.BlockSpec((t, D_A), lambda b, c: (rows(b, c), 1)),
        pl.BlockSpec((t, D_B), lambda b, c: (rows(b, c), 2)),
        pl.BlockSpec((t, CONV_DIM_B), lambda b, c: (rows(b, c), 2)),
        pl.BlockSpec((t, LANE), lambda b, c: (rows(b, c), 0)),
        pl.BlockSpec((CONV_W, D_A), cvec), pl.BlockSpec((1, D_A), cvec),
        pl.BlockSpec((A_BLOCKS, A_BLK, A_BLK), c3), pl.BlockSpec((1, D_A), cvec),
        pl.BlockSpec((A_BLOCKS, A_BLK, A_BLK), c3), pl.BlockSpec((1, D_A), cvec),
        pl.BlockSpec((1, D_A), cvec),
        pl.BlockSpec((CONV_W, CONV_DIM_B), cvec), pl.BlockSpec((1, CONV_DIM_B), cvec),
        pl.BlockSpec((1, LANE), cvec), pl.BlockSpec((1, LANE), cvec),
        pl.BlockSpec((1, D_B), cvec), pl.BlockSpec((1, D_B), cvec),
        pl.BlockSpec((t, D_MODEL), lambda b, c: (rows(b, c), 0)),
        pl.BlockSpec((None, t, p.shape[2]), lambda b, c: (layer, rows(b, c), 0)),
        pl.BlockSpec(wo.shape, cvec), pl.BlockSpec(wg.shape, cvec), pl.BlockSpec(wp.shape, cvec),
        pl.BlockSpec((1, D_MODEL), cvec),
    ]
    out_shape = [
        jax.ShapeDtypeStruct((m, D_MODEL), F32),
        jax.ShapeDtypeStruct((bsz, 1, D_A), F32),
        jax.ShapeDtypeStruct((bsz, CONV_W - 1, D_A), F32),
        jax.ShapeDtypeStruct((bsz, H_B, HD_B, N_B), F32),
        jax.ShapeDtypeStruct((bsz, CONV_W - 1, CONV_DIM_B), F32),
    ]
    out_specs = [
        pl.BlockSpec((t, D_MODEL), lambda b, c: (rows(b, c), 0)),
        pl.BlockSpec((1, 1, D_A), lambda b, c: (b, 0, 0)),
        pl.BlockSpec((1, CONV_W - 1, D_A), lambda b, c: (b, 0, 0)),
        pl.BlockSpec((1, H_B, HD_B, N_B), lambda b, c: (b, 0, 0, 0)),
        pl.BlockSpec((1, CONV_W - 1, CONV_DIM_B), lambda b, c: (b, 0, 0)),
    ]
    hout, ah, ac, bs, bc = pl.pallas_call(
        functools.partial(_ab_prompt_kernel, final=final),
        grid=(bsz, nc), in_specs=in_specs, out_specs=out_specs, out_shape=out_shape,
        scratch_shapes=[pltpu.VMEM((SUBLANE, D_A), F32), pltpu.VMEM((SUBLANE, CONV_DIM_B), F32),
                        pltpu.VMEM((SUBLANE, D_A), F32), pltpu.VMEM((H_B // 2, 2 * HD_B, N_B), F32)],
        compiler_params=_params("parallel", "arbitrary"),
        name="ab_prompt",
    )(u, u, u, u, dtr, w["acw"], w["acb"], w["wr"], w["br"], w["wi"], w["bi"], w["lam"],
      w["bcw"], w["bcb"], w["dtb"], w["alog"], w["dexp"], w["bnw"],
      h, p, wo, wg, wp, nf.reshape(1, D_MODEL))
    return hout, ah.reshape(bsz, D_A), ac, bs, bc


def _ab_sample_rows_kernel(ax_ref, ag_ref, xbc_ref, dt_ref, sah_ref, sac_ref, sbc_ref,
                           acw_ref, acb_ref, wr_ref, br_ref, wi_ref, bi_ref, lam_ref,
                           bcw_ref, bcb_ref, dtb_ref,
                           aout_ref, ah_ref, ac_ref, bc_ref, xact_ref, dts_ref):
    def conv1(x, buf_ref, w_ref, b_ref, nbuf_ref, width):
        y = b_ref[...] + w_ref[CONV_W - 1:CONV_W, :] * x
        for k in range(CONV_W - 1):
            y = y + w_ref[k:k + 1, :] * buf_ref[:, k * width:(k + 1) * width]
        for k in range(CONV_W - 2):
            nbuf_ref[:, k * width:(k + 1) * width] = buf_ref[:, (k + 1) * width:(k + 2) * width]
        nbuf_ref[:, (CONV_W - 2) * width:(CONV_W - 1) * width] = x
        return y

    xc = conv1(ax_ref[...], sac_ref, acw_ref, acb_ref, ac_ref, D_A)
    a, u = _lru_gates(xc, wr_ref, br_ref, wi_ref, bi_ref, lam_ref)
    h = a * sah_ref[...] + u
    ah_ref[...] = h
    aout_ref[...] = h * _silu_half(ag_ref[...])
    xact_ref[...] = _silu_half(conv1(xbc_ref[...], sbc_ref, bcw_ref, bcb_ref, bc_ref, CONV_DIM_B))
    dts_ref[...] = jax.nn.softplus(dt_ref[...] + dtb_ref[...])


def _pad_rows_t(x):
    pad = jnp.zeros((LANE - x.shape[0], x.shape[1]), F32)
    return jnp.concatenate([x, pad], axis=0).T


def _ab_sample_pairs(s_ref, bx_ref, b_ref, c_ref, dts_ref, alog_ref, so_ref, y_ref, head0):
    bb = SAMPLE_BB
    dts = dts_ref[...]
    dec_t = _pad_rows_t(jnp.exp(dts * (-jnp.exp(alog_ref[...]))))
    dts_t = _pad_rows_t(dts)
    for j in range(bx_ref.shape[1] // LANE):
        h0 = head0 + 2 * j
        xt = _pad_rows_t(bx_ref[:, j * LANE:(j + 1) * LANE])
        dtp = jnp.concatenate([jnp.broadcast_to(dts_t[h0:h0 + 1, :], (HD_B, LANE)),
                               jnp.broadcast_to(dts_t[h0 + 1:h0 + 2, :], (HD_B, LANE))], axis=0)
        xdt = xt * dtp
        for i in range(bb):
            brow = jnp.broadcast_to(b_ref[i:i + 1, :], (2 * HD_B, N_B))
            crow = jnp.broadcast_to(c_ref[i:i + 1, :].astype(BF16).astype(F32), (2 * HD_B, N_B))
            upd = jnp.broadcast_to(xdt[:, i:i + 1], (2 * HD_B, N_B)) * brow
            news = []
            for hh in range(2):
                dec = jnp.broadcast_to(dec_t[h0 + hh:h0 + hh + 1, i:i + 1], (HD_B, N_B))
                sn = dec * s_ref[i, 2 * j + hh] + upd[hh * HD_B:(hh + 1) * HD_B, :]
                so_ref[i, 2 * j + hh] = sn
                news.append(sn)
            prod = jnp.concatenate(news, axis=0).astype(BF16).astype(F32) * crow
            y_ref[i:i + 1, j * LANE:(j + 1) * LANE] = jnp.sum(prod.T, axis=0, keepdims=True)


def _ab_sample_state_kernel(s_ref, xact_ref, dts_ref, alog_ref, so_ref, y_ref):
    hpg = H_B // G_B
    gw = D_B // G_B
    for g in range(G_B):
        heads = pl.ds(g * hpg, hpg)
        _ab_sample_pairs(s_ref.at[:, heads], xact_ref.at[:, pl.ds(g * gw, gw)],
                         xact_ref.at[:, pl.ds(D_B + g * N_B, N_B)],
                         xact_ref.at[:, pl.ds(D_B + G_B * N_B + g * N_B, N_B)],
                         dts_ref, alog_ref, so_ref.at[:, heads], y_ref.at[:, pl.ds(g * gw, gw)], g * hpg)


def _ab_sample_finish_kernel(y_ref, xact_ref, z_ref, dexp_ref, bnw_ref, bout_ref):
    bout_ref[...] = _group_norm_gate(y_ref[...], xact_ref[:, 0:D_B], z_ref[...], dexp_ref, bnw_ref)


def _ab_sample_rows(u, dtr, s_ah, s_ac, s_bc, w):
    bsz = u.shape[0]
    full = lambda shape: pl.BlockSpec(shape, lambda i: tuple(0 for _ in shape))
    cw = CONV_W - 1
    aout, ah, ac, bc, xact, dts = pl.pallas_call(
        _ab_sample_rows_kernel, grid=(1,),
        in_specs=[pl.BlockSpec((bsz, D_A), lambda i: (0, 0)), pl.BlockSpec((bsz, D_A), lambda i: (0, 1)),
                  pl.BlockSpec((bsz, CONV_DIM_B), lambda i: (0, 2)), full((bsz, LANE)),
                  full((bsz, D_A)), full((bsz, cw * D_A)), full((bsz, cw * CONV_DIM_B)),
                  full((CONV_W, D_A)), full((1, D_A)),
                  full((A_BLOCKS, A_BLK, A_BLK)), full((1, D_A)),
                  full((A_BLOCKS, A_BLK, A_BLK)), full((1, D_A)), full((1, D_A)),
                  full((CONV_W, CONV_DIM_B)), full((1, CONV_DIM_B)), full((1, LANE))],
        out_specs=[full((bsz, D_A)), full((bsz, D_A)), full((bsz, cw * D_A)), full((bsz, cw * CONV_DIM_B)),
                   full((bsz, CONV_DIM_B)), full((bsz, LANE))],
        out_shape=[jax.ShapeDtypeStruct((bsz, D_A), F32), jax.ShapeDtypeStruct((bsz, D_A), F32),
                   jax.ShapeDtypeStruct((bsz, cw * D_A), F32), jax.ShapeDtypeStruct((bsz, cw * CONV_DIM_B), F32),
                   jax.ShapeDtypeStruct((bsz, CONV_DIM_B), F32), jax.ShapeDtypeStruct((bsz, LANE), F32)],
        compiler_params=_params("arbitrary"),
        name="ab_sample_rows",
    )(u, u, u, dtr, s_ah, s_ac.reshape(bsz, cw * D_A), s_bc.reshape(bsz, cw * CONV_DIM_B),
      w["acw"], w["acb"], w["wr"], w["br"], w["wi"], w["bi"], w["lam"], w["bcw"], w["bcb"], w["dtb"])
    return aout, ah, ac.reshape(bsz, cw, D_A), bc.reshape(bsz, cw, CONV_DIM_B), xact, dts


def _ab_sample_state(s_bs, xact, dts, w):
    bsz = xact.shape[0]
    bb = SAMPLE_BB
    assert bsz % bb == 0
    return pl.pallas_call(
        _ab_sample_state_kernel, grid=(bsz // bb,),
        in_specs=[pl.BlockSpec((bb, H_B, HD_B, N_B), lambda i: (i, 0, 0, 0)),
                  pl.BlockSpec((bb, CONV_DIM_B), lambda i: (i, 0)),
                  pl.BlockSpec((bb, LANE), lambda i: (i, 0)),
                  pl.BlockSpec((1, LANE), lambda i: (0, 0))],
        out_specs=[pl.BlockSpec((bb, H_B, HD_B, N_B), lambda i: (i, 0, 0, 0)),
                   pl.BlockSpec((bb, D_B), lambda i: (i, 0))],
        out_shape=[jax.ShapeDtypeStruct(s_bs.shape, F32), jax.ShapeDtypeStruct((bsz, D_B), F32)],
        compiler_params=_params("parallel"),
        name="ab_sample_state",
    )(s_bs, xact, dts, w["alog"])


def _ab_sample_finish(y, xact, u, w):
    bsz = y.shape[0]
    full = lambda shape: pl.BlockSpec(shape, lambda i: (0, 0))
    return pl.pallas_call(
        _ab_sample_finish_kernel, grid=(1,),
        in_specs=[full((bsz, D_B)), full((bsz, CONV_DIM_B)),
                  pl.BlockSpec((bsz, D_B), lambda i: (0, 2)),
                  full((1, D_B)), full((1, D_B))],
        out_specs=full((bsz, D_B)),
        out_shape=jax.ShapeDtypeStruct((bsz, D_B), F32),
        compiler_params=_params("arbitrary"),
        name="ab_sample_finish",
    )(y, xact, u, w["dexp"], w["bnw"])


def _hg_lower_bound(clb, layer):
    mx = jnp.max(clb, axis=0, keepdims=True)
    ex = jnp.exp(clb - mx)
    return jnp.sum(ex[1:layer + 1], axis=0, keepdims=True) / jnp.sum(ex, axis=0, keepdims=True)


def _hg_gates(fx_half, lb):
    f = 0.5 * (1.0 + lb) + (0.5 * (1.0 - lb)) * jnp.tanh(fx_half)
    return f, 1.0 - f


def _hg_out(o, gate_half, cnw):
    return o * lax.rsqrt(jnp.mean(o * o, axis=-1, keepdims=True) + EPS) * cnw * (gate_half + gate_half * jnp.tanh(gate_half))


def _hg_gamma():
    import numpy as np
    q = HG_CHUNK
    t = np.arange(q)[:, None]
    tau = np.arange(q)[None, :]
    mats = [(tau <= t)]
    for l in range(1, HG_MXU_LEVELS):
        w = 1 << l
        ref = (t // (2 * w)) * (2 * w) + w - 1
        upper = (t % (2 * w)) >= w
        mats.append(np.where(upper, (tau > ref) & (tau <= t), (tau > t) & (tau <= ref)))
    gam = np.concatenate(mats, axis=0).astype(np.float32)
    return jnp.asarray(np.concatenate([gam, gam], axis=1), dtype=BF16)


def _hg_level_table():
    import numpy as np
    q = HG_CHUNK
    t = np.arange(q)[:, None]
    s = np.arange(q)[None, :]
    x = t ^ s
    lvl = np.floor(np.log2(np.maximum(x, 1))).astype(np.int32)
    return jnp.asarray(np.where(t > s, lvl, -1).astype(np.int32))


def _c_prompt_kernel(q_ref, f_ref, v_ref, g_ref, clb_ref, cnw_ref, gam_ref, lvl_ref,
                     og_ref, cs_ref, st_ref, *, layer):
    c = pl.program_id(1)
    last = pl.num_programs(1) - 1
    qc = HG_CHUNK

    @pl.when(c == 0)
    def _():
        st_ref[...] = jnp.zeros_like(st_ref)

    gam = gam_ref[...]
    ntile = qc // SUBLANE
    sub = lax.broadcasted_iota(jnp.int32, (SUBLANE, DK_C), 0)
    sub_levels = HG_MXU_LEVELS
    sub_upper = [(sub & (1 << l)) != 0 for l in range(sub_levels)]

    def tiles(x):
        return [x[i * SUBLANE:(i + 1) * SUBLANE, :] for i in range(ntile)]

    def gate_split(hd, rows):
        f, kk = _hg_gates(f_ref[hd, rows, :].astype(F32), _hg_lower_bound(clb_ref[hd], layer))
        g = jnp.log(jnp.maximum(f, HG_F_MIN)) * LOG2_E
        g1 = g.astype(BF16)
        g2 = (g - g1.astype(F32)).astype(BF16)
        return (f, kk), jnp.concatenate([g1, g2], axis=0)

    def scores(hd, rows, fk, sums):
        f, kk = fk
        qh = q_ref[hd, rows, :].astype(F32) * (DK_C ** -0.5)
        bcum = sums[0:qc]
        st = st_ref[hd]
        o = _dot_tr((qh * jnp.exp2(bcum)).astype(BF16), st.astype(BF16))
        qt, kt, ft, bt = tiles(qh), tiles(kk), tiles(f), tiles(bcum)
        prods = []
        for l in range(HG_LEVELS):
            if l == 0:
                xt = [jnp.where(sub_upper[0], qt[i] * ft[i], kt[i]) for i in range(ntile)]
            elif l < HG_MXU_LEVELS:
                dec = tiles(jnp.exp2(sums[l * qc:(l + 1) * qc]))
                xt = [jnp.where(sub_upper[l], qt[i], kt[i]) * dec[i] for i in range(ntile)]
            else:
                wt = 1 << (l - HG_MXU_LEVELS)
                xt = []
                for blk in range(0, ntile, 2 * wt):
                    ref = (blk + wt) * SUBLANE - 1
                    bref = jnp.broadcast_to(bcum[ref:ref + 1, :], (SUBLANE, DK_C))
                    xt += [kt[i] * jnp.exp2(bref - bt[i]) for i in range(blk, blk + wt)]
                    xt += [qt[i] * jnp.exp2(bt[i] - bref) for i in range(blk + wt, blk + 2 * wt)]
            x = jnp.concatenate(xt, axis=0).astype(BF16)
            half = (1 << l) // BF16_ROWS
            if half == 0:
                p = tiles(_dot_tr(x, x))
                prods.append({i: p[i] for i in range(ntile)})
            else:
                ups = [r for r in range(qc // BF16_ROWS) if (r // half) & 1]
                pu = _dot_tr(jnp.concatenate([x[r * BF16_ROWS:(r + 1) * BF16_ROWS, :] for r in ups], axis=0), x)
                tpr = BF16_ROWS // SUBLANE
                prods.append({r * tpr + k: pu[(n * tpr + k) * SUBLANE:(n * tpr + k + 1) * SUBLANE, :]
                              for n, r in enumerate(ups) for k in range(tpr)})
        return qh, st, o, prods

    def level_masks():
        masks = {}
        for i in range(ntile):
            lv = lvl_ref[i * SUBLANE:(i + 1) * SUBLANE, :]
            for l in range(HG_LEVELS):
                if l < sub_levels or (i >> (l - sub_levels)) & 1:
                    masks[i, l] = lv == l
        return masks

    def combine(hd, rows, kk, bcum, qh, st, o, prods, masks):
        arows = []
        for i in range(ntile):
            a = jnp.zeros((SUBLANE, qc), F32)
            for l in range(HG_LEVELS):
                if (i, l) in masks:
                    a = jnp.where(masks[i, l], prods[l][i], a)
            arows.append(a)
        amat = jnp.concatenate(arows, axis=0)
        vb = v_ref[hd, rows, :].astype(BF16)
        o = o + _dot(amat.astype(BF16), vb) + jnp.sum(qh * kk, axis=-1, keepdims=True) * vb.astype(F32)
        blast = bcum[qc - 1:qc, :]
        kdec = (kk * jnp.exp2(blast - bcum)).astype(BF16)
        st_ref[hd] = st * jnp.exp2(blast) + _dot_tl(vb, kdec)
        return o

    nchunk = HG_BLOCK // qc

    def body(idx, carry):
        hg = idx // nchunk
        rows = pl.ds(pl.multiple_of((idx % nchunk) * qc, qc), qc)
        heads = [hg * HG_UNROLL + k for k in range(HG_UNROLL)]
        gs = [gate_split(hd, rows) for hd in heads]
        sums = _dot(gam, jnp.concatenate([s for _, s in gs], axis=1))
        sums = [sums[:, k * DK_C:(k + 1) * DK_C] for k in range(HG_UNROLL)]
        sc = [scores(hd, rows, gs[k][0], sums[k]) for k, hd in enumerate(heads)]
        masks = level_masks()
        outs = [combine(hd, rows, gs[k][0][1], sums[k][0:qc], *sc[k], masks) for k, hd in enumerate(heads)]
        for k, hd in enumerate(heads):
            og_ref[hd, rows, :] = _hg_out(outs[k], g_ref[hd, rows, :].astype(F32), cnw_ref[hd]).astype(BF16)
        return carry

    lax.fori_loop(0, (H_C // HG_UNROLL) * nchunk, body, 0)

    @pl.when(c == last)
    def _():
        for hd in range(H_C):
            cs_ref[0, hd] = st_ref[hd].T


def _c_prompt(u, bsz, seq, w, layer):
    tb = HG_BLOCK
    assert seq % tb == 0 and tb % HG_CHUNK == 0 and (1 << HG_LEVELS) == HG_CHUNK
    nc = seq // tb
    m = bsz * seq
    depth = w["clb"].shape[1]

    def part(k):
        return pl.BlockSpec((H_C, tb, LANE), lambda b, c: (k, b * nc + c, 0))

    c2 = lambda b, c: (0, 0)
    c3 = lambda b, c: (0, 0, 0)
    og, cs = pl.pallas_call(
        functools.partial(_c_prompt_kernel, layer=layer), grid=(bsz, nc),
        in_specs=[part(0), part(1), part(2), part(3),
                  pl.BlockSpec((H_C, depth, DK_C), c3), pl.BlockSpec((H_C, 1, DV_C), c3),
                  pl.BlockSpec(w["gam"].shape, c2), pl.BlockSpec(w["lvl"].shape, c2)],
        out_specs=[pl.BlockSpec((H_C, tb, LANE), lambda b, c: (0, b * nc + c, 0)),
                   pl.BlockSpec((1, H_C, DK_C, DV_C), lambda b, c: (b, 0, 0, 0))],
        out_shape=[jax.ShapeDtypeStruct((H_C, m, DV_C), BF16),
                   jax.ShapeDtypeStruct((bsz, H_C, DK_C, DV_C), F32)],
        scratch_shapes=[pltpu.VMEM((H_C, DV_C, DK_C), F32)],
        compiler_params=_params("parallel", "arbitrary"),
        name="c_prompt",
    )(u, u, u, u, w["clb"], w["cnw"], w["gam"], w["lvl"])
    return og, cs


def _c_sample_head(hd, q_ref, f_ref, v_ref, g_ref, s_ref, clb_ref, cnw_ref, og_ref, so_ref, layer):
    bb = SAMPLE_BB
    lane = lax.broadcasted_iota(jnp.int32, (DK_C, LANE), 1)
    first_rows = lax.broadcasted_iota(jnp.int32, (LANE, DV_C), 0) < bb
    lb = _hg_lower_bound(clb_ref[hd], layer)
    f, kk = _hg_gates(f_ref[hd], lb)
    f_t = _pad_rows_t(f)
    k_t = _pad_rows_t(kk)
    qs = q_ref[hd] * (DK_C ** -0.5)
    v = v_ref[hd]
    vpad = jnp.where(first_rows, jnp.tile(v, (LANE // bb, 1)), 0.0).astype(BF16)
    orows = []
    for i in range(bb):
        fcol = jnp.broadcast_to(f_t[:, i:i + 1], (DK_C, DV_C))
        kv = _dot(jnp.where(lane == i, k_t, 0.0).astype(BF16), vpad)
        sn = fcol * s_ref[i, hd] + kv
        so_ref[i, hd] = sn
        qrow = jnp.broadcast_to(qs[i:i + 1, :], (2 * SUBLANE, DK_C)).astype(BF16)
        orows.append(_dot(qrow, sn.astype(BF16))[0:1, :])
    o = jnp.concatenate(orows, axis=0)
    og_ref[hd] = _hg_out(o, g_ref[hd], cnw_ref[hd])


def _c_sample_kernel(q_ref, f_ref, v_ref, g_ref, s_ref, clb_ref, cnw_ref, og_ref, so_ref, *, layer):
    def head(hd, carry):
        _c_sample_head(hd, q_ref, f_ref, v_ref, g_ref, s_ref, clb_ref, cnw_ref, og_ref, so_ref, layer)
        return carry

    lax.fori_loop(0, H_C, head, 0, unroll=SAMPLE_UNROLL)


def _c_sample(u, s_c, w, layer):
    bsz = s_c.shape[0]
    bb = SAMPLE_BB
    assert bsz % bb == 0
    depth = w["clb"].shape[1]

    def part(k):
        return pl.BlockSpec((H_C, bb, LANE), lambda i: (k, i, 0))

    c3 = lambda i: (0, 0, 0)
    og, so = pl.pallas_call(
        functools.partial(_c_sample_kernel, layer=layer), grid=(bsz // bb,),
        in_specs=[part(0), part(1), part(2), part(3),
                  pl.BlockSpec((bb, H_C, DK_C, DV_C), lambda i: (i, 0, 0, 0)),
                  pl.BlockSpec((H_C, depth, DK_C), c3), pl.BlockSpec((H_C, 1, DV_C), c3)],
        out_specs=[pl.BlockSpec((H_C, bb, LANE), lambda i: (0, i, 0)),
                   pl.BlockSpec((bb, H_C, DK_C, DV_C), lambda i: (i, 0, 0, 0))],
        out_shape=[jax.ShapeDtypeStruct((H_C, bsz, DV_C), F32), jax.ShapeDtypeStruct(s_c.shape, F32)],
        compiler_params=_params("parallel"),
        name="c_sample",
    )(u, u, u, u, s_c, w["clb"], w["cnw"])
    return og, so


def _out_proj_c_sample_kernel(mix_ref, h_ref, p_ref, wo_ref, wg_ref, wp_ref, nf_ref,
                              q_ref, f_ref, v_ref, g_ref, s_ref, clb_ref, cnw_ref,
                              o_ref, og_ref, so_ref, *, final, layer):
    for hd in range(q_ref.shape[0]):
        _c_sample_head(hd, q_ref, f_ref, v_ref, g_ref, s_ref, clb_ref, cnw_ref, og_ref, so_ref, layer)
    _out_proj_kernel(mix_ref, h_ref, p_ref, wo_ref, wg_ref, wp_ref, nf_ref, o_ref, head_major=True, final=final)


def _out_proj_c_sample(mix, h, p, layer, wo, wg, wp, nf, u_s, s_c, w, *, final):
    m, d = h.shape
    bsz = s_c.shape[0]
    bb, split = SAMPLE_BB, FUSED_HEAD_SPLIT
    steps = (bsz // bb) * split
    hp = H_C // split
    tm = m // steps
    assert bsz % bb == 0 and H_C % split == 0 and m % steps == 0 and tm % BF16_ROWS == 0
    depth = w["clb"].shape[1]
    const = lambda i: (0, 0)

    def part(k):
        return pl.BlockSpec((hp, bb, LANE), lambda i: (k * split + i % split, i // split, 0))

    state = pl.BlockSpec((bb, hp, DK_C, DV_C), lambda i: (i // split, i % split, 0, 0))
    hrow = lambda i: (i % split, 0, 0)
    hout, og, so = pl.pallas_call(
        functools.partial(_out_proj_c_sample_kernel, final=final, layer=layer),
        grid=(steps,),
        in_specs=[pl.BlockSpec((mix.shape[0], tm, LANE), lambda i: (0, i, 0)),
                  pl.BlockSpec((tm, d), lambda i: (i, 0)),
                  pl.BlockSpec((None, tm, p.shape[2]), lambda i: (layer, i, 0)),
                  pl.BlockSpec(wo.shape, const), pl.BlockSpec(wg.shape, const),
                  pl.BlockSpec(wp.shape, const), pl.BlockSpec((1, d), const),
                  part(0), part(1), part(2), part(3), state,
                  pl.BlockSpec((hp, depth, DK_C), hrow), pl.BlockSpec((hp, 1, DV_C), hrow)],
        out_specs=[pl.BlockSpec((tm, d), lambda i: (i, 0)),
                   pl.BlockSpec((hp, bb, LANE), lambda i: (i % split, i // split, 0)),
                   state],
        out_shape=[jax.ShapeDtypeStruct((m, d), F32),
                   jax.ShapeDtypeStruct((H_C, bsz, DV_C), F32), jax.ShapeDtypeStruct(s_c.shape, F32)],
        compiler_params=_params("arbitrary"),
        name="out_proj_c_sample",
    )(mix, h, p, wo, wg, wp, nf.reshape(1, d), u_s, u_s, u_s, u_s, s_c, w["clb"], w["cnw"])
    return hout, og, so


def _row(v, width=None):
    v = v.astype(F32).reshape(1, -1)
    if width is not None and v.shape[1] < width:
        v = jnp.pad(v, ((0, 0), (0, width - v.shape[1])))
    return v


def kernel(x_prompt, x_sample, p_prompt, p_sample, state_a_h, state_a_conv, state_b_ssm, state_b_conv, state_c,
           norm_w, norm_f, ab_w_in, a_conv_w, a_conv_b, a_w_r, a_b_r, a_w_i, a_b_i, a_lam, b_conv_w, b_conv_b,
           b_dt_bias, b_a_log, b_d, b_norm_w, ab_w_out, c_w_in, c_lb, c_norm_w, c_w_out, ple_proj, ple_gate):
    depth = norm_w.shape[0]
    bp, seq, _ = x_prompt.shape
    bs = x_sample.shape[0]
    hp = x_prompt.reshape(bp * seq, D_MODEL)
    hs = x_sample.reshape(bs, D_MODEL)
    pp = p_prompt.reshape(depth, bp * seq, D_PLE)
    ps = p_sample.reshape(depth, bs, D_PLE)
    gam, lvl = _hg_gamma(), _hg_level_table()
    clb = c_lb.astype(F32).reshape(depth, H_C, DK_C).transpose(1, 0, 2)

    c_col_scale = jnp.concatenate([jnp.ones((HK_C,), F32), jnp.full((HK_C,), 0.5, F32),
                                   jnp.ones((D_C,), F32), jnp.full((D_C,), 0.5, F32)])

    ah_p, ac_p, bs_p, bc_p, c_p = [], [], [], [], []
    ah_s, ac_s, bs_s, bc_s, c_s = [], [], [], [], []
    next_u = next_w_in = None
    for i in range(depth):
        j = i // 2
        final = i == depth - 1
        wg = (0.5 * ple_gate[i]).astype(BF16)
        wp = (0.5 * ple_proj[i]).astype(BF16)
        if i % 2 == 0:
            col_scale = jnp.concatenate([jnp.ones((D_A,), F32), jnp.full((D_A + D_B,), 0.5, F32),
                                         jnp.ones((CONV_DIM_B,), F32)])
            w_ab_t = ab_w_in[j].T
            w_ab = (w_ab_t[:AB_MAIN] * col_scale[:, None]).astype(BF16)
            w_dt = jnp.pad(w_ab_t[AB_MAIN:], ((0, LANE - H_B), (0, 0))).astype(BF16)
            wo = ab_w_out[j].astype(BF16)
            w = dict(acw=a_conv_w[j].astype(F32), acb=_row(a_conv_b[j]),
                     wr=(0.5 * a_w_r[j]).astype(BF16), br=0.5 * _row(a_b_r[j]),
                     wi=a_w_i[j].astype(BF16), bi=_row(a_b_i[j]),
                     lam=_row(a_lam[j]), bcw=0.5 * b_conv_w[j].astype(F32), bcb=0.5 * _row(b_conv_b[j]),
                     dtb=_row(b_dt_bias[j], LANE), alog=_row(b_a_log[j], LANE),
                     dexp=_row(jnp.repeat(b_d[j], HD_B)), bnw=_row(b_norm_w[j]))
            us, dtrs = _in_proj(hs, norm_w[i], w_ab, w_dt, w_t=True)
            aout, s1, s2, s4, xact, dts = _ab_sample_rows(us, dtrs, state_a_h[j], state_a_conv[j], state_b_conv[j], w)
            ah_s.append(s1); ac_s.append(s2); bc_s.append(s4)
            if i + 1 < depth:
                u, dtr, next_w_in = _in_proj(hp, norm_w[i], w_ab, w_dt, w_t=True,
                                             prep=(c_w_in[(i + 1) // 2], c_col_scale))
            else:
                u, dtr = _in_proj(hp, norm_w[i], w_ab, w_dt, w_t=True)
            hp, s1, s2, s3, s4 = _ab_prompt(u, dtr, hp, pp, i, wo, wg, wp, norm_f, bp, seq, w, final=final)
            ah_p.append(s1); ac_p.append(s2); bs_p.append(s3); bc_p.append(s4)
            if i + 1 < depth:
                next_u, bs_new, y = _in_proj(hp, norm_w[i + 1], next_w_in, head_major=True, head_major_dtype=BF16,
                                             sample=(state_b_ssm[j], xact, dts), alog=w["alog"])
            else:
                bs_new, y = _ab_sample_state(state_b_ssm[j], xact, dts, w)
            bs_s.append(bs_new)
            mix = jnp.concatenate([aout, _ab_sample_finish(y, xact, us, w)], axis=-1)
            hs = _out_proj(mix, hs, ps, i, wo, wg, wp, norm_f, head_major=False, final=final)
        else:
            w_in = next_w_in if next_w_in is not None else (c_w_in[j] * c_col_scale).astype(BF16)
            wo = c_w_out[j].astype(BF16)
            w = dict(clb=clb, cnw=c_norm_w[j].astype(F32).reshape(H_C, 1, DV_C), gam=gam, lvl=lvl)
            u = next_u if next_u is not None else _in_proj(hp, norm_w[i], w_in, head_major=True,
                                                           head_major_dtype=BF16)
            next_u = next_w_in = None
            og, s1 = _c_prompt(u, bp, seq, w, i)
            c_p.append(s1)
            u = _in_proj(hs, norm_w[i], w_in, head_major=True)
            steps = (bs // SAMPLE_BB) * FUSED_HEAD_SPLIT
            if bs % SAMPLE_BB == 0 and (bp * seq) % (steps * BF16_ROWS) == 0 and (bp * seq) // steps <= OUT_PROJ_TM:
                hp, og, s1 = _out_proj_c_sample(og, hp, pp, i, wo, wg, wp, norm_f, u, state_c[j], w, final=final)
            else:
                hp = _out_proj(og, hp, pp, i, wo, wg, wp, norm_f, head_major=True, final=final)
                og, s1 = _c_sample(u, state_c[j], w, i)
            c_s.append(s1)
            hs = _out_proj(og, hs, ps, i, wo, wg, wp, norm_f, head_major=True, final=final)
    return (hp.reshape(bp, seq, D_MODEL), hs.reshape(bs, 1, D_MODEL),
            jnp.stack(ah_p), jnp.stack(ac_p), jnp.stack(bs_p), jnp.stack(bc_p), jnp.stack(c_p),
            jnp.stack(ah_s), jnp.stack(ac_s), jnp.stack(bs_s), jnp.stack(bc_s), jnp.stack(c_s))
```

```python
import functools

import jax
import jax.numpy as jnp
from jax import lax
from jax.experimental import pallas as pl
from jax.experimental.pallas import tpu as pltpu

F32 = jnp.float32
BF16 = jnp.bfloat16

D_MODEL = 1024
D_PLE = 256
EPS = 1e-6
CONV_W = 4
D_A = D_MODEL
A_BLOCKS = 8
A_BLK = D_A // A_BLOCKS
LRU_C = 8.0
D_B = D_MODEL
HD_B = 64
H_B = D_B // HD_B
N_B = 128
G_B = 2
CONV_DIM_B = D_B + 2 * G_B * N_B
D_C = 2 * D_MODEL
H_C = 16
DK_C = 128
DV_C = D_C // H_C
HK_C = H_C * DK_C
AB_MAIN = 2 * D_A + D_B + CONV_DIM_B
IN_C = 2 * HK_C + 2 * D_C

LANE = 128
SUBLANE = 8
BF16_ROWS = 16
LOG2_E = 1.4426950408889634
VMEM_LIMIT = 56 * 1024 * 1024

PROJ_TM = 2048
PROJ_VMEM_BUDGET = 46 * 1024 * 1024
PROJ_CHUNK_COLS = 256
PROJ_MIN_STEPS = 4
HOST_VMEM_EXTRA = 6 * 1024 * 1024
OUT_PROJ_TM = 1024
SSD_CHUNK = 128
AB_CHUNKS = 2
HG_CHUNK = 64
HG_LEVELS = 6
HG_MXU_LEVELS = 3
HG_F_MIN = 1e-30
HG_BLOCK = 512
HG_UNROLL = 16
SAMPLE_BB = 8
SAMPLE_UNROLL = 4
FUSED_HEAD_SPLIT = 2

_DN_TR = (((1,), (1,)), ((), ()))
_DN_TL = (((0,), (0,)), ((), ()))


def _dot(a, b):
    return jnp.dot(a, b, preferred_element_type=F32)


def _dot_tr(a, b):
    return lax.dot_general(a, b, _DN_TR, preferred_element_type=F32)


def _dot_tl(a, b):
    return lax.dot_general(a, b, _DN_TL, preferred_element_type=F32)


def _silu_half(x_half):
    return x_half + x_half * jnp.tanh(x_half)


def _rmsnorm(x, w):
    return x * lax.rsqrt(jnp.mean(x * x, axis=-1, keepdims=True) + EPS) * w


def _params(*sem, vmem_limit=VMEM_LIMIT):
    return pltpu.CompilerParams(dimension_semantics=sem, vmem_limit_bytes=vmem_limit)


def _in_proj_kernel(x_ref, nw_ref, w_ref, *rest, has_extra, head_major, w_t, fused, prep):
    rest = list(rest)
    wx_ref = rest.pop(0) if has_extra else None
    sample_in = [rest.pop(0) for _ in range(6)] if fused else None
    prep_in = [rest.pop(0) for _ in range(2)] if prep else None
    o_ref = rest.pop(0)
    ox_ref = rest.pop(0) if has_extra else None
    sample_out = [rest.pop(0) for _ in range(2)] if fused else None
    prep_ref = rest.pop(0) if prep else None
    xn_ref, = rest
    dot = _dot_tr if w_t else _dot

    @pl.when(pl.program_id(1) == 0)
    def _():
        xn_ref[...] = _rmsnorm(x_ref[...], nw_ref[...]).astype(BF16)
        if has_extra:
            ox_ref[...] = dot(xn_ref[...], wx_ref[...])

    if fused:
        _ab_sample_pairs(*sample_in, *sample_out, 0)
    if prep:
        prep_ref[...] = (prep_in[0][...] * prep_in[1][...]).astype(BF16)
    if head_major and not w_t and w_ref.shape[1] % PROJ_CHUNK_COLS == 0:
        per = PROJ_CHUNK_COLS // LANE
        for c in range(w_ref.shape[1] // PROJ_CHUNK_COLS):
            acc = dot(xn_ref[...], w_ref[:, c * PROJ_CHUNK_COLS:(c + 1) * PROJ_CHUNK_COLS])
            for k in range(per):
                o_ref[c * per + k] = acc[:, k * LANE:(k + 1) * LANE].astype(o_ref.dtype)
        return
    acc = dot(xn_ref[...], w_ref[...])
    if head_major:
        for k in range(acc.shape[1] // LANE):
            o_ref[k] = acc[:, k * LANE:(k + 1) * LANE].astype(o_ref.dtype)
    else:
        o_ref[...] = acc


def _sample_ssd_split(sbsz, steps):
    if sbsz % SAMPLE_BB or steps % (sbsz // SAMPLE_BB):
        return 0
    split = steps // (sbsz // SAMPLE_BB)
    if H_B % split or (H_B // split) % 2 or (H_B // G_B) % (H_B // split):
        return 0
    return split


def _sample_ssd_operands(sample, alog, split, step):
    s_bs, xact, dts = sample
    sbsz, bb = s_bs.shape[0], SAMPLE_BB
    hps = H_B // split
    regroup = lambda v: jnp.pad(v[:, :H_B].reshape(-1, split, hps).transpose(1, 0, 2),
                                ((0, 0), (0, 0), (0, LANE - hps)))
    blk = lambda *g: step(*g) // split
    grp = lambda *g: step(*g) % split
    bc_group = lambda *g: (grp(*g) * hps) // (H_B // G_B)
    state = pl.BlockSpec((bb, hps, HD_B, N_B), lambda *g: (blk(*g), grp(*g), 0, 0))
    xs = pl.BlockSpec((bb, hps * HD_B), lambda *g: (blk(*g), grp(*g)))
    in_specs = [state, xs,
                pl.BlockSpec((bb, N_B), lambda *g: (blk(*g), D_B // N_B + bc_group(*g))),
                pl.BlockSpec((bb, N_B), lambda *g: (blk(*g), D_B // N_B + G_B + bc_group(*g))),
                pl.BlockSpec((None, bb, LANE), lambda *g: (grp(*g), blk(*g), 0)),
                pl.BlockSpec((None, 1, LANE), lambda *g: (grp(*g), 0, 0))]
    args = [s_bs, xact, xact, xact, regroup(dts), regroup(alog)]
    out_shape = [jax.ShapeDtypeStruct(s_bs.shape, F32), jax.ShapeDtypeStruct((sbsz, D_B), F32)]
    return in_specs, args, out_shape, [state, xs]


def _in_proj(x, nw, w, w_extra=None, *, head_major=False, w_t=False, head_major_dtype=F32, sample=None, alog=None,
             prep=None):
    m, k = x.shape
    extra_cols = w_extra is not None
    n = w.shape[0 if w_t else 1]
    tm = min(m, PROJ_TM)
    out_bytes = jnp.dtype(head_major_dtype if head_major else F32).itemsize

    def split_of(c):
        return _sample_ssd_split(sample[0].shape[0], (m // tm) * (n // c)) if sample is not None else 0

    def vmem(c):
        extra = 2 * k * LANE * 2 + 2 * tm * LANE * 4 if extra_cols else 0
        ssd = 4 * SAMPLE_BB * (H_B // split_of(c)) * HD_B * N_B * 4 if split_of(c) else 0
        return 2 * tm * k * 4 + tm * k * 2 + 2 * k * c * 2 + 2 * tm * c * out_bytes + extra + ssd

    headroom = HOST_VMEM_EXTRA if sample is not None else 0
    fits = [c for c in range(n, 0, -LANE) if n % c == 0 and vmem(c) <= PROJ_VMEM_BUDGET + headroom]
    if m // tm == 1:
        fits = [c for c in fits if n // c >= PROJ_MIN_STEPS] or fits
    tn = next((c for c in fits if split_of(c)), fits[0])
    split = split_of(tn)
    assert m % tm == 0 and n % tn == 0 and tn % LANE == 0
    grid = (m // tm, n // tn)
    wspec = (lambda cols, idx: pl.BlockSpec((cols, k), lambda i, j: (idx(j), 0))) if w_t else \
            (lambda cols, idx: pl.BlockSpec((k, cols), lambda i, j: (0, idx(j))))
    in_specs = [pl.BlockSpec((tm, k), lambda i, j: (i, 0)),
                pl.BlockSpec((1, k), lambda i, j: (0, 0)),
                wspec(tn, lambda j: j)]
    args = [x, nw.reshape(1, k), w]
    if head_major:
        out_shape = [jax.ShapeDtypeStruct((n // LANE, m, LANE), head_major_dtype)]
        out_specs = [pl.BlockSpec((tn // LANE, tm, LANE), lambda i, j: (j, i, 0))]
    else:
        out_shape = [jax.ShapeDtypeStruct((m, n), F32)]
        out_specs = [pl.BlockSpec((tm, tn), lambda i, j: (i, j))]
    if extra_cols:
        in_specs.append(wspec(LANE, lambda j: 0))
        args.append(w_extra)
        out_shape.append(jax.ShapeDtypeStruct((m, LANE), F32))
        out_specs.append(pl.BlockSpec((tm, LANE), lambda i, j: (i, 0)))
    step = lambda i, j: i * grid[1] + j
    if split:
        s_in, s_args, s_shape, s_out = _sample_ssd_operands(sample, alog, split, step)
        in_specs += s_in
        args += s_args
        out_shape += s_shape
        out_specs += s_out
    steps = grid[0] * grid[1]
    prep_here = prep is not None and prep[0].shape[0] % (steps * BF16_ROWS) == 0
    if prep_here:
        k2, n2 = prep[0].shape
        in_specs += [pl.BlockSpec((k2 // steps, n2), lambda i, j: (step(i, j), 0)),
                     pl.BlockSpec((1, n2), lambda i, j: (0, 0))]
        args += [prep[0], prep[1].reshape(1, n2)]
        out_shape.append(jax.ShapeDtypeStruct((k2, n2), BF16))
        out_specs.append(pl.BlockSpec((k2 // steps, n2), lambda i, j: (step(i, j), 0)))
        headroom = HOST_VMEM_EXTRA
    outs = pl.pallas_call(
        functools.partial(_in_proj_kernel, has_extra=extra_cols, head_major=head_major, w_t=w_t, fused=bool(split),
                          prep=prep_here),
        grid=grid, in_specs=in_specs, out_specs=out_specs, out_shape=out_shape,
        scratch_shapes=[pltpu.VMEM((tm, k), BF16)],
        compiler_params=_params("arbitrary" if split or prep_here else "parallel", "arbitrary",
                                vmem_limit=VMEM_LIMIT + headroom),
        name="in_proj",
    )(*args)
    outs = list(outs)
    if sample is not None and not split:
        prep_out = [outs.pop()] if prep_here else []
        outs += list(_ab_sample_state(*sample, dict(alog=alog))) + prep_out
    if prep is not None and not prep_here:
        outs.append((prep[0] * prep[1]).astype(BF16))
    return outs if len(outs) > 1 else outs[0]


def _residual_ple(h, p_ref, wg_ref, wp_ref, nf_ref, final):
    gate_t = jnp.tanh(_dot(h.astype(BF16), wg_ref[...]))
    pe_half = _dot(p_ref[...].astype(BF16), wp_ref[...])
    h = h + pe_half + pe_half * gate_t
    return _rmsnorm(h, nf_ref[...]) if final else h


def _out_proj_kernel(mix_ref, h_ref, p_ref, wo_ref, wg_ref, wp_ref, nf_ref, o_ref, *, head_major, final):
    if head_major:
        mix = jnp.concatenate([mix_ref[k] for k in range(mix_ref.shape[0])], axis=-1)
    else:
        mix = mix_ref[...]
    h = h_ref[...] + _dot(mix.astype(BF16), wo_ref[...])
    o_ref[...] = _residual_ple(h, p_ref, wg_ref, wp_ref, nf_ref, final)


def _out_proj(mix, h, p, layer, wo, wg, wp, nf, *, head_major, final):
    m, d = h.shape
    tm = min(m, OUT_PROJ_TM)
    assert m % tm == 0
    if head_major:
        mix_spec = pl.BlockSpec((mix.shape[0], tm, LANE), lambda i: (0, i, 0))
    else:
        mix_spec = pl.BlockSpec((tm, mix.shape[1]), lambda i: (i, 0))
    const = lambda i: (0, 0)
    return pl.pallas_call(
        functools.partial(_out_proj_kernel, head_major=head_major, final=final),
        grid=(m // tm,),
        in_specs=[mix_spec,
                  pl.BlockSpec((tm, d), lambda i: (i, 0)),
                  pl.BlockSpec((None, tm, p.shape[2]), lambda i: (layer, i, 0)),
                  pl.BlockSpec(wo.shape, const), pl.BlockSpec(wg.shape, const),
                  pl.BlockSpec(wp.shape, const), pl.BlockSpec((1, d), const)],
        out_specs=pl.BlockSpec((tm, d), lambda i: (i, 0)),
        out_shape=jax.ShapeDtypeStruct((m, d), F32),
        compiler_params=_params("parallel"),
        name="out_proj",
    )(mix, h, p, wo, wg, wp, nf.reshape(1, d))


def _lru_gates(xc, wr_ref, br_ref, wi_ref, bi_ref, lam_ref):
    xcb = xc.astype(BF16)
    r_parts, i_parts = [], []
    for k in range(A_BLOCKS):
        xk = xcb[:, k * A_BLK:(k + 1) * A_BLK]
        r_parts.append(_dot(xk, wr_ref[k]))
        i_parts.append(_dot(xk, wi_ref[k]))
    tr = jnp.tanh(jnp.concatenate(r_parts, axis=-1) + br_ref[...])
    gi = jax.nn.sigmoid(jnp.concatenate(i_parts, axis=-1) + bi_ref[...])
    half_rate = (-0.5 * LRU_C) * jax.nn.softplus(-lam_ref[...])
    log_a = half_rate + half_rate * tr
    a = jnp.exp(log_a)
    v = jnp.tanh(-log_a) * (a * a + 1.0)
    u = jnp.where(v > 0.0, v * lax.rsqrt(v), 0.0) * (gi * xc)
    return a, u


def _group_norm_gate(y, bx, z_half, dexp_ref, bnw_ref):
    y = (y + dexp_ref[...] * bx) * _silu_half(z_half)
    gw = D_B // G_B
    parts = []
    for g in range(G_B):
        yg = y[:, g * gw:(g + 1) * gw]
        parts.append(yg * lax.rsqrt(jnp.mean(yg * yg, axis=-1, keepdims=True) + EPS))
    return jnp.concatenate(parts, axis=-1) * bnw_ref[...]


def _ab_prompt_kernel(ax_ref, ag_ref, z_ref, xbc_ref, dt_ref,
                      acw_ref, acb_ref, wr_ref, br_ref, wi_ref, bi_ref, lam_ref,
                      bcw_ref, bcb_ref, dtb_ref, alog_ref, dexp_ref, bnw_ref,
                      hin_ref, p_ref, wo_ref, wg_ref, wp_ref, nf_ref,
                      hout_ref, *state_refs, final):
    for k in range(AB_CHUNKS):
        r = pl.ds(k * SSD_CHUNK, SSD_CHUNK)
        _ab_prompt_chunk(ax_ref.at[r], ag_ref.at[r], z_ref.at[r], xbc_ref.at[r], dt_ref.at[r],
                         acw_ref, acb_ref, wr_ref, br_ref, wi_ref, bi_ref, lam_ref,
                         bcw_ref, bcb_ref, dtb_ref, alog_ref, dexp_ref, bnw_ref,
                         hin_ref.at[r], p_ref.at[r], wo_ref, wg_ref, wp_ref, nf_ref,
                         hout_ref.at[r], *state_refs, final=final, first=k == 0, last=k == AB_CHUNKS - 1)


def _ab_prompt_chunk(ax_ref, ag_ref, z_ref, xbc_ref, dt_ref,
                     acw_ref, acb_ref, wr_ref, br_ref, wi_ref, bi_ref, lam_ref,
                     bcw_ref, bcb_ref, dtb_ref, alog_ref, dexp_ref, bnw_ref,
                     hin_ref, p_ref, wo_ref, wg_ref, wp_ref, nf_ref,
                     hout_ref, ah_ref, ac_ref, bs_ref, bc_ref,
                     xpa_ref, xpb_ref, h_ref, s_ref, *, final, first, last):
    c = pl.program_id(1)
    t = SSD_CHUNK
    ntile = t // SUBLANE

    if first:
        @pl.when(c == 0)
        def _():
            xpa_ref[...] = jnp.zeros_like(xpa_ref)
            xpb_ref[...] = jnp.zeros_like(xpb_ref)
            h_ref[...] = jnp.zeros_like(h_ref)
            s_ref[...] = jnp.zeros_like(s_ref)

    def tiles(x):
        return [x[i * SUBLANE:(i + 1) * SUBLANE, :] for i in range(ntile)]

    def conv(x, tail_ref, w_ref, b_ref):
        sub = lax.broadcasted_iota(jnp.int32, (SUBLANE, x.shape[1]), 0)
        xt = [tail_ref[...]] + tiles(x)
        taps = [jnp.broadcast_to(w_ref[k:k + 1, :], (SUBLANE, x.shape[1])) for k in range(CONV_W)]
        bias = jnp.broadcast_to(b_ref[...], (SUBLANE, x.shape[1]))
        acc = [bias + taps[CONV_W - 1] * xt[i + 1] for i in range(ntile)]
        for s in range(1, CONV_W):
            wk = taps[CONV_W - 1 - s]
            for i in range(ntile):
                merged = jnp.where(sub >= SUBLANE - s, xt[i], xt[i + 1])
                acc[i] = acc[i] + wk * pltpu.roll(merged, s, 0)
        tail_ref[...] = xt[ntile]
        return jnp.concatenate(acc, axis=0)

    ax = ax_ref[...]
    xc = conv(ax, xpa_ref, acw_ref, acb_ref)
    a, u = _lru_gates(xc, wr_ref, br_ref, wi_ref, bi_ref, lam_ref)
    sub = lax.broadcasted_iota(jnp.int32, (SUBLANE, D_A), 0)
    at, ut = tiles(a), tiles(u)
    step = 1
    while step < SUBLANE:
        m = sub >= step
        for i in range(ntile):
            ut[i] = jnp.where(m, at[i] * pltpu.roll(ut[i], step, 0) + ut[i], ut[i])
            at[i] = jnp.where(m, at[i] * pltpu.roll(at[i], step, 0), at[i])
        step *= 2
    carry = h_ref[0:1, :]
    hs = []
    for i in range(ntile):
        hs.append(ut[i] + at[i] * carry)
        carry = hs[i][SUBLANE - 1:SUBLANE, :]
    h = jnp.concatenate(hs, axis=0)
    h_ref[0:1, :] = carry
    a_out = (h * _silu_half(ag_ref[...])).astype(BF16)

    xb = xbc_ref[...]
    xbc = _silu_half(conv(xb, xpb_ref, bcw_ref, bcb_ref))
    bx = xbc[:, 0:D_B]
    bxb = bx.astype(BF16)
    dt = jax.nn.softplus(dt_ref[...] + dtb_ref[...])
    adt = dt * (-LOG2_E * jnp.exp(alog_ref[...]))
    ti = lax.broadcasted_iota(jnp.int32, (t, t), 0)
    si = lax.broadcasted_iota(jnp.int32, (t, t), 1)
    causal = ti >= si
    acs = jnp.dot(causal.astype(F32), adt, preferred_element_type=F32,
                  precision=lax.Precision.HIGHEST)
    a_last = acs[t - 1:t, :]
    wq = jnp.exp2(a_last - acs) * dt
    eacs = jnp.exp2(acs)
    ealast = jnp.exp2(a_last)
    acs_t = acs.T
    dt_t = dt.T
    lane = lax.broadcasted_iota(jnp.int32, (t, LANE), 1)
    rowi = lax.broadcasted_iota(jnp.int32, (LANE, N_B), 0)
    hpg = H_B // G_B
    ys = []
    cb = None
    for j in range(H_B // 2):
        g = (2 * j) // hpg
        bg = xbc[:, D_B + g * N_B:D_B + (g + 1) * N_B]
        cg = xbc[:, D_B + G_B * N_B + g * N_B:D_B + G_B * N_B + (g + 1) * N_B]
        if (2 * j) % hpg == 0:
            cb = _dot_tr(cg.astype(BF16), bg.astype(BF16))
        xpair = bxb[:, j * LANE:(j + 1) * LANE]
        sp = s_ref[j]
        spb = sp.astype(BF16)
        y_h, up_h = [], []
        for hh in range(2):
            hd = 2 * j + hh
            seg = jnp.broadcast_to(acs[:, hd:hd + 1], (t, t)) - jnp.broadcast_to(acs_t[hd:hd + 1, :], (t, t))
            lmat = jnp.exp2(jnp.where(causal, seg, -1e30))
            mmat = (cb * lmat * jnp.broadcast_to(dt_t[hd:hd + 1, :], (t, t))).astype(BF16)
            ec = (jnp.broadcast_to(eacs[:, hd:hd + 1], (t, N_B)) * cg).astype(BF16)
            y_h.append(_dot(mmat, xpair) + _dot_tr(ec, spb))
            bw = (bg * jnp.broadcast_to(wq[:, hd:hd + 1], (t, N_B))).astype(BF16)
            up_h.append(_dot_tl(xpair, bw))
        ys.append(jnp.where(lane < HD_B, y_h[0], y_h[1]))
        dec = jnp.where(rowi < HD_B,
                        jnp.broadcast_to(ealast[:, 2 * j:2 * j + 1], (LANE, N_B)),
                        jnp.broadcast_to(ealast[:, 2 * j + 1:2 * j + 2], (LANE, N_B)))
        s_ref[j] = dec * sp + jnp.where(rowi < HD_B, up_h[0], up_h[1])
    y = jnp.concatenate(ys, axis=-1)
    b_out = _group_norm_gate(y, bx, z_ref[...], dexp_ref, bnw_ref).astype(BF16)

    hres = hin_ref[...] + _dot(a_out, wo_ref[0:D_A, :]) + _dot(b_out, wo_ref[D_A:D_A + D_B, :])
    hout_ref[...] = _residual_ple(hres, p_ref, wg_ref, wp_ref, nf_ref, final)

    if last:
        @pl.when(c == pl.num_programs(1) - 1)
        def _():
            ah_ref[0] = h[t - 1:t, :]
            ac_ref[0] = ax[t - (CONV_W - 1):t, :]
            bc_ref[0] = xb[t - (CONV_W - 1):t, :]
            for j in range(H_B // 2):
                sj = s_ref[j]
                bs_ref[0, 2 * j] = sj[0:HD_B, :]
                bs_ref[0, 2 * j + 1] = sj[HD_B:2 * HD_B, :]


def _ab_prompt(u, dtr, h, p, layer, wo, wg, wp, nf, bsz, seq, w, *, final):
    t = AB_CHUNKS * SSD_CHUNK
    assert seq % t == 0
    nc = seq // t
    m = bsz * seq
    rows = lambda b, c: b * nc + c
    cvec = lambda b, c: (0, 0)
    c3 = lambda b, c: (0, 0, 0)
    in_specs = [
        pl.BlockSpec((t, D_A), lambda b, c: (rows(b, c), 0)),
        pl.BlockSpec((t, D_A), lambda b, c: (rows(b, c), 1)),
        pl.BlockSpec((t, D_B), lambda b, c: (rows(b, c), 2)),
        pl.BlockSpec((t, CONV_DIM_B), lambda b, c: (rows(b, c), 2)),
        pl.BlockSpec((t, LANE), lambda b, c: (rows(b, c), 0)),
        pl.BlockSpec((CONV_W, D_A), cvec), pl.BlockSpec((1, D_A), cvec),
        pl.BlockSpec((A_BLOCKS, A_BLK, A_BLK), c3), pl.BlockSpec((1, D_A), cvec),
        pl.BlockSpec((A_BLOCKS, A_BLK, A_BLK), c3), pl.BlockSpec((1, D_A), cvec),
        pl.BlockSpec((1, D_A), cvec),
        pl.BlockSpec((CONV_W, CONV_DIM_B), cvec), pl.BlockSpec((1, CONV_DIM_B), cvec),
        pl.BlockSpec((1, LANE), cvec), pl.BlockSpec((1, LANE), cvec),
        pl.BlockSpec((1, D_B), cvec), pl.BlockSpec((1, D_B), cvec),
        pl.BlockSpec((t, D_MODEL), lambda b, c: (rows(b, c), 0)),
        pl.BlockSpec((None, t, p.shape[2]), lambda b, c: (layer, rows(b, c), 0)),
        pl.BlockSpec(wo.shape, cvec), pl.BlockSpec(wg.shape, cvec), pl.BlockSpec(wp.shape, cvec),
        pl.BlockSpec((1, D_MODEL), cvec),
    ]
    out_shape = [
        jax.ShapeDtypeStruct((m, D_MODEL), F32),
        jax.ShapeDtypeStruct((bsz, 1, D_A), F32),
        jax.ShapeDtypeStruct((bsz, CONV_W - 1, D_A), F32),
        jax.ShapeDtypeStruct((bsz, H_B, HD_B, N_B), F32),
        jax.ShapeDtypeStruct((bsz, CONV_W - 1, CONV_DIM_B), F32),
    ]
    out_specs = [
        pl.BlockSpec((t, D_MODEL), lambda b, c: (rows(b, c), 0)),
        pl.BlockSpec((1, 1, D_A), lambda b, c: (b, 0, 0)),
        pl.BlockSpec((1, CONV_W - 1, D_A), lambda b, c: (b, 0, 0)),
        pl.BlockSpec((1, H_B, HD_B, N_B), lambda b, c: (b, 0, 0, 0)),
        pl.BlockSpec((1, CONV_W - 1, CONV_DIM_B), lambda b, c: (b, 0, 0)),
    ]
    hout, ah, ac, bs, bc = pl.pallas_call(
        functools.partial(_ab_prompt_kernel, final=final),
        grid=(bsz, nc), in_specs=in_specs, out_specs=out_specs, out_shape=out_shape,
        scratch_shapes=[pltpu.VMEM((SUBLANE, D_A), F32), pltpu.VMEM((SUBLANE, CONV_DIM_B), F32),
                        pltpu.VMEM((SUBLANE, D_A), F32), pltpu.VMEM((H_B // 2, 2 * HD_B, N_B), F32)],
        compiler_params=_params("parallel", "arbitrary"),
        name="ab_prompt",
    )(u, u, u, u, dtr, w["acw"], w["acb"], w["wr"], w["br"], w["wi"], w["bi"], w["lam"],
      w["bcw"], w["bcb"], w["dtb"], w["alog"], w["dexp"], w["bnw"],
      h, p, wo, wg, wp, nf.reshape(1, D_MODEL))
    return hout, ah.reshape(bsz, D_A), ac, bs, bc


def _ab_sample_rows_kernel(ax_ref, ag_ref, xbc_ref, dt_ref, sah_ref, sac_ref, sbc_ref,
                           acw_ref, acb_ref, wr_ref, br_ref, wi_ref, bi_ref, lam_ref,
                           bcw_ref, bcb_ref, dtb_ref,
                           aout_ref, ah_ref, ac_ref, bc_ref, xact_ref, dts_ref):
    def conv1(x, buf_ref, w_ref, b_ref, nbuf_ref, width):
        y = b_ref[...] + w_ref[CONV_W - 1:CONV_W, :] * x
        for k in range(CONV_W - 1):
            y = y + w_ref[k:k + 1, :] * buf_ref[:, k * width:(k + 1) * width]
        for k in range(CONV_W - 2):
            nbuf_ref[:, k * width:(k + 1) * width] = buf_ref[:, (k + 1) * width:(k + 2) * width]
        nbuf_ref[:, (CONV_W - 2) * width:(CONV_W - 1) * width] = x
        return y

    xc = conv1(ax_ref[...], sac_ref, acw_ref, acb_ref, ac_ref, D_A)
    a, u = _lru_gates(xc, wr_ref, br_ref, wi_ref, bi_ref, lam_ref)
    h = a * sah_ref[...] + u
    ah_ref[...] = h
    aout_ref[...] = h * _silu_half(ag_ref[...])
    xact_ref[...] = _silu_half(conv1(xbc_ref[...], sbc_ref, bcw_ref, bcb_ref, bc_ref, CONV_DIM_B))
    dts_ref[...] = jax.nn.softplus(dt_ref[...] + dtb_ref[...])


def _pad_rows_t(x):
    pad = jnp.zeros((LANE - x.shape[0], x.shape[1]), F32)
    return jnp.concatenate([x, pad], axis=0).T


def _ab_sample_pairs(s_ref, bx_ref, b_ref, c_ref, dts_ref, alog_ref, so_ref, y_ref, head0):
    bb = SAMPLE_BB
    dts = dts_ref[...]
    dec_t = _pad_rows_t(jnp.exp(dts * (-jnp.exp(alog_ref[...]))))
    dts_t = _pad_rows_t(dts)
    for j in range(bx_ref.shape[1] // LANE):
        h0 = head0 + 2 * j
        xt = _pad_rows_t(bx_ref[:, j * LANE:(j + 1) * LANE])
        dtp = jnp.concatenate([jnp.broadcast_to(dts_t[h0:h0 + 1, :], (HD_B, LANE)),
                               jnp.broadcast_to(dts_t[h0 + 1:h0 + 2, :], (HD_B, LANE))], axis=0)
        xdt = xt * dtp
        for i in range(bb):
            brow = jnp.broadcast_to(b_ref[i:i + 1, :], (2 * HD_B, N_B))
            crow = jnp.broadcast_to(c_ref[i:i + 1, :].astype(BF16).astype(F32), (2 * HD_B, N_B))
            upd = jnp.broadcast_to(xdt[:, i:i + 1], (2 * HD_B, N_B)) * brow
            news = []
            for hh in range(2):
                dec = jnp.broadcast_to(dec_t[h0 + hh:h0 + hh + 1, i:i + 1], (HD_B, N_B))
                sn = dec * s_ref[i, 2 * j + hh] + upd[hh * HD_B:(hh + 1) * HD_B, :]
                so_ref[i, 2 * j + hh] = sn
                news.append(sn)
            prod = jnp.concatenate(news, axis=0).astype(BF16).astype(F32) * crow
            y_ref[i:i + 1, j * LANE:(j + 1) * LANE] = jnp.sum(prod.T, axis=0, keepdims=True)


def _ab_sample_state_kernel(s_ref, xact_ref, dts_ref, alog_ref, so_ref, y_ref):
    hpg = H_B // G_B
    gw = D_B // G_B
    for g in range(G_B):
        heads = pl.ds(g * hpg, hpg)
        _ab_sample_pairs(s_ref.at[:, heads], xact_ref.at[:, pl.ds(g * gw, gw)],
                         xact_ref.at[:, pl.ds(D_B + g * N_B, N_B)],
                         xact_ref.at[:, pl.ds(D_B + G_B * N_B + g * N_B, N_B)],
                         dts_ref, alog_ref, so_ref.at[:, heads], y_ref.at[:, pl.ds(g * gw, gw)], g * hpg)


def _ab_sample_finish_kernel(y_ref, xact_ref, z_ref, dexp_ref, bnw_ref, bout_ref):
    bout_ref[...] = _group_norm_gate(y_ref[...], xact_ref[:, 0:D_B], z_ref[...], dexp_ref, bnw_ref)


def _ab_sample_rows(u, dtr, s_ah, s_ac, s_bc, w):
    bsz = u.shape[0]
    full = lambda shape: pl.BlockSpec(shape, lambda i: tuple(0 for _ in shape))
    cw = CONV_W - 1
    aout, ah, ac, bc, xact, dts = pl.pallas_call(
        _ab_sample_rows_kernel, grid=(1,),
        in_specs=[pl.BlockSpec((bsz, D_A), lambda i: (0, 0)), pl.BlockSpec((bsz, D_A), lambda i: (0, 1)),
                  pl.BlockSpec((bsz, CONV_DIM_B), lambda i: (0, 2)), full((bsz, LANE)),
                  full((bsz, D_A)), full((bsz, cw * D_A)), full((bsz, cw * CONV_DIM_B)),
                  full((CONV_W, D_A)), full((1, D_A)),
                  full((A_BLOCKS, A_BLK, A_BLK)), full((1, D_A)),
                  full((A_BLOCKS, A_BLK, A_BLK)), full((1, D_A)), full((1, D_A)),
                  full((CONV_W, CONV_DIM_B)), full((1, CONV_DIM_B)), full((1, LANE))],
        out_specs=[full((bsz, D_A)), full((bsz, D_A)), full((bsz, cw * D_A)), full((bsz, cw * CONV_DIM_B)),
                   full((bsz, CONV_DIM_B)), full((bsz, LANE))],
        out_shape=[jax.ShapeDtypeStruct((bsz, D_A), F32), jax.ShapeDtypeStruct((bsz, D_A), F32),
                   jax.ShapeDtypeStruct((bsz, cw * D_A), F32), jax.ShapeDtypeStruct((bsz, cw * CONV_DIM_B), F32),
                   jax.ShapeDtypeStruct((bsz, CONV_DIM_B), F32), jax.ShapeDtypeStruct((bsz, LANE), F32)],
        compiler_params=_params("arbitrary"),
        name="ab_sample_rows",
    )(u, u, u, dtr, s_ah, s_ac.reshape(bsz, cw * D_A), s_bc.reshape(bsz, cw * CONV_DIM_B),
      w["acw"], w["acb"], w["wr"], w["br"], w["wi"], w["bi"], w["lam"], w["bcw"], w["bcb"], w["dtb"])
    return aout, ah, ac.reshape(bsz, cw, D_A), bc.reshape(bsz, cw, CONV_DIM_B), xact, dts


def _ab_sample_state(s_bs, xact, dts, w):
    bsz = xact.shape[0]
    bb = SAMPLE_BB
    assert bsz % bb == 0
    return pl.pallas_call(
        _ab_sample_state_kernel, grid=(bsz // bb,),
        in_specs=[pl.BlockSpec((bb, H_B, HD_B, N_B), lambda i: (i, 0, 0, 0)),
                  pl.BlockSpec((bb, CONV_DIM_B), lambda i: (i, 0)),
                  pl.BlockSpec((bb, LANE), lambda i: (i, 0)),
                  pl.BlockSpec((1, LANE), lambda i: (0, 0))],
        out_specs=[pl.BlockSpec((bb, H_B, HD_B, N_B), lambda i: (i, 0, 0, 0)),
                   pl.BlockSpec((bb, D_B), lambda i: (i, 0))],
        out_shape=[jax.ShapeDtypeStruct(s_bs.shape, F32), jax.ShapeDtypeStruct((bsz, D_B), F32)],
        compiler_params=_params("parallel"),
        name="ab_sample_state",
    )(s_bs, xact, dts, w["alog"])


def _ab_sample_finish(y, xact, u, w):
    bsz = y.shape[0]
    full = lambda shape: pl.BlockSpec(shape, lambda i: (0, 0))
    return pl.pallas_call(
        _ab_sample_finish_kernel, grid=(1,),
        in_specs=[full((bsz, D_B)), full((bsz, CONV_DIM_B)),
                  pl.BlockSpec((bsz, D_B), lambda i: (0, 2)),
                  full((1, D_B)), full((1, D_B))],
        out_specs=full((bsz, D_B)),
        out_shape=jax.ShapeDtypeStruct((bsz, D_B), F32),
        compiler_params=_params("arbitrary"),
        name="ab_sample_finish",
    )(y, xact, u, w["dexp"], w["bnw"])


def _hg_lower_bound(clb, layer):
    mx = jnp.max(clb, axis=0, keepdims=True)
    ex = jnp.exp(clb - mx)
    return jnp.sum(ex[1:layer + 1], axis=0, keepdims=True) / jnp.sum(ex, axis=0, keepdims=True)


def _hg_gates(fx_half, lb):
    f = 0.5 * (1.0 + lb) + (0.5 * (1.0 - lb)) * jnp.tanh(fx_half)
    return f, 1.0 - f


def _hg_out(o, gate_half, cnw):
    return o * lax.rsqrt(jnp.mean(o * o, axis=-1, keepdims=True) + EPS) * cnw * (gate_half + gate_half * jnp.tanh(gate_half))


def _hg_gamma():
    import numpy as np
    q = HG_CHUNK
    t = np.arange(q)[:, None]
    tau = np.arange(q)[None, :]
    mats = [(tau <= t)]
    for l in range(1, HG_MXU_LEVELS):
        w = 1 << l
        ref = (t // (2 * w)) * (2 * w) + w - 1
        upper = (t % (2 * w)) >= w
        mats.append(np.where(upper, (tau > ref) & (tau <= t), (tau > t) & (tau <= ref)))
    gam = np.concatenate(mats, axis=0).astype(np.float32)
    return jnp.asarray(np.concatenate([gam, gam], axis=1), dtype=BF16)


def _hg_level_table():
    import numpy as np
    q = HG_CHUNK
    t = np.arange(q)[:, None]
    s = np.arange(q)[None, :]
    x = t ^ s
    lvl = np.floor(np.log2(np.maximum(x, 1))).astype(np.int32)
    return jnp.asarray(np.where(t > s, lvl, -1).astype(np.int32))


def _c_prompt_kernel(q_ref, f_ref, v_ref, g_ref, clb_ref, cnw_ref, gam_ref, lvl_ref,
                     og_ref, cs_ref, st_ref, *, layer):
    c = pl.program_id(1)
    last = pl.num_programs(1) - 1
    qc = HG_CHUNK

    @pl.when(c == 0)
    def _():
        st_ref[...] = jnp.zeros_like(st_ref)

    gam = gam_ref[...]
    ntile = qc // SUBLANE
    sub = lax.broadcasted_iota(jnp.int32, (SUBLANE, DK_C), 0)
    sub_levels = HG_MXU_LEVELS
    sub_upper = [(sub & (1 << l)) != 0 for l in range(sub_levels)]

    def tiles(x):
        return [x[i * SUBLANE:(i + 1) * SUBLANE, :] for i in range(ntile)]

    def gate_split(hd, rows):
        f, kk = _hg_gates(f_ref[hd, rows, :].astype(F32), _hg_lower_bound(clb_ref[hd], layer))
        g = jnp.log(jnp.maximum(f, HG_F_MIN)) * LOG2_E
        g1 = g.astype(BF16)
        g2 = (g - g1.astype(F32)).astype(BF16)
        return (f, kk), jnp.concatenate([g1, g2], axis=0)

    def scores(hd, rows, fk, sums):
        f, kk = fk
        qh = q_ref[hd, rows, :].astype(F32) * (DK_C ** -0.5)
        bcum = sums[0:qc]
        st = st_ref[hd]
        o = _dot_tr((qh * jnp.exp2(bcum)).astype(BF16), st.astype(BF16))
        qt, kt, ft, bt = tiles(qh), tiles(kk), tiles(f), tiles(bcum)
        prods = []
        for l in range(HG_LEVELS):
            if l == 0:
                xt = [jnp.where(sub_upper[0], qt[i] * ft[i], kt[i]) for i in range(ntile)]
            elif l < HG_MXU_LEVELS:
                dec = tiles(jnp.exp2(sums[l * qc:(l + 1) * qc]))
                xt = [jnp.where(sub_upper[l], qt[i], kt[i]) * dec[i] for i in range(ntile)]
            else:
                wt = 1 << (l - HG_MXU_LEVELS)
                xt = []
                for blk in range(0, ntile, 2 * wt):
                    ref = (blk + wt) * SUBLANE - 1
                    bref = jnp.broadcast_to(bcum[ref:ref + 1, :], (SUBLANE, DK_C))
                    xt += [kt[i] * jnp.exp2(bref - bt[i]) for i in range(blk, blk + wt)]
                    xt += [qt[i] * jnp.exp2(bt[i] - bref) for i in range(blk + wt, blk + 2 * wt)]
            x = jnp.concatenate(xt, axis=0).astype(BF16)
            half = (1 << l) // BF16_ROWS
            if half == 0:
                p = tiles(_dot_tr(x, x))
                prods.append({i: p[i] for i in range(ntile)})
            else:
                ups = [r for r in range(qc // BF16_ROWS) if (r // half) & 1]
                pu = _dot_tr(jnp.concatenate([x[r * BF16_ROWS:(r + 1) * BF16_ROWS, :] for r in ups], axis=0), x)
                tpr = BF16_ROWS // SUBLANE
                prods.append({r * tpr + k: pu[(n * tpr + k) * SUBLANE:(n * tpr + k + 1) * SUBLANE, :]
                              for n, r in enumerate(ups) for k in range(tpr)})
        return qh, st, o, prods

    def level_masks():
        masks = {}
        for i in range(ntile):
            lv = lvl_ref[i * SUBLANE:(i + 1) * SUBLANE, :]
            for l in range(HG_LEVELS):
                if l < sub_levels or (i >> (l - sub_levels)) & 1:
                    masks[i, l] = lv == l
        return masks

    def combine(hd, rows, kk, bcum, qh, st, o, prods, masks):
        arows = []
        for i in range(ntile):
            a = jnp.zeros((SUBLANE, qc), F32)
            for l in range(HG_LEVELS):
                if (i, l) in masks:
                    a = jnp.where(masks[i, l], prods[l][i], a)
            arows.append(a)
        amat = jnp.concatenate(arows, axis=0)
        vb = v_ref[hd, rows, :].astype(BF16)
        o = o + _dot(amat.astype(BF16), vb) + jnp.sum(qh * kk, axis=-1, keepdims=True) * vb.astype(F32)
        blast = bcum[qc - 1:qc, :]
        kdec = (kk * jnp.exp2(blast - bcum)).astype(BF16)
        st_ref[hd] = st * jnp.exp2(blast) + _dot_tl(vb, kdec)
        return o

    nchunk = HG_BLOCK // qc

    def body(idx, carry):
        hg = idx // nchunk
        rows = pl.ds(pl.multiple_of((idx % nchunk) * qc, qc), qc)
        heads = [hg * HG_UNROLL + k for k in range(HG_UNROLL)]
        gs = [gate_split(hd, rows) for hd in heads]
        sums = _dot(gam, jnp.concatenate([s for _, s in gs], axis=1))
        sums = [sums[:, k * DK_C:(k + 1) * DK_C] for k in range(HG_UNROLL)]
        sc = [scores(hd, rows, gs[k][0], sums[k]) for k, hd in enumerate(heads)]
        masks = level_masks()
        outs = [combine(hd, rows, gs[k][0][1], sums[k][0:qc], *sc[k], masks) for k, hd in enumerate(heads)]
        for k, hd in enumerate(heads):
            og_ref[hd, rows, :] = _hg_out(outs[k], g_ref[hd, rows, :].astype(F32), cnw_ref[hd]).astype(BF16)
        return carry

    lax.fori_loop(0, (H_C // HG_UNROLL) * nchunk, body, 0)

    @pl.when(c == last)
    def _():
        for hd in range(H_C):
            cs_ref[0, hd] = st_ref[hd].T


def _c_prompt(u, bsz, seq, w, layer):
    tb = HG_BLOCK
    assert seq % tb == 0 and tb % HG_CHUNK == 0 and (1 << HG_LEVELS) == HG_CHUNK
    nc = seq // tb
    m = bsz * seq
    depth = w["clb"].shape[1]

    def part(k):
        return pl.BlockSpec((H_C, tb, LANE), lambda b, c: (k, b * nc + c, 0))

    c2 = lambda b, c: (0, 0)
    c3 = lambda b, c: (0, 0, 0)
    og, cs = pl.pallas_call(
        functools.partial(_c_prompt_kernel, layer=layer), grid=(bsz, nc),
        in_specs=[part(0), part(1), part(2), part(3),
                  pl.BlockSpec((H_C, depth, DK_C), c3), pl.BlockSpec((H_C, 1, DV_C), c3),
                  pl.BlockSpec(w["gam"].shape, c2), pl.BlockSpec(w["lvl"].shape, c2)],
        out_specs=[pl.BlockSpec((H_C, tb, LANE), lambda b, c: (0, b * nc + c, 0)),
                   pl.BlockSpec((1, H_C, DK_C, DV_C), lambda b, c: (b, 0, 0, 0))],
        out_shape=[jax.ShapeDtypeStruct((H_C, m, DV_C), BF16),
                   jax.ShapeDtypeStruct((bsz, H_C, DK_C, DV_C), F32)],
        scratch_shapes=[pltpu.VMEM((H_C, DV_C, DK_C), F32)],
        compiler_params=_params("parallel", "arbitrary"),
        name="c_prompt",
    )(u, u, u, u, w["clb"], w["cnw"], w["gam"], w["lvl"])
    return og, cs


def _c_sample_head(hd, q_ref, f_ref, v_ref, g_ref, s_ref, clb_ref, cnw_ref, og_ref, so_ref, layer):
    bb = SAMPLE_BB
    lane = lax.broadcasted_iota(jnp.int32, (DK_C, LANE), 1)
    first_rows = lax.broadcasted_iota(jnp.int32, (LANE, DV_C), 0) < bb
    lb = _hg_lower_bound(clb_ref[hd], layer)
    f, kk = _hg_gates(f_ref[hd], lb)
    f_t = _pad_rows_t(f)
    k_t = _pad_rows_t(kk)
    qs = q_ref[hd] * (DK_C ** -0.5)
    v = v_ref[hd]
    vpad = jnp.where(first_rows, jnp.tile(v, (LANE // bb, 1)), 0.0).astype(BF16)
    orows = []
    for i in range(bb):
        fcol = jnp.broadcast_to(f_t[:, i:i + 1], (DK_C, DV_C))
        kv = _dot(jnp.where(lane == i, k_t, 0.0).astype(BF16), vpad)
        sn = fcol * s_ref[i, hd] + kv
        so_ref[i, hd] = sn
        qrow = jnp.broadcast_to(qs[i:i + 1, :], (2 * SUBLANE, DK_C)).astype(BF16)
        orows.append(_dot(qrow, sn.astype(BF16))[0:1, :])
    o = jnp.concatenate(orows, axis=0)
    og_ref[hd] = _hg_out(o, g_ref[hd], cnw_ref[hd])


def _c_sample_kernel(q_ref, f_ref, v_ref, g_ref, s_ref, clb_ref, cnw_ref, og_ref, so_ref, *, layer):
    def head(hd, carry):
        _c_sample_head(hd, q_ref, f_ref, v_ref, g_ref, s_ref, clb_ref, cnw_ref, og_ref, so_ref, layer)
        return carry

    lax.fori_loop(0, H_C, head, 0, unroll=SAMPLE_UNROLL)


def _c_sample(u, s_c, w, layer):
    bsz = s_c.shape[0]
    bb = SAMPLE_BB
    assert bsz % bb == 0
    depth = w["clb"].shape[1]

    def part(k):
        return pl.BlockSpec((H_C, bb, LANE), lambda i: (k, i, 0))

    c3 = lambda i: (0, 0, 0)
    og, so = pl.pallas_call(
        functools.partial(_c_sample_kernel, layer=layer), grid=(bsz // bb,),
        in_specs=[part(0), part(1), part(2), part(3),
                  pl.BlockSpec((bb, H_C, DK_C, DV_C), lambda i: (i, 0, 0, 0)),
                  pl.BlockSpec((H_C, depth, DK_C), c3), pl.BlockSpec((H_C, 1, DV_C), c3)],
        out_specs=[pl.BlockSpec((H_C, bb, LANE), lambda i: (0, i, 0)),
                   pl.BlockSpec((bb, H_C, DK_C, DV_C), lambda i: (i, 0, 0, 0))],
        out_shape=[jax.ShapeDtypeStruct((H_C, bsz, DV_C), F32), jax.ShapeDtypeStruct(s_c.shape, F32)],
        compiler_params=_params("parallel"),
        name="c_sample",
    )(u, u, u, u, s_c, w["clb"], w["cnw"])
    return og, so


def _out_proj_c_sample_kernel(mix_ref, h_ref, p_ref, wo_ref, wg_ref, wp_ref, nf_ref,
                              q_ref, f_ref, v_ref, g_ref, s_ref, clb_ref, cnw_ref,
                              o_ref, og_ref, so_ref, *, final, layer):
    for hd in range(q_ref.shape[0]):
        _c_sample_head(hd, q_ref, f_ref, v_ref, g_ref, s_ref, clb_ref, cnw_ref, og_ref, so_ref, layer)
    _out_proj_kernel(mix_ref, h_ref, p_ref, wo_ref, wg_ref, wp_ref, nf_ref, o_ref, head_major=True, final=final)


def _out_proj_c_sample(mix, h, p, layer, wo, wg, wp, nf, u_s, s_c, w, *, final):
    m, d = h.shape
    bsz = s_c.shape[0]
    bb, split = SAMPLE_BB, FUSED_HEAD_SPLIT
    steps = (bsz // bb) * split
    hp = H_C // split
    tm = m // steps
    assert bsz % bb == 0 and H_C % split == 0 and m % steps == 0 and tm % BF16_ROWS == 0
    depth = w["clb"].shape[1]
    const = lambda i: (0, 0)

    def part(k):
        return pl.BlockSpec((hp, bb, LANE), lambda i: (k * split + i % split, i // split, 0))

    state = pl.BlockSpec((bb, hp, DK_C, DV_C), lambda i: (i // split, i % split, 0, 0))
    hrow = lambda i: (i % split, 0, 0)
    hout, og, so = pl.pallas_call(
        functools.partial(_out_proj_c_sample_kernel, final=final, layer=layer),
        grid=(steps,),
        in_specs=[pl.BlockSpec((mix.shape[0], tm, LANE), lambda i: (0, i, 0)),
                  pl.BlockSpec((tm, d), lambda i: (i, 0)),
                  pl.BlockSpec((None, tm, p.shape[2]), lambda i: (layer, i, 0)),
                  pl.BlockSpec(wo.shape, const), pl.BlockSpec(wg.shape, const),
                  pl.BlockSpec(wp.shape, const), pl.BlockSpec((1, d), const),
                  part(0), part(1), part(2), part(3), state,
                  pl.BlockSpec((hp, depth, DK_C), hrow), pl.BlockSpec((hp, 1, DV_C), hrow)],
        out_specs=[pl.BlockSpec((tm, d), lambda i: (i, 0)),
                   pl.BlockSpec((hp, bb, LANE), lambda i: (i % split, i // split, 0)),
                   state],
        out_shape=[jax.ShapeDtypeStruct((m, d), F32),
                   jax.ShapeDtypeStruct((H_C, bsz, DV_C), F32), jax.ShapeDtypeStruct(s_c.shape, F32)],
        compiler_params=_params("arbitrary"),
        name="out_proj_c_sample",
    )(mix, h, p, wo, wg, wp, nf.reshape(1, d), u_s, u_s, u_s, u_s, s_c, w["clb"], w["cnw"])
    return hout, og, so


def _row(v, width=None):
    v = v.astype(F32).reshape(1, -1)
    if width is not None and v.shape[1] < width:
        v = jnp.pad(v, ((0, 0), (0, width - v.shape[1])))
    return v


def kernel(x_prompt, x_sample, p_prompt, p_sample, state_a_h, state_a_conv, state_b_ssm, state_b_conv, state_c,
           norm_w, norm_f, ab_w_in, a_conv_w, a_conv_b, a_w_r, a_b_r, a_w_i, a_b_i, a_lam, b_conv_w, b_conv_b,
           b_dt_bias, b_a_log, b_d, b_norm_w, ab_w_out, c_w_in, c_lb, c_norm_w, c_w_out, ple_proj, ple_gate):
    depth = norm_w.shape[0]
    bp, seq, _ = x_prompt.shape
    bs = x_sample.shape[0]
    hp = x_prompt.reshape(bp * seq, D_MODEL)
    hs = x_sample.reshape(bs, D_MODEL)
    pp = p_prompt.reshape(depth, bp * seq, D_PLE)
    ps = p_sample.reshape(depth, bs, D_PLE)
    gam, lvl = _hg_gamma(), _hg_level_table()
    clb = c_lb.astype(F32).reshape(depth, H_C, DK_C).transpose(1, 0, 2)

    c_col_scale = jnp.concatenate([jnp.ones((HK_C,), F32), jnp.full((HK_C,), 0.5, F32),
                                   jnp.ones((D_C,), F32), jnp.full((D_C,), 0.5, F32)])

    ah_p, ac_p, bs_p, bc_p, c_p = [], [], [], [], []
    ah_s, ac_s, bs_s, bc_s, c_s = [], [], [], [], []
    next_u = next_w_in = None
    for i in range(depth):
        j = i // 2
        final = i == depth - 1
        wg = (0.5 * ple_gate[i]).astype(BF16)
        wp = (0.5 * ple_proj[i]).astype(BF16)
        if i % 2 == 0:
            col_scale = jnp.concatenate([jnp.ones((D_A,), F32), jnp.full((D_A + D_B,), 0.5, F32),
                                         jnp.ones((CONV_DIM_B,), F32)])
            w_ab_t = ab_w_in[j].T
            w_ab = (w_ab_t[:AB_MAIN] * col_scale[:, None]).astype(BF16)
            w_dt = jnp.pad(w_ab_t[AB_MAIN:], ((0, LANE - H_B), (0, 0))).astype(BF16)
            wo = ab_w_out[j].astype(BF16)
            w = dict(acw=a_conv_w[j].astype(F32), acb=_row(a_conv_b[j]),
                     wr=(0.5 * a_w_r[j]).astype(BF16), br=0.5 * _row(a_b_r[j]),
                     wi=a_w_i[j].astype(BF16), bi=_row(a_b_i[j]),
                     lam=_row(a_lam[j]), bcw=0.5 * b_conv_w[j].astype(F32), bcb=0.5 * _row(b_conv_b[j]),
                     dtb=_row(b_dt_bias[j], LANE), alog=_row(b_a_log[j], LANE),
                     dexp=_row(jnp.repeat(b_d[j], HD_B)), bnw=_row(b_norm_w[j]))
            us, dtrs = _in_proj(hs, norm_w[i], w_ab, w_dt, w_t=True)
            aout, s1, s2, s4, xact, dts = _ab_sample_rows(us, dtrs, state_a_h[j], state_a_conv[j], state_b_conv[j], w)
            ah_s.append(s1); ac_s.append(s2); bc_s.append(s4)
            if i + 1 < depth:
                u, dtr, next_w_in = _in_proj(hp, norm_w[i], w_ab, w_dt, w_t=True,
                                             prep=(c_w_in[(i + 1) // 2], c_col_scale))
            else:
                u, dtr = _in_proj(hp, norm_w[i], w_ab, w_dt, w_t=True)
            hp, s1, s2, s3, s4 = _ab_prompt(u, dtr, hp, pp, i, wo, wg, wp, norm_f, bp, seq, w, final=final)
            ah_p.append(s1); ac_p.append(s2); bs_p.append(s3); bc_p.append(s4)
            if i + 1 < depth:
                next_u, bs_new, y = _in_proj(hp, norm_w[i + 1], next_w_in, head_major=True, head_major_dtype=BF16,
                                             sample=(state_b_ssm[j], xact, dts), alog=w["alog"])
            else:
                bs_new, y = _ab_sample_state(state_b_ssm[j], xact, dts, w)
            bs_s.append(bs_new)
            mix = jnp.concatenate([aout, _ab_sample_finish(y, xact, us, w)], axis=-1)
            hs = _out_proj(mix, hs, ps, i, wo, wg, wp, norm_f, head_major=False, final=final)
        else:
            w_in = next_w_in if next_w_in is not None else (c_w_in[j] * c_col_scale).astype(BF16)
            wo = c_w_out[j].astype(BF16)
            w = dict(clb=clb, cnw=c_norm_w[j].astype(F32).reshape(H_C, 1, DV_C), gam=gam, lvl=lvl)
            u = next_u if next_u is not None else _in_proj(hp, norm_w[i], w_in, head_major=True,
                                                           head_major_dtype=BF16)
            next_u = next_w_in = None
            og, s1 = _c_prompt(u, bp, seq, w, i)
            c_p.append(s1)
            u = _in_proj(hs, norm_w[i], w_in, head_major=True)
            steps = (bs // SAMPLE_BB) * FUSED_HEAD_SPLIT
            if bs % SAMPLE_BB == 0 and (bp * seq) % (steps * BF16_ROWS) == 0 and (bp * seq) // steps <= OUT_PROJ_TM:
                hp, og, s1 = _out_proj_c_sample(og, hp, pp, i, wo, wg, wp, norm_f, u, state_c[j], w, final=final)
            else:
                hp = _out_proj(og, hp, pp, i, wo, wg, wp, norm_f, head_major=True, final=final)
                og, s1 = _c_sample(u, state_c[j], w, i)
            c_s.append(s1)
            hs = _out_proj(og, hs, ps, i, wo, wg, wp, norm_f, head_major=True, final=final)
    return (hp.reshape(bp, seq, D_MODEL), hs.reshape(bs, 1, D_MODEL),
            jnp.stack(ah_p), jnp.stack(ac_p), jnp.stack(bs_p), jnp.stack(bc_p), jnp.stack(c_p),
            jnp.stack(ah_s), jnp.stack(ac_s), jnp.stack(bs_s), jnp.stack(bc_s), jnp.stack(c_s))
```
